```python
import math
import numpy as np
import jax, jax.numpy as jnp
from jax import lax

D_MODEL = 1024
BATCH = 2
SEQ = 8192
DEPTH = 2

N_EVEN = (DEPTH + 1) // 2
N_ODD = DEPTH // 2

MLSTM_HEADS = 4
MLSTM_HEAD_DIM = 128
MLSTM_W = MLSTM_HEADS * MLSTM_HEAD_DIM
MLSTM_CHUNK = 128
CONV_K = 4
NSA_HEADS = 8
NSA_GROUPS = 2
NSA_HPG = NSA_HEADS // NSA_GROUPS
NSA_HEAD_DIM = 64
NSA_W = NSA_HEADS * NSA_HEAD_DIM
NSA_KV_W = NSA_GROUPS * NSA_HEAD_DIM
CMP_LEN = 32
CMP_STRIDE = 16
CMP_HIDDEN = 128
SEL_BLOCK = 64
SEL_TOPK = 16
WINDOW = 512
Q_BLOCK = 128
ROPE_THETA = 500000.0
ROPE_DIM = NSA_HEAD_DIM // 4
EVEN_MIX_W = MLSTM_W + NSA_W
MLSTM_SPLITS = (MLSTM_W,) * 5 + (MLSTM_HEADS, MLSTM_HEADS)
NSA_SPLITS = (NSA_W,) + (NSA_KV_W,) * 6 + (3 * NSA_HEADS, NSA_W)
MLSTM_COLS = sum(MLSTM_SPLITS)
NSA_COLS = sum(NSA_SPLITS)
EVEN_IN_COLS = MLSTM_COLS + NSA_COLS
HGRN_HEADS = 8
HGRN_HEAD_DIM = 128
HGRN_W = HGRN_HEADS * HGRN_HEAD_DIM
HGRN_CHUNK = 64
ODD_IN_COLS = 4 * HGRN_W
EPS = 1e-6
NEG = -1e30

kernel_name = "hybrid_mlstm_nsa_hgrn2_trunk"


def split_cols(a, sizes):
    idx = np.cumsum(sizes)[:-1].tolist()
    return jnp.split(a, idx, axis=-1)


def rms_norm(x, w):
    xf = x.astype(jnp.float32)
    return xf * lax.rsqrt(jnp.mean(xf * xf, axis=-1, keepdims=True) + EPS) * w.astype(jnp.float32)


def heads(a, n_heads):
    b, s, _ = a.shape
    return a.reshape(b, s, n_heads, -1).transpose(0, 2, 1, 3)


def partial_rope(x, pos):
    half = ROPE_DIM // 2
    inv_freq = ROPE_THETA ** (-jnp.arange(half, dtype=jnp.float32) / half)
    ang = pos.astype(jnp.float32)[:, None] * inv_freq
    cos, sin = jnp.cos(ang), jnp.sin(ang)
    x1, x2, rest = x[..., :half], x[..., half:ROPE_DIM], x[..., ROPE_DIM:]
    return jnp.concatenate([x1 * cos - x2 * sin, x1 * sin + x2 * cos, rest], axis=-1)


def masked_softmax(s, mask):
    p = jax.nn.softmax(jnp.where(mask, s, NEG), axis=-1)
    return jnp.where(mask, p, 0.0)


def causal_conv(x, w, b):
    c = x.shape[-1]
    out = lax.conv_general_dilated(
        x, w.astype(jnp.float32).reshape(CONV_K, 1, c), window_strides=(1,),
        padding=[(CONV_K - 1, 0)], dimension_numbers=('NWC', 'WIO', 'NWC'),
        feature_group_count=c)
    return out + b


def mlstm_chunkwise(q, k, v, i_pre, f_pre):
    b_, h_, s_, d = q.shape
    L = MLSTM_CHUNK
    nc = s_ // L
    logf = jax.nn.log_sigmoid(f_pre)

    def to_chunks(a):
        return jnp.moveaxis(a.reshape(b_, h_, nc, L, *a.shape[3:]), 2, 0)

    causal = jnp.tril(jnp.ones((L, L), dtype=bool))

    def step(carry, xs):
        C, n, m = carry
        qj, kj, vj, ij, fj = xs
        bcum = jnp.cumsum(fj, axis=-1)
        D = jnp.where(causal, bcum[..., :, None] - bcum[..., None, :] + ij[..., None, :], -jnp.inf)
        inter = bcum + m[..., None]
        m_row = jnp.maximum(jnp.max(D, axis=-1), inter)
        Smat = jnp.einsum('bhjd,bhsd->bhjs', qj, kj) * jnp.exp(D - m_row[..., None])
        w_inter = jnp.exp(inter - m_row)
        num = jnp.einsum('bhjs,bhse->bhje', Smat, vj) + w_inter[..., None] * jnp.einsum('bhjd,bhde->bhje', qj, C)
        den = Smat.sum(-1) + w_inter * jnp.einsum('bhjd,bhd->bhj', qj, n)
        h = num / jnp.maximum(jnp.abs(den), jnp.exp(-m_row))[..., None]
        bL = bcum[..., -1]
        dl = bL[..., None] - bcum + ij
        m_new = jnp.maximum(bL + m, jnp.max(dl, axis=-1))
        wk = jnp.exp(dl - m_new[..., None])
        decay = jnp.exp(bL + m - m_new)
        C_new = decay[..., None, None] * C + jnp.einsum('bhs,bhsd,bhse->bhde', wk, kj, vj)
        n_new = decay[..., None] * n + jnp.einsum('bhs,bhsd->bhd', wk, kj)
        return (C_new, n_new, m_new), h

    init = (jnp.zeros((b_, h_, d, d), jnp.float32), jnp.zeros((b_, h_, d), jnp.float32),
            jnp.zeros((b_, h_), jnp.float32))
    _, hs = lax.scan(step, init, (to_chunks(q), to_chunks(k), to_chunks(v), to_chunks(i_pre), to_chunks(logf)))
    return jnp.moveaxis(hs, 0, 2).reshape(b_, h_, s_, d)


def mlstm_branch(cols, f_bias, conv_w, conv_b, head_norm):
    b_, s_, _ = cols.shape
    q, k, v, o, z, ig, fg = split_cols(cols, MLSTM_SPLITS)
    qk = jax.nn.silu(causal_conv(jnp.concatenate([q, k], axis=-1), conv_w, conv_b))
    q, k = jnp.split(qk, 2, axis=-1)
    qh = heads(q, MLSTM_HEADS)
    kh = heads(k, MLSTM_HEADS) * (MLSTM_HEAD_DIM ** -0.5)
    vh = heads(v, MLSTM_HEADS)
    i_pre = ig.transpose(0, 2, 1)
    f_pre = (fg + f_bias).transpose(0, 2, 1)
    h = mlstm_chunkwise(qh, kh, vh, i_pre, f_pre).transpose(0, 2, 1, 3)
    h = rms_norm(h, head_norm.reshape(MLSTM_HEADS, MLSTM_HEAD_DIM)).reshape(b_, s_, MLSTM_W)
    return h * jax.nn.sigmoid(o) * jax.nn.silu(z)


def compress(blocks, w1, w2):
    flat = blocks.reshape(*blocks.shape[:3], CMP_LEN * NSA_HEAD_DIM)
    return jax.nn.gelu(flat @ w1) @ w2


def nsa_branch(cols, q_norm, k_norm, ck_pos, ck_w1, ck_w2, cv_pos, cv_w1, cv_w2):
    b_, s_, _ = cols.shape
    G, R, d = NSA_GROUPS, NSA_HPG, NSA_HEAD_DIM
    q, kc, vc, ks, vs, kw, vw, gt, z = split_cols(cols, NSA_SPLITS)
    pos = jnp.arange(s_)
    qh = partial_rope(rms_norm(heads(q, NSA_HEADS), q_norm), pos) * (d ** -0.5)
    qh = qh.reshape(b_, G, R, s_, d)
    ks_h = partial_rope(rms_norm(heads(ks, G), k_norm[1]), pos)
    vs_h = heads(vs, G)
    kw_h = partial_rope(rms_norm(heads(kw, G), k_norm[2]), pos)
    vw_h = heads(vw, G)

    n_cmp = (s_ - CMP_LEN) // CMP_STRIDE + 1
    cmp_idx = np.arange(n_cmp)[:, None] * CMP_STRIDE + np.arange(CMP_LEN)[None, :]
    cmp_end = jnp.asarray(cmp_idx[:, -1])
    k_cmp = compress(heads(kc, G)[:, :, cmp_idx] + ck_pos, ck_w1, ck_w2)
    k_cmp = partial_rope(rms_norm(k_cmp, k_norm[0]), cmp_end)
    v_cmp = compress(heads(vc, G)[:, :, cmp_idx] + cv_pos, cv_w1, cv_w2)

    n_sel = s_ // SEL_BLOCK
    cs = np.arange(n_cmp) * CMP_STRIDE
    ss = np.arange(n_sel) * SEL_BLOCK
    ov = np.clip(np.minimum(cs[:, None] + CMP_LEN, ss[None, :] + SEL_BLOCK)
                 - np.maximum(cs[:, None], ss[None, :]), 0, None).astype(np.float32)
    ov = jnp.asarray(ov)
    k_top = min(SEL_TOPK, n_sel)

    gates = jax.nn.sigmoid(gt.reshape(b_, s_, NSA_HEADS, 3)).transpose(0, 2, 1, 3).reshape(b_, G, R, s_, 3)
    kw_pad = jnp.pad(kw_h, ((0, 0), (0, 0), (WINDOW, 0), (0, 0)))
    vw_pad = jnp.pad(vw_h, ((0, 0), (0, 0), (WINDOW, 0), (0, 0)))
    bi = jnp.arange(b_)[:, None, None, None]
    gi = jnp.arange(G)[None, :, None, None]
    nqb = s_ // Q_BLOCK

    def to_blocks(a):
        a = a.reshape(*a.shape[:-2], nqb, Q_BLOCK, a.shape[-1])
        return jnp.moveaxis(a, -3, 0)

    def block_fn(args):
        qb, q_blk, g_blk = args
        t = qb * Q_BLOCK + jnp.arange(Q_BLOCK)
        s_c = jnp.einsum('bgrqd,bgnd->bgrqn', q_blk, k_cmp)
        p_c = masked_softmax(s_c, cmp_end[None, :] <= t[:, None])
        o_c = jnp.einsum('bgrqn,bgnd->bgrqd', p_c, v_cmp)
        imp = jnp.einsum('bgrqn,nm->bgqm', p_c, ov)
        blk = jnp.arange(n_sel)[None, :]
        cur = (t // SEL_BLOCK)[:, None]
        forced = (blk == 0) | (blk == cur) | (blk == cur - 1)
        imp = jnp.where(blk > cur, NEG, jnp.where(forced, -NEG, imp))
        _, sel = lax.top_k(imp, k_top)
        valid = sel <= cur[None, None]
        tok = (sel[..., None] * SEL_BLOCK + jnp.arange(SEL_BLOCK)).reshape(b_, G, Q_BLOCK, k_top * SEL_BLOCK)
        tok_ok = jnp.repeat(valid, SEL_BLOCK, axis=-1) & (tok <= t[:, None])
        ks_g = ks_h[bi, gi, tok]
        vs_g = vs_h[bi, gi, tok]
        s_s = jnp.einsum('bgrqd,bgqnd->bgrqn', q_blk, ks_g)
        p_s = masked_softmax(s_s, tok_ok[:, :, None])
        o_s = jnp.einsum('bgrqn,bgqnd->bgrqd', p_s, vs_g)
        kwb = lax.dynamic_slice_in_dim(kw_pad, qb * Q_BLOCK, Q_BLOCK + WINDOW, axis=2)
        vwb = lax.dynamic_slice_in_dim(vw_pad, qb * Q_BLOCK, Q_BLOCK + WINDOW, axis=2)
        sp = qb * Q_BLOCK - WINDOW + jnp.arange(Q_BLOCK + WINDOW)
        wmask = (sp[None, :] >= 0) & (sp[None, :] <= t[:, None]) & (t[:, None] - sp[None, :] < WINDOW)
        s_w = jnp.einsum('bgrqd,bgnd->bgrqn', q_blk, kwb)
        p_w = masked_softmax(s_w, wmask)
        o_w = jnp.einsum('bgrqn,bgnd->bgrqd', p_w, vwb)
        return g_blk[..., 0:1] * o_c + g_blk[..., 1:2] * o_s + g_blk[..., 2:3] * o_w

    out = lax.map(block_fn, (jnp.arange(nqb), to_blocks(qh), to_blocks(gates)))
    out = jnp.moveaxis(out, 0, 3).reshape(b_, NSA_HEADS, s_, d).transpose(0, 2, 1, 3).reshape(b_, s_, NSA_W)
    return out * jax.nn.silu(z)


def hgrn2_chunkwise(q, k, v, logf):
    b_, h_, s_, dk = q.shape
    dv = v.shape[-1]
    L = HGRN_CHUNK
    nc = s_ // L

    def to_chunks(a):
        return jnp.moveaxis(a.reshape(b_, h_, nc, L, a.shape[-1]), 2, 0)

    causal = jnp.tril(jnp.ones((L, L), dtype=bool))[..., None]

    def step(Sst, xs):
        qj, kj, vj, gj = xs
        bcum = jnp.cumsum(gj, axis=-2)
        diff = bcum[..., :, None, :] - bcum[..., None, :, :]
        dec = jnp.exp(jnp.where(causal, diff, -jnp.inf))
        A = jnp.einsum('bhjc,bhsc,bhjsc->bhjs', qj, kj, dec)
        o = jnp.einsum('bhjs,bhsv->bhjv', A, vj) + jnp.einsum('bhjc,bhcv->bhjv', qj * jnp.exp(bcum), Sst)
        bL = bcum[..., -1, :]
        S_new = jnp.exp(bL)[..., None] * Sst + jnp.einsum('bhsc,bhsv->bhcv', kj * jnp.exp(bL[..., None, :] - bcum), vj)
        return S_new, o

    init = jnp.zeros((b_, h_, dk, dv), jnp.float32)
    _, os_ = lax.scan(step, init, (to_chunks(q), to_chunks(k), to_chunks(v), to_chunks(logf)))
    return jnp.moveaxis(os_, 0, 2).reshape(b_, h_, s_, dv)


def hgrn2_layer(h, w_in, b_in, lb, head_norm, w_out):
    b_, s_, _ = h.shape
    proj = h @ w_in + b_in
    fp, iv, qp, z = jnp.split(proj, 4, axis=-1)
    logf = jnp.log(lb + (1.0 - lb) * jax.nn.sigmoid(fp))
    k = (1.0 - lb) * jax.nn.sigmoid(-fp)
    q = jax.nn.silu(qp)
    o = hgrn2_chunkwise(heads(q, HGRN_HEADS), heads(k, HGRN_HEADS), heads(iv, HGRN_HEADS), heads(logf, HGRN_HEADS))
    o = rms_norm(o.transpose(0, 2, 1, 3), head_norm.reshape(HGRN_HEADS, HGRN_HEAD_DIM)).reshape(b_, s_, HGRN_W)
    return (o * jax.nn.silu(z)) @ w_out


def even_layer(h, w_in, b_in, f_bias, conv_w, conv_b, head_norm, q_norm, k_norm,
               ck_pos, ck_w1, ck_w2, cv_pos, cv_w1, cv_w2, w_out):
    proj = h @ w_in + b_in
    m_cols, n_cols = jnp.split(proj, [MLSTM_COLS], axis=-1)
    ya = mlstm_branch(m_cols, f_bias, conv_w, conv_b, head_norm)
    yb = nsa_branch(n_cols, q_norm, k_norm, ck_pos, ck_w1, ck_w2, cv_pos, cv_w1, cv_w2)
    return jnp.concatenate([ya, yb], axis=-1) @ w_out


def setup_inputs(seed: int = 0) -> dict:
    key = jax.random.key(seed)
    ks = jax.random.split(key, 24)

    def nrm(k, shape, scale):
        return scale * jax.random.normal(k, shape, jnp.float32)

    NE, NO = N_EVEN, N_ODD
    return {
        "x": nrm(ks[0], (BATCH, SEQ, D_MODEL), 1.0),
        "norm_w": 1.0 + nrm(ks[1], (DEPTH, D_MODEL), 0.02),
        "ev_w_in": nrm(ks[2], (NE, D_MODEL, EVEN_IN_COLS), D_MODEL ** -0.5),
        "ev_b_in": nrm(ks[3], (NE, EVEN_IN_COLS), 0.02),
        "mlstm_f_bias": jnp.linspace(3.0, 6.0, MLSTM_HEADS, dtype=jnp.float32)[None, :] + nrm(ks[4], (NE, MLSTM_HEADS), 0.1),
        "mlstm_conv_w": nrm(ks[5], (NE, CONV_K, 2 * MLSTM_W), CONV_K ** -0.5),
        "mlstm_conv_b": nrm(ks[6], (NE, 2 * MLSTM_W), 0.02),
        "mlstm_head_norm": 1.0 + nrm(ks[7], (NE, MLSTM_W), 0.02),
        "nsa_q_norm": 1.0 + nrm(ks[8], (NE, NSA_HEAD_DIM), 0.02),
        "nsa_k_norm": 1.0 + nrm(ks[9], (NE, 3, NSA_HEAD_DIM), 0.02),
        "cmp_k_pos": nrm(ks[10], (NE, CMP_LEN, NSA_HEAD_DIM), 0.1),
        "cmp_k_w1": nrm(ks[11], (NE, CMP_LEN * NSA_HEAD_DIM, CMP_HIDDEN), (CMP_LEN * NSA_HEAD_DIM) ** -0.5),
        "cmp_k_w2": nrm(ks[12], (NE, CMP_HIDDEN, NSA_HEAD_DIM), CMP_HIDDEN ** -0.5),
        "cmp_v_pos": nrm(ks[13], (NE, CMP_LEN, NSA_HEAD_DIM), 0.1),
        "cmp_v_w1": nrm(ks[14], (NE, CMP_LEN * NSA_HEAD_DIM, CMP_HIDDEN), (CMP_LEN * NSA_HEAD_DIM) ** -0.5),
        "cmp_v_w2": nrm(ks[15], (NE, CMP_HIDDEN, NSA_HEAD_DIM), CMP_HIDDEN ** -0.5),
        "ev_w_out": nrm(ks[16], (NE, EVEN_MIX_W, D_MODEL), EVEN_MIX_W ** -0.5),
        "od_w_in": nrm(ks[17], (NO, D_MODEL, ODD_IN_COLS), D_MODEL ** -0.5),
        "od_b_in": nrm(ks[18], (NO, ODD_IN_COLS), 0.02),
        "hgrn_lb_logits": nrm(ks[19], (NO + 1, HGRN_W), 0.5),
        "hgrn_head_norm": 1.0 + nrm(ks[20], (NO, HGRN_W), 0.02),
        "od_w_out": nrm(ks[21], (NO, HGRN_W, D_MODEL), HGRN_W ** -0.5),
    }


def reference(x, norm_w, ev_w_in, ev_b_in, mlstm_f_bias, mlstm_conv_w, mlstm_conv_b, mlstm_head_norm,
              nsa_q_norm, nsa_k_norm, cmp_k_pos, cmp_k_w1, cmp_k_w2, cmp_v_pos, cmp_v_w1, cmp_v_w2,
              ev_w_out, od_w_in, od_b_in, hgrn_lb_logits, hgrn_head_norm, od_w_out):
    lbs = jnp.cumsum(jax.nn.softmax(hgrn_lb_logits.astype(jnp.float32), axis=0), axis=0)
    for layer in range(DEPTH):
        h = rms_norm(x, norm_w[layer])
        j = layer // 2
        if layer % 2 == 0:
            y = even_layer(h, ev_w_in[j], ev_b_in[j], mlstm_f_bias[j], mlstm_conv_w[j], mlstm_conv_b[j],
                           mlstm_head_norm[j], nsa_q_norm[j], nsa_k_norm[j], cmp_k_pos[j], cmp_k_w1[j],
                           cmp_k_w2[j], cmp_v_pos[j], cmp_v_w1[j], cmp_v_w2[j], ev_w_out[j])
        else:
            y = hgrn2_layer(h, od_w_in[j], od_b_in[j], lbs[j], hgrn_head_norm[j], od_w_out[j])
        x = x + y.astype(x.dtype)
    return x
```

```python
import functools

import numpy as np
import jax
import jax.numpy as jnp
from jax import lax
from jax.experimental import pallas as pl
from jax.experimental.pallas import tpu as pltpu

F32 = jnp.float32
BF16 = jnp.bfloat16

LANES = 128
EPS = 1e-6
NEG = -1e30
VMEM_LIMIT_BYTES = 56 * 1024 * 1024

M_HEADS = 4
M_DIM = 128
M_W = M_HEADS * M_DIM
M_CHUNK = 128
CONV_K = 4
N_HEADS = 8
N_GROUPS = 2
N_HPG = N_HEADS // N_GROUPS
N_DIM = 64
N_W = N_HEADS * N_DIM
N_KV_W = N_GROUPS * N_DIM
CMP_LEN = 32
CMP_STRIDE = 16
CMP_HIDDEN = 128
SEL_BLOCK = 64
SEL_SHIFT = 6
SEL_TOPK = 16
WINDOW = 512
Q_BLOCK = 128
ROPE_THETA = 500000.0
ROPE_DIM = N_DIM // 4
ROPE_HALF = ROPE_DIM // 2
KV_TILE = 512
H_HEADS = 8
H_DIM = 128
H_W = H_HEADS * H_DIM
H_TILE = 128
H_BLOCK = 16
H_BLOCK_SHIFT = 4

C_MQ, C_MK, C_MV, C_MO, C_MZ = 0, 512, 1024, 1536, 2048
C_NQ = 2560
C_KC, C_VC, C_KS, C_VS = 3072, 3200, 3328, 3456
C_KW, C_VW, C_SMALL = 3584, 3712, 3840
C_NZ = 4096
EV_COLS = 4608
SM_I, SM_F, SM_G = 0, 4, 8


def _cparams(*sem):
    return pltpu.CompilerParams(dimension_semantics=sem, vmem_limit_bytes=VMEM_LIMIT_BYTES)


def _dot(a, b):
    return jnp.dot(a, b, preferred_element_type=F32)


def _dot_nt(a, b):
    return lax.dot_general(a, b, (((1,), (1,)), ((), ())), preferred_element_type=F32)


def _dot_tn(a, b):
    return lax.dot_general(a, b, (((0,), (0,)), ((), ())), preferred_element_type=F32)


def _sigmoid(x):
    e = jnp.exp(-jnp.abs(x))
    r = 1.0 / (1.0 + e)
    return jnp.where(x >= 0, r, e * r)


def _silu(x):
    return x * _sigmoid(x)


def _split3(x):
    hi = x.astype(BF16)
    r = x - hi.astype(F32)
    mid = r.astype(BF16)
    lo = (r - mid.astype(F32)).astype(BF16)
    return hi, mid, lo


def _dot01(m01, x):
    hi, mid, lo = _split3(x)
    return _dot(m01, hi) + _dot(m01, mid) + _dot(m01, lo)


def _norm_proj_kernel(x_ref, nw_ref, w_ref, b_ref, o_ref, h_ref):
    @pl.when(pl.program_id(1) == 0)
    def _():
        x = x_ref[...]
        ms = jnp.mean(x * x, axis=-1, keepdims=True)
        h_ref[...] = (x * lax.rsqrt(ms + EPS) * nw_ref[...]).astype(BF16)

    o_ref[...] = (_dot(h_ref[...], w_ref[...]) + b_ref[...]).astype(o_ref.dtype)


def _norm_proj(x2d, norm_w, w_bf16, bias, tm=1024, tn=512):
    t, d = x2d.shape
    n = w_bf16.shape[1]
    return pl.pallas_call(
        _norm_proj_kernel,
        grid=(t // tm, n // tn),
        in_specs=[
            pl.BlockSpec((tm, d), lambda i, j: (i, 0)),
            pl.BlockSpec((1, d), lambda i, j: (0, 0)),
            pl.BlockSpec((d, tn), lambda i, j: (0, j)),
            pl.BlockSpec((1, tn), lambda i, j: (0, j)),
        ],
        out_specs=pl.BlockSpec((tm, tn), lambda i, j: (i, j)),
        out_shape=jax.ShapeDtypeStruct((t, n), F32),
        scratch_shapes=[pltpu.VMEM((tm, d), BF16)],
        compiler_params=_cparams("parallel", "arbitrary"),
        name="norm_proj",
    )(x2d, norm_w.reshape(1, d), w_bf16, bias.reshape(1, n))


def _out_proj2_kernel(ya_ref, yb_ref, wa_ref, wb_ref, x_ref, o_ref):
    o_ref[...] = x_ref[...] + _dot(ya_ref[...], wa_ref[...]) + _dot(yb_ref[...], wb_ref[...])


def _out_proj2(ya, yb, wa, wb, x2d, tm=1024):
    t, d = x2d.shape
    ka, kb = ya.shape[1], yb.shape[1]
    return pl.pallas_call(
        _out_proj2_kernel,
        grid=(t // tm,),
        in_specs=[
            pl.BlockSpec((tm, ka), lambda i: (i, 0)),
            pl.BlockSpec((tm, kb), lambda i: (i, 0)),
            pl.BlockSpec((ka, d), lambda i: (0, 0)),
            pl.BlockSpec((kb, d), lambda i: (0, 0)),
            pl.BlockSpec((tm, d), lambda i: (i, 0)),
        ],
        out_specs=pl.BlockSpec((tm, d), lambda i: (i, 0)),
        out_shape=jax.ShapeDtypeStruct((t, d), F32),
        compiler_params=_cparams("parallel"),
        name="out_proj2",
    )(ya, yb, wa, wb, x2d)


def _out_proj1_kernel(y_ref, w_ref, x_ref, o_ref):
    o_ref[...] = x_ref[...] + _dot(y_ref[...], w_ref[...])


def _out_proj1(y, w, x2d, tm=1024):
    t, d = x2d.shape
    k = y.shape[1]
    return pl.pallas_call(
        _out_proj1_kernel,
        grid=(t // tm,),
        in_specs=[
            pl.BlockSpec((tm, k), lambda i: (i, 0)),
            pl.BlockSpec((k, d), lambda i: (0, 0)),
            pl.BlockSpec((tm, d), lambda i: (i, 0)),
        ],
        out_specs=pl.BlockSpec((tm, d), lambda i: (i, 0)),
        out_shape=jax.ShapeDtypeStruct((t, d), F32),
        compiler_params=_cparams("parallel"),
        name="out_proj1",
    )(y, w, x2d)


def _mlstm_kernel(q_ref, k_ref, v_ref, o_ref, z_ref, sm_ref, cw_ref, cb_ref, fb_ref, hn_ref,
                  out_ref, xbuf, c_ref, n_ref, m_ref):
    L, D = M_CHUNK, M_DIM

    @pl.when(pl.program_id(1) == 0)
    def _():
        xbuf[0:8, :] = jnp.zeros((8, 2 * M_W), F32)
        c_ref[...] = jnp.zeros_like(c_ref)
        n_ref[...] = jnp.zeros_like(n_ref)
        m_ref[...] = jnp.zeros_like(m_ref)

    xbuf[8:8 + L, 0:M_W] = q_ref[...]
    xbuf[8:8 + L, M_W:2 * M_W] = k_ref[...]
    acc = jnp.broadcast_to(cb_ref[...], (L, 2 * M_W))
    for j in range(CONV_K):
        acc = acc + cw_ref[j:j + 1, :] * xbuf[8 - (CONV_K - 1) + j:8 - (CONV_K - 1) + j + L, :]
    qk = _silu(acc)
    xbuf[0:8, :] = xbuf[L:L + 8, :]

    sm = sm_ref[...]
    row = lax.broadcasted_iota(jnp.int32, (L, L), 0)
    col = lax.broadcasted_iota(jnp.int32, (L, L), 1)
    eye = (row == col).astype(F32)
    tri = (col <= row).astype(F32)
    tri_t = (row <= col).astype(F32)
    causal = col <= row

    for h in range(M_HEADS):
        hs = slice(h * D, (h + 1) * D)
        q = qk[:, hs]
        k = qk[:, M_W + h * D:M_W + (h + 1) * D] * (D ** -0.5)
        v = v_ref[:, hs]
        i_col = sm[:, SM_I + h:SM_I + h + 1]
        fpre = sm[:, SM_F + h:SM_F + h + 1] + fb_ref[:, SM_F + h:SM_F + h + 1]
        f_col = jnp.minimum(fpre, 0.0) - jnp.log(1.0 + jnp.exp(-jnp.abs(fpre)))
        f_row = jnp.sum(eye * f_col, axis=0, keepdims=True)
        i_row = jnp.sum(eye * i_col, axis=0, keepdims=True)
        bcum_col = jnp.sum(tri * f_row, axis=1, keepdims=True)
        bcum_row = jnp.sum(tri_t * f_col, axis=0, keepdims=True)
        m_prev = m_ref[h]
        dmat = jnp.where(causal, bcum_col - bcum_row + i_row, -jnp.inf)
        inter = bcum_col + m_prev
        m_row = jnp.maximum(jnp.max(dmat, axis=1, keepdims=True), inter)
        qb = q.astype(BF16)
        kb = k.astype(BF16)
        vb = v.astype(BF16)
        smat = _dot_nt(qb, kb) * jnp.exp(dmat - m_row)
        w_inter = jnp.exp(inter - m_row)
        num = _dot(smat.astype(BF16), vb) + w_inter * _dot(qb, c_ref[h].astype(BF16))
        den = jnp.sum(smat, axis=1, keepdims=True) + w_inter * jnp.sum(q * n_ref[h], axis=1, keepdims=True)
        hh = num / jnp.maximum(jnp.abs(den), jnp.exp(-m_row))
        b_last = bcum_col[L - 1:L, :]
        dl = b_last - bcum_col + i_col
        m_new = jnp.maximum(b_last + m_prev, jnp.max(dl, axis=0, keepdims=True))
        wk = jnp.exp(dl - m_new)
        decay = jnp.exp(b_last + m_prev - m_new)
        wkk = wk * k
        c_ref[h] = decay * c_ref[h] + _dot_tn(wkk.astype(BF16), vb)
        n_ref[h] = decay * n_ref[h] + jnp.sum(wkk, axis=0, keepdims=True)
        m_ref[h] = m_new
        hn = hh * lax.rsqrt(jnp.mean(hh * hh, axis=1, keepdims=True) + EPS) * hn_ref[:, hs]
        out_ref[:, hs] = (hn * _sigmoid(o_ref[:, hs]) * _silu(z_ref[:, hs])).astype(out_ref.dtype)


def _mlstm(proj, conv_w, conv_b, f_bias_row, head_norm, batch, seq):
    L = M_CHUNK
    nc = seq // L
    cb = lambda off: pl.BlockSpec((L, M_W), lambda b, c, off=off: (b * nc + c, off // M_W))
    return pl.pallas_call(
        _mlstm_kernel,
        grid=(batch, nc),
        in_specs=[
            cb(C_MQ), cb(C_MK), cb(C_MV), cb(C_MO), cb(C_MZ),
            pl.BlockSpec((L, LANES), lambda b, c: (b * nc + c, C_SMALL // LANES)),
            pl.BlockSpec((CONV_K, 2 * M_W), lambda b, c: (0, 0)),
            pl.BlockSpec((1, 2 * M_W), lambda b, c: (0, 0)),
            pl.BlockSpec((1, LANES), lambda b, c: (0, 0)),
            pl.BlockSpec((1, M_W), lambda b, c: (0, 0)),
        ],
        out_specs=pl.BlockSpec((L, M_W), lambda b, c: (b * nc + c, 0)),
        out_shape=jax.ShapeDtypeStruct((batch * seq, M_W), BF16),
        scratch_shapes=[
            pltpu.VMEM((L + 8, 2 * M_W), F32),
            pltpu.VMEM((M_HEADS, M_DIM, M_DIM), F32),
            pltpu.VMEM((M_HEADS, 1, M_DIM), F32),
            pltpu.VMEM((M_HEADS, 1, 1), F32),
        ],
        compiler_params=_cparams("parallel", "arbitrary"),
        name="mlstm",
    )(proj, proj, proj, proj, proj, proj, conv_w, conv_b.reshape(1, -1), f_bias_row, head_norm.reshape(1, -1))


def _group_rms(x, w_row):
    lane = lax.broadcasted_iota(jnp.int32, x.shape, 1)
    left = lane < N_DIM
    sq = x * x
    tot = jnp.sum(sq, axis=1, keepdims=True)
    lft = jnp.sum(jnp.where(left, sq, 0.0), axis=1, keepdims=True)
    ms = jnp.where(left, lft, tot - lft) * (1.0 / N_DIM)
    return x * lax.rsqrt(ms + EPS) * w_row


def _rope(x, cs, s1, s2):
    return x * cs + pltpu.roll(x, ROPE_HALF, 1) * s1 + pltpu.roll(x, LANES - ROPE_HALF, 1) * s2


def _dup_halves(x):
    lane = lax.broadcasted_iota(jnp.int32, x.shape, 1)
    sw = pltpu.roll(x, N_DIM, 1)
    left = lane < N_DIM
    return jnp.where(left, x, sw), jnp.where(left, sw, x)


def _nsa_prep_kernel(q_ref, kv_ref, kw_ref, cs_ref, s1_ref, s2_ref, qn_ref, kn_ref,
                     qh_ref, ka_ref, kwd_ref, vsa_ref, vsb_ref, vwa_ref, vwb_ref, kc_ref, vc_ref, gt_ref,
                     *, tp, seq):
    cs, s1, s2 = cs_ref[...], s1_ref[...], s2_ref[...]
    for j in range(N_W // LANES):
        sl = slice(j * LANES, (j + 1) * LANES)
        qn = _group_rms(q_ref[:, sl], qn_ref[...])
        qh_ref[:, sl] = (_rope(qn, cs, s1, s2) * (N_DIM ** -0.5)).astype(BF16)
    kv = kv_ref[...]
    kc_ref[...] = kv[:, 0:LANES]
    vc_ref[...] = kv[:, LANES:2 * LANES]
    ks = _rope(_group_rms(kv[:, 2 * LANES:3 * LANES], kn_ref[1:2, :]), cs, s1, s2)
    k0, k1 = _dup_halves(ks)
    pos0 = (pl.program_id(0) * tp) % seq
    pos = pos0 + lax.broadcasted_iota(jnp.int32, (tp, LANES), 0)
    blk = lax.broadcasted_iota(jnp.int32, (tp, LANES), 1)
    onehot = jnp.where(jnp.right_shift(pos, SEL_SHIFT) == blk, 1.0, 0.0).astype(BF16)
    ka_ref[:, 0:LANES] = k0.astype(BF16)
    ka_ref[:, LANES:2 * LANES] = onehot
    ka_ref[:, 2 * LANES:3 * LANES] = k1.astype(BF16)
    ka_ref[:, 3 * LANES:4 * LANES] = onehot
    vs = kv[:, 3 * LANES:4 * LANES]
    vsa_ref[...] = vs.astype(BF16)
    vsb_ref[...] = pltpu.roll(vs, N_DIM, 1).astype(BF16)
    kwb = kw_ref[...]
    kw = _rope(_group_rms(kwb[:, 0:LANES], kn_ref[2:3, :]), cs, s1, s2)
    w0, w1 = _dup_halves(kw)
    kwd_ref[:, 0:LANES] = w0.astype(BF16)
    kwd_ref[:, LANES:2 * LANES] = w1.astype(BF16)
    vw = kwb[:, LANES:2 * LANES]
    vwa_ref[...] = vw.astype(BF16)
    vwb_ref[...] = pltpu.roll(vw, N_DIM, 1).astype(BF16)
    gt_ref[...] = _sigmoid(kwb[:, 2 * LANES:3 * LANES])


def _nsa_prep(proj, cs, s1, s2, qn_row, kn_rows, batch, seq, tp=512):
    t = batch * seq
    nsb = seq // tp
    rb = lambda w, off: pl.BlockSpec((tp, w), lambda i, off=off, w=w: (i, off // w))
    tab = pl.BlockSpec((tp, LANES), lambda i: (i % nsb, 0))
    o = lambda w: pl.BlockSpec((tp, w), lambda i: (i, 0))
    sds = lambda w, dt: jax.ShapeDtypeStruct((t, w), dt)
    return pl.pallas_call(
        functools.partial(_nsa_prep_kernel, tp=tp, seq=seq),
        grid=(t // tp,),
        in_specs=[rb(512, C_NQ), rb(512, C_KC), rb(512, C_KW), tab, tab, tab,
                  pl.BlockSpec((1, LANES), lambda i: (0, 0)), pl.BlockSpec((3, LANES), lambda i: (0, 0))],
        out_specs=[o(512), o(512), o(256), o(128), o(128), o(128), o(128), o(128), o(128), o(128)],
        out_shape=[sds(512, BF16), sds(512, BF16), sds(256, BF16), sds(128, BF16), sds(128, BF16),
                   sds(128, BF16), sds(128, BF16), sds(128, F32), sds(128, F32), sds(128, F32)],
        compiler_params=_cparams("parallel"),
        name="nsa_prep",
    )(proj, proj, proj, cs, s1, s2, qn_row, kn_rows)


def _gelu_tanh(x):
    return 0.5 * x * (1.0 + jnp.tanh(0.7978845608028654 * (x + 0.044715 * x * x * x)))


def _compress_mlp(x_ref, pa_ref, pb_ref, wa_ref, wb_ref, w2_ref, hb_ref):
    n = x_ref.shape[0]
    x = x_ref[...]
    ha = _dot((x + pa_ref[...]).astype(BF16), wa_ref[...])
    hb_ref[0:n, :] = _dot((x + pb_ref[...]).astype(BF16), wb_ref[...])
    hb_ref[n:n + 8, :] = jnp.zeros((8, hb_ref.shape[1]), F32)
    hid = ha + hb_ref[1:n + 1, :]
    return _dot(_gelu_tanh(hid).astype(BF16), w2_ref[...])


def _nsa_compress_kernel(kx_ref, vx_ref, kpa_ref, kpb_ref, vpa_ref, vpb_ref, kwa_ref, kwb_ref, kw2_ref,
                         vwa_ref, vwb_ref, vw2_ref, cs_ref, s1_ref, s2_ref, kn_ref,
                         kcd_ref, vca_ref, vcb_ref, hb_ref):
    n = kx_ref.shape[0]
    rowi = lax.broadcasted_iota(jnp.int32, (n, LANES), 0)
    live = rowi < n - 1
    kc = _compress_mlp(kx_ref, kpa_ref, kpb_ref, kwa_ref, kwb_ref, kw2_ref, hb_ref)
    kc = _rope(_group_rms(kc, kn_ref[0:1, :]), cs_ref[...], s1_ref[...], s2_ref[...])
    kc = jnp.where(live, kc, 0.0)
    c0, c1 = _dup_halves(kc)
    kcd_ref[:, 0:LANES] = c0.astype(BF16)
    kcd_ref[:, LANES:2 * LANES] = c1.astype(BF16)
    vc = _compress_mlp(vx_ref, vpa_ref, vpb_ref, vwa_ref, vwb_ref, vw2_ref, hb_ref)
    vc = jnp.where(live, vc, 0.0)
    vca_ref[...] = vc.astype(BF16)
    vcb_ref[...] = pltpu.roll(vc, N_DIM, 1).astype(BF16)


def _nsa_compress(kx, vx, kp, vp, cs, s1, s2, kn_rows, batch, nrow):
    wide = CMP_STRIDE * LANES
    full = lambda a: pl.BlockSpec(a.shape, lambda b: (0,) * a.ndim)
    xs = pl.BlockSpec((nrow, wide), lambda b: (b, 0))
    o = lambda w: pl.BlockSpec((nrow, w), lambda b: (b, 0))
    sds = lambda w: jax.ShapeDtypeStruct((batch * nrow, w), BF16)
    args = (kx, vx, kp[0], kp[1], vp[0], vp[1], kp[2], kp[3], kp[4], vp[2], vp[3], vp[4], cs, s1, s2, kn_rows)
    return pl.pallas_call(
        _nsa_compress_kernel,
        grid=(batch,),
        in_specs=[xs, xs] + [full(a) for a in args[2:]],
        out_specs=[o(256), o(128), o(128)],
        out_shape=[sds(256), sds(128), sds(128)],
        scratch_shapes=[pltpu.VMEM((nrow + 8, 2 * CMP_HIDDEN), F32)],
        compiler_params=_cparams("parallel"),
        name="nsa_compress",
    )(*args)


def _compress_params(pos, w1, w2):
    def expand(w1_half, pos_half):
        w = w1_half.reshape(CMP_STRIDE, N_DIM, CMP_HIDDEN)
        z = jnp.zeros_like(w)
        g0 = jnp.concatenate([w, z], axis=-1)
        g1 = jnp.concatenate([z, w], axis=-1)
        wx = jnp.stack([g0, g1], axis=1).reshape(CMP_STRIDE * LANES, 2 * CMP_HIDDEN)
        px = jnp.concatenate([pos_half, pos_half], axis=-1).reshape(1, CMP_STRIDE * LANES)
        return wx.astype(BF16), px.astype(F32)

    half = CMP_STRIDE * N_DIM
    wa, pa = expand(w1[:half], pos[:CMP_STRIDE])
    wb, pb = expand(w1[half:], pos[CMP_STRIDE:])
    z2 = jnp.zeros_like(w2)
    w2x = jnp.concatenate([jnp.concatenate([w2, z2], axis=1), jnp.concatenate([z2, w2], axis=1)], axis=0)
    return pa, pb, wa, wb, w2x.astype(BF16)


def _nsa_attn_kernel(qh_ref, gt_ref, z_ref, ka_ref, vsa_ref, vsb_ref,
                     kw0, kw1, kw2, kw3, kw4, wa0, wa1, wa2, wa3, wa4, wb0, wb1, wb2, wb3, wb4,
                     kcd_ref, vca_ref, vcb_ref, ov_ref, out_ref, *, ncmp):
    QB = Q_BLOCK
    i = pl.program_id(1)
    t0 = i * QB
    kw_refs = (kw0, kw1, kw2, kw3, kw4)
    vwa_refs = (wa0, wa1, wa2, wa3, wa4)
    vwb_refs = (wb0, wb1, wb2, wb3, wb4)
    nwt = len(kw_refs)

    lane = lax.broadcasted_iota(jnp.int32, (QB, LANES), 1)
    rowq = lax.broadcasted_iota(jnp.int32, (QB, LANES), 0)
    left = lane < N_DIM
    t_col = t0 + lax.broadcasted_iota(jnp.int32, (QB, 1), 0)
    gates = gt_ref[...]

    def head_q(h):
        slab = qh_ref[:, (h // 2) * LANES:(h // 2 + 1) * LANES]
        keep = left if h % 2 == 0 else jnp.logical_not(left)
        return jnp.where(keep, slab, jnp.zeros_like(slab))

    def pick_v(h, g, a, b):
        return a if (h % 2) == g else b

    ncol = lax.broadcasted_iota(jnp.int32, (QB, ncmp), 1)
    cmp_ok = (ncol * CMP_STRIDE + (CMP_LEN - 1) <= t_col) & (ncol < ncmp - 1)
    blk_f = lane.astype(F32)
    cur = jnp.right_shift(t_col, SEL_SHIFT)
    o_cmp = [None] * N_HEADS
    negmask = [None] * N_GROUPS
    for g in range(N_GROUPS):
        kc = kcd_ref[:, g * LANES:(g + 1) * LANES]
        psum = jnp.zeros((QB, ncmp), F32)
        for hh in range(N_HPG):
            h = g * N_HPG + hh
            s = jnp.where(cmp_ok, _dot_nt(head_q(h), kc), NEG)
            mx = jnp.max(s, axis=1, keepdims=True)
            p = jnp.where(cmp_ok, jnp.exp(s - mx), 0.0)
            den = jnp.sum(p, axis=1, keepdims=True)
            p = p * (1.0 / jnp.maximum(den, 1e-30))
            o_cmp[h] = _dot(p.astype(BF16), pick_v(h, g, vca_ref, vcb_ref)[...])
            psum = psum + p
        p_hi = psum.astype(BF16)
        p_lo = (psum - p_hi.astype(F32)).astype(BF16)
        imp = _dot(p_hi, ov_ref[...]) + _dot(p_lo, ov_ref[...])
        forced = (lane == 0) | (lane == cur) | (lane == cur - 1)
        work = jnp.where(lane > cur, NEG, jnp.where(forced, -NEG, imp))
        sel = jnp.zeros((QB, LANES), jnp.bool_)
        for _ in range(SEL_TOPK):
            mx = jnp.max(work, axis=1, keepdims=True)
            idx = jnp.min(jnp.where(work == mx, blk_f, float(LANES)), axis=1, keepdims=True)
            pick = blk_f == idx
            sel = sel | pick
            work = jnp.where(pick, -jnp.inf, work)
        sel = sel & (lane <= cur)
        negmask[g] = jnp.where(sel, 0.0, NEG).astype(BF16)

    n_tiles = (t0 + QB + KV_TILE - 1) // KV_TILE
    kpos = lax.broadcasted_iota(jnp.int32, (QB, KV_TILE), 1)
    o_sel = [None] * N_HEADS
    for h in range(N_HEADS):
        g = h // N_HPG
        q_aug = jnp.concatenate([head_q(h), negmask[g]], axis=1)
        v_ref = pick_v(h, g, vsa_ref, vsb_ref)

        def tile_step(jt, carry, q_aug=q_aug, v_ref=v_ref, g=g):
            m, l, acc = carry
            k0 = pl.multiple_of(jt * KV_TILE, KV_TILE)
            kt = ka_ref[pl.ds(k0, KV_TILE), 2 * g * LANES:2 * (g + 1) * LANES]
            s = _dot_nt(q_aug, kt)
            s = jnp.where(k0 + kpos <= t_col, s, NEG)
            m_new = jnp.maximum(m, jnp.max(s, axis=1, keepdims=True))
            alpha = jnp.exp(m - m_new)
            p = jnp.exp(s - m_new)
            l = alpha * l + jnp.sum(p, axis=1, keepdims=True)
            acc = alpha * acc + _dot(p.astype(BF16), v_ref[pl.ds(k0, KV_TILE), :])
            return m_new, l, acc

        init = (jnp.full((QB, 1), 2.0 * NEG, F32), jnp.zeros((QB, 1), F32), jnp.zeros((QB, LANES), F32))
        m, l, acc = lax.fori_loop(0, n_tiles, tile_step, init)
        o_sel[h] = acc * (1.0 / l)

    o_win = [None] * N_HEADS
    for h in range(N_HEADS):
        g = h // N_HPG
        qm = head_q(h)
        ss = []
        for c in range(nwt):
            s = _dot_nt(qm, kw_refs[c][:, g * LANES:(g + 1) * LANES])
            ok = (i - (nwt - 1) + c) >= 0
            if c == 0:
                ok = ok & (lane > rowq)
            elif c == nwt - 1:
                ok = ok & (lane <= rowq)
            ss.append(jnp.where(ok, s, NEG))
        mx = ss[0].max(axis=1, keepdims=True)
        for c in range(1, nwt):
            mx = jnp.maximum(mx, ss[c].max(axis=1, keepdims=True))
        den = jnp.zeros((QB, 1), F32)
        acc = jnp.zeros((QB, LANES), F32)
        for c in range(nwt):
            p = jnp.exp(ss[c] - mx)
            den = den + jnp.sum(p, axis=1, keepdims=True)
            acc = acc + _dot(p.astype(BF16), pick_v(h, g, vwa_refs[c], vwb_refs[c])[...])
        o_win[h] = acc * (1.0 / den)

    for j in range(N_HEADS // 2):
        pair = []
        for h in (2 * j, 2 * j + 1):
            gc = gates[:, SM_G + 3 * h:SM_G + 3 * h + 1]
            gs = gates[:, SM_G + 3 * h + 1:SM_G + 3 * h + 2]
            gw = gates[:, SM_G + 3 * h + 2:SM_G + 3 * h + 3]
            pair.append(gc * o_cmp[h] + gs * o_sel[h] + gw * o_win[h])
        sl = slice(j * LANES, (j + 1) * LANES)
        out_ref[:, sl] = (jnp.where(left, pair[0], pair[1]) * _silu(z_ref[:, sl])).astype(out_ref.dtype)


def _nsa_attn(proj, qh, gates, ka, vsa, vsb, kwd, vwa, vwb, kcd, vca, vcb, ov, batch, seq):
    QB = Q_BLOCK
    nqb = seq // QB
    ncmp = seq // CMP_STRIDE
    nwt = WINDOW // QB + 1
    row = lambda w: pl.BlockSpec((QB, w), lambda b, i: (b * nqb + i, 0))
    per_b = lambda n, w: pl.BlockSpec((n, w), lambda b, i: (b, 0))
    win = lambda w, c: pl.BlockSpec((QB, w), lambda b, i, c=c: (b * nqb + jnp.maximum(i - (nwt - 1) + c, 0), 0))
    in_specs = ([row(N_W), row(LANES), pl.BlockSpec((QB, N_W), lambda b, i: (b * nqb + i, C_NZ // N_W)),
                 per_b(seq, 4 * LANES), per_b(seq, LANES), per_b(seq, LANES)]
                + [win(2 * LANES, c) for c in range(nwt)]
                + [win(LANES, c) for c in range(nwt)]
                + [win(LANES, c) for c in range(nwt)]
                + [per_b(ncmp, 2 * LANES), per_b(ncmp, LANES), per_b(ncmp, LANES),
                   pl.BlockSpec((ncmp, LANES), lambda b, i: (0, 0))])
    return pl.pallas_call(
        functools.partial(_nsa_attn_kernel, ncmp=ncmp),
        grid=(batch, nqb),
        in_specs=in_specs,
        out_specs=row(N_W),
        out_shape=jax.ShapeDtypeStruct((batch * seq, N_W), BF16),
        compiler_params=_cparams("parallel", "arbitrary"),
        name="nsa_attn",
    )(qh, gates, proj, ka, vsa, vsb, *([kwd] * nwt), *([vwa] * nwt), *([vwb] * nwt), kcd, vca, vcb, ov)


def _hgrn_kernel(f_ref, i_ref, q_ref, z_ref, lb_ref, hn_ref, out_ref, kpad, bpad, vpad, st_ref):
    T, D, BL = H_TILE, H_DIM, H_BLOCK
    nblk = T // BL

    @pl.when(pl.program_id(1) == 0)
    def _():
        st_ref[...] = jnp.zeros_like(st_ref)

    lb = lb_ref[...]
    fp = f_ref[...]
    e = jnp.exp(-jnp.abs(fp))
    r = 1.0 / (1.0 + e)
    sig = jnp.where(fp >= 0, r, e * r)
    nsig = jnp.where(fp >= 0, e * r, r)
    logf = jnp.log(lb + (1.0 - lb) * sig)
    kk = (1.0 - lb) * nsig

    row = lax.broadcasted_iota(jnp.int32, (T, T), 0)
    col = lax.broadcasted_iota(jnp.int32, (T, T), 1)
    same = jnp.right_shift(row, H_BLOCK_SHIFT) == jnp.right_shift(col, H_BLOCK_SHIFT)
    tri01 = jnp.where(same & (col <= row), 1.0, 0.0).astype(BF16)
    blk01 = jnp.where(same, 1.0, 0.0).astype(BF16)
    hi, mid, lo = _split3(logf)
    bc = _dot(tri01, hi) + _dot(tri01, mid) + _dot(tri01, lo)
    bend = _dot(blk01, hi) + _dot(blk01, mid) + _dot(blk01, lo)

    zpad = jnp.zeros((BL, H_W), F32)
    kpad[0:BL, :] = zpad
    bpad[0:BL, :] = zpad
    vpad[0:BL, :] = zpad
    kpad[BL:BL + T, :] = kk
    bpad[BL:BL + T, :] = bc
    vpad[BL:BL + T, :] = i_ref[...]

    rmod = jnp.bitwise_and(lax.broadcasted_iota(jnp.int32, (T, 1), 0), BL - 1)
    for h in range(H_HEADS):
        hs = slice(h * D, (h + 1) * D)
        q = _silu(q_ref[:, hs])
        b = bc[:, hs]
        v = vpad[BL:BL + T, hs]
        o = jnp.zeros((T, D), F32)
        for d in range(BL):
            ks = kpad[BL - d:BL - d + T, hs]
            bs = bpad[BL - d:BL - d + T, hs]
            vs = vpad[BL - d:BL - d + T, hs]
            a = jnp.sum(q * ks * jnp.exp(b - bs), axis=1, keepdims=True)
            o = o + jnp.where(rmod >= d, a, 0.0) * vs
        qt = (q * jnp.exp(b)).astype(BF16)
        kt = (kk[:, hs] * jnp.exp(bend[:, hs] - b)).astype(BF16)
        dec = jnp.exp(bend[:, hs])
        vb = v.astype(BF16)
        st = st_ref[h]
        outs = []
        for n in range(nblk):
            rs = slice(n * BL, (n + 1) * BL)
            outs.append(o[rs] + _dot_nt(qt[rs], st.astype(BF16)))
            st = st * dec[n * BL:n * BL + 1, :] + _dot_tn(vb[rs], kt[rs])
        st_ref[h] = st
        oh = jnp.concatenate(outs, axis=0)
        on = oh * lax.rsqrt(jnp.mean(oh * oh, axis=1, keepdims=True) + EPS) * hn_ref[:, hs]
        out_ref[:, hs] = (on * _silu(z_ref[:, hs])).astype(out_ref.dtype)


def _hgrn(proj, lb_row, hn_row, batch, seq):
    T = H_TILE
    nt = seq // T
    cb = lambda j: pl.BlockSpec((T, H_W), lambda b, c, j=j: (b * nt + c, j))
    vec = pl.BlockSpec((1, H_W), lambda b, c: (0, 0))
    return pl.pallas_call(
        _hgrn_kernel,
        grid=(batch, nt),
        in_specs=[cb(0), cb(1), cb(2), cb(3), vec, vec],
        out_specs=pl.BlockSpec((T, H_W), lambda b, c: (b * nt + c, 0)),
        out_shape=jax.ShapeDtypeStruct((batch * seq, H_W), BF16),
        scratch_shapes=[pltpu.VMEM((T + H_BLOCK, H_W), F32)] * 3 + [pltpu.VMEM((H_HEADS, H_DIM, H_DIM), F32)],
        compiler_params=_cparams("parallel", "arbitrary"),
        name="hgrn2",
    )(proj, proj, proj, proj, lb_row, hn_row)


def _even_relayout(a):
    nb = 5 * M_W + 2 * M_HEADS
    g0 = nb + N_W + 6 * N_KV_W
    col = lambda lo, n: lax.slice_in_dim(a, lo, lo + n, axis=-1)
    pad = jnp.zeros(a.shape[:-1] + (C_NZ - C_SMALL - SM_G - 3 * N_HEADS,), a.dtype)
    out = jnp.concatenate([
        col(0, 5 * M_W),
        col(nb, N_W + 6 * N_KV_W),
        col(5 * M_W, 2 * M_HEADS),
        col(g0, 3 * N_HEADS), pad,
        col(g0 + 3 * N_HEADS, N_W),
    ], axis=-1)
    assert out.shape[-1] == EV_COLS
    return out


def _rope_tables(positions):
    inv_freq = ROPE_THETA ** (-np.arange(ROPE_HALF, dtype=np.float64) / ROPE_HALF)
    ang = positions.astype(np.float64)[:, None] * inv_freq[None, :]
    cos, sin = np.cos(ang).astype(np.float32), np.sin(ang).astype(np.float32)
    n = positions.shape[0]
    z8 = np.zeros((n, ROPE_HALF), np.float32)
    rest1 = np.ones((n, N_DIM - ROPE_DIM), np.float32)
    rest0 = np.zeros((n, N_DIM - ROPE_DIM), np.float32)
    cs = np.concatenate([cos, cos, rest1], axis=1)
    s1 = np.concatenate([z8, sin, rest0], axis=1)
    s2 = np.concatenate([-sin, z8, rest0], axis=1)
    two = lambda a: np.concatenate([a, a], axis=1)
    return two(cs), two(s1), two(s2)


def _overlap_matrix(seq):
    ncmp = seq // CMP_STRIDE
    n_sel = seq // SEL_BLOCK
    cs = np.arange(ncmp) * CMP_STRIDE
    ss = np.arange(n_sel) * SEL_BLOCK
    ov = np.clip(np.minimum(cs[:, None] + CMP_LEN, ss[None, :] + SEL_BLOCK)
                 - np.maximum(cs[:, None], ss[None, :]), 0, None).astype(np.float32)
    ov[ncmp - 1, :] = 0.0
    return np.pad(ov, ((0, 0), (0, LANES - n_sel)))


def _even_layer(x2d, norm_w, w_in, b_in, f_bias, conv_w, conv_b, head_norm, q_norm, k_norm,
                ck_pos, ck_w1, ck_w2, cv_pos, cv_w1, cv_w2, w_out, batch, seq):
    assert seq // SEL_BLOCK <= LANES and seq % KV_TILE == 0
    proj = _norm_proj(x2d, norm_w, _even_relayout(w_in).astype(BF16), _even_relayout(b_in))

    f_bias_row = jnp.zeros((1, LANES), F32).at[0, SM_F:SM_F + M_HEADS].set(f_bias)
    ya = _mlstm(proj, conv_w, conv_b, f_bias_row, head_norm, batch, seq)

    cs, s1, s2 = (jnp.asarray(a) for a in _rope_tables(np.arange(seq)))
    qn_row = jnp.concatenate([q_norm, q_norm]).reshape(1, LANES)
    kn_rows = jnp.concatenate([k_norm, k_norm], axis=1)
    qh, ka, kwd, vsa, vsb, vwa, vwb, kc, vc, gates = _nsa_prep(proj, cs, s1, s2, qn_row, kn_rows, batch, seq)

    ncmp = seq // CMP_STRIDE
    ccs, cs1, cs2 = (jnp.asarray(a) for a in _rope_tables(np.arange(ncmp) * CMP_STRIDE + CMP_LEN - 1))
    kcd, vca, vcb = _nsa_compress(
        kc.reshape(batch * ncmp, CMP_STRIDE * LANES), vc.reshape(batch * ncmp, CMP_STRIDE * LANES),
        _compress_params(ck_pos, ck_w1, ck_w2), _compress_params(cv_pos, cv_w1, cv_w2),
        ccs, cs1, cs2, kn_rows, batch, ncmp)

    ov = jnp.asarray(_overlap_matrix(seq)).astype(BF16)
    yb = _nsa_attn(proj, qh, gates, ka, vsa, vsb, kwd, vwa, vwb, kcd, vca, vcb, ov, batch, seq)

    w_o = w_out.astype(BF16)
    return _out_proj2(ya, yb, w_o[:M_W], w_o[M_W:], x2d)


def _odd_layer(x2d, norm_w, w_in, b_in, lb, head_norm, w_out, batch, seq):
    proj = _norm_proj(x2d, norm_w, w_in.astype(BF16), b_in)
    y = _hgrn(proj, lb.reshape(1, H_W), head_norm.reshape(1, H_W), batch, seq)
    return _out_proj1(y, w_out.astype(BF16), x2d)


def kernel(x, norm_w, ev_w_in, ev_b_in, mlstm_f_bias, mlstm_conv_w, mlstm_conv_b, mlstm_head_norm, nsa_q_norm, nsa_k_norm, cmp_k_pos, cmp_k_w1, cmp_k_w2, cmp_v_pos, cmp_v_w1, cmp_v_w2, ev_w_out, od_w_in, od_b_in, hgrn_lb_logits, hgrn_head_norm, od_w_out):
    batch, seq, d = x.shape
    depth = norm_w.shape[0]
    lbs = jnp.cumsum(jax.nn.softmax(hgrn_lb_logits.astype(F32), axis=0), axis=0)
    x2d = x.reshape(batch * seq, d)
    for layer in range(depth):
        j = layer // 2
        if layer % 2 == 0:
            x2d = _even_layer(x2d, norm_w[layer], ev_w_in[j], ev_b_in[j], mlstm_f_bias[j], mlstm_conv_w[j],
                              mlstm_conv_b[j], mlstm_head_norm[j], nsa_q_norm[j], nsa_k_norm[j], cmp_k_pos[j],
                              cmp_k_w1[j], cmp_k_w2[j], cmp_v_pos[j], cmp_v_w1[j], cmp_v_w2[j], ev_w_out[j],
                              batch, seq)
        else:
            x2d = _odd_layer(x2d, norm_w[layer], od_w_in[j], od_b_in[j], lbs[j], hgrn_head_norm[j], od_w_out[j],
                             batch, seq)
    return x2d.reshape(batch, seq, d)
```

```python
import functools

import numpy as np
import jax
import jax.numpy as jnp
from jax import lax
from jax.experimental import pallas as pl
from jax.experimental.pallas import tpu as pltpu

F32 = jnp.float32
BF16 = jnp.bfloat16

LANES = 128
EPS = 1e-6
NEG = -1e30
VMEM_LIMIT_BYTES = 56 * 1024 * 1024

M_HEADS = 4
M_DIM = 128
M_W = M_HEADS * M_DIM
M_CHUNK = 128
CONV_K = 4
N_HEADS = 8
N_GROUPS = 2
N_HPG = N_HEADS // N_GROUPS
N_DIM = 64
N_W = N_HEADS * N_DIM
N_KV_W = N_GROUPS * N_DIM
CMP_LEN = 32
CMP_STRIDE = 16
CMP_HIDDEN = 128
SEL_BLOCK = 64
SEL_SHIFT = 6
SEL_TOPK = 16
WINDOW = 512
Q_BLOCK = 128
ROPE_THETA = 500000.0
ROPE_DIM = N_DIM // 4
ROPE_HALF = ROPE_DIM // 2
KV_TILE = 512
V_ROWS = 80
LOG2E = 1.4426950408889634
H_HEADS = 8
H_DIM = 128
H_W = H_HEADS * H_DIM
H_TILE = 128
H_BLOCK = 16
H_BLOCK_SHIFT = 4

C_MQ, C_MK, C_MV, C_MO, C_MZ = 0, 512, 1024, 1536, 2048
C_NQ = 2560
C_KC, C_VC, C_KS, C_VS = 3072, 3200, 3328, 3456
C_KW, C_VW, C_SMALL = 3584, 3712, 3840
C_NZ = 4096
EV_COLS = 4608
SM_I, SM_F, SM_G = 0, 4, 8


def _cparams(*sem):
    return pltpu.CompilerParams(dimension_semantics=sem, vmem_limit_bytes=VMEM_LIMIT_BYTES)


def _dot(a, b):
    return jnp.dot(a, b, preferred_element_type=F32)


def _dot_nt(a, b):
    return lax.dot_general(a, b, (((1,), (1,)), ((), ())), preferred_element_type=F32)


def _dot_tn(a, b):
    return lax.dot_general(a, b, (((0,), (0,)), ((), ())), preferred_element_type=F32)


def _sigmoid(x):
    e = jnp.exp(-jnp.abs(x))
    r = 1.0 / (1.0 + e)
    return jnp.where(x >= 0, r, e * r)


def _silu(x):
    return x * _sigmoid(x)


def _split3(x):
    hi = x.astype(BF16)
    r = x - hi.astype(F32)
    mid = r.astype(BF16)
    lo = (r - mid.astype(F32)).astype(BF16)
    return hi, mid, lo


def _dot01(m01, x):
    hi, mid, lo = _split3(x)
    return _dot(m01, hi) + _dot(m01, mid) + _dot(m01, lo)


def _norm_proj_kernel(x_ref, nw_ref, w_ref, b_ref, o_ref, h_ref):
    @pl.when(pl.program_id(1) == 0)
    def _():
        x = x_ref[...]
        ms = jnp.mean(x * x, axis=-1, keepdims=True)
        h_ref[...] = (x * lax.rsqrt(ms + EPS) * nw_ref[...]).astype(BF16)

    o_ref[...] = (_dot(h_ref[...], w_ref[...]) + b_ref[...]).astype(o_ref.dtype)


def _norm_proj(x2d, norm_w, w_bf16, bias, tm=1024, tn=512):
    t, d = x2d.shape
    n = w_bf16.shape[1]
    return pl.pallas_call(
        _norm_proj_kernel,
        grid=(t // tm, n // tn),
        in_specs=[
            pl.BlockSpec((tm, d), lambda i, j: (i, 0)),
            pl.BlockSpec((1, d), lambda i, j: (0, 0)),
            pl.BlockSpec((d, tn), lambda i, j: (0, j)),
            pl.BlockSpec((1, tn), lambda i, j: (0, j)),
        ],
        out_specs=pl.BlockSpec((tm, tn), lambda i, j: (i, j)),
        out_shape=jax.ShapeDtypeStruct((t, n), F32),
        scratch_shapes=[pltpu.VMEM((tm, d), BF16)],
        compiler_params=_cparams("parallel", "arbitrary"),
        name="norm_proj",
    )(x2d, norm_w.reshape(1, d), w_bf16, bias.reshape(1, n))


def _out_proj2_kernel(ya_ref, yb_ref, wa_ref, wb_ref, x_ref, o_ref):
    o_ref[...] = x_ref[...] + _dot(ya_ref[...], wa_ref[...]) + _dot(yb_ref[...], wb_ref[...])


def _out_proj2(ya, yb, wa, wb, x2d, tm=1024):
    t, d = x2d.shape
    ka, kb = ya.shape[1], yb.shape[1]
    return pl.pallas_call(
        _out_proj2_kernel,
        grid=(t // tm,),
        in_specs=[
            pl.BlockSpec((tm, ka), lambda i: (i, 0)),
            pl.BlockSpec((tm, kb), lambda i: (i, 0)),
            pl.BlockSpec((ka, d), lambda i: (0, 0)),
            pl.BlockSpec((kb, d), lambda i: (0, 0)),
            pl.BlockSpec((tm, d), lambda i: (i, 0)),
        ],
        out_specs=pl.BlockSpec((tm, d), lambda i: (i, 0)),
        out_shape=jax.ShapeDtypeStruct((t, d), F32),
        compiler_params=_cparams("parallel"),
        name="out_proj2",
    )(ya, yb, wa, wb, x2d)


def _out_proj1_kernel(y_ref, w_ref, x_ref, o_ref):
    o_ref[...] = x_ref[...] + _dot(y_ref[...], w_ref[...])


def _out_proj1(y, w, x2d, tm=1024):
    t, d = x2d.shape
    k = y.shape[1]
    return pl.pallas_call(
        _out_proj1_kernel,
        grid=(t // tm,),
        in_specs=[
            pl.BlockSpec((tm, k), lambda i: (i, 0)),
            pl.BlockSpec((k, d), lambda i: (0, 0)),
            pl.BlockSpec((tm, d), lambda i: (i, 0)),
        ],
        out_specs=pl.BlockSpec((tm, d), lambda i: (i, 0)),
        out_shape=jax.ShapeDtypeStruct((t, d), F32),
        compiler_params=_cparams("parallel"),
        name="out_proj1",
    )(y, w, x2d)


def _mlstm_kernel(q_ref, k_ref, v_ref, o_ref, z_ref, sm_ref, cw_ref, cb_ref, fb_ref, hn_ref,
                  out_ref, xbuf, c_ref, n_ref, m_ref):
    L, D = M_CHUNK, M_DIM

    @pl.when(pl.program_id(1) == 0)
    def _():
        xbuf[0:8, :] = jnp.zeros((8, 2 * M_W), F32)
        c_ref[...] = jnp.zeros_like(c_ref)
        n_ref[...] = jnp.zeros_like(n_ref)
        m_ref[...] = jnp.zeros_like(m_ref)

    xbuf[8:8 + L, 0:M_W] = q_ref[...]
    xbuf[8:8 + L, M_W:2 * M_W] = k_ref[...]
    acc = jnp.broadcast_to(cb_ref[...], (L, 2 * M_W))
    for j in range(CONV_K):
        acc = acc + cw_ref[j:j + 1, :] * xbuf[8 - (CONV_K - 1) + j:8 - (CONV_K - 1) + j + L, :]
    qk = _silu(acc)
    xbuf[0:8, :] = xbuf[L:L + 8, :]

    sm = sm_ref[...]
    row = lax.broadcasted_iota(jnp.int32, (L, L), 0)
    col = lax.broadcasted_iota(jnp.int32, (L, L), 1)
    eye = (row == col).astype(F32)
    tri = (col <= row).astype(F32)
    tri_t = (row <= col).astype(F32)
    causal = col <= row

    for h in range(M_HEADS):
        hs = slice(h * D, (h + 1) * D)
        q = qk[:, hs]
        k = qk[:, M_W + h * D:M_W + (h + 1) * D] * (D ** -0.5)
        v = v_ref[:, hs]
        i_col = sm[:, SM_I + h:SM_I + h + 1]
        fpre = sm[:, SM_F + h:SM_F + h + 1] + fb_ref[:, SM_F + h:SM_F + h + 1]
        f_col = jnp.minimum(fpre, 0.0) - jnp.log(1.0 + jnp.exp(-jnp.abs(fpre)))
        f_row = jnp.sum(eye * f_col, axis=0, keepdims=True)
        i_row = jnp.sum(eye * i_col, axis=0, keepdims=True)
        bcum_col = jnp.sum(tri * f_row, axis=1, keepdims=True)
        bcum_row = jnp.sum(tri_t * f_col, axis=0, keepdims=True)
        m_prev = m_ref[h]
        dmat = jnp.where(causal, bcum_col - bcum_row + i_row, -jnp.inf)
        inter = bcum_col + m_prev
        m_row = jnp.maximum(jnp.max(dmat, axis=1, keepdims=True), inter)
        qb = q.astype(BF16)
        kb = k.astype(BF16)
        vb = v.astype(BF16)
        smat = _dot_nt(qb, kb) * jnp.exp(dmat - m_row)
        w_inter = jnp.exp(inter - m_row)
        num = _dot(smat.astype(BF16), vb) + w_inter * _dot(qb, c_ref[h].astype(BF16))
        den = jnp.sum(smat, axis=1, keepdims=True) + w_inter * jnp.sum(q * n_ref[h], axis=1, keepdims=True)
        hh = num / jnp.maximum(jnp.abs(den), jnp.exp(-m_row))
        b_last = bcum_col[L - 1:L, :]
        dl = b_last - bcum_col + i_col
        m_new = jnp.maximum(b_last + m_prev, jnp.max(dl, axis=0, keepdims=True))
        wk = jnp.exp(dl - m_new)
        decay = jnp.exp(b_last + m_prev - m_new)
        wkk = wk * k
        c_ref[h] = decay * c_ref[h] + _dot_tn(wkk.astype(BF16), vb)
        n_ref[h] = decay * n_ref[h] + jnp.sum(wkk, axis=0, keepdims=True)
        m_ref[h] = m_new
        hn = hh * lax.rsqrt(jnp.mean(hh * hh, axis=1, keepdims=True) + EPS) * hn_ref[:, hs]
        out_ref[:, hs] = (hn * _sigmoid(o_ref[:, hs]) * _silu(z_ref[:, hs])).astype(out_ref.dtype)


def _mlstm(proj, conv_w, conv_b, f_bias_row, head_norm, batch, seq):
    L = M_CHUNK
    nc = seq // L
    cb = lambda off: pl.BlockSpec((L, M_W), lambda b, c, off=off: (b * nc + c, off // M_W))
    return pl.pallas_call(
        _mlstm_kernel,
        grid=(batch, nc),
        in_specs=[
            cb(C_MQ), cb(C_MK), cb(C_MV), cb(C_MO), cb(C_MZ),
            pl.BlockSpec((L, LANES), lambda b, c: (b * nc + c, C_SMALL // LANES)),
            pl.BlockSpec((CONV_K, 2 * M_W), lambda b, c: (0, 0)),
            pl.BlockSpec((1, 2 * M_W), lambda b, c: (0, 0)),
            pl.BlockSpec((1, LANES), lambda b, c: (0, 0)),
            pl.BlockSpec((1, M_W), lambda b, c: (0, 0)),
        ],
        out_specs=pl.BlockSpec((L, M_W), lambda b, c: (b * nc + c, 0)),
        out_shape=jax.ShapeDtypeStruct((batch * seq, M_W), BF16),
        scratch_shapes=[
            pltpu.VMEM((L + 8, 2 * M_W), F32),
            pltpu.VMEM((M_HEADS, M_DIM, M_DIM), F32),
            pltpu.VMEM((M_HEADS, 1, M_DIM), F32),
            pltpu.VMEM((M_HEADS, 1, 1), F32),
        ],
        compiler_params=_cparams("parallel", "arbitrary"),
        name="mlstm",
    )(proj, proj, proj, proj, proj, proj, conv_w, conv_b.reshape(1, -1), f_bias_row, head_norm.reshape(1, -1))


def _group_rms(x, w_row):
    lane = lax.broadcasted_iota(jnp.int32, x.shape, 1)
    left = lane < N_DIM
    sq = x * x
    tot = jnp.sum(sq, axis=1, keepdims=True)
    lft = jnp.sum(jnp.where(left, sq, 0.0), axis=1, keepdims=True)
    ms = jnp.where(left, lft, tot - lft) * (1.0 / N_DIM)
    return x * lax.rsqrt(ms + EPS) * w_row


def _rope(x, cs, s1, s2):
    return x * cs + pltpu.roll(x, ROPE_HALF, 1) * s1 + pltpu.roll(x, LANES - ROPE_HALF, 1) * s2


def _dup_halves(x):
    lane = lax.broadcasted_iota(jnp.int32, x.shape, 1)
    sw = pltpu.roll(x, N_DIM, 1)
    left = lane < N_DIM
    return jnp.where(left, x, sw), jnp.where(left, sw, x)


def _nsa_prep_kernel(q_ref, kv_ref, kw_ref, cs_ref, s1_ref, s2_ref, qn_ref, kn_ref,
                     qt_ref, ka_ref, kwd_ref, vst_ref, vwt_ref, kc_ref, vc_ref, gtt_ref,
                     *, tp, seq):
    cs, s1, s2 = cs_ref[...], s1_ref[...], s2_ref[...]
    for j in range(N_W // LANES):
        sl = slice(j * LANES, (j + 1) * LANES)
        qn = _group_rms(q_ref[:, sl], qn_ref[...])
        qt_ref[sl, :] = (_rope(qn, cs, s1, s2) * (N_DIM ** -0.5 * LOG2E)).T.astype(BF16)
    kv = kv_ref[...]
    kc_ref[...] = kv[:, 0:LANES]
    vc_ref[...] = kv[:, LANES:2 * LANES]
    ks = _rope(_group_rms(kv[:, 2 * LANES:3 * LANES], kn_ref[1:2, :]), cs, s1, s2)
    k0, k1 = _dup_halves(ks)
    pos0 = (pl.program_id(0) * tp) % seq
    pos = pos0 + lax.broadcasted_iota(jnp.int32, (tp, LANES), 0)
    blk = lax.broadcasted_iota(jnp.int32, (tp, LANES), 1)
    onehot = jnp.where(jnp.right_shift(pos, SEL_SHIFT) == blk, 1.0, 0.0).astype(BF16)
    ka_ref[:, 0:LANES] = k0.astype(BF16)
    ka_ref[:, LANES:2 * LANES] = onehot
    ka_ref[:, 2 * LANES:3 * LANES] = k1.astype(BF16)
    ka_ref[:, 3 * LANES:4 * LANES] = onehot
    vst = kv[:, 3 * LANES:4 * LANES].T
    fill = (lax.broadcasted_iota(jnp.int32, (V_ROWS - N_DIM, tp), 0) == 0).astype(BF16)
    for g in range(N_GROUPS):
        vst_ref[0, g * V_ROWS:g * V_ROWS + N_DIM, :] = vst[g * N_DIM:(g + 1) * N_DIM].astype(BF16)
        vst_ref[0, g * V_ROWS + N_DIM:(g + 1) * V_ROWS, :] = fill
    kwb = kw_ref[...]
    kw = _rope(_group_rms(kwb[:, 0:LANES], kn_ref[2:3, :]), cs, s1, s2)
    w0, w1 = _dup_halves(kw)
    kwd_ref[:, 0:LANES] = w0.astype(BF16)
    kwd_ref[:, LANES:2 * LANES] = w1.astype(BF16)
    vwt_ref[...] = kwb[:, LANES:2 * LANES].T.astype(BF16)
    gtt_ref[...] = _sigmoid(kwb[:, 2 * LANES:3 * LANES]).T


def _nsa_prep(proj, cs, s1, s2, qn_row, kn_rows, batch, seq):
    tp = KV_TILE
    t = batch * seq
    nsb = seq // tp
    rb = lambda w, off: pl.BlockSpec((tp, w), lambda i, off=off, w=w: (i, off // w))
    tab = pl.BlockSpec((tp, LANES), lambda i: (i % nsb, 0))
    o = lambda w: pl.BlockSpec((tp, w), lambda i: (i, 0))
    ot = lambda r: pl.BlockSpec((r, tp), lambda i: (0, i))
    sds = lambda w, dt: jax.ShapeDtypeStruct((t, w), dt)
    return pl.pallas_call(
        functools.partial(_nsa_prep_kernel, tp=tp, seq=seq),
        grid=(t // tp,),
        in_specs=[rb(512, C_NQ), rb(512, C_KC), rb(512, C_KW), tab, tab, tab,
                  pl.BlockSpec((1, LANES), lambda i: (0, 0)), pl.BlockSpec((3, LANES), lambda i: (0, 0))],
        out_specs=[ot(N_W), o(512), o(256), pl.BlockSpec((1, N_GROUPS * V_ROWS, tp), lambda i: (i, 0, 0)),
                   ot(LANES), o(128), o(128), ot(LANES)],
        out_shape=[jax.ShapeDtypeStruct((N_W, t), BF16), sds(512, BF16), sds(256, BF16),
                   jax.ShapeDtypeStruct((t // tp, N_GROUPS * V_ROWS, tp), BF16),
                   jax.ShapeDtypeStruct((LANES, t), BF16),
                   sds(128, F32), sds(128, F32), jax.ShapeDtypeStruct((LANES, t), F32)],
        compiler_params=_cparams("parallel"),
        name="nsa_prep",
    )(proj, proj, proj, cs, s1, s2, qn_row, kn_rows)


def _gelu_tanh(x):
    return 0.5 * x * (1.0 + jnp.tanh(0.7978845608028654 * (x + 0.044715 * x * x * x)))


def _compress_mlp(x_ref, pa_ref, pb_ref, wa_ref, wb_ref, w2_ref, hb_ref):
    n = x_ref.shape[0]
    x = x_ref[...]
    ha = _dot((x + pa_ref[...]).astype(BF16), wa_ref[...])
    hb_ref[0:n, :] = _dot((x + pb_ref[...]).astype(BF16), wb_ref[...])
    hb_ref[n:n + 8, :] = jnp.zeros((8, hb_ref.shape[1]), F32)
    hid = ha + hb_ref[1:n + 1, :]
    return _dot(_gelu_tanh(hid).astype(BF16), w2_ref[...])


def _nsa_compress_kernel(kx_ref, vx_ref, kpa_ref, kpb_ref, vpa_ref, vpb_ref, kwa_ref, kwb_ref, kw2_ref,
                         vwa_ref, vwb_ref, vw2_ref, cs_ref, s1_ref, s2_ref, kn_ref,
                         kcd_ref, vct_ref, hb_ref):
    n = kx_ref.shape[0]
    rowi = lax.broadcasted_iota(jnp.int32, (n, LANES), 0)
    live = rowi < n - 1
    kc = _compress_mlp(kx_ref, kpa_ref, kpb_ref, kwa_ref, kwb_ref, kw2_ref, hb_ref)
    kc = _rope(_group_rms(kc, kn_ref[0:1, :]), cs_ref[...], s1_ref[...], s2_ref[...])
    kc = jnp.where(live, kc, 0.0)
    c0, c1 = _dup_halves(kc)
    kcd_ref[:, 0:LANES] = c0.astype(BF16)
    kcd_ref[:, LANES:2 * LANES] = c1.astype(BF16)
    vc = _compress_mlp(vx_ref, vpa_ref, vpb_ref, vwa_ref, vwb_ref, vw2_ref, hb_ref)
    vc = jnp.where(live, vc, 0.0)
    vct_ref[0] = vc.T.astype(BF16)


def _nsa_compress(kx, vx, kp, vp, cs, s1, s2, kn_rows, batch, nrow):
    wide = CMP_STRIDE * LANES
    full = lambda a: pl.BlockSpec(a.shape, lambda b: (0,) * a.ndim)
    xs = pl.BlockSpec((nrow, wide), lambda b: (b, 0))
    args = (kx, vx, kp[0], kp[1], vp[0], vp[1], kp[2], kp[3], kp[4], vp[2], vp[3], vp[4], cs, s1, s2, kn_rows)
    return pl.pallas_call(
        _nsa_compress_kernel,
        grid=(batch,),
        in_specs=[xs, xs] + [full(a) for a in args[2:]],
        out_specs=[pl.BlockSpec((nrow, 2 * LANES), lambda b: (b, 0)),
                   pl.BlockSpec((1, LANES, nrow), lambda b: (b, 0, 0))],
        out_shape=[jax.ShapeDtypeStruct((batch * nrow, 2 * LANES), BF16),
                   jax.ShapeDtypeStruct((batch, LANES, nrow), BF16)],
        scratch_shapes=[pltpu.VMEM((nrow + 8, 2 * CMP_HIDDEN), F32)],
        compiler_params=_cparams("parallel"),
        name="nsa_compress",
    )(*args)


def _compress_params(pos, w1, w2):
    def expand(w1_half, pos_half):
        w = w1_half.reshape(CMP_STRIDE, N_DIM, CMP_HIDDEN)
        z = jnp.zeros_like(w)
        g0 = jnp.concatenate([w, z], axis=-1)
        g1 = jnp.concatenate([z, w], axis=-1)
        wx = jnp.stack([g0, g1], axis=1).reshape(CMP_STRIDE * LANES, 2 * CMP_HIDDEN)
        px = jnp.concatenate([pos_half, pos_half], axis=-1).reshape(1, CMP_STRIDE * LANES)
        return wx.astype(BF16), px.astype(F32)

    half = CMP_STRIDE * N_DIM
    wa, pa = expand(w1[:half], pos[:CMP_STRIDE])
    wb, pb = expand(w1[half:], pos[CMP_STRIDE:])
    z2 = jnp.zeros_like(w2)
    w2x = jnp.concatenate([jnp.concatenate([w2, z2], axis=1), jnp.concatenate([z2, w2], axis=1)], axis=0)
    return pa, pb, wa, wb, w2x.astype(BF16)


def _nsa_attn_kernel(qt_ref, gtt_ref, z_ref, ka_ref, vst_ref,
                     kw0, kw1, kw2, kw3, kw4, wt0, wt1, wt2, wt3, wt4,
                     kcd_ref, vct_ref, ovt_ref, out_ref, qa_ref, s_ref, p_ref, *, ncmp):
    QB = Q_BLOCK
    NP = N_HEADS // 2
    PW = 2 * QB
    i = pl.program_id(1)
    t0 = i * QB
    kw_refs = (kw0, kw1, kw2, kw3, kw4)
    vwt_refs = (wt0, wt1, wt2, wt3, wt4)
    nwt = len(kw_refs)

    sub = lax.broadcasted_iota(jnp.int32, (QB, QB), 0)
    lane = lax.broadcasted_iota(jnp.int32, (QB, QB), 1)
    top = sub < N_DIM
    t_row = t0 + lax.broadcasted_iota(jnp.int32, (1, QB), 1)
    t_row2 = jnp.concatenate([t_row, t_row], axis=1)

    def pair_q(pr):
        slab = qt_ref[pr * LANES:(pr + 1) * LANES, :]
        zero = jnp.zeros_like(slab)
        return jnp.concatenate([jnp.where(top, slab, zero), jnp.where(top, zero, slab)], axis=1)

    nidx = lax.broadcasted_iota(jnp.int32, (ncmp, PW), 0)
    cmp_ok = (nidx * CMP_STRIDE + (CMP_LEN - 1) <= t_row2) & (nidx < ncmp - 1)
    blk_f = sub.astype(F32)
    cur = jnp.right_shift(t_row, SEL_SHIFT)
    o_cmp = [None] * NP
    for g in range(N_GROUPS):
        kc = kcd_ref[:, g * LANES:(g + 1) * LANES]
        vct = vct_ref[0, g * N_DIM:(g + 1) * N_DIM, :]
        psum = jnp.zeros((ncmp, QB), F32)
        for pp in range(N_HPG // 2):
            pr = g * (N_HPG // 2) + pp
            qp = pair_q(pr)
            qa_ref[pr, 0:LANES, :] = qp
            s = jnp.where(cmp_ok, _dot(kc, qp), NEG)
            mx = jnp.max(s, axis=0, keepdims=True)
            p = jnp.where(cmp_ok, jnp.exp2(s - mx), 0.0)
            den = jnp.sum(p, axis=0, keepdims=True)
            p = p * (1.0 / jnp.maximum(den, 1e-30))
            o_cmp[pr] = _dot(vct, p.astype(BF16))
            psum = psum + p[:, 0:QB] + p[:, QB:PW]
        p_hi = psum.astype(BF16)
        p_lo = (psum - p_hi.astype(F32)).astype(BF16)
        imp = _dot(ovt_ref[...], p_hi) + _dot(ovt_ref[...], p_lo)
        forced = (sub == 0) | (sub == cur) | (sub == cur - 1)
        work = jnp.where(sub > cur, NEG, jnp.where(forced, -NEG, imp))
        sel = jnp.zeros((QB, QB), jnp.bool_)
        for _ in range(SEL_TOPK):
            mx = jnp.max(work, axis=0, keepdims=True)
            idx = jnp.min(jnp.where(work == mx, blk_f, float(LANES)), axis=0, keepdims=True)
            pick = blk_f == idx
            sel = sel | pick
            work = jnp.where(pick, -jnp.inf, work)
        sel = sel & (sub <= cur)
        neg = jnp.where(sel, 0.0, NEG).astype(BF16)
        neg2 = jnp.concatenate([neg, neg], axis=1)
        for pp in range(N_HPG // 2):
            qa_ref[g * (N_HPG // 2) + pp, LANES:2 * LANES, :] = neg2

    CH = QB
    kidx = lax.broadcasted_iota(jnp.int32, (CH, PW), 0)

    def sel_step(jt, carry, masked):
        k0 = pl.multiple_of(jt * KV_TILE, KV_TILE)
        new = []
        for pr in range(NP):
            g = pr // (N_HPG // 2)
            m, acc = carry[pr]
            cmax = None
            for c in range(KV_TILE // CH):
                rs = slice(c * CH, (c + 1) * CH)
                kt = ka_ref[pl.ds(k0 + c * CH, CH), 2 * g * LANES:2 * (g + 1) * LANES]
                s = _dot(kt, qa_ref[pr])
                if masked:
                    s = jnp.where(k0 + c * CH + kidx <= t_row2, s, NEG)
                s_ref[pr, rs, :] = s
                cm = jnp.max(s, axis=0, keepdims=True)
                cmax = cm if cmax is None else jnp.maximum(cmax, cm)
            m_new = jnp.maximum(m, cmax)
            alpha = jnp.exp2(m - m_new)
            for c in range(KV_TILE // CH):
                rs = slice(c * CH, (c + 1) * CH)
                p_ref[pr, rs, :] = jnp.exp2(s_ref[pr, rs, :] - m_new).astype(BF16)
            acc = alpha * acc + _dot(vst_ref[jt, g * V_ROWS:(g + 1) * V_ROWS, :], p_ref[pr])
            new.append((m_new, acc))
        return tuple(new)

    init = tuple((jnp.full((1, PW), 2.0 * NEG, F32), jnp.zeros((V_ROWS, PW), F32)) for _ in range(NP))
    n_full = (t0 + 1) // KV_TILE
    carry = lax.fori_loop(0, n_full, functools.partial(sel_step, masked=False), init)
    carry = sel_step(n_full, carry, masked=True)
    o_sel = [acc[0:N_DIM] * (1.0 / acc[N_DIM:N_DIM + 1]) for (_, acc) in carry]

    o_win = [None] * NP
    above = jnp.concatenate([sub > lane, sub > lane], axis=1)
    for pr in range(NP):
        g = pr // (N_HPG // 2)
        qp = qa_ref[pr, 0:LANES, :]
        ss = []
        for c in range(nwt):
            s = _dot(kw_refs[c][:, g * LANES:(g + 1) * LANES], qp)
            ok = (i - (nwt - 1) + c) >= 0
            if c == 0:
                ok = ok & above
            elif c == nwt - 1:
                ok = ok & jnp.logical_not(above)
            ss.append(jnp.where(ok, s, NEG))
        mx = ss[0].max(axis=0, keepdims=True)
        for c in range(1, nwt):
            mx = jnp.maximum(mx, ss[c].max(axis=0, keepdims=True))
        den = jnp.zeros((1, PW), F32)
        acc = jnp.zeros((N_DIM, PW), F32)
        for c in range(nwt):
            p = jnp.exp2(ss[c] - mx)
            den = den + jnp.sum(p, axis=0, keepdims=True)
            acc = acc + _dot(vwt_refs[c][g * N_DIM:(g + 1) * N_DIM, :], p.astype(BF16))
        o_win[pr] = acc * (1.0 / den)

    for pr in range(NP):
        halves = []
        for par in range(2):
            h = 2 * pr + par
            cs = slice(par * QB, (par + 1) * QB)
            gc = gtt_ref[SM_G + 3 * h:SM_G + 3 * h + 1, :]
            gs = gtt_ref[SM_G + 3 * h + 1:SM_G + 3 * h + 2, :]
            gw = gtt_ref[SM_G + 3 * h + 2:SM_G + 3 * h + 3, :]
            halves.append(gc * o_cmp[pr][:, cs] + gs * o_sel[pr][:, cs] + gw * o_win[pr][:, cs])
        y = jnp.concatenate(halves, axis=0).T
        sl = slice(pr * LANES, (pr + 1) * LANES)
        out_ref[:, sl] = (y * _silu(z_ref[:, sl])).astype(out_ref.dtype)


def _nsa_attn(proj, qt, gtt, ka, vst, kwd, vwt, kcd, vct, ovt, batch, seq):
    QB = Q_BLOCK
    nqb = seq // QB
    ncmp = seq // CMP_STRIDE
    ntile = seq // KV_TILE
    nwt = WINDOW // QB + 1
    row = lambda w: pl.BlockSpec((QB, w), lambda b, i: (b * nqb + i, 0))
    colb = lambda r: pl.BlockSpec((r, QB), lambda b, i: (0, b * nqb + i))
    wblk = lambda b, i, c: b * nqb + jnp.maximum(i - (nwt - 1) + c, 0)
    in_specs = ([colb(N_W), colb(LANES), pl.BlockSpec((QB, N_W), lambda b, i: (b * nqb + i, C_NZ // N_W)),
                 pl.BlockSpec((seq, 4 * LANES), lambda b, i: (b, 0)),
                 pl.BlockSpec((ntile, N_GROUPS * V_ROWS, KV_TILE), lambda b, i: (b, 0, 0))]
                + [pl.BlockSpec((QB, 2 * LANES), lambda b, i, c=c: (wblk(b, i, c), 0)) for c in range(nwt)]
                + [pl.BlockSpec((LANES, QB), lambda b, i, c=c: (0, wblk(b, i, c))) for c in range(nwt)]
                + [pl.BlockSpec((ncmp, 2 * LANES), lambda b, i: (b, 0)),
                   pl.BlockSpec((1, LANES, ncmp), lambda b, i: (b, 0, 0)),
                   pl.BlockSpec((LANES, ncmp), lambda b, i: (0, 0))])
    return pl.pallas_call(
        functools.partial(_nsa_attn_kernel, ncmp=ncmp),
        grid=(batch, nqb),
        in_specs=in_specs,
        out_specs=row(N_W),
        out_shape=jax.ShapeDtypeStruct((batch * seq, N_W), BF16),
        scratch_shapes=[pltpu.VMEM((N_HEADS // 2, 2 * LANES, 2 * QB), BF16),
                        pltpu.VMEM((N_HEADS // 2, KV_TILE, 2 * QB), F32),
                        pltpu.VMEM((N_HEADS // 2, KV_TILE, 2 * QB), BF16)],
        compiler_params=_cparams("parallel", "arbitrary"),
        name="nsa_attn",
    )(qt, gtt, proj, ka, vst, *([kwd] * nwt), *([vwt] * nwt), kcd, vct, ovt)


def _hgrn_kernel(f_ref, i_ref, q_ref, z_ref, lb_ref, hn_ref, out_ref, kpad, bpad, vpad, st_ref):
    T, D, BL = H_TILE, H_DIM, H_BLOCK
    nblk = T // BL

    @pl.when(pl.program_id(1) == 0)
    def _():
        st_ref[...] = jnp.zeros_like(st_ref)

    lb = lb_ref[...]
    fp = f_ref[...]
    e = jnp.exp(-jnp.abs(fp))
    r = 1.0 / (1.0 + e)
    sig = jnp.where(fp >= 0, r, e * r)
    nsig = jnp.where(fp >= 0, e * r, r)
    logf = jnp.log(lb + (1.0 - lb) * sig)
    kk = (1.0 - lb) * nsig

    row = lax.broadcasted_iota(jnp.int32, (T, T), 0)
    col = lax.broadcasted_iota(jnp.int32, (T, T), 1)
    same = jnp.right_shift(row, H_BLOCK_SHIFT) == jnp.right_shift(col, H_BLOCK_SHIFT)
    tri01 = jnp.where(same & (col <= row), 1.0, 0.0).astype(BF16)
    blk01 = jnp.where(same, 1.0, 0.0).astype(BF16)
    hi, mid, lo = _split3(logf)
    bc = _dot(tri01, hi) + _dot(tri01, mid) + _dot(tri01, lo)
    bend = _dot(blk01, hi) + _dot(blk01, mid) + _dot(blk01, lo)

    zpad = jnp.zeros((BL, H_W), F32)
    kpad[0:BL, :] = zpad
    bpad[0:BL, :] = zpad
    vpad[0:BL, :] = zpad
    kpad[BL:BL + T, :] = kk
    bpad[BL:BL + T, :] = bc
    vpad[BL:BL + T, :] = i_ref[...]

    rmod = jnp.bitwise_and(lax.broadcasted_iota(jnp.int32, (T, 1), 0), BL - 1)
    for h in range(H_HEADS):
        hs = slice(h * D, (h + 1) * D)
        q = _silu(q_ref[:, hs])
        b = bc[:, hs]
        v = vpad[BL:BL + T, hs]
        o = jnp.zeros((T, D), F32)
        for d in range(BL):
            ks = kpad[BL - d:BL - d + T, hs]
            bs = bpad[BL - d:BL - d + T, hs]
            vs = vpad[BL - d:BL - d + T, hs]
            a = jnp.sum(q * ks * jnp.exp(b - bs), axis=1, keepdims=True)
            o = o + jnp.where(rmod >= d, a, 0.0) * vs
        qt = (q * jnp.exp(b)).astype(BF16)
        kt = (kk[:, hs] * jnp.exp(bend[:, hs] - b)).astype(BF16)
        dec = jnp.exp(bend[:, hs])
        vb = v.astype(BF16)
        st = st_ref[h]
        outs = []
        for n in range(nblk):
            rs = slice(n * BL, (n + 1) * BL)
            outs.append(o[rs] + _dot_nt(qt[rs], st.astype(BF16)))
            st = st * dec[n * BL:n * BL + 1, :] + _dot_tn(vb[rs], kt[rs])
        st_ref[h] = st
        oh = jnp.concatenate(outs, axis=0)
        on = oh * lax.rsqrt(jnp.mean(oh * oh, axis=1, keepdims=True) + EPS) * hn_ref[:, hs]
        out_ref[:, hs] = (on * _silu(z_ref[:, hs])).astype(out_ref.dtype)


def _hgrn(proj, lb_row, hn_row, batch, seq):
    T = H_TILE
    nt = seq // T
    cb = lambda j: pl.BlockSpec((T, H_W), lambda b, c, j=j: (b * nt + c, j))
    vec = pl.BlockSpec((1, H_W), lambda b, c: (0, 0))
    return pl.pallas_call(
        _hgrn_kernel,
        grid=(batch, nt),
        in_specs=[cb(0), cb(1), cb(2), cb(3), vec, vec],
        out_specs=pl.BlockSpec((T, H_W), lambda b, c: (b * nt + c, 0)),
        out_shape=jax.ShapeDtypeStruct((batch * seq, H_W), BF16),
        scratch_shapes=[pltpu.VMEM((T + H_BLOCK, H_W), F32)] * 3 + [pltpu.VMEM((H_HEADS, H_DIM, H_DIM), F32)],
        compiler_params=_cparams("parallel", "arbitrary"),
        name="hgrn2",
    )(proj, proj, proj, proj, lb_row, hn_row)


def _even_relayout(a):
    nb = 5 * M_W + 2 * M_HEADS
    g0 = nb + N_W + 6 * N_KV_W
    col = lambda lo, n: lax.slice_in_dim(a, lo, lo + n, axis=-1)
    pad = jnp.zeros(a.shape[:-1] + (C_NZ - C_SMALL - SM_G - 3 * N_HEADS,), a.dtype)
    out = jnp.concatenate([
        col(0, 5 * M_W),
        col(nb, N_W + 6 * N_KV_W),
        col(5 * M_W, 2 * M_HEADS),
        col(g0, 3 * N_HEADS), pad,
        col(g0 + 3 * N_HEADS, N_W),
    ], axis=-1)
    assert out.shape[-1] == EV_COLS
    return out


def _rope_tables(positions):
    inv_freq = ROPE_THETA ** (-np.arange(ROPE_HALF, dtype=np.float64) / ROPE_HALF)
    ang = positions.astype(np.float64)[:, None] * inv_freq[None, :]
    cos, sin = np.cos(ang).astype(np.float32), np.sin(ang).astype(np.float32)
    n = positions.shape[0]
    z8 = np.zeros((n, ROPE_HALF), np.float32)
    rest1 = np.ones((n, N_DIM - ROPE_DIM), np.float32)
    rest0 = np.zeros((n, N_DIM - ROPE_DIM), np.float32)
    cs = np.concatenate([cos, cos, rest1], axis=1)
    s1 = np.concatenate([z8, sin, rest0], axis=1)
    s2 = np.concatenate([-sin, z8, rest0], axis=1)
    two = lambda a: np.concatenate([a, a], axis=1)
    return two(cs), two(s1), two(s2)


def _overlap_matrix(seq):
    ncmp = seq // CMP_STRIDE
    n_sel = seq // SEL_BLOCK
    cs = np.arange(ncmp) * CMP_STRIDE
    ss = np.arange(n_sel) * SEL_BLOCK
    ov = np.clip(np.minimum(cs[:, None] + CMP_LEN, ss[None, :] + SEL_BLOCK)
                 - np.maximum(cs[:, None], ss[None, :]), 0, None).astype(np.float32)
    ov[ncmp - 1, :] = 0.0
    return np.pad(ov, ((0, 0), (0, LANES - n_sel)))


def _even_layer(x2d, norm_w, w_in, b_in, f_bias, conv_w, conv_b, head_norm, q_norm, k_norm,
                ck_pos, ck_w1, ck_w2, cv_pos, cv_w1, cv_w2, w_out, batch, seq):
    assert seq // SEL_BLOCK <= LANES and seq % KV_TILE == 0
    proj = _norm_proj(x2d, norm_w, _even_relayout(w_in).astype(BF16), _even_relayout(b_in))

    f_bias_row = jnp.zeros((1, LANES), F32).at[0, SM_F:SM_F + M_HEADS].set(f_bias)
    ya = _mlstm(proj, conv_w, conv_b, f_bias_row, head_norm, batch, seq)

    cs, s1, s2 = (jnp.asarray(a) for a in _rope_tables(np.arange(seq)))
    qn_row = jnp.concatenate([q_norm, q_norm]).reshape(1, LANES)
    kn_rows = jnp.concatenate([k_norm, k_norm], axis=1)
    qt, ka, kwd, vst, vwt, kc, vc, gtt = _nsa_prep(proj, cs, s1, s2, qn_row, kn_rows, batch, seq)

    ncmp = seq // CMP_STRIDE
    ccs, cs1, cs2 = (jnp.asarray(a) for a in _rope_tables(np.arange(ncmp) * CMP_STRIDE + CMP_LEN - 1))
    kcd, vct = _nsa_compress(
        kc.reshape(batch * ncmp, CMP_STRIDE * LANES), vc.reshape(batch * ncmp, CMP_STRIDE * LANES),
        _compress_params(ck_pos, ck_w1, ck_w2), _compress_params(cv_pos, cv_w1, cv_w2),
        ccs, cs1, cs2, kn_rows, batch, ncmp)

    ovt = jnp.asarray(_overlap_matrix(seq).T).astype(BF16)
    yb = _nsa_attn(proj, qt, gtt, ka, vst, kwd, vwt, kcd, vct, ovt, batch, seq)

    w_o = w_out.astype(BF16)
    return _out_proj2(ya, yb, w_o[:M_W], w_o[M_W:], x2d)


def _odd_layer(x2d, norm_w, w_in, b_in, lb, head_norm, w_out, batch, seq):
    proj = _norm_proj(x2d, norm_w, w_in.astype(BF16), b_in)
    y = _hgrn(proj, lb.reshape(1, H_W), head_norm.reshape(1, H_W), batch, seq)
    return _out_proj1(y, w_out.astype(BF16), x2d)


def kernel(x, norm_w, ev_w_in, ev_b_in, mlstm_f_bias, mlstm_conv_w, mlstm_conv_b, mlstm_head_norm, nsa_q_norm, nsa_k_norm, cmp_k_pos, cmp_k_w1, cmp_k_w2, cmp_v_pos, cmp_v_w1, cmp_v_w2, ev_w_out, od_w_in, od_b_in, hgrn_lb_logits, hgrn_head_norm, od_w_out):
    batch, seq, d = x.shape
    depth = norm_w.shape[0]
    lbs = jnp.cumsum(jax.nn.softmax(hgrn_lb_logits.astype(F32), axis=0), axis=0)
    x2d = x.reshape(batch * seq, d)
    for layer in range(depth):
        j = layer // 2
        if layer % 2 == 0:
            x2d = _even_layer(x2d, norm_w[layer], ev_w_in[j], ev_b_in[j], mlstm_f_bias[j], mlstm_conv_w[j],
                              mlstm_conv_b[j], mlstm_head_norm[j], nsa_q_norm[j], nsa_k_norm[j], cmp_k_pos[j],
                              cmp_k_w1[j], cmp_k_w2[j], cmp_v_pos[j], cmp_v_w1[j], cmp_v_w2[j], ev_w_out[j],
                              batch, seq)
        else:
            x2d = _odd_layer(x2d, norm_w[layer], od_w_in[j], od_b_in[j], lbs[j], hgrn_head_norm[j], od_w_out[j],
                             batch, seq)
    return x2d.reshape(batch, seq, d)
```

```python
import functools

import numpy as np
import jax
import jax.numpy as jnp
from jax import lax
from jax.experimental import pallas as pl
from jax.experimental.pallas import tpu as pltpu

F32 = jnp.float32
BF16 = jnp.bfloat16

LANES = 128
EPS = 1e-6
NEG = -1e30
VMEM_LIMIT_BYTES = 56 * 1024 * 1024

M_HEADS = 4
M_DIM = 128
M_W = M_HEADS * M_DIM
M_CHUNK = 128
CONV_K = 4
N_HEADS = 8
N_GROUPS = 2
N_HPG = N_HEADS // N_GROUPS
N_DIM = 64
N_W = N_HEADS * N_DIM
N_KV_W = N_GROUPS * N_DIM
CMP_LEN = 32
CMP_STRIDE = 16
CMP_HIDDEN = 128
SEL_BLOCK = 64
SEL_SHIFT = 6
SEL_TOPK = 16
WINDOW = 512
Q_BLOCK = 128
ROPE_THETA = 500000.0
ROPE_DIM = N_DIM // 4
ROPE_HALF = ROPE_DIM // 2
KV_TILE = 512
V_ROWS = 80
LOG2E = 1.4426950408889634
H_HEADS = 8
H_DIM = 128
H_W = H_HEADS * H_DIM
H_TILE = 128
H_BLOCK = 16
H_BLOCK_SHIFT = 4

C_MQ, C_MK, C_MV, C_MO, C_MZ = 0, 512, 1024, 1536, 2048
C_NQ = 2560
C_KC, C_VC, C_KS, C_VS = 3072, 3200, 3328, 3456
C_KW, C_VW, C_SMALL = 3584, 3712, 3840
C_NZ = 4096
EV_COLS = 4608
SM_I, SM_F, SM_G = 0, 4, 8


def _cparams(*sem):
    return pltpu.CompilerParams(dimension_semantics=sem, vmem_limit_bytes=VMEM_LIMIT_BYTES)


def _dot(a, b):
    return jnp.dot(a, b, preferred_element_type=F32)


def _dot_nt(a, b):
    return lax.dot_general(a, b, (((1,), (1,)), ((), ())), preferred_element_type=F32)


def _dot_tn(a, b):
    return lax.dot_general(a, b, (((0,), (0,)), ((), ())), preferred_element_type=F32)


def _sigmoid(x):
    e = jnp.exp(-jnp.abs(x))
    r = 1.0 / (1.0 + e)
    return jnp.where(x >= 0, r, e * r)


def _silu(x):
    return x * _sigmoid(x)


def _split3(x):
    hi = x.astype(BF16)
    r = x - hi.astype(F32)
    mid = r.astype(BF16)
    lo = (r - mid.astype(F32)).astype(BF16)
    return hi, mid, lo


def _dot01(m01, x):
    hi, mid, lo = _split3(x)
    return _dot(m01, hi) + _dot(m01, mid) + _dot(m01, lo)


def _norm_proj_kernel(x_ref, nw_ref, w_ref, b_ref, o_ref, h_ref):
    @pl.when(pl.program_id(1) == 0)
    def _():
        x = x_ref[...]
        ms = jnp.mean(x * x, axis=-1, keepdims=True)
        h_ref[...] = (x * lax.rsqrt(ms + EPS) * nw_ref[...]).astype(BF16)

    o_ref[...] = (_dot(h_ref[...], w_ref[...]) + b_ref[...]).astype(o_ref.dtype)


def _norm_proj(x2d, norm_w, w_bf16, bias, tm=1024, tn=512):
    t, d = x2d.shape
    n = w_bf16.shape[1]
    return pl.pallas_call(
        _norm_proj_kernel,
        grid=(t // tm, n // tn),
        in_specs=[
            pl.BlockSpec((tm, d), lambda i, j: (i, 0)),
            pl.BlockSpec((1, d), lambda i, j: (0, 0)),
            pl.BlockSpec((d, tn), lambda i, j: (0, j)),
            pl.BlockSpec((1, tn), lambda i, j: (0, j)),
        ],
        out_specs=pl.BlockSpec((tm, tn), lambda i, j: (i, j)),
        out_shape=jax.ShapeDtypeStruct((t, n), F32),
        scratch_shapes=[pltpu.VMEM((tm, d), BF16)],
        compiler_params=_cparams("parallel", "arbitrary"),
        name="norm_proj",
    )(x2d, norm_w.reshape(1, d), w_bf16, bias.reshape(1, n))


def _out_proj2_kernel(ya_ref, yb_ref, wa_ref, wb_ref, x_ref, o_ref):
    o_ref[...] = x_ref[...] + _dot(ya_ref[...], wa_ref[...]) + _dot(yb_ref[...], wb_ref[...])


def _out_proj2(ya, yb, wa, wb, x2d, tm=1024):
    t, d = x2d.shape
    ka, kb = ya.shape[1], yb.shape[1]
    return pl.pallas_call(
        _out_proj2_kernel,
        grid=(t // tm,),
        in_specs=[
            pl.BlockSpec((tm, ka), lambda i: (i, 0)),
            pl.BlockSpec((tm, kb), lambda i: (i, 0)),
            pl.BlockSpec((ka, d), lambda i: (0, 0)),
            pl.BlockSpec((kb, d), lambda i: (0, 0)),
            pl.BlockSpec((tm, d), lambda i: (i, 0)),
        ],
        out_specs=pl.BlockSpec((tm, d), lambda i: (i, 0)),
        out_shape=jax.ShapeDtypeStruct((t, d), F32),
        compiler_params=_cparams("parallel"),
        name="out_proj2",
    )(ya, yb, wa, wb, x2d)


def _out_proj1_kernel(y_ref, w_ref, x_ref, o_ref):
    o_ref[...] = x_ref[...] + _dot(y_ref[...], w_ref[...])


def _out_proj1(y, w, x2d, tm=1024):
    t, d = x2d.shape
    k = y.shape[1]
    return pl.pallas_call(
        _out_proj1_kernel,
        grid=(t // tm,),
        in_specs=[
            pl.BlockSpec((tm, k), lambda i: (i, 0)),
            pl.BlockSpec((k, d), lambda i: (0, 0)),
            pl.BlockSpec((tm, d), lambda i: (i, 0)),
        ],
        out_specs=pl.BlockSpec((tm, d), lambda i: (i, 0)),
        out_shape=jax.ShapeDtypeStruct((t, d), F32),
        compiler_params=_cparams("parallel"),
        name="out_proj1",
    )(y, w, x2d)


def _mlstm_kernel(q_ref, k_ref, v_ref, o_ref, z_ref, sm_ref, cw_ref, cb_ref, fb_ref, hn_ref,
                  out_ref, xbuf, c_ref, n_ref, m_ref):
    L, D = M_CHUNK, M_DIM

    @pl.when(pl.program_id(1) == 0)
    def _():
        xbuf[0:8, :] = jnp.zeros((8, 2 * M_W), F32)
        c_ref[...] = jnp.zeros_like(c_ref)
        n_ref[...] = jnp.zeros_like(n_ref)
        m_ref[...] = jnp.zeros_like(m_ref)

    xbuf[8:8 + L, 0:M_W] = q_ref[...]
    xbuf[8:8 + L, M_W:2 * M_W] = k_ref[...]
    acc = jnp.broadcast_to(cb_ref[...], (L, 2 * M_W))
    for j in range(CONV_K):
        acc = acc + cw_ref[j:j + 1, :] * xbuf[8 - (CONV_K - 1) + j:8 - (CONV_K - 1) + j + L, :]
    qk = _silu(acc)
    xbuf[0:8, :] = xbuf[L:L + 8, :]

    sm = sm_ref[...]
    row = lax.broadcasted_iota(jnp.int32, (L, L), 0)
    col = lax.broadcasted_iota(jnp.int32, (L, L), 1)
    eye = (row == col).astype(F32)
    tri = (col <= row).astype(F32)
    tri_t = (row <= col).astype(F32)
    causal = col <= row

    for h in range(M_HEADS):
        hs = slice(h * D, (h + 1) * D)
        q = qk[:, hs]
        k = qk[:, M_W + h * D:M_W + (h + 1) * D] * (D ** -0.5)
        v = v_ref[:, hs]
        i_col = sm[:, SM_I + h:SM_I + h + 1]
        fpre = sm[:, SM_F + h:SM_F + h + 1] + fb_ref[:, SM_F + h:SM_F + h + 1]
        f_col = jnp.minimum(fpre, 0.0) - jnp.log(1.0 + jnp.exp(-jnp.abs(fpre)))
        f_row = jnp.sum(eye * f_col, axis=0, keepdims=True)
        i_row = jnp.sum(eye * i_col, axis=0, keepdims=True)
        bcum_col = jnp.sum(tri * f_row, axis=1, keepdims=True)
        bcum_row = jnp.sum(tri_t * f_col, axis=0, keepdims=True)
        m_prev = m_ref[h]
        dmat = jnp.where(causal, bcum_col - bcum_row + i_row, -jnp.inf)
        inter = bcum_col + m_prev
        m_row = jnp.maximum(jnp.max(dmat, axis=1, keepdims=True), inter)
        qb = q.astype(BF16)
        kb = k.astype(BF16)
        vb = v.astype(BF16)
        smat = _dot_nt(qb, kb) * jnp.exp(dmat - m_row)
        w_inter = jnp.exp(inter - m_row)
        num = _dot(smat.astype(BF16), vb) + w_inter * _dot(qb, c_ref[h].astype(BF16))
        den = jnp.sum(smat, axis=1, keepdims=True) + w_inter * jnp.sum(q * n_ref[h], axis=1, keepdims=True)
        hh = num / jnp.maximum(jnp.abs(den), jnp.exp(-m_row))
        b_last = bcum_col[L - 1:L, :]
        dl = b_last - bcum_col + i_col
        m_new = jnp.maximum(b_last + m_prev, jnp.max(dl, axis=0, keepdims=True))
        wk = jnp.exp(dl - m_new)
        decay = jnp.exp(b_last + m_prev - m_new)
        wkk = wk * k
        c_ref[h] = decay * c_ref[h] + _dot_tn(wkk.astype(BF16), vb)
        n_ref[h] = decay * n_ref[h] + jnp.sum(wkk, axis=0, keepdims=True)
        m_ref[h] = m_new
        hn = hh * lax.rsqrt(jnp.mean(hh * hh, axis=1, keepdims=True) + EPS) * hn_ref[:, hs]
        out_ref[:, hs] = (hn * _sigmoid(o_ref[:, hs]) * _silu(z_ref[:, hs])).astype(out_ref.dtype)


def _mlstm(proj, conv_w, conv_b, f_bias_row, head_norm, batch, seq):
    L = M_CHUNK
    nc = seq // L
    cb = lambda off: pl.BlockSpec((L, M_W), lambda b, c, off=off: (b * nc + c, off // M_W))
    return pl.pallas_call(
        _mlstm_kernel,
        grid=(batch, nc),
        in_specs=[
            cb(C_MQ), cb(C_MK), cb(C_MV), cb(C_MO), cb(C_MZ),
            pl.BlockSpec((L, LANES), lambda b, c: (b * nc + c, C_SMALL // LANES)),
            pl.BlockSpec((CONV_K, 2 * M_W), lambda b, c: (0, 0)),
            pl.BlockSpec((1, 2 * M_W), lambda b, c: (0, 0)),
            pl.BlockSpec((1, LANES), lambda b, c: (0, 0)),
            pl.BlockSpec((1, M_W), lambda b, c: (0, 0)),
        ],
        out_specs=pl.BlockSpec((L, M_W), lambda b, c: (b * nc + c, 0)),
        out_shape=jax.ShapeDtypeStruct((batch * seq, M_W), BF16),
        scratch_shapes=[
            pltpu.VMEM((L + 8, 2 * M_W), F32),
            pltpu.VMEM((M_HEADS, M_DIM, M_DIM), F32),
            pltpu.VMEM((M_HEADS, 1, M_DIM), F32),
            pltpu.VMEM((M_HEADS, 1, 1), F32),
        ],
        compiler_params=_cparams("parallel", "arbitrary"),
        name="mlstm",
    )(proj, proj, proj, proj, proj, proj, conv_w, conv_b.reshape(1, -1), f_bias_row, head_norm.reshape(1, -1))


def _group_rms(x, w_row):
    lane = lax.broadcasted_iota(jnp.int32, x.shape, 1)
    left = lane < N_DIM
    sq = x * x
    tot = jnp.sum(sq, axis=1, keepdims=True)
    lft = jnp.sum(jnp.where(left, sq, 0.0), axis=1, keepdims=True)
    ms = jnp.where(left, lft, tot - lft) * (1.0 / N_DIM)
    return x * lax.rsqrt(ms + EPS) * w_row


def _rope(x, cs, s1, s2):
    return x * cs + pltpu.roll(x, ROPE_HALF, 1) * s1 + pltpu.roll(x, LANES - ROPE_HALF, 1) * s2


def _dup_halves(x):
    lane = lax.broadcasted_iota(jnp.int32, x.shape, 1)
    sw = pltpu.roll(x, N_DIM, 1)
    left = lane < N_DIM
    return jnp.where(left, x, sw), jnp.where(left, sw, x)


def _nsa_prep_kernel(q_ref, kv_ref, kw_ref, cs_ref, s1_ref, s2_ref, qn_ref, kn_ref,
                     qt_ref, ka_ref, kwd_ref, vst_ref, vwt_ref, kc_ref, vc_ref, gtt_ref,
                     *, tp, seq):
    cs, s1, s2 = cs_ref[...], s1_ref[...], s2_ref[...]
    for j in range(N_W // LANES):
        sl = slice(j * LANES, (j + 1) * LANES)
        qn = _group_rms(q_ref[:, sl], qn_ref[...])
        qt_ref[sl, :] = (_rope(qn, cs, s1, s2) * (N_DIM ** -0.5 * LOG2E)).T.astype(BF16)
    kv = kv_ref[...]
    kc_ref[...] = kv[:, 0:LANES]
    vc_ref[...] = kv[:, LANES:2 * LANES]
    ks = _rope(_group_rms(kv[:, 2 * LANES:3 * LANES], kn_ref[1:2, :]), cs, s1, s2)
    k0, k1 = _dup_halves(ks)
    pos0 = (pl.program_id(0) * tp) % seq
    pos = pos0 + lax.broadcasted_iota(jnp.int32, (tp, LANES), 0)
    blk = lax.broadcasted_iota(jnp.int32, (tp, LANES), 1)
    onehot = jnp.where(jnp.right_shift(pos, SEL_SHIFT) == blk, 1.0, 0.0).astype(BF16)
    ka_ref[:, 0:LANES] = k0.astype(BF16)
    ka_ref[:, LANES:2 * LANES] = onehot
    ka_ref[:, 2 * LANES:3 * LANES] = k1.astype(BF16)
    ka_ref[:, 3 * LANES:4 * LANES] = onehot
    vst = kv[:, 3 * LANES:4 * LANES].T
    fill = (lax.broadcasted_iota(jnp.int32, (V_ROWS - N_DIM, Q_BLOCK), 0) == 0).astype(BF16)
    for c in range(tp // Q_BLOCK):
        for g in range(N_GROUPS):
            vst_ref[c, g * V_ROWS:g * V_ROWS + N_DIM, :] = (
                vst[g * N_DIM:(g + 1) * N_DIM, c * Q_BLOCK:(c + 1) * Q_BLOCK].astype(BF16))
            vst_ref[c, g * V_ROWS + N_DIM:(g + 1) * V_ROWS, :] = fill
    kwb = kw_ref[...]
    kw = _rope(_group_rms(kwb[:, 0:LANES], kn_ref[2:3, :]), cs, s1, s2)
    w0, w1 = _dup_halves(kw)
    kwd_ref[:, 0:LANES] = w0.astype(BF16)
    kwd_ref[:, LANES:2 * LANES] = w1.astype(BF16)
    vwt_ref[...] = kwb[:, LANES:2 * LANES].T.astype(BF16)
    gtt_ref[...] = _sigmoid(kwb[:, 2 * LANES:3 * LANES]).T


def _nsa_prep(proj, cs, s1, s2, qn_row, kn_rows, batch, seq):
    tp = KV_TILE
    t = batch * seq
    nsb = seq // tp
    rb = lambda w, off: pl.BlockSpec((tp, w), lambda i, off=off, w=w: (i, off // w))
    tab = pl.BlockSpec((tp, LANES), lambda i: (i % nsb, 0))
    o = lambda w: pl.BlockSpec((tp, w), lambda i: (i, 0))
    ot = lambda r: pl.BlockSpec((r, tp), lambda i: (0, i))
    sds = lambda w, dt: jax.ShapeDtypeStruct((t, w), dt)
    return pl.pallas_call(
        functools.partial(_nsa_prep_kernel, tp=tp, seq=seq),
        grid=(t // tp,),
        in_specs=[rb(512, C_NQ), rb(512, C_KC), rb(512, C_KW), tab, tab, tab,
                  pl.BlockSpec((1, LANES), lambda i: (0, 0)), pl.BlockSpec((3, LANES), lambda i: (0, 0))],
        out_specs=[ot(N_W), o(512), o(256),
                   pl.BlockSpec((tp // Q_BLOCK, N_GROUPS * V_ROWS, Q_BLOCK), lambda i: (i, 0, 0)),
                   ot(LANES), o(128), o(128), ot(LANES)],
        out_shape=[jax.ShapeDtypeStruct((N_W, t), BF16), sds(512, BF16), sds(256, BF16),
                   jax.ShapeDtypeStruct((t // Q_BLOCK, N_GROUPS * V_ROWS, Q_BLOCK), BF16),
                   jax.ShapeDtypeStruct((LANES, t), BF16),
                   sds(128, F32), sds(128, F32), jax.ShapeDtypeStruct((LANES, t), F32)],
        compiler_params=_cparams("parallel"),
        name="nsa_prep",
    )(proj, proj, proj, cs, s1, s2, qn_row, kn_rows)


def _gelu_tanh(x):
    return 0.5 * x * (1.0 + jnp.tanh(0.7978845608028654 * (x + 0.044715 * x * x * x)))


def _compress_mlp(x_ref, pa_ref, pb_ref, wa_ref, wb_ref, w2_ref, hb_ref):
    n = x_ref.shape[0]
    x = x_ref[...]
    ha = _dot((x + pa_ref[...]).astype(BF16), wa_ref[...])
    hb_ref[0:n, :] = _dot((x + pb_ref[...]).astype(BF16), wb_ref[...])
    hb_ref[n:n + 8, :] = jnp.zeros((8, hb_ref.shape[1]), F32)
    hid = ha + hb_ref[1:n + 1, :]
    return _dot(_gelu_tanh(hid).astype(BF16), w2_ref[...])


def _nsa_compress_kernel(kx_ref, vx_ref, kpa_ref, kpb_ref, vpa_ref, vpb_ref, kwa_ref, kwb_ref, kw2_ref,
                         vwa_ref, vwb_ref, vw2_ref, cs_ref, s1_ref, s2_ref, kn_ref,
                         kcd_ref, vct_ref, hb_ref):
    n = kx_ref.shape[0]
    rowi = lax.broadcasted_iota(jnp.int32, (n, LANES), 0)
    live = rowi < n - 1
    kc = _compress_mlp(kx_ref, kpa_ref, kpb_ref, kwa_ref, kwb_ref, kw2_ref, hb_ref)
    kc = _rope(_group_rms(kc, kn_ref[0:1, :]), cs_ref[...], s1_ref[...], s2_ref[...])
    kc = jnp.where(live, kc, 0.0)
    c0, c1 = _dup_halves(kc)
    kcd_ref[:, 0:LANES] = c0.astype(BF16)
    kcd_ref[:, LANES:2 * LANES] = c1.astype(BF16)
    vc = _compress_mlp(vx_ref, vpa_ref, vpb_ref, vwa_ref, vwb_ref, vw2_ref, hb_ref)
    vc = jnp.where(live, vc, 0.0)
    vct_ref[0] = vc.T.astype(BF16)


def _nsa_compress(kx, vx, kp, vp, cs, s1, s2, kn_rows, batch, nrow):
    wide = CMP_STRIDE * LANES
    full = lambda a: pl.BlockSpec(a.shape, lambda b: (0,) * a.ndim)
    xs = pl.BlockSpec((nrow, wide), lambda b: (b, 0))
    args = (kx, vx, kp[0], kp[1], vp[0], vp[1], kp[2], kp[3], kp[4], vp[2], vp[3], vp[4], cs, s1, s2, kn_rows)
    return pl.pallas_call(
        _nsa_compress_kernel,
        grid=(batch,),
        in_specs=[xs, xs] + [full(a) for a in args[2:]],
        out_specs=[pl.BlockSpec((nrow, 2 * LANES), lambda b: (b, 0)),
                   pl.BlockSpec((1, LANES, nrow), lambda b: (b, 0, 0))],
        out_shape=[jax.ShapeDtypeStruct((batch * nrow, 2 * LANES), BF16),
                   jax.ShapeDtypeStruct((batch, LANES, nrow), BF16)],
        scratch_shapes=[pltpu.VMEM((nrow + 8, 2 * CMP_HIDDEN), F32)],
        compiler_params=_cparams("parallel"),
        name="nsa_compress",
    )(*args)


def _compress_params(pos, w1, w2):
    def expand(w1_half, pos_half):
        w = w1_half.reshape(CMP_STRIDE, N_DIM, CMP_HIDDEN)
        z = jnp.zeros_like(w)
        g0 = jnp.concatenate([w, z], axis=-1)
        g1 = jnp.concatenate([z, w], axis=-1)
        wx = jnp.stack([g0, g1], axis=1).reshape(CMP_STRIDE * LANES, 2 * CMP_HIDDEN)
        px = jnp.concatenate([pos_half, pos_half], axis=-1).reshape(1, CMP_STRIDE * LANES)
        return wx.astype(BF16), px.astype(F32)

    half = CMP_STRIDE * N_DIM
    wa, pa = expand(w1[:half], pos[:CMP_STRIDE])
    wb, pb = expand(w1[half:], pos[CMP_STRIDE:])
    z2 = jnp.zeros_like(w2)
    w2x = jnp.concatenate([jnp.concatenate([w2, z2], axis=1), jnp.concatenate([z2, w2], axis=1)], axis=0)
    return pa, pb, wa, wb, w2x.astype(BF16)


def _nsa_attn_kernel(qt_ref, gtt_ref, z_ref, ka_ref, vst_ref,
                     kw0, kw1, kw2, kw3, kw4, wt0, wt1, wt2, wt3, wt4,
                     kcd_ref, vct_ref, ovt_ref, out_ref, qa_ref, acc_ref, *, ncmp):
    QB = Q_BLOCK
    NP = N_HEADS // 2
    PW = 2 * QB
    i = pl.program_id(1)
    t0 = i * QB
    kw_refs = (kw0, kw1, kw2, kw3, kw4)
    vwt_refs = (wt0, wt1, wt2, wt3, wt4)
    nwt = len(kw_refs)

    sub = lax.broadcasted_iota(jnp.int32, (QB, QB), 0)
    lane = lax.broadcasted_iota(jnp.int32, (QB, QB), 1)
    top = sub < N_DIM
    t_row = t0 + lax.broadcasted_iota(jnp.int32, (1, QB), 1)
    t_row2 = jnp.concatenate([t_row, t_row], axis=1)

    def pair_q(pr):
        slab = qt_ref[pr * LANES:(pr + 1) * LANES, :]
        zero = jnp.zeros_like(slab)
        return jnp.concatenate([jnp.where(top, slab, zero), jnp.where(top, zero, slab)], axis=1)

    nidx = lax.broadcasted_iota(jnp.int32, (ncmp, PW), 0)
    cmp_ok = (nidx * CMP_STRIDE + (CMP_LEN - 1) <= t_row2) & (nidx < ncmp - 1)
    blk_f = sub.astype(F32)
    cur = jnp.right_shift(t_row, SEL_SHIFT)
    grp = lambda pr: pr // (N_HPG // 2)
    for pr in range(NP):
        qa_ref[pr, 0:LANES, :] = pair_q(pr)
    s_cmp = [_dot(kcd_ref[:, grp(pr) * LANES:(grp(pr) + 1) * LANES], qa_ref[pr, 0:LANES, :]) for pr in range(NP)]
    p_cmp = []
    for pr in range(NP):
        s = jnp.where(cmp_ok, s_cmp[pr], NEG)
        mx = jnp.max(s, axis=0, keepdims=True)
        p = jnp.where(cmp_ok, jnp.exp2(s - mx), 0.0)
        den = jnp.sum(p, axis=0, keepdims=True)
        p_cmp.append(p * (1.0 / jnp.maximum(den, 1e-30)))
    o_cmp = [_dot(vct_ref[0, grp(pr) * N_DIM:(grp(pr) + 1) * N_DIM, :], p_cmp[pr].astype(BF16))
             for pr in range(NP)]
    imps = []
    for g in range(N_GROUPS):
        psum = jnp.zeros((ncmp, QB), F32)
        for pp in range(N_HPG // 2):
            p = p_cmp[g * (N_HPG // 2) + pp]
            psum = psum + p[:, 0:QB] + p[:, QB:PW]
        p_hi = psum.astype(BF16)
        p_lo = (psum - p_hi.astype(F32)).astype(BF16)
        imps.append(_dot(ovt_ref[...], p_hi) + _dot(ovt_ref[...], p_lo))
    forced = (sub == 0) | (sub == cur) | (sub == cur - 1)
    works = [jnp.where(sub > cur, NEG, jnp.where(forced, -NEG, imp)) for imp in imps]
    sels = [jnp.zeros((QB, QB), jnp.bool_) for _ in range(N_GROUPS)]
    for _ in range(SEL_TOPK):
        for g in range(N_GROUPS):
            mx = jnp.max(works[g], axis=0, keepdims=True)
            idx = jnp.min(jnp.where(works[g] == mx, blk_f, float(LANES)), axis=0, keepdims=True)
            pick = blk_f == idx
            sels[g] = sels[g] | pick
            works[g] = jnp.where(pick, -jnp.inf, works[g])
    for g in range(N_GROUPS):
        neg = jnp.where(sels[g] & (sub <= cur), 0.0, NEG).astype(BF16)
        neg2 = jnp.concatenate([neg, neg], axis=1)
        for pp in range(N_HPG // 2):
            qa_ref[g * (N_HPG // 2) + pp, LANES:2 * LANES, :] = neg2

    CH = QB
    above = jnp.concatenate([sub > lane, sub > lane], axis=1)

    def sel_scores(cj):
        k0 = pl.multiple_of(cj * CH, CH)
        return [_dot(ka_ref[pl.ds(k0, CH), 2 * grp(pr) * LANES:2 * (grp(pr) + 1) * LANES], qa_ref[pr])
                for pr in range(NP)]

    def sel_softmax(ss, ms, masked):
        new, alphas, ps = [], [], []
        for pr in range(NP):
            s = jnp.where(above, NEG, ss[pr]) if masked else ss[pr]
            m_new = jnp.maximum(ms[pr], jnp.max(s, axis=0, keepdims=True))
            alphas.append(jnp.exp2(ms[pr] - m_new))
            ps.append(jnp.exp2(s - m_new).astype(BF16))
            new.append(m_new)
        return tuple(new), alphas, ps

    def sel_values(cj, alphas, ps):
        for pr in range(NP):
            g = grp(pr)
            acc_ref[pr] = alphas[pr] * acc_ref[pr] + _dot(vst_ref[cj, g * V_ROWS:(g + 1) * V_ROWS, :], ps[pr])

    def sel_chunk(cj, ms, masked):
        ms, alphas, ps = sel_softmax(sel_scores(cj), ms, masked)
        sel_values(cj, alphas, ps)
        return ms

    def sel_tile(jt, ms):
        nc = KV_TILE // CH
        k0 = pl.multiple_of(jt * KV_TILE, KV_TILE)
        big = [_dot(ka_ref[pl.ds(k0, KV_TILE), 2 * grp(pr) * LANES:2 * (grp(pr) + 1) * LANES], qa_ref[pr])
               for pr in range(NP)]
        for c in range(nc):
            ss = [b[c * CH:(c + 1) * CH] for b in big]
            ms, alphas, ps = sel_softmax(ss, ms, False)
            sel_values(jt * nc + c, alphas, ps)
        return ms

    acc_ref[...] = jnp.zeros_like(acc_ref)
    ms = tuple(jnp.full((1, PW), 2.0 * NEG, F32) for _ in range(NP))
    n_tile = i // (KV_TILE // CH)
    ms = lax.fori_loop(0, n_tile, sel_tile, ms)
    ms = lax.fori_loop(n_tile * (KV_TILE // CH), i, lambda cj, ms: sel_chunk(cj, ms, False), ms)
    sel_chunk(i, ms, True)
    o_sel = [acc_ref[pr, 0:N_DIM, :] * (1.0 / acc_ref[pr, N_DIM:N_DIM + 1, :]) for pr in range(NP)]

    s_win = [[_dot(kw_refs[c][:, grp(pr) * LANES:(grp(pr) + 1) * LANES], qa_ref[pr, 0:LANES, :])
              for c in range(nwt)] for pr in range(NP)]
    p_win, d_win = [], []
    for pr in range(NP):
        ss = []
        for c in range(nwt):
            ok = (i - (nwt - 1) + c) >= 0
            if c == 0:
                ok = ok & above
            elif c == nwt - 1:
                ok = ok & jnp.logical_not(above)
            ss.append(jnp.where(ok, s_win[pr][c], NEG))
        mx = ss[0].max(axis=0, keepdims=True)
        for c in range(1, nwt):
            mx = jnp.maximum(mx, ss[c].max(axis=0, keepdims=True))
        ps = [jnp.exp2(ss[c] - mx) for c in range(nwt)]
        den = ps[0].sum(axis=0, keepdims=True)
        for c in range(1, nwt):
            den = den + ps[c].sum(axis=0, keepdims=True)
        p_win.append([p.astype(BF16) for p in ps])
        d_win.append(den)
    o_win = []
    for pr in range(NP):
        g = grp(pr)
        acc = _dot(vwt_refs[0][g * N_DIM:(g + 1) * N_DIM, :], p_win[pr][0])
        for c in range(1, nwt):
            acc = acc + _dot(vwt_refs[c][g * N_DIM:(g + 1) * N_DIM, :], p_win[pr][c])
        o_win.append(acc * (1.0 / d_win[pr]))

    for pr in range(NP):
        halves = []
        for par in range(2):
            h = 2 * pr + par
            cs = slice(par * QB, (par + 1) * QB)
            gc = gtt_ref[SM_G + 3 * h:SM_G + 3 * h + 1, :]
            gs = gtt_ref[SM_G + 3 * h + 1:SM_G + 3 * h + 2, :]
            gw = gtt_ref[SM_G + 3 * h + 2:SM_G + 3 * h + 3, :]
            halves.append(gc * o_cmp[pr][:, cs] + gs * o_sel[pr][:, cs] + gw * o_win[pr][:, cs])
        y = jnp.concatenate(halves, axis=0).T
        sl = slice(pr * LANES, (pr + 1) * LANES)
        out_ref[:, sl] = (y * _silu(z_ref[:, sl])).astype(out_ref.dtype)


def _nsa_attn(proj, qt, gtt, ka, vst, kwd, vwt, kcd, vct, ovt, batch, seq):
    QB = Q_BLOCK
    nqb = seq // QB
    ncmp = seq // CMP_STRIDE
    ntile = seq // KV_TILE
    nwt = WINDOW // QB + 1
    row = lambda w: pl.BlockSpec((QB, w), lambda b, i: (b * nqb + i, 0))
    colb = lambda r: pl.BlockSpec((r, QB), lambda b, i: (0, b * nqb + i))
    wblk = lambda b, i, c: b * nqb + jnp.maximum(i - (nwt - 1) + c, 0)
    in_specs = ([colb(N_W), colb(LANES), pl.BlockSpec((QB, N_W), lambda b, i: (b * nqb + i, C_NZ // N_W)),
                 pl.BlockSpec((seq, 4 * LANES), lambda b, i: (b, 0)),
                 pl.BlockSpec((nqb, N_GROUPS * V_ROWS, QB), lambda b, i: (b, 0, 0))]
                + [pl.BlockSpec((QB, 2 * LANES), lambda b, i, c=c: (wblk(b, i, c), 0)) for c in range(nwt)]
                + [pl.BlockSpec((LANES, QB), lambda b, i, c=c: (0, wblk(b, i, c))) for c in range(nwt)]
                + [pl.BlockSpec((ncmp, 2 * LANES), lambda b, i: (b, 0)),
                   pl.BlockSpec((1, LANES, ncmp), lambda b, i: (b, 0, 0)),
                   pl.BlockSpec((LANES, ncmp), lambda b, i: (0, 0))])
    return pl.pallas_call(
        functools.partial(_nsa_attn_kernel, ncmp=ncmp),
        grid=(batch, nqb),
        in_specs=in_specs,
        out_specs=row(N_W),
        out_shape=jax.ShapeDtypeStruct((batch * seq, N_W), BF16),
        scratch_shapes=[pltpu.VMEM((N_HEADS // 2, 2 * LANES, 2 * QB), BF16),
                        pltpu.VMEM((N_HEADS // 2, V_ROWS, 2 * QB), F32)],
        compiler_params=_cparams("parallel", "arbitrary"),
        name="nsa_attn",
    )(qt, gtt, proj, ka, vst, *([kwd] * nwt), *([vwt] * nwt), kcd, vct, ovt)


def _hgrn_kernel(f_ref, i_ref, q_ref, z_ref, lb_ref, hn_ref, out_ref, kpad, bpad, vpad, st_ref):
    T, D, BL = H_TILE, H_DIM, H_BLOCK
    nblk = T // BL

    @pl.when(pl.program_id(1) == 0)
    def _():
        st_ref[...] = jnp.zeros_like(st_ref)

    lb = lb_ref[...]
    fp = f_ref[...]
    e = jnp.exp(-jnp.abs(fp))
    r = 1.0 / (1.0 + e)
    sig = jnp.where(fp >= 0, r, e * r)
    nsig = jnp.where(fp >= 0, e * r, r)
    logf = jnp.log(lb + (1.0 - lb) * sig)
    kk = (1.0 - lb) * nsig

    row = lax.broadcasted_iota(jnp.int32, (T, T), 0)
    col = lax.broadcasted_iota(jnp.int32, (T, T), 1)
    same = jnp.right_shift(row, H_BLOCK_SHIFT) == jnp.right_shift(col, H_BLOCK_SHIFT)
    tri01 = jnp.where(same & (col <= row), 1.0, 0.0).astype(BF16)
    blk01 = jnp.where(same, 1.0, 0.0).astype(BF16)
    hi, mid, lo = _split3(logf)
    bc = _dot(tri01, hi) + _dot(tri01, mid) + _dot(tri01, lo)
    bend = _dot(blk01, hi) + _dot(blk01, mid) + _dot(blk01, lo)

    zpad = jnp.zeros((BL, H_W), F32)
    kpad[0:BL, :] = zpad
    bpad[0:BL, :] = zpad
    vpad[0:BL, :] = zpad
    kpad[BL:BL + T, :] = kk
    bpad[BL:BL + T, :] = bc
    vpad[BL:BL + T, :] = i_ref[...]

    rmod = jnp.bitwise_and(lax.broadcasted_iota(jnp.int32, (T, 1), 0), BL - 1)
    for h in range(H_HEADS):
        hs = slice(h * D, (h + 1) * D)
        q = _silu(q_ref[:, hs])
        b = bc[:, hs]
        v = vpad[BL:BL + T, hs]
        o = jnp.zeros((T, D), F32)
        for d in range(BL):
            ks = kpad[BL - d:BL - d + T, hs]
            bs = bpad[BL - d:BL - d + T, hs]
            vs = vpad[BL - d:BL - d + T, hs]
            a = jnp.sum(q * ks * jnp.exp(b - bs), axis=1, keepdims=True)
            o = o + jnp.where(rmod >= d, a, 0.0) * vs
        qt = (q * jnp.exp(b)).astype(BF16)
        kt = (kk[:, hs] * jnp.exp(bend[:, hs] - b)).astype(BF16)
        dec = jnp.exp(bend[:, hs])
        vb = v.astype(BF16)
        st = st_ref[h]
        outs = []
        for n in range(nblk):
            rs = slice(n * BL, (n + 1) * BL)
            outs.append(o[rs] + _dot_nt(qt[rs], st.astype(BF16)))
            st = st * dec[n * BL:n * BL + 1, :] + _dot_tn(vb[rs], kt[rs])
        st_ref[h] = st
        oh = jnp.concatenate(outs, axis=0)
        on = oh * lax.rsqrt(jnp.mean(oh * oh, axis=1, keepdims=True) + EPS) * hn_ref[:, hs]
        out_ref[:, hs] = (on * _silu(z_ref[:, hs])).astype(out_ref.dtype)


def _hgrn(proj, lb_row, hn_row, batch, seq):
    T = H_TILE
    nt = seq // T
    cb = lambda j: pl.BlockSpec((T, H_W), lambda b, c, j=j: (b * nt + c, j))
    vec = pl.BlockSpec((1, H_W), lambda b, c: (0, 0))
    return pl.pallas_call(
        _hgrn_kernel,
        grid=(batch, nt),
        in_specs=[cb(0), cb(1), cb(2), cb(3), vec, vec],
        out_specs=pl.BlockSpec((T, H_W), lambda b, c: (b * nt + c, 0)),
        out_shape=jax.ShapeDtypeStruct((batch * seq, H_W), BF16),
        scratch_shapes=[pltpu.VMEM((T + H_BLOCK, H_W), F32)] * 3 + [pltpu.VMEM((H_HEADS, H_DIM, H_DIM), F32)],
        compiler_params=_cparams("parallel", "arbitrary"),
        name="hgrn2",
    )(proj, proj, proj, proj, lb_row, hn_row)


def _even_relayout(a):
    nb = 5 * M_W + 2 * M_HEADS
    g0 = nb + N_W + 6 * N_KV_W
    col = lambda lo, n: lax.slice_in_dim(a, lo, lo + n, axis=-1)
    pad = jnp.zeros(a.shape[:-1] + (C_NZ - C_SMALL - SM_G - 3 * N_HEADS,), a.dtype)
    out = jnp.concatenate([
        col(0, 5 * M_W),
        col(nb, N_W + 6 * N_KV_W),
        col(5 * M_W, 2 * M_HEADS),
        col(g0, 3 * N_HEADS), pad,
        col(g0 + 3 * N_HEADS, N_W),
    ], axis=-1)
    assert out.shape[-1] == EV_COLS
    return out


def _rope_tables(positions):
    inv_freq = ROPE_THETA ** (-np.arange(ROPE_HALF, dtype=np.float64) / ROPE_HALF)
    ang = positions.astype(np.float64)[:, None] * inv_freq[None, :]
    cos, sin = np.cos(ang).astype(np.float32), np.sin(ang).astype(np.float32)
    n = positions.shape[0]
    z8 = np.zeros((n, ROPE_HALF), np.float32)
    rest1 = np.ones((n, N_DIM - ROPE_DIM), np.float32)
    rest0 = np.zeros((n, N_DIM - ROPE_DIM), np.float32)
    cs = np.concatenate([cos, cos, rest1], axis=1)
    s1 = np.concatenate([z8, sin, rest0], axis=1)
    s2 = np.concatenate([-sin, z8, rest0], axis=1)
    two = lambda a: np.concatenate([a, a], axis=1)
    return two(cs), two(s1), two(s2)


def _overlap_matrix(seq):
    ncmp = seq // CMP_STRIDE
    n_sel = seq // SEL_BLOCK
    cs = np.arange(ncmp) * CMP_STRIDE
    ss = np.arange(n_sel) * SEL_BLOCK
    ov = np.clip(np.minimum(cs[:, None] + CMP_LEN, ss[None, :] + SEL_BLOCK)
                 - np.maximum(cs[:, None], ss[None, :]), 0, None).astype(np.float32)
    ov[ncmp - 1, :] = 0.0
    return np.pad(ov, ((0, 0), (0, LANES - n_sel)))


def _even_layer(x2d, norm_w, w_in, b_in, f_bias, conv_w, conv_b, head_norm, q_norm, k_norm,
                ck_pos, ck_w1, ck_w2, cv_pos, cv_w1, cv_w2, w_out, batch, seq):
    assert seq // SEL_BLOCK <= LANES and seq % KV_TILE == 0
    proj = _norm_proj(x2d, norm_w, _even_relayout(w_in).astype(BF16), _even_relayout(b_in))

    f_bias_row = jnp.zeros((1, LANES), F32).at[0, SM_F:SM_F + M_HEADS].set(f_bias)
    ya = _mlstm(proj, conv_w, conv_b, f_bias_row, head_norm, batch, seq)

    cs, s1, s2 = (jnp.asarray(a) for a in _rope_tables(np.arange(seq)))
    qn_row = jnp.concatenate([q_norm, q_norm]).reshape(1, LANES)
    kn_rows = jnp.concatenate([k_norm, k_norm], axis=1)
    qt, ka, kwd, vst, vwt, kc, vc, gtt = _nsa_prep(proj, cs, s1, s2, qn_row, kn_rows, batch, seq)

    ncmp = seq // CMP_STRIDE
    ccs, cs1, cs2 = (jnp.asarray(a) for a in _rope_tables(np.arange(ncmp) * CMP_STRIDE + CMP_LEN - 1))
    kcd, vct = _nsa_compress(
        kc.reshape(batch * ncmp, CMP_STRIDE * LANES), vc.reshape(batch * ncmp, CMP_STRIDE * LANES),
        _compress_params(ck_pos, ck_w1, ck_w2), _compress_params(cv_pos, cv_w1, cv_w2),
        ccs, cs1, cs2, kn_rows, batch, ncmp)

    ovt = jnp.asarray(_overlap_matrix(seq).T).astype(BF16)
    yb = _nsa_attn(proj, qt, gtt, ka, vst, kwd, vwt, kcd, vct, ovt, batch, seq)

    w_o = w_out.astype(BF16)
    return _out_proj2(ya, yb, w_o[:M_W], w_o[M_W:], x2d)


def _odd_layer(x2d, norm_w, w_in, b_in, lb, head_norm, w_out, batch, seq):
    proj = _norm_proj(x2d, norm_w, w_in.astype(BF16), b_in)
    y = _hgrn(proj, lb.reshape(1, H_W), head_norm.reshape(1, H_W), batch, seq)
    return _out_proj1(y, w_out.astype(BF16), x2d)


def kernel(x, norm_w, ev_w_in, ev_b_in, mlstm_f_bias, mlstm_conv_w, mlstm_conv_b, mlstm_head_norm, nsa_q_norm, nsa_k_norm, cmp_k_pos, cmp_k_w1, cmp_k_w2, cmp_v_pos, cmp_v_w1, cmp_v_w2, ev_w_out, od_w_in, od_b_in, hgrn_lb_logits, hgrn_head_norm, od_w_out):
    batch, seq, d = x.shape
    depth = norm_w.shape[0]
    lbs = jnp.cumsum(jax.nn.softmax(hgrn_lb_logits.astype(F32), axis=0), axis=0)
    x2d = x.reshape(batch * seq, d)
    for layer in range(depth):
        j = layer // 2
        if layer % 2 == 0:
            x2d = _even_layer(x2d, norm_w[layer], ev_w_in[j], ev_b_in[j], mlstm_f_bias[j], mlstm_conv_w[j],
                              mlstm_conv_b[j], mlstm_head_norm[j], nsa_q_norm[j], nsa_k_norm[j], cmp_k_pos[j],
                              cmp_k_w1[j], cmp_k_w2[j], cmp_v_pos[j], cmp_v_w1[j], cmp_v_w2[j], ev_w_out[j],
                              batch, seq)
        else:
            x2d = _odd_layer(x2d, norm_w[layer], od_w_in[j], od_b_in[j], lbs[j], hgrn_head_norm[j], od_w_out[j],
                             batch, seq)
    return x2d.reshape(batch, seq, d)
```

```python
import functools

import numpy as np
import jax
import jax.numpy as jnp
from jax import lax
from jax.experimental import pallas as pl
from jax.experimental.pallas import tpu as pltpu

F32 = jnp.float32
BF16 = jnp.bfloat16

LANES = 128
EPS = 1e-6
NEG = -1e30
VMEM_LIMIT_BYTES = 56 * 1024 * 1024

M_HEADS = 4
M_DIM = 128
M_W = M_HEADS * M_DIM
M_CHUNK = 128
CONV_K = 4
N_HEADS = 8
N_GROUPS = 2
N_HPG = N_HEADS // N_GROUPS
N_DIM = 64
N_W = N_HEADS * N_DIM
N_KV_W = N_GROUPS * N_DIM
CMP_LEN = 32
CMP_STRIDE = 16
CMP_HIDDEN = 128
SEL_BLOCK = 64
SEL_SHIFT = 6
SEL_TOPK = 16
WINDOW = 512
Q_BLOCK = 128
ROPE_THETA = 500000.0
ROPE_DIM = N_DIM // 4
ROPE_HALF = ROPE_DIM // 2
KV_TILE = 512
V_ROWS = 80
LOG2E = 1.4426950408889634
H_HEADS = 8
H_DIM = 128
H_W = H_HEADS * H_DIM
H_TILE = 128
H_LAG = 4

C_MQ, C_MK, C_MV, C_MO, C_MZ = 0, 512, 1024, 1536, 2048
C_NQ = 2560
C_KC, C_VC, C_KS, C_VS = 3072, 3200, 3328, 3456
C_KW, C_VW, C_SMALL = 3584, 3712, 3840
C_NZ = 4096
EV_COLS = 4608
SM_I, SM_F, SM_G = 0, 4, 8


def _cparams(*sem):
    return pltpu.CompilerParams(dimension_semantics=sem, vmem_limit_bytes=VMEM_LIMIT_BYTES)


def _dot(a, b):
    return jnp.dot(a, b, preferred_element_type=F32)


def _dot_nt(a, b):
    return lax.dot_general(a, b, (((1,), (1,)), ((), ())), preferred_element_type=F32)


def _dot_tn(a, b):
    return lax.dot_general(a, b, (((0,), (0,)), ((), ())), preferred_element_type=F32)


def _sigmoid(x):
    e = jnp.exp(-jnp.abs(x))
    r = 1.0 / (1.0 + e)
    return jnp.where(x >= 0, r, e * r)


def _silu(x):
    return x * _sigmoid(x)


def _split3(x):
    hi = x.astype(BF16)
    r = x - hi.astype(F32)
    mid = r.astype(BF16)
    lo = (r - mid.astype(F32)).astype(BF16)
    return hi, mid, lo


def _dot01(m01, x):
    hi, mid, lo = _split3(x)
    return _dot(m01, hi) + _dot(m01, mid) + _dot(m01, lo)


def _norm_proj_kernel(x_ref, nw_ref, w_ref, b_ref, o_ref, h_ref):
    @pl.when(pl.program_id(1) == 0)
    def _():
        x = x_ref[...]
        ms = jnp.mean(x * x, axis=-1, keepdims=True)
        h_ref[...] = (x * lax.rsqrt(ms + EPS) * nw_ref[...]).astype(BF16)

    o_ref[...] = (_dot(h_ref[...], w_ref[...]) + b_ref[...]).astype(o_ref.dtype)


def _norm_proj(x2d, norm_w, w_bf16, bias, tm=1024, tn=512):
    t, d = x2d.shape
    n = w_bf16.shape[1]
    return pl.pallas_call(
        _norm_proj_kernel,
        grid=(t // tm, n // tn),
        in_specs=[
            pl.BlockSpec((tm, d), lambda i, j: (i, 0)),
            pl.BlockSpec((1, d), lambda i, j: (0, 0)),
            pl.BlockSpec((d, tn), lambda i, j: (0, j)),
            pl.BlockSpec((1, tn), lambda i, j: (0, j)),
        ],
        out_specs=pl.BlockSpec((tm, tn), lambda i, j: (i, j)),
        out_shape=jax.ShapeDtypeStruct((t, n), F32),
        scratch_shapes=[pltpu.VMEM((tm, d), BF16)],
        compiler_params=_cparams("parallel", "arbitrary"),
        name="norm_proj",
    )(x2d, norm_w.reshape(1, d), w_bf16, bias.reshape(1, n))


def _out_proj2_kernel(ya_ref, yb_ref, wa_ref, wb_ref, x_ref, o_ref):
    o_ref[...] = x_ref[...] + _dot(ya_ref[...], wa_ref[...]) + _dot(yb_ref[...], wb_ref[...])


def _out_proj2(ya, yb, wa, wb, x2d, tm=1024):
    t, d = x2d.shape
    ka, kb = ya.shape[1], yb.shape[1]
    return pl.pallas_call(
        _out_proj2_kernel,
        grid=(t // tm,),
        in_specs=[
            pl.BlockSpec((tm, ka), lambda i: (i, 0)),
            pl.BlockSpec((tm, kb), lambda i: (i, 0)),
            pl.BlockSpec((ka, d), lambda i: (0, 0)),
            pl.BlockSpec((kb, d), lambda i: (0, 0)),
            pl.BlockSpec((tm, d), lambda i: (i, 0)),
        ],
        out_specs=pl.BlockSpec((tm, d), lambda i: (i, 0)),
        out_shape=jax.ShapeDtypeStruct((t, d), F32),
        compiler_params=_cparams("parallel"),
        name="out_proj2",
    )(ya, yb, wa, wb, x2d)


def _out_proj1_kernel(y_ref, w_ref, x_ref, o_ref):
    o_ref[...] = x_ref[...] + _dot(y_ref[...], w_ref[...])


def _out_proj1(y, w, x2d, tm=1024):
    t, d = x2d.shape
    k = y.shape[1]
    return pl.pallas_call(
        _out_proj1_kernel,
        grid=(t // tm,),
        in_specs=[
            pl.BlockSpec((tm, k), lambda i: (i, 0)),
            pl.BlockSpec((k, d), lambda i: (0, 0)),
            pl.BlockSpec((tm, d), lambda i: (i, 0)),
        ],
        out_specs=pl.BlockSpec((tm, d), lambda i: (i, 0)),
        out_shape=jax.ShapeDtypeStruct((t, d), F32),
        compiler_params=_cparams("parallel"),
        name="out_proj1",
    )(y, w, x2d)


def _mlstm_kernel(q_ref, k_ref, v_ref, o_ref, z_ref, sm_ref, cw_ref, cb_ref, fb_ref, hn_ref,
                  out_ref, xbuf, c_ref, n_ref, m_ref):
    L, D = M_CHUNK, M_DIM

    @pl.when(pl.program_id(1) == 0)
    def _():
        xbuf[0:8, :] = jnp.zeros((8, 2 * M_W), F32)
        c_ref[...] = jnp.zeros_like(c_ref)
        n_ref[...] = jnp.zeros_like(n_ref)
        m_ref[...] = jnp.zeros_like(m_ref)

    xbuf[8:8 + L, 0:M_W] = q_ref[...]
    xbuf[8:8 + L, M_W:2 * M_W] = k_ref[...]
    acc = jnp.broadcast_to(cb_ref[...], (L, 2 * M_W))
    for j in range(CONV_K):
        acc = acc + cw_ref[j:j + 1, :] * xbuf[8 - (CONV_K - 1) + j:8 - (CONV_K - 1) + j + L, :]
    qk = _silu(acc)
    xbuf[0:8, :] = xbuf[L:L + 8, :]

    sm = sm_ref[...]
    row = lax.broadcasted_iota(jnp.int32, (L, L), 0)
    col = lax.broadcasted_iota(jnp.int32, (L, L), 1)
    eye = (row == col).astype(F32)
    tri = (col <= row).astype(F32)
    tri_t = (row <= col).astype(F32)
    causal = col <= row

    heads = []
    for h in range(M_HEADS):
        hs = slice(h * D, (h + 1) * D)
        q = qk[:, hs]
        k = qk[:, M_W + h * D:M_W + (h + 1) * D] * (D ** -0.5)
        i_col = sm[:, SM_I + h:SM_I + h + 1]
        fpre = sm[:, SM_F + h:SM_F + h + 1] + fb_ref[:, SM_F + h:SM_F + h + 1]
        f_col = jnp.minimum(fpre, 0.0) - jnp.log(1.0 + jnp.exp(-jnp.abs(fpre)))
        f_row = jnp.sum(eye * f_col, axis=0, keepdims=True)
        i_row = jnp.sum(eye * i_col, axis=0, keepdims=True)
        bcum_col = jnp.sum(tri * f_row, axis=1, keepdims=True)
        bcum_row = jnp.sum(tri_t * f_col, axis=0, keepdims=True)
        m_prev = m_ref[h]
        dmat = jnp.where(causal, bcum_col - bcum_row + i_row, -jnp.inf)
        inter = bcum_col + m_prev
        m_row = jnp.maximum(jnp.max(dmat, axis=1, keepdims=True), inter)
        b_last = bcum_col[L - 1:L, :]
        dl = b_last - bcum_col + i_col
        m_new = jnp.maximum(b_last + m_prev, jnp.max(dl, axis=0, keepdims=True))
        wkk = jnp.exp(dl - m_new) * k
        heads.append(dict(
            hs=hs, q=q, qb=q.astype(BF16), kb=k.astype(BF16), vb=v_ref[:, hs].astype(BF16), wkk=wkk,
            e=jnp.exp(dmat - m_row), w_inter=jnp.exp(inter - m_row), floor=jnp.exp(-m_row),
            decay=jnp.exp(b_last + m_prev - m_new), m_new=m_new))
    s_qk = [_dot_nt(t["qb"], t["kb"]) for t in heads]
    s_qc = [_dot(t["qb"], c_ref[h].astype(BF16)) for h, t in enumerate(heads)]
    s_kv = [_dot_tn(t["wkk"].astype(BF16), t["vb"]) for t in heads]
    smats = [s_qk[h] * t["e"] for h, t in enumerate(heads)]
    s_sv = [_dot(smats[h].astype(BF16), t["vb"]) for h, t in enumerate(heads)]
    for h, t in enumerate(heads):
        hs = t["hs"]
        num = s_sv[h] + t["w_inter"] * s_qc[h]
        den = (jnp.sum(smats[h], axis=1, keepdims=True)
               + t["w_inter"] * jnp.sum(t["q"] * n_ref[h], axis=1, keepdims=True))
        hh = num / jnp.maximum(jnp.abs(den), t["floor"])
        c_ref[h] = t["decay"] * c_ref[h] + s_kv[h]
        n_ref[h] = t["decay"] * n_ref[h] + jnp.sum(t["wkk"], axis=0, keepdims=True)
        m_ref[h] = t["m_new"]
        hn = hh * lax.rsqrt(jnp.mean(hh * hh, axis=1, keepdims=True) + EPS) * hn_ref[:, hs]
        out_ref[:, hs] = (hn * _sigmoid(o_ref[:, hs]) * _silu(z_ref[:, hs])).astype(out_ref.dtype)


def _mlstm(proj, conv_w, conv_b, f_bias_row, head_norm, batch, seq):
    L = M_CHUNK
    nc = seq // L
    cb = lambda off: pl.BlockSpec((L, M_W), lambda b, c, off=off: (b * nc + c, off // M_W))
    return pl.pallas_call(
        _mlstm_kernel,
        grid=(batch, nc),
        in_specs=[
            cb(C_MQ), cb(C_MK), cb(C_MV), cb(C_MO), cb(C_MZ),
            pl.BlockSpec((L, LANES), lambda b, c: (b * nc + c, C_SMALL // LANES)),
            pl.BlockSpec((CONV_K, 2 * M_W), lambda b, c: (0, 0)),
            pl.BlockSpec((1, 2 * M_W), lambda b, c: (0, 0)),
            pl.BlockSpec((1, LANES), lambda b, c: (0, 0)),
            pl.BlockSpec((1, M_W), lambda b, c: (0, 0)),
        ],
        out_specs=pl.BlockSpec((L, M_W), lambda b, c: (b * nc + c, 0)),
        out_shape=jax.ShapeDtypeStruct((batch * seq, M_W), BF16),
        scratch_shapes=[
            pltpu.VMEM((L + 8, 2 * M_W), F32),
            pltpu.VMEM((M_HEADS, M_DIM, M_DIM), F32),
            pltpu.VMEM((M_HEADS, 1, M_DIM), F32),
            pltpu.VMEM((M_HEADS, 1, 1), F32),
        ],
        compiler_params=_cparams("parallel", "arbitrary"),
        name="mlstm",
    )(proj, proj, proj, proj, proj, proj, conv_w, conv_b.reshape(1, -1), f_bias_row, head_norm.reshape(1, -1))


def _group_rms(x, w_row):
    lane = lax.broadcasted_iota(jnp.int32, x.shape, 1)
    left = lane < N_DIM
    sq = x * x
    tot = jnp.sum(sq, axis=1, keepdims=True)
    lft = jnp.sum(jnp.where(left, sq, 0.0), axis=1, keepdims=True)
    ms = jnp.where(left, lft, tot - lft) * (1.0 / N_DIM)
    return x * lax.rsqrt(ms + EPS) * w_row


def _rope(x, cs, s1, s2):
    return x * cs + pltpu.roll(x, ROPE_HALF, 1) * s1 + pltpu.roll(x, LANES - ROPE_HALF, 1) * s2


def _dup_halves(x):
    lane = lax.broadcasted_iota(jnp.int32, x.shape, 1)
    sw = pltpu.roll(x, N_DIM, 1)
    left = lane < N_DIM
    return jnp.where(left, x, sw), jnp.where(left, sw, x)


def _nsa_prep_kernel(q_ref, kv_ref, kw_ref, cs_ref, s1_ref, s2_ref, qn_ref, kn_ref,
                     qt_ref, ka_ref, kwd_ref, vst_ref, vwt_ref, kc_ref, vc_ref, gtt_ref,
                     *, tp, seq):
    cs, s1, s2 = cs_ref[...], s1_ref[...], s2_ref[...]
    for j in range(N_W // LANES):
        sl = slice(j * LANES, (j + 1) * LANES)
        qn = _group_rms(q_ref[:, sl], qn_ref[...])
        qt_ref[sl, :] = (_rope(qn, cs, s1, s2) * (N_DIM ** -0.5 * LOG2E)).T.astype(BF16)
    kv = kv_ref[...]
    kc_ref[...] = kv[:, 0:LANES]
    vc_ref[...] = kv[:, LANES:2 * LANES]
    ks = _rope(_group_rms(kv[:, 2 * LANES:3 * LANES], kn_ref[1:2, :]), cs, s1, s2)
    k0, k1 = _dup_halves(ks)
    pos0 = (pl.program_id(0) * tp) % seq
    pos = pos0 + lax.broadcasted_iota(jnp.int32, (tp, LANES), 0)
    blk = lax.broadcasted_iota(jnp.int32, (tp, LANES), 1)
    onehot = jnp.where(jnp.right_shift(pos, SEL_SHIFT) == blk, 1.0, 0.0).astype(BF16)
    ka_ref[:, 0:LANES] = k0.astype(BF16)
    ka_ref[:, LANES:2 * LANES] = onehot
    ka_ref[:, 2 * LANES:3 * LANES] = k1.astype(BF16)
    ka_ref[:, 3 * LANES:4 * LANES] = onehot
    vst = kv[:, 3 * LANES:4 * LANES].T
    fill = (lax.broadcasted_iota(jnp.int32, (V_ROWS - N_DIM, Q_BLOCK), 0) == 0).astype(BF16)
    for c in range(tp // Q_BLOCK):
        for g in range(N_GROUPS):
            vst_ref[c, g * V_ROWS:g * V_ROWS + N_DIM, :] = (
                vst[g * N_DIM:(g + 1) * N_DIM, c * Q_BLOCK:(c + 1) * Q_BLOCK].astype(BF16))
            vst_ref[c, g * V_ROWS + N_DIM:(g + 1) * V_ROWS, :] = fill
    kwb = kw_ref[...]
    kw = _rope(_group_rms(kwb[:, 0:LANES], kn_ref[2:3, :]), cs, s1, s2)
    w0, w1 = _dup_halves(kw)
    kwd_ref[:, 0:LANES] = w0.astype(BF16)
    kwd_ref[:, LANES:2 * LANES] = w1.astype(BF16)
    vwt_ref[...] = kwb[:, LANES:2 * LANES].T.astype(BF16)
    gtt_ref[...] = _sigmoid(kwb[:, 2 * LANES:3 * LANES]).T


def _nsa_prep(proj, cs, s1, s2, qn_row, kn_rows, batch, seq):
    tp = KV_TILE
    t = batch * seq
    nsb = seq // tp
    rb = lambda w, off: pl.BlockSpec((tp, w), lambda i, off=off, w=w: (i, off // w))
    tab = pl.BlockSpec((tp, LANES), lambda i: (i % nsb, 0))
    o = lambda w: pl.BlockSpec((tp, w), lambda i: (i, 0))
    ot = lambda r: pl.BlockSpec((r, tp), lambda i: (0, i))
    sds = lambda w, dt: jax.ShapeDtypeStruct((t, w), dt)
    return pl.pallas_call(
        functools.partial(_nsa_prep_kernel, tp=tp, seq=seq),
        grid=(t // tp,),
        in_specs=[rb(512, C_NQ), rb(512, C_KC), rb(512, C_KW), tab, tab, tab,
                  pl.BlockSpec((1, LANES), lambda i: (0, 0)), pl.BlockSpec((3, LANES), lambda i: (0, 0))],
        out_specs=[ot(N_W), o(512), o(256),
                   pl.BlockSpec((tp // Q_BLOCK, N_GROUPS * V_ROWS, Q_BLOCK), lambda i: (i, 0, 0)),
                   ot(LANES), o(128), o(128), ot(LANES)],
        out_shape=[jax.ShapeDtypeStruct((N_W, t), BF16), sds(512, BF16), sds(256, BF16),
                   jax.ShapeDtypeStruct((t // Q_BLOCK, N_GROUPS * V_ROWS, Q_BLOCK), BF16),
                   jax.ShapeDtypeStruct((LANES, t), BF16),
                   sds(128, F32), sds(128, F32), jax.ShapeDtypeStruct((LANES, t), F32)],
        compiler_params=_cparams("parallel"),
        name="nsa_prep",
    )(proj, proj, proj, cs, s1, s2, qn_row, kn_rows)


def _gelu_tanh(x):
    return 0.5 * x * (1.0 + jnp.tanh(0.7978845608028654 * (x + 0.044715 * x * x * x)))


def _compress_mlp(x_ref, pa_ref, pb_ref, wa_ref, wb_ref, w2_ref, hb_ref):
    n = x_ref.shape[0]
    x = x_ref[...]
    ha = _dot((x + pa_ref[...]).astype(BF16), wa_ref[...])
    hb_ref[0:n, :] = _dot((x + pb_ref[...]).astype(BF16), wb_ref[...])
    hb_ref[n:n + 8, :] = jnp.zeros((8, hb_ref.shape[1]), F32)
    hid = ha + hb_ref[1:n + 1, :]
    return _dot(_gelu_tanh(hid).astype(BF16), w2_ref[...])


def _nsa_compress_kernel(kx_ref, vx_ref, kpa_ref, kpb_ref, vpa_ref, vpb_ref, kwa_ref, kwb_ref, kw2_ref,
                         vwa_ref, vwb_ref, vw2_ref, cs_ref, s1_ref, s2_ref, kn_ref,
                         kcd_ref, vct_ref, hb_ref):
    n = kx_ref.shape[0]
    rowi = lax.broadcasted_iota(jnp.int32, (n, LANES), 0)
    live = rowi < n - 1
    kc = _compress_mlp(kx_ref, kpa_ref, kpb_ref, kwa_ref, kwb_ref, kw2_ref, hb_ref)
    kc = _rope(_group_rms(kc, kn_ref[0:1, :]), cs_ref[...], s1_ref[...], s2_ref[...])
    kc = jnp.where(live, kc, 0.0)
    c0, c1 = _dup_halves(kc)
    kcd_ref[:, 0:LANES] = c0.astype(BF16)
    kcd_ref[:, LANES:2 * LANES] = c1.astype(BF16)
    vc = _compress_mlp(vx_ref, vpa_ref, vpb_ref, vwa_ref, vwb_ref, vw2_ref, hb_ref)
    vc = jnp.where(live, vc, 0.0)
    vct_ref[0] = vc.T.astype(BF16)


def _nsa_compress(kx, vx, kp, vp, cs, s1, s2, kn_rows, batch, nrow):
    wide = CMP_STRIDE * LANES
    full = lambda a: pl.BlockSpec(a.shape, lambda b: (0,) * a.ndim)
    xs = pl.BlockSpec((nrow, wide), lambda b: (b, 0))
    args = (kx, vx, kp[0], kp[1], vp[0], vp[1], kp[2], kp[3], kp[4], vp[2], vp[3], vp[4], cs, s1, s2, kn_rows)
    return pl.pallas_call(
        _nsa_compress_kernel,
        grid=(batch,),
        in_specs=[xs, xs] + [full(a) for a in args[2:]],
        out_specs=[pl.BlockSpec((nrow, 2 * LANES), lambda b: (b, 0)),
                   pl.BlockSpec((1, LANES, nrow), lambda b: (b, 0, 0))],
        out_shape=[jax.ShapeDtypeStruct((batch * nrow, 2 * LANES), BF16),
                   jax.ShapeDtypeStruct((batch, LANES, nrow), BF16)],
        scratch_shapes=[pltpu.VMEM((nrow + 8, 2 * CMP_HIDDEN), F32)],
        compiler_params=_cparams("parallel"),
        name="nsa_compress",
    )(*args)


def _compress_params(pos, w1, w2):
    def expand(w1_half, pos_half):
        w = w1_half.reshape(CMP_STRIDE, N_DIM, CMP_HIDDEN)
        z = jnp.zeros_like(w)
        g0 = jnp.concatenate([w, z], axis=-1)
        g1 = jnp.concatenate([z, w], axis=-1)
        wx = jnp.stack([g0, g1], axis=1).reshape(CMP_STRIDE * LANES, 2 * CMP_HIDDEN)
        px = jnp.concatenate([pos_half, pos_half], axis=-1).reshape(1, CMP_STRIDE * LANES)
        return wx.astype(BF16), px.astype(F32)

    half = CMP_STRIDE * N_DIM
    wa, pa = expand(w1[:half], pos[:CMP_STRIDE])
    wb, pb = expand(w1[half:], pos[CMP_STRIDE:])
    z2 = jnp.zeros_like(w2)
    w2x = jnp.concatenate([jnp.concatenate([w2, z2], axis=1), jnp.concatenate([z2, w2], axis=1)], axis=0)
    return pa, pb, wa, wb, w2x.astype(BF16)


def _nsa_attn_kernel(qt_ref, gtt_ref, z_ref, ka_ref, vst_ref,
                     kw0, kw1, kw2, kw3, kw4, wt0, wt1, wt2, wt3, wt4,
                     kcd_ref, vct_ref, ovt_ref, out_ref, qa_ref, acc_ref, *, ncmp):
    QB = Q_BLOCK
    NP = N_HEADS // 2
    PW = 2 * QB
    i = pl.program_id(1)
    t0 = i * QB
    kw_refs = (kw0, kw1, kw2, kw3, kw4)
    vwt_refs = (wt0, wt1, wt2, wt3, wt4)
    nwt = len(kw_refs)

    sub = lax.broadcasted_iota(jnp.int32, (QB, QB), 0)
    lane = lax.broadcasted_iota(jnp.int32, (QB, QB), 1)
    top = sub < N_DIM
    t_row = t0 + lax.broadcasted_iota(jnp.int32, (1, QB), 1)
    t_row2 = jnp.concatenate([t_row, t_row], axis=1)

    def pair_q(pr):
        slab = qt_ref[pr * LANES:(pr + 1) * LANES, :]
        zero = jnp.zeros_like(slab)
        return jnp.concatenate([jnp.where(top, slab, zero), jnp.where(top, zero, slab)], axis=1)

    nidx = lax.broadcasted_iota(jnp.int32, (ncmp, PW), 0)
    cmp_ok = (nidx * CMP_STRIDE + (CMP_LEN - 1) <= t_row2) & (nidx < ncmp - 1)
    blk_f = sub.astype(F32)
    cur = jnp.right_shift(t_row, SEL_SHIFT)
    grp = lambda pr: pr // (N_HPG // 2)
    for pr in range(NP):
        qa_ref[pr, 0:LANES, :] = pair_q(pr)
    s_cmp = [_dot(kcd_ref[:, grp(pr) * LANES:(grp(pr) + 1) * LANES], qa_ref[pr, 0:LANES, :]) for pr in range(NP)]
    p_cmp = []
    for pr in range(NP):
        s = jnp.where(cmp_ok, s_cmp[pr], NEG)
        mx = jnp.max(s, axis=0, keepdims=True)
        p = jnp.where(cmp_ok, jnp.exp2(s - mx), 0.0)
        den = jnp.sum(p, axis=0, keepdims=True)
        p_cmp.append(p * (1.0 / jnp.maximum(den, 1e-30)))
    o_cmp = [_dot(vct_ref[0, grp(pr) * N_DIM:(grp(pr) + 1) * N_DIM, :], p_cmp[pr].astype(BF16))
             for pr in range(NP)]
    imps = []
    for g in range(N_GROUPS):
        psum = jnp.zeros((ncmp, QB), F32)
        for pp in range(N_HPG // 2):
            p = p_cmp[g * (N_HPG // 2) + pp]
            psum = psum + p[:, 0:QB] + p[:, QB:PW]
        p_hi = psum.astype(BF16)
        p_lo = (psum - p_hi.astype(F32)).astype(BF16)
        imps.append(_dot(ovt_ref[...], p_hi) + _dot(ovt_ref[...], p_lo))
    forced = (sub == 0) | (sub == cur) | (sub == cur - 1)
    works = [jnp.where(sub > cur, NEG, jnp.where(forced, -NEG, imp)) for imp in imps]
    sels = [jnp.zeros((QB, QB), jnp.bool_) for _ in range(N_GROUPS)]
    for _ in range(SEL_TOPK):
        for g in range(N_GROUPS):
            mx = jnp.max(works[g], axis=0, keepdims=True)
            idx = jnp.min(jnp.where(works[g] == mx, blk_f, float(LANES)), axis=0, keepdims=True)
            pick = blk_f == idx
            sels[g] = sels[g] | pick
            works[g] = jnp.where(pick, -jnp.inf, works[g])
    for g in range(N_GROUPS):
        neg = jnp.where(sels[g] & (sub <= cur), 0.0, NEG).astype(BF16)
        neg2 = jnp.concatenate([neg, neg], axis=1)
        for pp in range(N_HPG // 2):
            qa_ref[g * (N_HPG // 2) + pp, LANES:2 * LANES, :] = neg2

    CH = QB
    above = jnp.concatenate([sub > lane, sub > lane], axis=1)

    def sel_scores(cj):
        k0 = pl.multiple_of(cj * CH, CH)
        return [_dot(ka_ref[pl.ds(k0, CH), 2 * grp(pr) * LANES:2 * (grp(pr) + 1) * LANES], qa_ref[pr])
                for pr in range(NP)]

    def sel_softmax(ss, ms, masked):
        new, alphas, ps = [], [], []
        for pr in range(NP):
            s = jnp.where(above, NEG, ss[pr]) if masked else ss[pr]
            m_new = jnp.maximum(ms[pr], jnp.max(s, axis=0, keepdims=True))
            alphas.append(jnp.exp2(ms[pr] - m_new))
            ps.append(jnp.exp2(s - m_new).astype(BF16))
            new.append(m_new)
        return tuple(new), alphas, ps

    def sel_values(cj, alphas, ps):
        for pr in range(NP):
            g = grp(pr)
            acc_ref[pr] = alphas[pr] * acc_ref[pr] + _dot(vst_ref[cj, g * V_ROWS:(g + 1) * V_ROWS, :], ps[pr])

    def sel_chunk(cj, ms, masked):
        ms, alphas, ps = sel_softmax(sel_scores(cj), ms, masked)
        sel_values(cj, alphas, ps)
        return ms

    def sel_tile(jt, ms):
        nc = KV_TILE // CH
        k0 = pl.multiple_of(jt * KV_TILE, KV_TILE)
        big = [_dot(ka_ref[pl.ds(k0, KV_TILE), 2 * grp(pr) * LANES:2 * (grp(pr) + 1) * LANES], qa_ref[pr])
               for pr in range(NP)]
        for c in range(nc):
            ss = [b[c * CH:(c + 1) * CH] for b in big]
            ms, alphas, ps = sel_softmax(ss, ms, False)
            sel_values(jt * nc + c, alphas, ps)
        return ms

    acc_ref[...] = jnp.zeros_like(acc_ref)
    ms = tuple(jnp.full((1, PW), 2.0 * NEG, F32) for _ in range(NP))
    n_tile = i // (KV_TILE // CH)
    ms = lax.fori_loop(0, n_tile, sel_tile, ms)
    ms = lax.fori_loop(n_tile * (KV_TILE // CH), i, lambda cj, ms: sel_chunk(cj, ms, False), ms)
    sel_chunk(i, ms, True)
    o_sel = [acc_ref[pr, 0:N_DIM, :] * (1.0 / acc_ref[pr, N_DIM:N_DIM + 1, :]) for pr in range(NP)]

    s_win = [[_dot(kw_refs[c][:, grp(pr) * LANES:(grp(pr) + 1) * LANES], qa_ref[pr, 0:LANES, :])
              for c in range(nwt)] for pr in range(NP)]
    p_win, d_win = [], []
    for pr in range(NP):
        ss = []
        for c in range(nwt):
            ok = (i - (nwt - 1) + c) >= 0
            if c == 0:
                ok = ok & above
            elif c == nwt - 1:
                ok = ok & jnp.logical_not(above)
            ss.append(jnp.where(ok, s_win[pr][c], NEG))
        mx = ss[0].max(axis=0, keepdims=True)
        for c in range(1, nwt):
            mx = jnp.maximum(mx, ss[c].max(axis=0, keepdims=True))
        ps = [jnp.exp2(ss[c] - mx) for c in range(nwt)]
        den = ps[0].sum(axis=0, keepdims=True)
        for c in range(1, nwt):
            den = den + ps[c].sum(axis=0, keepdims=True)
        p_win.append([p.astype(BF16) for p in ps])
        d_win.append(den)
    o_win = []
    for pr in range(NP):
        g = grp(pr)
        acc = _dot(vwt_refs[0][g * N_DIM:(g + 1) * N_DIM, :], p_win[pr][0])
        for c in range(1, nwt):
            acc = acc + _dot(vwt_refs[c][g * N_DIM:(g + 1) * N_DIM, :], p_win[pr][c])
        o_win.append(acc * (1.0 / d_win[pr]))

    for pr in range(NP):
        halves = []
        for par in range(2):
            h = 2 * pr + par
            cs = slice(par * QB, (par + 1) * QB)
            gc = gtt_ref[SM_G + 3 * h:SM_G + 3 * h + 1, :]
            gs = gtt_ref[SM_G + 3 * h + 1:SM_G + 3 * h + 2, :]
            gw = gtt_ref[SM_G + 3 * h + 2:SM_G + 3 * h + 3, :]
            halves.append(gc * o_cmp[pr][:, cs] + gs * o_sel[pr][:, cs] + gw * o_win[pr][:, cs])
        y = jnp.concatenate(halves, axis=0).T
        sl = slice(pr * LANES, (pr + 1) * LANES)
        out_ref[:, sl] = (y * _silu(z_ref[:, sl])).astype(out_ref.dtype)


def _nsa_attn(proj, qt, gtt, ka, vst, kwd, vwt, kcd, vct, ovt, batch, seq):
    QB = Q_BLOCK
    nqb = seq // QB
    ncmp = seq // CMP_STRIDE
    ntile = seq // KV_TILE
    nwt = WINDOW // QB + 1
    row = lambda w: pl.BlockSpec((QB, w), lambda b, i: (b * nqb + i, 0))
    colb = lambda r: pl.BlockSpec((r, QB), lambda b, i: (0, b * nqb + i))
    wblk = lambda b, i, c: b * nqb + jnp.maximum(i - (nwt - 1) + c, 0)
    in_specs = ([colb(N_W), colb(LANES), pl.BlockSpec((QB, N_W), lambda b, i: (b * nqb + i, C_NZ // N_W)),
                 pl.BlockSpec((seq, 4 * LANES), lambda b, i: (b, 0)),
                 pl.BlockSpec((nqb, N_GROUPS * V_ROWS, QB), lambda b, i: (b, 0, 0))]
                + [pl.BlockSpec((QB, 2 * LANES), lambda b, i, c=c: (wblk(b, i, c), 0)) for c in range(nwt)]
                + [pl.BlockSpec((LANES, QB), lambda b, i, c=c: (0, wblk(b, i, c))) for c in range(nwt)]
                + [pl.BlockSpec((ncmp, 2 * LANES), lambda b, i: (b, 0)),
                   pl.BlockSpec((1, LANES, ncmp), lambda b, i: (b, 0, 0)),
                   pl.BlockSpec((LANES, ncmp), lambda b, i: (0, 0))])
    return pl.pallas_call(
        functools.partial(_nsa_attn_kernel, ncmp=ncmp),
        grid=(batch, nqb),
        in_specs=in_specs,
        out_specs=row(N_W),
        out_shape=jax.ShapeDtypeStruct((batch * seq, N_W), BF16),
        scratch_shapes=[pltpu.VMEM((N_HEADS // 2, 2 * LANES, 2 * QB), BF16),
                        pltpu.VMEM((N_HEADS // 2, V_ROWS, 2 * QB), F32)],
        compiler_params=_cparams("parallel", "arbitrary"),
        name="nsa_attn",
    )(qt, gtt, proj, ka, vst, *([kwd] * nwt), *([vwt] * nwt), kcd, vct, ovt)


def _hgrn_kernel(f_ref, i_ref, q_ref, z_ref, lb_ref, hn_ref, out_ref, st_ref):
    T, D = H_TILE, H_DIM
    W = H_W

    @pl.when(pl.program_id(1) == 0)
    def _():
        st_ref[...] = jnp.zeros_like(st_ref)

    lb = lb_ref[...]
    fp = f_ref[...]
    e = jnp.exp(-jnp.abs(fp))
    r = 1.0 / (1.0 + e)
    sig = jnp.where(fp >= 0, r, e * r)
    nsig = jnp.where(fp >= 0, e * r, r)
    logf2 = jnp.log(lb + (1.0 - lb) * sig) * LOG2E
    kk = (1.0 - lb) * nsig

    row = lax.broadcasted_iota(jnp.int32, (T, T), 0)
    col = lax.broadcasted_iota(jnp.int32, (T, T), 1)
    tri01 = jnp.where(col <= row, 1.0, 0.0).astype(BF16)
    bcum = _dot01(tri01, logf2)
    rowi = lax.broadcasted_iota(jnp.int32, (T, 1), 0)

    segs = []
    sg = 2 * H_LAG
    while sg <= T:
        upper = jnp.bitwise_and(rowi, sg - 1) >= sg // 2
        if sg == T:
            bmid = jnp.broadcast_to(bcum[sg // 2 - 1:sg // 2, :], (T, W))
            same = None
        else:
            b3 = bcum.reshape(T // sg, sg, W)
            bmid = jnp.broadcast_to(b3[:, sg // 2 - 1:sg // 2, :], (T // sg, sg, W)).reshape(T, W)
            shift = sg.bit_length() - 1
            same = jnp.right_shift(row, shift) == jnp.right_shift(col, shift)
        segs.append((upper, bmid, same))
        sg *= 2

    sub3 = lax.broadcasted_iota(jnp.int32, (T // 8, 8, 1), 1)
    lag_ok = [jnp.bitwise_and(sub3, H_LAG - 1) >= d for d in range(H_LAG)]
    b_end = bcum[T - 1:T, :]
    v_all = i_ref[...]
    for h in range(H_HEADS):
        hs = slice(h * D, (h + 1) * D)
        q = _silu(q_ref[:, hs])
        k = kk[:, hs]
        b = bcum[:, hs]
        v = v_all[:, hs]
        vb = v.astype(BF16)
        a_seg = []
        for upper, bmid, same in segs:
            bm = bmid[:, hs]
            qm = (q * jnp.exp2(jnp.where(upper, b - bm, NEG))).astype(BF16)
            km = (k * jnp.exp2(jnp.where(upper, NEG, bm - b))).astype(BF16)
            a_seg.append(_dot_nt(qm, km))
        st = st_ref[h]
        o_state = _dot_nt((q * jnp.exp2(b)).astype(BF16), st.astype(BF16))
        kt = (k * jnp.exp2(b_end[:, hs] - b)).astype(BF16)
        st_ref[h] = st * jnp.exp2(b_end[:, hs]) + _dot_tn(vb, kt)
        q3, k3, b3, v3 = (a.reshape(T // 8, 8, D) for a in (q, k, b, v))
        o3 = jnp.sum(q3 * k3, axis=2, keepdims=True) * v3
        for d in range(1, H_LAG):
            ks, bs, vs = (pltpu.roll(a, d, 1) for a in (k3, b3, v3))
            a = jnp.sum(q3 * ks * jnp.exp2(b3 - bs), axis=2, keepdims=True)
            o3 = o3 + jnp.where(lag_ok[d], a, 0.0) * vs
        a_tot = None
        for (upper, bmid, same), a in zip(segs, a_seg):
            a = a if same is None else jnp.where(same, a, 0.0)
            a_tot = a if a_tot is None else a_tot + a
        oh = o3.reshape(T, D) + o_state + _dot(a_tot.astype(BF16), vb)
        on = oh * lax.rsqrt(jnp.mean(oh * oh, axis=1, keepdims=True) + EPS) * hn_ref[:, hs]
        out_ref[:, hs] = (on * _silu(z_ref[:, hs])).astype(out_ref.dtype)


def _hgrn(proj, lb_row, hn_row, batch, seq):
    T = H_TILE
    nt = seq // T
    cb = lambda j: pl.BlockSpec((T, H_W), lambda b, c, j=j: (b * nt + c, j))
    vec = pl.BlockSpec((1, H_W), lambda b, c: (0, 0))
    return pl.pallas_call(
        _hgrn_kernel,
        grid=(batch, nt),
        in_specs=[cb(0), cb(1), cb(2), cb(3), vec, vec],
        out_specs=pl.BlockSpec((T, H_W), lambda b, c: (b * nt + c, 0)),
        out_shape=jax.ShapeDtypeStruct((batch * seq, H_W), BF16),
        scratch_shapes=[pltpu.VMEM((H_HEADS, H_DIM, H_DIM), F32)],
        compiler_params=_cparams("parallel", "arbitrary"),
        name="hgrn2",
    )(proj, proj, proj, proj, lb_row, hn_row)


def _even_relayout(a):
    nb = 5 * M_W + 2 * M_HEADS
    g0 = nb + N_W + 6 * N_KV_W
    col = lambda lo, n: lax.slice_in_dim(a, lo, lo + n, axis=-1)
    pad = jnp.zeros(a.shape[:-1] + (C_NZ - C_SMALL - SM_G - 3 * N_HEADS,), a.dtype)
    out = jnp.concatenate([
        col(0, 5 * M_W),
        col(nb, N_W + 6 * N_KV_W),
        col(5 * M_W, 2 * M_HEADS),
        col(g0, 3 * N_HEADS), pad,
        col(g0 + 3 * N_HEADS, N_W),
    ], axis=-1)
    assert out.shape[-1] == EV_COLS
    return out


def _rope_tables(positions):
    inv_freq = ROPE_THETA ** (-np.arange(ROPE_HALF, dtype=np.float64) / ROPE_HALF)
    ang = positions.astype(np.float64)[:, None] * inv_freq[None, :]
    cos, sin = np.cos(ang).astype(np.float32), np.sin(ang).astype(np.float32)
    n = positions.shape[0]
    z8 = np.zeros((n, ROPE_HALF), np.float32)
    rest1 = np.ones((n, N_DIM - ROPE_DIM), np.float32)
    rest0 = np.zeros((n, N_DIM - ROPE_DIM), np.float32)
    cs = np.concatenate([cos, cos, rest1], axis=1)
    s1 = np.concatenate([z8, sin, rest0], axis=1)
    s2 = np.concatenate([-sin, z8, rest0], axis=1)
    two = lambda a: np.concatenate([a, a], axis=1)
    return two(cs), two(s1), two(s2)


def _overlap_matrix(seq):
    ncmp = seq // CMP_STRIDE
    n_sel = seq // SEL_BLOCK
    cs = np.arange(ncmp) * CMP_STRIDE
    ss = np.arange(n_sel) * SEL_BLOCK
    ov = np.clip(np.minimum(cs[:, None] + CMP_LEN, ss[None, :] + SEL_BLOCK)
                 - np.maximum(cs[:, None], ss[None, :]), 0, None).astype(np.float32)
    ov[ncmp - 1, :] = 0.0
    return np.pad(ov, ((0, 0), (0, LANES - n_sel)))


def _even_layer(x2d, norm_w, w_in, b_in, f_bias, conv_w, conv_b, head_norm, q_norm, k_norm,
                ck_pos, ck_w1, ck_w2, cv_pos, cv_w1, cv_w2, w_out, batch, seq):
    assert seq // SEL_BLOCK <= LANES and seq % KV_TILE == 0
    proj = _norm_proj(x2d, norm_w, _even_relayout(w_in).astype(BF16), _even_relayout(b_in))

    f_bias_row = jnp.zeros((1, LANES), F32).at[0, SM_F:SM_F + M_HEADS].set(f_bias)
    ya = _mlstm(proj, conv_w, conv_b, f_bias_row, head_norm, batch, seq)

    cs, s1, s2 = (jnp.asarray(a) for a in _rope_tables(np.arange(seq)))
    qn_row = jnp.concatenate([q_norm, q_norm]).reshape(1, LANES)
    kn_rows = jnp.concatenate([k_norm, k_norm], axis=1)
    qt, ka, kwd, vst, vwt, kc, vc, gtt = _nsa_prep(proj, cs, s1, s2, qn_row, kn_rows, batch, seq)

    ncmp = seq // CMP_STRIDE
    ccs, cs1, cs2 = (jnp.asarray(a) for a in _rope_tables(np.arange(ncmp) * CMP_STRIDE + CMP_LEN - 1))
    kcd, vct = _nsa_compress(
        kc.reshape(batch * ncmp, CMP_STRIDE * LANES), vc.reshape(batch * ncmp, CMP_STRIDE * LANES),
        _compress_params(ck_pos, ck_w1, ck_w2), _compress_params(cv_pos, cv_w1, cv_w2),
        ccs, cs1, cs2, kn_rows, batch, ncmp)

    ovt = jnp.asarray(_overlap_matrix(seq).T).astype(BF16)
    yb = _nsa_attn(proj, qt, gtt, ka, vst, kwd, vwt, kcd, vct, ovt, batch, seq)

    w_o = w_out.astype(BF16)
    return _out_proj2(ya, yb, w_o[:M_W], w_o[M_W:], x2d)


def _odd_layer(x2d, norm_w, w_in, b_in, lb, head_norm, w_out, batch, seq):
    proj = _norm_proj(x2d, norm_w, w_in.astype(BF16), b_in)
    y = _hgrn(proj, lb.reshape(1, H_W), head_norm.reshape(1, H_W), batch, seq)
    return _out_proj1(y, w_out.astype(BF16), x2d)


def kernel(x, norm_w, ev_w_in, ev_b_in, mlstm_f_bias, mlstm_conv_w, mlstm_conv_b, mlstm_head_norm, nsa_q_norm, nsa_k_norm, cmp_k_pos, cmp_k_w1, cmp_k_w2, cmp_v_pos, cmp_v_w1, cmp_v_w2, ev_w_out, od_w_in, od_b_in, hgrn_lb_logits, hgrn_head_norm, od_w_out):
    batch, seq, d = x.shape
    depth = norm_w.shape[0]
    lbs = jnp.cumsum(jax.nn.softmax(hgrn_lb_logits.astype(F32), axis=0), axis=0)
    x2d = x.reshape(batch * seq, d)
    for layer in range(depth):
        j = layer // 2
        if layer % 2 == 0:
            x2d = _even_layer(x2d, norm_w[layer], ev_w_in[j], ev_b_in[j], mlstm_f_bias[j], mlstm_conv_w[j],
                              mlstm_conv_b[j], mlstm_head_norm[j], nsa_q_norm[j], nsa_k_norm[j], cmp_k_pos[j],
                              cmp_k_w1[j], cmp_k_w2[j], cmp_v_pos[j], cmp_v_w1[j], cmp_v_w2[j], ev_w_out[j],
                              batch, seq)
        else:
            x2d = _odd_layer(x2d, norm_w[layer], od_w_in[j], od_b_in[j], lbs[j], hgrn_head_norm[j], od_w_out[j],
                             batch, seq)
    return x2d.reshape(batch, seq, d)
```

```python
import functools

import numpy as np
import jax
import jax.numpy as jnp
from jax import lax
from jax.experimental import pallas as pl
from jax.experimental.pallas import tpu as pltpu

F32 = jnp.float32
BF16 = jnp.bfloat16

LANES = 128
EPS = 1e-6
NEG = -1e30
VMEM_LIMIT_BYTES = 56 * 1024 * 1024

M_HEADS = 4
M_DIM = 128
M_W = M_HEADS * M_DIM
M_CHUNK = 128
CONV_K = 4
N_HEADS = 8
N_GROUPS = 2
N_HPG = N_HEADS // N_GROUPS
N_DIM = 64
N_W = N_HEADS * N_DIM
N_KV_W = N_GROUPS * N_DIM
CMP_LEN = 32
CMP_STRIDE = 16
CMP_HIDDEN = 128
SEL_BLOCK = 64
SEL_SHIFT = 6
SEL_TOPK = 16
WINDOW = 512
Q_BLOCK = 128
ROPE_THETA = 500000.0
ROPE_DIM = N_DIM // 4
ROPE_HALF = ROPE_DIM // 2
KV_TILE = 512
V_ROWS = 80
LOG2E = 1.4426950408889634
H_HEADS = 8
H_DIM = 128
H_W = H_HEADS * H_DIM
H_TILE = 128
H_LAG = 4

C_MQ, C_MK, C_MV, C_MO, C_MZ = 0, 512, 1024, 1536, 2048
C_NQ = 2560
C_KC, C_VC, C_KS, C_VS = 3072, 3200, 3328, 3456
C_KW, C_VW, C_SMALL = 3584, 3712, 3840
C_NZ = 4096
EV_COLS = 4608
SM_I, SM_F, SM_G = 0, 4, 8


def _cparams(*sem):
    return pltpu.CompilerParams(dimension_semantics=sem, vmem_limit_bytes=VMEM_LIMIT_BYTES)


def _dot(a, b):
    return jnp.dot(a, b, preferred_element_type=F32)


def _dot_nt(a, b):
    return lax.dot_general(a, b, (((1,), (1,)), ((), ())), preferred_element_type=F32)


def _dot_tn(a, b):
    return lax.dot_general(a, b, (((0,), (0,)), ((), ())), preferred_element_type=F32)


def _sigmoid(x):
    e = jnp.exp(-jnp.abs(x))
    r = 1.0 / (1.0 + e)
    return jnp.where(x >= 0, r, e * r)


def _silu(x):
    return x * _sigmoid(x)


def _split3(x):
    hi = x.astype(BF16)
    r = x - hi.astype(F32)
    mid = r.astype(BF16)
    lo = (r - mid.astype(F32)).astype(BF16)
    return hi, mid, lo


def _dot01(m01, x):
    hi, mid, lo = _split3(x)
    return _dot(m01, hi) + _dot(m01, mid) + _dot(m01, lo)


def _norm_proj_kernel(x_ref, nw_ref, w_ref, b_ref, o_ref, g_ref, h_ref, *, g_tiles, g_off):
    j = pl.program_id(1)

    @pl.when(j == 0)
    def _():
        x = x_ref[...]
        ms = jnp.mean(x * x, axis=-1, keepdims=True)
        h_ref[...] = (x * lax.rsqrt(ms + EPS) * nw_ref[...]).astype(BF16)

    acc = _dot(h_ref[...], w_ref[...]) + b_ref[...]
    o_ref[...] = acc.astype(o_ref.dtype)

    @pl.when((j >= g_tiles[0]) & (j < g_tiles[1]))
    def _():
        g_ref[...] = acc[:, g_off:g_off + g_ref.shape[1]]


def _norm_proj(x2d, norm_w, w_bf16, bias, g_cols, tm=1024, tn=512):
    t, d = x2d.shape
    n = w_bf16.shape[1]
    g_w = min(g_cols[1] - g_cols[0], tn)
    g_tiles = (g_cols[0] // tn, -(-g_cols[1] // tn))
    return pl.pallas_call(
        functools.partial(_norm_proj_kernel, g_tiles=g_tiles, g_off=g_cols[0] % tn),
        grid=(t // tm, n // tn),
        in_specs=[
            pl.BlockSpec((tm, d), lambda i, j: (i, 0)),
            pl.BlockSpec((1, d), lambda i, j: (0, 0)),
            pl.BlockSpec((d, tn), lambda i, j: (0, j)),
            pl.BlockSpec((1, tn), lambda i, j: (0, j)),
        ],
        out_specs=[pl.BlockSpec((tm, tn), lambda i, j: (i, j)),
                   pl.BlockSpec((tm, g_w), lambda i, j: (i, jnp.clip(j - g_tiles[0], 0, g_tiles[1] - g_tiles[0] - 1)))],
        out_shape=[jax.ShapeDtypeStruct((t, n), BF16), jax.ShapeDtypeStruct((t, g_cols[1] - g_cols[0]), F32)],
        scratch_shapes=[pltpu.VMEM((tm, d), BF16)],
        compiler_params=_cparams("parallel", "arbitrary"),
        name="norm_proj",
    )(x2d, norm_w.reshape(1, d), w_bf16, bias.reshape(1, n))


def _out_proj2_kernel(ya_ref, yb_ref, wa_ref, wb_ref, x_ref, o_ref):
    o_ref[...] = x_ref[...] + _dot(ya_ref[...], wa_ref[...]) + _dot(yb_ref[...], wb_ref[...])


def _out_proj2(ya, yb, wa, wb, x2d, tm=1024):
    t, d = x2d.shape
    ka, kb = ya.shape[1], yb.shape[1]
    return pl.pallas_call(
        _out_proj2_kernel,
        grid=(t // tm,),
        in_specs=[
            pl.BlockSpec((tm, ka), lambda i: (i, 0)),
            pl.BlockSpec((tm, kb), lambda i: (i, 0)),
            pl.BlockSpec((ka, d), lambda i: (0, 0)),
            pl.BlockSpec((kb, d), lambda i: (0, 0)),
            pl.BlockSpec((tm, d), lambda i: (i, 0)),
        ],
        out_specs=pl.BlockSpec((tm, d), lambda i: (i, 0)),
        out_shape=jax.ShapeDtypeStruct((t, d), F32),
        compiler_params=_cparams("parallel"),
        name="out_proj2",
    )(ya, yb, wa, wb, x2d)


def _out_proj1_kernel(y_ref, w_ref, x_ref, o_ref):
    o_ref[...] = x_ref[...] + _dot(y_ref[...], w_ref[...])


def _out_proj1(y, w, x2d, tm=1024):
    t, d = x2d.shape
    k = y.shape[1]
    return pl.pallas_call(
        _out_proj1_kernel,
        grid=(t // tm,),
        in_specs=[
            pl.BlockSpec((tm, k), lambda i: (i, 0)),
            pl.BlockSpec((k, d), lambda i: (0, 0)),
            pl.BlockSpec((tm, d), lambda i: (i, 0)),
        ],
        out_specs=pl.BlockSpec((tm, d), lambda i: (i, 0)),
        out_shape=jax.ShapeDtypeStruct((t, d), F32),
        compiler_params=_cparams("parallel"),
        name="out_proj1",
    )(y, w, x2d)


def _mlstm_kernel(q_ref, k_ref, v_ref, o_ref, z_ref, sm_ref, cw_ref, cb_ref, fb_ref, hn_ref,
                  out_ref, xbuf, c_ref, n_ref, m_ref):
    L, D = M_CHUNK, M_DIM

    @pl.when(pl.program_id(1) == 0)
    def _():
        xbuf[0:8, :] = jnp.zeros((8, 2 * M_W), F32)
        c_ref[...] = jnp.zeros_like(c_ref)
        n_ref[...] = jnp.zeros_like(n_ref)
        m_ref[...] = jnp.zeros_like(m_ref)

    xbuf[8:8 + L, 0:M_W] = q_ref[...].astype(F32)
    xbuf[8:8 + L, M_W:2 * M_W] = k_ref[...].astype(F32)
    acc = jnp.broadcast_to(cb_ref[...], (L, 2 * M_W))
    for j in range(CONV_K):
        acc = acc + cw_ref[j:j + 1, :] * xbuf[8 - (CONV_K - 1) + j:8 - (CONV_K - 1) + j + L, :]
    qk = _silu(acc)
    xbuf[0:8, :] = xbuf[L:L + 8, :]

    sm = sm_ref[...]
    row = lax.broadcasted_iota(jnp.int32, (L, L), 0)
    col = lax.broadcasted_iota(jnp.int32, (L, L), 1)
    eye = (row == col).astype(F32)
    tri = (col <= row).astype(F32)
    tri_t = (row <= col).astype(F32)
    causal = col <= row

    heads = []
    for h in range(M_HEADS):
        hs = slice(h * D, (h + 1) * D)
        q = qk[:, hs]
        k = qk[:, M_W + h * D:M_W + (h + 1) * D] * (D ** -0.5)
        i_col = sm[:, SM_I + h:SM_I + h + 1]
        fpre = sm[:, SM_F + h:SM_F + h + 1] + fb_ref[:, SM_F + h:SM_F + h + 1]
        f_col = jnp.minimum(fpre, 0.0) - jnp.log(1.0 + jnp.exp(-jnp.abs(fpre)))
        f_row = jnp.sum(eye * f_col, axis=0, keepdims=True)
        i_row = jnp.sum(eye * i_col, axis=0, keepdims=True)
        bcum_col = jnp.sum(tri * f_row, axis=1, keepdims=True)
        bcum_row = jnp.sum(tri_t * f_col, axis=0, keepdims=True)
        m_prev = m_ref[h]
        dmat = jnp.where(causal, bcum_col - bcum_row + i_row, -jnp.inf)
        inter = bcum_col + m_prev
        m_row = jnp.maximum(jnp.max(dmat, axis=1, keepdims=True), inter)
        b_last = bcum_col[L - 1:L, :]
        dl = b_last - bcum_col + i_col
        m_new = jnp.maximum(b_last + m_prev, jnp.max(dl, axis=0, keepdims=True))
        wkk = jnp.exp(dl - m_new) * k
        heads.append(dict(
            hs=hs, q=q, qb=q.astype(BF16), kb=k.astype(BF16), vb=v_ref[:, hs].astype(BF16), wkk=wkk,
            e=jnp.exp(dmat - m_row), w_inter=jnp.exp(inter - m_row), floor=jnp.exp(-m_row),
            decay=jnp.exp(b_last + m_prev - m_new), m_new=m_new))
    s_qk = [_dot_nt(t["qb"], t["kb"]) for t in heads]
    s_qc = [_dot(t["qb"], c_ref[h].astype(BF16)) for h, t in enumerate(heads)]
    s_kv = [_dot_tn(t["wkk"].astype(BF16), t["vb"]) for t in heads]
    smats = [s_qk[h] * t["e"] for h, t in enumerate(heads)]
    s_sv = [_dot(smats[h].astype(BF16), t["vb"]) for h, t in enumerate(heads)]
    for h, t in enumerate(heads):
        hs = t["hs"]
        num = s_sv[h] + t["w_inter"] * s_qc[h]
        den = (jnp.sum(smats[h], axis=1, keepdims=True)
               + t["w_inter"] * jnp.sum(t["q"] * n_ref[h], axis=1, keepdims=True))
        hh = num / jnp.maximum(jnp.abs(den), t["floor"])
        c_ref[h] = t["decay"] * c_ref[h] + s_kv[h]
        n_ref[h] = t["decay"] * n_ref[h] + jnp.sum(t["wkk"], axis=0, keepdims=True)
        m_ref[h] = t["m_new"]
        hn = hh * lax.rsqrt(jnp.mean(hh * hh, axis=1, keepdims=True) + EPS) * hn_ref[:, hs]
        gate = _sigmoid(o_ref[:, hs].astype(F32)) * _silu(z_ref[:, hs].astype(F32))
        out_ref[:, hs] = (hn * gate).astype(out_ref.dtype)


def _mlstm(proj, small, conv_w, conv_b, f_bias_row, head_norm, batch, seq):
    L = M_CHUNK
    nc = seq // L
    cb = lambda off: pl.BlockSpec((L, M_W), lambda b, c, off=off: (b * nc + c, off // M_W))
    return pl.pallas_call(
        _mlstm_kernel,
        grid=(batch, nc),
        in_specs=[
            cb(C_MQ), cb(C_MK), cb(C_MV), cb(C_MO), cb(C_MZ),
            pl.BlockSpec((L, LANES), lambda b, c: (b * nc + c, 0)),
            pl.BlockSpec((CONV_K, 2 * M_W), lambda b, c: (0, 0)),
            pl.BlockSpec((1, 2 * M_W), lambda b, c: (0, 0)),
            pl.BlockSpec((1, LANES), lambda b, c: (0, 0)),
            pl.BlockSpec((1, M_W), lambda b, c: (0, 0)),
        ],
        out_specs=pl.BlockSpec((L, M_W), lambda b, c: (b * nc + c, 0)),
        out_shape=jax.ShapeDtypeStruct((batch * seq, M_W), BF16),
        scratch_shapes=[
            pltpu.VMEM((L + 8, 2 * M_W), F32),
            pltpu.VMEM((M_HEADS, M_DIM, M_DIM), F32),
            pltpu.VMEM((M_HEADS, 1, M_DIM), F32),
            pltpu.VMEM((M_HEADS, 1, 1), F32),
        ],
        compiler_params=_cparams("parallel", "arbitrary"),
        name="mlstm",
    )(proj, proj, proj, proj, proj, small, conv_w, conv_b.reshape(1, -1), f_bias_row, head_norm.reshape(1, -1))


def _group_rms(x, w_row):
    lane = lax.broadcasted_iota(jnp.int32, x.shape, 1)
    left = lane < N_DIM
    sq = x * x
    tot = jnp.sum(sq, axis=1, keepdims=True)
    lft = jnp.sum(jnp.where(left, sq, 0.0), axis=1, keepdims=True)
    ms = jnp.where(left, lft, tot - lft) * (1.0 / N_DIM)
    return x * lax.rsqrt(ms + EPS) * w_row


def _rope(x, cs, s1, s2):
    return x * cs + pltpu.roll(x, ROPE_HALF, 1) * s1 + pltpu.roll(x, LANES - ROPE_HALF, 1) * s2


def _dup_halves(x):
    lane = lax.broadcasted_iota(jnp.int32, x.shape, 1)
    sw = pltpu.roll(x, N_DIM, 1)
    left = lane < N_DIM
    return jnp.where(left, x, sw), jnp.where(left, sw, x)


def _nsa_prep_kernel(q_ref, kv_ref, kw_ref, sm_ref, cs_ref, s1_ref, s2_ref, qn_ref, kn_ref,
                     qt_ref, ka_ref, kwd_ref, vst_ref, vwt_ref, kc_ref, vc_ref, gtt_ref,
                     *, tp, seq):
    cs, s1, s2 = cs_ref[...], s1_ref[...], s2_ref[...]
    for j in range(N_W // LANES):
        sl = slice(j * LANES, (j + 1) * LANES)
        qn = _group_rms(q_ref[:, sl].astype(F32), qn_ref[...])
        qt_ref[sl, :] = (_rope(qn, cs, s1, s2) * (N_DIM ** -0.5 * LOG2E)).T.astype(BF16)
    kv = kv_ref[...].astype(F32)
    kc_ref[...] = kv[:, 0:LANES]
    vc_ref[...] = kv[:, LANES:2 * LANES]
    ks = _rope(_group_rms(kv[:, 2 * LANES:3 * LANES], kn_ref[1:2, :]), cs, s1, s2)
    k0, k1 = _dup_halves(ks)
    pos0 = (pl.program_id(0) * tp) % seq
    pos = pos0 + lax.broadcasted_iota(jnp.int32, (tp, LANES), 0)
    blk = lax.broadcasted_iota(jnp.int32, (tp, LANES), 1)
    onehot = jnp.where(jnp.right_shift(pos, SEL_SHIFT) == blk, 1.0, 0.0).astype(BF16)
    ka_ref[:, 0:LANES] = k0.astype(BF16)
    ka_ref[:, LANES:2 * LANES] = onehot
    ka_ref[:, 2 * LANES:3 * LANES] = k1.astype(BF16)
    ka_ref[:, 3 * LANES:4 * LANES] = onehot
    vst = kv[:, 3 * LANES:4 * LANES].T
    fill = (lax.broadcasted_iota(jnp.int32, (V_ROWS - N_DIM, Q_BLOCK), 0) == 0).astype(BF16)
    for c in range(tp // Q_BLOCK):
        for g in range(N_GROUPS):
            vst_ref[c, g * V_ROWS:g * V_ROWS + N_DIM, :] = (
                vst[g * N_DIM:(g + 1) * N_DIM, c * Q_BLOCK:(c + 1) * Q_BLOCK].astype(BF16))
            vst_ref[c, g * V_ROWS + N_DIM:(g + 1) * V_ROWS, :] = fill
    kwb = kw_ref[...].astype(F32)
    kw = _rope(_group_rms(kwb[:, 0:LANES], kn_ref[2:3, :]), cs, s1, s2)
    w0, w1 = _dup_halves(kw)
    kwd_ref[:, 0:LANES] = w0.astype(BF16)
    kwd_ref[:, LANES:2 * LANES] = w1.astype(BF16)
    vwt_ref[...] = kwb[:, LANES:2 * LANES].T.astype(BF16)
    gtt_ref[...] = _sigmoid(sm_ref[...]).T


def _nsa_prep(proj, small, cs, s1, s2, qn_row, kn_rows, batch, seq):
    tp = KV_TILE
    t = batch * seq
    nsb = seq // tp
    rb = lambda w, off: pl.BlockSpec((tp, w), lambda i, off=off, w=w: (i, off // w))
    tab = pl.BlockSpec((tp, LANES), lambda i: (i % nsb, 0))
    o = lambda w: pl.BlockSpec((tp, w), lambda i: (i, 0))
    ot = lambda r: pl.BlockSpec((r, tp), lambda i: (0, i))
    sds = lambda w, dt: jax.ShapeDtypeStruct((t, w), dt)
    return pl.pallas_call(
        functools.partial(_nsa_prep_kernel, tp=tp, seq=seq),
        grid=(t // tp,),
        in_specs=[rb(512, C_NQ), rb(512, C_KC), rb(512, C_KW), o(LANES), tab, tab, tab,
                  pl.BlockSpec((1, LANES), lambda i: (0, 0)), pl.BlockSpec((3, LANES), lambda i: (0, 0))],
        out_specs=[ot(N_W), o(512), o(256),
                   pl.BlockSpec((tp // Q_BLOCK, N_GROUPS * V_ROWS, Q_BLOCK), lambda i: (i, 0, 0)),
                   ot(LANES), o(128), o(128), ot(LANES)],
        out_shape=[jax.ShapeDtypeStruct((N_W, t), BF16), sds(512, BF16), sds(256, BF16),
                   jax.ShapeDtypeStruct((t // Q_BLOCK, N_GROUPS * V_ROWS, Q_BLOCK), BF16),
                   jax.ShapeDtypeStruct((LANES, t), BF16),
                   sds(128, F32), sds(128, F32), jax.ShapeDtypeStruct((LANES, t), F32)],
        compiler_params=_cparams("parallel"),
        name="nsa_prep",
    )(proj, proj, proj, small, cs, s1, s2, qn_row, kn_rows)


def _gelu_tanh(x):
    return 0.5 * x * (1.0 + jnp.tanh(0.7978845608028654 * (x + 0.044715 * x * x * x)))


def _compress_mlp(x_ref, pa_ref, pb_ref, wa_ref, wb_ref, w2_ref, hb_ref):
    n = x_ref.shape[0]
    x = x_ref[...]
    ha = _dot((x + pa_ref[...]).astype(BF16), wa_ref[...])
    hb_ref[0:n, :] = _dot((x + pb_ref[...]).astype(BF16), wb_ref[...])
    hb_ref[n:n + 8, :] = jnp.zeros((8, hb_ref.shape[1]), F32)
    hid = ha + hb_ref[1:n + 1, :]
    return _dot(_gelu_tanh(hid).astype(BF16), w2_ref[...])


def _nsa_compress_kernel(kx_ref, vx_ref, kpa_ref, kpb_ref, vpa_ref, vpb_ref, kwa_ref, kwb_ref, kw2_ref,
                         vwa_ref, vwb_ref, vw2_ref, cs_ref, s1_ref, s2_ref, kn_ref,
                         kcd_ref, vct_ref, hb_ref):
    n = kx_ref.shape[0]
    rowi = lax.broadcasted_iota(jnp.int32, (n, LANES), 0)
    live = rowi < n - 1
    kc = _compress_mlp(kx_ref, kpa_ref, kpb_ref, kwa_ref, kwb_ref, kw2_ref, hb_ref)
    kc = _rope(_group_rms(kc, kn_ref[0:1, :]), cs_ref[...], s1_ref[...], s2_ref[...])
    kc = jnp.where(live, kc, 0.0)
    c0, c1 = _dup_halves(kc)
    kcd_ref[:, 0:LANES] = c0.astype(BF16)
    kcd_ref[:, LANES:2 * LANES] = c1.astype(BF16)
    vc = _compress_mlp(vx_ref, vpa_ref, vpb_ref, vwa_ref, vwb_ref, vw2_ref, hb_ref)
    vc = jnp.where(live, vc, 0.0)
    vct_ref[0] = vc.T.astype(BF16)


def _nsa_compress(kx, vx, kp, vp, cs, s1, s2, kn_rows, batch, nrow):
    wide = CMP_STRIDE * LANES
    full = lambda a: pl.BlockSpec(a.shape, lambda b: (0,) * a.ndim)
    xs = pl.BlockSpec((nrow, wide), lambda b: (b, 0))
    args = (kx, vx, kp[0], kp[1], vp[0], vp[1], kp[2], kp[3], kp[4], vp[2], vp[3], vp[4], cs, s1, s2, kn_rows)
    return pl.pallas_call(
        _nsa_compress_kernel,
        grid=(batch,),
        in_specs=[xs, xs] + [full(a) for a in args[2:]],
        out_specs=[pl.BlockSpec((nrow, 2 * LANES), lambda b: (b, 0)),
                   pl.BlockSpec((1, LANES, nrow), lambda b: (b, 0, 0))],
        out_shape=[jax.ShapeDtypeStruct((batch * nrow, 2 * LANES), BF16),
                   jax.ShapeDtypeStruct((batch, LANES, nrow), BF16)],
        scratch_shapes=[pltpu.VMEM((nrow + 8, 2 * CMP_HIDDEN), F32)],
        compiler_params=_cparams("parallel"),
        name="nsa_compress",
    )(*args)


def _compress_params(pos, w1, w2):
    def expand(w1_half, pos_half):
        w = w1_half.reshape(CMP_STRIDE, N_DIM, CMP_HIDDEN)
        z = jnp.zeros_like(w)
        g0 = jnp.concatenate([w, z], axis=-1)
        g1 = jnp.concatenate([z, w], axis=-1)
        wx = jnp.stack([g0, g1], axis=1).reshape(CMP_STRIDE * LANES, 2 * CMP_HIDDEN)
        px = jnp.concatenate([pos_half, pos_half], axis=-1).reshape(1, CMP_STRIDE * LANES)
        return wx.astype(BF16), px.astype(F32)

    half = CMP_STRIDE * N_DIM
    wa, pa = expand(w1[:half], pos[:CMP_STRIDE])
    wb, pb = expand(w1[half:], pos[CMP_STRIDE:])
    z2 = jnp.zeros_like(w2)
    w2x = jnp.concatenate([jnp.concatenate([w2, z2], axis=1), jnp.concatenate([z2, w2], axis=1)], axis=0)
    return pa, pb, wa, wb, w2x.astype(BF16)


def _nsa_attn_kernel(qt_ref, gtt_ref, z_ref, ka_ref, vst_ref,
                     kw0, kw1, kw2, kw3, kw4, wt0, wt1, wt2, wt3, wt4,
                     kcd_ref, vct_ref, ovt_ref, out_ref, qa_ref, acc_ref, *, ncmp):
    QB = Q_BLOCK
    NP = N_HEADS // 2
    PW = 2 * QB
    i = pl.program_id(1)
    t0 = i * QB
    kw_refs = (kw0, kw1, kw2, kw3, kw4)
    vwt_refs = (wt0, wt1, wt2, wt3, wt4)
    nwt = len(kw_refs)

    sub = lax.broadcasted_iota(jnp.int32, (QB, QB), 0)
    lane = lax.broadcasted_iota(jnp.int32, (QB, QB), 1)
    top = sub < N_DIM
    t_row = t0 + lax.broadcasted_iota(jnp.int32, (1, QB), 1)
    t_row2 = jnp.concatenate([t_row, t_row], axis=1)

    def pair_q(pr):
        slab = qt_ref[pr * LANES:(pr + 1) * LANES, :]
        zero = jnp.zeros_like(slab)
        return jnp.concatenate([jnp.where(top, slab, zero), jnp.where(top, zero, slab)], axis=1)

    nidx = lax.broadcasted_iota(jnp.int32, (ncmp, PW), 0)
    cmp_ok = (nidx * CMP_STRIDE + (CMP_LEN - 1) <= t_row2) & (nidx < ncmp - 1)
    blk_f = sub.astype(F32)
    cur = jnp.right_shift(t_row, SEL_SHIFT)
    grp = lambda pr: pr // (N_HPG // 2)
    for pr in range(NP):
        qa_ref[pr, 0:LANES, :] = pair_q(pr)
    s_cmp = [_dot(kcd_ref[:, grp(pr) * LANES:(grp(pr) + 1) * LANES], qa_ref[pr, 0:LANES, :]) for pr in range(NP)]
    p_cmp = []
    for pr in range(NP):
        s = jnp.where(cmp_ok, s_cmp[pr], NEG)
        mx = jnp.max(s, axis=0, keepdims=True)
        p = jnp.where(cmp_ok, jnp.exp2(s - mx), 0.0)
        den = jnp.sum(p, axis=0, keepdims=True)
        p_cmp.append(p * (1.0 / jnp.maximum(den, 1e-30)))
    o_cmp = [_dot(vct_ref[0, grp(pr) * N_DIM:(grp(pr) + 1) * N_DIM, :], p_cmp[pr].astype(BF16))
             for pr in range(NP)]
    imps = []
    for g in range(N_GROUPS):
        psum = jnp.zeros((ncmp, QB), F32)
        for pp in range(N_HPG // 2):
            p = p_cmp[g * (N_HPG // 2) + pp]
            psum = psum + p[:, 0:QB] + p[:, QB:PW]
        p_hi = psum.astype(BF16)
        p_lo = (psum - p_hi.astype(F32)).astype(BF16)
        imps.append(_dot(ovt_ref[...], p_hi) + _dot(ovt_ref[...], p_lo))
    forced = (sub == 0) | (sub == cur) | (sub == cur - 1)
    works = [jnp.where(sub > cur, NEG, jnp.where(forced, -NEG, imp)) for imp in imps]
    sels = [jnp.zeros((QB, QB), jnp.bool_) for _ in range(N_GROUPS)]
    for _ in range(SEL_TOPK):
        for g in range(N_GROUPS):
            mx = jnp.max(works[g], axis=0, keepdims=True)
            idx = jnp.min(jnp.where(works[g] == mx, blk_f, float(LANES)), axis=0, keepdims=True)
            pick = blk_f == idx
            sels[g] = sels[g] | pick
            works[g] = jnp.where(pick, -jnp.inf, works[g])
    for g in range(N_GROUPS):
        neg = jnp.where(sels[g] & (sub <= cur), 0.0, NEG).astype(BF16)
        neg2 = jnp.concatenate([neg, neg], axis=1)
        for pp in range(N_HPG // 2):
            qa_ref[g * (N_HPG // 2) + pp, LANES:2 * LANES, :] = neg2

    CH = QB
    above = jnp.concatenate([sub > lane, sub > lane], axis=1)

    def sel_scores(cj):
        k0 = pl.multiple_of(cj * CH, CH)
        return [_dot(ka_ref[pl.ds(k0, CH), 2 * grp(pr) * LANES:2 * (grp(pr) + 1) * LANES], qa_ref[pr])
                for pr in range(NP)]

    def sel_softmax(ss, ms, masked):
        new, alphas, ps = [], [], []
        for pr in range(NP):
            s = jnp.where(above, NEG, ss[pr]) if masked else ss[pr]
            m_new = jnp.maximum(ms[pr], jnp.max(s, axis=0, keepdims=True))
            alphas.append(jnp.exp2(ms[pr] - m_new))
            ps.append(jnp.exp2(s - m_new).astype(BF16))
            new.append(m_new)
        return tuple(new), alphas, ps

    def sel_values(cj, alphas, ps):
        for pr in range(NP):
            g = grp(pr)
            acc_ref[pr] = alphas[pr] * acc_ref[pr] + _dot(vst_ref[cj, g * V_ROWS:(g + 1) * V_ROWS, :], ps[pr])

    def sel_chunk(cj, ms, masked):
        ms, alphas, ps = sel_softmax(sel_scores(cj), ms, masked)
        sel_values(cj, alphas, ps)
        return ms

    def sel_tile(jt, ms):
        nc = KV_TILE // CH
        k0 = pl.multiple_of(jt * KV_TILE, KV_TILE)
        big = [_dot(ka_ref[pl.ds(k0, KV_TILE), 2 * grp(pr) * LANES:2 * (grp(pr) + 1) * LANES], qa_ref[pr])
               for pr in range(NP)]
        for c in range(nc):
            ss = [b[c * CH:(c + 1) * CH] for b in big]
            ms, alphas, ps = sel_softmax(ss, ms, False)
            sel_values(jt * nc + c, alphas, ps)
        return ms

    acc_ref[...] = jnp.zeros_like(acc_ref)
    ms = tuple(jnp.full((1, PW), 2.0 * NEG, F32) for _ in range(NP))
    n_tile = i // (KV_TILE // CH)
    ms = lax.fori_loop(0, n_tile, sel_tile, ms)
    ms = lax.fori_loop(n_tile * (KV_TILE // CH), i, lambda cj, ms: sel_chunk(cj, ms, False), ms)
    sel_chunk(i, ms, True)
    o_sel = [acc_ref[pr, 0:N_DIM, :] * (1.0 / acc_ref[pr, N_DIM:N_DIM + 1, :]) for pr in range(NP)]

    s_win = [[_dot(kw_refs[c][:, grp(pr) * LANES:(grp(pr) + 1) * LANES], qa_ref[pr, 0:LANES, :])
              for c in range(nwt)] for pr in range(NP)]
    p_win, d_win = [], []
    for pr in range(NP):
        ss = []
        for c in range(nwt):
            ok = (i - (nwt - 1) + c) >= 0
            if c == 0:
                ok = ok & above
            elif c == nwt - 1:
                ok = ok & jnp.logical_not(above)
            ss.append(jnp.where(ok, s_win[pr][c], NEG))
        mx = ss[0].max(axis=0, keepdims=True)
        for c in range(1, nwt):
            mx = jnp.maximum(mx, ss[c].max(axis=0, keepdims=True))
        ps = [jnp.exp2(ss[c] - mx) for c in range(nwt)]
        den = ps[0].sum(axis=0, keepdims=True)
        for c in range(1, nwt):
            den = den + ps[c].sum(axis=0, keepdims=True)
        p_win.append([p.astype(BF16) for p in ps])
        d_win.append(den)
    o_win = []
    for pr in range(NP):
        g = grp(pr)
        acc = _dot(vwt_refs[0][g * N_DIM:(g + 1) * N_DIM, :], p_win[pr][0])
        for c in range(1, nwt):
            acc = acc + _dot(vwt_refs[c][g * N_DIM:(g + 1) * N_DIM, :], p_win[pr][c])
        o_win.append(acc * (1.0 / d_win[pr]))

    for pr in range(NP):
        halves = []
        for par in range(2):
            h = 2 * pr + par
            cs = slice(par * QB, (par + 1) * QB)
            gc = gtt_ref[SM_G + 3 * h:SM_G + 3 * h + 1, :]
            gs = gtt_ref[SM_G + 3 * h + 1:SM_G + 3 * h + 2, :]
            gw = gtt_ref[SM_G + 3 * h + 2:SM_G + 3 * h + 3, :]
            halves.append(gc * o_cmp[pr][:, cs] + gs * o_sel[pr][:, cs] + gw * o_win[pr][:, cs])
        y = jnp.concatenate(halves, axis=0).T
        sl = slice(pr * LANES, (pr + 1) * LANES)
        out_ref[:, sl] = (y * _silu(z_ref[:, sl].astype(F32))).astype(out_ref.dtype)


def _nsa_attn(proj, qt, gtt, ka, vst, kwd, vwt, kcd, vct, ovt, batch, seq):
    QB = Q_BLOCK
    nqb = seq // QB
    ncmp = seq // CMP_STRIDE
    ntile = seq // KV_TILE
    nwt = WINDOW // QB + 1
    row = lambda w: pl.BlockSpec((QB, w), lambda b, i: (b * nqb + i, 0))
    colb = lambda r: pl.BlockSpec((r, QB), lambda b, i: (0, b * nqb + i))
    wblk = lambda b, i, c: b * nqb + jnp.maximum(i - (nwt - 1) + c, 0)
    in_specs = ([colb(N_W), colb(LANES), pl.BlockSpec((QB, N_W), lambda b, i: (b * nqb + i, C_NZ // N_W)),
                 pl.BlockSpec((seq, 4 * LANES), lambda b, i: (b, 0)),
                 pl.BlockSpec((nqb, N_GROUPS * V_ROWS, QB), lambda b, i: (b, 0, 0))]
                + [pl.BlockSpec((QB, 2 * LANES), lambda b, i, c=c: (wblk(b, i, c), 0)) for c in range(nwt)]
                + [pl.BlockSpec((LANES, QB), lambda b, i, c=c: (0, wblk(b, i, c))) for c in range(nwt)]
                + [pl.BlockSpec((ncmp, 2 * LANES), lambda b, i: (b, 0)),
                   pl.BlockSpec((1, LANES, ncmp), lambda b, i: (b, 0, 0)),
                   pl.BlockSpec((LANES, ncmp), lambda b, i: (0, 0))])
    return pl.pallas_call(
        functools.partial(_nsa_attn_kernel, ncmp=ncmp),
        grid=(batch, nqb),
        in_specs=in_specs,
        out_specs=row(N_W),
        out_shape=jax.ShapeDtypeStruct((batch * seq, N_W), BF16),
        scratch_shapes=[pltpu.VMEM((N_HEADS // 2, 2 * LANES, 2 * QB), BF16),
                        pltpu.VMEM((N_HEADS // 2, V_ROWS, 2 * QB), F32)],
        compiler_params=_cparams("parallel", "arbitrary"),
        name="nsa_attn",
    )(qt, gtt, proj, ka, vst, *([kwd] * nwt), *([vwt] * nwt), kcd, vct, ovt)


def _hgrn_kernel(f_ref, i_ref, q_ref, z_ref, lb_ref, hn_ref, out_ref, st_ref):
    T, D = H_TILE, H_DIM
    W = H_W

    @pl.when(pl.program_id(1) == 0)
    def _():
        st_ref[...] = jnp.zeros_like(st_ref)

    lb = lb_ref[...]
    fp = f_ref[...]
    e = jnp.exp(-jnp.abs(fp))
    r = 1.0 / (1.0 + e)
    sig = jnp.where(fp >= 0, r, e * r)
    nsig = jnp.where(fp >= 0, e * r, r)
    logf2 = jnp.log(lb + (1.0 - lb) * sig) * LOG2E
    kk = (1.0 - lb) * nsig

    row = lax.broadcasted_iota(jnp.int32, (T, T), 0)
    col = lax.broadcasted_iota(jnp.int32, (T, T), 1)
    tri01 = jnp.where(col <= row, 1.0, 0.0).astype(BF16)
    bcum = _dot01(tri01, logf2)
    rowi = lax.broadcasted_iota(jnp.int32, (T, 1), 0)

    segs = []
    sg = 2 * H_LAG
    while sg <= T:
        upper = jnp.bitwise_and(rowi, sg - 1) >= sg // 2
        if sg == T:
            bmid = jnp.broadcast_to(bcum[sg // 2 - 1:sg // 2, :], (T, W))
            same = None
        else:
            b3 = bcum.reshape(T // sg, sg, W)
            bmid = jnp.broadcast_to(b3[:, sg // 2 - 1:sg // 2, :], (T // sg, sg, W)).reshape(T, W)
            shift = sg.bit_length() - 1
            same = jnp.right_shift(row, shift) == jnp.right_shift(col, shift)
        segs.append((upper, bmid, same))
        sg *= 2

    sub3 = lax.broadcasted_iota(jnp.int32, (T // 8, 8, 1), 1)
    lag_ok = [jnp.bitwise_and(sub3, H_LAG - 1) >= d for d in range(H_LAG)]
    b_end = bcum[T - 1:T, :]
    v_all = i_ref[...].astype(F32)
    for h in range(H_HEADS):
        hs = slice(h * D, (h + 1) * D)
        q = _silu(q_ref[:, hs].astype(F32))
        k = kk[:, hs]
        b = bcum[:, hs]
        v = v_all[:, hs]
        vb = v.astype(BF16)
        a_seg = []
        for upper, bmid, same in segs:
            bm = bmid[:, hs]
            qm = (q * jnp.exp2(jnp.where(upper, b - bm, NEG))).astype(BF16)
            km = (k * jnp.exp2(jnp.where(upper, NEG, bm - b))).astype(BF16)
            a_seg.append(_dot_nt(qm, km))
        st = st_ref[h]
        o_state = _dot_nt((q * jnp.exp2(b)).astype(BF16), st.astype(BF16))
        kt = (k * jnp.exp2(b_end[:, hs] - b)).astype(BF16)
        st_ref[h] = st * jnp.exp2(b_end[:, hs]) + _dot_tn(vb, kt)
        q3, k3, b3, v3 = (a.reshape(T // 8, 8, D) for a in (q, k, b, v))
        o3 = jnp.sum(q3 * k3, axis=2, keepdims=True) * v3
        for d in range(1, H_LAG):
            ks, bs, vs = (pltpu.roll(a, d, 1) for a in (k3, b3, v3))
            a = jnp.sum(q3 * ks * jnp.exp2(b3 - bs), axis=2, keepdims=True)
            o3 = o3 + jnp.where(lag_ok[d], a, 0.0) * vs
        a_tot = None
        for (upper, bmid, same), a in zip(segs, a_seg):
            a = a if same is None else jnp.where(same, a, 0.0)
            a_tot = a if a_tot is None else a_tot + a
        oh = o3.reshape(T, D) + o_state + _dot(a_tot.astype(BF16), vb)
        on = oh * lax.rsqrt(jnp.mean(oh * oh, axis=1, keepdims=True) + EPS) * hn_ref[:, hs]
        out_ref[:, hs] = (on * _silu(z_ref[:, hs].astype(F32))).astype(out_ref.dtype)


def _hgrn(proj, fgate, lb_row, hn_row, batch, seq):
    T = H_TILE
    nt = seq // T
    cb = lambda j: pl.BlockSpec((T, H_W), lambda b, c, j=j: (b * nt + c, j))
    vec = pl.BlockSpec((1, H_W), lambda b, c: (0, 0))
    return pl.pallas_call(
        _hgrn_kernel,
        grid=(batch, nt),
        in_specs=[pl.BlockSpec((T, H_W), lambda b, c: (b * nt + c, 0)), cb(1), cb(2), cb(3), vec, vec],
        out_specs=pl.BlockSpec((T, H_W), lambda b, c: (b * nt + c, 0)),
        out_shape=jax.ShapeDtypeStruct((batch * seq, H_W), BF16),
        scratch_shapes=[pltpu.VMEM((H_HEADS, H_DIM, H_DIM), F32)],
        compiler_params=_cparams("parallel", "arbitrary"),
        name="hgrn2",
    )(fgate, proj, proj, proj, lb_row, hn_row)


def _even_relayout(a):
    nb = 5 * M_W + 2 * M_HEADS
    g0 = nb + N_W + 6 * N_KV_W
    col = lambda lo, n: lax.slice_in_dim(a, lo, lo + n, axis=-1)
    pad = jnp.zeros(a.shape[:-1] + (C_NZ - C_SMALL - SM_G - 3 * N_HEADS,), a.dtype)
    out = jnp.concatenate([
        col(0, 5 * M_W),
        col(nb, N_W + 6 * N_KV_W),
        col(5 * M_W, 2 * M_HEADS),
        col(g0, 3 * N_HEADS), pad,
        col(g0 + 3 * N_HEADS, N_W),
    ], axis=-1)
    assert out.shape[-1] == EV_COLS
    return out


def _rope_tables(positions):
    inv_freq = ROPE_THETA ** (-np.arange(ROPE_HALF, dtype=np.float64) / ROPE_HALF)
    ang = positions.astype(np.float64)[:, None] * inv_freq[None, :]
    cos, sin = np.cos(ang).astype(np.float32), np.sin(ang).astype(np.float32)
    n = positions.shape[0]
    z8 = np.zeros((n, ROPE_HALF), np.float32)
    rest1 = np.ones((n, N_DIM - ROPE_DIM), np.float32)
    rest0 = np.zeros((n, N_DIM - ROPE_DIM), np.float32)
    cs = np.concatenate([cos, cos, rest1], axis=1)
    s1 = np.concatenate([z8, sin, rest0], axis=1)
    s2 = np.concatenate([-sin, z8, rest0], axis=1)
    two = lambda a: np.concatenate([a, a], axis=1)
    return two(cs), two(s1), two(s2)


def _overlap_matrix(seq):
    ncmp = seq // CMP_STRIDE
    n_sel = seq // SEL_BLOCK
    cs = np.arange(ncmp) * CMP_STRIDE
    ss = np.arange(n_sel) * SEL_BLOCK
    ov = np.clip(np.minimum(cs[:, None] + CMP_LEN, ss[None, :] + SEL_BLOCK)
                 - np.maximum(cs[:, None], ss[None, :]), 0, None).astype(np.float32)
    ov[ncmp - 1, :] = 0.0
    return np.pad(ov, ((0, 0), (0, LANES - n_sel)))


def _even_layer(x2d, norm_w, w_in, b_in, f_bias, conv_w, conv_b, head_norm, q_norm, k_norm,
                ck_pos, ck_w1, ck_w2, cv_pos, cv_w1, cv_w2, w_out, batch, seq):
    assert seq // SEL_BLOCK <= LANES and seq % KV_TILE == 0
    proj, small = _norm_proj(x2d, norm_w, _even_relayout(w_in).astype(BF16), _even_relayout(b_in),
                             (C_SMALL, C_SMALL + LANES))

    f_bias_row = jnp.zeros((1, LANES), F32).at[0, SM_F:SM_F + M_HEADS].set(f_bias)
    ya = _mlstm(proj, small, conv_w, conv_b, f_bias_row, head_norm, batch, seq)

    cs, s1, s2 = (jnp.asarray(a) for a in _rope_tables(np.arange(seq)))
    qn_row = jnp.concatenate([q_norm, q_norm]).reshape(1, LANES)
    kn_rows = jnp.concatenate([k_norm, k_norm], axis=1)
    qt, ka, kwd, vst, vwt, kc, vc, gtt = _nsa_prep(proj, small, cs, s1, s2, qn_row, kn_rows, batch, seq)

    ncmp = seq // CMP_STRIDE
    ccs, cs1, cs2 = (jnp.asarray(a) for a in _rope_tables(np.arange(ncmp) * CMP_STRIDE + CMP_LEN - 1))
    kcd, vct = _nsa_compress(
        kc.reshape(batch * ncmp, CMP_STRIDE * LANES), vc.reshape(batch * ncmp, CMP_STRIDE * LANES),
        _compress_params(ck_pos, ck_w1, ck_w2), _compress_params(cv_pos, cv_w1, cv_w2),
        ccs, cs1, cs2, kn_rows, batch, ncmp)

    ovt = jnp.asarray(_overlap_matrix(seq).T).astype(BF16)
    yb = _nsa_attn(proj, qt, gtt, ka, vst, kwd, vwt, kcd, vct, ovt, batch, seq)

    w_o = w_out.astype(BF16)
    return _out_proj2(ya, yb, w_o[:M_W], w_o[M_W:], x2d)


def _odd_layer(x2d, norm_w, w_in, b_in, lb, head_norm, w_out, batch, seq):
    proj, fgate = _norm_proj(x2d, norm_w, w_in.astype(BF16), b_in, (0, H_W))
    y = _hgrn(proj, fgate, lb.reshape(1, H_W), head_norm.reshape(1, H_W), batch, seq)
    return _out_proj1(y, w_out.astype(BF16), x2d)


def kernel(x, norm_w, ev_w_in, ev_b_in, mlstm_f_bias, mlstm_conv_w, mlstm_conv_b, mlstm_head_norm, nsa_q_norm, nsa_k_norm, cmp_k_pos, cmp_k_w1, cmp_k_w2, cmp_v_pos, cmp_v_w1, cmp_v_w2, ev_w_out, od_w_in, od_b_in, hgrn_lb_logits, hgrn_head_norm, od_w_out):
    batch, seq, d = x.shape
    depth = norm_w.shape[0]
    lbs = jnp.cumsum(jax.nn.softmax(hgrn_lb_logits.astype(F32), axis=0), axis=0)
    x2d = x.reshape(batch * seq, d)
    for layer in range(depth):
        j = layer // 2
        if layer % 2 == 0:
            x2d = _even_layer(x2d, norm_w[layer], ev_w_in[j], ev_b_in[j], mlstm_f_bias[j], mlstm_conv_w[j],
                              mlstm_conv_b[j], mlstm_head_norm[j], nsa_q_norm[j], nsa_k_norm[j], cmp_k_pos[j],
                              cmp_k_w1[j], cmp_k_w2[j], cmp_v_pos[j], cmp_v_w1[j], cmp_v_w2[j], ev_w_out[j],
                              batch, seq)
        else:
            x2d = _odd_layer(x2d, norm_w[layer], od_w_in[j], od_b_in[j], lbs[j], hgrn_head_norm[j], od_w_out[j],
                             batch, seq)
    return x2d.reshape(batch, seq, d)
```

```python
import functools

import numpy as np
import jax
import jax.numpy as jnp
from jax import lax
from jax.experimental import pallas as pl
from jax.experimental.pallas import tpu as pltpu

F32 = jnp.float32
BF16 = jnp.bfloat16

LANES = 128
EPS = 1e-6
NEG = -1e30
VMEM_LIMIT_BYTES = 56 * 1024 * 1024

M_HEADS = 4
M_DIM = 128
M_W = M_HEADS * M_DIM
M_CHUNK = 128
CONV_K = 4
N_HEADS = 8
N_GROUPS = 2
N_HPG = N_HEADS // N_GROUPS
N_DIM = 64
N_W = N_HEADS * N_DIM
N_KV_W = N_GROUPS * N_DIM
CMP_LEN = 32
CMP_STRIDE = 16
CMP_HIDDEN = 128
SEL_BLOCK = 64
SEL_SHIFT = 6
SEL_TOPK = 16
WINDOW = 512
Q_BLOCK = 128
ROPE_THETA = 500000.0
ROPE_DIM = N_DIM // 4
ROPE_HALF = ROPE_DIM // 2
KV_TILE = 512
V_ROWS = 80
LOG2E = 1.4426950408889634
H_HEADS = 8
H_DIM = 128
H_W = H_HEADS * H_DIM
H_TILE = 128
H_LAG = 4

C_MQ, C_MK, C_MV, C_MO, C_MZ = 0, 512, 1024, 1536, 2048
C_NQ = 2560
C_KC, C_VC, C_KS, C_VS = 3072, 3200, 3328, 3456
C_KW, C_VW, C_SMALL = 3584, 3712, 3840
C_NZ = 4096
EV_COLS = 4608
SM_I, SM_F, SM_G = 0, 4, 8


def _cparams(*sem):
    return pltpu.CompilerParams(dimension_semantics=sem, vmem_limit_bytes=VMEM_LIMIT_BYTES)


def _dot(a, b):
    return jnp.dot(a, b, preferred_element_type=F32)


def _dot_nt(a, b):
    return lax.dot_general(a, b, (((1,), (1,)), ((), ())), preferred_element_type=F32)


def _dot_tn(a, b):
    return lax.dot_general(a, b, (((0,), (0,)), ((), ())), preferred_element_type=F32)


def _sigmoid(x):
    e = jnp.exp(-jnp.abs(x))
    r = 1.0 / (1.0 + e)
    return jnp.where(x >= 0, r, e * r)


def _silu(x):
    return x * _sigmoid(x)


def _split3(x):
    hi = x.astype(BF16)
    r = x - hi.astype(F32)
    mid = r.astype(BF16)
    lo = (r - mid.astype(F32)).astype(BF16)
    return hi, mid, lo


def _dot01(m01, x):
    hi, mid, lo = _split3(x)
    return _dot(m01, hi) + _dot(m01, mid) + _dot(m01, lo)


def _norm_proj_kernel(x_ref, nw_ref, w_ref, b_ref, o_ref, g_ref, h_ref, *, g_tiles, g_off):
    j = pl.program_id(1)

    @pl.when(j == 0)
    def _():
        x = x_ref[...]
        ms = jnp.mean(x * x, axis=-1, keepdims=True)
        h_ref[...] = (x * lax.rsqrt(ms + EPS) * nw_ref[...]).astype(BF16)

    acc = _dot(h_ref[...], w_ref[...]) + b_ref[...]
    o_ref[...] = acc.astype(o_ref.dtype)

    @pl.when((j >= g_tiles[0]) & (j < g_tiles[1]))
    def _():
        g_ref[...] = acc[:, g_off:g_off + g_ref.shape[1]]


def _norm_proj(x2d, norm_w, w_bf16, bias, g_cols, tm=1024, tn=512):
    t, d = x2d.shape
    n = w_bf16.shape[1]
    g_w = min(g_cols[1] - g_cols[0], tn)
    g_tiles = (g_cols[0] // tn, -(-g_cols[1] // tn))
    return pl.pallas_call(
        functools.partial(_norm_proj_kernel, g_tiles=g_tiles, g_off=g_cols[0] % tn),
        grid=(t // tm, n // tn),
        in_specs=[
            pl.BlockSpec((tm, d), lambda i, j: (i, 0)),
            pl.BlockSpec((1, d), lambda i, j: (0, 0)),
            pl.BlockSpec((d, tn), lambda i, j: (0, j)),
            pl.BlockSpec((1, tn), lambda i, j: (0, j)),
        ],
        out_specs=[pl.BlockSpec((tm, tn), lambda i, j: (i, j)),
                   pl.BlockSpec((tm, g_w), lambda i, j: (i, jnp.clip(j - g_tiles[0], 0, g_tiles[1] - g_tiles[0] - 1)))],
        out_shape=[jax.ShapeDtypeStruct((t, n), BF16), jax.ShapeDtypeStruct((t, g_cols[1] - g_cols[0]), F32)],
        scratch_shapes=[pltpu.VMEM((tm, d), BF16)],
        compiler_params=_cparams("parallel", "arbitrary"),
        name="norm_proj",
    )(x2d, norm_w.reshape(1, d), w_bf16, bias.reshape(1, n))


def _out_proj2_kernel(ya_ref, yb_ref, wa_ref, wb_ref, x_ref, o_ref):
    o_ref[...] = x_ref[...] + _dot(ya_ref[...], wa_ref[...]) + _dot(yb_ref[...], wb_ref[...])


def _out_proj2(ya, yb, wa, wb, x2d, tm=1024):
    t, d = x2d.shape
    ka, kb = ya.shape[1], yb.shape[1]
    return pl.pallas_call(
        _out_proj2_kernel,
        grid=(t // tm,),
        in_specs=[
            pl.BlockSpec((tm, ka), lambda i: (i, 0)),
            pl.BlockSpec((tm, kb), lambda i: (i, 0)),
            pl.BlockSpec((ka, d), lambda i: (0, 0)),
            pl.BlockSpec((kb, d), lambda i: (0, 0)),
            pl.BlockSpec((tm, d), lambda i: (i, 0)),
        ],
        out_specs=pl.BlockSpec((tm, d), lambda i: (i, 0)),
        out_shape=jax.ShapeDtypeStruct((t, d), F32),
        compiler_params=_cparams("parallel"),
        name="out_proj2",
    )(ya, yb, wa, wb, x2d)


def _out_proj1_kernel(y_ref, w_ref, x_ref, o_ref):
    o_ref[...] = x_ref[...] + _dot(y_ref[...], w_ref[...])


def _out_proj1(y, w, x2d, tm=1024):
    t, d = x2d.shape
    k = y.shape[1]
    return pl.pallas_call(
        _out_proj1_kernel,
        grid=(t // tm,),
        in_specs=[
            pl.BlockSpec((tm, k), lambda i: (i, 0)),
            pl.BlockSpec((k, d), lambda i: (0, 0)),
            pl.BlockSpec((tm, d), lambda i: (i, 0)),
        ],
        out_specs=pl.BlockSpec((tm, d), lambda i: (i, 0)),
        out_shape=jax.ShapeDtypeStruct((t, d), F32),
        compiler_params=_cparams("parallel"),
        name="out_proj1",
    )(y, w, x2d)


def _mlstm_kernel(q_ref, k_ref, v_ref, o_ref, z_ref, sm_ref, cw_ref, cb_ref, fb_ref, hn_ref,
                  out_ref, xbuf, c_ref, n_ref, m_ref):
    L, D = M_CHUNK, M_DIM

    XR = xbuf.shape[0]

    @pl.when(pl.program_id(1) == 0)
    def _():
        xbuf[...] = jnp.zeros_like(xbuf)
        c_ref[...] = jnp.zeros_like(c_ref)
        n_ref[...] = jnp.zeros_like(n_ref)
        m_ref[...] = jnp.zeros_like(m_ref)

    xbuf[16:16 + L, 0:M_W] = q_ref[...]
    xbuf[16:16 + L, M_W:2 * M_W] = k_ref[...]
    x2 = xbuf[...]
    t_i = lax.broadcasted_iota(jnp.int32, (L, XR), 0)
    r_i = lax.broadcasted_iota(jnp.int32, (L, XR), 1)
    acc = cb_ref[...] + cw_ref[CONV_K - 1:CONV_K, :] * x2[16:16 + L].astype(F32)
    for d in range(1, CONV_K):
        shift = jnp.where(r_i == t_i + 16 - d, 1.0, 0.0).astype(BF16)
        acc = acc + cw_ref[CONV_K - 1 - d:CONV_K - d, :] * _dot(shift, x2)
    qk = _silu(acc)
    xbuf[0:16, :] = xbuf[L:L + 16, :]

    row = lax.broadcasted_iota(jnp.int32, (L, L), 0)
    col = lax.broadcasted_iota(jnp.int32, (L, L), 1)
    causal = col <= row
    fpre = sm_ref[...] + fb_ref[...]
    lane = lax.broadcasted_iota(jnp.int32, (L, LANES), 1)
    is_f = (lane >= SM_F) & (lane < SM_F + M_HEADS)
    gates = jnp.where(is_f, jnp.minimum(fpre, 0.0) - jnp.log(1.0 + jnp.exp(-jnp.abs(fpre))), fpre)
    bcols = _dot01(jnp.where(causal, 1.0, 0.0).astype(BF16), gates)
    gates_t = gates.T
    bcols_t = bcols.T

    heads = []
    for h in range(M_HEADS):
        hs = slice(h * D, (h + 1) * D)
        q = qk[:, hs]
        k = qk[:, M_W + h * D:M_W + (h + 1) * D] * (D ** -0.5)
        i_col = gates[:, SM_I + h:SM_I + h + 1]
        i_row = gates_t[SM_I + h:SM_I + h + 1, :]
        bcum_col = bcols[:, SM_F + h:SM_F + h + 1]
        bcum_row = bcols_t[SM_F + h:SM_F + h + 1, :]
        m_prev = m_ref[h]
        dmat = jnp.where(causal, bcum_col - bcum_row + i_row, -jnp.inf)
        inter = bcum_col + m_prev
        m_row = jnp.maximum(jnp.max(dmat, axis=1, keepdims=True), inter)
        b_last = bcum_col[L - 1:L, :]
        dl = b_last - bcum_col + i_col
        m_new = jnp.maximum(b_last + m_prev, jnp.max(dl, axis=0, keepdims=True))
        wkk = jnp.exp(dl - m_new) * k
        heads.append(dict(
            hs=hs, q=q, qb=q.astype(BF16), kb=k.astype(BF16), vb=v_ref[:, hs].astype(BF16), wkk=wkk,
            e=jnp.exp(dmat - m_row), w_inter=jnp.exp(inter - m_row), floor=jnp.exp(-m_row),
            decay=jnp.exp(b_last + m_prev - m_new), m_new=m_new))
    s_qk = [_dot_nt(t["qb"], t["kb"]) for t in heads]
    s_qc = [_dot(t["qb"], c_ref[h].astype(BF16)) for h, t in enumerate(heads)]
    s_kv = [_dot_tn(t["wkk"].astype(BF16), t["vb"]) for t in heads]
    smats = [s_qk[h] * t["e"] for h, t in enumerate(heads)]
    s_sv = [_dot(smats[h].astype(BF16), t["vb"]) for h, t in enumerate(heads)]
    for h, t in enumerate(heads):
        hs = t["hs"]
        num = s_sv[h] + t["w_inter"] * s_qc[h]
        den = (jnp.sum(smats[h], axis=1, keepdims=True)
               + t["w_inter"] * jnp.sum(t["q"] * n_ref[h], axis=1, keepdims=True))
        hh = num / jnp.maximum(jnp.abs(den), t["floor"])
        c_ref[h] = t["decay"] * c_ref[h] + s_kv[h]
        n_ref[h] = t["decay"] * n_ref[h] + jnp.sum(t["wkk"], axis=0, keepdims=True)
        m_ref[h] = t["m_new"]
        hn = hh * lax.rsqrt(jnp.mean(hh * hh, axis=1, keepdims=True) + EPS) * hn_ref[:, hs]
        gate = _sigmoid(o_ref[:, hs].astype(F32)) * _silu(z_ref[:, hs].astype(F32))
        out_ref[:, hs] = (hn * gate).astype(out_ref.dtype)


def _mlstm(proj, small, conv_w, conv_b, f_bias_row, head_norm, batch, seq):
    L = M_CHUNK
    nc = seq // L
    cb = lambda off: pl.BlockSpec((L, M_W), lambda b, c, off=off: (b * nc + c, off // M_W))
    return pl.pallas_call(
        _mlstm_kernel,
        grid=(batch, nc),
        in_specs=[
            cb(C_MQ), cb(C_MK), cb(C_MV), cb(C_MO), cb(C_MZ),
            pl.BlockSpec((L, LANES), lambda b, c: (b * nc + c, 0)),
            pl.BlockSpec((CONV_K, 2 * M_W), lambda b, c: (0, 0)),
            pl.BlockSpec((1, 2 * M_W), lambda b, c: (0, 0)),
            pl.BlockSpec((1, LANES), lambda b, c: (0, 0)),
            pl.BlockSpec((1, M_W), lambda b, c: (0, 0)),
        ],
        out_specs=pl.BlockSpec((L, M_W), lambda b, c: (b * nc + c, 0)),
        out_shape=jax.ShapeDtypeStruct((batch * seq, M_W), BF16),
        scratch_shapes=[
            pltpu.VMEM((2 * L, 2 * M_W), BF16),
            pltpu.VMEM((M_HEADS, M_DIM, M_DIM), F32),
            pltpu.VMEM((M_HEADS, 1, M_DIM), F32),
            pltpu.VMEM((M_HEADS, 1, 1), F32),
        ],
        compiler_params=_cparams("parallel", "arbitrary"),
        name="mlstm",
    )(proj, proj, proj, proj, proj, small, conv_w, conv_b.reshape(1, -1), f_bias_row, head_norm.reshape(1, -1))


def _group_rms(x, w_row):
    lane = lax.broadcasted_iota(jnp.int32, x.shape, 1)
    left = lane < N_DIM
    sq = x * x
    tot = jnp.sum(sq, axis=1, keepdims=True)
    lft = jnp.sum(jnp.where(left, sq, 0.0), axis=1, keepdims=True)
    ms = jnp.where(left, lft, tot - lft) * (1.0 / N_DIM)
    return x * lax.rsqrt(ms + EPS) * w_row


def _rope(x, cs, s1, s2):
    return x * cs + pltpu.roll(x, ROPE_HALF, 1) * s1 + pltpu.roll(x, LANES - ROPE_HALF, 1) * s2


def _dup_halves(x):
    lane = lax.broadcasted_iota(jnp.int32, x.shape, 1)
    sw = pltpu.roll(x, N_DIM, 1)
    left = lane < N_DIM
    return jnp.where(left, x, sw), jnp.where(left, sw, x)


def _nsa_prep_kernel(q_ref, kv_ref, kw_ref, sm_ref, cs_ref, s1_ref, s2_ref, qn_ref, kn_ref,
                     qt_ref, ka_ref, kwd_ref, vst_ref, vwt_ref, kc_ref, vc_ref, gtt_ref,
                     *, tp, seq):
    cs, s1, s2 = cs_ref[...], s1_ref[...], s2_ref[...]
    for j in range(N_W // LANES):
        sl = slice(j * LANES, (j + 1) * LANES)
        qn = _group_rms(q_ref[:, sl].astype(F32), qn_ref[...])
        qt_ref[sl, :] = (_rope(qn, cs, s1, s2) * (N_DIM ** -0.5 * LOG2E)).T.astype(BF16)
    kv = kv_ref[...].astype(F32)
    kc_ref[...] = kv[:, 0:LANES]
    vc_ref[...] = kv[:, LANES:2 * LANES]
    ks = _rope(_group_rms(kv[:, 2 * LANES:3 * LANES], kn_ref[1:2, :]), cs, s1, s2)
    k0, k1 = _dup_halves(ks)
    pos0 = (pl.program_id(0) * tp) % seq
    pos = pos0 + lax.broadcasted_iota(jnp.int32, (tp, LANES), 0)
    blk = lax.broadcasted_iota(jnp.int32, (tp, LANES), 1)
    onehot = jnp.where(jnp.right_shift(pos, SEL_SHIFT) == blk, 1.0, 0.0).astype(BF16)
    ka_ref[:, 0:LANES] = k0.astype(BF16)
    ka_ref[:, LANES:2 * LANES] = onehot
    ka_ref[:, 2 * LANES:3 * LANES] = k1.astype(BF16)
    ka_ref[:, 3 * LANES:4 * LANES] = onehot
    vst = kv[:, 3 * LANES:4 * LANES].T
    fill = (lax.broadcasted_iota(jnp.int32, (V_ROWS - N_DIM, Q_BLOCK), 0) == 0).astype(BF16)
    for c in range(tp // Q_BLOCK):
        for g in range(N_GROUPS):
            vst_ref[c, g * V_ROWS:g * V_ROWS + N_DIM, :] = (
                vst[g * N_DIM:(g + 1) * N_DIM, c * Q_BLOCK:(c + 1) * Q_BLOCK].astype(BF16))
            vst_ref[c, g * V_ROWS + N_DIM:(g + 1) * V_ROWS, :] = fill
    kwb = kw_ref[...].astype(F32)
    kw = _rope(_group_rms(kwb[:, 0:LANES], kn_ref[2:3, :]), cs, s1, s2)
    w0, w1 = _dup_halves(kw)
    kwd_ref[:, 0:LANES] = w0.astype(BF16)
    kwd_ref[:, LANES:2 * LANES] = w1.astype(BF16)
    vwt_ref[...] = kwb[:, LANES:2 * LANES].T.astype(BF16)
    gtt_ref[...] = _sigmoid(sm_ref[...]).T


def _nsa_prep(proj, small, cs, s1, s2, qn_row, kn_rows, batch, seq):
    tp = KV_TILE
    t = batch * seq
    nsb = seq // tp
    rb = lambda w, off: pl.BlockSpec((tp, w), lambda i, off=off, w=w: (i, off // w))
    tab = pl.BlockSpec((tp, LANES), lambda i: (i % nsb, 0))
    o = lambda w: pl.BlockSpec((tp, w), lambda i: (i, 0))
    ot = lambda r: pl.BlockSpec((r, tp), lambda i: (0, i))
    sds = lambda w, dt: jax.ShapeDtypeStruct((t, w), dt)
    return pl.pallas_call(
        functools.partial(_nsa_prep_kernel, tp=tp, seq=seq),
        grid=(t // tp,),
        in_specs=[rb(512, C_NQ), rb(512, C_KC), rb(512, C_KW), o(LANES), tab, tab, tab,
                  pl.BlockSpec((1, LANES), lambda i: (0, 0)), pl.BlockSpec((3, LANES), lambda i: (0, 0))],
        out_specs=[ot(N_W), o(512), o(256),
                   pl.BlockSpec((tp // Q_BLOCK, N_GROUPS * V_ROWS, Q_BLOCK), lambda i: (i, 0, 0)),
                   ot(LANES), o(128), o(128), ot(LANES)],
        out_shape=[jax.ShapeDtypeStruct((N_W, t), BF16), sds(512, BF16), sds(256, BF16),
                   jax.ShapeDtypeStruct((t // Q_BLOCK, N_GROUPS * V_ROWS, Q_BLOCK), BF16),
                   jax.ShapeDtypeStruct((LANES, t), BF16),
                   sds(128, F32), sds(128, F32), jax.ShapeDtypeStruct((LANES, t), F32)],
        compiler_params=_cparams("parallel"),
        name="nsa_prep",
    )(proj, proj, proj, small, cs, s1, s2, qn_row, kn_rows)


def _gelu_tanh(x):
    return 0.5 * x * (1.0 + jnp.tanh(0.7978845608028654 * (x + 0.044715 * x * x * x)))


def _compress_mlp(x_ref, pa_ref, pb_ref, wa_ref, wb_ref, w2_ref, hb_ref):
    n = x_ref.shape[0]
    x = x_ref[...]
    ha = _dot((x + pa_ref[...]).astype(BF16), wa_ref[...])
    hb_ref[0:n, :] = _dot((x + pb_ref[...]).astype(BF16), wb_ref[...])
    hb_ref[n:n + 8, :] = jnp.zeros((8, hb_ref.shape[1]), F32)
    hid = ha + hb_ref[1:n + 1, :]
    return _dot(_gelu_tanh(hid).astype(BF16), w2_ref[...])


def _nsa_compress_kernel(kx_ref, vx_ref, kpa_ref, kpb_ref, vpa_ref, vpb_ref, kwa_ref, kwb_ref, kw2_ref,
                         vwa_ref, vwb_ref, vw2_ref, cs_ref, s1_ref, s2_ref, kn_ref,
                         kcd_ref, vct_ref, hb_ref):
    n = kx_ref.shape[0]
    rowi = lax.broadcasted_iota(jnp.int32, (n, LANES), 0)
    live = rowi < n - 1
    kc = _compress_mlp(kx_ref, kpa_ref, kpb_ref, kwa_ref, kwb_ref, kw2_ref, hb_ref)
    kc = _rope(_group_rms(kc, kn_ref[0:1, :]), cs_ref[...], s1_ref[...], s2_ref[...])
    kc = jnp.where(live, kc, 0.0)
    c0, c1 = _dup_halves(kc)
    kcd_ref[:, 0:LANES] = c0.astype(BF16)
    kcd_ref[:, LANES:2 * LANES] = c1.astype(BF16)
    vc = _compress_mlp(vx_ref, vpa_ref, vpb_ref, vwa_ref, vwb_ref, vw2_ref, hb_ref)
    vc = jnp.where(live, vc, 0.0)
    vct_ref[0] = vc.T.astype(BF16)


def _nsa_compress(kx, vx, kp, vp, cs, s1, s2, kn_rows, batch, nrow):
    wide = CMP_STRIDE * LANES
    full = lambda a: pl.BlockSpec(a.shape, lambda b: (0,) * a.ndim)
    xs = pl.BlockSpec((nrow, wide), lambda b: (b, 0))
    args = (kx, vx, kp[0], kp[1], vp[0], vp[1], kp[2], kp[3], kp[4], vp[2], vp[3], vp[4], cs, s1, s2, kn_rows)
    return pl.pallas_call(
        _nsa_compress_kernel,
        grid=(batch,),
        in_specs=[xs, xs] + [full(a) for a in args[2:]],
        out_specs=[pl.BlockSpec((nrow, 2 * LANES), lambda b: (b, 0)),
                   pl.BlockSpec((1, LANES, nrow), lambda b: (b, 0, 0))],
        out_shape=[jax.ShapeDtypeStruct((batch * nrow, 2 * LANES), BF16),
                   jax.ShapeDtypeStruct((batch, LANES, nrow), BF16)],
        scratch_shapes=[pltpu.VMEM((nrow + 8, 2 * CMP_HIDDEN), F32)],
        compiler_params=_cparams("parallel"),
        name="nsa_compress",
    )(*args)


def _compress_params(pos, w1, w2):
    def expand(w1_half, pos_half):
        w = w1_half.reshape(CMP_STRIDE, N_DIM, CMP_HIDDEN)
        z = jnp.zeros_like(w)
        g0 = jnp.concatenate([w, z], axis=-1)
        g1 = jnp.concatenate([z, w], axis=-1)
        wx = jnp.stack([g0, g1], axis=1).reshape(CMP_STRIDE * LANES, 2 * CMP_HIDDEN)
        px = jnp.concatenate([pos_half, pos_half], axis=-1).reshape(1, CMP_STRIDE * LANES)
        return wx.astype(BF16), px.astype(F32)

    half = CMP_STRIDE * N_DIM
    wa, pa = expand(w1[:half], pos[:CMP_STRIDE])
    wb, pb = expand(w1[half:], pos[CMP_STRIDE:])
    z2 = jnp.zeros_like(w2)
    w2x = jnp.concatenate([jnp.concatenate([w2, z2], axis=1), jnp.concatenate([z2, w2], axis=1)], axis=0)
    return pa, pb, wa, wb, w2x.astype(BF16)


def _nsa_attn_kernel(qt_ref, gtt_ref, z_ref, ka_ref, vst_ref,
                     kw0, kw1, kw2, kw3, kw4, wt0, wt1, wt2, wt3, wt4,
                     kcd_ref, vct_ref, ovt_ref, out_ref, qa_ref, acc_ref, *, ncmp):
    QB = Q_BLOCK
    NP = N_HEADS // 2
    PW = 2 * QB
    i = pl.program_id(1)
    t0 = i * QB
    kw_refs = (kw0, kw1, kw2, kw3, kw4)
    vwt_refs = (wt0, wt1, wt2, wt3, wt4)
    nwt = len(kw_refs)

    sub = lax.broadcasted_iota(jnp.int32, (QB, QB), 0)
    lane = lax.broadcasted_iota(jnp.int32, (QB, QB), 1)
    top = sub < N_DIM
    t_row = t0 + lax.broadcasted_iota(jnp.int32, (1, QB), 1)
    t_row2 = jnp.concatenate([t_row, t_row], axis=1)

    def pair_q(pr):
        slab = qt_ref[pr * LANES:(pr + 1) * LANES, :]
        zero = jnp.zeros_like(slab)
        return jnp.concatenate([jnp.where(top, slab, zero), jnp.where(top, zero, slab)], axis=1)

    nidx = lax.broadcasted_iota(jnp.int32, (ncmp, PW), 0)
    cmp_ok = (nidx * CMP_STRIDE + (CMP_LEN - 1) <= t_row2) & (nidx < ncmp - 1)
    blk_f = sub.astype(F32)
    cur = jnp.right_shift(t_row, SEL_SHIFT)
    grp = lambda pr: pr // (N_HPG // 2)
    for pr in range(NP):
        qa_ref[pr, 0:LANES, :] = pair_q(pr)
    s_cmp = [_dot(kcd_ref[:, grp(pr) * LANES:(grp(pr) + 1) * LANES], qa_ref[pr, 0:LANES, :]) for pr in range(NP)]
    p_cmp = []
    for pr in range(NP):
        s = jnp.where(cmp_ok, s_cmp[pr], NEG)
        mx = jnp.max(s, axis=0, keepdims=True)
        p = jnp.where(cmp_ok, jnp.exp2(s - mx), 0.0)
        den = jnp.sum(p, axis=0, keepdims=True)
        p_cmp.append(p * (1.0 / jnp.maximum(den, 1e-30)))
    o_cmp = [_dot(vct_ref[0, grp(pr) * N_DIM:(grp(pr) + 1) * N_DIM, :], p_cmp[pr].astype(BF16))
             for pr in range(NP)]
    imps = []
    for g in range(N_GROUPS):
        psum = jnp.zeros((ncmp, QB), F32)
        for pp in range(N_HPG // 2):
            p = p_cmp[g * (N_HPG // 2) + pp]
            psum = psum + p[:, 0:QB] + p[:, QB:PW]
        p_hi = psum.astype(BF16)
        p_lo = (psum - p_hi.astype(F32)).astype(BF16)
        imps.append(_dot(ovt_ref[...], p_hi) + _dot(ovt_ref[...], p_lo))
    forced = (sub == 0) | (sub == cur) | (sub == cur - 1)
    works = [jnp.where(sub > cur, NEG, jnp.where(forced, -NEG, imp)) for imp in imps]
    sels = [jnp.zeros((QB, QB), jnp.bool_) for _ in range(N_GROUPS)]
    for _ in range(SEL_TOPK):
        for g in range(N_GROUPS):
            mx = jnp.max(works[g], axis=0, keepdims=True)
            idx = jnp.min(jnp.where(works[g] == mx, blk_f, float(LANES)), axis=0, keepdims=True)
            pick = blk_f == idx
            sels[g] = sels[g] | pick
            works[g] = jnp.where(pick, -jnp.inf, works[g])
    for g in range(N_GROUPS):
        neg = jnp.where(sels[g] & (sub <= cur), 0.0, NEG).astype(BF16)
        neg2 = jnp.concatenate([neg, neg], axis=1)
        for pp in range(N_HPG // 2):
            qa_ref[g * (N_HPG // 2) + pp, LANES:2 * LANES, :] = neg2

    CH = QB
    above = jnp.concatenate([sub > lane, sub > lane], axis=1)

    def sel_scores(cj):
        k0 = pl.multiple_of(cj * CH, CH)
        return [_dot(ka_ref[pl.ds(k0, CH), 2 * grp(pr) * LANES:2 * (grp(pr) + 1) * LANES], qa_ref[pr])
                for pr in range(NP)]

    def sel_softmax(ss, ms, masked):
        new, alphas, ps = [], [], []
        for pr in range(NP):
            s = jnp.where(above, NEG, ss[pr]) if masked else ss[pr]
            m_new = jnp.maximum(ms[pr], jnp.max(s, axis=0, keepdims=True))
            alphas.append(jnp.exp2(ms[pr] - m_new))
            ps.append(jnp.exp2(s - m_new).astype(BF16))
            new.append(m_new)
        return tuple(new), alphas, ps

    def sel_values(cj, alphas, ps):
        for pr in range(NP):
            g = grp(pr)
            acc_ref[pr] = alphas[pr] * acc_ref[pr] + _dot(vst_ref[cj, g * V_ROWS:(g + 1) * V_ROWS, :], ps[pr])

    def sel_chunk(cj, ms, masked):
        ms, alphas, ps = sel_softmax(sel_scores(cj), ms, masked)
        sel_values(cj, alphas, ps)
        return ms

    def sel_tile(jt, ms):
        nc = KV_TILE // CH
        k0 = pl.multiple_of(jt * KV_TILE, KV_TILE)
        big = [_dot(ka_ref[pl.ds(k0, KV_TILE), 2 * grp(pr) * LANES:2 * (grp(pr) + 1) * LANES], qa_ref[pr])
               for pr in range(NP)]
        for c in range(nc):
            ss = [b[c * CH:(c + 1) * CH] for b in big]
            ms, alphas, ps = sel_softmax(ss, ms, False)
            sel_values(jt * nc + c, alphas, ps)
        return ms

    acc_ref[...] = jnp.zeros_like(acc_ref)
    ms = tuple(jnp.full((1, PW), 2.0 * NEG, F32) for _ in range(NP))
    n_tile = i // (KV_TILE // CH)
    ms = lax.fori_loop(0, n_tile, sel_tile, ms)
    ms = lax.fori_loop(n_tile * (KV_TILE // CH), i, lambda cj, ms: sel_chunk(cj, ms, False), ms)
    sel_chunk(i, ms, True)
    o_sel = [acc_ref[pr, 0:N_DIM, :] * (1.0 / acc_ref[pr, N_DIM:N_DIM + 1, :]) for pr in range(NP)]

    s_win = [[_dot(kw_refs[c][:, grp(pr) * LANES:(grp(pr) + 1) * LANES], qa_ref[pr, 0:LANES, :])
              for c in range(nwt)] for pr in range(NP)]
    p_win, d_win = [], []
    for pr in range(NP):
        ss = []
        for c in range(nwt):
            ok = (i - (nwt - 1) + c) >= 0
            if c == 0:
                ok = ok & above
            elif c == nwt - 1:
                ok = ok & jnp.logical_not(above)
            ss.append(jnp.where(ok, s_win[pr][c], NEG))
        mx = ss[0].max(axis=0, keepdims=True)
        for c in range(1, nwt):
            mx = jnp.maximum(mx, ss[c].max(axis=0, keepdims=True))
        ps = [jnp.exp2(ss[c] - mx) for c in range(nwt)]
        den = ps[0].sum(axis=0, keepdims=True)
        for c in range(1, nwt):
            den = den + ps[c].sum(axis=0, keepdims=True)
        p_win.append([p.astype(BF16) for p in ps])
        d_win.append(den)
    o_win = []
    for pr in range(NP):
        g = grp(pr)
        acc = _dot(vwt_refs[0][g * N_DIM:(g + 1) * N_DIM, :], p_win[pr][0])
        for c in range(1, nwt):
            acc = acc + _dot(vwt_refs[c][g * N_DIM:(g + 1) * N_DIM, :], p_win[pr][c])
        o_win.append(acc * (1.0 / d_win[pr]))

    for pr in range(NP):
        halves = []
        for par in range(2):
            h = 2 * pr + par
            cs = slice(par * QB, (par + 1) * QB)
            gc = gtt_ref[SM_G + 3 * h:SM_G + 3 * h + 1, :]
            gs = gtt_ref[SM_G + 3 * h + 1:SM_G + 3 * h + 2, :]
            gw = gtt_ref[SM_G + 3 * h + 2:SM_G + 3 * h + 3, :]
            halves.append(gc * o_cmp[pr][:, cs] + gs * o_sel[pr][:, cs] + gw * o_win[pr][:, cs])
        y = jnp.concatenate(halves, axis=0).T
        sl = slice(pr * LANES, (pr + 1) * LANES)
        out_ref[:, sl] = (y * _silu(z_ref[:, sl].astype(F32))).astype(out_ref.dtype)


def _nsa_attn(proj, qt, gtt, ka, vst, kwd, vwt, kcd, vct, ovt, batch, seq):
    QB = Q_BLOCK
    nqb = seq // QB
    ncmp = seq // CMP_STRIDE
    ntile = seq // KV_TILE
    nwt = WINDOW // QB + 1
    row = lambda w: pl.BlockSpec((QB, w), lambda b, i: (b * nqb + i, 0))
    colb = lambda r: pl.BlockSpec((r, QB), lambda b, i: (0, b * nqb + i))
    wblk = lambda b, i, c: b * nqb + jnp.maximum(i - (nwt - 1) + c, 0)
    in_specs = ([colb(N_W), colb(LANES), pl.BlockSpec((QB, N_W), lambda b, i: (b * nqb + i, C_NZ // N_W)),
                 pl.BlockSpec((seq, 4 * LANES), lambda b, i: (b, 0)),
                 pl.BlockSpec((nqb, N_GROUPS * V_ROWS, QB), lambda b, i: (b, 0, 0))]
                + [pl.BlockSpec((QB, 2 * LANES), lambda b, i, c=c: (wblk(b, i, c), 0)) for c in range(nwt)]
                + [pl.BlockSpec((LANES, QB), lambda b, i, c=c: (0, wblk(b, i, c))) for c in range(nwt)]
                + [pl.BlockSpec((ncmp, 2 * LANES), lambda b, i: (b, 0)),
                   pl.BlockSpec((1, LANES, ncmp), lambda b, i: (b, 0, 0)),
                   pl.BlockSpec((LANES, ncmp), lambda b, i: (0, 0))])
    return pl.pallas_call(
        functools.partial(_nsa_attn_kernel, ncmp=ncmp),
        grid=(batch, nqb),
        in_specs=in_specs,
        out_specs=row(N_W),
        out_shape=jax.ShapeDtypeStruct((batch * seq, N_W), BF16),
        scratch_shapes=[pltpu.VMEM((N_HEADS // 2, 2 * LANES, 2 * QB), BF16),
                        pltpu.VMEM((N_HEADS // 2, V_ROWS, 2 * QB), F32)],
        compiler_params=_cparams("parallel", "arbitrary"),
        name="nsa_attn",
    )(qt, gtt, proj, ka, vst, *([kwd] * nwt), *([vwt] * nwt), kcd, vct, ovt)


def _hgrn_kernel(f_ref, i_ref, q_ref, z_ref, lb_ref, hn_ref, out_ref, st_ref):
    T, D = H_TILE, H_DIM
    W = H_W

    @pl.when(pl.program_id(1) == 0)
    def _():
        st_ref[...] = jnp.zeros_like(st_ref)

    lb = lb_ref[...]
    fp = f_ref[...]
    e = jnp.exp(-jnp.abs(fp))
    r = 1.0 / (1.0 + e)
    sig = jnp.where(fp >= 0, r, e * r)
    nsig = jnp.where(fp >= 0, e * r, r)
    logf2 = jnp.log(lb + (1.0 - lb) * sig) * LOG2E
    kk = (1.0 - lb) * nsig

    row = lax.broadcasted_iota(jnp.int32, (T, T), 0)
    col = lax.broadcasted_iota(jnp.int32, (T, T), 1)
    tri01 = jnp.where(col <= row, 1.0, 0.0).astype(BF16)
    bcum = _dot01(tri01, logf2)
    rowi = lax.broadcasted_iota(jnp.int32, (T, 1), 0)

    segs = []
    sg = 2 * H_LAG
    while sg <= T:
        upper = jnp.bitwise_and(rowi, sg - 1) >= sg // 2
        if sg == T:
            bmid = jnp.broadcast_to(bcum[sg // 2 - 1:sg // 2, :], (T, W))
            same = None
        else:
            b3 = bcum.reshape(T // sg, sg, W)
            bmid = jnp.broadcast_to(b3[:, sg // 2 - 1:sg // 2, :], (T // sg, sg, W)).reshape(T, W)
            shift = sg.bit_length() - 1
            same = jnp.right_shift(row, shift) == jnp.right_shift(col, shift)
        segs.append((upper, bmid, same))
        sg *= 2

    sub3 = lax.broadcasted_iota(jnp.int32, (T // 8, 8, 1), 1)
    lag_ok = [jnp.bitwise_and(sub3, H_LAG - 1) >= d for d in range(H_LAG)]
    b_end = bcum[T - 1:T, :]
    v_all = i_ref[...].astype(F32)
    for h in range(H_HEADS):
        hs = slice(h * D, (h + 1) * D)
        q = _silu(q_ref[:, hs].astype(F32))
        k = kk[:, hs]
        b = bcum[:, hs]
        v = v_all[:, hs]
        vb = v.astype(BF16)
        a_seg = []
        for upper, bmid, same in segs:
            bm = bmid[:, hs]
            qm = (q * jnp.exp2(jnp.where(upper, b - bm, NEG))).astype(BF16)
            km = (k * jnp.exp2(jnp.where(upper, NEG, bm - b))).astype(BF16)
            a_seg.append(_dot_nt(qm, km))
        st = st_ref[h]
        o_state = _dot_nt((q * jnp.exp2(b)).astype(BF16), st.astype(BF16))
        kt = (k * jnp.exp2(b_end[:, hs] - b)).astype(BF16)
        st_ref[h] = st * jnp.exp2(b_end[:, hs]) + _dot_tn(vb, kt)
        q3, k3, b3, v3 = (a.reshape(T // 8, 8, D) for a in (q, k, b, v))
        o3 = jnp.sum(q3 * k3, axis=2, keepdims=True) * v3
        for d in range(1, H_LAG):
            ks, bs, vs = (pltpu.roll(a, d, 1) for a in (k3, b3, v3))
            a = jnp.sum(q3 * ks * jnp.exp2(b3 - bs), axis=2, keepdims=True)
            o3 = o3 + jnp.where(lag_ok[d], a, 0.0) * vs
        a_tot = None
        for (upper, bmid, same), a in zip(segs, a_seg):
            a = a if same is None else jnp.where(same, a, 0.0)
            a_tot = a if a_tot is None else a_tot + a
        oh = o3.reshape(T, D) + o_state + _dot(a_tot.astype(BF16), vb)
        on = oh * lax.rsqrt(jnp.mean(oh * oh, axis=1, keepdims=True) + EPS) * hn_ref[:, hs]
        out_ref[:, hs] = (on * _silu(z_ref[:, hs].astype(F32))).astype(out_ref.dtype)


def _hgrn(proj, fgate, lb_row, hn_row, batch, seq):
    T = H_TILE
    nt = seq // T
    cb = lambda j: pl.BlockSpec((T, H_W), lambda b, c, j=j: (b * nt + c, j))
    vec = pl.BlockSpec((1, H_W), lambda b, c: (0, 0))
    return pl.pallas_call(
        _hgrn_kernel,
        grid=(batch, nt),
        in_specs=[pl.BlockSpec((T, H_W), lambda b, c: (b * nt + c, 0)), cb(1), cb(2), cb(3), vec, vec],
        out_specs=pl.BlockSpec((T, H_W), lambda b, c: (b * nt + c, 0)),
        out_shape=jax.ShapeDtypeStruct((batch * seq, H_W), BF16),
        scratch_shapes=[pltpu.VMEM((H_HEADS, H_DIM, H_DIM), F32)],
        compiler_params=_cparams("parallel", "arbitrary"),
        name="hgrn2",
    )(fgate, proj, proj, proj, lb_row, hn_row)


def _even_relayout(a):
    nb = 5 * M_W + 2 * M_HEADS
    g0 = nb + N_W + 6 * N_KV_W
    col = lambda lo, n: lax.slice_in_dim(a, lo, lo + n, axis=-1)
    pad = jnp.zeros(a.shape[:-1] + (C_NZ - C_SMALL - SM_G - 3 * N_HEADS,), a.dtype)
    out = jnp.concatenate([
        col(0, 5 * M_W),
        col(nb, N_W + 6 * N_KV_W),
        col(5 * M_W, 2 * M_HEADS),
        col(g0, 3 * N_HEADS), pad,
        col(g0 + 3 * N_HEADS, N_W),
    ], axis=-1)
    assert out.shape[-1] == EV_COLS
    return out


def _rope_tables(positions):
    inv_freq = ROPE_THETA ** (-np.arange(ROPE_HALF, dtype=np.float64) / ROPE_HALF)
    ang = positions.astype(np.float64)[:, None] * inv_freq[None, :]
    cos, sin = np.cos(ang).astype(np.float32), np.sin(ang).astype(np.float32)
    n = positions.shape[0]
    z8 = np.zeros((n, ROPE_HALF), np.float32)
    rest1 = np.ones((n, N_DIM - ROPE_DIM), np.float32)
    rest0 = np.zeros((n, N_DIM - ROPE_DIM), np.float32)
    cs = np.concatenate([cos, cos, rest1], axis=1)
    s1 = np.concatenate([z8, sin, rest0], axis=1)
    s2 = np.concatenate([-sin, z8, rest0], axis=1)
    two = lambda a: np.concatenate([a, a], axis=1)
    return two(cs), two(s1), two(s2)


def _overlap_matrix(seq):
    ncmp = seq // CMP_STRIDE
    n_sel = seq // SEL_BLOCK
    cs = np.arange(ncmp) * CMP_STRIDE
    ss = np.arange(n_sel) * SEL_BLOCK
    ov = np.clip(np.minimum(cs[:, None] + CMP_LEN, ss[None, :] + SEL_BLOCK)
                 - np.maximum(cs[:, None], ss[None, :]), 0, None).astype(np.float32)
    ov[ncmp - 1, :] = 0.0
    return np.pad(ov, ((0, 0), (0, LANES - n_sel)))


def _even_layer(x2d, norm_w, w_in, b_in, f_bias, conv_w, conv_b, head_norm, q_norm, k_norm,
                ck_pos, ck_w1, ck_w2, cv_pos, cv_w1, cv_w2, w_out, batch, seq):
    assert seq // SEL_BLOCK <= LANES and seq % KV_TILE == 0
    proj, small = _norm_proj(x2d, norm_w, _even_relayout(w_in).astype(BF16), _even_relayout(b_in),
                             (C_SMALL, C_SMALL + LANES), tn=EV_COLS // 3)

    f_bias_row = jnp.zeros((1, LANES), F32).at[0, SM_F:SM_F + M_HEADS].set(f_bias)
    ya = _mlstm(proj, small, conv_w, conv_b, f_bias_row, head_norm, batch, seq)

    cs, s1, s2 = (jnp.asarray(a) for a in _rope_tables(np.arange(seq)))
    qn_row = jnp.concatenate([q_norm, q_norm]).reshape(1, LANES)
    kn_rows = jnp.concatenate([k_norm, k_norm], axis=1)
    qt, ka, kwd, vst, vwt, kc, vc, gtt = _nsa_prep(proj, small, cs, s1, s2, qn_row, kn_rows, batch, seq)

    ncmp = seq // CMP_STRIDE
    ccs, cs1, cs2 = (jnp.asarray(a) for a in _rope_tables(np.arange(ncmp) * CMP_STRIDE + CMP_LEN - 1))
    kcd, vct = _nsa_compress(
        kc.reshape(batch * ncmp, CMP_STRIDE * LANES), vc.reshape(batch * ncmp, CMP_STRIDE * LANES),
        _compress_params(ck_pos, ck_w1, ck_w2), _compress_params(cv_pos, cv_w1, cv_w2),
        ccs, cs1, cs2, kn_rows, batch, ncmp)

    ovt = jnp.asarray(_overlap_matrix(seq).T).astype(BF16)
    yb = _nsa_attn(proj, qt, gtt, ka, vst, kwd, vwt, kcd, vct, ovt, batch, seq)

    w_o = w_out.astype(BF16)
    return _out_proj2(ya, yb, w_o[:M_W], w_o[M_W:], x2d)


def _odd_layer(x2d, norm_w, w_in, b_in, lb, head_norm, w_out, batch, seq):
    proj, fgate = _norm_proj(x2d, norm_w, w_in.astype(BF16), b_in, (0, H_W), tn=H_W)
    y = _hgrn(proj, fgate, lb.reshape(1, H_W), head_norm.reshape(1, H_W), batch, seq)
    return _out_proj1(y, w_out.astype(BF16), x2d)


def kernel(x, norm_w, ev_w_in, ev_b_in, mlstm_f_bias, mlstm_conv_w, mlstm_conv_b, mlstm_head_norm, nsa_q_norm, nsa_k_norm, cmp_k_pos, cmp_k_w1, cmp_k_w2, cmp_v_pos, cmp_v_w1, cmp_v_w2, ev_w_out, od_w_in, od_b_in, hgrn_lb_logits, hgrn_head_norm, od_w_out):
    batch, seq, d = x.shape
    depth = norm_w.shape[0]
    lbs = jnp.cumsum(jax.nn.softmax(hgrn_lb_logits.astype(F32), axis=0), axis=0)
    x2d = x.reshape(batch * seq, d)
    for layer in range(depth):
        j = layer // 2
        if layer % 2 == 0:
            x2d = _even_layer(x2d, norm_w[layer], ev_w_in[j], ev_b_in[j], mlstm_f_bias[j], mlstm_conv_w[j],
                              mlstm_conv_b[j], mlstm_head_norm[j], nsa_q_norm[j], nsa_k_norm[j], cmp_k_pos[j],
                              cmp_k_w1[j], cmp_k_w2[j], cmp_v_pos[j], cmp_v_w1[j], cmp_v_w2[j], ev_w_out[j],
                              batch, seq)
        else:
            x2d = _odd_layer(x2d, norm_w[layer], od_w_in[j], od_b_in[j], lbs[j], hgrn_head_norm[j], od_w_out[j],
                             batch, seq)
    return x2d.reshape(batch, seq, d)
```

```python
import functools

import numpy as np
import jax
import jax.numpy as jnp
from jax import lax
from jax.experimental import pallas as pl
from jax.experimental.pallas import tpu as pltpu

F32 = jnp.float32
BF16 = jnp.bfloat16

LANES = 128
EPS = 1e-6
NEG = -1e30
VMEM_LIMIT_BYTES = 56 * 1024 * 1024

M_HEADS = 4
M_DIM = 128
M_W = M_HEADS * M_DIM
M_CHUNK = 128
CONV_K = 4
N_HEADS = 8
N_GROUPS = 2
N_HPG = N_HEADS // N_GROUPS
N_DIM = 64
N_W = N_HEADS * N_DIM
N_KV_W = N_GROUPS * N_DIM
CMP_LEN = 32
CMP_STRIDE = 16
CMP_HIDDEN = 128
CMP_CHUNK = 128
SEL_BLOCK = 64
SEL_SHIFT = 6
SEL_TOPK = 16
WINDOW = 512
Q_BLOCK = 128
ROPE_THETA = 500000.0
ROPE_DIM = N_DIM // 4
ROPE_HALF = ROPE_DIM // 2
KV_TILE = 512
V_ROWS = 80
LOG2E = 1.4426950408889634
H_HEADS = 8
H_DIM = 128
H_W = H_HEADS * H_DIM
H_TILE = 128
H_LAG = 4

C_MQ, C_MK, C_MV, C_MO, C_MZ = 0, 512, 1024, 1536, 2048
C_NQ = 2560
C_KC, C_VC, C_KS, C_VS = 3072, 3200, 3328, 3456
C_KW, C_VW, C_SMALL = 3584, 3712, 3840
C_NZ = 4096
EV_COLS = 4608
SM_I, SM_F, SM_G = 0, 4, 8


def _cparams(*sem):
    return pltpu.CompilerParams(dimension_semantics=sem, vmem_limit_bytes=VMEM_LIMIT_BYTES)


def _dot(a, b):
    return jnp.dot(a, b, preferred_element_type=F32)


def _dot_nt(a, b):
    return lax.dot_general(a, b, (((1,), (1,)), ((), ())), preferred_element_type=F32)


def _dot_tn(a, b):
    return lax.dot_general(a, b, (((0,), (0,)), ((), ())), preferred_element_type=F32)


def _sigmoid(x):
    return 1.0 / (1.0 + jnp.exp(-x))


def _silu(x):
    return x * _sigmoid(x)


def _split3(x):
    hi = x.astype(BF16)
    r = x - hi.astype(F32)
    mid = r.astype(BF16)
    lo = (r - mid.astype(F32)).astype(BF16)
    return hi, mid, lo


def _dot01(m01, x):
    hi, mid, lo = _split3(x)
    return _dot(m01, hi) + _dot(m01, mid) + _dot(m01, lo)


def _norm_proj_kernel(x_ref, nw_ref, w_ref, b_ref, o_ref, g_ref, h_ref, *, g_tiles, g_off):
    j = pl.program_id(1)

    @pl.when(j == 0)
    def _():
        x = x_ref[...]
        ms = jnp.mean(x * x, axis=-1, keepdims=True)
        h_ref[...] = (x * lax.rsqrt(ms + EPS) * nw_ref[...]).astype(BF16)

    acc = _dot(h_ref[...], w_ref[...]) + b_ref[...]
    o_ref[...] = acc.astype(o_ref.dtype)

    @pl.when((j >= g_tiles[0]) & (j < g_tiles[1]))
    def _():
        g_ref[...] = acc[:, g_off:g_off + g_ref.shape[1]]


def _norm_proj(x2d, norm_w, w_bf16, bias, g_cols, tm=1024, tn=512):
    t, d = x2d.shape
    n = w_bf16.shape[1]
    g_w = min(g_cols[1] - g_cols[0], tn)
    g_tiles = (g_cols[0] // tn, -(-g_cols[1] // tn))
    return pl.pallas_call(
        functools.partial(_norm_proj_kernel, g_tiles=g_tiles, g_off=g_cols[0] % tn),
        grid=(t // tm, n // tn),
        in_specs=[
            pl.BlockSpec((tm, d), lambda i, j: (i, 0)),
            pl.BlockSpec((1, d), lambda i, j: (0, 0)),
            pl.BlockSpec((d, tn), lambda i, j: (0, j)),
            pl.BlockSpec((1, tn), lambda i, j: (0, j)),
        ],
        out_specs=[pl.BlockSpec((tm, tn), lambda i, j: (i, j)),
                   pl.BlockSpec((tm, g_w), lambda i, j: (i, jnp.clip(j - g_tiles[0], 0, g_tiles[1] - g_tiles[0] - 1)))],
        out_shape=[jax.ShapeDtypeStruct((t, n), BF16), jax.ShapeDtypeStruct((t, g_cols[1] - g_cols[0]), F32)],
        scratch_shapes=[pltpu.VMEM((tm, d), BF16)],
        compiler_params=_cparams("parallel", "arbitrary"),
        name="norm_proj",
    )(x2d, norm_w.reshape(1, d), w_bf16, bias.reshape(1, n))


def _out_proj2_kernel(ya_ref, yb_ref, wa_ref, wb_ref, x_ref, o_ref):
    o_ref[...] = x_ref[...] + _dot(ya_ref[...], wa_ref[...]) + _dot(yb_ref[...], wb_ref[...])


def _out_proj2(ya, yb, wa, wb, x2d, tm=1024):
    t, d = x2d.shape
    ka, kb = ya.shape[1], yb.shape[1]
    return pl.pallas_call(
        _out_proj2_kernel,
        grid=(t // tm,),
        in_specs=[
            pl.BlockSpec((tm, ka), lambda i: (i, 0)),
            pl.BlockSpec((tm, kb), lambda i: (i, 0)),
            pl.BlockSpec((ka, d), lambda i: (0, 0)),
            pl.BlockSpec((kb, d), lambda i: (0, 0)),
            pl.BlockSpec((tm, d), lambda i: (i, 0)),
        ],
        out_specs=pl.BlockSpec((tm, d), lambda i: (i, 0)),
        out_shape=jax.ShapeDtypeStruct((t, d), F32),
        compiler_params=_cparams("parallel"),
        name="out_proj2",
    )(ya, yb, wa, wb, x2d)


def _out_proj1_kernel(y_ref, w_ref, x_ref, o_ref):
    o_ref[...] = x_ref[...] + _dot(y_ref[...], w_ref[...])


def _out_proj1(y, w, x2d, tm=1024):
    t, d = x2d.shape
    k = y.shape[1]
    return pl.pallas_call(
        _out_proj1_kernel,
        grid=(t // tm,),
        in_specs=[
            pl.BlockSpec((tm, k), lambda i: (i, 0)),
            pl.BlockSpec((k, d), lambda i: (0, 0)),
            pl.BlockSpec((tm, d), lambda i: (i, 0)),
        ],
        out_specs=pl.BlockSpec((tm, d), lambda i: (i, 0)),
        out_shape=jax.ShapeDtypeStruct((t, d), F32),
        compiler_params=_cparams("parallel"),
        name="out_proj1",
    )(y, w, x2d)


def _mlstm_kernel(q_ref, k_ref, v_ref, o_ref, z_ref, sm_ref, cw_ref, cb_ref, fb_ref, hn_ref,
                  out_ref, xbuf, c_ref, n_ref, m_ref):
    L, D = M_CHUNK, M_DIM

    XR = xbuf.shape[0]

    @pl.when(pl.program_id(1) == 0)
    def _():
        xbuf[...] = jnp.zeros_like(xbuf)
        c_ref[...] = jnp.zeros_like(c_ref)
        n_ref[...] = jnp.zeros_like(n_ref)
        m_ref[...] = jnp.zeros_like(m_ref)

    xbuf[16:16 + L, 0:M_W] = q_ref[...]
    xbuf[16:16 + L, M_W:2 * M_W] = k_ref[...]
    x2 = xbuf[...]
    t_i = lax.broadcasted_iota(jnp.int32, (L, XR), 0)
    r_i = lax.broadcasted_iota(jnp.int32, (L, XR), 1)
    acc = cb_ref[...] + cw_ref[CONV_K - 1:CONV_K, :] * x2[16:16 + L].astype(F32)
    for d in range(1, CONV_K):
        shift = jnp.where(r_i == t_i + 16 - d, 1.0, 0.0).astype(BF16)
        acc = acc + cw_ref[CONV_K - 1 - d:CONV_K - d, :] * _dot(shift, x2)
    qk = _silu(acc)
    xbuf[0:16, :] = xbuf[L:L + 16, :]

    row = lax.broadcasted_iota(jnp.int32, (L, L), 0)
    col = lax.broadcasted_iota(jnp.int32, (L, L), 1)
    causal = col <= row
    fpre = sm_ref[...] + fb_ref[...]
    lane = lax.broadcasted_iota(jnp.int32, (L, LANES), 1)
    is_f = (lane >= SM_F) & (lane < SM_F + M_HEADS)
    gates = jnp.where(is_f, jnp.minimum(fpre, 0.0) - jnp.log(1.0 + jnp.exp(-jnp.abs(fpre))), fpre)
    bcols = _dot01(jnp.where(causal, 1.0, 0.0).astype(BF16), gates)
    gates_t = gates.T
    bcols_t = bcols.T

    heads = []
    for h in range(M_HEADS):
        hs = slice(h * D, (h + 1) * D)
        q = qk[:, hs]
        k = qk[:, M_W + h * D:M_W + (h + 1) * D] * (D ** -0.5)
        i_col = gates[:, SM_I + h:SM_I + h + 1]
        i_row = gates_t[SM_I + h:SM_I + h + 1, :]
        bcum_col = bcols[:, SM_F + h:SM_F + h + 1]
        bcum_row = bcols_t[SM_F + h:SM_F + h + 1, :]
        m_prev = m_ref[h]
        dmat = jnp.where(causal, bcum_col - bcum_row + i_row, -jnp.inf)
        inter = bcum_col + m_prev
        m_row = jnp.maximum(jnp.max(dmat, axis=1, keepdims=True), inter)
        b_last = bcum_col[L - 1:L, :]
        dl = b_last - bcum_col + i_col
        m_new = jnp.maximum(b_last + m_prev, jnp.max(dl, axis=0, keepdims=True))
        wkk = jnp.exp(dl - m_new) * k
        heads.append(dict(
            hs=hs, q=q, qb=q.astype(BF16), kb=k.astype(BF16), vb=v_ref[:, hs].astype(BF16), wkk=wkk,
            e=jnp.exp(dmat - m_row), w_inter=jnp.exp(inter - m_row), floor=jnp.exp(-m_row),
            decay=jnp.exp(b_last + m_prev - m_new), m_new=m_new))
    s_qk = [_dot_nt(t["qb"], t["kb"]) for t in heads]
    s_qc = [_dot(t["qb"], c_ref[h].astype(BF16)) for h, t in enumerate(heads)]
    s_kv = [_dot_tn(t["wkk"].astype(BF16), t["vb"]) for t in heads]
    smats = [s_qk[h] * t["e"] for h, t in enumerate(heads)]
    s_sv = [_dot(smats[h].astype(BF16), t["vb"]) for h, t in enumerate(heads)]
    for h, t in enumerate(heads):
        hs = t["hs"]
        num = s_sv[h] + t["w_inter"] * s_qc[h]
        den = (jnp.sum(smats[h], axis=1, keepdims=True)
               + t["w_inter"] * jnp.sum(t["q"] * n_ref[h], axis=1, keepdims=True))
        hh = num / jnp.maximum(jnp.abs(den), t["floor"])
        c_ref[h] = t["decay"] * c_ref[h] + s_kv[h]
        n_ref[h] = t["decay"] * n_ref[h] + jnp.sum(t["wkk"], axis=0, keepdims=True)
        m_ref[h] = t["m_new"]
        hn = hh * lax.rsqrt(jnp.mean(hh * hh, axis=1, keepdims=True) + EPS) * hn_ref[:, hs]
        gate = _sigmoid(o_ref[:, hs].astype(F32)) * _silu(z_ref[:, hs].astype(F32))
        out_ref[:, hs] = (hn * gate).astype(out_ref.dtype)


def _mlstm(proj, small, conv_w, conv_b, f_bias_row, head_norm, batch, seq):
    L = M_CHUNK
    nc = seq // L
    cb = lambda off: pl.BlockSpec((L, M_W), lambda b, c, off=off: (b * nc + c, off // M_W))
    return pl.pallas_call(
        _mlstm_kernel,
        grid=(batch, nc),
        in_specs=[
            cb(C_MQ), cb(C_MK), cb(C_MV), cb(C_MO), cb(C_MZ),
            pl.BlockSpec((L, LANES), lambda b, c: (b * nc + c, 0)),
            pl.BlockSpec((CONV_K, 2 * M_W), lambda b, c: (0, 0)),
            pl.BlockSpec((1, 2 * M_W), lambda b, c: (0, 0)),
            pl.BlockSpec((1, LANES), lambda b, c: (0, 0)),
            pl.BlockSpec((1, M_W), lambda b, c: (0, 0)),
        ],
        out_specs=pl.BlockSpec((L, M_W), lambda b, c: (b * nc + c, 0)),
        out_shape=jax.ShapeDtypeStruct((batch * seq, M_W), BF16),
        scratch_shapes=[
            pltpu.VMEM((2 * L, 2 * M_W), BF16),
            pltpu.VMEM((M_HEADS, M_DIM, M_DIM), F32),
            pltpu.VMEM((M_HEADS, 1, M_DIM), F32),
            pltpu.VMEM((M_HEADS, 1, 1), F32),
        ],
        compiler_params=_cparams("parallel", "arbitrary"),
        name="mlstm",
    )(proj, proj, proj, proj, proj, small, conv_w, conv_b.reshape(1, -1), f_bias_row, head_norm.reshape(1, -1))


def _group_rms(x, w_row):
    lane = lax.broadcasted_iota(jnp.int32, x.shape, 1)
    left = lane < N_DIM
    sq = x * x
    tot = jnp.sum(sq, axis=1, keepdims=True)
    lft = jnp.sum(jnp.where(left, sq, 0.0), axis=1, keepdims=True)
    ms = jnp.where(left, lft, tot - lft) * (1.0 / N_DIM)
    return x * lax.rsqrt(ms + EPS) * w_row


def _rope(x, cs, s1, s2):
    return x * cs + pltpu.roll(x, ROPE_HALF, 1) * s1 + pltpu.roll(x, LANES - ROPE_HALF, 1) * s2


def _dup_halves(x):
    lane = lax.broadcasted_iota(jnp.int32, x.shape, 1)
    sw = pltpu.roll(x, N_DIM, 1)
    left = lane < N_DIM
    return jnp.where(left, x, sw), jnp.where(left, sw, x)


def _nsa_prep_kernel(q_ref, kv_ref, kw_ref, sm_ref, cs_ref, s1_ref, s2_ref, qn_ref, kn_ref,
                     qt_ref, ka_ref, kwd_ref, vst_ref, vwt_ref, kc_ref, vc_ref, gtt_ref,
                     *, tp, seq):
    cs, s1, s2 = cs_ref[...], s1_ref[...], s2_ref[...]
    for j in range(N_W // LANES):
        sl = slice(j * LANES, (j + 1) * LANES)
        qn = _group_rms(q_ref[:, sl].astype(F32), qn_ref[...])
        qt_ref[sl, :] = (_rope(qn, cs, s1, s2) * (N_DIM ** -0.5 * LOG2E)).T.astype(BF16)
    kv = kv_ref[...].astype(F32)
    kc_ref[...] = kv[:, 0:LANES]
    vc_ref[...] = kv[:, LANES:2 * LANES]
    ks = _rope(_group_rms(kv[:, 2 * LANES:3 * LANES], kn_ref[1:2, :]), cs, s1, s2)
    k0, k1 = _dup_halves(ks)
    pos0 = (pl.program_id(0) * tp) % seq
    pos = pos0 + lax.broadcasted_iota(jnp.int32, (tp, LANES), 0)
    blk = lax.broadcasted_iota(jnp.int32, (tp, LANES), 1)
    onehot = jnp.where(jnp.right_shift(pos, SEL_SHIFT) == blk, 1.0, 0.0).astype(BF16)
    ka_ref[:, 0:LANES] = k0.astype(BF16)
    ka_ref[:, LANES:2 * LANES] = onehot
    ka_ref[:, 2 * LANES:3 * LANES] = k1.astype(BF16)
    ka_ref[:, 3 * LANES:4 * LANES] = onehot
    vst = kv[:, 3 * LANES:4 * LANES].T
    fill = (lax.broadcasted_iota(jnp.int32, (V_ROWS - N_DIM, Q_BLOCK), 0) == 0).astype(BF16)
    for c in range(tp // Q_BLOCK):
        for g in range(N_GROUPS):
            vst_ref[c, g * V_ROWS:g * V_ROWS + N_DIM, :] = (
                vst[g * N_DIM:(g + 1) * N_DIM, c * Q_BLOCK:(c + 1) * Q_BLOCK].astype(BF16))
            vst_ref[c, g * V_ROWS + N_DIM:(g + 1) * V_ROWS, :] = fill
    kwb = kw_ref[...].astype(F32)
    kw = _rope(_group_rms(kwb[:, 0:LANES], kn_ref[2:3, :]), cs, s1, s2)
    w0, w1 = _dup_halves(kw)
    kwd_ref[:, 0:LANES] = w0.astype(BF16)
    kwd_ref[:, LANES:2 * LANES] = w1.astype(BF16)
    vwt_ref[...] = kwb[:, LANES:2 * LANES].T.astype(BF16)
    gtt_ref[...] = _sigmoid(sm_ref[...]).T


def _nsa_prep(proj, small, cs, s1, s2, qn_row, kn_rows, batch, seq):
    tp = KV_TILE
    t = batch * seq
    nsb = seq // tp
    rb = lambda w, off: pl.BlockSpec((tp, w), lambda i, off=off, w=w: (i, off // w))
    tab = pl.BlockSpec((tp, LANES), lambda i: (i % nsb, 0))
    o = lambda w: pl.BlockSpec((tp, w), lambda i: (i, 0))
    ot = lambda r: pl.BlockSpec((r, tp), lambda i: (0, i))
    sds = lambda w, dt: jax.ShapeDtypeStruct((t, w), dt)
    return pl.pallas_call(
        functools.partial(_nsa_prep_kernel, tp=tp, seq=seq),
        grid=(t // tp,),
        in_specs=[rb(512, C_NQ), rb(512, C_KC), rb(512, C_KW), o(LANES), tab, tab, tab,
                  pl.BlockSpec((1, LANES), lambda i: (0, 0)), pl.BlockSpec((3, LANES), lambda i: (0, 0))],
        out_specs=[ot(N_W), o(512), o(256),
                   pl.BlockSpec((tp // Q_BLOCK, N_GROUPS * V_ROWS, Q_BLOCK), lambda i: (i, 0, 0)),
                   ot(LANES), o(128), o(128), ot(LANES)],
        out_shape=[jax.ShapeDtypeStruct((N_W, t), BF16), sds(512, BF16), sds(256, BF16),
                   jax.ShapeDtypeStruct((t // Q_BLOCK, N_GROUPS * V_ROWS, Q_BLOCK), BF16),
                   jax.ShapeDtypeStruct((LANES, t), BF16),
                   sds(128, F32), sds(128, F32), jax.ShapeDtypeStruct((LANES, t), F32)],
        compiler_params=_cparams("parallel"),
        name="nsa_prep",
    )(proj, proj, proj, small, cs, s1, s2, qn_row, kn_rows)


def _gelu_tanh(x):
    return 0.5 * x * (1.0 + jnp.tanh(0.7978845608028654 * (x + 0.044715 * x * x * x)))


def _compress_mlp(x_ref, pa_ref, pb_ref, wa_ref, wb_ref, w2_ref, hb_ref):
    n = x_ref.shape[0]
    x = x_ref[...]
    ha = _dot((x + pa_ref[...]).astype(BF16), wa_ref[...])
    hb_ref[0:n, :] = _dot((x + pb_ref[...]).astype(BF16), wb_ref[...])
    hb_ref[n:n + 8, :] = jnp.zeros((8, hb_ref.shape[1]), F32)
    hid = ha + hb_ref[1:n + 1, :]
    return _dot(_gelu_tanh(hid).astype(BF16), w2_ref[...])


def _nsa_compress_kernel(kx_ref, vx_ref, kpa_ref, kpb_ref, vpa_ref, vpb_ref, kwa_ref, kwb_ref, kw2_ref,
                         vwa_ref, vwb_ref, vw2_ref, cs_ref, s1_ref, s2_ref, kn_ref,
                         kcd_ref, vct_ref, hb_ref):
    n = kx_ref.shape[0]
    rowi = lax.broadcasted_iota(jnp.int32, (n, LANES), 0)
    live = rowi < n - 1
    kc = _compress_mlp(kx_ref, kpa_ref, kpb_ref, kwa_ref, kwb_ref, kw2_ref, hb_ref)
    kc = _rope(_group_rms(kc, kn_ref[0:1, :]), cs_ref[...], s1_ref[...], s2_ref[...])
    kc = jnp.where(live, kc, 0.0)
    c0, c1 = _dup_halves(kc)
    kcd_ref[:, 0:LANES] = c0.astype(BF16)
    kcd_ref[:, LANES:2 * LANES] = c1.astype(BF16)
    vc = _compress_mlp(vx_ref, vpa_ref, vpb_ref, vwa_ref, vwb_ref, vw2_ref, hb_ref)
    vc = jnp.where(live, vc, 0.0)
    vct_ref[0] = vc.T.astype(BF16)


def _nsa_compress(kx, vx, kp, vp, cs, s1, s2, kn_rows, batch, nrow):
    wide = CMP_STRIDE * LANES
    full = lambda a: pl.BlockSpec(a.shape, lambda b: (0,) * a.ndim)
    xs = pl.BlockSpec((nrow, wide), lambda b: (b, 0))
    args = (kx, vx, kp[0], kp[1], vp[0], vp[1], kp[2], kp[3], kp[4], vp[2], vp[3], vp[4], cs, s1, s2, kn_rows)
    return pl.pallas_call(
        _nsa_compress_kernel,
        grid=(batch,),
        in_specs=[xs, xs] + [full(a) for a in args[2:]],
        out_specs=[pl.BlockSpec((nrow, 2 * LANES), lambda b: (b, 0)),
                   pl.BlockSpec((1, LANES, nrow), lambda b: (b, 0, 0))],
        out_shape=[jax.ShapeDtypeStruct((batch * nrow, 2 * LANES), BF16),
                   jax.ShapeDtypeStruct((batch, LANES, nrow), BF16)],
        scratch_shapes=[pltpu.VMEM((nrow + 8, 2 * CMP_HIDDEN), F32)],
        compiler_params=_cparams("parallel"),
        name="nsa_compress",
    )(*args)


def _compress_params(pos, w1, w2):
    def expand(w1_half, pos_half):
        w = w1_half.reshape(CMP_STRIDE, N_DIM, CMP_HIDDEN)
        z = jnp.zeros_like(w)
        g0 = jnp.concatenate([w, z], axis=-1)
        g1 = jnp.concatenate([z, w], axis=-1)
        wx = jnp.stack([g0, g1], axis=1).reshape(CMP_STRIDE * LANES, 2 * CMP_HIDDEN)
        px = jnp.concatenate([pos_half, pos_half], axis=-1).reshape(1, CMP_STRIDE * LANES)
        return wx.astype(BF16), px.astype(F32)

    half = CMP_STRIDE * N_DIM
    wa, pa = expand(w1[:half], pos[:CMP_STRIDE])
    wb, pb = expand(w1[half:], pos[CMP_STRIDE:])
    z2 = jnp.zeros_like(w2)
    w2x = jnp.concatenate([jnp.concatenate([w2, z2], axis=1), jnp.concatenate([z2, w2], axis=1)], axis=0)
    return pa, pb, wa, wb, w2x.astype(BF16)


def _nsa_attn_kernel(qt_ref, gtt_ref, z_ref, ka_ref, vst_ref,
                     kw0, kw1, kw2, kw3, kw4, wt0, wt1, wt2, wt3, wt4,
                     kcd_ref, vct_ref, ovt_ref, out_ref, qa_ref, acc_ref, *, ncmp):
    QB = Q_BLOCK
    NP = N_HEADS // 2
    PW = 2 * QB
    i = pl.program_id(1)
    t0 = i * QB
    kw_refs = (kw0, kw1, kw2, kw3, kw4)
    vwt_refs = (wt0, wt1, wt2, wt3, wt4)
    nwt = len(kw_refs)

    sub = lax.broadcasted_iota(jnp.int32, (QB, QB), 0)
    lane = lax.broadcasted_iota(jnp.int32, (QB, QB), 1)
    top = sub < N_DIM
    t_row = t0 + lax.broadcasted_iota(jnp.int32, (1, QB), 1)
    t_row2 = jnp.concatenate([t_row, t_row], axis=1)

    def pair_q(pr):
        slab = qt_ref[pr * LANES:(pr + 1) * LANES, :]
        zero = jnp.zeros_like(slab)
        return jnp.concatenate([jnp.where(top, slab, zero), jnp.where(top, zero, slab)], axis=1)

    blk_f = sub.astype(F32)
    cur = jnp.right_shift(t_row, SEL_SHIFT)
    grp = lambda pr: pr // (N_HPG // 2)
    for pr in range(NP):
        qa_ref[pr, 0:LANES, :] = pair_q(pr)

    def compressed(n):
        nidx = lax.broadcasted_iota(jnp.int32, (n, PW), 0)
        ok = (nidx * CMP_STRIDE + (CMP_LEN - 1) <= t_row2) & (nidx < ncmp - 1)
        s_cmp = [_dot(kcd_ref[0:n, grp(pr) * LANES:(grp(pr) + 1) * LANES], qa_ref[pr, 0:LANES, :])
                 for pr in range(NP)]
        p_cmp = []
        for pr in range(NP):
            s = jnp.where(ok, s_cmp[pr], NEG)
            mx = jnp.max(s, axis=0, keepdims=True)
            p = jnp.where(ok, jnp.exp2(s - mx), 0.0)
            den = jnp.sum(p, axis=0, keepdims=True)
            p_cmp.append(p * (1.0 / jnp.maximum(den, 1e-30)))
        outs = [_dot(vct_ref[0, grp(pr) * N_DIM:(grp(pr) + 1) * N_DIM, 0:n], p_cmp[pr].astype(BF16))
                for pr in range(NP)]
        for g in range(N_GROUPS):
            psum = jnp.zeros((n, QB), F32)
            for pp in range(N_HPG // 2):
                p = p_cmp[g * (N_HPG // 2) + pp]
                psum = psum + p[:, 0:QB] + p[:, QB:PW]
            p_hi = psum.astype(BF16)
            p_lo = (psum - p_hi.astype(F32)).astype(BF16)
            outs.append(_dot(ovt_ref[:, 0:n], p_hi) + _dot(ovt_ref[:, 0:n], p_lo))
        return tuple(outs)

    n_opts = list(range(CMP_CHUNK, ncmp + 1, CMP_CHUNK))
    need = (t0 + QB - CMP_LEN) // CMP_STRIDE + 1
    outs = lax.switch(jnp.clip((need - 1) // CMP_CHUNK, 0, len(n_opts) - 1),
                      [functools.partial(compressed, n) for n in n_opts])
    o_cmp, imps = outs[:NP], outs[NP:]
    forced = (sub == 0) | (sub == cur) | (sub == cur - 1)
    works = [jnp.where(sub > cur, NEG, jnp.where(forced, -NEG, imp)) for imp in imps]
    sels = [jnp.zeros((QB, QB), jnp.bool_) for _ in range(N_GROUPS)]
    for _ in range(SEL_TOPK):
        for g in range(N_GROUPS):
            mx = jnp.max(works[g], axis=0, keepdims=True)
            idx = jnp.min(jnp.where(works[g] == mx, blk_f, float(LANES)), axis=0, keepdims=True)
            pick = blk_f == idx
            sels[g] = sels[g] | pick
            works[g] = jnp.where(pick, -jnp.inf, works[g])
    for g in range(N_GROUPS):
        neg = jnp.where(sels[g] & (sub <= cur), 0.0, NEG).astype(BF16)
        neg2 = jnp.concatenate([neg, neg], axis=1)
        for pp in range(N_HPG // 2):
            qa_ref[g * (N_HPG // 2) + pp, LANES:2 * LANES, :] = neg2

    CH = QB
    above = jnp.concatenate([sub > lane, sub > lane], axis=1)

    def sel_scores(cj):
        k0 = pl.multiple_of(cj * CH, CH)
        return [_dot(ka_ref[pl.ds(k0, CH), 2 * grp(pr) * LANES:2 * (grp(pr) + 1) * LANES], qa_ref[pr])
                for pr in range(NP)]

    def sel_softmax(ss, ms, masked):
        new, alphas, ps = [], [], []
        for pr in range(NP):
            s = jnp.where(above, NEG, ss[pr]) if masked else ss[pr]
            m_new = jnp.maximum(ms[pr], jnp.max(s, axis=0, keepdims=True))
            alphas.append(jnp.exp2(ms[pr] - m_new))
            ps.append(jnp.exp2(s - m_new).astype(BF16))
            new.append(m_new)
        return tuple(new), alphas, ps

    def sel_values(cj, alphas, ps):
        for pr in range(NP):
            g = grp(pr)
            acc_ref[pr] = alphas[pr] * acc_ref[pr] + _dot(vst_ref[cj, g * V_ROWS:(g + 1) * V_ROWS, :], ps[pr])

    def sel_chunk(cj, ms, masked):
        ms, alphas, ps = sel_softmax(sel_scores(cj), ms, masked)
        sel_values(cj, alphas, ps)
        return ms

    def sel_tile(jt, ms, nc):
        c0 = jt * nc
        k0 = pl.multiple_of(c0 * CH, CH)
        big = [_dot(ka_ref[pl.ds(k0, nc * CH), 2 * grp(pr) * LANES:2 * (grp(pr) + 1) * LANES], qa_ref[pr])
               for pr in range(NP)]
        for c in range(nc):
            ss = [b[c * CH:(c + 1) * CH] for b in big]
            ms, alphas, ps = sel_softmax(ss, ms, False)
            sel_values(c0 + c, alphas, ps)
        return ms

    acc_ref[...] = jnp.zeros_like(acc_ref)
    ms = tuple(jnp.full((1, PW), 2.0 * NEG, F32) for _ in range(NP))
    done = 0
    for nc in (2 * KV_TILE // CH, KV_TILE // CH):
        n_run = (i - done) // nc
        ms = lax.fori_loop(done // nc, done // nc + n_run, functools.partial(sel_tile, nc=nc), ms)
        done = done + n_run * nc
    ms = lax.fori_loop(done, i, lambda cj, ms: sel_chunk(cj, ms, False), ms)
    sel_chunk(i, ms, True)
    o_sel = [acc_ref[pr, 0:N_DIM, :] * (1.0 / acc_ref[pr, N_DIM:N_DIM + 1, :]) for pr in range(NP)]

    s_win = [[_dot(kw_refs[c][:, grp(pr) * LANES:(grp(pr) + 1) * LANES], qa_ref[pr, 0:LANES, :])
              for c in range(nwt)] for pr in range(NP)]
    p_win, d_win = [], []
    for pr in range(NP):
        ss = []
        for c in range(nwt):
            ok = (i - (nwt - 1) + c) >= 0
            if c == 0:
                ok = ok & above
            elif c == nwt - 1:
                ok = ok & jnp.logical_not(above)
            ss.append(jnp.where(ok, s_win[pr][c], NEG))
        mx = ss[0].max(axis=0, keepdims=True)
        for c in range(1, nwt):
            mx = jnp.maximum(mx, ss[c].max(axis=0, keepdims=True))
        ps = [jnp.exp2(ss[c] - mx) for c in range(nwt)]
        den = ps[0].sum(axis=0, keepdims=True)
        for c in range(1, nwt):
            den = den + ps[c].sum(axis=0, keepdims=True)
        p_win.append([p.astype(BF16) for p in ps])
        d_win.append(den)
    o_win = []
    for pr in range(NP):
        g = grp(pr)
        acc = _dot(vwt_refs[0][g * N_DIM:(g + 1) * N_DIM, :], p_win[pr][0])
        for c in range(1, nwt):
            acc = acc + _dot(vwt_refs[c][g * N_DIM:(g + 1) * N_DIM, :], p_win[pr][c])
        o_win.append(acc * (1.0 / d_win[pr]))

    for pr in range(NP):
        halves = []
        for par in range(2):
            h = 2 * pr + par
            cs = slice(par * QB, (par + 1) * QB)
            gc = gtt_ref[SM_G + 3 * h:SM_G + 3 * h + 1, :]
            gs = gtt_ref[SM_G + 3 * h + 1:SM_G + 3 * h + 2, :]
            gw = gtt_ref[SM_G + 3 * h + 2:SM_G + 3 * h + 3, :]
            halves.append(gc * o_cmp[pr][:, cs] + gs * o_sel[pr][:, cs] + gw * o_win[pr][:, cs])
        y = jnp.concatenate(halves, axis=0).T
        sl = slice(pr * LANES, (pr + 1) * LANES)
        out_ref[:, sl] = (y * _silu(z_ref[:, sl].astype(F32))).astype(out_ref.dtype)


def _nsa_attn(proj, qt, gtt, ka, vst, kwd, vwt, kcd, vct, ovt, batch, seq):
    QB = Q_BLOCK
    nqb = seq // QB
    ncmp = seq // CMP_STRIDE
    ntile = seq // KV_TILE
    nwt = WINDOW // QB + 1
    row = lambda w: pl.BlockSpec((QB, w), lambda b, i: (b * nqb + i, 0))
    colb = lambda r: pl.BlockSpec((r, QB), lambda b, i: (0, b * nqb + i))
    wblk = lambda b, i, c: b * nqb + jnp.maximum(i - (nwt - 1) + c, 0)
    in_specs = ([colb(N_W), colb(LANES), pl.BlockSpec((QB, N_W), lambda b, i: (b * nqb + i, C_NZ // N_W)),
                 pl.BlockSpec((seq, 4 * LANES), lambda b, i: (b, 0)),
                 pl.BlockSpec((nqb, N_GROUPS * V_ROWS, QB), lambda b, i: (b, 0, 0))]
                + [pl.BlockSpec((QB, 2 * LANES), lambda b, i, c=c: (wblk(b, i, c), 0)) for c in range(nwt)]
                + [pl.BlockSpec((LANES, QB), lambda b, i, c=c: (0, wblk(b, i, c))) for c in range(nwt)]
                + [pl.BlockSpec((ncmp, 2 * LANES), lambda b, i: (b, 0)),
                   pl.BlockSpec((1, LANES, ncmp), lambda b, i: (b, 0, 0)),
                   pl.BlockSpec((LANES, ncmp), lambda b, i: (0, 0))])
    return pl.pallas_call(
        functools.partial(_nsa_attn_kernel, ncmp=ncmp),
        grid=(batch, nqb),
        in_specs=in_specs,
        out_specs=row(N_W),
        out_shape=jax.ShapeDtypeStruct((batch * seq, N_W), BF16),
        scratch_shapes=[pltpu.VMEM((N_HEADS // 2, 2 * LANES, 2 * QB), BF16),
                        pltpu.VMEM((N_HEADS // 2, V_ROWS, 2 * QB), F32)],
        compiler_params=_cparams("parallel", "arbitrary"),
        name="nsa_attn",
    )(qt, gtt, proj, ka, vst, *([kwd] * nwt), *([vwt] * nwt), kcd, vct, ovt)


def _hgrn_kernel(f_ref, i_ref, q_ref, z_ref, lb_ref, hn_ref, out_ref, st_ref):
    T, D = H_TILE, H_DIM
    W = H_W

    @pl.when(pl.program_id(1) == 0)
    def _():
        st_ref[...] = jnp.zeros_like(st_ref)

    lb = lb_ref[...]
    fp = f_ref[...]
    e = jnp.exp(-jnp.abs(fp))
    r = 1.0 / (1.0 + e)
    sig = jnp.where(fp >= 0, r, e * r)
    nsig = jnp.where(fp >= 0, e * r, r)
    logf2 = jnp.log(lb + (1.0 - lb) * sig) * LOG2E
    kk = (1.0 - lb) * nsig

    row = lax.broadcasted_iota(jnp.int32, (T, T), 0)
    col = lax.broadcasted_iota(jnp.int32, (T, T), 1)
    tri01 = jnp.where(col <= row, 1.0, 0.0).astype(BF16)
    bcum = _dot01(tri01, logf2)
    rowi = lax.broadcasted_iota(jnp.int32, (T, 1), 0)

    segs = []
    sg = 2 * H_LAG
    while sg <= T:
        upper = jnp.bitwise_and(rowi, sg - 1) >= sg // 2
        if sg == T:
            bmid = jnp.broadcast_to(bcum[sg // 2 - 1:sg // 2, :], (T, W))
            same = None
        else:
            b3 = bcum.reshape(T // sg, sg, W)
            bmid = jnp.broadcast_to(b3[:, sg // 2 - 1:sg // 2, :], (T // sg, sg, W)).reshape(T, W)
            shift = sg.bit_length() - 1
            same = jnp.right_shift(row, shift) == jnp.right_shift(col, shift)
        segs.append((upper, bmid, same))
        sg *= 2

    sub3 = lax.broadcasted_iota(jnp.int32, (T // 8, 8, 1), 1)
    lag_ok = [jnp.bitwise_and(sub3, H_LAG - 1) >= d for d in range(H_LAG)]
    b_end = bcum[T - 1:T, :]
    v_all = i_ref[...].astype(F32)
    for h in range(H_HEADS):
        hs = slice(h * D, (h + 1) * D)
        q = _silu(q_ref[:, hs].astype(F32))
        k = kk[:, hs]
        b = bcum[:, hs]
        v = v_all[:, hs]
        vb = v.astype(BF16)
        a_seg = []
        for upper, bmid, same in segs:
            bm = bmid[:, hs]
            qm = (q * jnp.exp2(jnp.where(upper, b - bm, NEG))).astype(BF16)
            km = (k * jnp.exp2(jnp.where(upper, NEG, bm - b))).astype(BF16)
            a_seg.append(_dot_nt(qm, km))
        st = st_ref[h]
        o_state = _dot_nt((q * jnp.exp2(b)).astype(BF16), st.astype(BF16))
        kt = (k * jnp.exp2(b_end[:, hs] - b)).astype(BF16)
        st_ref[h] = st * jnp.exp2(b_end[:, hs]) + _dot_tn(vb, kt)
        q3, k3, b3, v3 = (a.reshape(T // 8, 8, D) for a in (q, k, b, v))
        o3 = jnp.sum(q3 * k3, axis=2, keepdims=True) * v3
        for d in range(1, H_LAG):
            ks, bs, vs = (pltpu.roll(a, d, 1) for a in (k3, b3, v3))
            a = jnp.sum(q3 * ks * jnp.exp2(b3 - bs), axis=2, keepdims=True)
            o3 = o3 + jnp.where(lag_ok[d], a, 0.0) * vs
        a_tot = None
        for (upper, bmid, same), a in zip(segs, a_seg):
            a = a if same is None else jnp.where(same, a, 0.0)
            a_tot = a if a_tot is None else a_tot + a
        oh = o3.reshape(T, D) + o_state + _dot(a_tot.astype(BF16), vb)
        on = oh * lax.rsqrt(jnp.mean(oh * oh, axis=1, keepdims=True) + EPS) * hn_ref[:, hs]
        out_ref[:, hs] = (on * _silu(z_ref[:, hs].astype(F32))).astype(out_ref.dtype)


def _hgrn(proj, fgate, lb_row, hn_row, batch, seq):
    T = H_TILE
    nt = seq // T
    cb = lambda j: pl.BlockSpec((T, H_W), lambda b, c, j=j: (b * nt + c, j))
    vec = pl.BlockSpec((1, H_W), lambda b, c: (0, 0))
    return pl.pallas_call(
        _hgrn_kernel,
        grid=(batch, nt),
        in_specs=[pl.BlockSpec((T, H_W), lambda b, c: (b * nt + c, 0)), cb(1), cb(2), cb(3), vec, vec],
        out_specs=pl.BlockSpec((T, H_W), lambda b, c: (b * nt + c, 0)),
        out_shape=jax.ShapeDtypeStruct((batch * seq, H_W), BF16),
        scratch_shapes=[pltpu.VMEM((H_HEADS, H_DIM, H_DIM), F32)],
        compiler_params=_cparams("parallel", "arbitrary"),
        name="hgrn2",
    )(fgate, proj, proj, proj, lb_row, hn_row)


def _even_relayout(a):
    nb = 5 * M_W + 2 * M_HEADS
    g0 = nb + N_W + 6 * N_KV_W
    col = lambda lo, n: lax.slice_in_dim(a, lo, lo + n, axis=-1)
    pad = jnp.zeros(a.shape[:-1] + (C_NZ - C_SMALL - SM_G - 3 * N_HEADS,), a.dtype)
    out = jnp.concatenate([
        col(0, 5 * M_W),
        col(nb, N_W + 6 * N_KV_W),
        col(5 * M_W, 2 * M_HEADS),
        col(g0, 3 * N_HEADS), pad,
        col(g0 + 3 * N_HEADS, N_W),
    ], axis=-1)
    assert out.shape[-1] == EV_COLS
    return out


def _rope_tables(positions):
    inv_freq = ROPE_THETA ** (-np.arange(ROPE_HALF, dtype=np.float64) / ROPE_HALF)
    ang = positions.astype(np.float64)[:, None] * inv_freq[None, :]
    cos, sin = np.cos(ang).astype(np.float32), np.sin(ang).astype(np.float32)
    n = positions.shape[0]
    z8 = np.zeros((n, ROPE_HALF), np.float32)
    rest1 = np.ones((n, N_DIM - ROPE_DIM), np.float32)
    rest0 = np.zeros((n, N_DIM - ROPE_DIM), np.float32)
    cs = np.concatenate([cos, cos, rest1], axis=1)
    s1 = np.concatenate([z8, sin, rest0], axis=1)
    s2 = np.concatenate([-sin, z8, rest0], axis=1)
    two = lambda a: np.concatenate([a, a], axis=1)
    return two(cs), two(s1), two(s2)


def _overlap_matrix(seq):
    ncmp = seq // CMP_STRIDE
    n_sel = seq // SEL_BLOCK
    cs = np.arange(ncmp) * CMP_STRIDE
    ss = np.arange(n_sel) * SEL_BLOCK
    ov = np.clip(np.minimum(cs[:, None] + CMP_LEN, ss[None, :] + SEL_BLOCK)
                 - np.maximum(cs[:, None], ss[None, :]), 0, None).astype(np.float32)
    ov[ncmp - 1, :] = 0.0
    return np.pad(ov, ((0, 0), (0, LANES - n_sel)))


def _even_layer(x2d, norm_w, w_in, b_in, f_bias, conv_w, conv_b, head_norm, q_norm, k_norm,
                ck_pos, ck_w1, ck_w2, cv_pos, cv_w1, cv_w2, w_out, batch, seq):
    assert seq // SEL_BLOCK <= LANES and seq % KV_TILE == 0
    proj, small = _norm_proj(x2d, norm_w, _even_relayout(w_in).astype(BF16), _even_relayout(b_in),
                             (C_SMALL, C_SMALL + LANES), tn=EV_COLS // 3)

    f_bias_row = jnp.zeros((1, LANES), F32).at[0, SM_F:SM_F + M_HEADS].set(f_bias)
    ya = _mlstm(proj, small, conv_w, conv_b, f_bias_row, head_norm, batch, seq)

    cs, s1, s2 = (jnp.asarray(a) for a in _rope_tables(np.arange(seq)))
    qn_row = jnp.concatenate([q_norm, q_norm]).reshape(1, LANES)
    kn_rows = jnp.concatenate([k_norm, k_norm], axis=1)
    qt, ka, kwd, vst, vwt, kc, vc, gtt = _nsa_prep(proj, small, cs, s1, s2, qn_row, kn_rows, batch, seq)

    ncmp = seq // CMP_STRIDE
    ccs, cs1, cs2 = (jnp.asarray(a) for a in _rope_tables(np.arange(ncmp) * CMP_STRIDE + CMP_LEN - 1))
    kcd, vct = _nsa_compress(
        kc.reshape(batch * ncmp, CMP_STRIDE * LANES), vc.reshape(batch * ncmp, CMP_STRIDE * LANES),
        _compress_params(ck_pos, ck_w1, ck_w2), _compress_params(cv_pos, cv_w1, cv_w2),
        ccs, cs1, cs2, kn_rows, batch, ncmp)

    ovt = jnp.asarray(_overlap_matrix(seq).T).astype(BF16)
    yb = _nsa_attn(proj, qt, gtt, ka, vst, kwd, vwt, kcd, vct, ovt, batch, seq)

    w_o = w_out.astype(BF16)
    return _out_proj2(ya, yb, w_o[:M_W], w_o[M_W:], x2d)


def _odd_layer(x2d, norm_w, w_in, b_in, lb, head_norm, w_out, batch, seq):
    proj, fgate = _norm_proj(x2d, norm_w, w_in.astype(BF16), b_in, (0, H_W), tn=H_W)
    y = _hgrn(proj, fgate, lb.reshape(1, H_W), head_norm.reshape(1, H_W), batch, seq)
    return _out_proj1(y, w_out.astype(BF16), x2d)


def kernel(x, norm_w, ev_w_in, ev_b_in, mlstm_f_bias, mlstm_conv_w, mlstm_conv_b, mlstm_head_norm, nsa_q_norm, nsa_k_norm, cmp_k_pos, cmp_k_w1, cmp_k_w2, cmp_v_pos, cmp_v_w1, cmp_v_w2, ev_w_out, od_w_in, od_b_in, hgrn_lb_logits, hgrn_head_norm, od_w_out):
    batch, seq, d = x.shape
    depth = norm_w.shape[0]
    lbs = jnp.cumsum(jax.nn.softmax(hgrn_lb_logits.astype(F32), axis=0), axis=0)
    x2d = x.reshape(batch * seq, d)
    for layer in range(depth):
        j = layer // 2
        if layer % 2 == 0:
            x2d = _even_layer(x2d, norm_w[layer], ev_w_in[j], ev_b_in[j], mlstm_f_bias[j], mlstm_conv_w[j],
                              mlstm_conv_b[j], mlstm_head_norm[j], nsa_q_norm[j], nsa_k_norm[j], cmp_k_pos[j],
                              cmp_k_w1[j], cmp_k_w2[j], cmp_v_pos[j], cmp_v_w1[j], cmp_v_w2[j], ev_w_out[j],
                              batch, seq)
        else:
            x2d = _odd_layer(x2d, norm_w[layer], od_w_in[j], od_b_in[j], lbs[j], hgrn_head_norm[j], od_w_out[j],
                             batch, seq)
    return x2d.reshape(batch, seq, d)
```

```python
import functools

import numpy as np
import jax
import jax.numpy as jnp
from jax import lax
from jax.experimental import pallas as pl
from jax.experimental.pallas import tpu as pltpu

F32 = jnp.float32
BF16 = jnp.bfloat16

LANES = 128
EPS = 1e-6
NEG = -1e30
VMEM_LIMIT_BYTES = 56 * 1024 * 1024

M_HEADS = 4
M_DIM = 128
M_W = M_HEADS * M_DIM
M_CHUNK = 128
CONV_K = 4
N_HEADS = 8
N_GROUPS = 2
N_HPG = N_HEADS // N_GROUPS
N_DIM = 64
N_W = N_HEADS * N_DIM
N_KV_W = N_GROUPS * N_DIM
CMP_LEN = 32
CMP_STRIDE = 16
CMP_HIDDEN = 128
CMP_CHUNK = 128
SEL_BLOCK = 64
SEL_SHIFT = 6
SEL_TOPK = 16
WINDOW = 512
Q_BLOCK = 128
ROPE_THETA = 500000.0
ROPE_DIM = N_DIM // 4
ROPE_HALF = ROPE_DIM // 2
KV_TILE = 512
V_ROWS = 80
LOG2E = 1.4426950408889634
H_HEADS = 8
H_DIM = 128
H_W = H_HEADS * H_DIM
H_TILE = 128
H_LAG = 2

C_MQ, C_MK, C_MV, C_MO, C_MZ = 0, 512, 1024, 1536, 2048
C_NQ = 2560
C_KC, C_VC, C_KS, C_VS = 3072, 3200, 3328, 3456
C_KW, C_VW, C_SMALL = 3584, 3712, 3840
C_NZ = 4096
EV_COLS = 4608
SM_I, SM_F, SM_G = 0, 4, 8


def _cparams(*sem):
    return pltpu.CompilerParams(dimension_semantics=sem, vmem_limit_bytes=VMEM_LIMIT_BYTES)


def _dot(a, b):
    return jnp.dot(a, b, preferred_element_type=F32)


def _dot_nt(a, b):
    return lax.dot_general(a, b, (((1,), (1,)), ((), ())), preferred_element_type=F32)


def _dot_tn(a, b):
    return lax.dot_general(a, b, (((0,), (0,)), ((), ())), preferred_element_type=F32)


def _sigmoid(x):
    return 1.0 / (1.0 + jnp.exp(-x))


def _silu(x):
    return x * _sigmoid(x)


def _split3(x):
    hi = x.astype(BF16)
    r = x - hi.astype(F32)
    mid = r.astype(BF16)
    lo = (r - mid.astype(F32)).astype(BF16)
    return hi, mid, lo


def _dot01(m01, x):
    hi, mid, lo = _split3(x)
    return _dot(m01, hi) + _dot(m01, mid) + _dot(m01, lo)


def _norm_proj_kernel(x_ref, nw_ref, w_ref, b_ref, o_ref, g_ref, h_ref, *, g_tiles, g_off):
    j = pl.program_id(1)

    @pl.when(j == 0)
    def _():
        x = x_ref[...]
        ms = jnp.mean(x * x, axis=-1, keepdims=True)
        h_ref[...] = (x * lax.rsqrt(ms + EPS) * nw_ref[...]).astype(BF16)

    acc = _dot(h_ref[...], w_ref[...]) + b_ref[...]
    o_ref[...] = acc.astype(o_ref.dtype)

    @pl.when((j >= g_tiles[0]) & (j < g_tiles[1]))
    def _():
        g_ref[...] = acc[:, g_off:g_off + g_ref.shape[1]]


def _norm_proj(x2d, norm_w, w_bf16, bias, g_cols, tm=1024, tn=512):
    t, d = x2d.shape
    n = w_bf16.shape[1]
    g_w = min(g_cols[1] - g_cols[0], tn)
    g_tiles = (g_cols[0] // tn, -(-g_cols[1] // tn))
    return pl.pallas_call(
        functools.partial(_norm_proj_kernel, g_tiles=g_tiles, g_off=g_cols[0] % tn),
        grid=(t // tm, n // tn),
        in_specs=[
            pl.BlockSpec((tm, d), lambda i, j: (i, 0)),
            pl.BlockSpec((1, d), lambda i, j: (0, 0)),
            pl.BlockSpec((d, tn), lambda i, j: (0, j)),
            pl.BlockSpec((1, tn), lambda i, j: (0, j)),
        ],
        out_specs=[pl.BlockSpec((tm, tn), lambda i, j: (i, j)),
                   pl.BlockSpec((tm, g_w), lambda i, j: (i, jnp.clip(j - g_tiles[0], 0, g_tiles[1] - g_tiles[0] - 1)))],
        out_shape=[jax.ShapeDtypeStruct((t, n), BF16), jax.ShapeDtypeStruct((t, g_cols[1] - g_cols[0]), F32)],
        scratch_shapes=[pltpu.VMEM((tm, d), BF16)],
        compiler_params=_cparams("parallel", "arbitrary"),
        name="norm_proj",
    )(x2d, norm_w.reshape(1, d), w_bf16, bias.reshape(1, n))


def _out_proj2_kernel(ya_ref, yb_ref, wa_ref, wb_ref, x_ref, o_ref):
    o_ref[...] = x_ref[...] + _dot(ya_ref[...], wa_ref[...]) + _dot(yb_ref[...], wb_ref[...])


def _out_proj2(ya, yb, wa, wb, x2d, tm=1024):
    t, d = x2d.shape
    ka, kb = ya.shape[1], yb.shape[1]
    return pl.pallas_call(
        _out_proj2_kernel,
        grid=(t // tm,),
        in_specs=[
            pl.BlockSpec((tm, ka), lambda i: (i, 0)),
            pl.BlockSpec((tm, kb), lambda i: (i, 0)),
            pl.BlockSpec((ka, d), lambda i: (0, 0)),
            pl.BlockSpec((kb, d), lambda i: (0, 0)),
            pl.BlockSpec((tm, d), lambda i: (i, 0)),
        ],
        out_specs=pl.BlockSpec((tm, d), lambda i: (i, 0)),
        out_shape=jax.ShapeDtypeStruct((t, d), F32),
        compiler_params=_cparams("parallel"),
        name="out_proj2",
    )(ya, yb, wa, wb, x2d)


def _out_proj1_kernel(y_ref, w_ref, x_ref, o_ref):
    o_ref[...] = x_ref[...] + _dot(y_ref[...], w_ref[...])


def _out_proj1(y, w, x2d, tm=1024):
    t, d = x2d.shape
    k = y.shape[1]
    return pl.pallas_call(
        _out_proj1_kernel,
        grid=(t // tm,),
        in_specs=[
            pl.BlockSpec((tm, k), lambda i: (i, 0)),
            pl.BlockSpec((k, d), lambda i: (0, 0)),
            pl.BlockSpec((tm, d), lambda i: (i, 0)),
        ],
        out_specs=pl.BlockSpec((tm, d), lambda i: (i, 0)),
        out_shape=jax.ShapeDtypeStruct((t, d), F32),
        compiler_params=_cparams("parallel"),
        name="out_proj1",
    )(y, w, x2d)


def _mlstm_kernel(q_ref, k_ref, v_ref, o_ref, z_ref, sm_ref, cw_ref, cb_ref, fb_ref, hn_ref,
                  out_ref, xbuf, c_ref, n_ref, m_ref):
    L, D = M_CHUNK, M_DIM

    XR = xbuf.shape[0]

    @pl.when(pl.program_id(1) == 0)
    def _():
        xbuf[...] = jnp.zeros_like(xbuf)
        c_ref[...] = jnp.zeros_like(c_ref)
        n_ref[...] = jnp.zeros_like(n_ref)
        m_ref[...] = jnp.zeros_like(m_ref)

    xbuf[16:16 + L, 0:M_W] = q_ref[...]
    xbuf[16:16 + L, M_W:2 * M_W] = k_ref[...]
    x2 = xbuf[...]
    t_i = lax.broadcasted_iota(jnp.int32, (L, XR), 0)
    r_i = lax.broadcasted_iota(jnp.int32, (L, XR), 1)
    acc = cb_ref[...] + cw_ref[CONV_K - 1:CONV_K, :] * x2[16:16 + L].astype(F32)
    for d in range(1, CONV_K):
        shift = jnp.where(r_i == t_i + 16 - d, 1.0, 0.0).astype(BF16)
        acc = acc + cw_ref[CONV_K - 1 - d:CONV_K - d, :] * _dot(shift, x2)
    qk = _silu(acc)
    xbuf[0:16, :] = xbuf[L:L + 16, :]

    row = lax.broadcasted_iota(jnp.int32, (L, L), 0)
    col = lax.broadcasted_iota(jnp.int32, (L, L), 1)
    causal = col <= row
    fpre = sm_ref[...] + fb_ref[...]
    lane = lax.broadcasted_iota(jnp.int32, (L, LANES), 1)
    is_f = (lane >= SM_F) & (lane < SM_F + M_HEADS)
    gates = jnp.where(is_f, jnp.minimum(fpre, 0.0) - jnp.log(1.0 + jnp.exp(-jnp.abs(fpre))), fpre)
    bcols = _dot01(jnp.where(causal, 1.0, 0.0).astype(BF16), gates)
    gates_t = gates.T
    bcols_t = bcols.T

    heads = []
    for h in range(M_HEADS):
        hs = slice(h * D, (h + 1) * D)
        q = qk[:, hs]
        k = qk[:, M_W + h * D:M_W + (h + 1) * D] * (D ** -0.5)
        i_col = gates[:, SM_I + h:SM_I + h + 1]
        i_row = gates_t[SM_I + h:SM_I + h + 1, :]
        bcum_col = bcols[:, SM_F + h:SM_F + h + 1]
        bcum_row = bcols_t[SM_F + h:SM_F + h + 1, :]
        m_prev = m_ref[h]
        dmat = jnp.where(causal, bcum_col - bcum_row + i_row, -jnp.inf)
        inter = bcum_col + m_prev
        m_row = jnp.maximum(jnp.max(dmat, axis=1, keepdims=True), inter)
        b_last = bcum_col[L - 1:L, :]
        dl = b_last - bcum_col + i_col
        m_new = jnp.maximum(b_last + m_prev, jnp.max(dl, axis=0, keepdims=True))
        wkk = jnp.exp(dl - m_new) * k
        heads.append(dict(
            hs=hs, q=q, qb=q.astype(BF16), kb=k.astype(BF16), vb=v_ref[:, hs].astype(BF16), wkk=wkk,
            e=jnp.exp(dmat - m_row), w_inter=jnp.exp(inter - m_row), floor=jnp.exp(-m_row),
            decay=jnp.exp(b_last + m_prev - m_new), m_new=m_new))
    s_qk = [_dot_nt(t["qb"], t["kb"]) for t in heads]
    s_qc = [_dot(t["qb"], c_ref[h].astype(BF16)) for h, t in enumerate(heads)]
    s_kv = [_dot_tn(t["wkk"].astype(BF16), t["vb"]) for t in heads]
    smats = [s_qk[h] * t["e"] for h, t in enumerate(heads)]
    s_sv = [_dot(smats[h].astype(BF16), t["vb"]) for h, t in enumerate(heads)]
    for h, t in enumerate(heads):
        hs = t["hs"]
        num = s_sv[h] + t["w_inter"] * s_qc[h]
        den = (jnp.sum(smats[h], axis=1, keepdims=True)
               + t["w_inter"] * jnp.sum(t["q"] * n_ref[h], axis=1, keepdims=True))
        hh = num / jnp.maximum(jnp.abs(den), t["floor"])
        c_ref[h] = t["decay"] * c_ref[h] + s_kv[h]
        n_ref[h] = t["decay"] * n_ref[h] + jnp.sum(t["wkk"], axis=0, keepdims=True)
        m_ref[h] = t["m_new"]
        hn = hh * lax.rsqrt(jnp.mean(hh * hh, axis=1, keepdims=True) + EPS) * hn_ref[:, hs]
        gate = _sigmoid(o_ref[:, hs].astype(F32)) * _silu(z_ref[:, hs].astype(F32))
        out_ref[:, hs] = (hn * gate).astype(out_ref.dtype)


def _mlstm(proj, small, conv_w, conv_b, f_bias_row, head_norm, batch, seq):
    L = M_CHUNK
    nc = seq // L
    cb = lambda off: pl.BlockSpec((L, M_W), lambda b, c, off=off: (b * nc + c, off // M_W))
    return pl.pallas_call(
        _mlstm_kernel,
        grid=(batch, nc),
        in_specs=[
            cb(C_MQ), cb(C_MK), cb(C_MV), cb(C_MO), cb(C_MZ),
            pl.BlockSpec((L, LANES), lambda b, c: (b * nc + c, 0)),
            pl.BlockSpec((CONV_K, 2 * M_W), lambda b, c: (0, 0)),
            pl.BlockSpec((1, 2 * M_W), lambda b, c: (0, 0)),
            pl.BlockSpec((1, LANES), lambda b, c: (0, 0)),
            pl.BlockSpec((1, M_W), lambda b, c: (0, 0)),
        ],
        out_specs=pl.BlockSpec((L, M_W), lambda b, c: (b * nc + c, 0)),
        out_shape=jax.ShapeDtypeStruct((batch * seq, M_W), BF16),
        scratch_shapes=[
            pltpu.VMEM((2 * L, 2 * M_W), BF16),
            pltpu.VMEM((M_HEADS, M_DIM, M_DIM), F32),
            pltpu.VMEM((M_HEADS, 1, M_DIM), F32),
            pltpu.VMEM((M_HEADS, 1, 1), F32),
        ],
        compiler_params=_cparams("parallel", "arbitrary"),
        name="mlstm",
    )(proj, proj, proj, proj, proj, small, conv_w, conv_b.reshape(1, -1), f_bias_row, head_norm.reshape(1, -1))


def _group_rms(x, w_row):
    lane = lax.broadcasted_iota(jnp.int32, x.shape, 1)
    left = lane < N_DIM
    sq = x * x
    tot = jnp.sum(sq, axis=1, keepdims=True)
    lft = jnp.sum(jnp.where(left, sq, 0.0), axis=1, keepdims=True)
    ms = jnp.where(left, lft, tot - lft) * (1.0 / N_DIM)
    return x * lax.rsqrt(ms + EPS) * w_row


def _rope(x, cs, s1, s2):
    return x * cs + pltpu.roll(x, ROPE_HALF, 1) * s1 + pltpu.roll(x, LANES - ROPE_HALF, 1) * s2


def _dup_halves(x):
    lane = lax.broadcasted_iota(jnp.int32, x.shape, 1)
    sw = pltpu.roll(x, N_DIM, 1)
    left = lane < N_DIM
    return jnp.where(left, x, sw), jnp.where(left, sw, x)


def _nsa_prep_kernel(q_ref, kv_ref, kw_ref, sm_ref, cs_ref, s1_ref, s2_ref, qn_ref, kn_ref,
                     qt_ref, ka_ref, kwd_ref, vst_ref, vwt_ref, kc_ref, vc_ref, gtt_ref,
                     *, tp, seq):
    cs, s1, s2 = cs_ref[...], s1_ref[...], s2_ref[...]
    for j in range(N_W // LANES):
        sl = slice(j * LANES, (j + 1) * LANES)
        qn = _group_rms(q_ref[:, sl].astype(F32), qn_ref[...])
        qt_ref[sl, :] = (_rope(qn, cs, s1, s2) * (N_DIM ** -0.5 * LOG2E)).T.astype(BF16)
    kv = kv_ref[...].astype(F32)
    kc_ref[...] = kv[:, 0:LANES]
    vc_ref[...] = kv[:, LANES:2 * LANES]
    ks = _rope(_group_rms(kv[:, 2 * LANES:3 * LANES], kn_ref[1:2, :]), cs, s1, s2)
    k0, k1 = _dup_halves(ks)
    pos0 = (pl.program_id(0) * tp) % seq
    pos = pos0 + lax.broadcasted_iota(jnp.int32, (tp, LANES), 0)
    blk = lax.broadcasted_iota(jnp.int32, (tp, LANES), 1)
    onehot = jnp.where(jnp.right_shift(pos, SEL_SHIFT) == blk, 1.0, 0.0).astype(BF16)
    ka_ref[:, 0:LANES] = k0.astype(BF16)
    ka_ref[:, LANES:2 * LANES] = onehot
    ka_ref[:, 2 * LANES:3 * LANES] = k1.astype(BF16)
    ka_ref[:, 3 * LANES:4 * LANES] = onehot
    vst = kv[:, 3 * LANES:4 * LANES].T
    fill = (lax.broadcasted_iota(jnp.int32, (V_ROWS - N_DIM, Q_BLOCK), 0) == 0).astype(BF16)
    for c in range(tp // Q_BLOCK):
        for g in range(N_GROUPS):
            vst_ref[c, g * V_ROWS:g * V_ROWS + N_DIM, :] = (
                vst[g * N_DIM:(g + 1) * N_DIM, c * Q_BLOCK:(c + 1) * Q_BLOCK].astype(BF16))
            vst_ref[c, g * V_ROWS + N_DIM:(g + 1) * V_ROWS, :] = fill
    kwb = kw_ref[...].astype(F32)
    kw = _rope(_group_rms(kwb[:, 0:LANES], kn_ref[2:3, :]), cs, s1, s2)
    w0, w1 = _dup_halves(kw)
    kwd_ref[:, 0:LANES] = w0.astype(BF16)
    kwd_ref[:, LANES:2 * LANES] = w1.astype(BF16)
    vwt_ref[...] = kwb[:, LANES:2 * LANES].T.astype(BF16)
    gtt_ref[...] = _sigmoid(sm_ref[...]).T


def _nsa_prep(proj, small, cs, s1, s2, qn_row, kn_rows, batch, seq):
    tp = KV_TILE
    t = batch * seq
    nsb = seq // tp
    rb = lambda w, off: pl.BlockSpec((tp, w), lambda i, off=off, w=w: (i, off // w))
    tab = pl.BlockSpec((tp, LANES), lambda i: (i % nsb, 0))
    o = lambda w: pl.BlockSpec((tp, w), lambda i: (i, 0))
    ot = lambda r: pl.BlockSpec((r, tp), lambda i: (0, i))
    sds = lambda w, dt: jax.ShapeDtypeStruct((t, w), dt)
    return pl.pallas_call(
        functools.partial(_nsa_prep_kernel, tp=tp, seq=seq),
        grid=(t // tp,),
        in_specs=[rb(512, C_NQ), rb(512, C_KC), rb(512, C_KW), o(LANES), tab, tab, tab,
                  pl.BlockSpec((1, LANES), lambda i: (0, 0)), pl.BlockSpec((3, LANES), lambda i: (0, 0))],
        out_specs=[ot(N_W), o(512), o(256),
                   pl.BlockSpec((tp // Q_BLOCK, N_GROUPS * V_ROWS, Q_BLOCK), lambda i: (i, 0, 0)),
                   ot(LANES), o(128), o(128), ot(LANES)],
        out_shape=[jax.ShapeDtypeStruct((N_W, t), BF16), sds(512, BF16), sds(256, BF16),
                   jax.ShapeDtypeStruct((t // Q_BLOCK, N_GROUPS * V_ROWS, Q_BLOCK), BF16),
                   jax.ShapeDtypeStruct((LANES, t), BF16),
                   sds(128, F32), sds(128, F32), jax.ShapeDtypeStruct((LANES, t), F32)],
        compiler_params=_cparams("parallel"),
        name="nsa_prep",
    )(proj, proj, proj, small, cs, s1, s2, qn_row, kn_rows)


def _gelu_tanh(x):
    return 0.5 * x * (1.0 + jnp.tanh(0.7978845608028654 * (x + 0.044715 * x * x * x)))


def _compress_mlp(x_ref, pa_ref, pb_ref, wa_ref, wb_ref, w2_ref, hb_ref):
    n = x_ref.shape[0]
    x = x_ref[...]
    ha = _dot((x + pa_ref[...]).astype(BF16), wa_ref[...])
    hb_ref[0:n, :] = _dot((x + pb_ref[...]).astype(BF16), wb_ref[...])
    hb_ref[n:n + 8, :] = jnp.zeros((8, hb_ref.shape[1]), F32)
    hid = ha + hb_ref[1:n + 1, :]
    return _dot(_gelu_tanh(hid).astype(BF16), w2_ref[...])


def _nsa_compress_kernel(kx_ref, vx_ref, kpa_ref, kpb_ref, vpa_ref, vpb_ref, kwa_ref, kwb_ref, kw2_ref,
                         vwa_ref, vwb_ref, vw2_ref, cs_ref, s1_ref, s2_ref, kn_ref,
                         kcd_ref, vct_ref, hb_ref):
    n = kx_ref.shape[0]
    rowi = lax.broadcasted_iota(jnp.int32, (n, LANES), 0)
    live = rowi < n - 1
    kc = _compress_mlp(kx_ref, kpa_ref, kpb_ref, kwa_ref, kwb_ref, kw2_ref, hb_ref)
    kc = _rope(_group_rms(kc, kn_ref[0:1, :]), cs_ref[...], s1_ref[...], s2_ref[...])
    kc = jnp.where(live, kc, 0.0)
    c0, c1 = _dup_halves(kc)
    kcd_ref[:, 0:LANES] = c0.astype(BF16)
    kcd_ref[:, LANES:2 * LANES] = c1.astype(BF16)
    vc = _compress_mlp(vx_ref, vpa_ref, vpb_ref, vwa_ref, vwb_ref, vw2_ref, hb_ref)
    vc = jnp.where(live, vc, 0.0)
    vct_ref[0] = vc.T.astype(BF16)


def _nsa_compress(kx, vx, kp, vp, cs, s1, s2, kn_rows, batch, nrow):
    wide = CMP_STRIDE * LANES
    full = lambda a: pl.BlockSpec(a.shape, lambda b: (0,) * a.ndim)
    xs = pl.BlockSpec((nrow, wide), lambda b: (b, 0))
    args = (kx, vx, kp[0], kp[1], vp[0], vp[1], kp[2], kp[3], kp[4], vp[2], vp[3], vp[4], cs, s1, s2, kn_rows)
    return pl.pallas_call(
        _nsa_compress_kernel,
        grid=(batch,),
        in_specs=[xs, xs] + [full(a) for a in args[2:]],
        out_specs=[pl.BlockSpec((nrow, 2 * LANES), lambda b: (b, 0)),
                   pl.BlockSpec((1, LANES, nrow), lambda b: (b, 0, 0))],
        out_shape=[jax.ShapeDtypeStruct((batch * nrow, 2 * LANES), BF16),
                   jax.ShapeDtypeStruct((batch, LANES, nrow), BF16)],
        scratch_shapes=[pltpu.VMEM((nrow + 8, 2 * CMP_HIDDEN), F32)],
        compiler_params=_cparams("parallel"),
        name="nsa_compress",
    )(*args)


def _compress_params(pos, w1, w2):
    def expand(w1_half, pos_half):
        w = w1_half.reshape(CMP_STRIDE, N_DIM, CMP_HIDDEN)
        z = jnp.zeros_like(w)
        g0 = jnp.concatenate([w, z], axis=-1)
        g1 = jnp.concatenate([z, w], axis=-1)
        wx = jnp.stack([g0, g1], axis=1).reshape(CMP_STRIDE * LANES, 2 * CMP_HIDDEN)
        px = jnp.concatenate([pos_half, pos_half], axis=-1).reshape(1, CMP_STRIDE * LANES)
        return wx.astype(BF16), px.astype(F32)

    half = CMP_STRIDE * N_DIM
    wa, pa = expand(w1[:half], pos[:CMP_STRIDE])
    wb, pb = expand(w1[half:], pos[CMP_STRIDE:])
    z2 = jnp.zeros_like(w2)
    w2x = jnp.concatenate([jnp.concatenate([w2, z2], axis=1), jnp.concatenate([z2, w2], axis=1)], axis=0)
    return pa, pb, wa, wb, w2x.astype(BF16)


def _nsa_attn_kernel(qt_ref, gtt_ref, z_ref, ka_ref, vst_ref,
                     kw0, kw1, kw2, kw3, kw4, wt0, wt1, wt2, wt3, wt4,
                     kcd_ref, vct_ref, ovt_ref, out_ref, qa_ref, acc_ref, *, ncmp):
    QB = Q_BLOCK
    NP = N_HEADS // 2
    PW = 2 * QB
    i = pl.program_id(1)
    t0 = i * QB
    kw_refs = (kw0, kw1, kw2, kw3, kw4)
    vwt_refs = (wt0, wt1, wt2, wt3, wt4)
    nwt = len(kw_refs)

    sub = lax.broadcasted_iota(jnp.int32, (QB, QB), 0)
    lane = lax.broadcasted_iota(jnp.int32, (QB, QB), 1)
    top = sub < N_DIM
    t_row = t0 + lax.broadcasted_iota(jnp.int32, (1, QB), 1)
    t_row2 = jnp.concatenate([t_row, t_row], axis=1)

    def pair_q(pr):
        slab = qt_ref[pr * LANES:(pr + 1) * LANES, :]
        zero = jnp.zeros_like(slab)
        return jnp.concatenate([jnp.where(top, slab, zero), jnp.where(top, zero, slab)], axis=1)

    blk_f = sub.astype(F32)
    cur = jnp.right_shift(t_row, SEL_SHIFT)
    grp = lambda pr: pr // (N_HPG // 2)
    for pr in range(NP):
        qa_ref[pr, 0:LANES, :] = pair_q(pr)

    def compressed(n):
        nidx = lax.broadcasted_iota(jnp.int32, (n, PW), 0)
        ok = (nidx * CMP_STRIDE + (CMP_LEN - 1) <= t_row2) & (nidx < ncmp - 1)
        s_cmp = [_dot(kcd_ref[0:n, grp(pr) * LANES:(grp(pr) + 1) * LANES], qa_ref[pr, 0:LANES, :])
                 for pr in range(NP)]
        p_cmp = []
        for pr in range(NP):
            s = jnp.where(ok, s_cmp[pr], NEG)
            mx = jnp.max(s, axis=0, keepdims=True)
            p = jnp.where(ok, jnp.exp2(s - mx), 0.0)
            den = jnp.sum(p, axis=0, keepdims=True)
            p_cmp.append(p * (1.0 / jnp.maximum(den, 1e-30)))
        outs = [_dot(vct_ref[0, grp(pr) * N_DIM:(grp(pr) + 1) * N_DIM, 0:n], p_cmp[pr].astype(BF16))
                for pr in range(NP)]
        for g in range(N_GROUPS):
            psum = jnp.zeros((n, QB), F32)
            for pp in range(N_HPG // 2):
                p = p_cmp[g * (N_HPG // 2) + pp]
                psum = psum + p[:, 0:QB] + p[:, QB:PW]
            p_hi = psum.astype(BF16)
            p_lo = (psum - p_hi.astype(F32)).astype(BF16)
            outs.append(_dot(ovt_ref[:, 0:n], p_hi) + _dot(ovt_ref[:, 0:n], p_lo))
        return tuple(outs)

    n_opts = list(range(CMP_CHUNK, ncmp + 1, CMP_CHUNK))
    need = (t0 + QB - CMP_LEN) // CMP_STRIDE + 1
    outs = lax.switch(jnp.clip((need - 1) // CMP_CHUNK, 0, len(n_opts) - 1),
                      [functools.partial(compressed, n) for n in n_opts])
    o_cmp, imps = outs[:NP], outs[NP:]
    forced = (sub == 0) | (sub == cur) | (sub == cur - 1)
    works = [jnp.where(sub > cur, NEG, jnp.where(forced, -NEG, imp)) for imp in imps]
    sels = [jnp.zeros((QB, QB), jnp.bool_) for _ in range(N_GROUPS)]
    for _ in range(SEL_TOPK):
        for g in range(N_GROUPS):
            mx = jnp.max(works[g], axis=0, keepdims=True)
            idx = jnp.min(jnp.where(works[g] == mx, blk_f, float(LANES)), axis=0, keepdims=True)
            pick = blk_f == idx
            sels[g] = sels[g] | pick
            works[g] = jnp.where(pick, -jnp.inf, works[g])
    for g in range(N_GROUPS):
        neg = jnp.where(sels[g] & (sub <= cur), 0.0, NEG).astype(BF16)
        neg2 = jnp.concatenate([neg, neg], axis=1)
        for pp in range(N_HPG // 2):
            qa_ref[g * (N_HPG // 2) + pp, LANES:2 * LANES, :] = neg2

    CH = QB
    above = jnp.concatenate([sub > lane, sub > lane], axis=1)

    def sel_scores(cj):
        k0 = pl.multiple_of(cj * CH, CH)
        return [_dot(ka_ref[pl.ds(k0, CH), 2 * grp(pr) * LANES:2 * (grp(pr) + 1) * LANES], qa_ref[pr])
                for pr in range(NP)]

    def sel_softmax(ss, ms, masked):
        new, alphas, ps = [], [], []
        for pr in range(NP):
            s = jnp.where(above, NEG, ss[pr]) if masked else ss[pr]
            m_new = jnp.maximum(ms[pr], jnp.max(s, axis=0, keepdims=True))
            alphas.append(jnp.exp2(ms[pr] - m_new))
            ps.append(jnp.exp2(s - m_new).astype(BF16))
            new.append(m_new)
        return tuple(new), alphas, ps

    def sel_values(cj, alphas, ps):
        for pr in range(NP):
            g = grp(pr)
            acc_ref[pr] = alphas[pr] * acc_ref[pr] + _dot(vst_ref[cj, g * V_ROWS:(g + 1) * V_ROWS, :], ps[pr])

    def sel_chunk(cj, ms, masked):
        ms, alphas, ps = sel_softmax(sel_scores(cj), ms, masked)
        sel_values(cj, alphas, ps)
        return ms

    def sel_tile(jt, ms, nc):
        c0 = jt * nc
        k0 = pl.multiple_of(c0 * CH, CH)
        big = [_dot(ka_ref[pl.ds(k0, nc * CH), 2 * grp(pr) * LANES:2 * (grp(pr) + 1) * LANES], qa_ref[pr])
               for pr in range(NP)]
        for c in range(nc):
            ss = [b[c * CH:(c + 1) * CH] for b in big]
            ms, alphas, ps = sel_softmax(ss, ms, False)
            sel_values(c0 + c, alphas, ps)
        return ms

    acc_ref[...] = jnp.zeros_like(acc_ref)
    ms = tuple(jnp.full((1, PW), 2.0 * NEG, F32) for _ in range(NP))
    done = 0
    for nc in (4 * KV_TILE // CH, 2 * KV_TILE // CH, KV_TILE // CH):
        n_run = (i - done) // nc
        ms = lax.fori_loop(done // nc, done // nc + n_run, functools.partial(sel_tile, nc=nc), ms)
        done = done + n_run * nc
    ms = lax.fori_loop(done, i, lambda cj, ms: sel_chunk(cj, ms, False), ms)
    s_diag = sel_scores(i)
    s_win = [[_dot(kw_refs[c][:, grp(pr) * LANES:(grp(pr) + 1) * LANES], qa_ref[pr, 0:LANES, :])
              for c in range(nwt)] for pr in range(NP)]
    ms, alphas, ps_diag = sel_softmax(s_diag, ms, True)
    p_win, d_win = [], []
    for pr in range(NP):
        ss = []
        for c in range(nwt):
            ok = (i - (nwt - 1) + c) >= 0
            if c == 0:
                ok = ok & above
            elif c == nwt - 1:
                ok = ok & jnp.logical_not(above)
            ss.append(jnp.where(ok, s_win[pr][c], NEG))
        mx = ss[0].max(axis=0, keepdims=True)
        for c in range(1, nwt):
            mx = jnp.maximum(mx, ss[c].max(axis=0, keepdims=True))
        ps = [jnp.exp2(ss[c] - mx) for c in range(nwt)]
        den = ps[0].sum(axis=0, keepdims=True)
        for c in range(1, nwt):
            den = den + ps[c].sum(axis=0, keepdims=True)
        p_win.append([p.astype(BF16) for p in ps])
        d_win.append(den)
    sel_values(i, alphas, ps_diag)
    o_win = []
    for pr in range(NP):
        g = grp(pr)
        acc = _dot(vwt_refs[0][g * N_DIM:(g + 1) * N_DIM, :], p_win[pr][0])
        for c in range(1, nwt):
            acc = acc + _dot(vwt_refs[c][g * N_DIM:(g + 1) * N_DIM, :], p_win[pr][c])
        o_win.append(acc * (1.0 / d_win[pr]))
    o_sel = [acc_ref[pr, 0:N_DIM, :] * (1.0 / acc_ref[pr, N_DIM:N_DIM + 1, :]) for pr in range(NP)]

    for pr in range(NP):
        halves = []
        for par in range(2):
            h = 2 * pr + par
            cs = slice(par * QB, (par + 1) * QB)
            gc = gtt_ref[SM_G + 3 * h:SM_G + 3 * h + 1, :]
            gs = gtt_ref[SM_G + 3 * h + 1:SM_G + 3 * h + 2, :]
            gw = gtt_ref[SM_G + 3 * h + 2:SM_G + 3 * h + 3, :]
            halves.append(gc * o_cmp[pr][:, cs] + gs * o_sel[pr][:, cs] + gw * o_win[pr][:, cs])
        y = jnp.concatenate(halves, axis=0).T
        sl = slice(pr * LANES, (pr + 1) * LANES)
        out_ref[:, sl] = (y * _silu(z_ref[:, sl].astype(F32))).astype(out_ref.dtype)


def _nsa_attn(proj, qt, gtt, ka, vst, kwd, vwt, kcd, vct, ovt, batch, seq):
    QB = Q_BLOCK
    nqb = seq // QB
    ncmp = seq // CMP_STRIDE
    ntile = seq // KV_TILE
    nwt = WINDOW // QB + 1
    row = lambda w: pl.BlockSpec((QB, w), lambda b, i: (b * nqb + i, 0))
    colb = lambda r: pl.BlockSpec((r, QB), lambda b, i: (0, b * nqb + i))
    wblk = lambda b, i, c: b * nqb + jnp.maximum(i - (nwt - 1) + c, 0)
    in_specs = ([colb(N_W), colb(LANES), pl.BlockSpec((QB, N_W), lambda b, i: (b * nqb + i, C_NZ // N_W)),
                 pl.BlockSpec((seq, 4 * LANES), lambda b, i: (b, 0)),
                 pl.BlockSpec((nqb, N_GROUPS * V_ROWS, QB), lambda b, i: (b, 0, 0))]
                + [pl.BlockSpec((QB, 2 * LANES), lambda b, i, c=c: (wblk(b, i, c), 0)) for c in range(nwt)]
                + [pl.BlockSpec((LANES, QB), lambda b, i, c=c: (0, wblk(b, i, c))) for c in range(nwt)]
                + [pl.BlockSpec((ncmp, 2 * LANES), lambda b, i: (b, 0)),
                   pl.BlockSpec((1, LANES, ncmp), lambda b, i: (b, 0, 0)),
                   pl.BlockSpec((LANES, ncmp), lambda b, i: (0, 0))])
    return pl.pallas_call(
        functools.partial(_nsa_attn_kernel, ncmp=ncmp),
        grid=(batch, nqb),
        in_specs=in_specs,
        out_specs=row(N_W),
        out_shape=jax.ShapeDtypeStruct((batch * seq, N_W), BF16),
        scratch_shapes=[pltpu.VMEM((N_HEADS // 2, 2 * LANES, 2 * QB), BF16),
                        pltpu.VMEM((N_HEADS // 2, V_ROWS, 2 * QB), F32)],
        compiler_params=_cparams("parallel", "arbitrary"),
        name="nsa_attn",
    )(qt, gtt, proj, ka, vst, *([kwd] * nwt), *([vwt] * nwt), kcd, vct, ovt)


def _hgrn_kernel(f_ref, i_ref, q_ref, z_ref, lb_ref, hn_ref, out_ref, st_ref):
    T, D = H_TILE, H_DIM
    W = H_W

    @pl.when(pl.program_id(1) == 0)
    def _():
        st_ref[...] = jnp.zeros_like(st_ref)

    lb = lb_ref[...]
    fp = f_ref[...]
    e = jnp.exp(-jnp.abs(fp))
    r = 1.0 / (1.0 + e)
    sig = jnp.where(fp >= 0, r, e * r)
    nsig = jnp.where(fp >= 0, e * r, r)
    logf2 = jnp.log(lb + (1.0 - lb) * sig) * LOG2E
    kk = (1.0 - lb) * nsig

    row = lax.broadcasted_iota(jnp.int32, (T, T), 0)
    col = lax.broadcasted_iota(jnp.int32, (T, T), 1)
    tri01 = jnp.where(col <= row, 1.0, 0.0).astype(BF16)
    bcum = _dot01(tri01, logf2)
    rowi = lax.broadcasted_iota(jnp.int32, (T, 1), 0)

    segs = []
    sg = 2 * H_LAG
    while sg <= T:
        if sg == T:
            bmid = jnp.broadcast_to(bcum[sg // 2 - 1:sg // 2, :], (T, W))
        else:
            b3 = bcum.reshape(T // sg, sg, W)
            bmid = jnp.broadcast_to(b3[:, sg // 2 - 1:sg // 2, :], (T // sg, sg, W)).reshape(T, W)
        x = bcum - bmid
        shift = sg.bit_length() - 1
        pair_ok = ((jnp.right_shift(row, shift) == jnp.right_shift(col, shift))
                   & (jnp.bitwise_and(row, sg - 1) >= sg // 2) & (jnp.bitwise_and(col, sg - 1) < sg // 2))
        segs.append((jnp.exp2(jnp.minimum(x, -x)), pair_ok))
        sg *= 2

    sub3 = lax.broadcasted_iota(jnp.int32, (T // 8, 8, 1), 1)
    lag_ok = [jnp.bitwise_and(sub3, H_LAG - 1) >= d for d in range(H_LAG)]
    b_end = bcum[T - 1:T, :]
    v_all = i_ref[...].astype(F32)
    for h in range(H_HEADS):
        hs = slice(h * D, (h + 1) * D)
        q = _silu(q_ref[:, hs].astype(F32))
        k = kk[:, hs]
        b = bcum[:, hs]
        v = v_all[:, hs]
        vb = v.astype(BF16)
        a_seg = []
        for fac, _ in segs:
            f = fac[:, hs]
            a_seg.append(_dot_nt((q * f).astype(BF16), (k * f).astype(BF16)))
        st = st_ref[h]
        o_state = _dot_nt((q * jnp.exp2(b)).astype(BF16), st.astype(BF16))
        kt = (k * jnp.exp2(b_end[:, hs] - b)).astype(BF16)
        st_ref[h] = st * jnp.exp2(b_end[:, hs]) + _dot_tn(vb, kt)
        q3, k3, b3, v3 = (a.reshape(T // 8, 8, D) for a in (q, k, b, v))
        o3 = jnp.sum(q3 * k3, axis=2, keepdims=True) * v3
        for d in range(1, H_LAG):
            ks, bs, vs = (pltpu.roll(a, d, 1) for a in (k3, b3, v3))
            a = jnp.sum(q3 * ks * jnp.exp2(b3 - bs), axis=2, keepdims=True)
            o3 = o3 + jnp.where(lag_ok[d], a, 0.0) * vs
        a_tot = jnp.zeros((T, T), F32)
        for (_, pair_ok), a in zip(segs, a_seg):
            a_tot = jnp.where(pair_ok, a, a_tot)
        oh = o3.reshape(T, D) + o_state + _dot(a_tot.astype(BF16), vb)
        on = oh * lax.rsqrt(jnp.mean(oh * oh, axis=1, keepdims=True) + EPS) * hn_ref[:, hs]
        out_ref[:, hs] = (on * _silu(z_ref[:, hs].astype(F32))).astype(out_ref.dtype)


def _hgrn(proj, fgate, lb_row, hn_row, batch, seq):
    T = H_TILE
    nt = seq // T
    cb = lambda j: pl.BlockSpec((T, H_W), lambda b, c, j=j: (b * nt + c, j))
    vec = pl.BlockSpec((1, H_W), lambda b, c: (0, 0))
    return pl.pallas_call(
        _hgrn_kernel,
        grid=(batch, nt),
        in_specs=[pl.BlockSpec((T, H_W), lambda b, c: (b * nt + c, 0)), cb(1), cb(2), cb(3), vec, vec],
        out_specs=pl.BlockSpec((T, H_W), lambda b, c: (b * nt + c, 0)),
        out_shape=jax.ShapeDtypeStruct((batch * seq, H_W), BF16),
        scratch_shapes=[pltpu.VMEM((H_HEADS, H_DIM, H_DIM), F32)],
        compiler_params=_cparams("parallel", "arbitrary"),
        name="hgrn2",
    )(fgate, proj, proj, proj, lb_row, hn_row)


def _even_relayout(a):
    nb = 5 * M_W + 2 * M_HEADS
    g0 = nb + N_W + 6 * N_KV_W
    col = lambda lo, n: lax.slice_in_dim(a, lo, lo + n, axis=-1)
    pad = jnp.zeros(a.shape[:-1] + (C_NZ - C_SMALL - SM_G - 3 * N_HEADS,), a.dtype)
    out = jnp.concatenate([
        col(0, 5 * M_W),
        col(nb, N_W + 6 * N_KV_W),
        col(5 * M_W, 2 * M_HEADS),
        col(g0, 3 * N_HEADS), pad,
        col(g0 + 3 * N_HEADS, N_W),
    ], axis=-1)
    assert out.shape[-1] == EV_COLS
    return out


def _rope_tables(positions):
    inv_freq = ROPE_THETA ** (-np.arange(ROPE_HALF, dtype=np.float64) / ROPE_HALF)
    ang = positions.astype(np.float64)[:, None] * inv_freq[None, :]
    cos, sin = np.cos(ang).astype(np.float32), np.sin(ang).astype(np.float32)
    n = positions.shape[0]
    z8 = np.zeros((n, ROPE_HALF), np.float32)
    rest1 = np.ones((n, N_DIM - ROPE_DIM), np.float32)
    rest0 = np.zeros((n, N_DIM - ROPE_DIM), np.float32)
    cs = np.concatenate([cos, cos, rest1], axis=1)
    s1 = np.concatenate([z8, sin, rest0], axis=1)
    s2 = np.concatenate([-sin, z8, rest0], axis=1)
    two = lambda a: np.concatenate([a, a], axis=1)
    return two(cs), two(s1), two(s2)


def _overlap_matrix(seq):
    ncmp = seq // CMP_STRIDE
    n_sel = seq // SEL_BLOCK
    cs = np.arange(ncmp) * CMP_STRIDE
    ss = np.arange(n_sel) * SEL_BLOCK
    ov = np.clip(np.minimum(cs[:, None] + CMP_LEN, ss[None, :] + SEL_BLOCK)
                 - np.maximum(cs[:, None], ss[None, :]), 0, None).astype(np.float32)
    ov[ncmp - 1, :] = 0.0
    return np.pad(ov, ((0, 0), (0, LANES - n_sel)))


def _even_layer(x2d, norm_w, w_in, b_in, f_bias, conv_w, conv_b, head_norm, q_norm, k_norm,
                ck_pos, ck_w1, ck_w2, cv_pos, cv_w1, cv_w2, w_out, batch, seq):
    assert seq // SEL_BLOCK <= LANES and seq % KV_TILE == 0
    proj, small = _norm_proj(x2d, norm_w, _even_relayout(w_in.astype(BF16)), _even_relayout(b_in),
                             (C_SMALL, C_SMALL + LANES), tn=EV_COLS // 3)

    f_bias_row = jnp.zeros((1, LANES), F32).at[0, SM_F:SM_F + M_HEADS].set(f_bias)
    ya = _mlstm(proj, small, conv_w, conv_b, f_bias_row, head_norm, batch, seq)

    cs, s1, s2 = (jnp.asarray(a) for a in _rope_tables(np.arange(seq)))
    qn_row = jnp.concatenate([q_norm, q_norm]).reshape(1, LANES)
    kn_rows = jnp.concatenate([k_norm, k_norm], axis=1)
    qt, ka, kwd, vst, vwt, kc, vc, gtt = _nsa_prep(proj, small, cs, s1, s2, qn_row, kn_rows, batch, seq)

    ncmp = seq // CMP_STRIDE
    ccs, cs1, cs2 = (jnp.asarray(a) for a in _rope_tables(np.arange(ncmp) * CMP_STRIDE + CMP_LEN - 1))
    kcd, vct = _nsa_compress(
        kc.reshape(batch * ncmp, CMP_STRIDE * LANES), vc.reshape(batch * ncmp, CMP_STRIDE * LANES),
        _compress_params(ck_pos, ck_w1, ck_w2), _compress_params(cv_pos, cv_w1, cv_w2),
        ccs, cs1, cs2, kn_rows, batch, ncmp)

    ovt = jnp.asarray(_overlap_matrix(seq).T).astype(BF16)
    yb = _nsa_attn(proj, qt, gtt, ka, vst, kwd, vwt, kcd, vct, ovt, batch, seq)

    w_o = w_out.astype(BF16)
    return _out_proj2(ya, yb, w_o[:M_W], w_o[M_W:], x2d)


def _odd_layer(x2d, norm_w, w_in, b_in, lb, head_norm, w_out, batch, seq):
    proj, fgate = _norm_proj(x2d, norm_w, w_in.astype(BF16), b_in, (0, H_W), tn=H_W)
    y = _hgrn(proj, fgate, lb.reshape(1, H_W), head_norm.reshape(1, H_W), batch, seq)
    return _out_proj1(y, w_out.astype(BF16), x2d)


def kernel(x, norm_w, ev_w_in, ev_b_in, mlstm_f_bias, mlstm_conv_w, mlstm_conv_b, mlstm_head_norm, nsa_q_norm, nsa_k_norm, cmp_k_pos, cmp_k_w1, cmp_k_w2, cmp_v_pos, cmp_v_w1, cmp_v_w2, ev_w_out, od_w_in, od_b_in, hgrn_lb_logits, hgrn_head_norm, od_w_out):
    batch, seq, d = x.shape
    depth = norm_w.shape[0]
    lbs = jnp.cumsum(jax.nn.softmax(hgrn_lb_logits.astype(F32), axis=0), axis=0)
    x2d = x.reshape(batch * seq, d)
    for layer in range(depth):
        j = layer // 2
        if layer % 2 == 0:
            x2d = _even_layer(x2d, norm_w[layer], ev_w_in[j], ev_b_in[j], mlstm_f_bias[j], mlstm_conv_w[j],
                              mlstm_conv_b[j], mlstm_head_norm[j], nsa_q_norm[j], nsa_k_norm[j], cmp_k_pos[j],
                              cmp_k_w1[j], cmp_k_w2[j], cmp_v_pos[j], cmp_v_w1[j], cmp_v_w2[j], ev_w_out[j],
                              batch, seq)
        else:
            x2d = _odd_layer(x2d, norm_w[layer], od_w_in[j], od_b_in[j], lbs[j], hgrn_head_norm[j], od_w_out[j],
                             batch, seq)
    return x2d.reshape(batch, seq, d)
```

```python
import functools

import numpy as np
import jax
import jax.numpy as jnp
from jax import lax
from jax.experimental import pallas as pl
from jax.experimental.pallas import tpu as pltpu

F32 = jnp.float32
BF16 = jnp.bfloat16

LANES = 128
EPS = 1e-6
NEG = -1e30
VMEM_LIMIT_BYTES = 56 * 1024 * 1024

M_HEADS = 4
M_DIM = 128
M_W = M_HEADS * M_DIM
M_CHUNK = 128
CONV_K = 4
N_HEADS = 8
N_GROUPS = 2
N_HPG = N_HEADS // N_GROUPS
N_DIM = 64
N_W = N_HEADS * N_DIM
N_KV_W = N_GROUPS * N_DIM
CMP_LEN = 32
CMP_STRIDE = 16
CMP_HIDDEN = 128
CMP_CHUNK = 128
SEL_BLOCK = 64
SEL_SHIFT = 6
SEL_TOPK = 16
WINDOW = 512
Q_BLOCK = 128
ROPE_THETA = 500000.0
ROPE_DIM = N_DIM // 4
ROPE_HALF = ROPE_DIM // 2
KV_TILE = 512
V_ROWS = 80
LOG2E = 1.4426950408889634
H_HEADS = 8
H_DIM = 128
H_W = H_HEADS * H_DIM
H_TILE = 128
H_LAG = 2

C_MQ, C_MK, C_MV, C_MO, C_MZ = 0, 512, 1024, 1536, 2048
C_NQ = 2560
C_KC, C_VC, C_KS, C_VS = 3072, 3200, 3328, 3456
C_KW, C_VW, C_SMALL = 3584, 3712, 3840
C_NZ = 4096
EV_COLS = 4608
SM_I, SM_F, SM_G = 0, 4, 8


def _cparams(*sem):
    return pltpu.CompilerParams(dimension_semantics=sem, vmem_limit_bytes=VMEM_LIMIT_BYTES)


def _dot(a, b):
    return jnp.dot(a, b, preferred_element_type=F32)


def _dot_nt(a, b):
    return lax.dot_general(a, b, (((1,), (1,)), ((), ())), preferred_element_type=F32)


def _dot_tn(a, b):
    return lax.dot_general(a, b, (((0,), (0,)), ((), ())), preferred_element_type=F32)


def _sigmoid(x):
    return 1.0 / (1.0 + jnp.exp(-x))


def _silu(x):
    return x * _sigmoid(x)


def _split3(x):
    hi = x.astype(BF16)
    r = x - hi.astype(F32)
    mid = r.astype(BF16)
    lo = (r - mid.astype(F32)).astype(BF16)
    return hi, mid, lo


def _dot01(m01, x):
    hi, mid, lo = _split3(x)
    return _dot(m01, hi) + _dot(m01, mid) + _dot(m01, lo)


def _norm_proj_kernel(*refs, n_res, g_tiles, g_off):
    x_ref = refs[0]
    y_refs, wy_refs = refs[1:1 + n_res], refs[1 + n_res:1 + 2 * n_res]
    nw_ref, w_ref, b_ref = refs[1 + 2 * n_res:4 + 2 * n_res]
    outs = refs[4 + 2 * n_res:]
    xo_ref = outs[0] if n_res else None
    o_ref, g_ref, h_ref = outs[-3:]
    j = pl.program_id(1)

    @pl.when(j == 0)
    def _():
        x = x_ref[...]
        for y_ref, wy_ref in zip(y_refs, wy_refs):
            x = x + _dot(y_ref[...], wy_ref[...])
        if n_res:
            xo_ref[...] = x
        ms = jnp.mean(x * x, axis=-1, keepdims=True)
        h_ref[...] = (x * lax.rsqrt(ms + EPS) * nw_ref[...]).astype(BF16)

    acc = _dot(h_ref[...], w_ref[...]) + b_ref[...]
    o_ref[...] = acc.astype(o_ref.dtype)

    @pl.when((j >= g_tiles[0]) & (j < g_tiles[1]))
    def _():
        g_ref[...] = acc[:, g_off:g_off + g_ref.shape[1]]


def _norm_proj(x2d, norm_w, w_bf16, bias, g_cols, tn, res=(), tm=1024):
    t, d = x2d.shape
    n = w_bf16.shape[1]
    g_w = min(g_cols[1] - g_cols[0], tn)
    g_tiles = (g_cols[0] // tn, -(-g_cols[1] // tn))
    row = lambda w: pl.BlockSpec((tm, w), lambda i, j: (i, 0))
    const = lambda a: pl.BlockSpec(a.shape, lambda i, j: (0, 0))
    ys, wys = [y for y, _ in res], [wy for _, wy in res]
    out_specs = [pl.BlockSpec((tm, tn), lambda i, j: (i, j)),
                 pl.BlockSpec((tm, g_w), lambda i, j: (i, jnp.clip(j - g_tiles[0], 0, g_tiles[1] - g_tiles[0] - 1)))]
    out_shape = [jax.ShapeDtypeStruct((t, n), BF16), jax.ShapeDtypeStruct((t, g_cols[1] - g_cols[0]), F32)]
    if res:
        out_specs = [row(d)] + out_specs
        out_shape = [jax.ShapeDtypeStruct((t, d), F32)] + out_shape
    return pl.pallas_call(
        functools.partial(_norm_proj_kernel, n_res=len(res), g_tiles=g_tiles, g_off=g_cols[0] % tn),
        grid=(t // tm, n // tn),
        in_specs=([row(d)] + [row(y.shape[1]) for y in ys] + [const(wy) for wy in wys]
                  + [pl.BlockSpec((1, d), lambda i, j: (0, 0)),
                     pl.BlockSpec((d, tn), lambda i, j: (0, j)),
                     pl.BlockSpec((1, tn), lambda i, j: (0, j))]),
        out_specs=out_specs,
        out_shape=out_shape,
        scratch_shapes=[pltpu.VMEM((tm, d), BF16)],
        compiler_params=_cparams("parallel", "arbitrary"),
        name="norm_proj",
    )(x2d, *ys, *wys, norm_w.reshape(1, d), w_bf16, bias.reshape(1, n))


def _out_proj2_kernel(ya_ref, yb_ref, wa_ref, wb_ref, x_ref, o_ref):
    o_ref[...] = x_ref[...] + _dot(ya_ref[...], wa_ref[...]) + _dot(yb_ref[...], wb_ref[...])


def _out_proj2(ya, yb, wa, wb, x2d, tm=1024):
    t, d = x2d.shape
    ka, kb = ya.shape[1], yb.shape[1]
    return pl.pallas_call(
        _out_proj2_kernel,
        grid=(t // tm,),
        in_specs=[
            pl.BlockSpec((tm, ka), lambda i: (i, 0)),
            pl.BlockSpec((tm, kb), lambda i: (i, 0)),
            pl.BlockSpec((ka, d), lambda i: (0, 0)),
            pl.BlockSpec((kb, d), lambda i: (0, 0)),
            pl.BlockSpec((tm, d), lambda i: (i, 0)),
        ],
        out_specs=pl.BlockSpec((tm, d), lambda i: (i, 0)),
        out_shape=jax.ShapeDtypeStruct((t, d), F32),
        compiler_params=_cparams("parallel"),
        name="out_proj2",
    )(ya, yb, wa, wb, x2d)


def _out_proj1_kernel(y_ref, w_ref, x_ref, o_ref):
    o_ref[...] = x_ref[...] + _dot(y_ref[...], w_ref[...])


def _out_proj1(y, w, x2d, tm=1024):
    t, d = x2d.shape
    k = y.shape[1]
    return pl.pallas_call(
        _out_proj1_kernel,
        grid=(t // tm,),
        in_specs=[
            pl.BlockSpec((tm, k), lambda i: (i, 0)),
            pl.BlockSpec((k, d), lambda i: (0, 0)),
            pl.BlockSpec((tm, d), lambda i: (i, 0)),
        ],
        out_specs=pl.BlockSpec((tm, d), lambda i: (i, 0)),
        out_shape=jax.ShapeDtypeStruct((t, d), F32),
        compiler_params=_cparams("parallel"),
        name="out_proj1",
    )(y, w, x2d)


def _mlstm_kernel(q_ref, k_ref, v_ref, o_ref, z_ref, sm_ref, cw_ref, cb_ref, fb_ref, hn_ref,
                  out_ref, xbuf, c_ref, n_ref, m_ref):
    L, D = M_CHUNK, M_DIM
    NB = q_ref.shape[0]
    XR = xbuf.shape[1]

    @pl.when(pl.program_id(0) == 0)
    def _():
        xbuf[...] = jnp.zeros_like(xbuf)
        c_ref[...] = jnp.zeros_like(c_ref)
        n_ref[...] = jnp.zeros_like(n_ref)
        m_ref[...] = jnp.zeros_like(m_ref)

    t_i = lax.broadcasted_iota(jnp.int32, (L, XR), 0)
    r_i = lax.broadcasted_iota(jnp.int32, (L, XR), 1)
    row = lax.broadcasted_iota(jnp.int32, (L, L), 0)
    col = lax.broadcasted_iota(jnp.int32, (L, L), 1)
    causal = col <= row
    lane = lax.broadcasted_iota(jnp.int32, (L, LANES), 1)
    is_f = (lane >= SM_F) & (lane < SM_F + M_HEADS)

    heads = []
    for b in range(NB):
        xbuf[b, 16:16 + L, 0:M_W] = q_ref[b]
        xbuf[b, 16:16 + L, M_W:2 * M_W] = k_ref[b]
        x2 = xbuf[b]
        acc = cb_ref[...] + cw_ref[CONV_K - 1:CONV_K, :] * x2[16:16 + L].astype(F32)
        for d in range(1, CONV_K):
            shift = jnp.where(r_i == t_i + 16 - d, 1.0, 0.0).astype(BF16)
            acc = acc + cw_ref[CONV_K - 1 - d:CONV_K - d, :] * _dot(shift, x2)
        qk = _silu(acc)
        xbuf[b, 0:16, :] = xbuf[b, L:L + 16, :]

        fpre = sm_ref[b] + fb_ref[...]
        gates = jnp.where(is_f, jnp.minimum(fpre, 0.0) - jnp.log(1.0 + jnp.exp(-jnp.abs(fpre))), fpre)
        bcols = _dot01(jnp.where(causal, 1.0, 0.0).astype(BF16), gates)
        gates_t = gates.T
        bcols_t = bcols.T
        for h in range(M_HEADS):
            hs = slice(h * D, (h + 1) * D)
            q = qk[:, hs]
            k = qk[:, M_W + h * D:M_W + (h + 1) * D] * (D ** -0.5)
            i_col = gates[:, SM_I + h:SM_I + h + 1]
            i_row = gates_t[SM_I + h:SM_I + h + 1, :]
            bcum_col = bcols[:, SM_F + h:SM_F + h + 1]
            bcum_row = bcols_t[SM_F + h:SM_F + h + 1, :]
            m_prev = m_ref[b * M_HEADS + h]
            dmat = jnp.where(causal, bcum_col - bcum_row + i_row, -jnp.inf)
            inter = bcum_col + m_prev
            m_row = jnp.maximum(jnp.max(dmat, axis=1, keepdims=True), inter)
            b_last = bcum_col[L - 1:L, :]
            dl = b_last - bcum_col + i_col
            m_new = jnp.maximum(b_last + m_prev, jnp.max(dl, axis=0, keepdims=True))
            wkk = jnp.exp(dl - m_new) * k
            heads.append(dict(
                b=b, hs=hs, st=b * M_HEADS + h, q=q, qb=q.astype(BF16), kb=k.astype(BF16),
                vb=v_ref[b, :, hs].astype(BF16), wkk=wkk,
                e=jnp.exp(dmat - m_row), w_inter=jnp.exp(inter - m_row), floor=jnp.exp(-m_row),
                decay=jnp.exp(b_last + m_prev - m_new), m_new=m_new))
    s_qk = [_dot_nt(t["qb"], t["kb"]) for t in heads]
    s_qc = [_dot(t["qb"], c_ref[t["st"]].astype(BF16)) for t in heads]
    s_kv = [_dot_tn(t["wkk"].astype(BF16), t["vb"]) for t in heads]
    smats = [s_qk[j] * t["e"] for j, t in enumerate(heads)]
    s_sv = [_dot(smats[j].astype(BF16), t["vb"]) for j, t in enumerate(heads)]
    for j, t in enumerate(heads):
        b, hs, st = t["b"], t["hs"], t["st"]
        num = s_sv[j] + t["w_inter"] * s_qc[j]
        den = (jnp.sum(smats[j], axis=1, keepdims=True)
               + t["w_inter"] * jnp.sum(t["q"] * n_ref[st], axis=1, keepdims=True))
        hh = num / jnp.maximum(jnp.abs(den), t["floor"])
        c_ref[st] = t["decay"] * c_ref[st] + s_kv[j]
        n_ref[st] = t["decay"] * n_ref[st] + jnp.sum(t["wkk"], axis=0, keepdims=True)
        m_ref[st] = t["m_new"]
        hn = hh * lax.rsqrt(jnp.mean(hh * hh, axis=1, keepdims=True) + EPS) * hn_ref[:, hs]
        gate = _sigmoid(o_ref[b, :, hs].astype(F32)) * _silu(z_ref[b, :, hs].astype(F32))
        out_ref[b, :, hs] = (hn * gate).astype(out_ref.dtype)


def _mlstm(proj, small, conv_w, conv_b, f_bias_row, head_norm, batch, seq):
    L = M_CHUNK
    nc = seq // L
    p3 = proj.reshape(batch, seq, proj.shape[1])
    s3 = small.reshape(batch, seq, LANES)
    cb = lambda off: pl.BlockSpec((batch, L, M_W), lambda c, off=off: (0, c, off // M_W))
    full = lambda r, w: pl.BlockSpec((r, w), lambda c: (0, 0))
    out = pl.pallas_call(
        _mlstm_kernel,
        grid=(nc,),
        in_specs=[
            cb(C_MQ), cb(C_MK), cb(C_MV), cb(C_MO), cb(C_MZ),
            pl.BlockSpec((batch, L, LANES), lambda c: (0, c, 0)),
            full(CONV_K, 2 * M_W), full(1, 2 * M_W), full(1, LANES), full(1, M_W),
        ],
        out_specs=pl.BlockSpec((batch, L, M_W), lambda c: (0, c, 0)),
        out_shape=jax.ShapeDtypeStruct((batch, seq, M_W), BF16),
        scratch_shapes=[
            pltpu.VMEM((batch, 2 * L, 2 * M_W), BF16),
            pltpu.VMEM((batch * M_HEADS, M_DIM, M_DIM), F32),
            pltpu.VMEM((batch * M_HEADS, 1, M_DIM), F32),
            pltpu.VMEM((batch * M_HEADS, 1, 1), F32),
        ],
        compiler_params=_cparams("arbitrary"),
        name="mlstm",
    )(p3, p3, p3, p3, p3, s3, conv_w, conv_b.reshape(1, -1), f_bias_row, head_norm.reshape(1, -1))
    return out.reshape(batch * seq, M_W)


def _group_rms(x, w_row):
    lane = lax.broadcasted_iota(jnp.int32, x.shape, 1)
    left = lane < N_DIM
    sq = x * x
    tot = jnp.sum(sq, axis=1, keepdims=True)
    lft = jnp.sum(jnp.where(left, sq, 0.0), axis=1, keepdims=True)
    ms = jnp.where(left, lft, tot - lft) * (1.0 / N_DIM)
    return x * lax.rsqrt(ms + EPS) * w_row


def _rope(x, cs, s1, s2):
    return x * cs + pltpu.roll(x, ROPE_HALF, 1) * s1 + pltpu.roll(x, LANES - ROPE_HALF, 1) * s2


def _dup_halves(x):
    lane = lax.broadcasted_iota(jnp.int32, x.shape, 1)
    sw = pltpu.roll(x, N_DIM, 1)
    left = lane < N_DIM
    return jnp.where(left, x, sw), jnp.where(left, sw, x)


def _nsa_prep_kernel(q_ref, kv_ref, kw_ref, sm_ref, cs_ref, s1_ref, s2_ref, qn_ref, kn_ref,
                     qt_ref, ka_ref, kwd_ref, vst_ref, vwt_ref, kc_ref, vc_ref, gtt_ref,
                     *, tp, seq):
    cs, s1, s2 = cs_ref[...], s1_ref[...], s2_ref[...]
    for j in range(N_W // LANES):
        sl = slice(j * LANES, (j + 1) * LANES)
        qn = _group_rms(q_ref[:, sl].astype(F32), qn_ref[...])
        qt_ref[sl, :] = (_rope(qn, cs, s1, s2) * (N_DIM ** -0.5 * LOG2E)).T.astype(BF16)
    kv = kv_ref[...].astype(F32)
    kc_ref[...] = kv[:, 0:LANES]
    vc_ref[...] = kv[:, LANES:2 * LANES]
    ks = _rope(_group_rms(kv[:, 2 * LANES:3 * LANES], kn_ref[1:2, :]), cs, s1, s2)
    k0, k1 = _dup_halves(ks)
    pos0 = (pl.program_id(0) * tp) % seq
    pos = pos0 + lax.broadcasted_iota(jnp.int32, (tp, LANES), 0)
    blk = lax.broadcasted_iota(jnp.int32, (tp, LANES), 1)
    onehot = jnp.where(jnp.right_shift(pos, SEL_SHIFT) == blk, 1.0, 0.0).astype(BF16)
    ka_ref[:, 0:LANES] = k0.astype(BF16)
    ka_ref[:, LANES:2 * LANES] = onehot
    ka_ref[:, 2 * LANES:3 * LANES] = k1.astype(BF16)
    ka_ref[:, 3 * LANES:4 * LANES] = onehot
    vst = kv[:, 3 * LANES:4 * LANES].T
    fill = (lax.broadcasted_iota(jnp.int32, (V_ROWS - N_DIM, Q_BLOCK), 0) == 0).astype(BF16)
    for c in range(tp // Q_BLOCK):
        for g in range(N_GROUPS):
            vst_ref[c, g * V_ROWS:g * V_ROWS + N_DIM, :] = (
                vst[g * N_DIM:(g + 1) * N_DIM, c * Q_BLOCK:(c + 1) * Q_BLOCK].astype(BF16))
            vst_ref[c, g * V_ROWS + N_DIM:(g + 1) * V_ROWS, :] = fill
    kwb = kw_ref[...].astype(F32)
    kw = _rope(_group_rms(kwb[:, 0:LANES], kn_ref[2:3, :]), cs, s1, s2)
    w0, w1 = _dup_halves(kw)
    kwd_ref[:, 0:LANES] = w0.astype(BF16)
    kwd_ref[:, LANES:2 * LANES] = w1.astype(BF16)
    vwt_ref[...] = kwb[:, LANES:2 * LANES].T.astype(BF16)
    gtt_ref[...] = _sigmoid(sm_ref[...]).T


def _nsa_prep(proj, small, cs, s1, s2, qn_row, kn_rows, batch, seq):
    tp = KV_TILE
    t = batch * seq
    nsb = seq // tp
    rb = lambda w, off: pl.BlockSpec((tp, w), lambda i, off=off, w=w: (i, off // w))
    tab = pl.BlockSpec((tp, LANES), lambda i: (i % nsb, 0))
    o = lambda w: pl.BlockSpec((tp, w), lambda i: (i, 0))
    ot = lambda r: pl.BlockSpec((r, tp), lambda i: (0, i))
    sds = lambda w, dt: jax.ShapeDtypeStruct((t, w), dt)
    return pl.pallas_call(
        functools.partial(_nsa_prep_kernel, tp=tp, seq=seq),
        grid=(t // tp,),
        in_specs=[rb(512, C_NQ), rb(512, C_KC), rb(512, C_KW), o(LANES), tab, tab, tab,
                  pl.BlockSpec((1, LANES), lambda i: (0, 0)), pl.BlockSpec((3, LANES), lambda i: (0, 0))],
        out_specs=[ot(N_W), o(512), o(256),
                   pl.BlockSpec((tp // Q_BLOCK, N_GROUPS * V_ROWS, Q_BLOCK), lambda i: (i, 0, 0)),
                   ot(LANES), o(128), o(128), ot(LANES)],
        out_shape=[jax.ShapeDtypeStruct((N_W, t), BF16), sds(512, BF16), sds(256, BF16),
                   jax.ShapeDtypeStruct((t // Q_BLOCK, N_GROUPS * V_ROWS, Q_BLOCK), BF16),
                   jax.ShapeDtypeStruct((LANES, t), BF16),
                   sds(128, F32), sds(128, F32), jax.ShapeDtypeStruct((LANES, t), F32)],
        compiler_params=_cparams("parallel"),
        name="nsa_prep",
    )(proj, proj, proj, small, cs, s1, s2, qn_row, kn_rows)


def _gelu_tanh(x):
    return 0.5 * x * (1.0 + jnp.tanh(0.7978845608028654 * (x + 0.044715 * x * x * x)))


def _compress_mlp(x_ref, pa_ref, pb_ref, wa_ref, wb_ref, w2_ref, hb_ref):
    n = x_ref.shape[0]
    x = x_ref[...]
    ha = _dot((x + pa_ref[...]).astype(BF16), wa_ref[...])
    hb_ref[0:n, :] = _dot((x + pb_ref[...]).astype(BF16), wb_ref[...])
    hb_ref[n:n + 8, :] = jnp.zeros((8, hb_ref.shape[1]), F32)
    hid = ha + hb_ref[1:n + 1, :]
    return _dot(_gelu_tanh(hid).astype(BF16), w2_ref[...])


def _nsa_compress_kernel(kx_ref, vx_ref, kpa_ref, kpb_ref, vpa_ref, vpb_ref, kwa_ref, kwb_ref, kw2_ref,
                         vwa_ref, vwb_ref, vw2_ref, cs_ref, s1_ref, s2_ref, kn_ref,
                         kcd_ref, vct_ref, hb_ref):
    n = kx_ref.shape[0]
    rowi = lax.broadcasted_iota(jnp.int32, (n, LANES), 0)
    live = rowi < n - 1
    kc = _compress_mlp(kx_ref, kpa_ref, kpb_ref, kwa_ref, kwb_ref, kw2_ref, hb_ref)
    kc = _rope(_group_rms(kc, kn_ref[0:1, :]), cs_ref[...], s1_ref[...], s2_ref[...])
    kc = jnp.where(live, kc, 0.0)
    c0, c1 = _dup_halves(kc)
    kcd_ref[:, 0:LANES] = c0.astype(BF16)
    kcd_ref[:, LANES:2 * LANES] = c1.astype(BF16)
    vc = _compress_mlp(vx_ref, vpa_ref, vpb_ref, vwa_ref, vwb_ref, vw2_ref, hb_ref)
    vc = jnp.where(live, vc, 0.0)
    vct_ref[0] = vc.T.astype(BF16)


def _nsa_compress(kx, vx, kp, vp, cs, s1, s2, kn_rows, batch, nrow):
    wide = CMP_STRIDE * LANES
    full = lambda a: pl.BlockSpec(a.shape, lambda b: (0,) * a.ndim)
    xs = pl.BlockSpec((nrow, wide), lambda b: (b, 0))
    args = (kx, vx, kp[0], kp[1], vp[0], vp[1], kp[2], kp[3], kp[4], vp[2], vp[3], vp[4], cs, s1, s2, kn_rows)
    return pl.pallas_call(
        _nsa_compress_kernel,
        grid=(batch,),
        in_specs=[xs, xs] + [full(a) for a in args[2:]],
        out_specs=[pl.BlockSpec((nrow, 2 * LANES), lambda b: (b, 0)),
                   pl.BlockSpec((1, LANES, nrow), lambda b: (b, 0, 0))],
        out_shape=[jax.ShapeDtypeStruct((batch * nrow, 2 * LANES), BF16),
                   jax.ShapeDtypeStruct((batch, LANES, nrow), BF16)],
        scratch_shapes=[pltpu.VMEM((nrow + 8, 2 * CMP_HIDDEN), F32)],
        compiler_params=_cparams("parallel"),
        name="nsa_compress",
    )(*args)


def _compress_params(pos, w1, w2):
    def expand(w1_half, pos_half):
        w = w1_half.reshape(CMP_STRIDE, N_DIM, CMP_HIDDEN)
        z = jnp.zeros_like(w)
        g0 = jnp.concatenate([w, z], axis=-1)
        g1 = jnp.concatenate([z, w], axis=-1)
        wx = jnp.stack([g0, g1], axis=1).reshape(CMP_STRIDE * LANES, 2 * CMP_HIDDEN)
        px = jnp.concatenate([pos_half, pos_half], axis=-1).reshape(1, CMP_STRIDE * LANES)
        return wx.astype(BF16), px.astype(F32)

    half = CMP_STRIDE * N_DIM
    wa, pa = expand(w1[:half], pos[:CMP_STRIDE])
    wb, pb = expand(w1[half:], pos[CMP_STRIDE:])
    z2 = jnp.zeros_like(w2)
    w2x = jnp.concatenate([jnp.concatenate([w2, z2], axis=1), jnp.concatenate([z2, w2], axis=1)], axis=0)
    return pa, pb, wa, wb, w2x.astype(BF16)


def _nsa_attn_kernel(qt_ref, gtt_ref, z_ref, ka_ref, vst_ref,
                     kw0, kw1, kw2, kw3, kw4, wt0, wt1, wt2, wt3, wt4,
                     kcd_ref, vct_ref, ovt_ref, out_ref, qa_ref, acc_ref, *, ncmp):
    QB = Q_BLOCK
    NP = N_HEADS // 2
    PW = 2 * QB
    i = pl.program_id(1)
    t0 = i * QB
    kw_refs = (kw0, kw1, kw2, kw3, kw4)
    vwt_refs = (wt0, wt1, wt2, wt3, wt4)
    nwt = len(kw_refs)

    sub = lax.broadcasted_iota(jnp.int32, (QB, QB), 0)
    lane = lax.broadcasted_iota(jnp.int32, (QB, QB), 1)
    top = sub < N_DIM
    t_row = t0 + lax.broadcasted_iota(jnp.int32, (1, QB), 1)
    t_row2 = jnp.concatenate([t_row, t_row], axis=1)

    def pair_q(pr):
        slab = qt_ref[pr * LANES:(pr + 1) * LANES, :]
        zero = jnp.zeros_like(slab)
        return jnp.concatenate([jnp.where(top, slab, zero), jnp.where(top, zero, slab)], axis=1)

    blk_f = sub.astype(F32)
    cur = jnp.right_shift(t_row, SEL_SHIFT)
    grp = lambda pr: pr // (N_HPG // 2)
    for pr in range(NP):
        qa_ref[pr, 0:LANES, :] = pair_q(pr)

    def compressed(n):
        nidx = lax.broadcasted_iota(jnp.int32, (n, PW), 0)
        ok = (nidx * CMP_STRIDE + (CMP_LEN - 1) <= t_row2) & (nidx < ncmp - 1)
        s_cmp = [_dot(kcd_ref[0:n, grp(pr) * LANES:(grp(pr) + 1) * LANES], qa_ref[pr, 0:LANES, :])
                 for pr in range(NP)]
        p_cmp = []
        for pr in range(NP):
            s = jnp.where(ok, s_cmp[pr], NEG)
            mx = jnp.max(s, axis=0, keepdims=True)
            p = jnp.where(ok, jnp.exp2(s - mx), 0.0)
            den = jnp.sum(p, axis=0, keepdims=True)
            p_cmp.append(p * (1.0 / jnp.maximum(den, 1e-30)))
        outs = [_dot(vct_ref[0, grp(pr) * N_DIM:(grp(pr) + 1) * N_DIM, 0:n], p_cmp[pr].astype(BF16))
                for pr in range(NP)]
        for g in range(N_GROUPS):
            psum = jnp.zeros((n, QB), F32)
            for pp in range(N_HPG // 2):
                p = p_cmp[g * (N_HPG // 2) + pp]
                psum = psum + p[:, 0:QB] + p[:, QB:PW]
            p_hi = psum.astype(BF16)
            p_lo = (psum - p_hi.astype(F32)).astype(BF16)
            outs.append(_dot(ovt_ref[:, 0:n], p_hi) + _dot(ovt_ref[:, 0:n], p_lo))
        return tuple(outs)

    n_opts = list(range(CMP_CHUNK, ncmp + 1, CMP_CHUNK))
    need = (t0 + QB - CMP_LEN) // CMP_STRIDE + 1
    outs = lax.switch(jnp.clip((need - 1) // CMP_CHUNK, 0, len(n_opts) - 1),
                      [functools.partial(compressed, n) for n in n_opts])
    o_cmp, imps = outs[:NP], outs[NP:]
    forced = (sub == 0) | (sub == cur) | (sub == cur - 1)
    works = [jnp.where(sub > cur, NEG, jnp.where(forced, -NEG, imp)) for imp in imps]
    sels = [jnp.zeros((QB, QB), jnp.bool_) for _ in range(N_GROUPS)]
    for _ in range(SEL_TOPK):
        for g in range(N_GROUPS):
            mx = jnp.max(works[g], axis=0, keepdims=True)
            idx = jnp.min(jnp.where(works[g] == mx, blk_f, float(LANES)), axis=0, keepdims=True)
            pick = blk_f == idx
            sels[g] = sels[g] | pick
            works[g] = jnp.where(pick, -jnp.inf, works[g])
    for g in range(N_GROUPS):
        neg = jnp.where(sels[g] & (sub <= cur), 0.0, NEG).astype(BF16)
        neg2 = jnp.concatenate([neg, neg], axis=1)
        for pp in range(N_HPG // 2):
            qa_ref[g * (N_HPG // 2) + pp, LANES:2 * LANES, :] = neg2

    CH = QB
    above = jnp.concatenate([sub > lane, sub > lane], axis=1)

    def sel_scores(cj):
        k0 = pl.multiple_of(cj * CH, CH)
        return [_dot(ka_ref[pl.ds(k0, CH), 2 * grp(pr) * LANES:2 * (grp(pr) + 1) * LANES], qa_ref[pr])
                for pr in range(NP)]

    def sel_softmax(ss, ms, masked):
        new, alphas, ps = [], [], []
        for pr in range(NP):
            s = jnp.where(above, NEG, ss[pr]) if masked else ss[pr]
            m_new = jnp.maximum(ms[pr], jnp.max(s, axis=0, keepdims=True))
            alphas.append(jnp.exp2(ms[pr] - m_new))
            ps.append(jnp.exp2(s - m_new).astype(BF16))
            new.append(m_new)
        return tuple(new), alphas, ps

    def sel_values(cj, alphas, ps):
        for pr in range(NP):
            g = grp(pr)
            acc_ref[pr] = alphas[pr] * acc_ref[pr] + _dot(vst_ref[cj, g * V_ROWS:(g + 1) * V_ROWS, :], ps[pr])

    def sel_chunk(cj, ms, masked):
        ms, alphas, ps = sel_softmax(sel_scores(cj), ms, masked)
        sel_values(cj, alphas, ps)
        return ms

    def sel_tile(jt, ms, nc):
        c0 = jt * nc
        k0 = pl.multiple_of(c0 * CH, CH)
        big = [_dot(ka_ref[pl.ds(k0, nc * CH), 2 * grp(pr) * LANES:2 * (grp(pr) + 1) * LANES], qa_ref[pr])
               for pr in range(NP)]
        for c in range(nc):
            ss = [b[c * CH:(c + 1) * CH] for b in big]
            ms, alphas, ps = sel_softmax(ss, ms, False)
            sel_values(c0 + c, alphas, ps)
        return ms

    acc_ref[...] = jnp.zeros_like(acc_ref)
    ms = tuple(jnp.full((1, PW), 2.0 * NEG, F32) for _ in range(NP))
    done = 0
    for nc in (4 * KV_TILE // CH, 2 * KV_TILE // CH, KV_TILE // CH):
        n_run = (i - done) // nc
        ms = lax.fori_loop(done // nc, done // nc + n_run, functools.partial(sel_tile, nc=nc), ms)
        done = done + n_run * nc
    ms = lax.fori_loop(done, i, lambda cj, ms: sel_chunk(cj, ms, False), ms)
    s_diag = sel_scores(i)
    s_win = [[_dot(kw_refs[c][:, grp(pr) * LANES:(grp(pr) + 1) * LANES], qa_ref[pr, 0:LANES, :])
              for c in range(nwt)] for pr in range(NP)]
    ms, alphas, ps_diag = sel_softmax(s_diag, ms, True)
    p_win, d_win = [], []
    for pr in range(NP):
        ss = []
        for c in range(nwt):
            ok = (i - (nwt - 1) + c) >= 0
            if c == 0:
                ok = ok & above
            elif c == nwt - 1:
                ok = ok & jnp.logical_not(above)
            ss.append(jnp.where(ok, s_win[pr][c], NEG))
        mx = ss[0].max(axis=0, keepdims=True)
        for c in range(1, nwt):
            mx = jnp.maximum(mx, ss[c].max(axis=0, keepdims=True))
        ps = [jnp.exp2(ss[c] - mx) for c in range(nwt)]
        den = ps[0].sum(axis=0, keepdims=True)
        for c in range(1, nwt):
            den = den + ps[c].sum(axis=0, keepdims=True)
        p_win.append([p.astype(BF16) for p in ps])
        d_win.append(den)
    sel_values(i, alphas, ps_diag)
    o_win = []
    for pr in range(NP):
        g = grp(pr)
        acc = _dot(vwt_refs[0][g * N_DIM:(g + 1) * N_DIM, :], p_win[pr][0])
        for c in range(1, nwt):
            acc = acc + _dot(vwt_refs[c][g * N_DIM:(g + 1) * N_DIM, :], p_win[pr][c])
        o_win.append(acc * (1.0 / d_win[pr]))
    o_sel = [acc_ref[pr, 0:N_DIM, :] * (1.0 / acc_ref[pr, N_DIM:N_DIM + 1, :]) for pr in range(NP)]

    for pr in range(NP):
        halves = []
        for par in range(2):
            h = 2 * pr + par
            cs = slice(par * QB, (par + 1) * QB)
            gc = gtt_ref[SM_G + 3 * h:SM_G + 3 * h + 1, :]
            gs = gtt_ref[SM_G + 3 * h + 1:SM_G + 3 * h + 2, :]
            gw = gtt_ref[SM_G + 3 * h + 2:SM_G + 3 * h + 3, :]
            halves.append(gc * o_cmp[pr][:, cs] + gs * o_sel[pr][:, cs] + gw * o_win[pr][:, cs])
        y = jnp.concatenate(halves, axis=0).T
        sl = slice(pr * LANES, (pr + 1) * LANES)
        out_ref[:, sl] = (y * _silu(z_ref[:, sl].astype(F32))).astype(out_ref.dtype)


def _nsa_attn(proj, qt, gtt, ka, vst, kwd, vwt, kcd, vct, ovt, batch, seq):
    QB = Q_BLOCK
    nqb = seq // QB
    ncmp = seq // CMP_STRIDE
    ntile = seq // KV_TILE
    nwt = WINDOW // QB + 1
    row = lambda w: pl.BlockSpec((QB, w), lambda b, i: (b * nqb + i, 0))
    colb = lambda r: pl.BlockSpec((r, QB), lambda b, i: (0, b * nqb + i))
    wblk = lambda b, i, c: b * nqb + jnp.maximum(i - (nwt - 1) + c, 0)
    in_specs = ([colb(N_W), colb(LANES), pl.BlockSpec((QB, N_W), lambda b, i: (b * nqb + i, C_NZ // N_W)),
                 pl.BlockSpec((seq, 4 * LANES), lambda b, i: (b, 0)),
                 pl.BlockSpec((nqb, N_GROUPS * V_ROWS, QB), lambda b, i: (b, 0, 0))]
                + [pl.BlockSpec((QB, 2 * LANES), lambda b, i, c=c: (wblk(b, i, c), 0)) for c in range(nwt)]
                + [pl.BlockSpec((LANES, QB), lambda b, i, c=c: (0, wblk(b, i, c))) for c in range(nwt)]
                + [pl.BlockSpec((ncmp, 2 * LANES), lambda b, i: (b, 0)),
                   pl.BlockSpec((1, LANES, ncmp), lambda b, i: (b, 0, 0)),
                   pl.BlockSpec((LANES, ncmp), lambda b, i: (0, 0))])
    return pl.pallas_call(
        functools.partial(_nsa_attn_kernel, ncmp=ncmp),
        grid=(batch, nqb),
        in_specs=in_specs,
        out_specs=row(N_W),
        out_shape=jax.ShapeDtypeStruct((batch * seq, N_W), BF16),
        scratch_shapes=[pltpu.VMEM((N_HEADS // 2, 2 * LANES, 2 * QB), BF16),
                        pltpu.VMEM((N_HEADS // 2, V_ROWS, 2 * QB), F32)],
        compiler_params=_cparams("parallel", "arbitrary"),
        name="nsa_attn",
    )(qt, gtt, proj, ka, vst, *([kwd] * nwt), *([vwt] * nwt), kcd, vct, ovt)


def _hgrn_kernel(f_ref, i_ref, q_ref, z_ref, lb_ref, hn_ref, out_ref, st_ref):
    T, D = H_TILE, H_DIM
    W = H_W

    @pl.when(pl.program_id(1) == 0)
    def _():
        st_ref[...] = jnp.zeros_like(st_ref)

    lb = lb_ref[...]
    fp = f_ref[...]
    e = jnp.exp(-jnp.abs(fp))
    r = 1.0 / (1.0 + e)
    sig = jnp.where(fp >= 0, r, e * r)
    nsig = jnp.where(fp >= 0, e * r, r)
    logf2 = jnp.log(lb + (1.0 - lb) * sig) * LOG2E
    kk = (1.0 - lb) * nsig

    row = lax.broadcasted_iota(jnp.int32, (T, T), 0)
    col = lax.broadcasted_iota(jnp.int32, (T, T), 1)
    tri01 = jnp.where(col <= row, 1.0, 0.0).astype(BF16)
    bcum = _dot01(tri01, logf2)
    rowi = lax.broadcasted_iota(jnp.int32, (T, 1), 0)

    segs = []
    sg = 2 * H_LAG
    while sg <= T:
        if sg == T:
            bmid = jnp.broadcast_to(bcum[sg // 2 - 1:sg // 2, :], (T, W))
        else:
            b3 = bcum.reshape(T // sg, sg, W)
            bmid = jnp.broadcast_to(b3[:, sg // 2 - 1:sg // 2, :], (T // sg, sg, W)).reshape(T, W)
        x = bcum - bmid
        shift = sg.bit_length() - 1
        pair_ok = ((jnp.right_shift(row, shift) == jnp.right_shift(col, shift))
                   & (jnp.bitwise_and(row, sg - 1) >= sg // 2) & (jnp.bitwise_and(col, sg - 1) < sg // 2))
        segs.append((jnp.exp2(jnp.minimum(x, -x)), pair_ok))
        sg *= 2

    sub3 = lax.broadcasted_iota(jnp.int32, (T // 8, 8, 1), 1)
    lag_ok = [jnp.bitwise_and(sub3, H_LAG - 1) >= d for d in range(H_LAG)]
    b_end = bcum[T - 1:T, :]
    v_all = i_ref[...].astype(F32)
    for h in range(H_HEADS):
        hs = slice(h * D, (h + 1) * D)
        q = _silu(q_ref[:, hs].astype(F32))
        k = kk[:, hs]
        b = bcum[:, hs]
        v = v_all[:, hs]
        vb = v.astype(BF16)
        a_seg = []
        for fac, _ in segs:
            f = fac[:, hs]
            a_seg.append(_dot_nt((q * f).astype(BF16), (k * f).astype(BF16)))
        st = st_ref[h]
        o_state = _dot_nt((q * jnp.exp2(b)).astype(BF16), st.astype(BF16))
        kt = (k * jnp.exp2(b_end[:, hs] - b)).astype(BF16)
        st_ref[h] = st * jnp.exp2(b_end[:, hs]) + _dot_tn(vb, kt)
        q3, k3, b3, v3 = (a.reshape(T // 8, 8, D) for a in (q, k, b, v))
        o3 = jnp.sum(q3 * k3, axis=2, keepdims=True) * v3
        for d in range(1, H_LAG):
            ks, bs, vs = (pltpu.roll(a, d, 1) for a in (k3, b3, v3))
            a = jnp.sum(q3 * ks * jnp.exp2(b3 - bs), axis=2, keepdims=True)
            o3 = o3 + jnp.where(lag_ok[d], a, 0.0) * vs
        a_tot = jnp.zeros((T, T), F32)
        for (_, pair_ok), a in zip(segs, a_seg):
            a_tot = jnp.where(pair_ok, a, a_tot)
        oh = o3.reshape(T, D) + o_state + _dot(a_tot.astype(BF16), vb)
        on = oh * lax.rsqrt(jnp.mean(oh * oh, axis=1, keepdims=True) + EPS) * hn_ref[:, hs]
        out_ref[:, hs] = (on * _silu(z_ref[:, hs].astype(F32))).astype(out_ref.dtype)


def _hgrn(proj, fgate, lb_row, hn_row, batch, seq):
    T = H_TILE
    nt = seq // T
    cb = lambda j: pl.BlockSpec((T, H_W), lambda b, c, j=j: (b * nt + c, j))
    vec = pl.BlockSpec((1, H_W), lambda b, c: (0, 0))
    return pl.pallas_call(
        _hgrn_kernel,
        grid=(batch, nt),
        in_specs=[pl.BlockSpec((T, H_W), lambda b, c: (b * nt + c, 0)), cb(1), cb(2), cb(3), vec, vec],
        out_specs=pl.BlockSpec((T, H_W), lambda b, c: (b * nt + c, 0)),
        out_shape=jax.ShapeDtypeStruct((batch * seq, H_W), BF16),
        scratch_shapes=[pltpu.VMEM((H_HEADS, H_DIM, H_DIM), F32)],
        compiler_params=_cparams("parallel", "arbitrary"),
        name="hgrn2",
    )(fgate, proj, proj, proj, lb_row, hn_row)


def _even_relayout(a):
    nb = 5 * M_W + 2 * M_HEADS
    g0 = nb + N_W + 6 * N_KV_W
    col = lambda lo, n: lax.slice_in_dim(a, lo, lo + n, axis=-1)
    pad = jnp.zeros(a.shape[:-1] + (C_NZ - C_SMALL - SM_G - 3 * N_HEADS,), a.dtype)
    out = jnp.concatenate([
        col(0, 5 * M_W),
        col(nb, N_W + 6 * N_KV_W),
        col(5 * M_W, 2 * M_HEADS),
        col(g0, 3 * N_HEADS), pad,
        col(g0 + 3 * N_HEADS, N_W),
    ], axis=-1)
    assert out.shape[-1] == EV_COLS
    return out


def _rope_tables(positions):
    inv_freq = ROPE_THETA ** (-np.arange(ROPE_HALF, dtype=np.float64) / ROPE_HALF)
    ang = positions.astype(np.float64)[:, None] * inv_freq[None, :]
    cos, sin = np.cos(ang).astype(np.float32), np.sin(ang).astype(np.float32)
    n = positions.shape[0]
    z8 = np.zeros((n, ROPE_HALF), np.float32)
    rest1 = np.ones((n, N_DIM - ROPE_DIM), np.float32)
    rest0 = np.zeros((n, N_DIM - ROPE_DIM), np.float32)
    cs = np.concatenate([cos, cos, rest1], axis=1)
    s1 = np.concatenate([z8, sin, rest0], axis=1)
    s2 = np.concatenate([-sin, z8, rest0], axis=1)
    two = lambda a: np.concatenate([a, a], axis=1)
    return two(cs), two(s1), two(s2)


def _overlap_matrix(seq):
    ncmp = seq // CMP_STRIDE
    n_sel = seq // SEL_BLOCK
    cs = np.arange(ncmp) * CMP_STRIDE
    ss = np.arange(n_sel) * SEL_BLOCK
    ov = np.clip(np.minimum(cs[:, None] + CMP_LEN, ss[None, :] + SEL_BLOCK)
                 - np.maximum(cs[:, None], ss[None, :]), 0, None).astype(np.float32)
    ov[ncmp - 1, :] = 0.0
    return np.pad(ov, ((0, 0), (0, LANES - n_sel)))


def _even_layer(x2d, res, norm_w, w_in, b_in, f_bias, conv_w, conv_b, head_norm, q_norm, k_norm,
                ck_pos, ck_w1, ck_w2, cv_pos, cv_w1, cv_w2, w_out, batch, seq):
    assert seq // SEL_BLOCK <= LANES and seq % KV_TILE == 0
    outs = _norm_proj(x2d, norm_w, _even_relayout(w_in.astype(BF16)), _even_relayout(b_in),
                      (C_SMALL, C_SMALL + LANES), tn=EV_COLS // 3, res=res)
    if res:
        x2d = outs[0]
    proj, small = outs[-2:]

    f_bias_row = jnp.zeros((1, LANES), F32).at[0, SM_F:SM_F + M_HEADS].set(f_bias)
    ya = _mlstm(proj, small, conv_w, conv_b, f_bias_row, head_norm, batch, seq)

    cs, s1, s2 = (jnp.asarray(a) for a in _rope_tables(np.arange(seq)))
    qn_row = jnp.concatenate([q_norm, q_norm]).reshape(1, LANES)
    kn_rows = jnp.concatenate([k_norm, k_norm], axis=1)
    qt, ka, kwd, vst, vwt, kc, vc, gtt = _nsa_prep(proj, small, cs, s1, s2, qn_row, kn_rows, batch, seq)

    ncmp = seq // CMP_STRIDE
    ccs, cs1, cs2 = (jnp.asarray(a) for a in _rope_tables(np.arange(ncmp) * CMP_STRIDE + CMP_LEN - 1))
    kcd, vct = _nsa_compress(
        kc.reshape(batch * ncmp, CMP_STRIDE * LANES), vc.reshape(batch * ncmp, CMP_STRIDE * LANES),
        _compress_params(ck_pos, ck_w1, ck_w2), _compress_params(cv_pos, cv_w1, cv_w2),
        ccs, cs1, cs2, kn_rows, batch, ncmp)

    ovt = jnp.asarray(_overlap_matrix(seq).T).astype(BF16)
    yb = _nsa_attn(proj, qt, gtt, ka, vst, kwd, vwt, kcd, vct, ovt, batch, seq)

    w_o = w_out.astype(BF16)
    return x2d, [(ya, w_o[:M_W]), (yb, w_o[M_W:])]


def _odd_layer(x2d, res, norm_w, w_in, b_in, lb, head_norm, w_out, batch, seq):
    outs = _norm_proj(x2d, norm_w, w_in.astype(BF16), b_in, (0, H_W), tn=H_W, res=res)
    if res:
        x2d = outs[0]
    proj, fgate = outs[-2:]
    y = _hgrn(proj, fgate, lb.reshape(1, H_W), head_norm.reshape(1, H_W), batch, seq)
    return x2d, [(y, w_out.astype(BF16))]


def kernel(x, norm_w, ev_w_in, ev_b_in, mlstm_f_bias, mlstm_conv_w, mlstm_conv_b, mlstm_head_norm, nsa_q_norm, nsa_k_norm, cmp_k_pos, cmp_k_w1, cmp_k_w2, cmp_v_pos, cmp_v_w1, cmp_v_w2, ev_w_out, od_w_in, od_b_in, hgrn_lb_logits, hgrn_head_norm, od_w_out):
    batch, seq, d = x.shape
    depth = norm_w.shape[0]
    lbs = jnp.cumsum(jax.nn.softmax(hgrn_lb_logits.astype(F32), axis=0), axis=0)
    x2d = x.reshape(batch * seq, d)
    res = []
    for layer in range(depth):
        j = layer // 2
        if layer % 2 == 0:
            x2d, res = _even_layer(x2d, res, norm_w[layer], ev_w_in[j], ev_b_in[j], mlstm_f_bias[j],
                                   mlstm_conv_w[j], mlstm_conv_b[j], mlstm_head_norm[j], nsa_q_norm[j],
                                   nsa_k_norm[j], cmp_k_pos[j], cmp_k_w1[j], cmp_k_w2[j], cmp_v_pos[j],
                                   cmp_v_w1[j], cmp_v_w2[j], ev_w_out[j], batch, seq)
        else:
            x2d, res = _odd_layer(x2d, res, norm_w[layer], od_w_in[j], od_b_in[j], lbs[j], hgrn_head_norm[j],
                                  od_w_out[j], batch, seq)
    if len(res) == 2:
        x2d = _out_proj2(res[0][0], res[1][0], res[0][1], res[1][1], x2d)
    elif res:
        x2d = _out_proj1(res[0][0], res[0][1], x2d)
    return x2d.reshape(batch, seq, d)
```

```python
import functools

import numpy as np
import jax
import jax.numpy as jnp
from jax import lax
from jax.experimental import pallas as pl
from jax.experimental.pallas import tpu as pltpu

F32 = jnp.float32
BF16 = jnp.bfloat16

LANES = 128
EPS = 1e-6
NEG = -1e30
VMEM_LIMIT_BYTES = 56 * 1024 * 1024

M_HEADS = 4
M_DIM = 128
M_W = M_HEADS * M_DIM
M_CHUNK = 128
CONV_K = 4
N_HEADS = 8
N_GROUPS = 2
N_HPG = N_HEADS // N_GROUPS
N_DIM = 64
N_W = N_HEADS * N_DIM
N_KV_W = N_GROUPS * N_DIM
CMP_LEN = 32
CMP_STRIDE = 16
CMP_HIDDEN = 128
CMP_CHUNK = 128
SEL_BLOCK = 64
SEL_SHIFT = 6
SEL_TOPK = 16
WINDOW = 512
Q_BLOCK = 128
ROPE_THETA = 500000.0
ROPE_DIM = N_DIM // 4
ROPE_HALF = ROPE_DIM // 2
KV_TILE = 512
V_ROWS = 80
LOG2E = 1.4426950408889634
H_HEADS = 8
H_DIM = 128
H_W = H_HEADS * H_DIM
H_TILE = 128
H_LAG = 2

C_MQ, C_MK, C_MV, C_MO, C_MZ = 0, 512, 1024, 1536, 2048
C_NQ = 2560
C_KC, C_VC, C_KS, C_VS = 3072, 3200, 3328, 3456
C_KW, C_VW, C_SMALL = 3584, 3712, 3840
C_NZ = 4096
EV_COLS = 4608
SM_I, SM_F, SM_G = 0, 4, 8


def _cparams(*sem):
    return pltpu.CompilerParams(dimension_semantics=sem, vmem_limit_bytes=VMEM_LIMIT_BYTES)


def _dot(a, b):
    return jnp.dot(a, b, preferred_element_type=F32)


def _dot_nt(a, b):
    return lax.dot_general(a, b, (((1,), (1,)), ((), ())), preferred_element_type=F32)


def _dot_tn(a, b):
    return lax.dot_general(a, b, (((0,), (0,)), ((), ())), preferred_element_type=F32)


def _sigmoid(x):
    return 1.0 / (1.0 + jnp.exp(-x))


def _silu(x):
    return x * _sigmoid(x)


def _split3(x):
    hi = x.astype(BF16)
    r = x - hi.astype(F32)
    mid = r.astype(BF16)
    lo = (r - mid.astype(F32)).astype(BF16)
    return hi, mid, lo


def _dot01(m01, x):
    hi, mid, lo = _split3(x)
    return _dot(m01, hi) + _dot(m01, mid) + _dot(m01, lo)


def _norm_proj_kernel(*refs, n_res, g_tiles, g_off):
    x_ref = refs[0]
    y_refs, wy_refs = refs[1:1 + n_res], refs[1 + n_res:1 + 2 * n_res]
    nw_ref, w_ref, b_ref = refs[1 + 2 * n_res:4 + 2 * n_res]
    outs = refs[4 + 2 * n_res:]
    xo_ref = outs[0] if n_res else None
    o_ref, g_ref, h_ref = outs[-3:]
    j = pl.program_id(1)

    @pl.when(j == 0)
    def _():
        x = x_ref[...]
        for y_ref, wy_ref in zip(y_refs, wy_refs):
            x = x + _dot(y_ref[...], wy_ref[...])
        if n_res:
            xo_ref[...] = x
        ms = jnp.mean(x * x, axis=-1, keepdims=True)
        h_ref[...] = (x * lax.rsqrt(ms + EPS) * nw_ref[...]).astype(BF16)

    acc = _dot(h_ref[...], w_ref[...]) + b_ref[...]
    o_ref[...] = acc.astype(o_ref.dtype)

    @pl.when((j >= g_tiles[0]) & (j < g_tiles[1]))
    def _():
        g_ref[...] = acc[:, g_off:g_off + g_ref.shape[1]]


def _norm_proj(x2d, norm_w, w_bf16, bias, g_cols, tn, res=(), tm=1024):
    t, d = x2d.shape
    n = w_bf16.shape[1]
    g_w = min(g_cols[1] - g_cols[0], tn)
    g_tiles = (g_cols[0] // tn, -(-g_cols[1] // tn))
    row = lambda w: pl.BlockSpec((tm, w), lambda i, j: (i, 0))
    const = lambda a: pl.BlockSpec(a.shape, lambda i, j: (0, 0))
    ys, wys = [y for y, _ in res], [wy for _, wy in res]
    out_specs = [pl.BlockSpec((tm, tn), lambda i, j: (i, j)),
                 pl.BlockSpec((tm, g_w), lambda i, j: (i, jnp.clip(j - g_tiles[0], 0, g_tiles[1] - g_tiles[0] - 1)))]
    out_shape = [jax.ShapeDtypeStruct((t, n), BF16), jax.ShapeDtypeStruct((t, g_cols[1] - g_cols[0]), F32)]
    if res:
        out_specs = [row(d)] + out_specs
        out_shape = [jax.ShapeDtypeStruct((t, d), F32)] + out_shape
    return pl.pallas_call(
        functools.partial(_norm_proj_kernel, n_res=len(res), g_tiles=g_tiles, g_off=g_cols[0] % tn),
        grid=(t // tm, n // tn),
        in_specs=([row(d)] + [row(y.shape[1]) for y in ys] + [const(wy) for wy in wys]
                  + [pl.BlockSpec((1, d), lambda i, j: (0, 0)),
                     pl.BlockSpec((d, tn), lambda i, j: (0, j)),
                     pl.BlockSpec((1, tn), lambda i, j: (0, j))]),
        out_specs=out_specs,
        out_shape=out_shape,
        scratch_shapes=[pltpu.VMEM((tm, d), BF16)],
        compiler_params=_cparams("parallel", "arbitrary"),
        name="norm_proj",
    )(x2d, *ys, *wys, norm_w.reshape(1, d), w_bf16, bias.reshape(1, n))


def _out_proj2_kernel(ya_ref, yb_ref, wa_ref, wb_ref, x_ref, o_ref):
    o_ref[...] = x_ref[...] + _dot(ya_ref[...], wa_ref[...]) + _dot(yb_ref[...], wb_ref[...])


def _out_proj2(ya, yb, wa, wb, x2d, tm=1024):
    t, d = x2d.shape
    ka, kb = ya.shape[1], yb.shape[1]
    return pl.pallas_call(
        _out_proj2_kernel,
        grid=(t // tm,),
        in_specs=[
            pl.BlockSpec((tm, ka), lambda i: (i, 0)),
            pl.BlockSpec((tm, kb), lambda i: (i, 0)),
            pl.BlockSpec((ka, d), lambda i: (0, 0)),
            pl.BlockSpec((kb, d), lambda i: (0, 0)),
            pl.BlockSpec((tm, d), lambda i: (i, 0)),
        ],
        out_specs=pl.BlockSpec((tm, d), lambda i: (i, 0)),
        out_shape=jax.ShapeDtypeStruct((t, d), F32),
        compiler_params=_cparams("parallel"),
        name="out_proj2",
    )(ya, yb, wa, wb, x2d)


def _out_proj1_kernel(y_ref, w_ref, x_ref, o_ref):
    o_ref[...] = x_ref[...] + _dot(y_ref[...], w_ref[...])


def _out_proj1(y, w, x2d, tm=1024):
    t, d = x2d.shape
    k = y.shape[1]
    return pl.pallas_call(
        _out_proj1_kernel,
        grid=(t // tm,),
        in_specs=[
            pl.BlockSpec((tm, k), lambda i: (i, 0)),
            pl.BlockSpec((k, d), lambda i: (0, 0)),
            pl.BlockSpec((tm, d), lambda i: (i, 0)),
        ],
        out_specs=pl.BlockSpec((tm, d), lambda i: (i, 0)),
        out_shape=jax.ShapeDtypeStruct((t, d), F32),
        compiler_params=_cparams("parallel"),
        name="out_proj1",
    )(y, w, x2d)


def _mlstm_kernel(q_ref, k_ref, v_ref, o_ref, z_ref, sm_ref, cw_ref, cb_ref, fb_ref, hn_ref,
                  out_ref, xbuf, c_ref, n_ref, m_ref):
    L, D = M_CHUNK, M_DIM
    NB = q_ref.shape[0]
    XR = xbuf.shape[1]

    @pl.when(pl.program_id(0) == 0)
    def _():
        xbuf[...] = jnp.zeros_like(xbuf)
        c_ref[...] = jnp.zeros_like(c_ref)
        n_ref[...] = jnp.zeros_like(n_ref)
        m_ref[...] = jnp.zeros_like(m_ref)

    t_i = lax.broadcasted_iota(jnp.int32, (L, XR), 0)
    r_i = lax.broadcasted_iota(jnp.int32, (L, XR), 1)
    row = lax.broadcasted_iota(jnp.int32, (L, L), 0)
    col = lax.broadcasted_iota(jnp.int32, (L, L), 1)
    causal = col <= row
    lane = lax.broadcasted_iota(jnp.int32, (L, LANES), 1)
    is_f = (lane >= SM_F) & (lane < SM_F + M_HEADS)

    heads = []
    for b in range(NB):
        xbuf[b, 16:16 + L, 0:M_W] = q_ref[b]
        xbuf[b, 16:16 + L, M_W:2 * M_W] = k_ref[b]
        x2 = xbuf[b]
        acc = cb_ref[...] + cw_ref[CONV_K - 1:CONV_K, :] * x2[16:16 + L].astype(F32)
        for d in range(1, CONV_K):
            shift = jnp.where(r_i == t_i + 16 - d, 1.0, 0.0).astype(BF16)
            acc = acc + cw_ref[CONV_K - 1 - d:CONV_K - d, :] * _dot(shift, x2)
        qk = _silu(acc)
        xbuf[b, 0:16, :] = xbuf[b, L:L + 16, :]

        fpre = sm_ref[b] + fb_ref[...]
        gates = jnp.where(is_f, jnp.minimum(fpre, 0.0) - jnp.log(1.0 + jnp.exp(-jnp.abs(fpre))), fpre)
        bcols = _dot01(jnp.where(causal, 1.0, 0.0).astype(BF16), gates)
        gates_t = gates.T
        bcols_t = bcols.T
        for h in range(M_HEADS):
            hs = slice(h * D, (h + 1) * D)
            q = qk[:, hs]
            k = qk[:, M_W + h * D:M_W + (h + 1) * D] * (D ** -0.5)
            i_col = gates[:, SM_I + h:SM_I + h + 1]
            i_row = gates_t[SM_I + h:SM_I + h + 1, :]
            bcum_col = bcols[:, SM_F + h:SM_F + h + 1]
            bcum_row = bcols_t[SM_F + h:SM_F + h + 1, :]
            m_prev = m_ref[b * M_HEADS + h]
            dmat = jnp.where(causal, bcum_col - bcum_row + i_row, -jnp.inf)
            inter = bcum_col + m_prev
            m_row = jnp.maximum(jnp.max(dmat, axis=1, keepdims=True), inter)
            b_last = bcum_col[L - 1:L, :]
            dl = b_last - bcum_col + i_col
            m_new = jnp.maximum(b_last + m_prev, jnp.max(dl, axis=0, keepdims=True))
            wkk = jnp.exp(dl - m_new) * k
            heads.append(dict(
                b=b, hs=hs, st=b * M_HEADS + h, q=q, qb=q.astype(BF16), kb=k.astype(BF16),
                vb=v_ref[b, :, hs].astype(BF16), wkk=wkk,
                e=jnp.exp(dmat - m_row), w_inter=jnp.exp(inter - m_row), floor=jnp.exp(-m_row),
                decay=jnp.exp(b_last + m_prev - m_new), m_new=m_new))
    s_qk = [_dot_nt(t["qb"], t["kb"]) for t in heads]
    s_qc = [_dot(t["qb"], c_ref[t["st"]].astype(BF16)) for t in heads]
    s_kv = [_dot_tn(t["wkk"].astype(BF16), t["vb"]) for t in heads]
    smats = [s_qk[j] * t["e"] for j, t in enumerate(heads)]
    s_sv = [_dot(smats[j].astype(BF16), t["vb"]) for j, t in enumerate(heads)]
    for j, t in enumerate(heads):
        b, hs, st = t["b"], t["hs"], t["st"]
        num = s_sv[j] + t["w_inter"] * s_qc[j]
        den = (jnp.sum(smats[j], axis=1, keepdims=True)
               + t["w_inter"] * jnp.sum(t["q"] * n_ref[st], axis=1, keepdims=True))
        hh = num / jnp.maximum(jnp.abs(den), t["floor"])
        c_ref[st] = t["decay"] * c_ref[st] + s_kv[j]
        n_ref[st] = t["decay"] * n_ref[st] + jnp.sum(t["wkk"], axis=0, keepdims=True)
        m_ref[st] = t["m_new"]
        hn = hh * lax.rsqrt(jnp.mean(hh * hh, axis=1, keepdims=True) + EPS) * hn_ref[:, hs]
        gate = _sigmoid(o_ref[b, :, hs].astype(F32)) * _silu(z_ref[b, :, hs].astype(F32))
        out_ref[b, :, hs] = (hn * gate).astype(out_ref.dtype)


def _mlstm(proj, small, conv_w, conv_b, f_bias_row, head_norm, batch, seq):
    L = M_CHUNK
    nc = seq // L
    p3 = proj.reshape(batch, seq, proj.shape[1])
    s3 = small.reshape(batch, seq, LANES)
    cb = lambda off: pl.BlockSpec((batch, L, M_W), lambda c, off=off: (0, c, off // M_W))
    full = lambda r, w: pl.BlockSpec((r, w), lambda c: (0, 0))
    out = pl.pallas_call(
        _mlstm_kernel,
        grid=(nc,),
        in_specs=[
            cb(C_MQ), cb(C_MK), cb(C_MV), cb(C_MO), cb(C_MZ),
            pl.BlockSpec((batch, L, LANES), lambda c: (0, c, 0)),
            full(CONV_K, 2 * M_W), full(1, 2 * M_W), full(1, LANES), full(1, M_W),
        ],
        out_specs=pl.BlockSpec((batch, L, M_W), lambda c: (0, c, 0)),
        out_shape=jax.ShapeDtypeStruct((batch, seq, M_W), BF16),
        scratch_shapes=[
            pltpu.VMEM((batch, 2 * L, 2 * M_W), BF16),
            pltpu.VMEM((batch * M_HEADS, M_DIM, M_DIM), F32),
            pltpu.VMEM((batch * M_HEADS, 1, M_DIM), F32),
            pltpu.VMEM((batch * M_HEADS, 1, 1), F32),
        ],
        compiler_params=_cparams("arbitrary"),
        name="mlstm",
    )(p3, p3, p3, p3, p3, s3, conv_w, conv_b.reshape(1, -1), f_bias_row, head_norm.reshape(1, -1))
    return out.reshape(batch * seq, M_W)


def _group_rms(x, w_row):
    lane = lax.broadcasted_iota(jnp.int32, x.shape, 1)
    left = lane < N_DIM
    sq = x * x
    tot = jnp.sum(sq, axis=1, keepdims=True)
    lft = jnp.sum(jnp.where(left, sq, 0.0), axis=1, keepdims=True)
    ms = jnp.where(left, lft, tot - lft) * (1.0 / N_DIM)
    return x * lax.rsqrt(ms + EPS) * w_row


def _rope(x, cs, s1, s2):
    return x * cs + pltpu.roll(x, ROPE_HALF, 1) * s1 + pltpu.roll(x, LANES - ROPE_HALF, 1) * s2


def _dup_halves(x):
    lane = lax.broadcasted_iota(jnp.int32, x.shape, 1)
    sw = pltpu.roll(x, N_DIM, 1)
    left = lane < N_DIM
    return jnp.where(left, x, sw), jnp.where(left, sw, x)


def _nsa_prep_kernel(q_ref, kv_ref, kw_ref, sm_ref, cs_ref, s1_ref, s2_ref, qn_ref, kn_ref,
                     qt_ref, ka_ref, kwd_ref, vst_ref, vwt_ref, kc_ref, vc_ref, gtt_ref,
                     *, tp, seq):
    cs, s1, s2 = cs_ref[...], s1_ref[...], s2_ref[...]
    for j in range(N_W // LANES):
        sl = slice(j * LANES, (j + 1) * LANES)
        qn = _group_rms(q_ref[:, sl].astype(F32), qn_ref[...])
        qt_ref[sl, :] = (_rope(qn, cs, s1, s2) * (N_DIM ** -0.5 * LOG2E)).T.astype(BF16)
    kv = kv_ref[...].astype(F32)
    kc_ref[...] = kv[:, 0:LANES]
    vc_ref[...] = kv[:, LANES:2 * LANES]
    ks = _rope(_group_rms(kv[:, 2 * LANES:3 * LANES], kn_ref[1:2, :]), cs, s1, s2)
    k0, k1 = _dup_halves(ks)
    pos0 = (pl.program_id(0) * tp) % seq
    pos = pos0 + lax.broadcasted_iota(jnp.int32, (tp, LANES), 0)
    blk = lax.broadcasted_iota(jnp.int32, (tp, LANES), 1)
    onehot = jnp.where(jnp.right_shift(pos, SEL_SHIFT) == blk, 1.0, 0.0).astype(BF16)
    ka_ref[:, 0:LANES] = k0.astype(BF16)
    ka_ref[:, LANES:2 * LANES] = onehot
    ka_ref[:, 2 * LANES:3 * LANES] = k1.astype(BF16)
    ka_ref[:, 3 * LANES:4 * LANES] = onehot
    vst = kv[:, 3 * LANES:4 * LANES].T
    fill = (lax.broadcasted_iota(jnp.int32, (V_ROWS - N_DIM, Q_BLOCK), 0) == 0).astype(BF16)
    for c in range(tp // Q_BLOCK):
        for g in range(N_GROUPS):
            vst_ref[c, g * V_ROWS:g * V_ROWS + N_DIM, :] = (
                vst[g * N_DIM:(g + 1) * N_DIM, c * Q_BLOCK:(c + 1) * Q_BLOCK].astype(BF16))
            vst_ref[c, g * V_ROWS + N_DIM:(g + 1) * V_ROWS, :] = fill
    kwb = kw_ref[...].astype(F32)
    kw = _rope(_group_rms(kwb[:, 0:LANES], kn_ref[2:3, :]), cs, s1, s2)
    w0, w1 = _dup_halves(kw)
    kwd_ref[:, 0:LANES] = w0.astype(BF16)
    kwd_ref[:, LANES:2 * LANES] = w1.astype(BF16)
    vwt_ref[...] = kwb[:, LANES:2 * LANES].T.astype(BF16)
    gtt_ref[...] = _sigmoid(sm_ref[...]).T


def _nsa_prep(proj, small, cs, s1, s2, qn_row, kn_rows, batch, seq):
    tp = KV_TILE
    t = batch * seq
    nsb = seq // tp
    rb = lambda w, off: pl.BlockSpec((tp, w), lambda i, off=off, w=w: (i, off // w))
    tab = pl.BlockSpec((tp, LANES), lambda i: (i % nsb, 0))
    o = lambda w: pl.BlockSpec((tp, w), lambda i: (i, 0))
    ot = lambda r: pl.BlockSpec((r, tp), lambda i: (0, i))
    sds = lambda w, dt: jax.ShapeDtypeStruct((t, w), dt)
    return pl.pallas_call(
        functools.partial(_nsa_prep_kernel, tp=tp, seq=seq),
        grid=(t // tp,),
        in_specs=[rb(512, C_NQ), rb(512, C_KC), rb(512, C_KW), o(LANES), tab, tab, tab,
                  pl.BlockSpec((1, LANES), lambda i: (0, 0)), pl.BlockSpec((3, LANES), lambda i: (0, 0))],
        out_specs=[ot(N_W), o(512), o(256),
                   pl.BlockSpec((tp // Q_BLOCK, N_GROUPS * V_ROWS, Q_BLOCK), lambda i: (i, 0, 0)),
                   ot(LANES), o(128), o(128), ot(LANES)],
        out_shape=[jax.ShapeDtypeStruct((N_W, t), BF16), sds(512, BF16), sds(256, BF16),
                   jax.ShapeDtypeStruct((t // Q_BLOCK, N_GROUPS * V_ROWS, Q_BLOCK), BF16),
                   jax.ShapeDtypeStruct((LANES, t), BF16),
                   sds(128, F32), sds(128, F32), jax.ShapeDtypeStruct((LANES, t), F32)],
        compiler_params=_cparams("parallel"),
        name="nsa_prep",
    )(proj, proj, proj, small, cs, s1, s2, qn_row, kn_rows)


def _gelu_tanh(x):
    return 0.5 * x * (1.0 + jnp.tanh(0.7978845608028654 * (x + 0.044715 * x * x * x)))


def _compress_mlp(x_ref, pa_ref, pb_ref, wa_ref, wb_ref, w2_ref, hb_ref):
    n = x_ref.shape[0]
    x = x_ref[...]
    ha = _dot((x + pa_ref[...]).astype(BF16), wa_ref[...])
    hb_ref[0:n, :] = _dot((x + pb_ref[...]).astype(BF16), wb_ref[...])
    hb_ref[n:n + 8, :] = jnp.zeros((8, hb_ref.shape[1]), F32)
    hid = ha + hb_ref[1:n + 1, :]
    return _dot(_gelu_tanh(hid).astype(BF16), w2_ref[...])


def _nsa_compress_kernel(kx_ref, vx_ref, kpa_ref, kpb_ref, vpa_ref, vpb_ref, kwa_ref, kwb_ref, kw2_ref,
                         vwa_ref, vwb_ref, vw2_ref, cs_ref, s1_ref, s2_ref, kn_ref,
                         kcd_ref, vct_ref, hb_ref):
    n = kx_ref.shape[0]
    rowi = lax.broadcasted_iota(jnp.int32, (n, LANES), 0)
    live = rowi < n - 1
    kc = _compress_mlp(kx_ref, kpa_ref, kpb_ref, kwa_ref, kwb_ref, kw2_ref, hb_ref)
    kc = _rope(_group_rms(kc, kn_ref[0:1, :]), cs_ref[...], s1_ref[...], s2_ref[...])
    kc = jnp.where(live, kc, 0.0)
    c0, c1 = _dup_halves(kc)
    kcd_ref[:, 0:LANES] = c0.astype(BF16)
    kcd_ref[:, LANES:2 * LANES] = c1.astype(BF16)
    vc = _compress_mlp(vx_ref, vpa_ref, vpb_ref, vwa_ref, vwb_ref, vw2_ref, hb_ref)
    vc = jnp.where(live, vc, 0.0)
    vct_ref[0] = vc.T.astype(BF16)


def _nsa_compress(kx, vx, kp, vp, cs, s1, s2, kn_rows, batch, nrow):
    wide = CMP_STRIDE * LANES
    full = lambda a: pl.BlockSpec(a.shape, lambda b: (0,) * a.ndim)
    xs = pl.BlockSpec((nrow, wide), lambda b: (b, 0))
    args = (kx, vx, kp[0], kp[1], vp[0], vp[1], kp[2], kp[3], kp[4], vp[2], vp[3], vp[4], cs, s1, s2, kn_rows)
    return pl.pallas_call(
        _nsa_compress_kernel,
        grid=(batch,),
        in_specs=[xs, xs] + [full(a) for a in args[2:]],
        out_specs=[pl.BlockSpec((nrow, 2 * LANES), lambda b: (b, 0)),
                   pl.BlockSpec((1, LANES, nrow), lambda b: (b, 0, 0))],
        out_shape=[jax.ShapeDtypeStruct((batch * nrow, 2 * LANES), BF16),
                   jax.ShapeDtypeStruct((batch, LANES, nrow), BF16)],
        scratch_shapes=[pltpu.VMEM((nrow + 8, 2 * CMP_HIDDEN), F32)],
        compiler_params=_cparams("parallel"),
        name="nsa_compress",
    )(*args)


def _compress_params(pos, w1, w2):
    def expand(w1_half, pos_half):
        w = w1_half.reshape(CMP_STRIDE, N_DIM, CMP_HIDDEN)
        z = jnp.zeros_like(w)
        g0 = jnp.concatenate([w, z], axis=-1)
        g1 = jnp.concatenate([z, w], axis=-1)
        wx = jnp.stack([g0, g1], axis=1).reshape(CMP_STRIDE * LANES, 2 * CMP_HIDDEN)
        px = jnp.concatenate([pos_half, pos_half], axis=-1).reshape(1, CMP_STRIDE * LANES)
        return wx.astype(BF16), px.astype(F32)

    half = CMP_STRIDE * N_DIM
    wa, pa = expand(w1[:half], pos[:CMP_STRIDE])
    wb, pb = expand(w1[half:], pos[CMP_STRIDE:])
    z2 = jnp.zeros_like(w2)
    w2x = jnp.concatenate([jnp.concatenate([w2, z2], axis=1), jnp.concatenate([z2, w2], axis=1)], axis=0)
    return pa, pb, wa, wb, w2x.astype(BF16)


def _nsa_attn_kernel(qt_ref, gtt_ref, z_ref, ka_ref, vst_ref,
                     kw0, kw1, kw2, kw3, kw4, wt0, wt1, wt2, wt3, wt4,
                     kcd_ref, vct_ref, ovt_ref, out_ref, qa_ref, acc_ref, *, ncmp):
    QB = Q_BLOCK
    NP = N_HEADS // 2
    PW = 2 * QB
    i = pl.program_id(1)
    t0 = i * QB
    kw_refs = (kw0, kw1, kw2, kw3, kw4)
    vwt_refs = (wt0, wt1, wt2, wt3, wt4)
    nwt = len(kw_refs)

    sub = lax.broadcasted_iota(jnp.int32, (QB, QB), 0)
    lane = lax.broadcasted_iota(jnp.int32, (QB, QB), 1)
    top = sub < N_DIM
    t_row = t0 + lax.broadcasted_iota(jnp.int32, (1, QB), 1)
    t_row2 = jnp.concatenate([t_row, t_row], axis=1)

    def pair_q(pr):
        slab = qt_ref[pr * LANES:(pr + 1) * LANES, :]
        zero = jnp.zeros_like(slab)
        return jnp.concatenate([jnp.where(top, slab, zero), jnp.where(top, zero, slab)], axis=1)

    blk_f = sub.astype(F32)
    cur = jnp.right_shift(t_row, SEL_SHIFT)
    grp = lambda pr: pr // (N_HPG // 2)
    for pr in range(NP):
        qa_ref[pr, 0:LANES, :] = pair_q(pr)

    def compressed(n):
        nidx = lax.broadcasted_iota(jnp.int32, (n, PW), 0)
        ok = (nidx * CMP_STRIDE + (CMP_LEN - 1) <= t_row2) & (nidx < ncmp - 1)
        any_ok = t_row2 >= CMP_LEN - 1
        s_cmp = [_dot(kcd_ref[0:n, grp(pr) * LANES:(grp(pr) + 1) * LANES], qa_ref[pr, 0:LANES, :])
                 for pr in range(NP)]
        p_cmp = []
        for pr in range(NP):
            s = jnp.where(ok, s_cmp[pr], NEG)
            mx = jnp.max(s, axis=0, keepdims=True)
            p = jnp.exp2(s - mx)
            den = jnp.sum(p, axis=0, keepdims=True)
            p_cmp.append(p * jnp.where(any_ok, 1.0 / den, 0.0))
        outs = [_dot(vct_ref[0, grp(pr) * N_DIM:(grp(pr) + 1) * N_DIM, 0:n], p_cmp[pr].astype(BF16))
                for pr in range(NP)]
        for g in range(N_GROUPS):
            psum = jnp.zeros((n, QB), F32)
            for pp in range(N_HPG // 2):
                p = p_cmp[g * (N_HPG // 2) + pp]
                psum = psum + p[:, 0:QB] + p[:, QB:PW]
            p_hi = psum.astype(BF16)
            p_lo = (psum - p_hi.astype(F32)).astype(BF16)
            outs.append(_dot(ovt_ref[:, 0:n], p_hi) + _dot(ovt_ref[:, 0:n], p_lo))
        return tuple(outs)

    n_opts = list(range(CMP_CHUNK, ncmp + 1, CMP_CHUNK))
    need = (t0 + QB - CMP_LEN) // CMP_STRIDE + 1
    outs = lax.switch(jnp.clip((need - 1) // CMP_CHUNK, 0, len(n_opts) - 1),
                      [functools.partial(compressed, n) for n in n_opts])
    o_cmp, imps = outs[:NP], outs[NP:]
    forced = (sub == 0) | (sub == cur) | (sub == cur - 1)
    works = [jnp.where(sub > cur, NEG, jnp.where(forced, -jnp.inf, imp)) for imp in imps]
    sels = [forced for _ in range(N_GROUPS)]
    for _ in range(SEL_TOPK - 3):
        for g in range(N_GROUPS):
            mx = jnp.max(works[g], axis=0, keepdims=True)
            idx = jnp.min(jnp.where(works[g] == mx, blk_f, float(LANES)), axis=0, keepdims=True)
            pick = blk_f == idx
            sels[g] = sels[g] | pick
            works[g] = jnp.where(pick, -jnp.inf, works[g])
    for g in range(N_GROUPS):
        neg = jnp.where(sels[g] & (sub <= cur), 0.0, NEG).astype(BF16)
        neg2 = jnp.concatenate([neg, neg], axis=1)
        for pp in range(N_HPG // 2):
            qa_ref[g * (N_HPG // 2) + pp, LANES:2 * LANES, :] = neg2

    CH = QB
    above = jnp.concatenate([sub > lane, sub > lane], axis=1)

    def sel_scores(cj):
        k0 = pl.multiple_of(cj * CH, CH)
        return [_dot(ka_ref[pl.ds(k0, CH), 2 * grp(pr) * LANES:2 * (grp(pr) + 1) * LANES], qa_ref[pr])
                for pr in range(NP)]

    def sel_softmax(ss, ms, masked):
        new, alphas, ps = [], [], []
        for pr in range(NP):
            s = jnp.where(above, NEG, ss[pr]) if masked else ss[pr]
            m_new = jnp.maximum(ms[pr], jnp.max(s, axis=0, keepdims=True))
            alphas.append(jnp.exp2(ms[pr] - m_new))
            ps.append(jnp.exp2(s - m_new).astype(BF16))
            new.append(m_new)
        return tuple(new), alphas, ps

    def sel_values(cj, alphas, ps):
        for pr in range(NP):
            g = grp(pr)
            acc_ref[pr] = alphas[pr] * acc_ref[pr] + _dot(vst_ref[cj, g * V_ROWS:(g + 1) * V_ROWS, :], ps[pr])

    def sel_chunk(cj, ms, masked):
        ms, alphas, ps = sel_softmax(sel_scores(cj), ms, masked)
        sel_values(cj, alphas, ps)
        return ms

    def sel_tile(jt, ms, nc):
        c0 = jt * nc
        k0 = pl.multiple_of(c0 * CH, CH)
        big = [_dot(ka_ref[pl.ds(k0, nc * CH), 2 * grp(pr) * LANES:2 * (grp(pr) + 1) * LANES], qa_ref[pr])
               for pr in range(NP)]
        for c in range(nc):
            ss = [b[c * CH:(c + 1) * CH] for b in big]
            ms, alphas, ps = sel_softmax(ss, ms, False)
            sel_values(c0 + c, alphas, ps)
        return ms

    acc_ref[...] = jnp.zeros_like(acc_ref)
    ms = tuple(jnp.full((1, PW), 2.0 * NEG, F32) for _ in range(NP))
    done = 0
    for nc in (4 * KV_TILE // CH, 2 * KV_TILE // CH, KV_TILE // CH):
        n_run = (i - done) // nc
        ms = lax.fori_loop(done // nc, done // nc + n_run, functools.partial(sel_tile, nc=nc), ms)
        done = done + n_run * nc
    ms = lax.fori_loop(done, i, lambda cj, ms: sel_chunk(cj, ms, False), ms)
    s_diag = sel_scores(i)
    s_win = [[_dot(kw_refs[c][:, grp(pr) * LANES:(grp(pr) + 1) * LANES], qa_ref[pr, 0:LANES, :])
              for c in range(nwt)] for pr in range(NP)]
    ms, alphas, ps_diag = sel_softmax(s_diag, ms, True)
    p_win, d_win = [], []
    for pr in range(NP):
        ss = []
        for c in range(nwt):
            ok = (i - (nwt - 1) + c) >= 0
            if c == 0:
                ok = ok & above
            elif c == nwt - 1:
                ok = ok & jnp.logical_not(above)
            ss.append(jnp.where(ok, s_win[pr][c], NEG))
        mx = ss[0].max(axis=0, keepdims=True)
        for c in range(1, nwt):
            mx = jnp.maximum(mx, ss[c].max(axis=0, keepdims=True))
        ps = [jnp.exp2(ss[c] - mx) for c in range(nwt)]
        den = ps[0].sum(axis=0, keepdims=True)
        for c in range(1, nwt):
            den = den + ps[c].sum(axis=0, keepdims=True)
        p_win.append([p.astype(BF16) for p in ps])
        d_win.append(den)
    sel_values(i, alphas, ps_diag)
    o_win = []
    for pr in range(NP):
        g = grp(pr)
        acc = _dot(vwt_refs[0][g * N_DIM:(g + 1) * N_DIM, :], p_win[pr][0])
        for c in range(1, nwt):
            acc = acc + _dot(vwt_refs[c][g * N_DIM:(g + 1) * N_DIM, :], p_win[pr][c])
        o_win.append(acc * (1.0 / d_win[pr]))
    o_sel = [acc_ref[pr, 0:N_DIM, :] * (1.0 / acc_ref[pr, N_DIM:N_DIM + 1, :]) for pr in range(NP)]

    for pr in range(NP):
        halves = []
        for par in range(2):
            h = 2 * pr + par
            cs = slice(par * QB, (par + 1) * QB)
            gc = gtt_ref[SM_G + 3 * h:SM_G + 3 * h + 1, :]
            gs = gtt_ref[SM_G + 3 * h + 1:SM_G + 3 * h + 2, :]
            gw = gtt_ref[SM_G + 3 * h + 2:SM_G + 3 * h + 3, :]
            halves.append(gc * o_cmp[pr][:, cs] + gs * o_sel[pr][:, cs] + gw * o_win[pr][:, cs])
        y = jnp.concatenate(halves, axis=0).T
        sl = slice(pr * LANES, (pr + 1) * LANES)
        out_ref[:, sl] = (y * _silu(z_ref[:, sl].astype(F32))).astype(out_ref.dtype)


def _nsa_attn(proj, qt, gtt, ka, vst, kwd, vwt, kcd, vct, ovt, batch, seq):
    QB = Q_BLOCK
    nqb = seq // QB
    ncmp = seq // CMP_STRIDE
    ntile = seq // KV_TILE
    nwt = WINDOW // QB + 1
    row = lambda w: pl.BlockSpec((QB, w), lambda b, i: (b * nqb + i, 0))
    colb = lambda r: pl.BlockSpec((r, QB), lambda b, i: (0, b * nqb + i))
    wblk = lambda b, i, c: b * nqb + jnp.maximum(i - (nwt - 1) + c, 0)
    in_specs = ([colb(N_W), colb(LANES), pl.BlockSpec((QB, N_W), lambda b, i: (b * nqb + i, C_NZ // N_W)),
                 pl.BlockSpec((seq, 4 * LANES), lambda b, i: (b, 0)),
                 pl.BlockSpec((nqb, N_GROUPS * V_ROWS, QB), lambda b, i: (b, 0, 0))]
                + [pl.BlockSpec((QB, 2 * LANES), lambda b, i, c=c: (wblk(b, i, c), 0)) for c in range(nwt)]
                + [pl.BlockSpec((LANES, QB), lambda b, i, c=c: (0, wblk(b, i, c))) for c in range(nwt)]
                + [pl.BlockSpec((ncmp, 2 * LANES), lambda b, i: (b, 0)),
                   pl.BlockSpec((1, LANES, ncmp), lambda b, i: (b, 0, 0)),
                   pl.BlockSpec((LANES, ncmp), lambda b, i: (0, 0))])
    return pl.pallas_call(
        functools.partial(_nsa_attn_kernel, ncmp=ncmp),
        grid=(batch, nqb),
        in_specs=in_specs,
        out_specs=row(N_W),
        out_shape=jax.ShapeDtypeStruct((batch * seq, N_W), BF16),
        scratch_shapes=[pltpu.VMEM((N_HEADS // 2, 2 * LANES, 2 * QB), BF16),
                        pltpu.VMEM((N_HEADS // 2, V_ROWS, 2 * QB), F32)],
        compiler_params=_cparams("parallel", "arbitrary"),
        name="nsa_attn",
    )(qt, gtt, proj, ka, vst, *([kwd] * nwt), *([vwt] * nwt), kcd, vct, ovt)


def _hgrn_kernel(f_ref, i_ref, q_ref, z_ref, lb_ref, hn_ref, out_ref, st_ref):
    T, D = H_TILE, H_DIM
    W = H_W

    @pl.when(pl.program_id(1) == 0)
    def _():
        st_ref[...] = jnp.zeros_like(st_ref)

    lb = lb_ref[...]
    fp = f_ref[...]
    e = jnp.exp(-jnp.abs(fp))
    r = 1.0 / (1.0 + e)
    sig = jnp.where(fp >= 0, r, e * r)
    nsig = jnp.where(fp >= 0, e * r, r)
    logf2 = jnp.log(lb + (1.0 - lb) * sig) * LOG2E
    kk = (1.0 - lb) * nsig

    row = lax.broadcasted_iota(jnp.int32, (T, T), 0)
    col = lax.broadcasted_iota(jnp.int32, (T, T), 1)
    tri01 = jnp.where(col <= row, 1.0, 0.0).astype(BF16)
    bcum = _dot01(tri01, logf2)
    rowi = lax.broadcasted_iota(jnp.int32, (T, 1), 0)

    segs = []
    sg = 2 * H_LAG
    while sg <= T:
        if sg == T:
            bmid = jnp.broadcast_to(bcum[sg // 2 - 1:sg // 2, :], (T, W))
        else:
            b3 = bcum.reshape(T // sg, sg, W)
            bmid = jnp.broadcast_to(b3[:, sg // 2 - 1:sg // 2, :], (T // sg, sg, W)).reshape(T, W)
        x = bcum - bmid
        shift = sg.bit_length() - 1
        pair_ok = ((jnp.right_shift(row, shift) == jnp.right_shift(col, shift))
                   & (jnp.bitwise_and(row, sg - 1) >= sg // 2) & (jnp.bitwise_and(col, sg - 1) < sg // 2))
        segs.append((jnp.exp2(jnp.minimum(x, -x)), pair_ok))
        sg *= 2

    sub3 = lax.broadcasted_iota(jnp.int32, (T // 8, 8, 1), 1)
    lag_ok = [jnp.bitwise_and(sub3, H_LAG - 1) >= d for d in range(H_LAG)]
    b_end = bcum[T - 1:T, :]
    v_all = i_ref[...].astype(F32)
    for h in range(H_HEADS):
        hs = slice(h * D, (h + 1) * D)
        q = _silu(q_ref[:, hs].astype(F32))
        k = kk[:, hs]
        b = bcum[:, hs]
        v = v_all[:, hs]
        vb = v.astype(BF16)
        a_seg = []
        for fac, _ in segs:
            f = fac[:, hs]
            a_seg.append(_dot_nt((q * f).astype(BF16), (k * f).astype(BF16)))
        st = st_ref[h]
        o_state = _dot_nt((q * jnp.exp2(b)).astype(BF16), st.astype(BF16))
        kt = (k * jnp.exp2(b_end[:, hs] - b)).astype(BF16)
        st_ref[h] = st * jnp.exp2(b_end[:, hs]) + _dot_tn(vb, kt)
        q3, k3, b3, v3 = (a.reshape(T // 8, 8, D) for a in (q, k, b, v))
        o3 = jnp.sum(q3 * k3, axis=2, keepdims=True) * v3
        for d in range(1, H_LAG):
            ks, bs, vs = (pltpu.roll(a, d, 1) for a in (k3, b3, v3))
            a = jnp.sum(q3 * ks * jnp.exp2(b3 - bs), axis=2, keepdims=True)
            o3 = o3 + jnp.where(lag_ok[d], a, 0.0) * vs
        a_tot = jnp.zeros((T, T), F32)
        for (_, pair_ok), a in zip(segs, a_seg):
            a_tot = jnp.where(pair_ok, a, a_tot)
        oh = o3.reshape(T, D) + o_state + _dot(a_tot.astype(BF16), vb)
        on = oh * lax.rsqrt(jnp.mean(oh * oh, axis=1, keepdims=True) + EPS) * hn_ref[:, hs]
        out_ref[:, hs] = (on * _silu(z_ref[:, hs].astype(F32))).astype(out_ref.dtype)


def _hgrn(proj, fgate, lb_row, hn_row, batch, seq):
    T = H_TILE
    nt = seq // T
    cb = lambda j: pl.BlockSpec((T, H_W), lambda b, c, j=j: (b * nt + c, j))
    vec = pl.BlockSpec((1, H_W), lambda b, c: (0, 0))
    return pl.pallas_call(
        _hgrn_kernel,
        grid=(batch, nt),
        in_specs=[pl.BlockSpec((T, H_W), lambda b, c: (b * nt + c, 0)), cb(1), cb(2), cb(3), vec, vec],
        out_specs=pl.BlockSpec((T, H_W), lambda b, c: (b * nt + c, 0)),
        out_shape=jax.ShapeDtypeStruct((batch * seq, H_W), BF16),
        scratch_shapes=[pltpu.VMEM((H_HEADS, H_DIM, H_DIM), F32)],
        compiler_params=_cparams("parallel", "arbitrary"),
        name="hgrn2",
    )(fgate, proj, proj, proj, lb_row, hn_row)


def _even_relayout(a):
    nb = 5 * M_W + 2 * M_HEADS
    g0 = nb + N_W + 6 * N_KV_W
    col = lambda lo, n: lax.slice_in_dim(a, lo, lo + n, axis=-1)
    pad = jnp.zeros(a.shape[:-1] + (C_NZ - C_SMALL - SM_G - 3 * N_HEADS,), a.dtype)
    out = jnp.concatenate([
        col(0, 5 * M_W),
        col(nb, N_W + 6 * N_KV_W),
        col(5 * M_W, 2 * M_HEADS),
        col(g0, 3 * N_HEADS), pad,
        col(g0 + 3 * N_HEADS, N_W),
    ], axis=-1)
    assert out.shape[-1] == EV_COLS
    return out


def _rope_tables(positions):
    inv_freq = ROPE_THETA ** (-np.arange(ROPE_HALF, dtype=np.float64) / ROPE_HALF)
    ang = positions.astype(np.float64)[:, None] * inv_freq[None, :]
    cos, sin = np.cos(ang).astype(np.float32), np.sin(ang).astype(np.float32)
    n = positions.shape[0]
    z8 = np.zeros((n, ROPE_HALF), np.float32)
    rest1 = np.ones((n, N_DIM - ROPE_DIM), np.float32)
    rest0 = np.zeros((n, N_DIM - ROPE_DIM), np.float32)
    cs = np.concatenate([cos, cos, rest1], axis=1)
    s1 = np.concatenate([z8, sin, rest0], axis=1)
    s2 = np.concatenate([-sin, z8, rest0], axis=1)
    two = lambda a: np.concatenate([a, a], axis=1)
    return two(cs), two(s1), two(s2)


def _overlap_matrix(seq):
    ncmp = seq // CMP_STRIDE
    n_sel = seq // SEL_BLOCK
    cs = np.arange(ncmp) * CMP_STRIDE
    ss = np.arange(n_sel) * SEL_BLOCK
    ov = np.clip(np.minimum(cs[:, None] + CMP_LEN, ss[None, :] + SEL_BLOCK)
                 - np.maximum(cs[:, None], ss[None, :]), 0, None).astype(np.float32)
    ov[ncmp - 1, :] = 0.0
    return np.pad(ov, ((0, 0), (0, LANES - n_sel)))


def _even_layer(x2d, res, norm_w, w_in, b_in, f_bias, conv_w, conv_b, head_norm, q_norm, k_norm,
                ck_pos, ck_w1, ck_w2, cv_pos, cv_w1, cv_w2, w_out, batch, seq):
    assert seq // SEL_BLOCK <= LANES and seq % KV_TILE == 0
    outs = _norm_proj(x2d, norm_w, _even_relayout(w_in.astype(BF16)), _even_relayout(b_in),
                      (C_SMALL, C_SMALL + LANES), tn=EV_COLS // 3, res=res)
    if res:
        x2d = outs[0]
    proj, small = outs[-2:]

    f_bias_row = jnp.zeros((1, LANES), F32).at[0, SM_F:SM_F + M_HEADS].set(f_bias)
    ya = _mlstm(proj, small, conv_w, conv_b, f_bias_row, head_norm, batch, seq)

    cs, s1, s2 = (jnp.asarray(a) for a in _rope_tables(np.arange(seq)))
    qn_row = jnp.concatenate([q_norm, q_norm]).reshape(1, LANES)
    kn_rows = jnp.concatenate([k_norm, k_norm], axis=1)
    qt, ka, kwd, vst, vwt, kc, vc, gtt = _nsa_prep(proj, small, cs, s1, s2, qn_row, kn_rows, batch, seq)

    ncmp = seq // CMP_STRIDE
    ccs, cs1, cs2 = (jnp.asarray(a) for a in _rope_tables(np.arange(ncmp) * CMP_STRIDE + CMP_LEN - 1))
    kcd, vct = _nsa_compress(
        kc.reshape(batch * ncmp, CMP_STRIDE * LANES), vc.reshape(batch * ncmp, CMP_STRIDE * LANES),
        _compress_params(ck_pos, ck_w1, ck_w2), _compress_params(cv_pos, cv_w1, cv_w2),
        ccs, cs1, cs2, kn_rows, batch, ncmp)

    ovt = jnp.asarray(_overlap_matrix(seq).T).astype(BF16)
    yb = _nsa_attn(proj, qt, gtt, ka, vst, kwd, vwt, kcd, vct, ovt, batch, seq)

    w_o = w_out.astype(BF16)
    return x2d, [(ya, w_o[:M_W]), (yb, w_o[M_W:])]


def _odd_layer(x2d, res, norm_w, w_in, b_in, lb, head_norm, w_out, batch, seq):
    outs = _norm_proj(x2d, norm_w, w_in.astype(BF16), b_in, (0, H_W), tn=H_W, res=res)
    if res:
        x2d = outs[0]
    proj, fgate = outs[-2:]
    y = _hgrn(proj, fgate, lb.reshape(1, H_W), head_norm.reshape(1, H_W), batch, seq)
    return x2d, [(y, w_out.astype(BF16))]


def kernel(x, norm_w, ev_w_in, ev_b_in, mlstm_f_bias, mlstm_conv_w, mlstm_conv_b, mlstm_head_norm, nsa_q_norm, nsa_k_norm, cmp_k_pos, cmp_k_w1, cmp_k_w2, cmp_v_pos, cmp_v_w1, cmp_v_w2, ev_w_out, od_w_in, od_b_in, hgrn_lb_logits, hgrn_head_norm, od_w_out):
    batch, seq, d = x.shape
    depth = norm_w.shape[0]
    lbs = jnp.cumsum(jax.nn.softmax(hgrn_lb_logits.astype(F32), axis=0), axis=0)
    x2d = x.reshape(batch * seq, d)
    res = []
    for layer in range(depth):
        j = layer // 2
        if layer % 2 == 0:
            x2d, res = _even_layer(x2d, res, norm_w[layer], ev_w_in[j], ev_b_in[j], mlstm_f_bias[j],
                                   mlstm_conv_w[j], mlstm_conv_b[j], mlstm_head_norm[j], nsa_q_norm[j],
                                   nsa_k_norm[j], cmp_k_pos[j], cmp_k_w1[j], cmp_k_w2[j], cmp_v_pos[j],
                                   cmp_v_w1[j], cmp_v_w2[j], ev_w_out[j], batch, seq)
        else:
            x2d, res = _odd_layer(x2d, res, norm_w[layer], od_w_in[j], od_b_in[j], lbs[j], hgrn_head_norm[j],
                                  od_w_out[j], batch, seq)
    if len(res) == 2:
        x2d = _out_proj2(res[0][0], res[1][0], res[0][1], res[1][1], x2d)
    elif res:
        x2d = _out_proj1(res[0][0], res[0][1], x2d)
    return x2d.reshape(batch, seq, d)
```

```python
import functools

import numpy as np
import jax
import jax.numpy as jnp
from jax import lax
from jax.experimental import pallas as pl
from jax.experimental.pallas import tpu as pltpu

F32 = jnp.float32
BF16 = jnp.bfloat16

LANES = 128
SUBLANES = 8
BF16_SUBLANES = 16
EPS = 1e-6
NEG = -1e30
VMEM_LIMIT_BYTES = 56 * 1024 * 1024

M_HEADS = 4
M_DIM = 128
M_W = M_HEADS * M_DIM
M_CHUNK = 128
CONV_K = 4
N_HEADS = 8
N_GROUPS = 2
N_HPG = N_HEADS // N_GROUPS
N_DIM = 64
N_W = N_HEADS * N_DIM
N_KV_W = N_GROUPS * N_DIM
CMP_LEN = 32
CMP_STRIDE = 16
CMP_HIDDEN = 128
CMP_CHUNK = 128
SEL_BLOCK = 64
SEL_SHIFT = 6
SEL_TOPK = 16
WINDOW = 512
Q_BLOCK = 128
ROPE_THETA = 500000.0
ROPE_DIM = N_DIM // 4
ROPE_HALF = ROPE_DIM // 2
KV_TILE = 512
V_ROWS = 80
LOG2E = 1.4426950408889634
H_HEADS = 8
H_DIM = 128
H_W = H_HEADS * H_DIM
H_TILE = 128
H_LAG = 2

C_MQ, C_MK, C_MV, C_MO, C_MZ = 0, 512, 1024, 1536, 2048
C_NQ = 2560
C_KC, C_VC, C_KS, C_VS = 3072, 3200, 3328, 3456
C_KW, C_VW, C_SMALL = 3584, 3712, 3840
C_NZ = 4096
EV_COLS = 4608
SM_I, SM_F, SM_G = 0, 4, 8


def _cparams(*sem):
    return pltpu.CompilerParams(dimension_semantics=sem, vmem_limit_bytes=VMEM_LIMIT_BYTES)


def _dot(a, b):
    return jnp.dot(a, b, preferred_element_type=F32)


def _dot_nt(a, b):
    return lax.dot_general(a, b, (((1,), (1,)), ((), ())), preferred_element_type=F32)


def _dot_tn(a, b):
    return lax.dot_general(a, b, (((0,), (0,)), ((), ())), preferred_element_type=F32)


def _sigmoid(x):
    return 1.0 / (1.0 + jnp.exp(-x))


def _silu(x):
    return x * _sigmoid(x)


def _split3(x):
    hi = x.astype(BF16)
    r = x - hi.astype(F32)
    mid = r.astype(BF16)
    lo = (r - mid.astype(F32)).astype(BF16)
    return hi, mid, lo


def _dot01(m01, x):
    hi, mid, lo = _split3(x)
    return _dot(m01, hi) + _dot(m01, mid) + _dot(m01, lo)


def _norm_proj_kernel(*refs, n_res, g_tiles, g_off):
    x_ref = refs[0]
    y_refs, wy_refs = refs[1:1 + n_res], refs[1 + n_res:1 + 2 * n_res]
    nw_ref, w_ref, b_ref = refs[1 + 2 * n_res:4 + 2 * n_res]
    outs = refs[4 + 2 * n_res:]
    xo_ref = outs[0] if n_res else None
    o_ref, g_ref, h_ref = outs[-3:]
    j = pl.program_id(1)

    @pl.when(j == 0)
    def _():
        x = x_ref[...]
        for y_ref, wy_ref in zip(y_refs, wy_refs):
            x = x + _dot(y_ref[...], wy_ref[...])
        if n_res:
            xo_ref[...] = x
        ms = jnp.mean(x * x, axis=-1, keepdims=True)
        h_ref[...] = (x * lax.rsqrt(ms + EPS) * nw_ref[...]).astype(BF16)

    acc = _dot(h_ref[...], w_ref[...]) + b_ref[...]
    o_ref[...] = acc.astype(o_ref.dtype)

    @pl.when((j >= g_tiles[0]) & (j < g_tiles[1]))
    def _():
        g_ref[...] = acc[:, g_off:g_off + g_ref.shape[1]]


def _norm_proj(x2d, norm_w, w_bf16, bias, g_cols, tn, res=(), tm=1024):
    t, d = x2d.shape
    n = w_bf16.shape[1]
    g_w = min(g_cols[1] - g_cols[0], tn)
    g_tiles = (g_cols[0] // tn, -(-g_cols[1] // tn))
    row = lambda w: pl.BlockSpec((tm, w), lambda i, j: (i, 0))
    const = lambda a: pl.BlockSpec(a.shape, lambda i, j: (0, 0))
    ys, wys = [y for y, _ in res], [wy for _, wy in res]
    out_specs = [pl.BlockSpec((tm, tn), lambda i, j: (i, j)),
                 pl.BlockSpec((tm, g_w), lambda i, j: (i, jnp.clip(j - g_tiles[0], 0, g_tiles[1] - g_tiles[0] - 1)))]
    out_shape = [jax.ShapeDtypeStruct((t, n), BF16), jax.ShapeDtypeStruct((t, g_cols[1] - g_cols[0]), F32)]
    if res:
        out_specs = [row(d)] + out_specs
        out_shape = [jax.ShapeDtypeStruct((t, d), F32)] + out_shape
    return pl.pallas_call(
        functools.partial(_norm_proj_kernel, n_res=len(res), g_tiles=g_tiles, g_off=g_cols[0] % tn),
        grid=(t // tm, n // tn),
        in_specs=([row(d)] + [row(y.shape[1]) for y in ys] + [const(wy) for wy in wys]
                  + [pl.BlockSpec((1, d), lambda i, j: (0, 0)),
                     pl.BlockSpec((d, tn), lambda i, j: (0, j)),
                     pl.BlockSpec((1, tn), lambda i, j: (0, j))]),
        out_specs=out_specs,
        out_shape=out_shape,
        scratch_shapes=[pltpu.VMEM((tm, d), BF16)],
        compiler_params=_cparams("parallel", "arbitrary"),
        name="norm_proj",
    )(x2d, *ys, *wys, norm_w.reshape(1, d), w_bf16, bias.reshape(1, n))


def _out_proj2_kernel(ya_ref, yb_ref, wa_ref, wb_ref, x_ref, o_ref):
    o_ref[...] = x_ref[...] + _dot(ya_ref[...], wa_ref[...]) + _dot(yb_ref[...], wb_ref[...])


def _out_proj2(ya, yb, wa, wb, x2d, tm=1024):
    t, d = x2d.shape
    ka, kb = ya.shape[1], yb.shape[1]
    return pl.pallas_call(
        _out_proj2_kernel,
        grid=(t // tm,),
        in_specs=[
            pl.BlockSpec((tm, ka), lambda i: (i, 0)),
            pl.BlockSpec((tm, kb), lambda i: (i, 0)),
            pl.BlockSpec((ka, d), lambda i: (0, 0)),
            pl.BlockSpec((kb, d), lambda i: (0, 0)),
            pl.BlockSpec((tm, d), lambda i: (i, 0)),
        ],
        out_specs=pl.BlockSpec((tm, d), lambda i: (i, 0)),
        out_shape=jax.ShapeDtypeStruct((t, d), F32),
        compiler_params=_cparams("parallel"),
        name="out_proj2",
    )(ya, yb, wa, wb, x2d)


def _out_proj1_kernel(y_ref, w_ref, x_ref, o_ref):
    o_ref[...] = x_ref[...] + _dot(y_ref[...], w_ref[...])


def _out_proj1(y, w, x2d, tm=1024):
    t, d = x2d.shape
    k = y.shape[1]
    return pl.pallas_call(
        _out_proj1_kernel,
        grid=(t // tm,),
        in_specs=[
            pl.BlockSpec((tm, k), lambda i: (i, 0)),
            pl.BlockSpec((k, d), lambda i: (0, 0)),
            pl.BlockSpec((tm, d), lambda i: (i, 0)),
        ],
        out_specs=pl.BlockSpec((tm, d), lambda i: (i, 0)),
        out_shape=jax.ShapeDtypeStruct((t, d), F32),
        compiler_params=_cparams("parallel"),
        name="out_proj1",
    )(y, w, x2d)


def _mlstm_kernel(q_ref, k_ref, v_ref, o_ref, z_ref, sm_ref, cw_ref, cb_ref, fb_ref, hn_ref,
                  out_ref, xbuf, c_ref, n_ref, m_ref):
    L, D = M_CHUNK, M_DIM
    NB = q_ref.shape[0]
    XR = xbuf.shape[1]
    X0 = BF16_SUBLANES

    @pl.when(pl.program_id(0) == 0)
    def _():
        xbuf[...] = jnp.zeros_like(xbuf)
        c_ref[...] = jnp.zeros_like(c_ref)
        n_ref[...] = jnp.zeros_like(n_ref)
        m_ref[...] = jnp.zeros_like(m_ref)

    t_i = lax.broadcasted_iota(jnp.int32, (L, XR), 0)
    r_i = lax.broadcasted_iota(jnp.int32, (L, XR), 1)
    row = lax.broadcasted_iota(jnp.int32, (L, L), 0)
    col = lax.broadcasted_iota(jnp.int32, (L, L), 1)
    causal = col <= row
    lane = lax.broadcasted_iota(jnp.int32, (L, LANES), 1)
    is_f = (lane >= SM_F) & (lane < SM_F + M_HEADS)

    heads = []
    for b in range(NB):
        xbuf[b, X0:X0 + L, 0:M_W] = q_ref[b]
        xbuf[b, X0:X0 + L, M_W:2 * M_W] = k_ref[b]
        x2 = xbuf[b]
        acc = cb_ref[...] + cw_ref[CONV_K - 1:CONV_K, :] * x2[X0:X0 + L].astype(F32)
        for d in range(1, CONV_K):
            shift = jnp.where(r_i == t_i + X0 - d, 1.0, 0.0).astype(BF16)
            acc = acc + cw_ref[CONV_K - 1 - d:CONV_K - d, :] * _dot(shift, x2)
        qk = _silu(acc)
        xbuf[b, 0:X0, :] = xbuf[b, L:L + X0, :]

        fpre = sm_ref[b] + fb_ref[...]
        gates = jnp.where(is_f, jnp.minimum(fpre, 0.0) - jnp.log(1.0 + jnp.exp(-jnp.abs(fpre))), fpre)
        bcols = _dot01(jnp.where(causal, 1.0, 0.0).astype(BF16), gates)
        gates_t = gates.T
        bcols_t = bcols.T
        for h in range(M_HEADS):
            hs = slice(h * D, (h + 1) * D)
            q = qk[:, hs]
            k = qk[:, M_W + h * D:M_W + (h + 1) * D] * (D ** -0.5)
            i_col = gates[:, SM_I + h:SM_I + h + 1]
            i_row = gates_t[SM_I + h:SM_I + h + 1, :]
            bcum_col = bcols[:, SM_F + h:SM_F + h + 1]
            bcum_row = bcols_t[SM_F + h:SM_F + h + 1, :]
            m_prev = m_ref[b * M_HEADS + h]
            dmat = jnp.where(causal, bcum_col - bcum_row + i_row, -jnp.inf)
            inter = bcum_col + m_prev
            m_row = jnp.maximum(jnp.max(dmat, axis=1, keepdims=True), inter)
            b_last = bcum_col[L - 1:L, :]
            dl = b_last - bcum_col + i_col
            m_new = jnp.maximum(b_last + m_prev, jnp.max(dl, axis=0, keepdims=True))
            wkk = jnp.exp(dl - m_new) * k
            heads.append(dict(
                b=b, hs=hs, st=b * M_HEADS + h, q=q, qb=q.astype(BF16), kb=k.astype(BF16),
                vb=v_ref[b, :, hs].astype(BF16), wkk=wkk,
                e=jnp.exp(dmat - m_row), w_inter=jnp.exp(inter - m_row), floor=jnp.exp(-m_row),
                decay=jnp.exp(b_last + m_prev - m_new), m_new=m_new))
    s_qk = [_dot_nt(t["qb"], t["kb"]) for t in heads]
    s_qc = [_dot(t["qb"], c_ref[t["st"]].astype(BF16)) for t in heads]
    s_kv = [_dot_tn(t["wkk"].astype(BF16), t["vb"]) for t in heads]
    smats = [s_qk[j] * t["e"] for j, t in enumerate(heads)]
    s_sv = [_dot(smats[j].astype(BF16), t["vb"]) for j, t in enumerate(heads)]
    for j, t in enumerate(heads):
        b, hs, st = t["b"], t["hs"], t["st"]
        num = s_sv[j] + t["w_inter"] * s_qc[j]
        den = (jnp.sum(smats[j], axis=1, keepdims=True)
               + t["w_inter"] * jnp.sum(t["q"] * n_ref[st], axis=1, keepdims=True))
        hh = num / jnp.maximum(jnp.abs(den), t["floor"])
        c_ref[st] = t["decay"] * c_ref[st] + s_kv[j]
        n_ref[st] = t["decay"] * n_ref[st] + jnp.sum(t["wkk"], axis=0, keepdims=True)
        m_ref[st] = t["m_new"]
        hn = hh * lax.rsqrt(jnp.mean(hh * hh, axis=1, keepdims=True) + EPS) * hn_ref[:, hs]
        gate = _sigmoid(o_ref[b, :, hs].astype(F32)) * _silu(z_ref[b, :, hs].astype(F32))
        out_ref[b, :, hs] = (hn * gate).astype(out_ref.dtype)


def _mlstm(proj, small, conv_w, conv_b, f_bias_row, head_norm, batch, seq):
    L = M_CHUNK
    nc = seq // L
    p3 = proj.reshape(batch, seq, proj.shape[1])
    s3 = small.reshape(batch, seq, LANES)
    cb = lambda off: pl.BlockSpec((batch, L, M_W), lambda c, off=off: (0, c, off // M_W))
    full = lambda r, w: pl.BlockSpec((r, w), lambda c: (0, 0))
    out = pl.pallas_call(
        _mlstm_kernel,
        grid=(nc,),
        in_specs=[
            cb(C_MQ), cb(C_MK), cb(C_MV), cb(C_MO), cb(C_MZ),
            pl.BlockSpec((batch, L, LANES), lambda c: (0, c, 0)),
            full(CONV_K, 2 * M_W), full(1, 2 * M_W), full(1, LANES), full(1, M_W),
        ],
        out_specs=pl.BlockSpec((batch, L, M_W), lambda c: (0, c, 0)),
        out_shape=jax.ShapeDtypeStruct((batch, seq, M_W), BF16),
        scratch_shapes=[
            pltpu.VMEM((batch, 2 * L, 2 * M_W), BF16),
            pltpu.VMEM((batch * M_HEADS, M_DIM, M_DIM), F32),
            pltpu.VMEM((batch * M_HEADS, 1, M_DIM), F32),
            pltpu.VMEM((batch * M_HEADS, 1, 1), F32),
        ],
        compiler_params=_cparams("arbitrary"),
        name="mlstm",
    )(p3, p3, p3, p3, p3, s3, conv_w, conv_b.reshape(1, -1), f_bias_row, head_norm.reshape(1, -1))
    return out.reshape(batch * seq, M_W)


def _group_rms(x, w_row):
    lane = lax.broadcasted_iota(jnp.int32, x.shape, 1)
    left = lane < N_DIM
    sq = x * x
    tot = jnp.sum(sq, axis=1, keepdims=True)
    lft = jnp.sum(jnp.where(left, sq, 0.0), axis=1, keepdims=True)
    ms = jnp.where(left, lft, tot - lft) * (1.0 / N_DIM)
    return x * lax.rsqrt(ms + EPS) * w_row


def _rope(x, cs, s1, s2):
    return x * cs + pltpu.roll(x, ROPE_HALF, 1) * s1 + pltpu.roll(x, LANES - ROPE_HALF, 1) * s2


def _dup_halves(x):
    lane = lax.broadcasted_iota(jnp.int32, x.shape, 1)
    sw = pltpu.roll(x, N_DIM, 1)
    left = lane < N_DIM
    return jnp.where(left, x, sw), jnp.where(left, sw, x)


def _nsa_prep_kernel(q_ref, kv_ref, kw_ref, sm_ref, cs_ref, s1_ref, s2_ref, qn_ref, kn_ref,
                     qt_ref, ka_ref, kwd_ref, vst_ref, vwt_ref, kc_ref, vc_ref, gtt_ref,
                     *, tp, seq):
    cs, s1, s2 = cs_ref[...], s1_ref[...], s2_ref[...]
    for j in range(N_W // LANES):
        sl = slice(j * LANES, (j + 1) * LANES)
        qn = _group_rms(q_ref[:, sl].astype(F32), qn_ref[...])
        qt_ref[sl, :] = (_rope(qn, cs, s1, s2) * (N_DIM ** -0.5 * LOG2E)).T.astype(BF16)
    kv = kv_ref[...].astype(F32)
    kc_ref[...] = kv[:, 0:LANES]
    vc_ref[...] = kv[:, LANES:2 * LANES]
    ks = _rope(_group_rms(kv[:, 2 * LANES:3 * LANES], kn_ref[1:2, :]), cs, s1, s2)
    k0, k1 = _dup_halves(ks)
    pos0 = (pl.program_id(0) * tp) % seq
    pos = pos0 + lax.broadcasted_iota(jnp.int32, (tp, LANES), 0)
    blk = lax.broadcasted_iota(jnp.int32, (tp, LANES), 1)
    onehot = jnp.where(jnp.right_shift(pos, SEL_SHIFT) == blk, 1.0, 0.0).astype(BF16)
    ka_ref[:, 0:LANES] = k0.astype(BF16)
    ka_ref[:, LANES:2 * LANES] = onehot
    ka_ref[:, 2 * LANES:3 * LANES] = k1.astype(BF16)
    ka_ref[:, 3 * LANES:4 * LANES] = onehot
    vst = kv[:, 3 * LANES:4 * LANES].T
    fill = (lax.broadcasted_iota(jnp.int32, (V_ROWS - N_DIM, Q_BLOCK), 0) == 0).astype(BF16)
    for c in range(tp // Q_BLOCK):
        for g in range(N_GROUPS):
            vst_ref[c, g * V_ROWS:g * V_ROWS + N_DIM, :] = (
                vst[g * N_DIM:(g + 1) * N_DIM, c * Q_BLOCK:(c + 1) * Q_BLOCK].astype(BF16))
            vst_ref[c, g * V_ROWS + N_DIM:(g + 1) * V_ROWS, :] = fill
    kwb = kw_ref[...].astype(F32)
    kw = _rope(_group_rms(kwb[:, 0:LANES], kn_ref[2:3, :]), cs, s1, s2)
    w0, w1 = _dup_halves(kw)
    kwd_ref[:, 0:LANES] = w0.astype(BF16)
    kwd_ref[:, LANES:2 * LANES] = w1.astype(BF16)
    vwt_ref[...] = kwb[:, LANES:2 * LANES].T.astype(BF16)
    gtt_ref[...] = _sigmoid(sm_ref[...]).T


def _nsa_prep(proj, small, cs, s1, s2, qn_row, kn_rows, batch, seq):
    tp = KV_TILE
    t = batch * seq
    nsb = seq // tp
    rb = lambda w, off: pl.BlockSpec((tp, w), lambda i, off=off, w=w: (i, off // w))
    tab = pl.BlockSpec((tp, LANES), lambda i: (i % nsb, 0))
    o = lambda w: pl.BlockSpec((tp, w), lambda i: (i, 0))
    ot = lambda r: pl.BlockSpec((r, tp), lambda i: (0, i))
    sds = lambda w, dt: jax.ShapeDtypeStruct((t, w), dt)
    return pl.pallas_call(
        functools.partial(_nsa_prep_kernel, tp=tp, seq=seq),
        grid=(t // tp,),
        in_specs=[rb(512, C_NQ), rb(512, C_KC), rb(512, C_KW), o(LANES), tab, tab, tab,
                  pl.BlockSpec((1, LANES), lambda i: (0, 0)), pl.BlockSpec((3, LANES), lambda i: (0, 0))],
        out_specs=[ot(N_W), o(512), o(256),
                   pl.BlockSpec((tp // Q_BLOCK, N_GROUPS * V_ROWS, Q_BLOCK), lambda i: (i, 0, 0)),
                   ot(LANES), o(128), o(128), ot(LANES)],
        out_shape=[jax.ShapeDtypeStruct((N_W, t), BF16), sds(512, BF16), sds(256, BF16),
                   jax.ShapeDtypeStruct((t // Q_BLOCK, N_GROUPS * V_ROWS, Q_BLOCK), BF16),
                   jax.ShapeDtypeStruct((LANES, t), BF16),
                   sds(128, F32), sds(128, F32), jax.ShapeDtypeStruct((LANES, t), F32)],
        compiler_params=_cparams("parallel"),
        name="nsa_prep",
    )(proj, proj, proj, small, cs, s1, s2, qn_row, kn_rows)


def _gelu_tanh(x):
    return 0.5 * x * (1.0 + jnp.tanh(0.7978845608028654 * (x + 0.044715 * x * x * x)))


def _compress_mlp(x_ref, pa_ref, pb_ref, wa_ref, wb_ref, w2_ref, hb_ref):
    n = x_ref.shape[0]
    x = x_ref[...]
    ha = _dot((x + pa_ref[...]).astype(BF16), wa_ref[...])
    hb_ref[0:n, :] = _dot((x + pb_ref[...]).astype(BF16), wb_ref[...])
    hb_ref[n:n + SUBLANES, :] = jnp.zeros((SUBLANES, hb_ref.shape[1]), F32)
    hid = ha + hb_ref[1:n + 1, :]
    return _dot(_gelu_tanh(hid).astype(BF16), w2_ref[...])


def _nsa_compress_kernel(kx_ref, vx_ref, kpa_ref, kpb_ref, vpa_ref, vpb_ref, kwa_ref, kwb_ref, kw2_ref,
                         vwa_ref, vwb_ref, vw2_ref, cs_ref, s1_ref, s2_ref, kn_ref,
                         kcd_ref, vct_ref, hb_ref):
    n = kx_ref.shape[0]
    rowi = lax.broadcasted_iota(jnp.int32, (n, LANES), 0)
    live = rowi < n - 1
    kc = _compress_mlp(kx_ref, kpa_ref, kpb_ref, kwa_ref, kwb_ref, kw2_ref, hb_ref)
    kc = _rope(_group_rms(kc, kn_ref[0:1, :]), cs_ref[...], s1_ref[...], s2_ref[...])
    kc = jnp.where(live, kc, 0.0)
    c0, c1 = _dup_halves(kc)
    kcd_ref[:, 0:LANES] = c0.astype(BF16)
    kcd_ref[:, LANES:2 * LANES] = c1.astype(BF16)
    vc = _compress_mlp(vx_ref, vpa_ref, vpb_ref, vwa_ref, vwb_ref, vw2_ref, hb_ref)
    vc = jnp.where(live, vc, 0.0)
    vct_ref[0] = vc.T.astype(BF16)


def _nsa_compress(kx, vx, kp, vp, cs, s1, s2, kn_rows, batch, nrow):
    wide = CMP_STRIDE * LANES
    full = lambda a: pl.BlockSpec(a.shape, lambda b: (0,) * a.ndim)
    xs = pl.BlockSpec((nrow, wide), lambda b: (b, 0))
    args = (kx, vx, kp[0], kp[1], vp[0], vp[1], kp[2], kp[3], kp[4], vp[2], vp[3], vp[4], cs, s1, s2, kn_rows)
    return pl.pallas_call(
        _nsa_compress_kernel,
        grid=(batch,),
        in_specs=[xs, xs] + [full(a) for a in args[2:]],
        out_specs=[pl.BlockSpec((nrow, 2 * LANES), lambda b: (b, 0)),
                   pl.BlockSpec((1, LANES, nrow), lambda b: (b, 0, 0))],
        out_shape=[jax.ShapeDtypeStruct((batch * nrow, 2 * LANES), BF16),
                   jax.ShapeDtypeStruct((batch, LANES, nrow), BF16)],
        scratch_shapes=[pltpu.VMEM((nrow + SUBLANES, 2 * CMP_HIDDEN), F32)],
        compiler_params=_cparams("parallel"),
        name="nsa_compress",
    )(*args)


def _compress_params(pos, w1, w2):
    def expand(w1_half, pos_half):
        w = w1_half.reshape(CMP_STRIDE, N_DIM, CMP_HIDDEN)
        z = jnp.zeros_like(w)
        g0 = jnp.concatenate([w, z], axis=-1)
        g1 = jnp.concatenate([z, w], axis=-1)
        wx = jnp.stack([g0, g1], axis=1).reshape(CMP_STRIDE * LANES, 2 * CMP_HIDDEN)
        px = jnp.concatenate([pos_half, pos_half], axis=-1).reshape(1, CMP_STRIDE * LANES)
        return wx.astype(BF16), px.astype(F32)

    half = CMP_STRIDE * N_DIM
    wa, pa = expand(w1[:half], pos[:CMP_STRIDE])
    wb, pb = expand(w1[half:], pos[CMP_STRIDE:])
    z2 = jnp.zeros_like(w2)
    w2x = jnp.concatenate([jnp.concatenate([w2, z2], axis=1), jnp.concatenate([z2, w2], axis=1)], axis=0)
    return pa, pb, wa, wb, w2x.astype(BF16)


def _nsa_attn_kernel(qt_ref, gtt_ref, z_ref, ka_ref, vst_ref,
                     kw0, kw1, kw2, kw3, kw4, wt0, wt1, wt2, wt3, wt4,
                     kcd_ref, vct_ref, ovt_ref, out_ref, qa_ref, acc_ref, *, ncmp):
    QB = Q_BLOCK
    NP = N_HEADS // 2
    PW = 2 * QB
    i = pl.program_id(1)
    t0 = i * QB
    kw_refs = (kw0, kw1, kw2, kw3, kw4)
    vwt_refs = (wt0, wt1, wt2, wt3, wt4)
    nwt = len(kw_refs)

    sub = lax.broadcasted_iota(jnp.int32, (QB, QB), 0)
    lane = lax.broadcasted_iota(jnp.int32, (QB, QB), 1)
    top = sub < N_DIM
    t_row = t0 + lax.broadcasted_iota(jnp.int32, (1, QB), 1)
    t_row2 = jnp.concatenate([t_row, t_row], axis=1)

    def pair_q(pr):
        slab = qt_ref[pr * LANES:(pr + 1) * LANES, :]
        zero = jnp.zeros_like(slab)
        return jnp.concatenate([jnp.where(top, slab, zero), jnp.where(top, zero, slab)], axis=1)

    blk_f = sub.astype(F32)
    cur = jnp.right_shift(t_row, SEL_SHIFT)
    grp = lambda pr: pr // (N_HPG // 2)
    for pr in range(NP):
        qa_ref[pr, 0:LANES, :] = pair_q(pr)

    def compressed(n):
        nidx = lax.broadcasted_iota(jnp.int32, (n, PW), 0)
        ok = (nidx * CMP_STRIDE + (CMP_LEN - 1) <= t_row2) & (nidx < ncmp - 1)
        any_ok = t_row2 >= CMP_LEN - 1
        s_cmp = [_dot(kcd_ref[0:n, grp(pr) * LANES:(grp(pr) + 1) * LANES], qa_ref[pr, 0:LANES, :])
                 for pr in range(NP)]
        p_cmp = []
        for pr in range(NP):
            s = jnp.where(ok, s_cmp[pr], NEG)
            mx = jnp.max(s, axis=0, keepdims=True)
            p = jnp.exp2(s - mx)
            den = jnp.sum(p, axis=0, keepdims=True)
            p_cmp.append(p * jnp.where(any_ok, 1.0 / den, 0.0))
        outs = [_dot(vct_ref[0, grp(pr) * N_DIM:(grp(pr) + 1) * N_DIM, 0:n], p_cmp[pr].astype(BF16))
                for pr in range(NP)]
        for g in range(N_GROUPS):
            psum = jnp.zeros((n, QB), F32)
            for pp in range(N_HPG // 2):
                p = p_cmp[g * (N_HPG // 2) + pp]
                psum = psum + p[:, 0:QB] + p[:, QB:PW]
            p_hi = psum.astype(BF16)
            p_lo = (psum - p_hi.astype(F32)).astype(BF16)
            outs.append(_dot(ovt_ref[:, 0:n], p_hi) + _dot(ovt_ref[:, 0:n], p_lo))
        return tuple(outs)

    n_opts = list(range(CMP_CHUNK, ncmp + 1, CMP_CHUNK))
    need = (t0 + QB - CMP_LEN) // CMP_STRIDE + 1
    outs = lax.switch(jnp.clip((need - 1) // CMP_CHUNK, 0, len(n_opts) - 1),
                      [functools.partial(compressed, n) for n in n_opts])
    o_cmp, imps = outs[:NP], outs[NP:]
    forced = (sub == 0) | (sub == cur) | (sub == cur - 1)
    n_rounds = SEL_TOPK - 3
    valid = sub <= cur
    works0 = [jnp.where(valid, jnp.where(forced, -jnp.inf, imp), NEG) for imp in imps]

    def rounds(works, exact):
        sels = [forced for _ in range(N_GROUPS)]
        for _ in range(n_rounds):
            for g in range(N_GROUPS):
                mx = jnp.max(works[g], axis=0, keepdims=True)
                if exact:
                    idx = jnp.min(jnp.where(works[g] == mx, blk_f, float(LANES)), axis=0, keepdims=True)
                    pick = blk_f == idx
                else:
                    pick = works[g] == mx
                sels[g] = sels[g] | pick
                works[g] = jnp.where(pick, -jnp.inf, works[g])
        return tuple(jnp.where(s & valid, 0.0, NEG) for s in sels)

    bias = rounds(list(works0), exact=False)
    picked = [jnp.sum(jnp.where((b == 0.0) & jnp.logical_not(forced), 1.0, 0.0), axis=0, keepdims=True)
              for b in bias]
    tied = jnp.max(functools.reduce(jnp.maximum, picked)) > n_rounds
    bias = lax.cond(tied, lambda: rounds(list(works0), exact=True), lambda: bias)
    for g in range(N_GROUPS):
        neg = bias[g].astype(BF16)
        neg2 = jnp.concatenate([neg, neg], axis=1)
        for pp in range(N_HPG // 2):
            qa_ref[g * (N_HPG // 2) + pp, LANES:2 * LANES, :] = neg2

    CH = QB
    above = jnp.concatenate([sub > lane, sub > lane], axis=1)

    def sel_scores(cj):
        k0 = pl.multiple_of(cj * CH, CH)
        return [_dot(ka_ref[pl.ds(k0, CH), 2 * grp(pr) * LANES:2 * (grp(pr) + 1) * LANES], qa_ref[pr])
                for pr in range(NP)]

    def sel_softmax(ss, ms, masked):
        new, alphas, ps = [], [], []
        for pr in range(NP):
            s = jnp.where(above, NEG, ss[pr]) if masked else ss[pr]
            m_new = jnp.maximum(ms[pr], jnp.max(s, axis=0, keepdims=True))
            alphas.append(jnp.exp2(ms[pr] - m_new))
            ps.append(jnp.exp2(s - m_new).astype(BF16))
            new.append(m_new)
        return tuple(new), alphas, ps

    def sel_values(cj, alphas, ps):
        for pr in range(NP):
            g = grp(pr)
            acc_ref[pr] = alphas[pr] * acc_ref[pr] + _dot(vst_ref[cj, g * V_ROWS:(g + 1) * V_ROWS, :], ps[pr])

    def sel_chunk(cj, ms, masked):
        ms, alphas, ps = sel_softmax(sel_scores(cj), ms, masked)
        sel_values(cj, alphas, ps)
        return ms

    def sel_tile(jt, ms, nc):
        c0 = jt * nc
        k0 = pl.multiple_of(c0 * CH, CH)
        big = [_dot(ka_ref[pl.ds(k0, nc * CH), 2 * grp(pr) * LANES:2 * (grp(pr) + 1) * LANES], qa_ref[pr])
               for pr in range(NP)]
        for c in range(nc):
            ss = [b[c * CH:(c + 1) * CH] for b in big]
            ms, alphas, ps = sel_softmax(ss, ms, False)
            sel_values(c0 + c, alphas, ps)
        return ms

    acc_ref[...] = jnp.zeros_like(acc_ref)
    ms = tuple(jnp.full((1, PW), 2.0 * NEG, F32) for _ in range(NP))
    done = 0
    for nc in (4 * KV_TILE // CH, 2 * KV_TILE // CH, KV_TILE // CH):
        n_run = (i - done) // nc
        ms = lax.fori_loop(done // nc, done // nc + n_run, functools.partial(sel_tile, nc=nc), ms)
        done = done + n_run * nc
    ms = lax.fori_loop(done, i, lambda cj, ms: sel_chunk(cj, ms, False), ms)
    s_diag = sel_scores(i)
    s_win = [[_dot(kw_refs[c][:, grp(pr) * LANES:(grp(pr) + 1) * LANES], qa_ref[pr, 0:LANES, :])
              for c in range(nwt)] for pr in range(NP)]
    ms, alphas, ps_diag = sel_softmax(s_diag, ms, True)
    p_win, d_win = [], []
    for pr in range(NP):
        ss = []
        for c in range(nwt):
            ok = (i - (nwt - 1) + c) >= 0
            if c == 0:
                ok = ok & above
            elif c == nwt - 1:
                ok = ok & jnp.logical_not(above)
            ss.append(jnp.where(ok, s_win[pr][c], NEG))
        mx = ss[0].max(axis=0, keepdims=True)
        for c in range(1, nwt):
            mx = jnp.maximum(mx, ss[c].max(axis=0, keepdims=True))
        ps = [jnp.exp2(ss[c] - mx) for c in range(nwt)]
        den = ps[0].sum(axis=0, keepdims=True)
        for c in range(1, nwt):
            den = den + ps[c].sum(axis=0, keepdims=True)
        p_win.append([p.astype(BF16) for p in ps])
        d_win.append(den)
    sel_values(i, alphas, ps_diag)
    o_win = []
    for pr in range(NP):
        g = grp(pr)
        acc = _dot(vwt_refs[0][g * N_DIM:(g + 1) * N_DIM, :], p_win[pr][0])
        for c in range(1, nwt):
            acc = acc + _dot(vwt_refs[c][g * N_DIM:(g + 1) * N_DIM, :], p_win[pr][c])
        o_win.append(acc * (1.0 / d_win[pr]))
    o_sel = [acc_ref[pr, 0:N_DIM, :] * (1.0 / acc_ref[pr, N_DIM:N_DIM + 1, :]) for pr in range(NP)]

    for pr in range(NP):
        halves = []
        for par in range(2):
            h = 2 * pr + par
            cs = slice(par * QB, (par + 1) * QB)
            gc = gtt_ref[SM_G + 3 * h:SM_G + 3 * h + 1, :]
            gs = gtt_ref[SM_G + 3 * h + 1:SM_G + 3 * h + 2, :]
            gw = gtt_ref[SM_G + 3 * h + 2:SM_G + 3 * h + 3, :]
            halves.append(gc * o_cmp[pr][:, cs] + gs * o_sel[pr][:, cs] + gw * o_win[pr][:, cs])
        y = jnp.concatenate(halves, axis=0).T
        sl = slice(pr * LANES, (pr + 1) * LANES)
        out_ref[:, sl] = (y * _silu(z_ref[:, sl].astype(F32))).astype(out_ref.dtype)


def _nsa_attn(proj, qt, gtt, ka, vst, kwd, vwt, kcd, vct, ovt, batch, seq):
    QB = Q_BLOCK
    nqb = seq // QB
    ncmp = seq // CMP_STRIDE
    ntile = seq // KV_TILE
    nwt = WINDOW // QB + 1
    row = lambda w: pl.BlockSpec((QB, w), lambda b, i: (b * nqb + i, 0))
    colb = lambda r: pl.BlockSpec((r, QB), lambda b, i: (0, b * nqb + i))
    wblk = lambda b, i, c: b * nqb + jnp.maximum(i - (nwt - 1) + c, 0)
    in_specs = ([colb(N_W), colb(LANES), pl.BlockSpec((QB, N_W), lambda b, i: (b * nqb + i, C_NZ // N_W)),
                 pl.BlockSpec((seq, 4 * LANES), lambda b, i: (b, 0)),
                 pl.BlockSpec((nqb, N_GROUPS * V_ROWS, QB), lambda b, i: (b, 0, 0))]
                + [pl.BlockSpec((QB, 2 * LANES), lambda b, i, c=c: (wblk(b, i, c), 0)) for c in range(nwt)]
                + [pl.BlockSpec((LANES, QB), lambda b, i, c=c: (0, wblk(b, i, c))) for c in range(nwt)]
                + [pl.BlockSpec((ncmp, 2 * LANES), lambda b, i: (b, 0)),
                   pl.BlockSpec((1, LANES, ncmp), lambda b, i: (b, 0, 0)),
                   pl.BlockSpec((LANES, ncmp), lambda b, i: (0, 0))])
    return pl.pallas_call(
        functools.partial(_nsa_attn_kernel, ncmp=ncmp),
        grid=(batch, nqb),
        in_specs=in_specs,
        out_specs=row(N_W),
        out_shape=jax.ShapeDtypeStruct((batch * seq, N_W), BF16),
        scratch_shapes=[pltpu.VMEM((N_HEADS // 2, 2 * LANES, 2 * QB), BF16),
                        pltpu.VMEM((N_HEADS // 2, V_ROWS, 2 * QB), F32)],
        compiler_params=_cparams("parallel", "arbitrary"),
        name="nsa_attn",
    )(qt, gtt, proj, ka, vst, *([kwd] * nwt), *([vwt] * nwt), kcd, vct, ovt)


def _hgrn_kernel(f_ref, i_ref, q_ref, z_ref, lb_ref, hn_ref, out_ref, st_ref):
    T, D = H_TILE, H_DIM
    W = H_W

    @pl.when(pl.program_id(1) == 0)
    def _():
        st_ref[...] = jnp.zeros_like(st_ref)

    lb = lb_ref[...]
    fp = f_ref[...]
    e = jnp.exp(-jnp.abs(fp))
    r = 1.0 / (1.0 + e)
    sig = jnp.where(fp >= 0, r, e * r)
    nsig = jnp.where(fp >= 0, e * r, r)
    logf2 = jnp.log(lb + (1.0 - lb) * sig) * LOG2E
    kk = (1.0 - lb) * nsig

    row = lax.broadcasted_iota(jnp.int32, (T, T), 0)
    col = lax.broadcasted_iota(jnp.int32, (T, T), 1)
    tri01 = jnp.where(col <= row, 1.0, 0.0).astype(BF16)
    bcum = _dot01(tri01, logf2)
    rowi = lax.broadcasted_iota(jnp.int32, (T, 1), 0)

    segs = []
    sg = 2 * H_LAG
    while sg <= T:
        if sg == T:
            bmid = jnp.broadcast_to(bcum[sg // 2 - 1:sg // 2, :], (T, W))
        else:
            b3 = bcum.reshape(T // sg, sg, W)
            bmid = jnp.broadcast_to(b3[:, sg // 2 - 1:sg // 2, :], (T // sg, sg, W)).reshape(T, W)
        x = bcum - bmid
        shift = sg.bit_length() - 1
        pair_ok = ((jnp.right_shift(row, shift) == jnp.right_shift(col, shift))
                   & (jnp.bitwise_and(row, sg - 1) >= sg // 2) & (jnp.bitwise_and(col, sg - 1) < sg // 2))
        segs.append((jnp.exp2(jnp.minimum(x, -x)), pair_ok))
        sg *= 2

    sub3 = lax.broadcasted_iota(jnp.int32, (T // SUBLANES, SUBLANES, 1), 1)
    lag_ok = [jnp.bitwise_and(sub3, H_LAG - 1) >= d for d in range(H_LAG)]
    b_end = bcum[T - 1:T, :]
    v_all = i_ref[...].astype(F32)
    for h in range(H_HEADS):
        hs = slice(h * D, (h + 1) * D)
        q = _silu(q_ref[:, hs].astype(F32))
        k = kk[:, hs]
        b = bcum[:, hs]
        v = v_all[:, hs]
        vb = v.astype(BF16)
        a_seg = []
        for fac, _ in segs:
            f = fac[:, hs]
            a_seg.append(_dot_nt((q * f).astype(BF16), (k * f).astype(BF16)))
        st = st_ref[h]
        o_state = _dot_nt((q * jnp.exp2(b)).astype(BF16), st.astype(BF16))
        kt = (k * jnp.exp2(b_end[:, hs] - b)).astype(BF16)
        st_ref[h] = st * jnp.exp2(b_end[:, hs]) + _dot_tn(vb, kt)
        q3, k3, b3, v3 = (a.reshape(T // SUBLANES, SUBLANES, D) for a in (q, k, b, v))
        o3 = jnp.sum(q3 * k3, axis=2, keepdims=True) * v3
        for d in range(1, H_LAG):
            ks, bs, vs = (pltpu.roll(a, d, 1) for a in (k3, b3, v3))
            a = jnp.sum(q3 * ks * jnp.exp2(b3 - bs), axis=2, keepdims=True)
            o3 = o3 + jnp.where(lag_ok[d], a, 0.0) * vs
        a_tot = jnp.zeros((T, T), F32)
        for (_, pair_ok), a in zip(segs, a_seg):
            a_tot = jnp.where(pair_ok, a, a_tot)
        oh = o3.reshape(T, D) + o_state + _dot(a_tot.astype(BF16), vb)
        on = oh * lax.rsqrt(jnp.mean(oh * oh, axis=1, keepdims=True) + EPS) * hn_ref[:, hs]
        out_ref[:, hs] = (on * _silu(z_ref[:, hs].astype(F32))).astype(out_ref.dtype)


def _hgrn(proj, fgate, lb_row, hn_row, batch, seq):
    T = H_TILE
    nt = seq // T
    cb = lambda j: pl.BlockSpec((T, H_W), lambda b, c, j=j: (b * nt + c, j))
    vec = pl.BlockSpec((1, H_W), lambda b, c: (0, 0))
    return pl.pallas_call(
        _hgrn_kernel,
        grid=(batch, nt),
        in_specs=[pl.BlockSpec((T, H_W), lambda b, c: (b * nt + c, 0)), cb(1), cb(2), cb(3), vec, vec],
        out_specs=pl.BlockSpec((T, H_W), lambda b, c: (b * nt + c, 0)),
        out_shape=jax.ShapeDtypeStruct((batch * seq, H_W), BF16),
        scratch_shapes=[pltpu.VMEM((H_HEADS, H_DIM, H_DIM), F32)],
        compiler_params=_cparams("parallel", "arbitrary"),
        name="hgrn2",
    )(fgate, proj, proj, proj, lb_row, hn_row)


def _even_relayout(a):
    nb = 5 * M_W + 2 * M_HEADS
    g0 = nb + N_W + 6 * N_KV_W
    col = lambda lo, n: lax.slice_in_dim(a, lo, lo + n, axis=-1)
    pad = jnp.zeros(a.shape[:-1] + (C_NZ - C_SMALL - SM_G - 3 * N_HEADS,), a.dtype)
    out = jnp.concatenate([
        col(0, 5 * M_W),
        col(nb, N_W + 6 * N_KV_W),
        col(5 * M_W, 2 * M_HEADS),
        col(g0, 3 * N_HEADS), pad,
        col(g0 + 3 * N_HEADS, N_W),
    ], axis=-1)
    assert out.shape[-1] == EV_COLS
    return out


def _rope_tables(positions):
    inv_freq = ROPE_THETA ** (-np.arange(ROPE_HALF, dtype=np.float64) / ROPE_HALF)
    ang = positions.astype(np.float64)[:, None] * inv_freq[None, :]
    cos, sin = np.cos(ang).astype(np.float32), np.sin(ang).astype(np.float32)
    n = positions.shape[0]
    z8 = np.zeros((n, ROPE_HALF), np.float32)
    rest1 = np.ones((n, N_DIM - ROPE_DIM), np.float32)
    rest0 = np.zeros((n, N_DIM - ROPE_DIM), np.float32)
    cs = np.concatenate([cos, cos, rest1], axis=1)
    s1 = np.concatenate([z8, sin, rest0], axis=1)
    s2 = np.concatenate([-sin, z8, rest0], axis=1)
    two = lambda a: np.concatenate([a, a], axis=1)
    return two(cs), two(s1), two(s2)


def _overlap_matrix(seq):
    ncmp = seq // CMP_STRIDE
    n_sel = seq // SEL_BLOCK
    cs = np.arange(ncmp) * CMP_STRIDE
    ss = np.arange(n_sel) * SEL_BLOCK
    ov = np.clip(np.minimum(cs[:, None] + CMP_LEN, ss[None, :] + SEL_BLOCK)
                 - np.maximum(cs[:, None], ss[None, :]), 0, None).astype(np.float32)
    ov[ncmp - 1, :] = 0.0
    return np.pad(ov, ((0, 0), (0, LANES - n_sel)))


def _even_layer(x2d, res, norm_w, w_in, b_in, f_bias, conv_w, conv_b, head_norm, q_norm, k_norm,
                ck_pos, ck_w1, ck_w2, cv_pos, cv_w1, cv_w2, w_out, batch, seq):
    assert seq // SEL_BLOCK <= LANES and seq % KV_TILE == 0
    outs = _norm_proj(x2d, norm_w, _even_relayout(w_in.astype(BF16)), _even_relayout(b_in),
                      (C_SMALL, C_SMALL + LANES), tn=EV_COLS // 3, res=res)
    if res:
        x2d = outs[0]
    proj, small = outs[-2:]

    f_bias_row = jnp.zeros((1, LANES), F32).at[0, SM_F:SM_F + M_HEADS].set(f_bias)
    ya = _mlstm(proj, small, conv_w, conv_b, f_bias_row, head_norm, batch, seq)

    cs, s1, s2 = (jnp.asarray(a) for a in _rope_tables(np.arange(seq)))
    qn_row = jnp.concatenate([q_norm, q_norm]).reshape(1, LANES)
    kn_rows = jnp.concatenate([k_norm, k_norm], axis=1)
    qt, ka, kwd, vst, vwt, kc, vc, gtt = _nsa_prep(proj, small, cs, s1, s2, qn_row, kn_rows, batch, seq)

    ncmp = seq // CMP_STRIDE
    ccs, cs1, cs2 = (jnp.asarray(a) for a in _rope_tables(np.arange(ncmp) * CMP_STRIDE + CMP_LEN - 1))
    kcd, vct = _nsa_compress(
        kc.reshape(batch * ncmp, CMP_STRIDE * LANES), vc.reshape(batch * ncmp, CMP_STRIDE * LANES),
        _compress_params(ck_pos, ck_w1, ck_w2), _compress_params(cv_pos, cv_w1, cv_w2),
        ccs, cs1, cs2, kn_rows, batch, ncmp)

    ovt = jnp.asarray(_overlap_matrix(seq).T).astype(BF16)
    yb = _nsa_attn(proj, qt, gtt, ka, vst, kwd, vwt, kcd, vct, ovt, batch, seq)

    w_o = w_out.astype(BF16)
    return x2d, [(ya, w_o[:M_W]), (yb, w_o[M_W:])]


def _odd_layer(x2d, res, norm_w, w_in, b_in, lb, head_norm, w_out, batch, seq):
    outs = _norm_proj(x2d, norm_w, w_in.astype(BF16), b_in, (0, H_W), tn=H_W, res=res)
    if res:
        x2d = outs[0]
    proj, fgate = outs[-2:]
    y = _hgrn(proj, fgate, lb.reshape(1, H_W), head_norm.reshape(1, H_W), batch, seq)
    return x2d, [(y, w_out.astype(BF16))]


def kernel(x, norm_w, ev_w_in, ev_b_in, mlstm_f_bias, mlstm_conv_w, mlstm_conv_b, mlstm_head_norm, nsa_q_norm, nsa_k_norm, cmp_k_pos, cmp_k_w1, cmp_k_w2, cmp_v_pos, cmp_v_w1, cmp_v_w2, ev_w_out, od_w_in, od_b_in, hgrn_lb_logits, hgrn_head_norm, od_w_out):
    batch, seq, d = x.shape
    depth = norm_w.shape[0]
    lbs = jnp.cumsum(jax.nn.softmax(hgrn_lb_logits.astype(F32), axis=0), axis=0)
    x2d = x.reshape(batch * seq, d)
    res = []
    for layer in range(depth):
        j = layer // 2
        if layer % 2 == 0:
            x2d, res = _even_layer(x2d, res, norm_w[layer], ev_w_in[j], ev_b_in[j], mlstm_f_bias[j],
                                   mlstm_conv_w[j], mlstm_conv_b[j], mlstm_head_norm[j], nsa_q_norm[j],
                                   nsa_k_norm[j], cmp_k_pos[j], cmp_k_w1[j], cmp_k_w2[j], cmp_v_pos[j],
                                   cmp_v_w1[j], cmp_v_w2[j], ev_w_out[j], batch, seq)
        else:
            x2d, res = _odd_layer(x2d, res, norm_w[layer], od_w_in[j], od_b_in[j], lbs[j], hgrn_head_norm[j],
                                  od_w_out[j], batch, seq)
    if len(res) == 2:
        x2d = _out_proj2(res[0][0], res[1][0], res[0][1], res[1][1], x2d)
    elif res:
        x2d = _out_proj1(res[0][0], res[0][1], x2d)
    return x2d.reshape(batch, seq, d)
```

```python
import functools

import numpy as np
import jax
import jax.numpy as jnp
from jax import lax
from jax.experimental import pallas as pl
from jax.experimental.pallas import tpu as pltpu

F32 = jnp.float32
BF16 = jnp.bfloat16

LANES = 128
SUBLANES = 8
BF16_SUBLANES = 16
EPS = 1e-6
NEG = -1e30
VMEM_LIMIT_BYTES = 56 * 1024 * 1024

M_HEADS = 4
M_DIM = 128
M_W = M_HEADS * M_DIM
M_CHUNK = 128
CONV_K = 4
N_HEADS = 8
N_GROUPS = 2
N_HPG = N_HEADS // N_GROUPS
N_DIM = 64
N_W = N_HEADS * N_DIM
N_KV_W = N_GROUPS * N_DIM
CMP_LEN = 32
CMP_STRIDE = 16
CMP_HIDDEN = 128
CMP_CHUNK = 128
SEL_BLOCK = 64
SEL_SHIFT = 6
SEL_TOPK = 16
WINDOW = 512
Q_BLOCK = 128
ROPE_THETA = 500000.0
ROPE_DIM = N_DIM // 4
ROPE_HALF = ROPE_DIM // 2
KV_TILE = 512
V_ROWS = 80
LOG2E = 1.4426950408889634
H_HEADS = 8
H_DIM = 128
H_W = H_HEADS * H_DIM
H_TILE = 128
H_LAG = 2

C_MQ, C_MK, C_MV, C_MO, C_MZ = 0, 512, 1024, 1536, 2048
C_NQ = 2560
C_KC, C_VC, C_KS, C_VS = 3072, 3200, 3328, 3456
C_KW, C_VW, C_SMALL = 3584, 3712, 3840
C_NZ = 4096
EV_COLS = 4608
SM_I, SM_F, SM_G = 0, 4, 8


def _cparams(*sem):
    return pltpu.CompilerParams(dimension_semantics=sem, vmem_limit_bytes=VMEM_LIMIT_BYTES)


def _dot(a, b):
    return jnp.dot(a, b, preferred_element_type=F32)


def _dot_nt(a, b):
    return lax.dot_general(a, b, (((1,), (1,)), ((), ())), preferred_element_type=F32)


def _dot_tn(a, b):
    return lax.dot_general(a, b, (((0,), (0,)), ((), ())), preferred_element_type=F32)


def _sigmoid(x):
    return 1.0 / (1.0 + jnp.exp(-x))


def _silu(x):
    return x * _sigmoid(x)


def _split3(x):
    hi = x.astype(BF16)
    r = x - hi.astype(F32)
    mid = r.astype(BF16)
    lo = (r - mid.astype(F32)).astype(BF16)
    return hi, mid, lo


def _dot01(m01, x):
    hi, mid, lo = _split3(x)
    return _dot(m01, hi) + _dot(m01, mid) + _dot(m01, lo)


def _norm_proj_kernel(*refs, n_res, g_tiles, g_off):
    x_ref = refs[0]
    y_refs, wy_refs = refs[1:1 + n_res], refs[1 + n_res:1 + 2 * n_res]
    nw_ref, w_ref, b_ref = refs[1 + 2 * n_res:4 + 2 * n_res]
    outs = refs[4 + 2 * n_res:]
    xo_ref = outs[0] if n_res else None
    o_ref, g_ref, h_ref = outs[-3:]
    j = pl.program_id(1)

    @pl.when(j == 0)
    def _():
        x = x_ref[...]
        for y_ref, wy_ref in zip(y_refs, wy_refs):
            x = x + _dot(y_ref[...], wy_ref[...])
        if n_res:
            xo_ref[...] = x
        ms = jnp.mean(x * x, axis=-1, keepdims=True)
        h_ref[...] = (x * lax.rsqrt(ms + EPS) * nw_ref[...]).astype(BF16)

    acc = _dot(h_ref[...], w_ref[...]) + b_ref[...]
    o_ref[...] = acc.astype(o_ref.dtype)

    @pl.when((j >= g_tiles[0]) & (j < g_tiles[1]))
    def _():
        g_ref[...] = acc[:, g_off:g_off + g_ref.shape[1]]


def _norm_proj(x2d, norm_w, w_bf16, bias, g_cols, tn, res=(), tm=1024):
    t, d = x2d.shape
    n = w_bf16.shape[1]
    g_w = min(g_cols[1] - g_cols[0], tn)
    g_tiles = (g_cols[0] // tn, -(-g_cols[1] // tn))
    row = lambda w: pl.BlockSpec((tm, w), lambda i, j: (i, 0))
    const = lambda a: pl.BlockSpec(a.shape, lambda i, j: (0, 0))
    ys, wys = [y for y, _ in res], [wy for _, wy in res]
    out_specs = [pl.BlockSpec((tm, tn), lambda i, j: (i, j)),
                 pl.BlockSpec((tm, g_w), lambda i, j: (i, jnp.clip(j - g_tiles[0], 0, g_tiles[1] - g_tiles[0] - 1)))]
    out_shape = [jax.ShapeDtypeStruct((t, n), BF16), jax.ShapeDtypeStruct((t, g_cols[1] - g_cols[0]), F32)]
    if res:
        out_specs = [row(d)] + out_specs
        out_shape = [jax.ShapeDtypeStruct((t, d), F32)] + out_shape
    return pl.pallas_call(
        functools.partial(_norm_proj_kernel, n_res=len(res), g_tiles=g_tiles, g_off=g_cols[0] % tn),
        grid=(t // tm, n // tn),
        in_specs=([row(d)] + [row(y.shape[1]) for y in ys] + [const(wy) for wy in wys]
                  + [pl.BlockSpec((1, d), lambda i, j: (0, 0)),
                     pl.BlockSpec((d, tn), lambda i, j: (0, j)),
                     pl.BlockSpec((1, tn), lambda i, j: (0, j))]),
        out_specs=out_specs,
        out_shape=out_shape,
        scratch_shapes=[pltpu.VMEM((tm, d), BF16)],
        compiler_params=_cparams("parallel", "arbitrary"),
        name="norm_proj",
    )(x2d, *ys, *wys, norm_w.reshape(1, d), w_bf16, bias.reshape(1, n))


def _out_proj2_kernel(ya_ref, yb_ref, wa_ref, wb_ref, x_ref, o_ref):
    o_ref[...] = x_ref[...] + _dot(ya_ref[...], wa_ref[...]) + _dot(yb_ref[...], wb_ref[...])


def _out_proj2(ya, yb, wa, wb, x2d, tm=1024):
    t, d = x2d.shape
    ka, kb = ya.shape[1], yb.shape[1]
    return pl.pallas_call(
        _out_proj2_kernel,
        grid=(t // tm,),
        in_specs=[
            pl.BlockSpec((tm, ka), lambda i: (i, 0)),
            pl.BlockSpec((tm, kb), lambda i: (i, 0)),
            pl.BlockSpec((ka, d), lambda i: (0, 0)),
            pl.BlockSpec((kb, d), lambda i: (0, 0)),
            pl.BlockSpec((tm, d), lambda i: (i, 0)),
        ],
        out_specs=pl.BlockSpec((tm, d), lambda i: (i, 0)),
        out_shape=jax.ShapeDtypeStruct((t, d), F32),
        compiler_params=_cparams("parallel"),
        name="out_proj2",
    )(ya, yb, wa, wb, x2d)


def _out_proj1_kernel(y_ref, w_ref, x_ref, o_ref):
    o_ref[...] = x_ref[...] + _dot(y_ref[...], w_ref[...])


def _out_proj1(y, w, x2d, tm=1024):
    t, d = x2d.shape
    k = y.shape[1]
    return pl.pallas_call(
        _out_proj1_kernel,
        grid=(t // tm,),
        in_specs=[
            pl.BlockSpec((tm, k), lambda i: (i, 0)),
            pl.BlockSpec((k, d), lambda i: (0, 0)),
            pl.BlockSpec((tm, d), lambda i: (i, 0)),
        ],
        out_specs=pl.BlockSpec((tm, d), lambda i: (i, 0)),
        out_shape=jax.ShapeDtypeStruct((t, d), F32),
        compiler_params=_cparams("parallel"),
        name="out_proj1",
    )(y, w, x2d)


def _mlstm_kernel(q_ref, k_ref, v_ref, o_ref, z_ref, sm_ref, cw_ref, cb_ref, fb_ref, hn_ref,
                  out_ref, xbuf, c_ref, n_ref, m_ref):
    L, D = M_CHUNK, M_DIM
    NB = q_ref.shape[0]
    XR = xbuf.shape[1]
    X0 = BF16_SUBLANES

    @pl.when(pl.program_id(0) == 0)
    def _():
        xbuf[...] = jnp.zeros_like(xbuf)
        c_ref[...] = jnp.zeros_like(c_ref)
        n_ref[...] = jnp.zeros_like(n_ref)
        m_ref[...] = jnp.zeros_like(m_ref)

    t_i = lax.broadcasted_iota(jnp.int32, (L, XR), 0)
    r_i = lax.broadcasted_iota(jnp.int32, (L, XR), 1)
    row = lax.broadcasted_iota(jnp.int32, (L, L), 0)
    col = lax.broadcasted_iota(jnp.int32, (L, L), 1)
    causal = col <= row
    lane = lax.broadcasted_iota(jnp.int32, (L, LANES), 1)
    is_f = (lane >= SM_F) & (lane < SM_F + M_HEADS)

    heads = []
    for b in range(NB):
        xbuf[b, X0:X0 + L, 0:M_W] = q_ref[b]
        xbuf[b, X0:X0 + L, M_W:2 * M_W] = k_ref[b]
        x2 = xbuf[b]
        acc = cb_ref[...] + cw_ref[CONV_K - 1:CONV_K, :] * x2[X0:X0 + L].astype(F32)
        for d in range(1, CONV_K):
            shift = jnp.where(r_i == t_i + X0 - d, 1.0, 0.0).astype(BF16)
            acc = acc + cw_ref[CONV_K - 1 - d:CONV_K - d, :] * _dot(shift, x2)
        qk = _silu(acc)
        xbuf[b, 0:X0, :] = xbuf[b, L:L + X0, :]

        fpre = sm_ref[b] + fb_ref[...]
        gates = jnp.where(is_f, jnp.minimum(fpre, 0.0) - jnp.log(1.0 + jnp.exp(-jnp.abs(fpre))), fpre)
        bcols = _dot01(jnp.where(causal, 1.0, 0.0).astype(BF16), gates)
        gates_t = gates.T
        bcols_t = bcols.T
        for h in range(M_HEADS):
            hs = slice(h * D, (h + 1) * D)
            q = qk[:, hs]
            k = qk[:, M_W + h * D:M_W + (h + 1) * D] * (D ** -0.5)
            i_col = gates[:, SM_I + h:SM_I + h + 1]
            i_row = gates_t[SM_I + h:SM_I + h + 1, :]
            bcum_col = bcols[:, SM_F + h:SM_F + h + 1]
            bcum_row = bcols_t[SM_F + h:SM_F + h + 1, :]
            m_prev = m_ref[b * M_HEADS + h]
            dmat = jnp.where(causal, bcum_col - bcum_row + i_row, -jnp.inf)
            inter = bcum_col + m_prev
            m_row = jnp.maximum(jnp.max(dmat, axis=1, keepdims=True), inter)
            b_last = bcum_col[L - 1:L, :]
            dl = b_last - bcum_col + i_col
            m_new = jnp.maximum(b_last + m_prev, jnp.max(dl, axis=0, keepdims=True))
            wkk = jnp.exp(dl - m_new) * k
            heads.append(dict(
                b=b, hs=hs, st=b * M_HEADS + h, q=q, qb=q.astype(BF16), kb=k.astype(BF16),
                vb=v_ref[b, :, hs].astype(BF16), wkk=wkk,
                e=jnp.exp(dmat - m_row), w_inter=jnp.exp(inter - m_row), floor=jnp.exp(-m_row),
                decay=jnp.exp(b_last + m_prev - m_new), m_new=m_new))
    s_qk = [_dot_nt(t["qb"], t["kb"]) for t in heads]
    s_qc = [_dot(t["qb"], c_ref[t["st"]].astype(BF16)) for t in heads]
    s_kv = [_dot_tn(t["wkk"].astype(BF16), t["vb"]) for t in heads]
    smats = [s_qk[j] * t["e"] for j, t in enumerate(heads)]
    s_sv = [_dot(smats[j].astype(BF16), t["vb"]) for j, t in enumerate(heads)]
    for j, t in enumerate(heads):
        b, hs, st = t["b"], t["hs"], t["st"]
        num = s_sv[j] + t["w_inter"] * s_qc[j]
        den = (jnp.sum(smats[j], axis=1, keepdims=True)
               + t["w_inter"] * jnp.sum(t["q"] * n_ref[st], axis=1, keepdims=True))
        hh = num / jnp.maximum(jnp.abs(den), t["floor"])
        c_ref[st] = t["decay"] * c_ref[st] + s_kv[j]
        n_ref[st] = t["decay"] * n_ref[st] + jnp.sum(t["wkk"], axis=0, keepdims=True)
        m_ref[st] = t["m_new"]
        hn = hh * lax.rsqrt(jnp.mean(hh * hh, axis=1, keepdims=True) + EPS) * hn_ref[:, hs]
        gate = _sigmoid(o_ref[b, :, hs].astype(F32)) * _silu(z_ref[b, :, hs].astype(F32))
        out_ref[b, :, hs] = (hn * gate).astype(out_ref.dtype)


def _mlstm(proj, small, conv_w, conv_b, f_bias_row, head_norm, batch, seq):
    L = M_CHUNK
    nc = seq // L
    p3 = proj.reshape(batch, seq, proj.shape[1])
    s3 = small.reshape(batch, seq, LANES)
    cb = lambda off: pl.BlockSpec((batch, L, M_W), lambda c, off=off: (0, c, off // M_W))
    full = lambda r, w: pl.BlockSpec((r, w), lambda c: (0, 0))
    out = pl.pallas_call(
        _mlstm_kernel,
        grid=(nc,),
        in_specs=[
            cb(C_MQ), cb(C_MK), cb(C_MV), cb(C_MO), cb(C_MZ),
            pl.BlockSpec((batch, L, LANES), lambda c: (0, c, 0)),
            full(CONV_K, 2 * M_W), full(1, 2 * M_W), full(1, LANES), full(1, M_W),
        ],
        out_specs=pl.BlockSpec((batch, L, M_W), lambda c: (0, c, 0)),
        out_shape=jax.ShapeDtypeStruct((batch, seq, M_W), BF16),
        scratch_shapes=[
            pltpu.VMEM((batch, 2 * L, 2 * M_W), BF16),
            pltpu.VMEM((batch * M_HEADS, M_DIM, M_DIM), F32),
            pltpu.VMEM((batch * M_HEADS, 1, M_DIM), F32),
            pltpu.VMEM((batch * M_HEADS, 1, 1), F32),
        ],
        compiler_params=_cparams("arbitrary"),
        name="mlstm",
    )(p3, p3, p3, p3, p3, s3, conv_w, conv_b.reshape(1, -1), f_bias_row, head_norm.reshape(1, -1))
    return out.reshape(batch * seq, M_W)


def _group_rms(x, w_row):
    lane = lax.broadcasted_iota(jnp.int32, x.shape, 1)
    left = lane < N_DIM
    sq = x * x
    tot = jnp.sum(sq, axis=1, keepdims=True)
    lft = jnp.sum(jnp.where(left, sq, 0.0), axis=1, keepdims=True)
    ms = jnp.where(left, lft, tot - lft) * (1.0 / N_DIM)
    return x * lax.rsqrt(ms + EPS) * w_row


def _rope(x, cs, s1, s2):
    return x * cs + pltpu.roll(x, ROPE_HALF, 1) * s1 + pltpu.roll(x, LANES - ROPE_HALF, 1) * s2


def _dup_halves(x):
    lane = lax.broadcasted_iota(jnp.int32, x.shape, 1)
    sw = pltpu.roll(x, N_DIM, 1)
    left = lane < N_DIM
    return jnp.where(left, x, sw), jnp.where(left, sw, x)


def _nsa_prep_kernel(q_ref, kv_ref, kw_ref, sm_ref, cs_ref, s1_ref, s2_ref, qn_ref, kn_ref,
                     qt_ref, ka_ref, kwd_ref, vst_ref, vwt_ref, kc_ref, vc_ref, gtt_ref,
                     *, tp, seq):
    cs, s1, s2 = cs_ref[...], s1_ref[...], s2_ref[...]
    for j in range(N_W // LANES):
        sl = slice(j * LANES, (j + 1) * LANES)
        qn = _group_rms(q_ref[:, sl].astype(F32), qn_ref[...])
        qt_ref[sl, :] = (_rope(qn, cs, s1, s2) * (N_DIM ** -0.5 * LOG2E)).T.astype(BF16)
    kv = kv_ref[...].astype(F32)
    kc_ref[...] = kv[:, 0:LANES]
    vc_ref[...] = kv[:, LANES:2 * LANES]
    ks = _rope(_group_rms(kv[:, 2 * LANES:3 * LANES], kn_ref[1:2, :]), cs, s1, s2)
    k0, k1 = _dup_halves(ks)
    pos0 = (pl.program_id(0) * tp) % seq
    pos = pos0 + lax.broadcasted_iota(jnp.int32, (tp, LANES), 0)
    blk = lax.broadcasted_iota(jnp.int32, (tp, LANES), 1)
    onehot = jnp.where(jnp.right_shift(pos, SEL_SHIFT) == blk, 1.0, 0.0).astype(BF16)
    ka_ref[:, 0:LANES] = k0.astype(BF16)
    ka_ref[:, LANES:2 * LANES] = onehot
    ka_ref[:, 2 * LANES:3 * LANES] = k1.astype(BF16)
    ka_ref[:, 3 * LANES:4 * LANES] = onehot
    vst = kv[:, 3 * LANES:4 * LANES].T
    fill = (lax.broadcasted_iota(jnp.int32, (V_ROWS - N_DIM, Q_BLOCK), 0) == 0).astype(BF16)
    for c in range(tp // Q_BLOCK):
        for g in range(N_GROUPS):
            vst_ref[c, g * V_ROWS:g * V_ROWS + N_DIM, :] = (
                vst[g * N_DIM:(g + 1) * N_DIM, c * Q_BLOCK:(c + 1) * Q_BLOCK].astype(BF16))
            vst_ref[c, g * V_ROWS + N_DIM:(g + 1) * V_ROWS, :] = fill
    kwb = kw_ref[...].astype(F32)
    kw = _rope(_group_rms(kwb[:, 0:LANES], kn_ref[2:3, :]), cs, s1, s2)
    w0, w1 = _dup_halves(kw)
    kwd_ref[:, 0:LANES] = w0.astype(BF16)
    kwd_ref[:, LANES:2 * LANES] = w1.astype(BF16)
    vwt_ref[...] = kwb[:, LANES:2 * LANES].T.astype(BF16)
    gtt_ref[...] = _sigmoid(sm_ref[...]).T


def _nsa_prep(proj, small, cs, s1, s2, qn_row, kn_rows, batch, seq):
    tp = KV_TILE
    t = batch * seq
    nsb = seq // tp
    rb = lambda w, off: pl.BlockSpec((tp, w), lambda i, off=off, w=w: (i, off // w))
    tab = pl.BlockSpec((tp, LANES), lambda i: (i % nsb, 0))
    o = lambda w: pl.BlockSpec((tp, w), lambda i: (i, 0))
    ot = lambda r: pl.BlockSpec((r, tp), lambda i: (0, i))
    sds = lambda w, dt: jax.ShapeDtypeStruct((t, w), dt)
    return pl.pallas_call(
        functools.partial(_nsa_prep_kernel, tp=tp, seq=seq),
        grid=(t // tp,),
        in_specs=[rb(512, C_NQ), rb(512, C_KC), rb(512, C_KW), o(LANES), tab, tab, tab,
                  pl.BlockSpec((1, LANES), lambda i: (0, 0)), pl.BlockSpec((3, LANES), lambda i: (0, 0))],
        out_specs=[ot(N_W), o(512), o(256),
                   pl.BlockSpec((tp // Q_BLOCK, N_GROUPS * V_ROWS, Q_BLOCK), lambda i: (i, 0, 0)),
                   ot(LANES), o(128), o(128), ot(LANES)],
        out_shape=[jax.ShapeDtypeStruct((N_W, t), BF16), sds(512, BF16), sds(256, BF16),
                   jax.ShapeDtypeStruct((t // Q_BLOCK, N_GROUPS * V_ROWS, Q_BLOCK), BF16),
                   jax.ShapeDtypeStruct((LANES, t), BF16),
                   sds(128, F32), sds(128, F32), jax.ShapeDtypeStruct((LANES, t), F32)],
        compiler_params=_cparams("parallel"),
        name="nsa_prep",
    )(proj, proj, proj, small, cs, s1, s2, qn_row, kn_rows)


def _gelu_tanh(x):
    return 0.5 * x * (1.0 + jnp.tanh(0.7978845608028654 * (x + 0.044715 * x * x * x)))


def _compress_mlp(x_ref, pa_ref, pb_ref, wa_ref, wb_ref, w2_ref, hb_ref):
    n = x_ref.shape[0]
    x = x_ref[...]
    ha = _dot((x + pa_ref[...]).astype(BF16), wa_ref[...])
    hb_ref[0:n, :] = _dot((x + pb_ref[...]).astype(BF16), wb_ref[...])
    hb_ref[n:n + SUBLANES, :] = jnp.zeros((SUBLANES, hb_ref.shape[1]), F32)
    hid = ha + hb_ref[1:n + 1, :]
    return _dot(_gelu_tanh(hid).astype(BF16), w2_ref[...])


def _nsa_compress_kernel(kx_ref, vx_ref, kpa_ref, kpb_ref, vpa_ref, vpb_ref, kwa_ref, kwb_ref, kw2_ref,
                         vwa_ref, vwb_ref, vw2_ref, cs_ref, s1_ref, s2_ref, kn_ref,
                         kcd_ref, vct_ref, hb_ref):
    n = kx_ref.shape[0]
    rowi = lax.broadcasted_iota(jnp.int32, (n, LANES), 0)
    live = rowi < n - 1
    kc = _compress_mlp(kx_ref, kpa_ref, kpb_ref, kwa_ref, kwb_ref, kw2_ref, hb_ref)
    kc = _rope(_group_rms(kc, kn_ref[0:1, :]), cs_ref[...], s1_ref[...], s2_ref[...])
    kc = jnp.where(live, kc, 0.0)
    c0, c1 = _dup_halves(kc)
    kcd_ref[:, 0:LANES] = c0.astype(BF16)
    kcd_ref[:, LANES:2 * LANES] = c1.astype(BF16)
    vc = _compress_mlp(vx_ref, vpa_ref, vpb_ref, vwa_ref, vwb_ref, vw2_ref, hb_ref)
    vc = jnp.where(live, vc, 0.0)
    vct_ref[0] = vc.T.astype(BF16)


def _nsa_compress(kx, vx, kp, vp, cs, s1, s2, kn_rows, batch, nrow):
    wide = CMP_STRIDE * LANES
    full = lambda a: pl.BlockSpec(a.shape, lambda b: (0,) * a.ndim)
    xs = pl.BlockSpec((nrow, wide), lambda b: (b, 0))
    args = (kx, vx, kp[0], kp[1], vp[0], vp[1], kp[2], kp[3], kp[4], vp[2], vp[3], vp[4], cs, s1, s2, kn_rows)
    return pl.pallas_call(
        _nsa_compress_kernel,
        grid=(batch,),
        in_specs=[xs, xs] + [full(a) for a in args[2:]],
        out_specs=[pl.BlockSpec((nrow, 2 * LANES), lambda b: (b, 0)),
                   pl.BlockSpec((1, LANES, nrow), lambda b: (b, 0, 0))],
        out_shape=[jax.ShapeDtypeStruct((batch * nrow, 2 * LANES), BF16),
                   jax.ShapeDtypeStruct((batch, LANES, nrow), BF16)],
        scratch_shapes=[pltpu.VMEM((nrow + SUBLANES, 2 * CMP_HIDDEN), F32)],
        compiler_params=_cparams("parallel"),
        name="nsa_compress",
    )(*args)


def _compress_params(pos, w1, w2):
    def expand(w1_half, pos_half):
        w = w1_half.reshape(CMP_STRIDE, N_DIM, CMP_HIDDEN)
        z = jnp.zeros_like(w)
        g0 = jnp.concatenate([w, z], axis=-1)
        g1 = jnp.concatenate([z, w], axis=-1)
        wx = jnp.stack([g0, g1], axis=1).reshape(CMP_STRIDE * LANES, 2 * CMP_HIDDEN)
        px = jnp.concatenate([pos_half, pos_half], axis=-1).reshape(1, CMP_STRIDE * LANES)
        return wx.astype(BF16), px.astype(F32)

    half = CMP_STRIDE * N_DIM
    wa, pa = expand(w1[:half], pos[:CMP_STRIDE])
    wb, pb = expand(w1[half:], pos[CMP_STRIDE:])
    z2 = jnp.zeros_like(w2)
    w2x = jnp.concatenate([jnp.concatenate([w2, z2], axis=1), jnp.concatenate([z2, w2], axis=1)], axis=0)
    return pa, pb, wa, wb, w2x.astype(BF16)


def _nsa_attn_kernel(qt_ref, gtt_ref, z_ref, ka_ref, vst_ref,
                     kw0, kw1, kw2, kw3, kw4, wt0, wt1, wt2, wt3, wt4,
                     kcd_ref, vct_ref, ovt_ref, out_ref, qa_ref, acc_ref, *, ncmp):
    QB = Q_BLOCK
    NP = N_HEADS // 2
    PW = 2 * QB
    i = pl.program_id(1)
    t0 = i * QB
    kw_refs = (kw0, kw1, kw2, kw3, kw4)
    vwt_refs = (wt0, wt1, wt2, wt3, wt4)
    nwt = len(kw_refs)

    sub = lax.broadcasted_iota(jnp.int32, (QB, QB), 0)
    lane = lax.broadcasted_iota(jnp.int32, (QB, QB), 1)
    top = sub < N_DIM
    t_row = t0 + lax.broadcasted_iota(jnp.int32, (1, QB), 1)
    t_row2 = jnp.concatenate([t_row, t_row], axis=1)

    def pair_q(pr):
        slab = qt_ref[pr * LANES:(pr + 1) * LANES, :]
        zero = jnp.zeros_like(slab)
        return jnp.concatenate([jnp.where(top, slab, zero), jnp.where(top, zero, slab)], axis=1)

    blk_f = sub.astype(F32)
    cur = jnp.right_shift(t_row, SEL_SHIFT)
    grp = lambda pr: pr // (N_HPG // 2)
    for pr in range(NP):
        qa_ref[pr, 0:LANES, :] = pair_q(pr)

    def compressed(n):
        nidx = lax.broadcasted_iota(jnp.int32, (n, PW), 0)
        ok = (nidx * CMP_STRIDE + (CMP_LEN - 1) <= t_row2) & (nidx < ncmp - 1)
        any_ok = t_row2 >= CMP_LEN - 1
        s_cmp = [_dot(kcd_ref[0:n, grp(pr) * LANES:(grp(pr) + 1) * LANES], qa_ref[pr, 0:LANES, :])
                 for pr in range(NP)]
        p_cmp = []
        for pr in range(NP):
            s = jnp.where(ok, s_cmp[pr], NEG)
            mx = jnp.max(s, axis=0, keepdims=True)
            p = jnp.exp2(s - mx)
            den = jnp.sum(p, axis=0, keepdims=True)
            p_cmp.append(p * jnp.where(any_ok, 1.0 / den, 0.0))
        outs = [_dot(vct_ref[0, grp(pr) * N_DIM:(grp(pr) + 1) * N_DIM, 0:n], p_cmp[pr].astype(BF16))
                for pr in range(NP)]
        for g in range(N_GROUPS):
            psum = jnp.zeros((n, QB), F32)
            for pp in range(N_HPG // 2):
                p = p_cmp[g * (N_HPG // 2) + pp]
                psum = psum + p[:, 0:QB] + p[:, QB:PW]
            p_hi = psum.astype(BF16)
            p_lo = (psum - p_hi.astype(F32)).astype(BF16)
            outs.append(_dot(ovt_ref[:, 0:n], p_hi) + _dot(ovt_ref[:, 0:n], p_lo))
        return tuple(outs)

    n_opts = list(range(CMP_CHUNK, ncmp + 1, CMP_CHUNK))
    need = (t0 + QB - CMP_LEN) // CMP_STRIDE + 1
    outs = lax.switch(jnp.clip((need - 1) // CMP_CHUNK, 0, len(n_opts) - 1),
                      [functools.partial(compressed, n) for n in n_opts])
    o_cmp, imps = outs[:NP], outs[NP:]
    forced = (sub == 0) | (sub == cur) | (sub == cur - 1)
    n_rounds = SEL_TOPK - 3
    valid = sub <= cur
    works0 = [jnp.where(valid, jnp.where(forced, -jnp.inf, imp), NEG) for imp in imps]

    def rounds(works, exact):
        sels = [forced for _ in range(N_GROUPS)]
        for _ in range(n_rounds):
            for g in range(N_GROUPS):
                mx = jnp.max(works[g], axis=0, keepdims=True)
                if exact:
                    idx = jnp.min(jnp.where(works[g] == mx, blk_f, float(LANES)), axis=0, keepdims=True)
                    pick = blk_f == idx
                else:
                    pick = works[g] == mx
                sels[g] = sels[g] | pick
                works[g] = jnp.where(pick, -jnp.inf, works[g])
        return tuple(jnp.where(s & valid, 0.0, NEG) for s in sels)

    bias = rounds(list(works0), exact=False)
    picked = [jnp.sum(jnp.where((b == 0.0) & jnp.logical_not(forced), 1.0, 0.0), axis=0, keepdims=True)
              for b in bias]
    tied = jnp.max(functools.reduce(jnp.maximum, picked)) > n_rounds
    bias = lax.cond(tied, lambda: rounds(list(works0), exact=True), lambda: bias)
    for g in range(N_GROUPS):
        neg = bias[g].astype(BF16)
        neg2 = jnp.concatenate([neg, neg], axis=1)
        for pp in range(N_HPG // 2):
            qa_ref[g * (N_HPG // 2) + pp, LANES:2 * LANES, :] = neg2

    CH = QB
    above = jnp.concatenate([sub > lane, sub > lane], axis=1)

    def sel_scores(cj):
        k0 = pl.multiple_of(cj * CH, CH)
        return [_dot(ka_ref[pl.ds(k0, CH), 2 * grp(pr) * LANES:2 * (grp(pr) + 1) * LANES], qa_ref[pr])
                for pr in range(NP)]

    def sel_softmax(ss, ms, masked):
        new, alphas, ps = [], [], []
        for pr in range(NP):
            s = jnp.where(above, NEG, ss[pr]) if masked else ss[pr]
            m_new = jnp.maximum(ms[pr], jnp.max(s, axis=0, keepdims=True))
            alphas.append(jnp.exp2(ms[pr] - m_new))
            ps.append(jnp.exp2(s - m_new).astype(BF16))
            new.append(m_new)
        return tuple(new), alphas, ps

    def sel_values(cj, alphas, ps):
        for pr in range(NP):
            g = grp(pr)
            acc_ref[pr] = alphas[pr] * acc_ref[pr] + _dot(vst_ref[cj, g * V_ROWS:(g + 1) * V_ROWS, :], ps[pr])

    def sel_chunk(cj, ms, masked):
        ms, alphas, ps = sel_softmax(sel_scores(cj), ms, masked)
        sel_values(cj, alphas, ps)
        return ms

    def sel_tile(jt, ms, nc):
        c0 = jt * nc
        k0 = pl.multiple_of(c0 * CH, CH)
        big = [_dot(ka_ref[pl.ds(k0, nc * CH), 2 * grp(pr) * LANES:2 * (grp(pr) + 1) * LANES], qa_ref[pr])
               for pr in range(NP)]
        for c in range(nc):
            ss = [b[c * CH:(c + 1) * CH] for b in big]
            ms, alphas, ps = sel_softmax(ss, ms, False)
            sel_values(c0 + c, alphas, ps)
        return ms

    acc_ref[...] = jnp.zeros_like(acc_ref)
    ms = tuple(jnp.full((1, PW), 2.0 * NEG, F32) for _ in range(NP))
    done = 0
    for nc in (4 * KV_TILE // CH, 2 * KV_TILE // CH, KV_TILE // CH):
        n_run = (i - done) // nc
        ms = lax.fori_loop(done // nc, done // nc + n_run, functools.partial(sel_tile, nc=nc), ms)
        done = done + n_run * nc
    ms = lax.fori_loop(done, i, lambda cj, ms: sel_chunk(cj, ms, False), ms)
    s_diag = sel_scores(i)
    s_win = [[_dot(kw_refs[c][:, grp(pr) * LANES:(grp(pr) + 1) * LANES], qa_ref[pr, 0:LANES, :])
              for c in range(nwt)] for pr in range(NP)]
    ms, alphas, ps_diag = sel_softmax(s_diag, ms, True)
    p_win, d_win = [], []
    for pr in range(NP):
        ss = []
        for c in range(nwt):
            ok = (i - (nwt - 1) + c) >= 0
            if c == 0:
                ok = ok & above
            elif c == nwt - 1:
                ok = ok & jnp.logical_not(above)
            ss.append(jnp.where(ok, s_win[pr][c], NEG))
        mx = ss[0].max(axis=0, keepdims=True)
        for c in range(1, nwt):
            mx = jnp.maximum(mx, ss[c].max(axis=0, keepdims=True))
        ps = [jnp.exp2(ss[c] - mx) for c in range(nwt)]
        den = ps[0].sum(axis=0, keepdims=True)
        for c in range(1, nwt):
            den = den + ps[c].sum(axis=0, keepdims=True)
        p_win.append([p.astype(BF16) for p in ps])
        d_win.append(den)
    sel_values(i, alphas, ps_diag)
    o_win = []
    for pr in range(NP):
        g = grp(pr)
        acc = _dot(vwt_refs[0][g * N_DIM:(g + 1) * N_DIM, :], p_win[pr][0])
        for c in range(1, nwt):
            acc = acc + _dot(vwt_refs[c][g * N_DIM:(g + 1) * N_DIM, :], p_win[pr][c])
        o_win.append(acc * (1.0 / d_win[pr]))
    o_sel = [acc_ref[pr, 0:N_DIM, :] * (1.0 / acc_ref[pr, N_DIM:N_DIM + 1, :]) for pr in range(NP)]

    for pr in range(NP):
        halves = []
        for par in range(2):
            h = 2 * pr + par
            cs = slice(par * QB, (par + 1) * QB)
            gc = gtt_ref[SM_G + 3 * h:SM_G + 3 * h + 1, :]
            gs = gtt_ref[SM_G + 3 * h + 1:SM_G + 3 * h + 2, :]
            gw = gtt_ref[SM_G + 3 * h + 2:SM_G + 3 * h + 3, :]
            halves.append(gc * o_cmp[pr][:, cs] + gs * o_sel[pr][:, cs] + gw * o_win[pr][:, cs])
        y = jnp.concatenate(halves, axis=0).T
        sl = slice(pr * LANES, (pr + 1) * LANES)
        out_ref[:, sl] = (y * _silu(z_ref[:, sl].astype(F32))).astype(out_ref.dtype)


def _nsa_attn(proj, qt, gtt, ka, vst, kwd, vwt, kcd, vct, ovt, batch, seq):
    QB = Q_BLOCK
    nqb = seq // QB
    ncmp = seq // CMP_STRIDE
    ntile = seq // KV_TILE
    nwt = WINDOW // QB + 1
    row = lambda w: pl.BlockSpec((QB, w), lambda b, i: (b * nqb + i, 0))
    colb = lambda r: pl.BlockSpec((r, QB), lambda b, i: (0, b * nqb + i))
    wblk = lambda b, i, c: b * nqb + jnp.maximum(i - (nwt - 1) + c, 0)
    in_specs = ([colb(N_W), colb(LANES), pl.BlockSpec((QB, N_W), lambda b, i: (b * nqb + i, C_NZ // N_W)),
                 pl.BlockSpec((seq, 4 * LANES), lambda b, i: (b, 0)),
                 pl.BlockSpec((nqb, N_GROUPS * V_ROWS, QB), lambda b, i: (b, 0, 0))]
                + [pl.BlockSpec((QB, 2 * LANES), lambda b, i, c=c: (wblk(b, i, c), 0)) for c in range(nwt)]
                + [pl.BlockSpec((LANES, QB), lambda b, i, c=c: (0, wblk(b, i, c))) for c in range(nwt)]
                + [pl.BlockSpec((ncmp, 2 * LANES), lambda b, i: (b, 0)),
                   pl.BlockSpec((1, LANES, ncmp), lambda b, i: (b, 0, 0)),
                   pl.BlockSpec((LANES, ncmp), lambda b, i: (0, 0))])
    return pl.pallas_call(
        functools.partial(_nsa_attn_kernel, ncmp=ncmp),
        grid=(batch, nqb),
        in_specs=in_specs,
        out_specs=row(N_W),
        out_shape=jax.ShapeDtypeStruct((batch * seq, N_W), BF16),
        scratch_shapes=[pltpu.VMEM((N_HEADS // 2, 2 * LANES, 2 * QB), BF16),
                        pltpu.VMEM((N_HEADS // 2, V_ROWS, 2 * QB), F32)],
        compiler_params=_cparams("parallel", "arbitrary"),
        name="nsa_attn",
    )(qt, gtt, proj, ka, vst, *([kwd] * nwt), *([vwt] * nwt), kcd, vct, ovt)


def _hgrn_kernel(f_ref, i_ref, q_ref, z_ref, lb_ref, hn_ref, out_ref, st_ref):
    T, D = H_TILE, H_DIM
    W = H_W

    @pl.when(pl.program_id(1) == 0)
    def _():
        st_ref[...] = jnp.zeros_like(st_ref)

    lb = lb_ref[...]
    fp = f_ref[...]
    e = jnp.exp(-jnp.abs(fp))
    r = 1.0 / (1.0 + e)
    sig = jnp.where(fp >= 0, r, e * r)
    nsig = jnp.where(fp >= 0, e * r, r)
    logf2 = jnp.log(lb + (1.0 - lb) * sig) * LOG2E
    kk = (1.0 - lb) * nsig

    row = lax.broadcasted_iota(jnp.int32, (T, T), 0)
    col = lax.broadcasted_iota(jnp.int32, (T, T), 1)
    tri01 = jnp.where(col <= row, 1.0, 0.0).astype(BF16)
    bcum = _dot01(tri01, logf2)
    rowi = lax.broadcasted_iota(jnp.int32, (T, 1), 0)

    segs = []
    sg = 2 * H_LAG
    while sg <= T:
        if sg == T:
            bmid = jnp.broadcast_to(bcum[sg // 2 - 1:sg // 2, :], (T, W))
        else:
            b3 = bcum.reshape(T // sg, sg, W)
            bmid = jnp.broadcast_to(b3[:, sg // 2 - 1:sg // 2, :], (T // sg, sg, W)).reshape(T, W)
        x = bcum - bmid
        shift = sg.bit_length() - 1
        pair_ok = ((jnp.right_shift(row, shift) == jnp.right_shift(col, shift))
                   & (jnp.bitwise_and(row, sg - 1) >= sg // 2) & (jnp.bitwise_and(col, sg - 1) < sg // 2))
        segs.append((jnp.exp2(jnp.minimum(x, -x)), pair_ok))
        sg *= 2

    sub3 = lax.broadcasted_iota(jnp.int32, (T // SUBLANES, SUBLANES, 1), 1)
    lag_ok = [jnp.bitwise_and(sub3, H_LAG - 1) >= d for d in range(H_LAG)]
    b_end = bcum[T - 1:T, :]
    v_all = i_ref[...].astype(F32)
    def head_scores(h):
        hs = slice(h * D, (h + 1) * D)
        q = _silu(q_ref[:, hs].astype(F32))
        k = kk[:, hs]
        b = bcum[:, hs]
        v = v_all[:, hs]
        vb = v.astype(BF16)
        a_seg = []
        for fac, _ in segs:
            f = fac[:, hs]
            a_seg.append(_dot_nt((q * f).astype(BF16), (k * f).astype(BF16)))
        st = st_ref[h]
        o_state = _dot_nt((q * jnp.exp2(b)).astype(BF16), st.astype(BF16))
        kt = (k * jnp.exp2(b_end[:, hs] - b)).astype(BF16)
        st_ref[h] = st * jnp.exp2(b_end[:, hs]) + _dot_tn(vb, kt)
        q3, k3, b3, v3 = (a.reshape(T // SUBLANES, SUBLANES, D) for a in (q, k, b, v))
        o3 = jnp.sum(q3 * k3, axis=2, keepdims=True) * v3
        for d in range(1, H_LAG):
            ks, bs, vs = (pltpu.roll(a, d, 1) for a in (k3, b3, v3))
            a = jnp.sum(q3 * ks * jnp.exp2(b3 - bs), axis=2, keepdims=True)
            o3 = o3 + jnp.where(lag_ok[d], a, 0.0) * vs
        return hs, a_seg, o3.reshape(T, D) + o_state, vb

    def head_finish(hs, a_seg, o_part, vb):
        a_tot = jnp.zeros((T, T), F32)
        for (_, pair_ok), a in zip(segs, a_seg):
            a_tot = jnp.where(pair_ok, a, a_tot)
        oh = o_part + _dot(a_tot.astype(BF16), vb)
        on = oh * lax.rsqrt(jnp.mean(oh * oh, axis=1, keepdims=True) + EPS) * hn_ref[:, hs]
        out_ref[:, hs] = (on * _silu(z_ref[:, hs].astype(F32))).astype(out_ref.dtype)

    pending = None
    for h in range(H_HEADS):
        cur = head_scores(h)
        if pending is not None:
            head_finish(*pending)
        pending = cur
    head_finish(*pending)


def _hgrn(proj, fgate, lb_row, hn_row, batch, seq):
    T = H_TILE
    nt = seq // T
    cb = lambda j: pl.BlockSpec((T, H_W), lambda b, c, j=j: (b * nt + c, j))
    vec = pl.BlockSpec((1, H_W), lambda b, c: (0, 0))
    return pl.pallas_call(
        _hgrn_kernel,
        grid=(batch, nt),
        in_specs=[pl.BlockSpec((T, H_W), lambda b, c: (b * nt + c, 0)), cb(1), cb(2), cb(3), vec, vec],
        out_specs=pl.BlockSpec((T, H_W), lambda b, c: (b * nt + c, 0)),
        out_shape=jax.ShapeDtypeStruct((batch * seq, H_W), BF16),
        scratch_shapes=[pltpu.VMEM((H_HEADS, H_DIM, H_DIM), F32)],
        compiler_params=_cparams("parallel", "arbitrary"),
        name="hgrn2",
    )(fgate, proj, proj, proj, lb_row, hn_row)


def _even_relayout(a):
    nb = 5 * M_W + 2 * M_HEADS
    g0 = nb + N_W + 6 * N_KV_W
    col = lambda lo, n: lax.slice_in_dim(a, lo, lo + n, axis=-1)
    pad = jnp.zeros(a.shape[:-1] + (C_NZ - C_SMALL - SM_G - 3 * N_HEADS,), a.dtype)
    out = jnp.concatenate([
        col(0, 5 * M_W),
        col(nb, N_W + 6 * N_KV_W),
        col(5 * M_W, 2 * M_HEADS),
        col(g0, 3 * N_HEADS), pad,
        col(g0 + 3 * N_HEADS, N_W),
    ], axis=-1)
    assert out.shape[-1] == EV_COLS
    return out


def _rope_tables(positions):
    inv_freq = ROPE_THETA ** (-np.arange(ROPE_HALF, dtype=np.float64) / ROPE_HALF)
    ang = positions.astype(np.float64)[:, None] * inv_freq[None, :]
    cos, sin = np.cos(ang).astype(np.float32), np.sin(ang).astype(np.float32)
    n = positions.shape[0]
    z8 = np.zeros((n, ROPE_HALF), np.float32)
    rest1 = np.ones((n, N_DIM - ROPE_DIM), np.float32)
    rest0 = np.zeros((n, N_DIM - ROPE_DIM), np.float32)
    cs = np.concatenate([cos, cos, rest1], axis=1)
    s1 = np.concatenate([z8, sin, rest0], axis=1)
    s2 = np.concatenate([-sin, z8, rest0], axis=1)
    two = lambda a: np.concatenate([a, a], axis=1)
    return two(cs), two(s1), two(s2)


def _overlap_matrix(seq):
    ncmp = seq // CMP_STRIDE
    n_sel = seq // SEL_BLOCK
    cs = np.arange(ncmp) * CMP_STRIDE
    ss = np.arange(n_sel) * SEL_BLOCK
    ov = np.clip(np.minimum(cs[:, None] + CMP_LEN, ss[None, :] + SEL_BLOCK)
                 - np.maximum(cs[:, None], ss[None, :]), 0, None).astype(np.float32)
    ov[ncmp - 1, :] = 0.0
    return np.pad(ov, ((0, 0), (0, LANES - n_sel)))


def _even_layer(x2d, res, norm_w, w_in, b_in, f_bias, conv_w, conv_b, head_norm, q_norm, k_norm,
                ck_pos, ck_w1, ck_w2, cv_pos, cv_w1, cv_w2, w_out, batch, seq):
    assert seq // SEL_BLOCK <= LANES and seq % KV_TILE == 0
    outs = _norm_proj(x2d, norm_w, _even_relayout(w_in.astype(BF16)), _even_relayout(b_in),
                      (C_SMALL, C_SMALL + LANES), tn=EV_COLS // 3, res=res)
    if res:
        x2d = outs[0]
    proj, small = outs[-2:]

    f_bias_row = jnp.zeros((1, LANES), F32).at[0, SM_F:SM_F + M_HEADS].set(f_bias)
    ya = _mlstm(proj, small, conv_w, conv_b, f_bias_row, head_norm, batch, seq)

    cs, s1, s2 = (jnp.asarray(a) for a in _rope_tables(np.arange(seq)))
    qn_row = jnp.concatenate([q_norm, q_norm]).reshape(1, LANES)
    kn_rows = jnp.concatenate([k_norm, k_norm], axis=1)
    qt, ka, kwd, vst, vwt, kc, vc, gtt = _nsa_prep(proj, small, cs, s1, s2, qn_row, kn_rows, batch, seq)

    ncmp = seq // CMP_STRIDE
    ccs, cs1, cs2 = (jnp.asarray(a) for a in _rope_tables(np.arange(ncmp) * CMP_STRIDE + CMP_LEN - 1))
    kcd, vct = _nsa_compress(
        kc.reshape(batch * ncmp, CMP_STRIDE * LANES), vc.reshape(batch * ncmp, CMP_STRIDE * LANES),
        _compress_params(ck_pos, ck_w1, ck_w2), _compress_params(cv_pos, cv_w1, cv_w2),
        ccs, cs1, cs2, kn_rows, batch, ncmp)

    ovt = jnp.asarray(_overlap_matrix(seq).T).astype(BF16)
    yb = _nsa_attn(proj, qt, gtt, ka, vst, kwd, vwt, kcd, vct, ovt, batch, seq)

    w_o = w_out.astype(BF16)
    return x2d, [(ya, w_o[:M_W]), (yb, w_o[M_W:])]


def _odd_layer(x2d, res, norm_w, w_in, b_in, lb, head_norm, w_out, batch, seq):
    outs = _norm_proj(x2d, norm_w, w_in.astype(BF16), b_in, (0, H_W), tn=H_W, res=res)
    if res:
        x2d = outs[0]
    proj, fgate = outs[-2:]
    y = _hgrn(proj, fgate, lb.reshape(1, H_W), head_norm.reshape(1, H_W), batch, seq)
    return x2d, [(y, w_out.astype(BF16))]


def kernel(x, norm_w, ev_w_in, ev_b_in, mlstm_f_bias, mlstm_conv_w, mlstm_conv_b, mlstm_head_norm, nsa_q_norm, nsa_k_norm, cmp_k_pos, cmp_k_w1, cmp_k_w2, cmp_v_pos, cmp_v_w1, cmp_v_w2, ev_w_out, od_w_in, od_b_in, hgrn_lb_logits, hgrn_head_norm, od_w_out):
    batch, seq, d = x.shape
    depth = norm_w.shape[0]
    lbs = jnp.cumsum(jax.nn.softmax(hgrn_lb_logits.astype(F32), axis=0), axis=0)
    x2d = x.reshape(batch * seq, d)
    res = []
    for layer in range(depth):
        j = layer // 2
        if layer % 2 == 0:
            x2d, res = _even_layer(x2d, res, norm_w[layer], ev_w_in[j], ev_b_in[j], mlstm_f_bias[j],
                                   mlstm_conv_w[j], mlstm_conv_b[j], mlstm_head_norm[j], nsa_q_norm[j],
                                   nsa_k_norm[j], cmp_k_pos[j], cmp_k_w1[j], cmp_k_w2[j], cmp_v_pos[j],
                                   cmp_v_w1[j], cmp_v_w2[j], ev_w_out[j], batch, seq)
        else:
            x2d, res = _odd_layer(x2d, res, norm_w[layer], od_w_in[j], od_b_in[j], lbs[j], hgrn_head_norm[j],
                                  od_w_out[j], batch, seq)
    if len(res) == 2:
        x2d = _out_proj2(res[0][0], res[1][0], res[0][1], res[1][1], x2d)
    elif res:
        x2d = _out_proj1(res[0][0], res[0][1], x2d)
    return x2d.reshape(batch, seq, d)
```

```python
import functools

import numpy as np
import jax
import jax.numpy as jnp
from jax import lax
from jax.experimental import pallas as pl
from jax.experimental.pallas import tpu as pltpu

F32 = jnp.float32
BF16 = jnp.bfloat16

LANES = 128
SUBLANES = 8
BF16_SUBLANES = 16
EPS = 1e-6
NEG = -1e30
VMEM_LIMIT_BYTES = 56 * 1024 * 1024

M_HEADS = 4
M_DIM = 128
M_W = M_HEADS * M_DIM
M_CHUNK = 128
CONV_K = 4
N_HEADS = 8
N_GROUPS = 2
N_HPG = N_HEADS // N_GROUPS
N_DIM = 64
N_W = N_HEADS * N_DIM
N_KV_W = N_GROUPS * N_DIM
CMP_LEN = 32
CMP_STRIDE = 16
CMP_HIDDEN = 128
CMP_CHUNK = 128
SEL_BLOCK = 64
SEL_SHIFT = 6
SEL_TOPK = 16
WINDOW = 512
Q_BLOCK = 128
ROPE_THETA = 500000.0
ROPE_DIM = N_DIM // 4
ROPE_HALF = ROPE_DIM // 2
KV_TILE = 512
V_ROWS = 80
LOG2E = 1.4426950408889634
H_HEADS = 8
H_DIM = 128
H_W = H_HEADS * H_DIM
H_TILE = 128
H_LAG = 2

C_MQ, C_MK, C_MV, C_MO, C_MZ = 0, 512, 1024, 1536, 2048
C_NQ = 2560
C_KC, C_VC, C_KS, C_VS = 3072, 3200, 3328, 3456
C_KW, C_VW, C_SMALL = 3584, 3712, 3840
C_NZ = 4096
EV_COLS = 4608
SM_I, SM_F, SM_G = 0, 4, 8


def _cparams(*sem):
    return pltpu.CompilerParams(dimension_semantics=sem, vmem_limit_bytes=VMEM_LIMIT_BYTES)


def _dot(a, b):
    return jnp.dot(a, b, preferred_element_type=F32)


def _dot_nt(a, b):
    return lax.dot_general(a, b, (((1,), (1,)), ((), ())), preferred_element_type=F32)


def _dot_tn(a, b):
    return lax.dot_general(a, b, (((0,), (0,)), ((), ())), preferred_element_type=F32)


def _sigmoid(x):
    return 1.0 / (1.0 + jnp.exp(-x))


def _silu(x):
    return x * _sigmoid(x)


def _split3(x):
    hi = x.astype(BF16)
    r = x - hi.astype(F32)
    mid = r.astype(BF16)
    lo = (r - mid.astype(F32)).astype(BF16)
    return hi, mid, lo


def _dot01(m01, x):
    hi, mid, lo = _split3(x)
    return _dot(m01, hi) + _dot(m01, mid) + _dot(m01, lo)


def _norm_proj_kernel(*refs, n_res, g_tiles, g_off):
    x_ref = refs[0]
    y_refs, wy_refs = refs[1:1 + n_res], refs[1 + n_res:1 + 2 * n_res]
    nw_ref, w_ref, b_ref = refs[1 + 2 * n_res:4 + 2 * n_res]
    outs = refs[4 + 2 * n_res:]
    xo_ref = outs[0] if n_res else None
    o_ref, g_ref, h_ref = outs[-3:]
    j = pl.program_id(1)

    @pl.when(j == 0)
    def _():
        x = x_ref[...]
        for y_ref, wy_ref in zip(y_refs, wy_refs):
            x = x + _dot(y_ref[...], wy_ref[...])
        if n_res:
            xo_ref[...] = x
        ms = jnp.mean(x * x, axis=-1, keepdims=True)
        h_ref[...] = (x * lax.rsqrt(ms + EPS) * nw_ref[...]).astype(BF16)

    acc = _dot(h_ref[...], w_ref[...]) + b_ref[...]
    o_ref[...] = acc.astype(o_ref.dtype)

    @pl.when((j >= g_tiles[0]) & (j < g_tiles[1]))
    def _():
        g_ref[...] = acc[:, g_off:g_off + g_ref.shape[1]]


def _norm_proj(x2d, norm_w, w_bf16, bias, g_cols, tn, res=(), tm=1024):
    t, d = x2d.shape
    n = w_bf16.shape[1]
    g_w = min(g_cols[1] - g_cols[0], tn)
    g_tiles = (g_cols[0] // tn, -(-g_cols[1] // tn))
    row = lambda w: pl.BlockSpec((tm, w), lambda i, j: (i, 0))
    const = lambda a: pl.BlockSpec(a.shape, lambda i, j: (0, 0))
    ys, wys = [y for y, _ in res], [wy for _, wy in res]
    out_specs = [pl.BlockSpec((tm, tn), lambda i, j: (i, j)),
                 pl.BlockSpec((tm, g_w), lambda i, j: (i, jnp.clip(j - g_tiles[0], 0, g_tiles[1] - g_tiles[0] - 1)))]
    out_shape = [jax.ShapeDtypeStruct((t, n), BF16), jax.ShapeDtypeStruct((t, g_cols[1] - g_cols[0]), F32)]
    if res:
        out_specs = [row(d)] + out_specs
        out_shape = [jax.ShapeDtypeStruct((t, d), F32)] + out_shape
    return pl.pallas_call(
        functools.partial(_norm_proj_kernel, n_res=len(res), g_tiles=g_tiles, g_off=g_cols[0] % tn),
        grid=(t // tm, n // tn),
        in_specs=([row(d)] + [row(y.shape[1]) for y in ys] + [const(wy) for wy in wys]
                  + [pl.BlockSpec((1, d), lambda i, j: (0, 0)),
                     pl.BlockSpec((d, tn), lambda i, j: (0, j)),
                     pl.BlockSpec((1, tn), lambda i, j: (0, j))]),
        out_specs=out_specs,
        out_shape=out_shape,
        scratch_shapes=[pltpu.VMEM((tm, d), BF16)],
        compiler_params=_cparams("parallel", "arbitrary"),
        name="norm_proj",
    )(x2d, *ys, *wys, norm_w.reshape(1, d), w_bf16, bias.reshape(1, n))


def _out_proj2_kernel(ya_ref, yb_ref, wa_ref, wb_ref, x_ref, o_ref):
    o_ref[...] = x_ref[...] + _dot(ya_ref[...], wa_ref[...]) + _dot(yb_ref[...], wb_ref[...])


def _out_proj2(ya, yb, wa, wb, x2d, tm=1024):
    t, d = x2d.shape
    ka, kb = ya.shape[1], yb.shape[1]
    return pl.pallas_call(
        _out_proj2_kernel,
        grid=(t // tm,),
        in_specs=[
            pl.BlockSpec((tm, ka), lambda i: (i, 0)),
            pl.BlockSpec((tm, kb), lambda i: (i, 0)),
            pl.BlockSpec((ka, d), lambda i: (0, 0)),
            pl.BlockSpec((kb, d), lambda i: (0, 0)),
            pl.BlockSpec((tm, d), lambda i: (i, 0)),
        ],
        out_specs=pl.BlockSpec((tm, d), lambda i: (i, 0)),
        out_shape=jax.ShapeDtypeStruct((t, d), F32),
        compiler_params=_cparams("parallel"),
        name="out_proj2",
    )(ya, yb, wa, wb, x2d)


def _out_proj1_kernel(y_ref, w_ref, x_ref, o_ref):
    o_ref[...] = x_ref[...] + _dot(y_ref[...], w_ref[...])


def _out_proj1(y, w, x2d, tm=1024):
    t, d = x2d.shape
    k = y.shape[1]
    return pl.pallas_call(
        _out_proj1_kernel,
        grid=(t // tm,),
        in_specs=[
            pl.BlockSpec((tm, k), lambda i: (i, 0)),
            pl.BlockSpec((k, d), lambda i: (0, 0)),
            pl.BlockSpec((tm, d), lambda i: (i, 0)),
        ],
        out_specs=pl.BlockSpec((tm, d), lambda i: (i, 0)),
        out_shape=jax.ShapeDtypeStruct((t, d), F32),
        compiler_params=_cparams("parallel"),
        name="out_proj1",
    )(y, w, x2d)


def _mlstm_kernel(q_ref, k_ref, v_ref, o_ref, z_ref, sm_ref, cw_ref, cb_ref, fb_ref, hn_ref,
                  out_ref, xbuf, c_ref, n_ref, m_ref):
    L, D = M_CHUNK, M_DIM
    NB = q_ref.shape[0]
    XR = xbuf.shape[1]
    X0 = BF16_SUBLANES

    @pl.when(pl.program_id(0) == 0)
    def _():
        xbuf[...] = jnp.zeros_like(xbuf)
        c_ref[...] = jnp.zeros_like(c_ref)
        n_ref[...] = jnp.zeros_like(n_ref)
        m_ref[...] = jnp.zeros_like(m_ref)

    t_i = lax.broadcasted_iota(jnp.int32, (L, XR), 0)
    r_i = lax.broadcasted_iota(jnp.int32, (L, XR), 1)
    row = lax.broadcasted_iota(jnp.int32, (L, L), 0)
    col = lax.broadcasted_iota(jnp.int32, (L, L), 1)
    causal = col <= row
    lane = lax.broadcasted_iota(jnp.int32, (L, LANES), 1)
    is_f = (lane >= SM_F) & (lane < SM_F + M_HEADS)

    heads = []
    for b in range(NB):
        xbuf[b, X0:X0 + L, 0:M_W] = q_ref[b]
        xbuf[b, X0:X0 + L, M_W:2 * M_W] = k_ref[b]
        x2 = xbuf[b]
        acc = cb_ref[...] + cw_ref[CONV_K - 1:CONV_K, :] * x2[X0:X0 + L].astype(F32)
        for d in range(1, CONV_K):
            shift = jnp.where(r_i == t_i + X0 - d, 1.0, 0.0).astype(BF16)
            acc = acc + cw_ref[CONV_K - 1 - d:CONV_K - d, :] * _dot(shift, x2)
        qk = _silu(acc)
        xbuf[b, 0:X0, :] = xbuf[b, L:L + X0, :]

        fpre = sm_ref[b] + fb_ref[...]
        gates = jnp.where(is_f, jnp.minimum(fpre, 0.0) - jnp.log(1.0 + jnp.exp(-jnp.abs(fpre))), fpre)
        bcols = _dot01(jnp.where(causal, 1.0, 0.0).astype(BF16), gates)
        gates_t = gates.T
        bcols_t = bcols.T
        for h in range(M_HEADS):
            hs = slice(h * D, (h + 1) * D)
            q = qk[:, hs]
            k = qk[:, M_W + h * D:M_W + (h + 1) * D] * (D ** -0.5)
            i_col = gates[:, SM_I + h:SM_I + h + 1]
            i_row = gates_t[SM_I + h:SM_I + h + 1, :]
            bcum_col = bcols[:, SM_F + h:SM_F + h + 1]
            bcum_row = bcols_t[SM_F + h:SM_F + h + 1, :]
            m_prev = m_ref[b * M_HEADS + h]
            dmat = jnp.where(causal, bcum_col - bcum_row + i_row, -jnp.inf)
            inter = bcum_col + m_prev
            m_row = jnp.maximum(jnp.max(dmat, axis=1, keepdims=True), inter)
            b_last = bcum_col[L - 1:L, :]
            dl = b_last - bcum_col + i_col
            m_new = jnp.maximum(b_last + m_prev, jnp.max(dl, axis=0, keepdims=True))
            wkk = jnp.exp(dl - m_new) * k
            heads.append(dict(
                b=b, hs=hs, st=b * M_HEADS + h, q=q, qb=q.astype(BF16), kb=k.astype(BF16),
                vb=v_ref[b, :, hs].astype(BF16), wkk=wkk,
                e=jnp.exp(dmat - m_row), w_inter=jnp.exp(inter - m_row), floor=jnp.exp(-m_row),
                decay=jnp.exp(b_last + m_prev - m_new), m_new=m_new))
    s_qk = [_dot_nt(t["qb"], t["kb"]) for t in heads]
    s_qc = [_dot(t["qb"], c_ref[t["st"]].astype(BF16)) for t in heads]
    s_kv = [_dot_tn(t["wkk"].astype(BF16), t["vb"]) for t in heads]
    ones_b = jnp.ones((L, LANES), BF16)
    smats = [(s_qk[j] * t["e"]).astype(BF16) for j, t in enumerate(heads)]
    s_sv = [_dot(smats[j], t["vb"]) for j, t in enumerate(heads)]
    s_ds = [_dot(smats[j], ones_b) for j in range(len(heads))]
    hhs = []
    for j, t in enumerate(heads):
        st = t["st"]
        num = s_sv[j] + t["w_inter"] * s_qc[j]
        den = s_ds[j] + t["w_inter"] * jnp.sum(t["q"] * n_ref[st], axis=1, keepdims=True)
        hhs.append(num / jnp.maximum(jnp.abs(den), t["floor"]))
        c_ref[st] = t["decay"] * c_ref[st] + s_kv[j]
        n_ref[st] = t["decay"] * n_ref[st] + jnp.sum(t["wkk"], axis=0, keepdims=True)
        m_ref[st] = t["m_new"]
    s_ms = [_dot((hh * hh).astype(BF16), ones_b) for hh in hhs]
    for j, t in enumerate(heads):
        b, hs = t["b"], t["hs"]
        hn = hhs[j] * lax.rsqrt(s_ms[j] * (1.0 / D) + EPS) * hn_ref[:, hs]
        gate = _sigmoid(o_ref[b, :, hs].astype(F32)) * _silu(z_ref[b, :, hs].astype(F32))
        out_ref[b, :, hs] = (hn * gate).astype(out_ref.dtype)


def _mlstm(proj, small, conv_w, conv_b, f_bias_row, head_norm, batch, seq):
    L = M_CHUNK
    nc = seq // L
    p3 = proj.reshape(batch, seq, proj.shape[1])
    s3 = small.reshape(batch, seq, LANES)
    cb = lambda off: pl.BlockSpec((batch, L, M_W), lambda c, off=off: (0, c, off // M_W))
    full = lambda r, w: pl.BlockSpec((r, w), lambda c: (0, 0))
    out = pl.pallas_call(
        _mlstm_kernel,
        grid=(nc,),
        in_specs=[
            cb(C_MQ), cb(C_MK), cb(C_MV), cb(C_MO), cb(C_MZ),
            pl.BlockSpec((batch, L, LANES), lambda c: (0, c, 0)),
            full(CONV_K, 2 * M_W), full(1, 2 * M_W), full(1, LANES), full(1, M_W),
        ],
        out_specs=pl.BlockSpec((batch, L, M_W), lambda c: (0, c, 0)),
        out_shape=jax.ShapeDtypeStruct((batch, seq, M_W), BF16),
        scratch_shapes=[
            pltpu.VMEM((batch, 2 * L, 2 * M_W), BF16),
            pltpu.VMEM((batch * M_HEADS, M_DIM, M_DIM), F32),
            pltpu.VMEM((batch * M_HEADS, 1, M_DIM), F32),
            pltpu.VMEM((batch * M_HEADS, 1, 1), F32),
        ],
        compiler_params=_cparams("arbitrary"),
        name="mlstm",
    )(p3, p3, p3, p3, p3, s3, conv_w, conv_b.reshape(1, -1), f_bias_row, head_norm.reshape(1, -1))
    return out.reshape(batch * seq, M_W)


def _group_rms(x, w_row):
    r = lax.broadcasted_iota(jnp.int32, (LANES, LANES), 0)
    c = lax.broadcasted_iota(jnp.int32, (LANES, LANES), 1)
    same_head = jnp.where((r < N_DIM) == (c < N_DIM), 1.0, 0.0).astype(BF16)
    ms = _dot_hilo(x * x, same_head) * (1.0 / N_DIM)
    return x * lax.rsqrt(ms + EPS) * w_row


def _lane_gather01(src_lane, width=LANES):
    r = lax.broadcasted_iota(jnp.int32, (LANES, width), 0)
    c = lax.broadcasted_iota(jnp.int32, (LANES, width), 1)
    return jnp.where(r == src_lane(c), 1.0, 0.0).astype(BF16)


def _dot_hilo(x, m01):
    hi = x.astype(BF16)
    lo = (x - hi.astype(F32)).astype(BF16)
    return _dot(hi, m01) + _dot(lo, m01)


def _rope(x, cs, s1, s2):
    return x * cs + pltpu.roll(x, ROPE_HALF, 1) * s1 + pltpu.roll(x, LANES - ROPE_HALF, 1) * s2


def _dup_halves(x):
    dup = _lane_gather01(lambda c: jnp.where(c < LANES, 0, N_DIM) + (c & (N_DIM - 1)), 2 * LANES)
    both = _dot(x, dup).astype(BF16)
    return both[:, 0:LANES], both[:, LANES:2 * LANES]


def _transpose01(x):
    return _dot_nt(_lane_gather01(lambda c: c), x)


def _nsa_prep_kernel(q_ref, kv_ref, kw_ref, sm_ref, cs_ref, s1_ref, s2_ref, qn_ref, kn_ref,
                     qt_ref, ka_ref, kwd_ref, vst_ref, vwt_ref, kc_ref, vc_ref, gtt_ref,
                     *, tp, seq):
    cs, s1, s2 = cs_ref[...], s1_ref[...], s2_ref[...]
    nq = N_W // LANES
    kv = kv_ref[...].astype(F32)
    kwb = kw_ref[...].astype(F32)
    kc_ref[...] = kv[:, 0:LANES]
    vc_ref[...] = kv[:, LANES:2 * LANES]
    slabs = [q_ref[:, j * LANES:(j + 1) * LANES].astype(F32) for j in range(nq)]
    slabs += [kv[:, 2 * LANES:3 * LANES], kwb[:, 0:LANES]]
    gains = [qn_ref[...]] * nq + [kn_ref[1:2, :], kn_ref[2:3, :]]
    normed = [_group_rms(x, w) for x, w in zip(slabs, gains)]
    roped = [_rope(x, cs, s1, s2) for x in normed]
    qs = [(x * (N_DIM ** -0.5 * LOG2E)).astype(BF16) for x in roped[:nq]]
    k0, k1 = _dup_halves(roped[nq].astype(BF16))
    w0, w1 = _dup_halves(roped[nq + 1].astype(BF16))
    gates = _sigmoid(sm_ref[...])
    g_hi = gates.astype(BF16)
    g_lo = (gates - g_hi.astype(F32)).astype(BF16)
    qts = [_transpose01(x) for x in qs]
    vst = _transpose01(kv_ref[:, 3 * LANES:4 * LANES])
    vwt = _transpose01(kw_ref[:, LANES:2 * LANES])
    gtt_ref[...] = _transpose01(g_hi) + _transpose01(g_lo)
    for j in range(nq):
        qt_ref[j * LANES:(j + 1) * LANES, :] = qts[j].astype(BF16)
    pos0 = (pl.program_id(0) * tp) % seq
    pos = pos0 + lax.broadcasted_iota(jnp.int32, (tp, LANES), 0)
    blk = lax.broadcasted_iota(jnp.int32, (tp, LANES), 1)
    onehot = jnp.where(jnp.right_shift(pos, SEL_SHIFT) == blk, 1.0, 0.0).astype(BF16)
    ka_ref[:, 0:LANES] = k0
    ka_ref[:, LANES:2 * LANES] = onehot
    ka_ref[:, 2 * LANES:3 * LANES] = k1
    ka_ref[:, 3 * LANES:4 * LANES] = onehot
    fill =(lax.broadcasted_iota(jnp.int32, (V_ROWS - N_DIM, Q_BLOCK), 0) == 0).astype(BF16)
    for c in range(tp // Q_BLOCK):
        for g in range(N_GROUPS):
            vst_ref[c, g * V_ROWS:g * V_ROWS + N_DIM, :] = (
                vst[g * N_DIM:(g + 1) * N_DIM, c * Q_BLOCK:(c + 1) * Q_BLOCK].astype(BF16))
            vst_ref[c, g * V_ROWS + N_DIM:(g + 1) * V_ROWS, :] = fill
    kwd_ref[:, 0:LANES] = w0
    kwd_ref[:, LANES:2 * LANES] = w1
    vwt_ref[...] = vwt.astype(BF16)


def _nsa_prep(proj, small, cs, s1, s2, qn_row, kn_rows, batch, seq):
    tp = KV_TILE
    t = batch * seq
    nsb = seq // tp
    rb = lambda w, off: pl.BlockSpec((tp, w), lambda i, off=off, w=w: (i, off // w))
    tab = pl.BlockSpec((tp, LANES), lambda i: (i % nsb, 0))
    o = lambda w: pl.BlockSpec((tp, w), lambda i: (i, 0))
    ot = lambda r: pl.BlockSpec((r, tp), lambda i: (0, i))
    sds = lambda w, dt: jax.ShapeDtypeStruct((t, w), dt)
    return pl.pallas_call(
        functools.partial(_nsa_prep_kernel, tp=tp, seq=seq),
        grid=(t // tp,),
        in_specs=[rb(512, C_NQ), rb(512, C_KC), rb(512, C_KW), o(LANES), tab, tab, tab,
                  pl.BlockSpec((1, LANES), lambda i: (0, 0)), pl.BlockSpec((3, LANES), lambda i: (0, 0))],
        out_specs=[ot(N_W), o(512), o(256),
                   pl.BlockSpec((tp // Q_BLOCK, N_GROUPS * V_ROWS, Q_BLOCK), lambda i: (i, 0, 0)),
                   ot(LANES), o(128), o(128), ot(LANES)],
        out_shape=[jax.ShapeDtypeStruct((N_W, t), BF16), sds(512, BF16), sds(256, BF16),
                   jax.ShapeDtypeStruct((t // Q_BLOCK, N_GROUPS * V_ROWS, Q_BLOCK), BF16),
                   jax.ShapeDtypeStruct((LANES, t), BF16),
                   sds(128, F32), sds(128, F32), jax.ShapeDtypeStruct((LANES, t), F32)],
        compiler_params=_cparams("parallel"),
        name="nsa_prep",
    )(proj, proj, proj, small, cs, s1, s2, qn_row, kn_rows)


def _gelu_tanh(x):
    return 0.5 * x * (1.0 + jnp.tanh(0.7978845608028654 * (x + 0.044715 * x * x * x)))


def _compress_mlp(x_ref, pa_ref, pb_ref, wa_ref, wb_ref, w2_ref, hb_ref):
    n = x_ref.shape[0]
    x = x_ref[...]
    ha = _dot((x + pa_ref[...]).astype(BF16), wa_ref[...])
    hb_ref[0:n, :] = _dot((x + pb_ref[...]).astype(BF16), wb_ref[...])
    hb_ref[n:n + SUBLANES, :] = jnp.zeros((SUBLANES, hb_ref.shape[1]), F32)
    hid = ha + hb_ref[1:n + 1, :]
    return _dot(_gelu_tanh(hid).astype(BF16), w2_ref[...])


def _nsa_compress_kernel(kx_ref, vx_ref, kpa_ref, kpb_ref, vpa_ref, vpb_ref, kwa_ref, kwb_ref, kw2_ref,
                         vwa_ref, vwb_ref, vw2_ref, cs_ref, s1_ref, s2_ref, kn_ref,
                         kcd_ref, vct_ref, hb_ref):
    n = kx_ref.shape[0]
    rowi = lax.broadcasted_iota(jnp.int32, (n, LANES), 0)
    live = rowi < n - 1
    kc = _compress_mlp(kx_ref, kpa_ref, kpb_ref, kwa_ref, kwb_ref, kw2_ref, hb_ref)
    kc = _rope(_group_rms(kc, kn_ref[0:1, :]), cs_ref[...], s1_ref[...], s2_ref[...])
    kc = jnp.where(live, kc, 0.0)
    c0, c1 = _dup_halves(kc.astype(BF16))
    kcd_ref[:, 0:LANES] = c0
    kcd_ref[:, LANES:2 * LANES] = c1
    vc = _compress_mlp(vx_ref, vpa_ref, vpb_ref, vwa_ref, vwb_ref, vw2_ref, hb_ref)
    vc = jnp.where(live, vc, 0.0)
    vct_ref[0] = vc.T.astype(BF16)


def _nsa_compress(kx, vx, kp, vp, cs, s1, s2, kn_rows, batch, nrow):
    wide = CMP_STRIDE * LANES
    full = lambda a: pl.BlockSpec(a.shape, lambda b: (0,) * a.ndim)
    xs = pl.BlockSpec((nrow, wide), lambda b: (b, 0))
    args = (kx, vx, kp[0], kp[1], vp[0], vp[1], kp[2], kp[3], kp[4], vp[2], vp[3], vp[4], cs, s1, s2, kn_rows)
    return pl.pallas_call(
        _nsa_compress_kernel,
        grid=(batch,),
        in_specs=[xs, xs] + [full(a) for a in args[2:]],
        out_specs=[pl.BlockSpec((nrow, 2 * LANES), lambda b: (b, 0)),
                   pl.BlockSpec((1, LANES, nrow), lambda b: (b, 0, 0))],
        out_shape=[jax.ShapeDtypeStruct((batch * nrow, 2 * LANES), BF16),
                   jax.ShapeDtypeStruct((batch, LANES, nrow), BF16)],
        scratch_shapes=[pltpu.VMEM((nrow + SUBLANES, 2 * CMP_HIDDEN), F32)],
        compiler_params=_cparams("parallel"),
        name="nsa_compress",
    )(*args)


def _compress_params(pos, w1, w2):
    def expand(w1_half, pos_half):
        w = w1_half.reshape(CMP_STRIDE, N_DIM, CMP_HIDDEN)
        z = jnp.zeros_like(w)
        g0 = jnp.concatenate([w, z], axis=-1)
        g1 = jnp.concatenate([z, w], axis=-1)
        wx = jnp.stack([g0, g1], axis=1).reshape(CMP_STRIDE * LANES, 2 * CMP_HIDDEN)
        px = jnp.concatenate([pos_half, pos_half], axis=-1).reshape(1, CMP_STRIDE * LANES)
        return wx.astype(BF16), px.astype(F32)

    half = CMP_STRIDE * N_DIM
    wa, pa = expand(w1[:half], pos[:CMP_STRIDE])
    wb, pb = expand(w1[half:], pos[CMP_STRIDE:])
    z2 = jnp.zeros_like(w2)
    w2x = jnp.concatenate([jnp.concatenate([w2, z2], axis=1), jnp.concatenate([z2, w2], axis=1)], axis=0)
    return pa, pb, wa, wb, w2x.astype(BF16)


def _nsa_attn_kernel(qt_ref, gtt_ref, z_ref, ka_ref, vst_ref,
                     kw0, kw1, kw2, kw3, kw4, wt0, wt1, wt2, wt3, wt4,
                     kcd_ref, vct_ref, ovt_ref, out_ref, qa_ref, acc_ref, *, ncmp):
    QB = Q_BLOCK
    NP = N_HEADS // 2
    PW = 2 * QB
    i = pl.program_id(1)
    t0 = i * QB
    kw_refs = (kw0, kw1, kw2, kw3, kw4)
    vwt_refs = (wt0, wt1, wt2, wt3, wt4)
    nwt = len(kw_refs)

    sub = lax.broadcasted_iota(jnp.int32, (QB, QB), 0)
    lane = lax.broadcasted_iota(jnp.int32, (QB, QB), 1)
    top = sub < N_DIM
    t_row = t0 + lax.broadcasted_iota(jnp.int32, (1, QB), 1)
    t_row2 = jnp.concatenate([t_row, t_row], axis=1)

    def pair_q(pr):
        slab = qt_ref[pr * LANES:(pr + 1) * LANES, :]
        zero = jnp.zeros_like(slab)
        return jnp.concatenate([jnp.where(top, slab, zero), jnp.where(top, zero, slab)], axis=1)

    blk_f = sub.astype(F32)
    cur = jnp.right_shift(t_row, SEL_SHIFT)
    grp = lambda pr: pr // (N_HPG // 2)
    for pr in range(NP):
        qa_ref[pr, 0:LANES, :] = pair_q(pr)

    def compressed(n):
        nidx = lax.broadcasted_iota(jnp.int32, (n, PW), 0)
        ok = (nidx * CMP_STRIDE + (CMP_LEN - 1) <= t_row2) & (nidx < ncmp - 1)
        any_ok = t_row2 >= CMP_LEN - 1
        s_cmp = [_dot(kcd_ref[0:n, grp(pr) * LANES:(grp(pr) + 1) * LANES], qa_ref[pr, 0:LANES, :])
                 for pr in range(NP)]
        p_cmp = []
        for pr in range(NP):
            s = jnp.where(ok, s_cmp[pr], NEG)
            mx = jnp.max(s, axis=0, keepdims=True)
            p = jnp.exp2(s - mx)
            den = jnp.sum(p, axis=0, keepdims=True)
            p_cmp.append(p * jnp.where(any_ok, 1.0 / den, 0.0))
        outs = [_dot(vct_ref[0, grp(pr) * N_DIM:(grp(pr) + 1) * N_DIM, 0:n], p_cmp[pr].astype(BF16))
                for pr in range(NP)]
        for g in range(N_GROUPS):
            psum = jnp.zeros((n, QB), F32)
            for pp in range(N_HPG // 2):
                p = p_cmp[g * (N_HPG // 2) + pp]
                psum = psum + p[:, 0:QB] + p[:, QB:PW]
            p_hi = psum.astype(BF16)
            p_lo = (psum - p_hi.astype(F32)).astype(BF16)
            outs.append(_dot(ovt_ref[:, 0:n], p_hi) + _dot(ovt_ref[:, 0:n], p_lo))
        return tuple(outs)

    n_opts = list(range(CMP_CHUNK, ncmp + 1, CMP_CHUNK))
    need = (t0 + QB - CMP_LEN) // CMP_STRIDE + 1
    outs = lax.switch(jnp.clip((need - 1) // CMP_CHUNK, 0, len(n_opts) - 1),
                      [functools.partial(compressed, n) for n in n_opts])
    o_cmp, imps = outs[:NP], outs[NP:]
    forced = (sub == 0) | (sub == cur) | (sub == cur - 1)
    n_rounds = SEL_TOPK - 3
    valid = sub <= cur
    works0 = [jnp.where(valid, jnp.where(forced, -jnp.inf, imp), NEG) for imp in imps]

    def rounds(works, exact):
        sels = [forced for _ in range(N_GROUPS)]
        for _ in range(n_rounds):
            for g in range(N_GROUPS):
                mx = jnp.max(works[g], axis=0, keepdims=True)
                if exact:
                    idx = jnp.min(jnp.where(works[g] == mx, blk_f, float(LANES)), axis=0, keepdims=True)
                    pick = blk_f == idx
                else:
                    pick = works[g] == mx
                sels[g] = sels[g] | pick
                works[g] = jnp.where(pick, -jnp.inf, works[g])
        return tuple(jnp.where(s & valid, 0.0, NEG) for s in sels)

    bias = rounds(list(works0), exact=False)
    picked = [jnp.sum(jnp.where((b == 0.0) & jnp.logical_not(forced), 1.0, 0.0), axis=0, keepdims=True)
              for b in bias]
    tied = jnp.max(functools.reduce(jnp.maximum, picked)) > n_rounds
    bias = lax.cond(tied, lambda: rounds(list(works0), exact=True), lambda: bias)
    for g in range(N_GROUPS):
        neg = bias[g].astype(BF16)
        neg2 = jnp.concatenate([neg, neg], axis=1)
        for pp in range(N_HPG // 2):
            qa_ref[g * (N_HPG // 2) + pp, LANES:2 * LANES, :] = neg2

    CH = QB
    above = jnp.concatenate([sub > lane, sub > lane], axis=1)

    def sel_scores(cj):
        k0 = pl.multiple_of(cj * CH, CH)
        return [_dot(ka_ref[pl.ds(k0, CH), 2 * grp(pr) * LANES:2 * (grp(pr) + 1) * LANES], qa_ref[pr])
                for pr in range(NP)]

    def sel_softmax(ss, ms, masked):
        new, alphas, ps = [], [], []
        for pr in range(NP):
            s = jnp.where(above, NEG, ss[pr]) if masked else ss[pr]
            m_new = jnp.maximum(ms[pr], jnp.max(s, axis=0, keepdims=True))
            alphas.append(jnp.exp2(ms[pr] - m_new))
            ps.append(jnp.exp2(s - m_new).astype(BF16))
            new.append(m_new)
        return tuple(new), alphas, ps

    def sel_values(cj, alphas, ps):
        for pr in range(NP):
            g = grp(pr)
            acc_ref[pr] = alphas[pr] * acc_ref[pr] + _dot(vst_ref[cj, g * V_ROWS:(g + 1) * V_ROWS, :], ps[pr])

    def sel_chunk(cj, ms, masked):
        ms, alphas, ps = sel_softmax(sel_scores(cj), ms, masked)
        sel_values(cj, alphas, ps)
        return ms

    def sel_tile(jt, ms, nc):
        c0 = jt * nc
        k0 = pl.multiple_of(c0 * CH, CH)
        big = [_dot(ka_ref[pl.ds(k0, nc * CH), 2 * grp(pr) * LANES:2 * (grp(pr) + 1) * LANES], qa_ref[pr])
               for pr in range(NP)]
        for c in range(nc):
            ss = [b[c * CH:(c + 1) * CH] for b in big]
            ms, alphas, ps = sel_softmax(ss, ms, False)
            sel_values(c0 + c, alphas, ps)
        return ms

    acc_ref[...] = jnp.zeros_like(acc_ref)
    ms = tuple(jnp.full((1, PW), 2.0 * NEG, F32) for _ in range(NP))
    done = 0
    for nc in (4 * KV_TILE // CH, 2 * KV_TILE // CH, KV_TILE // CH):
        n_run = (i - done) // nc
        ms = lax.fori_loop(done // nc, done // nc + n_run, functools.partial(sel_tile, nc=nc), ms)
        done = done + n_run * nc
    ms = lax.fori_loop(done, i, lambda cj, ms: sel_chunk(cj, ms, False), ms)
    s_diag = sel_scores(i)
    s_win = [[_dot(kw_refs[c][:, grp(pr) * LANES:(grp(pr) + 1) * LANES], qa_ref[pr, 0:LANES, :])
              for c in range(nwt)] for pr in range(NP)]
    ms, alphas, ps_diag = sel_softmax(s_diag, ms, True)
    p_win, d_win = [], []
    for pr in range(NP):
        ss = []
        for c in range(nwt):
            ok = (i - (nwt - 1) + c) >= 0
            if c == 0:
                ok = ok & above
            elif c == nwt - 1:
                ok = ok & jnp.logical_not(above)
            ss.append(jnp.where(ok, s_win[pr][c], NEG))
        mx = ss[0].max(axis=0, keepdims=True)
        for c in range(1, nwt):
            mx = jnp.maximum(mx, ss[c].max(axis=0, keepdims=True))
        ps = [jnp.exp2(ss[c] - mx) for c in range(nwt)]
        den = ps[0].sum(axis=0, keepdims=True)
        for c in range(1, nwt):
            den = den + ps[c].sum(axis=0, keepdims=True)
        p_win.append([p.astype(BF16) for p in ps])
        d_win.append(den)
    sel_values(i, alphas, ps_diag)
    o_win = []
    for pr in range(NP):
        g = grp(pr)
        acc = _dot(vwt_refs[0][g * N_DIM:(g + 1) * N_DIM, :], p_win[pr][0])
        for c in range(1, nwt):
            acc = acc + _dot(vwt_refs[c][g * N_DIM:(g + 1) * N_DIM, :], p_win[pr][c])
        o_win.append(acc * (1.0 / d_win[pr]))
    o_sel = [acc_ref[pr, 0:N_DIM, :] * (1.0 / acc_ref[pr, N_DIM:N_DIM + 1, :]) for pr in range(NP)]

    for pr in range(NP):
        halves = []
        for par in range(2):
            h = 2 * pr + par
            cs = slice(par * QB, (par + 1) * QB)
            gc = gtt_ref[SM_G + 3 * h:SM_G + 3 * h + 1, :]
            gs = gtt_ref[SM_G + 3 * h + 1:SM_G + 3 * h + 2, :]
            gw = gtt_ref[SM_G + 3 * h + 2:SM_G + 3 * h + 3, :]
            halves.append(gc * o_cmp[pr][:, cs] + gs * o_sel[pr][:, cs] + gw * o_win[pr][:, cs])
        y = jnp.concatenate(halves, axis=0).T
        sl = slice(pr * LANES, (pr + 1) * LANES)
        out_ref[:, sl] = (y * _silu(z_ref[:, sl].astype(F32))).astype(out_ref.dtype)


def _nsa_attn(proj, qt, gtt, ka, vst, kwd, vwt, kcd, vct, ovt, batch, seq):
    QB = Q_BLOCK
    nqb = seq // QB
    ncmp = seq // CMP_STRIDE
    ntile = seq // KV_TILE
    nwt = WINDOW // QB + 1
    row = lambda w: pl.BlockSpec((QB, w), lambda b, i: (b * nqb + i, 0))
    colb = lambda r: pl.BlockSpec((r, QB), lambda b, i: (0, b * nqb + i))
    wblk = lambda b, i, c: b * nqb + jnp.maximum(i - (nwt - 1) + c, 0)
    in_specs = ([colb(N_W), colb(LANES), pl.BlockSpec((QB, N_W), lambda b, i: (b * nqb + i, C_NZ // N_W)),
                 pl.BlockSpec((seq, 4 * LANES), lambda b, i: (b, 0)),
                 pl.BlockSpec((nqb, N_GROUPS * V_ROWS, QB), lambda b, i: (b, 0, 0))]
                + [pl.BlockSpec((QB, 2 * LANES), lambda b, i, c=c: (wblk(b, i, c), 0)) for c in range(nwt)]
                + [pl.BlockSpec((LANES, QB), lambda b, i, c=c: (0, wblk(b, i, c))) for c in range(nwt)]
                + [pl.BlockSpec((ncmp, 2 * LANES), lambda b, i: (b, 0)),
                   pl.BlockSpec((1, LANES, ncmp), lambda b, i: (b, 0, 0)),
                   pl.BlockSpec((LANES, ncmp), lambda b, i: (0, 0))])
    return pl.pallas_call(
        functools.partial(_nsa_attn_kernel, ncmp=ncmp),
        grid=(batch, nqb),
        in_specs=in_specs,
        out_specs=row(N_W),
        out_shape=jax.ShapeDtypeStruct((batch * seq, N_W), BF16),
        scratch_shapes=[pltpu.VMEM((N_HEADS // 2, 2 * LANES, 2 * QB), BF16),
                        pltpu.VMEM((N_HEADS // 2, V_ROWS, 2 * QB), F32)],
        compiler_params=_cparams("parallel", "arbitrary"),
        name="nsa_attn",
    )(qt, gtt, proj, ka, vst, *([kwd] * nwt), *([vwt] * nwt), kcd, vct, ovt)


def _hgrn_kernel(f_ref, i_ref, q_ref, z_ref, lb_ref, hn_ref, out_ref, st_ref):
    T, D = H_TILE, H_DIM
    W = H_W

    @pl.when(pl.program_id(1) == 0)
    def _():
        st_ref[...] = jnp.zeros_like(st_ref)

    lb = lb_ref[...]
    fp = f_ref[...]
    e = jnp.exp(-jnp.abs(fp))
    r = 1.0 / (1.0 + e)
    sig = jnp.where(fp >= 0, r, e * r)
    nsig = jnp.where(fp >= 0, e * r, r)
    logf2 = jnp.log(lb + (1.0 - lb) * sig) * LOG2E
    kk = (1.0 - lb) * nsig

    row = lax.broadcasted_iota(jnp.int32, (T, T), 0)
    col = lax.broadcasted_iota(jnp.int32, (T, T), 1)
    tri01 = jnp.where(col <= row, 1.0, 0.0).astype(BF16)
    bcum = _dot01(tri01, logf2)
    rowi = lax.broadcasted_iota(jnp.int32, (T, 1), 0)

    segs = []
    sg = 2 * H_LAG
    while sg <= T:
        if sg == T:
            bmid = jnp.broadcast_to(bcum[sg // 2 - 1:sg // 2, :], (T, W))
        else:
            b3 = bcum.reshape(T // sg, sg, W)
            bmid = jnp.broadcast_to(b3[:, sg // 2 - 1:sg // 2, :], (T // sg, sg, W)).reshape(T, W)
        x = bcum - bmid
        shift = sg.bit_length() - 1
        pair_ok = ((jnp.right_shift(row, shift) == jnp.right_shift(col, shift))
                   & (jnp.bitwise_and(row, sg - 1) >= sg // 2) & (jnp.bitwise_and(col, sg - 1) < sg // 2))
        segs.append((jnp.exp2(jnp.minimum(x, -x)), pair_ok))
        sg *= 2

    sub3 = lax.broadcasted_iota(jnp.int32, (T // SUBLANES, SUBLANES, 1), 1)
    lag_ok = [jnp.bitwise_and(sub3, H_LAG - 1) >= d for d in range(H_LAG)]
    b_end = bcum[T - 1:T, :]
    v_all = i_ref[...].astype(F32)
    def head_scores(h):
        hs = slice(h * D, (h + 1) * D)
        q = _silu(q_ref[:, hs].astype(F32))
        k = kk[:, hs]
        b = bcum[:, hs]
        v = v_all[:, hs]
        vb = v.astype(BF16)
        a_seg = []
        for fac, _ in segs:
            f = fac[:, hs]
            a_seg.append(_dot_nt((q * f).astype(BF16), (k * f).astype(BF16)))
        st = st_ref[h]
        o_state = _dot_nt((q * jnp.exp2(b)).astype(BF16), st.astype(BF16))
        kt = (k * jnp.exp2(b_end[:, hs] - b)).astype(BF16)
        st_ref[h] = st * jnp.exp2(b_end[:, hs]) + _dot_tn(vb, kt)
        q3, k3, b3, v3 = (a.reshape(T // SUBLANES, SUBLANES, D) for a in (q, k, b, v))
        o3 = jnp.sum(q3 * k3, axis=2, keepdims=True) * v3
        for d in range(1, H_LAG):
            ks, bs, vs = (pltpu.roll(a, d, 1) for a in (k3, b3, v3))
            a = jnp.sum(q3 * ks * jnp.exp2(b3 - bs), axis=2, keepdims=True)
            o3 = o3 + jnp.where(lag_ok[d], a, 0.0) * vs
        return hs, a_seg, o3.reshape(T, D) + o_state, vb

    def head_finish(hs, a_seg, o_part, vb):
        a_tot = jnp.zeros((T, T), F32)
        for (_, pair_ok), a in zip(segs, a_seg):
            a_tot = jnp.where(pair_ok, a, a_tot)
        oh = o_part + _dot(a_tot.astype(BF16), vb)
        on = oh * lax.rsqrt(jnp.mean(oh * oh, axis=1, keepdims=True) + EPS) * hn_ref[:, hs]
        out_ref[:, hs] = (on * _silu(z_ref[:, hs].astype(F32))).astype(out_ref.dtype)

    pending = None
    for h in range(H_HEADS):
        cur = head_scores(h)
        if pending is not None:
            head_finish(*pending)
        pending = cur
    head_finish(*pending)


def _hgrn(proj, fgate, lb_row, hn_row, batch, seq):
    T = H_TILE
    nt = seq // T
    cb = lambda j: pl.BlockSpec((T, H_W), lambda b, c, j=j: (b * nt + c, j))
    vec = pl.BlockSpec((1, H_W), lambda b, c: (0, 0))
    return pl.pallas_call(
        _hgrn_kernel,
        grid=(batch, nt),
        in_specs=[pl.BlockSpec((T, H_W), lambda b, c: (b * nt + c, 0)), cb(1), cb(2), cb(3), vec, vec],
        out_specs=pl.BlockSpec((T, H_W), lambda b, c: (b * nt + c, 0)),
        out_shape=jax.ShapeDtypeStruct((batch * seq, H_W), BF16),
        scratch_shapes=[pltpu.VMEM((H_HEADS, H_DIM, H_DIM), F32)],
        compiler_params=_cparams("parallel", "arbitrary"),
        name="hgrn2",
    )(fgate, proj, proj, proj, lb_row, hn_row)


def _even_relayout(a):
    nb = 5 * M_W + 2 * M_HEADS
    g0 = nb + N_W + 6 * N_KV_W
    col = lambda lo, n: lax.slice_in_dim(a, lo, lo + n, axis=-1)
    pad = jnp.zeros(a.shape[:-1] + (C_NZ - C_SMALL - SM_G - 3 * N_HEADS,), a.dtype)
    out = jnp.concatenate([
        col(0, 5 * M_W),
        col(nb, N_W + 6 * N_KV_W),
        col(5 * M_W, 2 * M_HEADS),
        col(g0, 3 * N_HEADS), pad,
        col(g0 + 3 * N_HEADS, N_W),
    ], axis=-1)
    assert out.shape[-1] == EV_COLS
    return out


def _rope_tables(positions):
    inv_freq = ROPE_THETA ** (-np.arange(ROPE_HALF, dtype=np.float64) / ROPE_HALF)
    ang = positions.astype(np.float64)[:, None] * inv_freq[None, :]
    cos, sin = np.cos(ang).astype(np.float32), np.sin(ang).astype(np.float32)
    n = positions.shape[0]
    z8 = np.zeros((n, ROPE_HALF), np.float32)
    rest1 = np.ones((n, N_DIM - ROPE_DIM), np.float32)
    rest0 = np.zeros((n, N_DIM - ROPE_DIM), np.float32)
    cs = np.concatenate([cos, cos, rest1], axis=1)
    s1 = np.concatenate([z8, sin, rest0], axis=1)
    s2 = np.concatenate([-sin, z8, rest0], axis=1)
    two = lambda a: np.concatenate([a, a], axis=1)
    return two(cs), two(s1), two(s2)


def _overlap_matrix(seq):
    ncmp = seq // CMP_STRIDE
    n_sel = seq // SEL_BLOCK
    cs = np.arange(ncmp) * CMP_STRIDE
    ss = np.arange(n_sel) * SEL_BLOCK
    ov = np.clip(np.minimum(cs[:, None] + CMP_LEN, ss[None, :] + SEL_BLOCK)
                 - np.maximum(cs[:, None], ss[None, :]), 0, None).astype(np.float32)
    ov[ncmp - 1, :] = 0.0
    return np.pad(ov, ((0, 0), (0, LANES - n_sel)))


def _even_layer(x2d, res, norm_w, w_in, b_in, f_bias, conv_w, conv_b, head_norm, q_norm, k_norm,
                ck_pos, ck_w1, ck_w2, cv_pos, cv_w1, cv_w2, w_out, batch, seq):
    assert seq // SEL_BLOCK <= LANES and seq % KV_TILE == 0
    outs = _norm_proj(x2d, norm_w, _even_relayout(w_in.astype(BF16)), _even_relayout(b_in),
                      (C_SMALL, C_SMALL + LANES), tn=EV_COLS // 3, res=res)
    if res:
        x2d = outs[0]
    proj, small = outs[-2:]

    f_bias_row = jnp.zeros((1, LANES), F32).at[0, SM_F:SM_F + M_HEADS].set(f_bias)
    ya = _mlstm(proj, small, conv_w, conv_b, f_bias_row, head_norm, batch, seq)

    cs, s1, s2 = (jnp.asarray(a) for a in _rope_tables(np.arange(seq)))
    qn_row = jnp.concatenate([q_norm, q_norm]).reshape(1, LANES)
    kn_rows = jnp.concatenate([k_norm, k_norm], axis=1)
    qt, ka, kwd, vst, vwt, kc, vc, gtt = _nsa_prep(proj, small, cs, s1, s2, qn_row, kn_rows, batch, seq)

    ncmp = seq // CMP_STRIDE
    ccs, cs1, cs2 = (jnp.asarray(a) for a in _rope_tables(np.arange(ncmp) * CMP_STRIDE + CMP_LEN - 1))
    kcd, vct = _nsa_compress(
        kc.reshape(batch * ncmp, CMP_STRIDE * LANES), vc.reshape(batch * ncmp, CMP_STRIDE * LANES),
        _compress_params(ck_pos, ck_w1, ck_w2), _compress_params(cv_pos, cv_w1, cv_w2),
        ccs, cs1, cs2, kn_rows, batch, ncmp)

    ovt = jnp.asarray(_overlap_matrix(seq).T).astype(BF16)
    yb = _nsa_attn(proj, qt, gtt, ka, vst, kwd, vwt, kcd, vct, ovt, batch, seq)

    w_o = w_out.astype(BF16)
    return x2d, [(ya, w_o[:M_W]), (yb, w_o[M_W:])]


def _odd_layer(x2d, res, norm_w, w_in, b_in, lb, head_norm, w_out, batch, seq):
    outs = _norm_proj(x2d, norm_w, w_in.astype(BF16), b_in, (0, H_W), tn=H_W, res=res)
    if res:
        x2d = outs[0]
    proj, fgate = outs[-2:]
    y = _hgrn(proj, fgate, lb.reshape(1, H_W), head_norm.reshape(1, H_W), batch, seq)
    return x2d, [(y, w_out.astype(BF16))]


def kernel(x, norm_w, ev_w_in, ev_b_in, mlstm_f_bias, mlstm_conv_w, mlstm_conv_b, mlstm_head_norm, nsa_q_norm, nsa_k_norm, cmp_k_pos, cmp_k_w1, cmp_k_w2, cmp_v_pos, cmp_v_w1, cmp_v_w2, ev_w_out, od_w_in, od_b_in, hgrn_lb_logits, hgrn_head_norm, od_w_out):
    batch, seq, d = x.shape
    depth = norm_w.shape[0]
    lbs = jnp.cumsum(jax.nn.softmax(hgrn_lb_logits.astype(F32), axis=0), axis=0)
    x2d = x.reshape(batch * seq, d)
    res = []
    for layer in range(depth):
        j = layer // 2
        if layer % 2 == 0:
            x2d, res = _even_layer(x2d, res, norm_w[layer], ev_w_in[j], ev_b_in[j], mlstm_f_bias[j],
                                   mlstm_conv_w[j], mlstm_conv_b[j], mlstm_head_norm[j], nsa_q_norm[j],
                                   nsa_k_norm[j], cmp_k_pos[j], cmp_k_w1[j], cmp_k_w2[j], cmp_v_pos[j],
                                   cmp_v_w1[j], cmp_v_w2[j], ev_w_out[j], batch, seq)
        else:
            x2d, res = _odd_layer(x2d, res, norm_w[layer], od_w_in[j], od_b_in[j], lbs[j], hgrn_head_norm[j],
                                  od_w_out[j], batch, seq)
    if len(res) == 2:
        x2d = _out_proj2(res[0][0], res[1][0], res[0][1], res[1][1], x2d)
    elif res:
        x2d = _out_proj1(res[0][0], res[0][1], x2d)
    return x2d.reshape(batch, seq, d)
```

```python
import functools

import numpy as np
import jax
import jax.numpy as jnp
from jax import lax
from jax.experimental import pallas as pl
from jax.experimental.pallas import tpu as pltpu

F32 = jnp.float32
BF16 = jnp.bfloat16

LANES = 128
SUBLANES = 8
BF16_SUBLANES = 16
EPS = 1e-6
NEG = -1e30
VMEM_LIMIT_BYTES = 56 * 1024 * 1024

M_HEADS = 4
M_DIM = 128
M_W = M_HEADS * M_DIM
M_CHUNK = 128
CONV_K = 4
N_HEADS = 8
N_GROUPS = 2
N_HPG = N_HEADS // N_GROUPS
N_DIM = 64
N_W = N_HEADS * N_DIM
N_KV_W = N_GROUPS * N_DIM
CMP_LEN = 32
CMP_STRIDE = 16
CMP_HIDDEN = 128
CMP_CHUNK = 128
SEL_BLOCK = 64
SEL_SHIFT = 6
SEL_TOPK = 16
WINDOW = 512
Q_BLOCK = 128
ROPE_THETA = 500000.0
ROPE_DIM = N_DIM // 4
ROPE_HALF = ROPE_DIM // 2
KV_TILE = 512
V_ROWS = 80
LOG2E = 1.4426950408889634
H_HEADS = 8
H_DIM = 128
H_W = H_HEADS * H_DIM
H_TILE = 128
H_LAG = 2

C_MQ, C_MK, C_MV, C_MO, C_MZ = 0, 512, 1024, 1536, 2048
C_NQ = 2560
C_KC, C_VC, C_KS, C_VS = 3072, 3200, 3328, 3456
C_KW, C_VW, C_SMALL = 3584, 3712, 3840
C_NZ = 4096
EV_COLS = 4608
SM_I, SM_F, SM_G = 0, 4, 8


def _cparams(*sem):
    return pltpu.CompilerParams(dimension_semantics=sem, vmem_limit_bytes=VMEM_LIMIT_BYTES)


def _dot(a, b):
    return jnp.dot(a, b, preferred_element_type=F32)


def _dot_nt(a, b):
    return lax.dot_general(a, b, (((1,), (1,)), ((), ())), preferred_element_type=F32)


def _dot_tn(a, b):
    return lax.dot_general(a, b, (((0,), (0,)), ((), ())), preferred_element_type=F32)


def _sigmoid(x):
    return 1.0 / (1.0 + jnp.exp(-x))


def _silu(x):
    return x * _sigmoid(x)


def _split3(x):
    hi = x.astype(BF16)
    r = x - hi.astype(F32)
    mid = r.astype(BF16)
    lo = (r - mid.astype(F32)).astype(BF16)
    return hi, mid, lo


def _dot01(m01, x):
    hi, mid, lo = _split3(x)
    return _dot(m01, hi) + _dot(m01, mid) + _dot(m01, lo)


def _norm_proj_kernel(*refs, n_res, g_tiles, g_off):
    x_ref = refs[0]
    y_refs, wy_refs = refs[1:1 + n_res], refs[1 + n_res:1 + 2 * n_res]
    nw_ref, w_ref, b_ref = refs[1 + 2 * n_res:4 + 2 * n_res]
    outs = refs[4 + 2 * n_res:]
    xo_ref = outs[0] if n_res else None
    o_ref, g_ref, h_ref = outs[-3:]
    j = pl.program_id(1)

    @pl.when(j == 0)
    def _():
        x = x_ref[...]
        for y_ref, wy_ref in zip(y_refs, wy_refs):
            x = x + _dot(y_ref[...], wy_ref[...])
        if n_res:
            xo_ref[...] = x
        ms = jnp.mean(x * x, axis=-1, keepdims=True)
        h_ref[...] = (x * lax.rsqrt(ms + EPS) * nw_ref[...]).astype(BF16)

    acc = _dot(h_ref[...], w_ref[...]) + b_ref[...]
    o_ref[...] = acc.astype(o_ref.dtype)

    @pl.when((j >= g_tiles[0]) & (j < g_tiles[1]))
    def _():
        g_ref[...] = acc[:, g_off:g_off + g_ref.shape[1]]


def _norm_proj(x2d, norm_w, w_bf16, bias, g_cols, tn, res=(), tm=1024):
    t, d = x2d.shape
    n = w_bf16.shape[1]
    g_w = min(g_cols[1] - g_cols[0], tn)
    g_tiles = (g_cols[0] // tn, -(-g_cols[1] // tn))
    row = lambda w: pl.BlockSpec((tm, w), lambda i, j: (i, 0))
    const = lambda a: pl.BlockSpec(a.shape, lambda i, j: (0, 0))
    ys, wys = [y for y, _ in res], [wy for _, wy in res]
    out_specs = [pl.BlockSpec((tm, tn), lambda i, j: (i, j)),
                 pl.BlockSpec((tm, g_w), lambda i, j: (i, jnp.clip(j - g_tiles[0], 0, g_tiles[1] - g_tiles[0] - 1)))]
    out_shape = [jax.ShapeDtypeStruct((t, n), BF16), jax.ShapeDtypeStruct((t, g_cols[1] - g_cols[0]), F32)]
    if res:
        out_specs = [row(d)] + out_specs
        out_shape = [jax.ShapeDtypeStruct((t, d), F32)] + out_shape
    return pl.pallas_call(
        functools.partial(_norm_proj_kernel, n_res=len(res), g_tiles=g_tiles, g_off=g_cols[0] % tn),
        grid=(t // tm, n // tn),
        in_specs=([row(d)] + [row(y.shape[1]) for y in ys] + [const(wy) for wy in wys]
                  + [pl.BlockSpec((1, d), lambda i, j: (0, 0)),
                     pl.BlockSpec((d, tn), lambda i, j: (0, j)),
                     pl.BlockSpec((1, tn), lambda i, j: (0, j))]),
        out_specs=out_specs,
        out_shape=out_shape,
        scratch_shapes=[pltpu.VMEM((tm, d), BF16)],
        compiler_params=_cparams("parallel", "arbitrary"),
        name="norm_proj",
    )(x2d, *ys, *wys, norm_w.reshape(1, d), w_bf16, bias.reshape(1, n))


def _out_proj2_kernel(ya_ref, yb_ref, wa_ref, wb_ref, x_ref, o_ref):
    o_ref[...] = x_ref[...] + _dot(ya_ref[...], wa_ref[...]) + _dot(yb_ref[...], wb_ref[...])


def _out_proj2(ya, yb, wa, wb, x2d, tm=1024):
    t, d = x2d.shape
    ka, kb = ya.shape[1], yb.shape[1]
    return pl.pallas_call(
        _out_proj2_kernel,
        grid=(t // tm,),
        in_specs=[
            pl.BlockSpec((tm, ka), lambda i: (i, 0)),
            pl.BlockSpec((tm, kb), lambda i: (i, 0)),
            pl.BlockSpec((ka, d), lambda i: (0, 0)),
            pl.BlockSpec((kb, d), lambda i: (0, 0)),
            pl.BlockSpec((tm, d), lambda i: (i, 0)),
        ],
        out_specs=pl.BlockSpec((tm, d), lambda i: (i, 0)),
        out_shape=jax.ShapeDtypeStruct((t, d), F32),
        compiler_params=_cparams("parallel"),
        name="out_proj2",
    )(ya, yb, wa, wb, x2d)


def _out_proj1_kernel(y_ref, w_ref, x_ref, o_ref):
    o_ref[...] = x_ref[...] + _dot(y_ref[...], w_ref[...])


def _out_proj1(y, w, x2d, tm=1024):
    t, d = x2d.shape
    k = y.shape[1]
    return pl.pallas_call(
        _out_proj1_kernel,
        grid=(t // tm,),
        in_specs=[
            pl.BlockSpec((tm, k), lambda i: (i, 0)),
            pl.BlockSpec((k, d), lambda i: (0, 0)),
            pl.BlockSpec((tm, d), lambda i: (i, 0)),
        ],
        out_specs=pl.BlockSpec((tm, d), lambda i: (i, 0)),
        out_shape=jax.ShapeDtypeStruct((t, d), F32),
        compiler_params=_cparams("parallel"),
        name="out_proj1",
    )(y, w, x2d)


def _mlstm_kernel(q_ref, k_ref, v_ref, o_ref, z_ref, sm_ref, cw_ref, cb_ref, fb_ref, hn_ref,
                  out_ref, xbuf, c_ref, n_ref, m_ref):
    L, D = M_CHUNK, M_DIM
    NB = q_ref.shape[0]
    XR = xbuf.shape[1]
    X0 = BF16_SUBLANES

    @pl.when(pl.program_id(0) == 0)
    def _():
        xbuf[...] = jnp.zeros_like(xbuf)
        c_ref[...] = jnp.zeros_like(c_ref)
        n_ref[...] = jnp.zeros_like(n_ref)
        m_ref[...] = jnp.zeros_like(m_ref)

    t_i = lax.broadcasted_iota(jnp.int32, (L, XR), 0)
    r_i = lax.broadcasted_iota(jnp.int32, (L, XR), 1)
    row = lax.broadcasted_iota(jnp.int32, (L, L), 0)
    col = lax.broadcasted_iota(jnp.int32, (L, L), 1)
    causal = col <= row
    lane = lax.broadcasted_iota(jnp.int32, (L, LANES), 1)
    is_f = (lane >= SM_F) & (lane < SM_F + M_HEADS)

    heads = []
    for b in range(NB):
        xbuf[b, X0:X0 + L, 0:M_W] = q_ref[b]
        xbuf[b, X0:X0 + L, M_W:2 * M_W] = k_ref[b]
        x2 = xbuf[b]
        acc = cb_ref[...] + cw_ref[CONV_K - 1:CONV_K, :] * x2[X0:X0 + L].astype(F32)
        for d in range(1, CONV_K):
            shift = jnp.where(r_i == t_i + X0 - d, 1.0, 0.0).astype(BF16)
            acc = acc + cw_ref[CONV_K - 1 - d:CONV_K - d, :] * _dot(shift, x2)
        qk = _silu(acc)
        xbuf[b, 0:X0, :] = xbuf[b, L:L + X0, :]

        fpre = sm_ref[b] + fb_ref[...]
        gates = jnp.where(is_f, jnp.minimum(fpre, 0.0) - jnp.log(1.0 + jnp.exp(-jnp.abs(fpre))), fpre)
        bcols = _dot01(jnp.where(causal, 1.0, 0.0).astype(BF16), gates)
        gates_t = gates.T
        bcols_t = bcols.T
        for h in range(M_HEADS):
            hs = slice(h * D, (h + 1) * D)
            q = qk[:, hs]
            k = qk[:, M_W + h * D:M_W + (h + 1) * D] * (D ** -0.5)
            i_col = gates[:, SM_I + h:SM_I + h + 1]
            i_row = gates_t[SM_I + h:SM_I + h + 1, :]
            bcum_col = bcols[:, SM_F + h:SM_F + h + 1]
            bcum_row = bcols_t[SM_F + h:SM_F + h + 1, :]
            m_prev = m_ref[b * M_HEADS + h]
            dmat = jnp.where(causal, bcum_col - bcum_row + i_row, -jnp.inf)
            inter = bcum_col + m_prev
            m_row = jnp.maximum(jnp.max(dmat, axis=1, keepdims=True), inter)
            b_last = bcum_col[L - 1:L, :]
            dl = b_last - bcum_col + i_col
            m_new = jnp.maximum(b_last + m_prev, jnp.max(dl, axis=0, keepdims=True))
            wkk = jnp.exp(dl - m_new) * k
            heads.append(dict(
                b=b, hs=hs, st=b * M_HEADS + h, q=q, qb=q.astype(BF16), kb=k.astype(BF16),
                vb=v_ref[b, :, hs].astype(BF16), wkk=wkk,
                e=jnp.exp(dmat - m_row), w_inter=jnp.exp(inter - m_row), floor=jnp.exp(-m_row),
                decay=jnp.exp(b_last + m_prev - m_new), m_new=m_new))
    s_qk = [_dot_nt(t["qb"], t["kb"]) for t in heads]
    s_qc = [_dot(t["qb"], c_ref[t["st"]].astype(BF16)) for t in heads]
    s_kv = [_dot_tn(t["wkk"].astype(BF16), t["vb"]) for t in heads]
    ones_b = jnp.ones((L, LANES), BF16)
    smats = [(s_qk[j] * t["e"]).astype(BF16) for j, t in enumerate(heads)]
    s_sv = [_dot(smats[j], t["vb"]) for j, t in enumerate(heads)]
    s_ds = [_dot(smats[j], ones_b) for j in range(len(heads))]
    s_qn = [_dot((t["q"] * n_ref[t["st"]]).astype(BF16), ones_b) for t in heads]
    hhs = []
    for j, t in enumerate(heads):
        st = t["st"]
        num = s_sv[j] + t["w_inter"] * s_qc[j]
        den = s_ds[j] + t["w_inter"] * s_qn[j]
        hhs.append(num / jnp.maximum(jnp.abs(den), t["floor"]))
        c_ref[st] = t["decay"] * c_ref[st] + s_kv[j]
        n_ref[st] = t["decay"] * n_ref[st] + jnp.sum(t["wkk"], axis=0, keepdims=True)
        m_ref[st] = t["m_new"]
    s_ms = [_dot((hh * hh).astype(BF16), ones_b) for hh in hhs]
    for j, t in enumerate(heads):
        b, hs = t["b"], t["hs"]
        hn = hhs[j] * lax.rsqrt(s_ms[j] * (1.0 / D) + EPS) * hn_ref[:, hs]
        gate = _sigmoid(o_ref[b, :, hs].astype(F32)) * _silu(z_ref[b, :, hs].astype(F32))
        out_ref[b, :, hs] = (hn * gate).astype(out_ref.dtype)


def _mlstm(proj, small, conv_w, conv_b, f_bias_row, head_norm, batch, seq):
    L = M_CHUNK
    nc = seq // L
    p3 = proj.reshape(batch, seq, proj.shape[1])
    s3 = small.reshape(batch, seq, LANES)
    cb = lambda off: pl.BlockSpec((batch, L, M_W), lambda c, off=off: (0, c, off // M_W))
    full = lambda r, w: pl.BlockSpec((r, w), lambda c: (0, 0))
    out = pl.pallas_call(
        _mlstm_kernel,
        grid=(nc,),
        in_specs=[
            cb(C_MQ), cb(C_MK), cb(C_MV), cb(C_MO), cb(C_MZ),
            pl.BlockSpec((batch, L, LANES), lambda c: (0, c, 0)),
            full(CONV_K, 2 * M_W), full(1, 2 * M_W), full(1, LANES), full(1, M_W),
        ],
        out_specs=pl.BlockSpec((batch, L, M_W), lambda c: (0, c, 0)),
        out_shape=jax.ShapeDtypeStruct((batch, seq, M_W), BF16),
        scratch_shapes=[
            pltpu.VMEM((batch, 2 * L, 2 * M_W), BF16),
            pltpu.VMEM((batch * M_HEADS, M_DIM, M_DIM), F32),
            pltpu.VMEM((batch * M_HEADS, 1, M_DIM), F32),
            pltpu.VMEM((batch * M_HEADS, 1, 1), F32),
        ],
        compiler_params=_cparams("arbitrary"),
        name="mlstm",
    )(p3, p3, p3, p3, p3, s3, conv_w, conv_b.reshape(1, -1), f_bias_row, head_norm.reshape(1, -1))
    return out.reshape(batch * seq, M_W)


def _group_rms(x, w_row):
    r = lax.broadcasted_iota(jnp.int32, (LANES, LANES), 0)
    c = lax.broadcasted_iota(jnp.int32, (LANES, LANES), 1)
    same_head = jnp.where((r < N_DIM) == (c < N_DIM), 1.0, 0.0).astype(BF16)
    ms = _dot_hilo(x * x, same_head) * (1.0 / N_DIM)
    return x * lax.rsqrt(ms + EPS) * w_row


def _lane_gather01(src_lane, width=LANES):
    r = lax.broadcasted_iota(jnp.int32, (LANES, width), 0)
    c = lax.broadcasted_iota(jnp.int32, (LANES, width), 1)
    return jnp.where(r == src_lane(c), 1.0, 0.0).astype(BF16)


def _dot_hilo(x, m01):
    hi = x.astype(BF16)
    lo = (x - hi.astype(F32)).astype(BF16)
    return _dot(hi, m01) + _dot(lo, m01)


def _rope(x, cs, s1, s2):
    return x * cs + pltpu.roll(x, ROPE_HALF, 1) * s1 + pltpu.roll(x, LANES - ROPE_HALF, 1) * s2


def _dup_halves(x):
    dup = _lane_gather01(lambda c: jnp.where(c < LANES, 0, N_DIM) + (c & (N_DIM - 1)), 2 * LANES)
    both = _dot(x, dup).astype(BF16)
    return both[:, 0:LANES], both[:, LANES:2 * LANES]


def _transpose01(x):
    return _dot_nt(_lane_gather01(lambda c: c), x)


def _nsa_prep_kernel(q_ref, kv_ref, kw_ref, sm_ref, cs_ref, s1_ref, s2_ref, qn_ref, kn_ref,
                     qt_ref, ka_ref, kwd_ref, vst_ref, vwt_ref, kc_ref, vc_ref, gtt_ref,
                     *, tp, seq):
    cs, s1, s2 = cs_ref[...], s1_ref[...], s2_ref[...]
    nq = N_W // LANES
    kv = kv_ref[...].astype(F32)
    kwb = kw_ref[...].astype(F32)
    kc_ref[...] = kv[:, 0:LANES]
    vc_ref[...] = kv[:, LANES:2 * LANES]
    slabs = [q_ref[:, j * LANES:(j + 1) * LANES].astype(F32) for j in range(nq)]
    slabs += [kv[:, 2 * LANES:3 * LANES], kwb[:, 0:LANES]]
    gains = [qn_ref[...]] * nq + [kn_ref[1:2, :], kn_ref[2:3, :]]
    normed = [_group_rms(x, w) for x, w in zip(slabs, gains)]
    roped = [_rope(x, cs, s1, s2) for x in normed]
    qs = [(x * (N_DIM ** -0.5 * LOG2E)).astype(BF16) for x in roped[:nq]]
    k0, k1 = _dup_halves(roped[nq].astype(BF16))
    w0, w1 = _dup_halves(roped[nq + 1].astype(BF16))
    gates = _sigmoid(sm_ref[...])
    g_hi = gates.astype(BF16)
    g_lo = (gates - g_hi.astype(F32)).astype(BF16)
    qts = [_transpose01(x) for x in qs]
    vst = _transpose01(kv_ref[:, 3 * LANES:4 * LANES])
    vwt = _transpose01(kw_ref[:, LANES:2 * LANES])
    gtt_ref[...] = _transpose01(g_hi) + _transpose01(g_lo)
    for j in range(nq):
        qt_ref[j * LANES:(j + 1) * LANES, :] = qts[j].astype(BF16)
    pos0 = (pl.program_id(0) * tp) % seq
    pos = pos0 + lax.broadcasted_iota(jnp.int32, (tp, LANES), 0)
    blk = lax.broadcasted_iota(jnp.int32, (tp, LANES), 1)
    onehot = jnp.where(jnp.right_shift(pos, SEL_SHIFT) == blk, 1.0, 0.0).astype(BF16)
    ka_ref[:, 0:LANES] = k0
    ka_ref[:, LANES:2 * LANES] = onehot
    ka_ref[:, 2 * LANES:3 * LANES] = k1
    ka_ref[:, 3 * LANES:4 * LANES] = onehot
    fill =(lax.broadcasted_iota(jnp.int32, (V_ROWS - N_DIM, Q_BLOCK), 0) == 0).astype(BF16)
    for c in range(tp // Q_BLOCK):
        for g in range(N_GROUPS):
            vst_ref[c, g * V_ROWS:g * V_ROWS + N_DIM, :] = (
                vst[g * N_DIM:(g + 1) * N_DIM, c * Q_BLOCK:(c + 1) * Q_BLOCK].astype(BF16))
            vst_ref[c, g * V_ROWS + N_DIM:(g + 1) * V_ROWS, :] = fill
    kwd_ref[:, 0:LANES] = w0
    kwd_ref[:, LANES:2 * LANES] = w1
    vwt_ref[...] = vwt.astype(BF16)


def _nsa_prep(proj, small, cs, s1, s2, qn_row, kn_rows, batch, seq):
    tp = KV_TILE
    t = batch * seq
    nsb = seq // tp
    rb = lambda w, off: pl.BlockSpec((tp, w), lambda i, off=off, w=w: (i, off // w))
    tab = pl.BlockSpec((tp, LANES), lambda i: (i % nsb, 0))
    o = lambda w: pl.BlockSpec((tp, w), lambda i: (i, 0))
    ot = lambda r: pl.BlockSpec((r, tp), lambda i: (0, i))
    sds = lambda w, dt: jax.ShapeDtypeStruct((t, w), dt)
    return pl.pallas_call(
        functools.partial(_nsa_prep_kernel, tp=tp, seq=seq),
        grid=(t // tp,),
        in_specs=[rb(512, C_NQ), rb(512, C_KC), rb(512, C_KW), o(LANES), tab, tab, tab,
                  pl.BlockSpec((1, LANES), lambda i: (0, 0)), pl.BlockSpec((3, LANES), lambda i: (0, 0))],
        out_specs=[ot(N_W), o(512), o(256),
                   pl.BlockSpec((tp // Q_BLOCK, N_GROUPS * V_ROWS, Q_BLOCK), lambda i: (i, 0, 0)),
                   ot(LANES), o(128), o(128), ot(LANES)],
        out_shape=[jax.ShapeDtypeStruct((N_W, t), BF16), sds(512, BF16), sds(256, BF16),
                   jax.ShapeDtypeStruct((t // Q_BLOCK, N_GROUPS * V_ROWS, Q_BLOCK), BF16),
                   jax.ShapeDtypeStruct((LANES, t), BF16),
                   sds(128, F32), sds(128, F32), jax.ShapeDtypeStruct((LANES, t), F32)],
        compiler_params=_cparams("parallel"),
        name="nsa_prep",
    )(proj, proj, proj, small, cs, s1, s2, qn_row, kn_rows)


def _gelu_tanh(x):
    return 0.5 * x * (1.0 + jnp.tanh(0.7978845608028654 * (x + 0.044715 * x * x * x)))


def _compress_mlp(x_ref, pa_ref, pb_ref, wa_ref, wb_ref, w2_ref, hb_ref):
    n = x_ref.shape[0]
    x = x_ref[...]
    ha = _dot((x + pa_ref[...]).astype(BF16), wa_ref[...])
    hb_ref[0:n, :] = _dot((x + pb_ref[...]).astype(BF16), wb_ref[...])
    hb_ref[n:n + SUBLANES, :] = jnp.zeros((SUBLANES, hb_ref.shape[1]), F32)
    hid = ha + hb_ref[1:n + 1, :]
    return _dot(_gelu_tanh(hid).astype(BF16), w2_ref[...])


def _nsa_compress_kernel(kx_ref, vx_ref, kpa_ref, kpb_ref, vpa_ref, vpb_ref, kwa_ref, kwb_ref, kw2_ref,
                         vwa_ref, vwb_ref, vw2_ref, cs_ref, s1_ref, s2_ref, kn_ref,
                         kcd_ref, vct_ref, hb_ref):
    n = kx_ref.shape[0]
    rowi = lax.broadcasted_iota(jnp.int32, (n, LANES), 0)
    live = rowi < n - 1
    kc = _compress_mlp(kx_ref, kpa_ref, kpb_ref, kwa_ref, kwb_ref, kw2_ref, hb_ref)
    kc = _rope(_group_rms(kc, kn_ref[0:1, :]), cs_ref[...], s1_ref[...], s2_ref[...])
    kc = jnp.where(live, kc, 0.0)
    c0, c1 = _dup_halves(kc.astype(BF16))
    kcd_ref[:, 0:LANES] = c0
    kcd_ref[:, LANES:2 * LANES] = c1
    vc = _compress_mlp(vx_ref, vpa_ref, vpb_ref, vwa_ref, vwb_ref, vw2_ref, hb_ref)
    vc = jnp.where(live, vc, 0.0)
    vct_ref[0] = vc.T.astype(BF16)


def _nsa_compress(kx, vx, kp, vp, cs, s1, s2, kn_rows, batch, nrow):
    wide = CMP_STRIDE * LANES
    full = lambda a: pl.BlockSpec(a.shape, lambda b: (0,) * a.ndim)
    xs = pl.BlockSpec((nrow, wide), lambda b: (b, 0))
    args = (kx, vx, kp[0], kp[1], vp[0], vp[1], kp[2], kp[3], kp[4], vp[2], vp[3], vp[4], cs, s1, s2, kn_rows)
    return pl.pallas_call(
        _nsa_compress_kernel,
        grid=(batch,),
        in_specs=[xs, xs] + [full(a) for a in args[2:]],
        out_specs=[pl.BlockSpec((nrow, 2 * LANES), lambda b: (b, 0)),
                   pl.BlockSpec((1, LANES, nrow), lambda b: (b, 0, 0))],
        out_shape=[jax.ShapeDtypeStruct((batch * nrow, 2 * LANES), BF16),
                   jax.ShapeDtypeStruct((batch, LANES, nrow), BF16)],
        scratch_shapes=[pltpu.VMEM((nrow + SUBLANES, 2 * CMP_HIDDEN), F32)],
        compiler_params=_cparams("parallel"),
        name="nsa_compress",
    )(*args)


def _compress_params(pos, w1, w2):
    def expand(w1_half, pos_half):
        w = w1_half.reshape(CMP_STRIDE, N_DIM, CMP_HIDDEN)
        z = jnp.zeros_like(w)
        g0 = jnp.concatenate([w, z], axis=-1)
        g1 = jnp.concatenate([z, w], axis=-1)
        wx = jnp.stack([g0, g1], axis=1).reshape(CMP_STRIDE * LANES, 2 * CMP_HIDDEN)
        px = jnp.concatenate([pos_half, pos_half], axis=-1).reshape(1, CMP_STRIDE * LANES)
        return wx.astype(BF16), px.astype(F32)

    half = CMP_STRIDE * N_DIM
    wa, pa = expand(w1[:half], pos[:CMP_STRIDE])
    wb, pb = expand(w1[half:], pos[CMP_STRIDE:])
    z2 = jnp.zeros_like(w2)
    w2x = jnp.concatenate([jnp.concatenate([w2, z2], axis=1), jnp.concatenate([z2, w2], axis=1)], axis=0)
    return pa, pb, wa, wb, w2x.astype(BF16)


def _nsa_attn_kernel(qt_ref, gtt_ref, z_ref, ka_ref, vst_ref,
                     kw0, kw1, kw2, kw3, kw4, wt0, wt1, wt2, wt3, wt4,
                     kcd_ref, vct_ref, ovt_ref, out_ref, qa_ref, acc_ref, *, ncmp):
    QB = Q_BLOCK
    NP = N_HEADS // 2
    PW = 2 * QB
    i = pl.program_id(1)
    t0 = i * QB
    kw_refs = (kw0, kw1, kw2, kw3, kw4)
    vwt_refs = (wt0, wt1, wt2, wt3, wt4)
    nwt = len(kw_refs)

    sub = lax.broadcasted_iota(jnp.int32, (QB, QB), 0)
    lane = lax.broadcasted_iota(jnp.int32, (QB, QB), 1)
    top = sub < N_DIM
    t_row = t0 + lax.broadcasted_iota(jnp.int32, (1, QB), 1)
    t_row2 = jnp.concatenate([t_row, t_row], axis=1)

    def pair_q(pr):
        slab = qt_ref[pr * LANES:(pr + 1) * LANES, :]
        zero = jnp.zeros_like(slab)
        return jnp.concatenate([jnp.where(top, slab, zero), jnp.where(top, zero, slab)], axis=1)

    blk_f = sub.astype(F32)
    cur = jnp.right_shift(t_row, SEL_SHIFT)
    grp = lambda pr: pr // (N_HPG // 2)
    for pr in range(NP):
        qa_ref[pr, 0:LANES, :] = pair_q(pr)

    def compressed(n):
        nidx = lax.broadcasted_iota(jnp.int32, (n, PW), 0)
        ok = (nidx * CMP_STRIDE + (CMP_LEN - 1) <= t_row2) & (nidx < ncmp - 1)
        any_ok = t_row2 >= CMP_LEN - 1
        s_cmp = [_dot(kcd_ref[0:n, grp(pr) * LANES:(grp(pr) + 1) * LANES], qa_ref[pr, 0:LANES, :])
                 for pr in range(NP)]
        p_cmp = []
        for pr in range(NP):
            s = jnp.where(ok, s_cmp[pr], NEG)
            mx = jnp.max(s, axis=0, keepdims=True)
            p = jnp.exp2(s - mx)
            den = jnp.sum(p, axis=0, keepdims=True)
            p_cmp.append(p * jnp.where(any_ok, 1.0 / den, 0.0))
        outs = [_dot(vct_ref[0, grp(pr) * N_DIM:(grp(pr) + 1) * N_DIM, 0:n], p_cmp[pr].astype(BF16))
                for pr in range(NP)]
        for g in range(N_GROUPS):
            psum = jnp.zeros((n, QB), F32)
            for pp in range(N_HPG // 2):
                p = p_cmp[g * (N_HPG // 2) + pp]
                psum = psum + p[:, 0:QB] + p[:, QB:PW]
            p_hi = psum.astype(BF16)
            p_lo = (psum - p_hi.astype(F32)).astype(BF16)
            outs.append(_dot(ovt_ref[:, 0:n], p_hi) + _dot(ovt_ref[:, 0:n], p_lo))
        return tuple(outs)

    n_opts = list(range(CMP_CHUNK, ncmp + 1, CMP_CHUNK))
    need = (t0 + QB - CMP_LEN) // CMP_STRIDE + 1
    outs = lax.switch(jnp.clip((need - 1) // CMP_CHUNK, 0, len(n_opts) - 1),
                      [functools.partial(compressed, n) for n in n_opts])
    o_cmp, imps = outs[:NP], outs[NP:]
    forced = (sub == 0) | (sub == cur) | (sub == cur - 1)
    n_rounds = SEL_TOPK - 3
    valid = sub <= cur
    works0 = [jnp.where(valid, jnp.where(forced, -jnp.inf, imp), NEG) for imp in imps]

    def rounds(works, exact):
        sels = [forced for _ in range(N_GROUPS)]
        for _ in range(n_rounds):
            for g in range(N_GROUPS):
                mx = jnp.max(works[g], axis=0, keepdims=True)
                if exact:
                    idx = jnp.min(jnp.where(works[g] == mx, blk_f, float(LANES)), axis=0, keepdims=True)
                    pick = blk_f == idx
                else:
                    pick = works[g] == mx
                sels[g] = sels[g] | pick
                works[g] = jnp.where(pick, -jnp.inf, works[g])
        return tuple(jnp.where(s & valid, 0.0, NEG) for s in sels)

    bias = rounds(list(works0), exact=False)
    picked = [jnp.sum(jnp.where((b == 0.0) & jnp.logical_not(forced), 1.0, 0.0), axis=0, keepdims=True)
              for b in bias]
    tied = jnp.max(functools.reduce(jnp.maximum, picked)) > n_rounds
    bias = lax.cond(tied, lambda: rounds(list(works0), exact=True), lambda: bias)
    for g in range(N_GROUPS):
        neg = bias[g].astype(BF16)
        neg2 = jnp.concatenate([neg, neg], axis=1)
        for pp in range(N_HPG // 2):
            qa_ref[g * (N_HPG // 2) + pp, LANES:2 * LANES, :] = neg2

    CH = QB
    above = jnp.concatenate([sub > lane, sub > lane], axis=1)

    def sel_scores(cj):
        k0 = pl.multiple_of(cj * CH, CH)
        return [_dot(ka_ref[pl.ds(k0, CH), 2 * grp(pr) * LANES:2 * (grp(pr) + 1) * LANES], qa_ref[pr])
                for pr in range(NP)]

    def sel_softmax(ss, ms, masked):
        new, alphas, ps = [], [], []
        for pr in range(NP):
            s = jnp.where(above, NEG, ss[pr]) if masked else ss[pr]
            m_new = jnp.maximum(ms[pr], jnp.max(s, axis=0, keepdims=True))
            alphas.append(jnp.exp2(ms[pr] - m_new))
            ps.append(jnp.exp2(s - m_new).astype(BF16))
            new.append(m_new)
        return tuple(new), alphas, ps

    def sel_values(cj, alphas, ps):
        for pr in range(NP):
            g = grp(pr)
            acc_ref[pr] = alphas[pr] * acc_ref[pr] + _dot(vst_ref[cj, g * V_ROWS:(g + 1) * V_ROWS, :], ps[pr])

    def sel_chunk(cj, ms, masked):
        ms, alphas, ps = sel_softmax(sel_scores(cj), ms, masked)
        sel_values(cj, alphas, ps)
        return ms

    def sel_tile(jt, ms, nc):
        c0 = jt * nc
        k0 = pl.multiple_of(c0 * CH, CH)
        big = [_dot(ka_ref[pl.ds(k0, nc * CH), 2 * grp(pr) * LANES:2 * (grp(pr) + 1) * LANES], qa_ref[pr])
               for pr in range(NP)]
        for c in range(nc):
            ss = [b[c * CH:(c + 1) * CH] for b in big]
            ms, alphas, ps = sel_softmax(ss, ms, False)
            sel_values(c0 + c, alphas, ps)
        return ms

    acc_ref[...] = jnp.zeros_like(acc_ref)
    ms = tuple(jnp.full((1, PW), 2.0 * NEG, F32) for _ in range(NP))
    done = 0
    for nc in (4 * KV_TILE // CH, 2 * KV_TILE // CH, KV_TILE // CH):
        n_run = (i - done) // nc
        ms = lax.fori_loop(done // nc, done // nc + n_run, functools.partial(sel_tile, nc=nc), ms)
        done = done + n_run * nc
    ms = lax.fori_loop(done, i, lambda cj, ms: sel_chunk(cj, ms, False), ms)
    s_diag = sel_scores(i)
    s_win = [[_dot(kw_refs[c][:, grp(pr) * LANES:(grp(pr) + 1) * LANES], qa_ref[pr, 0:LANES, :])
              for c in range(nwt)] for pr in range(NP)]
    ms, alphas, ps_diag = sel_softmax(s_diag, ms, True)
    p_win, d_win = [], []
    for pr in range(NP):
        ss = []
        for c in range(nwt):
            ok = (i - (nwt - 1) + c) >= 0
            if c == 0:
                ok = ok & above
            elif c == nwt - 1:
                ok = ok & jnp.logical_not(above)
            ss.append(jnp.where(ok, s_win[pr][c], NEG))
        mx = ss[0].max(axis=0, keepdims=True)
        for c in range(1, nwt):
            mx = jnp.maximum(mx, ss[c].max(axis=0, keepdims=True))
        ps = [jnp.exp2(ss[c] - mx) for c in range(nwt)]
        den = ps[0].sum(axis=0, keepdims=True)
        for c in range(1, nwt):
            den = den + ps[c].sum(axis=0, keepdims=True)
        p_win.append([p.astype(BF16) for p in ps])
        d_win.append(den)
    sel_values(i, alphas, ps_diag)
    o_win = []
    for pr in range(NP):
        g = grp(pr)
        acc = _dot(vwt_refs[0][g * N_DIM:(g + 1) * N_DIM, :], p_win[pr][0])
        for c in range(1, nwt):
            acc = acc + _dot(vwt_refs[c][g * N_DIM:(g + 1) * N_DIM, :], p_win[pr][c])
        o_win.append(acc * (1.0 / d_win[pr]))
    o_sel = [acc_ref[pr, 0:N_DIM, :] * (1.0 / acc_ref[pr, N_DIM:N_DIM + 1, :]) for pr in range(NP)]

    for pr in range(NP):
        halves = []
        for par in range(2):
            h = 2 * pr + par
            cs = slice(par * QB, (par + 1) * QB)
            gc = gtt_ref[SM_G + 3 * h:SM_G + 3 * h + 1, :]
            gs = gtt_ref[SM_G + 3 * h + 1:SM_G + 3 * h + 2, :]
            gw = gtt_ref[SM_G + 3 * h + 2:SM_G + 3 * h + 3, :]
            halves.append(gc * o_cmp[pr][:, cs] + gs * o_sel[pr][:, cs] + gw * o_win[pr][:, cs])
        y = jnp.concatenate(halves, axis=0).T
        sl = slice(pr * LANES, (pr + 1) * LANES)
        out_ref[:, sl] = (y * _silu(z_ref[:, sl].astype(F32))).astype(out_ref.dtype)


def _nsa_attn(proj, qt, gtt, ka, vst, kwd, vwt, kcd, vct, ovt, batch, seq):
    QB = Q_BLOCK
    nqb = seq // QB
    ncmp = seq // CMP_STRIDE
    ntile = seq // KV_TILE
    nwt = WINDOW // QB + 1
    row = lambda w: pl.BlockSpec((QB, w), lambda b, i: (b * nqb + i, 0))
    colb = lambda r: pl.BlockSpec((r, QB), lambda b, i: (0, b * nqb + i))
    wblk = lambda b, i, c: b * nqb + jnp.maximum(i - (nwt - 1) + c, 0)
    in_specs = ([colb(N_W), colb(LANES), pl.BlockSpec((QB, N_W), lambda b, i: (b * nqb + i, C_NZ // N_W)),
                 pl.BlockSpec((seq, 4 * LANES), lambda b, i: (b, 0)),
                 pl.BlockSpec((nqb, N_GROUPS * V_ROWS, QB), lambda b, i: (b, 0, 0))]
                + [pl.BlockSpec((QB, 2 * LANES), lambda b, i, c=c: (wblk(b, i, c), 0)) for c in range(nwt)]
                + [pl.BlockSpec((LANES, QB), lambda b, i, c=c: (0, wblk(b, i, c))) for c in range(nwt)]
                + [pl.BlockSpec((ncmp, 2 * LANES), lambda b, i: (b, 0)),
                   pl.BlockSpec((1, LANES, ncmp), lambda b, i: (b, 0, 0)),
                   pl.BlockSpec((LANES, ncmp), lambda b, i: (0, 0))])
    return pl.pallas_call(
        functools.partial(_nsa_attn_kernel, ncmp=ncmp),
        grid=(batch, nqb),
        in_specs=in_specs,
        out_specs=row(N_W),
        out_shape=jax.ShapeDtypeStruct((batch * seq, N_W), BF16),
        scratch_shapes=[pltpu.VMEM((N_HEADS // 2, 2 * LANES, 2 * QB), BF16),
                        pltpu.VMEM((N_HEADS // 2, V_ROWS, 2 * QB), F32)],
        compiler_params=_cparams("parallel", "arbitrary"),
        name="nsa_attn",
    )(qt, gtt, proj, ka, vst, *([kwd] * nwt), *([vwt] * nwt), kcd, vct, ovt)


def _hgrn_kernel(f_ref, i_ref, q_ref, z_ref, lb_ref, hn_ref, out_ref, st_ref):
    T, D = H_TILE, H_DIM
    W = H_W

    @pl.when(pl.program_id(1) == 0)
    def _():
        st_ref[...] = jnp.zeros_like(st_ref)

    lb = lb_ref[...]
    fp = f_ref[...]
    e = jnp.exp(-jnp.abs(fp))
    r = 1.0 / (1.0 + e)
    sig = jnp.where(fp >= 0, r, e * r)
    nsig = jnp.where(fp >= 0, e * r, r)
    logf2 = jnp.log(lb + (1.0 - lb) * sig) * LOG2E
    kk = (1.0 - lb) * nsig

    row = lax.broadcasted_iota(jnp.int32, (T, T), 0)
    col = lax.broadcasted_iota(jnp.int32, (T, T), 1)
    tri01 = jnp.where(col <= row, 1.0, 0.0).astype(BF16)
    bcum = _dot01(tri01, logf2)
    rowi = lax.broadcasted_iota(jnp.int32, (T, 1), 0)

    segs = []
    sg = 2 * H_LAG
    while sg <= T:
        if sg == T:
            bmid = jnp.broadcast_to(bcum[sg // 2 - 1:sg // 2, :], (T, W))
        else:
            b3 = bcum.reshape(T // sg, sg, W)
            bmid = jnp.broadcast_to(b3[:, sg // 2 - 1:sg // 2, :], (T // sg, sg, W)).reshape(T, W)
        x = bcum - bmid
        shift = sg.bit_length() - 1
        pair_ok = ((jnp.right_shift(row, shift) == jnp.right_shift(col, shift))
                   & (jnp.bitwise_and(row, sg - 1) >= sg // 2) & (jnp.bitwise_and(col, sg - 1) < sg // 2))
        segs.append((jnp.exp2(jnp.minimum(x, -x)), pair_ok))
        sg *= 2

    sub3 = lax.broadcasted_iota(jnp.int32, (T // SUBLANES, SUBLANES, 1), 1)
    lag_ok = [jnp.bitwise_and(sub3, H_LAG - 1) >= d for d in range(H_LAG)]
    b_end = bcum[T - 1:T, :]
    v_all = i_ref[...].astype(F32)
    def head_scores(h):
        hs = slice(h * D, (h + 1) * D)
        q = _silu(q_ref[:, hs].astype(F32))
        k = kk[:, hs]
        b = bcum[:, hs]
        v = v_all[:, hs]
        vb = v.astype(BF16)
        a_seg = []
        for fac, _ in segs:
            f = fac[:, hs]
            a_seg.append(_dot_nt((q * f).astype(BF16), (k * f).astype(BF16)))
        st = st_ref[h]
        o_state = _dot_nt((q * jnp.exp2(b)).astype(BF16), st.astype(BF16))
        kt = (k * jnp.exp2(b_end[:, hs] - b)).astype(BF16)
        st_ref[h] = st * jnp.exp2(b_end[:, hs]) + _dot_tn(vb, kt)
        q3, k3, b3, v3 = (a.reshape(T // SUBLANES, SUBLANES, D) for a in (q, k, b, v))
        o3 = jnp.sum(q3 * k3, axis=2, keepdims=True) * v3
        for d in range(1, H_LAG):
            ks, bs, vs = (pltpu.roll(a, d, 1) for a in (k3, b3, v3))
            a = jnp.sum(q3 * ks * jnp.exp2(b3 - bs), axis=2, keepdims=True)
            o3 = o3 + jnp.where(lag_ok[d], a, 0.0) * vs
        return hs, a_seg, o3.reshape(T, D) + o_state, vb

    def head_finish(hs, a_seg, o_part, vb):
        a_tot = jnp.zeros((T, T), F32)
        for (_, pair_ok), a in zip(segs, a_seg):
            a_tot = jnp.where(pair_ok, a, a_tot)
        oh = o_part + _dot(a_tot.astype(BF16), vb)
        on = oh * lax.rsqrt(jnp.mean(oh * oh, axis=1, keepdims=True) + EPS) * hn_ref[:, hs]
        out_ref[:, hs] = (on * _silu(z_ref[:, hs].astype(F32))).astype(out_ref.dtype)

    pending = None
    for h in range(H_HEADS):
        cur = head_scores(h)
        if pending is not None:
            head_finish(*pending)
        pending = cur
    head_finish(*pending)


def _hgrn(proj, fgate, lb_row, hn_row, batch, seq):
    T = H_TILE
    nt = seq // T
    cb = lambda j: pl.BlockSpec((T, H_W), lambda b, c, j=j: (b * nt + c, j))
    vec = pl.BlockSpec((1, H_W), lambda b, c: (0, 0))
    return pl.pallas_call(
        _hgrn_kernel,
        grid=(batch, nt),
        in_specs=[pl.BlockSpec((T, H_W), lambda b, c: (b * nt + c, 0)), cb(1), cb(2), cb(3), vec, vec],
        out_specs=pl.BlockSpec((T, H_W), lambda b, c: (b * nt + c, 0)),
        out_shape=jax.ShapeDtypeStruct((batch * seq, H_W), BF16),
        scratch_shapes=[pltpu.VMEM((H_HEADS, H_DIM, H_DIM), F32)],
        compiler_params=_cparams("parallel", "arbitrary"),
        name="hgrn2",
    )(fgate, proj, proj, proj, lb_row, hn_row)


def _even_relayout(a):
    nb = 5 * M_W + 2 * M_HEADS
    g0 = nb + N_W + 6 * N_KV_W
    col = lambda lo, n: lax.slice_in_dim(a, lo, lo + n, axis=-1)
    pad = jnp.zeros(a.shape[:-1] + (C_NZ - C_SMALL - SM_G - 3 * N_HEADS,), a.dtype)
    out = jnp.concatenate([
        col(0, 5 * M_W),
        col(nb, N_W + 6 * N_KV_W),
        col(5 * M_W, 2 * M_HEADS),
        col(g0, 3 * N_HEADS), pad,
        col(g0 + 3 * N_HEADS, N_W),
    ], axis=-1)
    assert out.shape[-1] == EV_COLS
    return out


def _rope_tables(positions):
    inv_freq = ROPE_THETA ** (-np.arange(ROPE_HALF, dtype=np.float64) / ROPE_HALF)
    ang = positions.astype(np.float64)[:, None] * inv_freq[None, :]
    cos, sin = np.cos(ang).astype(np.float32), np.sin(ang).astype(np.float32)
    n = positions.shape[0]
    z8 = np.zeros((n, ROPE_HALF), np.float32)
    rest1 = np.ones((n, N_DIM - ROPE_DIM), np.float32)
    rest0 = np.zeros((n, N_DIM - ROPE_DIM), np.float32)
    cs = np.concatenate([cos, cos, rest1], axis=1)
    s1 = np.concatenate([z8, sin, rest0], axis=1)
    s2 = np.concatenate([-sin, z8, rest0], axis=1)
    two = lambda a: np.concatenate([a, a], axis=1)
    return two(cs), two(s1), two(s2)


def _overlap_matrix(seq):
    ncmp = seq // CMP_STRIDE
    n_sel = seq // SEL_BLOCK
    cs = np.arange(ncmp) * CMP_STRIDE
    ss = np.arange(n_sel) * SEL_BLOCK
    ov = np.clip(np.minimum(cs[:, None] + CMP_LEN, ss[None, :] + SEL_BLOCK)
                 - np.maximum(cs[:, None], ss[None, :]), 0, None).astype(np.float32)
    ov[ncmp - 1, :] = 0.0
    return np.pad(ov, ((0, 0), (0, LANES - n_sel)))


def _even_layer(x2d, res, norm_w, w_in, b_in, f_bias, conv_w, conv_b, head_norm, q_norm, k_norm,
                ck_pos, ck_w1, ck_w2, cv_pos, cv_w1, cv_w2, w_out, batch, seq):
    assert seq // SEL_BLOCK <= LANES and seq % KV_TILE == 0
    outs = _norm_proj(x2d, norm_w, _even_relayout(w_in.astype(BF16)), _even_relayout(b_in),
                      (C_SMALL, C_SMALL + LANES), tn=EV_COLS // 3, res=res)
    if res:
        x2d = outs[0]
    proj, small = outs[-2:]

    f_bias_row = jnp.zeros((1, LANES), F32).at[0, SM_F:SM_F + M_HEADS].set(f_bias)
    ya = _mlstm(proj, small, conv_w, conv_b, f_bias_row, head_norm, batch, seq)

    cs, s1, s2 = (jnp.asarray(a) for a in _rope_tables(np.arange(seq)))
    qn_row = jnp.concatenate([q_norm, q_norm]).reshape(1, LANES)
    kn_rows = jnp.concatenate([k_norm, k_norm], axis=1)
    qt, ka, kwd, vst, vwt, kc, vc, gtt = _nsa_prep(proj, small, cs, s1, s2, qn_row, kn_rows, batch, seq)

    ncmp = seq // CMP_STRIDE
    ccs, cs1, cs2 = (jnp.asarray(a) for a in _rope_tables(np.arange(ncmp) * CMP_STRIDE + CMP_LEN - 1))
    kcd, vct = _nsa_compress(
        kc.reshape(batch * ncmp, CMP_STRIDE * LANES), vc.reshape(batch * ncmp, CMP_STRIDE * LANES),
        _compress_params(ck_pos, ck_w1, ck_w2), _compress_params(cv_pos, cv_w1, cv_w2),
        ccs, cs1, cs2, kn_rows, batch, ncmp)

    ovt = jnp.asarray(_overlap_matrix(seq).T).astype(BF16)
    yb = _nsa_attn(proj, qt, gtt, ka, vst, kwd, vwt, kcd, vct, ovt, batch, seq)

    w_o = w_out.astype(BF16)
    return x2d, [(ya, w_o[:M_W]), (yb, w_o[M_W:])]


def _odd_layer(x2d, res, norm_w, w_in, b_in, lb, head_norm, w_out, batch, seq):
    outs = _norm_proj(x2d, norm_w, w_in.astype(BF16), b_in, (0, H_W), tn=H_W, res=res)
    if res:
        x2d = outs[0]
    proj, fgate = outs[-2:]
    y = _hgrn(proj, fgate, lb.reshape(1, H_W), head_norm.reshape(1, H_W), batch, seq)
    return x2d, [(y, w_out.astype(BF16))]


def kernel(x, norm_w, ev_w_in, ev_b_in, mlstm_f_bias, mlstm_conv_w, mlstm_conv_b, mlstm_head_norm, nsa_q_norm, nsa_k_norm, cmp_k_pos, cmp_k_w1, cmp_k_w2, cmp_v_pos, cmp_v_w1, cmp_v_w2, ev_w_out, od_w_in, od_b_in, hgrn_lb_logits, hgrn_head_norm, od_w_out):
    batch, seq, d = x.shape
    depth = norm_w.shape[0]
    lbs = jnp.cumsum(jax.nn.softmax(hgrn_lb_logits.astype(F32), axis=0), axis=0)
    x2d = x.reshape(batch * seq, d)
    res = []
    for layer in range(depth):
        j = layer // 2
        if layer % 2 == 0:
            x2d, res = _even_layer(x2d, res, norm_w[layer], ev_w_in[j], ev_b_in[j], mlstm_f_bias[j],
                                   mlstm_conv_w[j], mlstm_conv_b[j], mlstm_head_norm[j], nsa_q_norm[j],
                                   nsa_k_norm[j], cmp_k_pos[j], cmp_k_w1[j], cmp_k_w2[j], cmp_v_pos[j],
                                   cmp_v_w1[j], cmp_v_w2[j], ev_w_out[j], batch, seq)
        else:
            x2d, res = _odd_layer(x2d, res, norm_w[layer], od_w_in[j], od_b_in[j], lbs[j], hgrn_head_norm[j],
                                  od_w_out[j], batch, seq)
    if len(res) == 2:
        x2d = _out_proj2(res[0][0], res[1][0], res[0][1], res[1][1], x2d)
    elif res:
        x2d = _out_proj1(res[0][0], res[0][1], x2d)
    return x2d.reshape(batch, seq, d)
```

```python
import functools

import numpy as np
import jax
import jax.numpy as jnp
from jax import lax
from jax.experimental import pallas as pl
from jax.experimental.pallas import tpu as pltpu

F32 = jnp.float32
BF16 = jnp.bfloat16

LANES = 128
SUBLANES = 8
BF16_SUBLANES = 16
EPS = 1e-6
NEG = -1e30
VMEM_LIMIT_BYTES = 56 * 1024 * 1024

M_HEADS = 4
M_DIM = 128
M_W = M_HEADS * M_DIM
M_CHUNK = 128
CONV_K = 4
N_HEADS = 8
N_GROUPS = 2
N_HPG = N_HEADS // N_GROUPS
N_DIM = 64
N_W = N_HEADS * N_DIM
N_KV_W = N_GROUPS * N_DIM
CMP_LEN = 32
CMP_STRIDE = 16
CMP_HIDDEN = 128
CMP_CHUNK = 128
SEL_BLOCK = 64
SEL_SHIFT = 6
SEL_TOPK = 16
WINDOW = 512
Q_BLOCK = 128
ROPE_THETA = 500000.0
ROPE_DIM = N_DIM // 4
ROPE_HALF = ROPE_DIM // 2
KV_TILE = 512
V_ROWS = 80
LOG2E = 1.4426950408889634
H_HEADS = 8
H_DIM = 128
H_W = H_HEADS * H_DIM
H_TILE = 128
H_LAG = 2

C_MQ, C_MK, C_MV, C_MO, C_MZ = 0, 512, 1024, 1536, 2048
C_NQ = 2560
C_KC, C_VC, C_KS, C_VS = 3072, 3200, 3328, 3456
C_KW, C_VW, C_SMALL = 3584, 3712, 3840
C_NZ = 4096
EV_COLS = 4608
SM_I, SM_F, SM_G = 0, 4, 8


def _cparams(*sem):
    return pltpu.CompilerParams(dimension_semantics=sem, vmem_limit_bytes=VMEM_LIMIT_BYTES)


def _dot(a, b):
    return jnp.dot(a, b, preferred_element_type=F32)


def _dot_nt(a, b):
    return lax.dot_general(a, b, (((1,), (1,)), ((), ())), preferred_element_type=F32)


def _dot_tn(a, b):
    return lax.dot_general(a, b, (((0,), (0,)), ((), ())), preferred_element_type=F32)


def _sigmoid(x):
    return 1.0 / (1.0 + jnp.exp(-x))


def _silu(x):
    return x * _sigmoid(x)


def _split3(x):
    hi = x.astype(BF16)
    r = x - hi.astype(F32)
    mid = r.astype(BF16)
    lo = (r - mid.astype(F32)).astype(BF16)
    return hi, mid, lo


def _dot01(m01, x):
    hi, mid, lo = _split3(x)
    return _dot(m01, hi) + _dot(m01, mid) + _dot(m01, lo)


def _norm_proj_kernel(*refs, n_res, g_tiles, g_off):
    x_ref = refs[0]
    y_refs, wy_refs = refs[1:1 + n_res], refs[1 + n_res:1 + 2 * n_res]
    nw_ref, w_ref, b_ref = refs[1 + 2 * n_res:4 + 2 * n_res]
    outs = refs[4 + 2 * n_res:]
    xo_ref = outs[0] if n_res else None
    o_ref, g_ref, h_ref = outs[-3:]
    j = pl.program_id(1)

    @pl.when(j == 0)
    def _():
        x = x_ref[...]
        for y_ref, wy_ref in zip(y_refs, wy_refs):
            x = x + _dot(y_ref[...], wy_ref[...])
        if n_res:
            xo_ref[...] = x
        ms = jnp.mean(x * x, axis=-1, keepdims=True)
        h_ref[...] = (x * lax.rsqrt(ms + EPS) * nw_ref[...]).astype(BF16)

    acc = _dot(h_ref[...], w_ref[...]) + b_ref[...]
    o_ref[...] = acc.astype(o_ref.dtype)

    @pl.when((j >= g_tiles[0]) & (j < g_tiles[1]))
    def _():
        g_ref[...] = acc[:, g_off:g_off + g_ref.shape[1]]


def _norm_proj(x2d, norm_w, w_bf16, bias, g_cols, tn, res=(), tm=1024):
    t, d = x2d.shape
    n = w_bf16.shape[1]
    g_w = min(g_cols[1] - g_cols[0], tn)
    g_tiles = (g_cols[0] // tn, -(-g_cols[1] // tn))
    row = lambda w: pl.BlockSpec((tm, w), lambda i, j: (i, 0))
    const = lambda a: pl.BlockSpec(a.shape, lambda i, j: (0, 0))
    ys, wys = [y for y, _ in res], [wy for _, wy in res]
    out_specs = [pl.BlockSpec((tm, tn), lambda i, j: (i, j)),
                 pl.BlockSpec((tm, g_w), lambda i, j: (i, jnp.clip(j - g_tiles[0], 0, g_tiles[1] - g_tiles[0] - 1)))]
    out_shape = [jax.ShapeDtypeStruct((t, n), BF16), jax.ShapeDtypeStruct((t, g_cols[1] - g_cols[0]), F32)]
    if res:
        out_specs = [row(d)] + out_specs
        out_shape = [jax.ShapeDtypeStruct((t, d), F32)] + out_shape
    return pl.pallas_call(
        functools.partial(_norm_proj_kernel, n_res=len(res), g_tiles=g_tiles, g_off=g_cols[0] % tn),
        grid=(t // tm, n // tn),
        in_specs=([row(d)] + [row(y.shape[1]) for y in ys] + [const(wy) for wy in wys]
                  + [pl.BlockSpec((1, d), lambda i, j: (0, 0)),
                     pl.BlockSpec((d, tn), lambda i, j: (0, j)),
                     pl.BlockSpec((1, tn), lambda i, j: (0, j))]),
        out_specs=out_specs,
        out_shape=out_shape,
        scratch_shapes=[pltpu.VMEM((tm, d), BF16)],
        compiler_params=_cparams("parallel", "arbitrary"),
        name="norm_proj",
    )(x2d, *ys, *wys, norm_w.reshape(1, d), w_bf16, bias.reshape(1, n))


def _out_proj2_kernel(ya_ref, yb_ref, wa_ref, wb_ref, x_ref, o_ref):
    o_ref[...] = x_ref[...] + _dot(ya_ref[...], wa_ref[...]) + _dot(yb_ref[...], wb_ref[...])


def _out_proj2(ya, yb, wa, wb, x2d, tm=1024):
    t, d = x2d.shape
    ka, kb = ya.shape[1], yb.shape[1]
    return pl.pallas_call(
        _out_proj2_kernel,
        grid=(t // tm,),
        in_specs=[
            pl.BlockSpec((tm, ka), lambda i: (i, 0)),
            pl.BlockSpec((tm, kb), lambda i: (i, 0)),
            pl.BlockSpec((ka, d), lambda i: (0, 0)),
            pl.BlockSpec((kb, d), lambda i: (0, 0)),
            pl.BlockSpec((tm, d), lambda i: (i, 0)),
        ],
        out_specs=pl.BlockSpec((tm, d), lambda i: (i, 0)),
        out_shape=jax.ShapeDtypeStruct((t, d), F32),
        compiler_params=_cparams("parallel"),
        name="out_proj2",
    )(ya, yb, wa, wb, x2d)


def _out_proj1_kernel(y_ref, w_ref, x_ref, o_ref):
    o_ref[...] = x_ref[...] + _dot(y_ref[...], w_ref[...])


def _out_proj1(y, w, x2d, tm=1024):
    t, d = x2d.shape
    k = y.shape[1]
    return pl.pallas_call(
        _out_proj1_kernel,
        grid=(t // tm,),
        in_specs=[
            pl.BlockSpec((tm, k), lambda i: (i, 0)),
            pl.BlockSpec((k, d), lambda i: (0, 0)),
            pl.BlockSpec((tm, d), lambda i: (i, 0)),
        ],
        out_specs=pl.BlockSpec((tm, d), lambda i: (i, 0)),
        out_shape=jax.ShapeDtypeStruct((t, d), F32),
        compiler_params=_cparams("parallel"),
        name="out_proj1",
    )(y, w, x2d)


def _mlstm_kernel(q_ref, k_ref, v_ref, o_ref, z_ref, sm_ref, cw_ref, cb_ref, fb_ref, hn_ref,
                  out_ref, xbuf, c_ref, n_ref, m_ref):
    L, D = M_CHUNK, M_DIM
    NB = q_ref.shape[0]
    XR = xbuf.shape[1]
    X0 = BF16_SUBLANES

    @pl.when(pl.program_id(0) == 0)
    def _():
        xbuf[...] = jnp.zeros_like(xbuf)
        c_ref[...] = jnp.zeros_like(c_ref)
        n_ref[...] = jnp.zeros_like(n_ref)
        m_ref[...] = jnp.zeros_like(m_ref)

    t_i = lax.broadcasted_iota(jnp.int32, (L, XR), 0)
    r_i = lax.broadcasted_iota(jnp.int32, (L, XR), 1)
    row = lax.broadcasted_iota(jnp.int32, (L, L), 0)
    col = lax.broadcasted_iota(jnp.int32, (L, L), 1)
    causal = col <= row
    lane = lax.broadcasted_iota(jnp.int32, (L, LANES), 1)
    is_f = (lane >= SM_F) & (lane < SM_F + M_HEADS)

    heads = []
    for b in range(NB):
        xbuf[b, X0:X0 + L, 0:M_W] = q_ref[b]
        xbuf[b, X0:X0 + L, M_W:2 * M_W] = k_ref[b]
        x2 = xbuf[b]
        acc = cb_ref[...] + cw_ref[CONV_K - 1:CONV_K, :] * x2[X0:X0 + L].astype(F32)
        for d in range(1, CONV_K):
            shift = jnp.where(r_i == t_i + X0 - d, 1.0, 0.0).astype(BF16)
            acc = acc + cw_ref[CONV_K - 1 - d:CONV_K - d, :] * _dot(shift, x2)
        qk = _silu(acc)
        xbuf[b, 0:X0, :] = xbuf[b, L:L + X0, :]

        fpre = sm_ref[b] + fb_ref[...]
        gates = jnp.where(is_f, jnp.minimum(fpre, 0.0) - jnp.log(1.0 + jnp.exp(-jnp.abs(fpre))), fpre)
        bcols = _dot01(jnp.where(causal, 1.0, 0.0).astype(BF16), gates)
        gates_t = gates.T
        bcols_t = bcols.T
        for h in range(M_HEADS):
            hs = slice(h * D, (h + 1) * D)
            q = qk[:, hs]
            k = qk[:, M_W + h * D:M_W + (h + 1) * D] * (D ** -0.5)
            i_col = gates[:, SM_I + h:SM_I + h + 1]
            i_row = gates_t[SM_I + h:SM_I + h + 1, :]
            bcum_col = bcols[:, SM_F + h:SM_F + h + 1]
            bcum_row = bcols_t[SM_F + h:SM_F + h + 1, :]
            m_prev = m_ref[b * M_HEADS + h]
            dmat = jnp.where(causal, bcum_col - bcum_row + i_row, -jnp.inf)
            inter = bcum_col + m_prev
            m_row = jnp.maximum(jnp.max(dmat, axis=1, keepdims=True), inter)
            b_last = bcum_col[L - 1:L, :]
            dl = b_last - bcum_col + i_col
            m_new = jnp.maximum(b_last + m_prev, jnp.max(dl, axis=0, keepdims=True))
            wkk = jnp.exp(dl - m_new) * k
            heads.append(dict(
                b=b, hs=hs, st=b * M_HEADS + h, q=q, qb=q.astype(BF16), kb=k.astype(BF16),
                vb=v_ref[b, :, hs].astype(BF16), wkk=wkk,
                e=jnp.exp(dmat - m_row), w_inter=jnp.exp(inter - m_row), floor=jnp.exp(-m_row),
                decay=jnp.exp(b_last + m_prev - m_new), m_new=m_new))
    s_qk = [_dot_nt(t["qb"], t["kb"]) for t in heads]
    s_qc = [_dot(t["qb"], c_ref[t["st"]].astype(BF16)) for t in heads]
    s_kv = [_dot_tn(t["wkk"].astype(BF16), t["vb"]) for t in heads]
    ones_b = jnp.ones((L, LANES), BF16)
    smats = [(s_qk[j] * t["e"]).astype(BF16) for j, t in enumerate(heads)]
    s_sv = [_dot(smats[j], t["vb"]) for j, t in enumerate(heads)]
    s_ds = [_dot(smats[j], ones_b) for j in range(len(heads))]
    s_qn = [_dot((t["q"] * n_ref[t["st"]]).astype(BF16), ones_b) for t in heads]
    hhs = []
    for j, t in enumerate(heads):
        st = t["st"]
        num = s_sv[j] + t["w_inter"] * s_qc[j]
        den = s_ds[j] + t["w_inter"] * s_qn[j]
        hhs.append(num / jnp.maximum(jnp.abs(den), t["floor"]))
        c_ref[st] = t["decay"] * c_ref[st] + s_kv[j]
        n_ref[st] = t["decay"] * n_ref[st] + jnp.sum(t["wkk"], axis=0, keepdims=True)
        m_ref[st] = t["m_new"]
    s_ms = [_dot((hh * hh).astype(BF16), ones_b) for hh in hhs]
    for j, t in enumerate(heads):
        b, hs = t["b"], t["hs"]
        hn = hhs[j] * lax.rsqrt(s_ms[j] * (1.0 / D) + EPS) * hn_ref[:, hs]
        gate = _sigmoid(o_ref[b, :, hs].astype(F32)) * _silu(z_ref[b, :, hs].astype(F32))
        out_ref[b, :, hs] = (hn * gate).astype(out_ref.dtype)


def _mlstm(proj, small, conv_w, conv_b, f_bias_row, head_norm, batch, seq):
    L = M_CHUNK
    nc = seq // L
    p3 = proj.reshape(batch, seq, proj.shape[1])
    s3 = small.reshape(batch, seq, LANES)
    cb = lambda off: pl.BlockSpec((batch, L, M_W), lambda c, off=off: (0, c, off // M_W))
    full = lambda r, w: pl.BlockSpec((r, w), lambda c: (0, 0))
    out = pl.pallas_call(
        _mlstm_kernel,
        grid=(nc,),
        in_specs=[
            cb(C_MQ), cb(C_MK), cb(C_MV), cb(C_MO), cb(C_MZ),
            pl.BlockSpec((batch, L, LANES), lambda c: (0, c, 0)),
            full(CONV_K, 2 * M_W), full(1, 2 * M_W), full(1, LANES), full(1, M_W),
        ],
        out_specs=pl.BlockSpec((batch, L, M_W), lambda c: (0, c, 0)),
        out_shape=jax.ShapeDtypeStruct((batch, seq, M_W), BF16),
        scratch_shapes=[
            pltpu.VMEM((batch, 2 * L, 2 * M_W), BF16),
            pltpu.VMEM((batch * M_HEADS, M_DIM, M_DIM), F32),
            pltpu.VMEM((batch * M_HEADS, 1, M_DIM), F32),
            pltpu.VMEM((batch * M_HEADS, 1, 1), F32),
        ],
        compiler_params=_cparams("arbitrary"),
        name="mlstm",
    )(p3, p3, p3, p3, p3, s3, conv_w, conv_b.reshape(1, -1), f_bias_row, head_norm.reshape(1, -1))
    return out.reshape(batch * seq, M_W)


def _group_rms(x, w_row):
    r = lax.broadcasted_iota(jnp.int32, (LANES, LANES), 0)
    c = lax.broadcasted_iota(jnp.int32, (LANES, LANES), 1)
    same_head = jnp.where((r < N_DIM) == (c < N_DIM), 1.0, 0.0).astype(BF16)
    ms = _dot_hilo(x * x, same_head) * (1.0 / N_DIM)
    return x * lax.rsqrt(ms + EPS) * w_row


def _lane_gather01(src_lane, width=LANES):
    r = lax.broadcasted_iota(jnp.int32, (LANES, width), 0)
    c = lax.broadcasted_iota(jnp.int32, (LANES, width), 1)
    return jnp.where(r == src_lane(c), 1.0, 0.0).astype(BF16)


def _dot_hilo(x, m01):
    hi = x.astype(BF16)
    lo = (x - hi.astype(F32)).astype(BF16)
    return _dot(hi, m01) + _dot(lo, m01)


def _rope(x, cs, s1, s2):
    return x * cs + pltpu.roll(x, ROPE_HALF, 1) * s1 + pltpu.roll(x, LANES - ROPE_HALF, 1) * s2


def _dup_halves(x):
    dup = _lane_gather01(lambda c: jnp.where(c < LANES, 0, N_DIM) + (c & (N_DIM - 1)), 2 * LANES)
    both = _dot(x, dup).astype(BF16)
    return both[:, 0:LANES], both[:, LANES:2 * LANES]


def _transpose01(x):
    return _dot_nt(_lane_gather01(lambda c: c), x)


def _nsa_prep_kernel(q_ref, kv_ref, kw_ref, sm_ref, cs_ref, s1_ref, s2_ref, qn_ref, kn_ref,
                     qt_ref, ka_ref, kwd_ref, vst_ref, vwt_ref, kc_ref, vc_ref, gtt_ref,
                     *, tp, seq):
    cs, s1, s2 = cs_ref[...], s1_ref[...], s2_ref[...]
    nq = N_W // LANES
    kv = kv_ref[...].astype(F32)
    kwb = kw_ref[...].astype(F32)
    kc_ref[...] = kv[:, 0:LANES]
    vc_ref[...] = kv[:, LANES:2 * LANES]
    slabs = [q_ref[:, j * LANES:(j + 1) * LANES].astype(F32) for j in range(nq)]
    slabs += [kv[:, 2 * LANES:3 * LANES], kwb[:, 0:LANES]]
    gains = [qn_ref[...]] * nq + [kn_ref[1:2, :], kn_ref[2:3, :]]
    normed = [_group_rms(x, w) for x, w in zip(slabs, gains)]
    roped = [_rope(x, cs, s1, s2) for x in normed]
    qs = [(x * (N_DIM ** -0.5 * LOG2E)).astype(BF16) for x in roped[:nq]]
    k0, k1 = _dup_halves(roped[nq].astype(BF16))
    w0, w1 = _dup_halves(roped[nq + 1].astype(BF16))
    gates = _sigmoid(sm_ref[...])
    g_hi = gates.astype(BF16)
    g_lo = (gates - g_hi.astype(F32)).astype(BF16)
    qts = [_transpose01(x) for x in qs]
    vst = _transpose01(kv_ref[:, 3 * LANES:4 * LANES])
    vwt = _transpose01(kw_ref[:, LANES:2 * LANES])
    gtt_ref[...] = _transpose01(g_hi) + _transpose01(g_lo)
    for j in range(nq):
        qt_ref[j * LANES:(j + 1) * LANES, :] = qts[j].astype(BF16)
    pos0 = (pl.program_id(0) * tp) % seq
    pos = pos0 + lax.broadcasted_iota(jnp.int32, (tp, LANES), 0)
    blk = lax.broadcasted_iota(jnp.int32, (tp, LANES), 1)
    onehot = jnp.where(jnp.right_shift(pos, SEL_SHIFT) == blk, 1.0, 0.0).astype(BF16)
    ka_ref[:, 0:LANES] = k0
    ka_ref[:, LANES:2 * LANES] = onehot
    ka_ref[:, 2 * LANES:3 * LANES] = k1
    ka_ref[:, 3 * LANES:4 * LANES] = onehot
    fill =(lax.broadcasted_iota(jnp.int32, (V_ROWS - N_DIM, Q_BLOCK), 0) == 0).astype(BF16)
    for c in range(tp // Q_BLOCK):
        for g in range(N_GROUPS):
            vst_ref[c, g * V_ROWS:g * V_ROWS + N_DIM, :] = (
                vst[g * N_DIM:(g + 1) * N_DIM, c * Q_BLOCK:(c + 1) * Q_BLOCK].astype(BF16))
            vst_ref[c, g * V_ROWS + N_DIM:(g + 1) * V_ROWS, :] = fill
    kwd_ref[:, 0:LANES] = w0
    kwd_ref[:, LANES:2 * LANES] = w1
    vwt_ref[...] = vwt.astype(BF16)


def _nsa_prep(proj, small, cs, s1, s2, qn_row, kn_rows, batch, seq):
    tp = KV_TILE
    t = batch * seq
    nsb = seq // tp
    rb = lambda w, off: pl.BlockSpec((tp, w), lambda i, off=off, w=w: (i, off // w))
    tab = pl.BlockSpec((tp, LANES), lambda i: (i % nsb, 0))
    o = lambda w: pl.BlockSpec((tp, w), lambda i: (i, 0))
    ot = lambda r: pl.BlockSpec((r, tp), lambda i: (0, i))
    sds = lambda w, dt: jax.ShapeDtypeStruct((t, w), dt)
    return pl.pallas_call(
        functools.partial(_nsa_prep_kernel, tp=tp, seq=seq),
        grid=(t // tp,),
        in_specs=[rb(512, C_NQ), rb(512, C_KC), rb(512, C_KW), o(LANES), tab, tab, tab,
                  pl.BlockSpec((1, LANES), lambda i: (0, 0)), pl.BlockSpec((3, LANES), lambda i: (0, 0))],
        out_specs=[ot(N_W), o(512), o(256),
                   pl.BlockSpec((tp // Q_BLOCK, N_GROUPS * V_ROWS, Q_BLOCK), lambda i: (i, 0, 0)),
                   ot(LANES), o(128), o(128), ot(LANES)],
        out_shape=[jax.ShapeDtypeStruct((N_W, t), BF16), sds(512, BF16), sds(256, BF16),
                   jax.ShapeDtypeStruct((t // Q_BLOCK, N_GROUPS * V_ROWS, Q_BLOCK), BF16),
                   jax.ShapeDtypeStruct((LANES, t), BF16),
                   sds(128, F32), sds(128, F32), jax.ShapeDtypeStruct((LANES, t), F32)],
        compiler_params=_cparams("parallel"),
        name="nsa_prep",
    )(proj, proj, proj, small, cs, s1, s2, qn_row, kn_rows)


def _gelu_tanh(x):
    return 0.5 * x * (1.0 + jnp.tanh(0.7978845608028654 * (x + 0.044715 * x * x * x)))


def _compress_mlp(x_ref, pa_ref, pb_ref, wa_ref, wb_ref, w2_ref, hb_ref):
    n = x_ref.shape[0]
    x = x_ref[...]
    ha = _dot((x + pa_ref[...]).astype(BF16), wa_ref[...])
    hb_ref[0:n, :] = _dot((x + pb_ref[...]).astype(BF16), wb_ref[...])
    hb_ref[n:n + SUBLANES, :] = jnp.zeros((SUBLANES, hb_ref.shape[1]), F32)
    hid = ha + hb_ref[1:n + 1, :]
    return _dot(_gelu_tanh(hid).astype(BF16), w2_ref[...])


def _nsa_compress_kernel(kx_ref, vx_ref, kpa_ref, kpb_ref, vpa_ref, vpb_ref, kwa_ref, kwb_ref, kw2_ref,
                         vwa_ref, vwb_ref, vw2_ref, cs_ref, s1_ref, s2_ref, kn_ref,
                         kcd_ref, vct_ref, hb_ref):
    n = kx_ref.shape[0]
    rowi = lax.broadcasted_iota(jnp.int32, (n, LANES), 0)
    live = rowi < n - 1
    kc = _compress_mlp(kx_ref, kpa_ref, kpb_ref, kwa_ref, kwb_ref, kw2_ref, hb_ref)
    kc = _rope(_group_rms(kc, kn_ref[0:1, :]), cs_ref[...], s1_ref[...], s2_ref[...])
    kc = jnp.where(live, kc, 0.0)
    c0, c1 = _dup_halves(kc.astype(BF16))
    kcd_ref[:, 0:LANES] = c0
    kcd_ref[:, LANES:2 * LANES] = c1
    vc = _compress_mlp(vx_ref, vpa_ref, vpb_ref, vwa_ref, vwb_ref, vw2_ref, hb_ref)
    vc = jnp.where(live, vc, 0.0)
    vct_ref[0] = vc.T.astype(BF16)


def _nsa_compress(kx, vx, kp, vp, cs, s1, s2, kn_rows, batch, nrow):
    wide = CMP_STRIDE * LANES
    full = lambda a: pl.BlockSpec(a.shape, lambda b: (0,) * a.ndim)
    xs = pl.BlockSpec((nrow, wide), lambda b: (b, 0))
    args = (kx, vx, kp[0], kp[1], vp[0], vp[1], kp[2], kp[3], kp[4], vp[2], vp[3], vp[4], cs, s1, s2, kn_rows)
    return pl.pallas_call(
        _nsa_compress_kernel,
        grid=(batch,),
        in_specs=[xs, xs] + [full(a) for a in args[2:]],
        out_specs=[pl.BlockSpec((nrow, 2 * LANES), lambda b: (b, 0)),
                   pl.BlockSpec((1, LANES, nrow), lambda b: (b, 0, 0))],
        out_shape=[jax.ShapeDtypeStruct((batch * nrow, 2 * LANES), BF16),
                   jax.ShapeDtypeStruct((batch, LANES, nrow), BF16)],
        scratch_shapes=[pltpu.VMEM((nrow + SUBLANES, 2 * CMP_HIDDEN), F32)],
        compiler_params=_cparams("parallel"),
        name="nsa_compress",
    )(*args)


def _compress_params(pos, w1, w2):
    def expand(w1_half, pos_half):
        w = w1_half.reshape(CMP_STRIDE, N_DIM, CMP_HIDDEN)
        z = jnp.zeros_like(w)
        g0 = jnp.concatenate([w, z], axis=-1)
        g1 = jnp.concatenate([z, w], axis=-1)
        wx = jnp.stack([g0, g1], axis=1).reshape(CMP_STRIDE * LANES, 2 * CMP_HIDDEN)
        px = jnp.concatenate([pos_half, pos_half], axis=-1).reshape(1, CMP_STRIDE * LANES)
        return wx.astype(BF16), px.astype(F32)

    half = CMP_STRIDE * N_DIM
    wa, pa = expand(w1[:half], pos[:CMP_STRIDE])
    wb, pb = expand(w1[half:], pos[CMP_STRIDE:])
    z2 = jnp.zeros_like(w2)
    w2x = jnp.concatenate([jnp.concatenate([w2, z2], axis=1), jnp.concatenate([z2, w2], axis=1)], axis=0)
    return pa, pb, wa, wb, w2x.astype(BF16)


def _nsa_attn_kernel(qt_ref, gtt_ref, z_ref, ka_ref, vst_ref,
                     kw0, kw1, kw2, kw3, kw4, wt0, wt1, wt2, wt3, wt4,
                     kcd_ref, vct_ref, ovt_ref, out_ref, qa_ref, acc_ref, *, ncmp):
    QB = Q_BLOCK
    NP = N_HEADS // 2
    PW = 2 * QB
    i = pl.program_id(1)
    t0 = i * QB
    kw_refs = (kw0, kw1, kw2, kw3, kw4)
    vwt_refs = (wt0, wt1, wt2, wt3, wt4)
    nwt = len(kw_refs)

    sub = lax.broadcasted_iota(jnp.int32, (QB, QB), 0)
    lane = lax.broadcasted_iota(jnp.int32, (QB, QB), 1)
    top = sub < N_DIM
    t_row = t0 + lax.broadcasted_iota(jnp.int32, (1, QB), 1)
    t_row2 = jnp.concatenate([t_row, t_row], axis=1)

    def pair_q(pr):
        slab = qt_ref[pr * LANES:(pr + 1) * LANES, :]
        zero = jnp.zeros_like(slab)
        return jnp.concatenate([jnp.where(top, slab, zero), jnp.where(top, zero, slab)], axis=1)

    blk_f = sub.astype(F32)
    cur = jnp.right_shift(t_row, SEL_SHIFT)
    grp = lambda pr: pr // (N_HPG // 2)
    for pr in range(NP):
        qa_ref[pr, 0:LANES, :] = pair_q(pr)

    def compressed(n):
        nidx = lax.broadcasted_iota(jnp.int32, (n, PW), 0)
        ok = (nidx * CMP_STRIDE + (CMP_LEN - 1) <= t_row2) & (nidx < ncmp - 1)
        any_ok = t_row2 >= CMP_LEN - 1
        s_cmp = [_dot(kcd_ref[0:n, grp(pr) * LANES:(grp(pr) + 1) * LANES], qa_ref[pr, 0:LANES, :])
                 for pr in range(NP)]
        p_cmp = []
        for pr in range(NP):
            s = jnp.where(ok, s_cmp[pr], NEG)
            mx = jnp.max(s, axis=0, keepdims=True)
            p = jnp.exp2(s - mx)
            den = jnp.sum(p, axis=0, keepdims=True)
            p_cmp.append(p * jnp.where(any_ok, 1.0 / den, 0.0))
        outs = [_dot(vct_ref[0, grp(pr) * N_DIM:(grp(pr) + 1) * N_DIM, 0:n], p_cmp[pr].astype(BF16))
                for pr in range(NP)]
        for g in range(N_GROUPS):
            psum = jnp.zeros((n, QB), F32)
            for pp in range(N_HPG // 2):
                p = p_cmp[g * (N_HPG // 2) + pp]
                psum = psum + p[:, 0:QB] + p[:, QB:PW]
            p_hi = psum.astype(BF16)
            p_lo = (psum - p_hi.astype(F32)).astype(BF16)
            outs.append(_dot(ovt_ref[:, 0:n], p_hi) + _dot(ovt_ref[:, 0:n], p_lo))
        return tuple(outs)

    n_opts = list(range(CMP_CHUNK, ncmp + 1, CMP_CHUNK))
    need = (t0 + QB - CMP_LEN) // CMP_STRIDE + 1
    outs = lax.switch(jnp.clip((need - 1) // CMP_CHUNK, 0, len(n_opts) - 1),
                      [functools.partial(compressed, n) for n in n_opts])
    o_cmp, imps = outs[:NP], outs[NP:]
    forced = (sub == 0) | (sub == cur) | (sub == cur - 1)
    n_rounds = SEL_TOPK - 3
    valid = sub <= cur
    works0 = [jnp.where(valid, jnp.where(forced, -jnp.inf, imp), NEG) for imp in imps]

    def rounds(works, exact):
        sels = [forced for _ in range(N_GROUPS)]
        for _ in range(n_rounds):
            for g in range(N_GROUPS):
                mx = jnp.max(works[g], axis=0, keepdims=True)
                if exact:
                    idx = jnp.min(jnp.where(works[g] == mx, blk_f, float(LANES)), axis=0, keepdims=True)
                    pick = blk_f == idx
                else:
                    pick = works[g] == mx
                sels[g] = sels[g] | pick
                works[g] = jnp.where(pick, -jnp.inf, works[g])
        return tuple(jnp.where(s & valid, 0.0, NEG) for s in sels)

    bias = rounds(list(works0), exact=False)
    picked = [jnp.sum(jnp.where((b == 0.0) & jnp.logical_not(forced), 1.0, 0.0), axis=0, keepdims=True)
              for b in bias]
    tied = jnp.max(functools.reduce(jnp.maximum, picked)) > n_rounds
    bias = lax.cond(tied, lambda: rounds(list(works0), exact=True), lambda: bias)
    for g in range(N_GROUPS):
        neg = bias[g].astype(BF16)
        neg2 = jnp.concatenate([neg, neg], axis=1)
        for pp in range(N_HPG // 2):
            qa_ref[g * (N_HPG // 2) + pp, LANES:2 * LANES, :] = neg2

    CH = QB
    above = jnp.concatenate([sub > lane, sub > lane], axis=1)

    def sel_scores(cj):
        k0 = pl.multiple_of(cj * CH, CH)
        return [_dot(ka_ref[pl.ds(k0, CH), 2 * grp(pr) * LANES:2 * (grp(pr) + 1) * LANES], qa_ref[pr])
                for pr in range(NP)]

    def sel_softmax(ss, ms, masked):
        new, alphas, ps = [], [], []
        for pr in range(NP):
            s = jnp.where(above, NEG, ss[pr]) if masked else ss[pr]
            m_new = jnp.maximum(ms[pr], jnp.max(s, axis=0, keepdims=True))
            alphas.append(jnp.exp2(ms[pr] - m_new))
            ps.append(jnp.exp2(s - m_new).astype(BF16))
            new.append(m_new)
        return tuple(new), alphas, ps

    def sel_values(cj, alphas, ps):
        for pr in range(NP):
            g = grp(pr)
            acc_ref[pr] = alphas[pr] * acc_ref[pr] + _dot(vst_ref[cj, g * V_ROWS:(g + 1) * V_ROWS, :], ps[pr])

    def sel_chunk(cj, ms, masked):
        ms, alphas, ps = sel_softmax(sel_scores(cj), ms, masked)
        sel_values(cj, alphas, ps)
        return ms

    def sel_tile(jt, ms, nc):
        c0 = jt * nc
        k0 = pl.multiple_of(c0 * CH, CH)
        big = [_dot(ka_ref[pl.ds(k0, nc * CH), 2 * grp(pr) * LANES:2 * (grp(pr) + 1) * LANES], qa_ref[pr])
               for pr in range(NP)]
        for c in range(nc):
            ss = [b[c * CH:(c + 1) * CH] for b in big]
            ms, alphas, ps = sel_softmax(ss, ms, False)
            sel_values(c0 + c, alphas, ps)
        return ms

    acc_ref[...] = jnp.zeros_like(acc_ref)
    ms = tuple(jnp.full((1, PW), 2.0 * NEG, F32) for _ in range(NP))
    done = 0
    for nc in (4 * KV_TILE // CH, 2 * KV_TILE // CH, KV_TILE // CH):
        n_run = (i - done) // nc
        ms = lax.fori_loop(done // nc, done // nc + n_run, functools.partial(sel_tile, nc=nc), ms)
        done = done + n_run * nc
    ms = lax.fori_loop(done, i, lambda cj, ms: sel_chunk(cj, ms, False), ms)
    s_diag = sel_scores(i)
    s_win = [[_dot(kw_refs[c][:, grp(pr) * LANES:(grp(pr) + 1) * LANES], qa_ref[pr, 0:LANES, :])
              for c in range(nwt)] for pr in range(NP)]
    ms, alphas, ps_diag = sel_softmax(s_diag, ms, True)
    p_win, d_win = [], []
    for pr in range(NP):
        ss = []
        for c in range(nwt):
            ok = (i - (nwt - 1) + c) >= 0
            if c == 0:
                ok = ok & above
            elif c == nwt - 1:
                ok = ok & jnp.logical_not(above)
            ss.append(jnp.where(ok, s_win[pr][c], NEG))
        mx = ss[0].max(axis=0, keepdims=True)
        for c in range(1, nwt):
            mx = jnp.maximum(mx, ss[c].max(axis=0, keepdims=True))
        ps = [jnp.exp2(ss[c] - mx) for c in range(nwt)]
        den = ps[0].sum(axis=0, keepdims=True)
        for c in range(1, nwt):
            den = den + ps[c].sum(axis=0, keepdims=True)
        p_win.append([p.astype(BF16) for p in ps])
        d_win.append(den)
    sel_values(i, alphas, ps_diag)
    o_win = []
    for pr in range(NP):
        g = grp(pr)
        acc = _dot(vwt_refs[0][g * N_DIM:(g + 1) * N_DIM, :], p_win[pr][0])
        for c in range(1, nwt):
            acc = acc + _dot(vwt_refs[c][g * N_DIM:(g + 1) * N_DIM, :], p_win[pr][c])
        o_win.append(acc * (1.0 / d_win[pr]))
    o_sel = [acc_ref[pr, 0:N_DIM, :] * (1.0 / acc_ref[pr, N_DIM:N_DIM + 1, :]) for pr in range(NP)]

    for pr in range(NP):
        halves = []
        for par in range(2):
            h = 2 * pr + par
            cs = slice(par * QB, (par + 1) * QB)
            gc = gtt_ref[SM_G + 3 * h:SM_G + 3 * h + 1, :]
            gs = gtt_ref[SM_G + 3 * h + 1:SM_G + 3 * h + 2, :]
            gw = gtt_ref[SM_G + 3 * h + 2:SM_G + 3 * h + 3, :]
            halves.append(gc * o_cmp[pr][:, cs] + gs * o_sel[pr][:, cs] + gw * o_win[pr][:, cs])
        y = jnp.concatenate(halves, axis=0).T
        sl = slice(pr * LANES, (pr + 1) * LANES)
        out_ref[:, sl] = (y * _silu(z_ref[:, sl].astype(F32))).astype(out_ref.dtype)


def _nsa_attn(proj, qt, gtt, ka, vst, kwd, vwt, kcd, vct, ovt, batch, seq):
    QB = Q_BLOCK
    nqb = seq // QB
    ncmp = seq // CMP_STRIDE
    nwt = WINDOW // QB + 1
    row = lambda w: pl.BlockSpec((QB, w), lambda b, i: (b * nqb + i, 0))
    colb = lambda r: pl.BlockSpec((r, QB), lambda b, i: (0, b * nqb + i))
    wblk = lambda b, i, c: b * nqb + jnp.maximum(i - (nwt - 1) + c, 0)
    in_specs = ([colb(N_W), colb(LANES), pl.BlockSpec((QB, N_W), lambda b, i: (b * nqb + i, C_NZ // N_W)),
                 pl.BlockSpec((seq, 4 * LANES), lambda b, i: (b, 0)),
                 pl.BlockSpec((nqb, N_GROUPS * V_ROWS, QB), lambda b, i: (b, 0, 0))]
                + [pl.BlockSpec((QB, 2 * LANES), lambda b, i, c=c: (wblk(b, i, c), 0)) for c in range(nwt)]
                + [pl.BlockSpec((LANES, QB), lambda b, i, c=c: (0, wblk(b, i, c))) for c in range(nwt)]
                + [pl.BlockSpec((ncmp, 2 * LANES), lambda b, i: (b, 0)),
                   pl.BlockSpec((1, LANES, ncmp), lambda b, i: (b, 0, 0)),
                   pl.BlockSpec((LANES, ncmp), lambda b, i: (0, 0))])
    return pl.pallas_call(
        functools.partial(_nsa_attn_kernel, ncmp=ncmp),
        grid=(batch, nqb),
        in_specs=in_specs,
        out_specs=row(N_W),
        out_shape=jax.ShapeDtypeStruct((batch * seq, N_W), BF16),
        scratch_shapes=[pltpu.VMEM((N_HEADS // 2, 2 * LANES, 2 * QB), BF16),
                        pltpu.VMEM((N_HEADS // 2, V_ROWS, 2 * QB), F32)],
        compiler_params=_cparams("parallel", "arbitrary"),
        name="nsa_attn",
    )(qt, gtt, proj, ka, vst, *([kwd] * nwt), *([vwt] * nwt), kcd, vct, ovt)


def _hgrn_kernel(f_ref, i_ref, q_ref, z_ref, lb_ref, hn_ref, out_ref, st_ref):
    T, D = H_TILE, H_DIM
    W = H_W

    @pl.when(pl.program_id(1) == 0)
    def _():
        st_ref[...] = jnp.zeros_like(st_ref)

    lb = lb_ref[...]
    fp = f_ref[...]
    e = jnp.exp(-jnp.abs(fp))
    r = 1.0 / (1.0 + e)
    sig = jnp.where(fp >= 0, r, e * r)
    nsig = jnp.where(fp >= 0, e * r, r)
    logf2 = jnp.log(lb + (1.0 - lb) * sig) * LOG2E
    kk = (1.0 - lb) * nsig

    row = lax.broadcasted_iota(jnp.int32, (T, T), 0)
    col = lax.broadcasted_iota(jnp.int32, (T, T), 1)
    tri01 = jnp.where(col <= row, 1.0, 0.0).astype(BF16)
    bcum = _dot01(tri01, logf2)

    segs = []
    sg = 2 * H_LAG
    while sg <= T:
        if sg == T:
            bmid = jnp.broadcast_to(bcum[sg // 2 - 1:sg // 2, :], (T, W))
        else:
            b3 = bcum.reshape(T // sg, sg, W)
            bmid = jnp.broadcast_to(b3[:, sg // 2 - 1:sg // 2, :], (T // sg, sg, W)).reshape(T, W)
        x = bcum - bmid
        shift = sg.bit_length() - 1
        pair_ok = ((jnp.right_shift(row, shift) == jnp.right_shift(col, shift))
                   & (jnp.bitwise_and(row, sg - 1) >= sg // 2) & (jnp.bitwise_and(col, sg - 1) < sg // 2))
        segs.append((jnp.exp2(jnp.minimum(x, -x)), pair_ok))
        sg *= 2

    sub3 = lax.broadcasted_iota(jnp.int32, (T // SUBLANES, SUBLANES, 1), 1)
    lag_ok = [jnp.bitwise_and(sub3, H_LAG - 1) >= d for d in range(H_LAG)]
    b_end = bcum[T - 1:T, :]
    v_all = i_ref[...].astype(F32)
    def head_scores(h):
        hs = slice(h * D, (h + 1) * D)
        q = _silu(q_ref[:, hs].astype(F32))
        k = kk[:, hs]
        b = bcum[:, hs]
        v = v_all[:, hs]
        vb = v.astype(BF16)
        a_seg = []
        for fac, _ in segs:
            f = fac[:, hs]
            a_seg.append(_dot_nt((q * f).astype(BF16), (k * f).astype(BF16)))
        st = st_ref[h]
        o_state = _dot_nt((q * jnp.exp2(b)).astype(BF16), st.astype(BF16))
        kt = (k * jnp.exp2(b_end[:, hs] - b)).astype(BF16)
        st_ref[h] = st * jnp.exp2(b_end[:, hs]) + _dot_tn(vb, kt)
        q3, k3, b3, v3 = (a.reshape(T // SUBLANES, SUBLANES, D) for a in (q, k, b, v))
        o3 = jnp.sum(q3 * k3, axis=2, keepdims=True) * v3
        for d in range(1, H_LAG):
            ks, bs, vs = (pltpu.roll(a, d, 1) for a in (k3, b3, v3))
            a = jnp.sum(q3 * ks * jnp.exp2(b3 - bs), axis=2, keepdims=True)
            o3 = o3 + jnp.where(lag_ok[d], a, 0.0) * vs
        return hs, a_seg, o3.reshape(T, D) + o_state, vb

    def head_finish(hs, a_seg, o_part, vb):
        a_tot = jnp.zeros((T, T), F32)
        for (_, pair_ok), a in zip(segs, a_seg):
            a_tot = jnp.where(pair_ok, a, a_tot)
        oh = o_part + _dot(a_tot.astype(BF16), vb)
        on = oh * lax.rsqrt(jnp.mean(oh * oh, axis=1, keepdims=True) + EPS) * hn_ref[:, hs]
        out_ref[:, hs] = (on * _silu(z_ref[:, hs].astype(F32))).astype(out_ref.dtype)

    pending = None
    for h in range(H_HEADS):
        cur = head_scores(h)
        if pending is not None:
            head_finish(*pending)
        pending = cur
    head_finish(*pending)


def _hgrn(proj, fgate, lb_row, hn_row, batch, seq):
    T = H_TILE
    nt = seq // T
    cb = lambda j: pl.BlockSpec((T, H_W), lambda b, c, j=j: (b * nt + c, j))
    vec = pl.BlockSpec((1, H_W), lambda b, c: (0, 0))
    return pl.pallas_call(
        _hgrn_kernel,
        grid=(batch, nt),
        in_specs=[pl.BlockSpec((T, H_W), lambda b, c: (b * nt + c, 0)), cb(1), cb(2), cb(3), vec, vec],
        out_specs=pl.BlockSpec((T, H_W), lambda b, c: (b * nt + c, 0)),
        out_shape=jax.ShapeDtypeStruct((batch * seq, H_W), BF16),
        scratch_shapes=[pltpu.VMEM((H_HEADS, H_DIM, H_DIM), F32)],
        compiler_params=_cparams("parallel", "arbitrary"),
        name="hgrn2",
    )(fgate, proj, proj, proj, lb_row, hn_row)


def _even_relayout(a):
    nb = 5 * M_W + 2 * M_HEADS
    g0 = nb + N_W + 6 * N_KV_W
    col = lambda lo, n: lax.slice_in_dim(a, lo, lo + n, axis=-1)
    pad = jnp.zeros(a.shape[:-1] + (C_NZ - C_SMALL - SM_G - 3 * N_HEADS,), a.dtype)
    out = jnp.concatenate([
        col(0, 5 * M_W),
        col(nb, N_W + 6 * N_KV_W),
        col(5 * M_W, 2 * M_HEADS),
        col(g0, 3 * N_HEADS), pad,
        col(g0 + 3 * N_HEADS, N_W),
    ], axis=-1)
    assert out.shape[-1] == EV_COLS
    return out


def _rope_tables(positions):
    inv_freq = ROPE_THETA ** (-np.arange(ROPE_HALF, dtype=np.float64) / ROPE_HALF)
    ang = positions.astype(np.float64)[:, None] * inv_freq[None, :]
    cos, sin = np.cos(ang).astype(np.float32), np.sin(ang).astype(np.float32)
    n = positions.shape[0]
    z8 = np.zeros((n, ROPE_HALF), np.float32)
    rest1 = np.ones((n, N_DIM - ROPE_DIM), np.float32)
    rest0 = np.zeros((n, N_DIM - ROPE_DIM), np.float32)
    cs = np.concatenate([cos, cos, rest1], axis=1)
    s1 = np.concatenate([z8, sin, rest0], axis=1)
    s2 = np.concatenate([-sin, z8, rest0], axis=1)
    two = lambda a: np.concatenate([a, a], axis=1)
    return two(cs), two(s1), two(s2)


def _overlap_matrix(seq):
    ncmp = seq // CMP_STRIDE
    n_sel = seq // SEL_BLOCK
    cs = np.arange(ncmp) * CMP_STRIDE
    ss = np.arange(n_sel) * SEL_BLOCK
    ov = np.clip(np.minimum(cs[:, None] + CMP_LEN, ss[None, :] + SEL_BLOCK)
                 - np.maximum(cs[:, None], ss[None, :]), 0, None).astype(np.float32)
    ov[ncmp - 1, :] = 0.0
    return np.pad(ov, ((0, 0), (0, LANES - n_sel)))


def _even_layer(x2d, res, norm_w, w_in, b_in, f_bias, conv_w, conv_b, head_norm, q_norm, k_norm,
                ck_pos, ck_w1, ck_w2, cv_pos, cv_w1, cv_w2, w_out, batch, seq):
    assert seq // SEL_BLOCK <= LANES and seq % KV_TILE == 0
    outs = _norm_proj(x2d, norm_w, _even_relayout(w_in.astype(BF16)), _even_relayout(b_in),
                      (C_SMALL, C_SMALL + LANES), tn=EV_COLS // 3, res=res)
    if res:
        x2d = outs[0]
    proj, small = outs[-2:]

    f_bias_row = jnp.zeros((1, LANES), F32).at[0, SM_F:SM_F + M_HEADS].set(f_bias)
    ya = _mlstm(proj, small, conv_w, conv_b, f_bias_row, head_norm, batch, seq)

    cs, s1, s2 = (jnp.asarray(a) for a in _rope_tables(np.arange(seq)))
    qn_row = jnp.concatenate([q_norm, q_norm]).reshape(1, LANES)
    kn_rows = jnp.concatenate([k_norm, k_norm], axis=1)
    qt, ka, kwd, vst, vwt, kc, vc, gtt = _nsa_prep(proj, small, cs, s1, s2, qn_row, kn_rows, batch, seq)

    ncmp = seq // CMP_STRIDE
    ccs, cs1, cs2 = (jnp.asarray(a) for a in _rope_tables(np.arange(ncmp) * CMP_STRIDE + CMP_LEN - 1))
    kcd, vct = _nsa_compress(
        kc.reshape(batch * ncmp, CMP_STRIDE * LANES), vc.reshape(batch * ncmp, CMP_STRIDE * LANES),
        _compress_params(ck_pos, ck_w1, ck_w2), _compress_params(cv_pos, cv_w1, cv_w2),
        ccs, cs1, cs2, kn_rows, batch, ncmp)

    ovt = jnp.asarray(_overlap_matrix(seq).T).astype(BF16)
    yb = _nsa_attn(proj, qt, gtt, ka, vst, kwd, vwt, kcd, vct, ovt, batch, seq)

    w_o = w_out.astype(BF16)
    return x2d, [(ya, w_o[:M_W]), (yb, w_o[M_W:])]


def _odd_layer(x2d, res, norm_w, w_in, b_in, lb, head_norm, w_out, batch, seq):
    outs = _norm_proj(x2d, norm_w, w_in.astype(BF16), b_in, (0, H_W), tn=H_W, res=res)
    if res:
        x2d = outs[0]
    proj, fgate = outs[-2:]
    y = _hgrn(proj, fgate, lb.reshape(1, H_W), head_norm.reshape(1, H_W), batch, seq)
    return x2d, [(y, w_out.astype(BF16))]


def kernel(x, norm_w, ev_w_in, ev_b_in, mlstm_f_bias, mlstm_conv_w, mlstm_conv_b, mlstm_head_norm, nsa_q_norm, nsa_k_norm, cmp_k_pos, cmp_k_w1, cmp_k_w2, cmp_v_pos, cmp_v_w1, cmp_v_w2, ev_w_out, od_w_in, od_b_in, hgrn_lb_logits, hgrn_head_norm, od_w_out):
    batch, seq, d = x.shape
    depth = norm_w.shape[0]
    lbs = jnp.cumsum(jax.nn.softmax(hgrn_lb_logits.astype(F32), axis=0), axis=0)
    x2d = x.reshape(batch * seq, d)
    res = []
    for layer in range(depth):
        j = layer // 2
        if layer % 2 == 0:
            x2d, res = _even_layer(x2d, res, norm_w[layer], ev_w_in[j], ev_b_in[j], mlstm_f_bias[j],
                                   mlstm_conv_w[j], mlstm_conv_b[j], mlstm_head_norm[j], nsa_q_norm[j],
                                   nsa_k_norm[j], cmp_k_pos[j], cmp_k_w1[j], cmp_k_w2[j], cmp_v_pos[j],
                                   cmp_v_w1[j], cmp_v_w2[j], ev_w_out[j], batch, seq)
        else:
            x2d, res = _odd_layer(x2d, res, norm_w[layer], od_w_in[j], od_b_in[j], lbs[j], hgrn_head_norm[j],
                                  od_w_out[j], batch, seq)
    if len(res) == 2:
        x2d = _out_proj2(res[0][0], res[1][0], res[0][1], res[1][1], x2d)
    elif res:
        x2d = _out_proj1(res[0][0], res[0][1], x2d)
    return x2d.reshape(batch, seq, d)
```

```python
import functools

import numpy as np
import jax
import jax.numpy as jnp
from jax import lax
from jax.experimental import pallas as pl
from jax.experimental.pallas import tpu as pltpu

F32 = jnp.float32
BF16 = jnp.bfloat16

LANES = 128
SUBLANES = 8
BF16_SUBLANES = 16
EPS = 1e-6
NEG = -1e30
VMEM_LIMIT_BYTES = 56 * 1024 * 1024

M_HEADS = 4
M_DIM = 128
M_W = M_HEADS * M_DIM
M_CHUNK = 128
CONV_K = 4
N_HEADS = 8
N_GROUPS = 2
N_HPG = N_HEADS // N_GROUPS
N_DIM = 64
N_W = N_HEADS * N_DIM
N_KV_W = N_GROUPS * N_DIM
CMP_LEN = 32
CMP_STRIDE = 16
CMP_HIDDEN = 128
CMP_CHUNK = 128
SEL_BLOCK = 64
SEL_SHIFT = 6
SEL_TOPK = 16
WINDOW = 512
Q_BLOCK = 128
ROPE_THETA = 500000.0
ROPE_DIM = N_DIM // 4
ROPE_HALF = ROPE_DIM // 2
KV_TILE = 512
V_ROWS = 80
LOG2E = 1.4426950408889634
H_HEADS = 8
H_DIM = 128
H_W = H_HEADS * H_DIM
H_TILE = 128
H_LAG = 2

C_MQ, C_MK, C_MV, C_MO, C_MZ = 0, 512, 1024, 1536, 2048
C_NQ = 2560
C_KC, C_VC, C_KS, C_VS = 3072, 3200, 3328, 3456
C_KW, C_VW, C_SMALL = 3584, 3712, 3840
C_NZ = 4096
EV_COLS = 4608
SM_I, SM_F, SM_G = 0, 4, 8


def _cparams(*sem):
    return pltpu.CompilerParams(dimension_semantics=sem, vmem_limit_bytes=VMEM_LIMIT_BYTES)


def _dot(a, b):
    return jnp.dot(a, b, preferred_element_type=F32)


def _dot_nt(a, b):
    return lax.dot_general(a, b, (((1,), (1,)), ((), ())), preferred_element_type=F32)


def _dot_tn(a, b):
    return lax.dot_general(a, b, (((0,), (0,)), ((), ())), preferred_element_type=F32)


def _sigmoid(x):
    return 1.0 / (1.0 + jnp.exp(-x))


def _silu(x):
    return x * _sigmoid(x)


def _split3(x):
    hi = x.astype(BF16)
    r = x - hi.astype(F32)
    mid = r.astype(BF16)
    lo = (r - mid.astype(F32)).astype(BF16)
    return hi, mid, lo


def _dot01(m01, x):
    hi, mid, lo = _split3(x)
    return _dot(m01, hi) + _dot(m01, mid) + _dot(m01, lo)


def _norm_proj_kernel(*refs, n_res, g_tiles, g_off):
    x_ref = refs[0]
    y_refs, wy_refs = refs[1:1 + n_res], refs[1 + n_res:1 + 2 * n_res]
    nw_ref, w_ref, b_ref = refs[1 + 2 * n_res:4 + 2 * n_res]
    outs = refs[4 + 2 * n_res:]
    xo_ref = outs[0] if n_res else None
    o_ref, g_ref, h_ref = outs[-3:]
    j = pl.program_id(1)

    @pl.when(j == 0)
    def _():
        x = x_ref[...]
        for y_ref, wy_ref in zip(y_refs, wy_refs):
            x = x + _dot(y_ref[...], wy_ref[...])
        if n_res:
            xo_ref[...] = x
        ms = jnp.mean(x * x, axis=-1, keepdims=True)
        h_ref[...] = (x * lax.rsqrt(ms + EPS) * nw_ref[...]).astype(BF16)

    acc = _dot(h_ref[...], w_ref[...]) + b_ref[...]
    o_ref[...] = acc.astype(o_ref.dtype)

    @pl.when((j >= g_tiles[0]) & (j < g_tiles[1]))
    def _():
        g_ref[...] = acc[:, g_off:g_off + g_ref.shape[1]]


def _norm_proj(x2d, norm_w, w_bf16, bias, g_cols, tn, res=(), tm=1024):
    t, d = x2d.shape
    n = w_bf16.shape[1]
    g_w = min(g_cols[1] - g_cols[0], tn)
    g_tiles = (g_cols[0] // tn, -(-g_cols[1] // tn))
    row = lambda w: pl.BlockSpec((tm, w), lambda i, j: (i, 0))
    const = lambda a: pl.BlockSpec(a.shape, lambda i, j: (0, 0))
    ys, wys = [y for y, _ in res], [wy for _, wy in res]
    out_specs = [pl.BlockSpec((tm, tn), lambda i, j: (i, j)),
                 pl.BlockSpec((tm, g_w), lambda i, j: (i, jnp.clip(j - g_tiles[0], 0, g_tiles[1] - g_tiles[0] - 1)))]
    out_shape = [jax.ShapeDtypeStruct((t, n), BF16), jax.ShapeDtypeStruct((t, g_cols[1] - g_cols[0]), F32)]
    if res:
        out_specs = [row(d)] + out_specs
        out_shape = [jax.ShapeDtypeStruct((t, d), F32)] + out_shape
    return pl.pallas_call(
        functools.partial(_norm_proj_kernel, n_res=len(res), g_tiles=g_tiles, g_off=g_cols[0] % tn),
        grid=(t // tm, n // tn),
        in_specs=([row(d)] + [row(y.shape[1]) for y in ys] + [const(wy) for wy in wys]
                  + [pl.BlockSpec((1, d), lambda i, j: (0, 0)),
                     pl.BlockSpec((d, tn), lambda i, j: (0, j)),
                     pl.BlockSpec((1, tn), lambda i, j: (0, j))]),
        out_specs=out_specs,
        out_shape=out_shape,
        scratch_shapes=[pltpu.VMEM((tm, d), BF16)],
        compiler_params=_cparams("parallel", "arbitrary"),
        name="norm_proj",
    )(x2d, *ys, *wys, norm_w.reshape(1, d), w_bf16, bias.reshape(1, n))


def _out_proj2_kernel(ya_ref, yb_ref, wa_ref, wb_ref, x_ref, o_ref):
    o_ref[...] = x_ref[...] + _dot(ya_ref[...], wa_ref[...]) + _dot(yb_ref[...], wb_ref[...])


def _out_proj2(ya, yb, wa, wb, x2d, tm=1024):
    t, d = x2d.shape
    ka, kb = ya.shape[1], yb.shape[1]
    return pl.pallas_call(
        _out_proj2_kernel,
        grid=(t // tm,),
        in_specs=[
            pl.BlockSpec((tm, ka), lambda i: (i, 0)),
            pl.BlockSpec((tm, kb), lambda i: (i, 0)),
            pl.BlockSpec((ka, d), lambda i: (0, 0)),
            pl.BlockSpec((kb, d), lambda i: (0, 0)),
            pl.BlockSpec((tm, d), lambda i: (i, 0)),
        ],
        out_specs=pl.BlockSpec((tm, d), lambda i: (i, 0)),
        out_shape=jax.ShapeDtypeStruct((t, d), F32),
        compiler_params=_cparams("parallel"),
        name="out_proj2",
    )(ya, yb, wa, wb, x2d)


def _out_proj1_kernel(y_ref, w_ref, x_ref, o_ref):
    o_ref[...] = x_ref[...] + _dot(y_ref[...], w_ref[...])


def _out_proj1(y, w, x2d, tm=1024):
    t, d = x2d.shape
    k = y.shape[1]
    return pl.pallas_call(
        _out_proj1_kernel,
        grid=(t // tm,),
        in_specs=[
            pl.BlockSpec((tm, k), lambda i: (i, 0)),
            pl.BlockSpec((k, d), lambda i: (0, 0)),
            pl.BlockSpec((tm, d), lambda i: (i, 0)),
        ],
        out_specs=pl.BlockSpec((tm, d), lambda i: (i, 0)),
        out_shape=jax.ShapeDtypeStruct((t, d), F32),
        compiler_params=_cparams("parallel"),
        name="out_proj1",
    )(y, w, x2d)


def _mlstm_kernel(q_ref, k_ref, v_ref, o_ref, z_ref, sm_ref, cw_ref, cb_ref, fb_ref, hn_ref,
                  out_ref, xbuf, c_ref, n_ref, m_ref):
    L, D = M_CHUNK, M_DIM
    NB = q_ref.shape[0]
    XR = xbuf.shape[1]
    X0 = BF16_SUBLANES

    @pl.when(pl.program_id(0) == 0)
    def _():
        xbuf[...] = jnp.zeros_like(xbuf)
        c_ref[...] = jnp.zeros_like(c_ref)
        n_ref[...] = jnp.zeros_like(n_ref)
        m_ref[...] = jnp.zeros_like(m_ref)

    t_i = lax.broadcasted_iota(jnp.int32, (L, XR), 0)
    r_i = lax.broadcasted_iota(jnp.int32, (L, XR), 1)
    row = lax.broadcasted_iota(jnp.int32, (L, L), 0)
    col = lax.broadcasted_iota(jnp.int32, (L, L), 1)
    causal = col <= row
    lane = lax.broadcasted_iota(jnp.int32, (L, LANES), 1)
    is_f = (lane >= SM_F) & (lane < SM_F + M_HEADS)

    heads = []
    for b in range(NB):
        xbuf[b, X0:X0 + L, 0:M_W] = q_ref[b]
        xbuf[b, X0:X0 + L, M_W:2 * M_W] = k_ref[b]
        x2 = xbuf[b]
        acc = cb_ref[...] + cw_ref[CONV_K - 1:CONV_K, :] * x2[X0:X0 + L].astype(F32)
        for d in range(1, CONV_K):
            shift = jnp.where(r_i == t_i + X0 - d, 1.0, 0.0).astype(BF16)
            acc = acc + cw_ref[CONV_K - 1 - d:CONV_K - d, :] * _dot(shift, x2)
        qk = _silu(acc)
        xbuf[b, 0:X0, :] = xbuf[b, L:L + X0, :]

        fpre = sm_ref[b] + fb_ref[...]
        gates = jnp.where(is_f, jnp.minimum(fpre, 0.0) - jnp.log(1.0 + jnp.exp(-jnp.abs(fpre))), fpre)
        bcols = _dot01(jnp.where(causal, 1.0, 0.0).astype(BF16), gates)
        gates_t = gates.T
        bcols_t = bcols.T
        for h in range(M_HEADS):
            hs = slice(h * D, (h + 1) * D)
            q = qk[:, hs]
            k = qk[:, M_W + h * D:M_W + (h + 1) * D] * (D ** -0.5)
            i_col = gates[:, SM_I + h:SM_I + h + 1]
            i_row = gates_t[SM_I + h:SM_I + h + 1, :]
            bcum_col = bcols[:, SM_F + h:SM_F + h + 1]
            bcum_row = bcols_t[SM_F + h:SM_F + h + 1, :]
            m_prev = m_ref[b * M_HEADS + h]
            dmat = jnp.where(causal, bcum_col - bcum_row + i_row, -jnp.inf)
            inter = bcum_col + m_prev
            m_row = jnp.maximum(jnp.max(dmat, axis=1, keepdims=True), inter)
            b_last = bcum_col[L - 1:L, :]
            dl = b_last - bcum_col + i_col
            m_new = jnp.maximum(b_last + m_prev, jnp.max(dl, axis=0, keepdims=True))
            wkk = jnp.exp(dl - m_new) * k
            heads.append(dict(
                b=b, hs=hs, st=b * M_HEADS + h, q=q, qb=q.astype(BF16), kb=k.astype(BF16),
                vb=v_ref[b, :, hs].astype(BF16), wkk=wkk,
                e=jnp.exp(dmat - m_row), w_inter=jnp.exp(inter - m_row), floor=jnp.exp(-m_row),
                decay=jnp.exp(b_last + m_prev - m_new), m_new=m_new))
    s_qk = [_dot_nt(t["qb"], t["kb"]) for t in heads]
    s_qc = [_dot(t["qb"], c_ref[t["st"]].astype(BF16)) for t in heads]
    s_kv = [_dot_tn(t["wkk"].astype(BF16), t["vb"]) for t in heads]
    ones_b = jnp.ones((L, LANES), BF16)
    smats = [(s_qk[j] * t["e"]).astype(BF16) for j, t in enumerate(heads)]
    s_sv = [_dot(smats[j], t["vb"]) for j, t in enumerate(heads)]
    s_ds = [_dot(smats[j], ones_b) for j in range(len(heads))]
    s_qn = [_dot((t["q"] * n_ref[t["st"]]).astype(BF16), ones_b) for t in heads]
    hhs = []
    for j, t in enumerate(heads):
        st = t["st"]
        num = s_sv[j] + t["w_inter"] * s_qc[j]
        den = s_ds[j] + t["w_inter"] * s_qn[j]
        hhs.append(num / jnp.maximum(jnp.abs(den), t["floor"]))
        c_ref[st] = t["decay"] * c_ref[st] + s_kv[j]
        n_ref[st] = t["decay"] * n_ref[st] + jnp.sum(t["wkk"], axis=0, keepdims=True)
        m_ref[st] = t["m_new"]
    s_ms = [_dot((hh * hh).astype(BF16), ones_b) for hh in hhs]
    for j, t in enumerate(heads):
        b, hs = t["b"], t["hs"]
        hn = hhs[j] * lax.rsqrt(s_ms[j] * (1.0 / D) + EPS) * hn_ref[:, hs]
        gate = _sigmoid(o_ref[b, :, hs].astype(F32)) * _silu(z_ref[b, :, hs].astype(F32))
        out_ref[b, :, hs] = (hn * gate).astype(out_ref.dtype)


def _mlstm(proj, small, conv_w, conv_b, f_bias_row, head_norm, batch, seq):
    L = M_CHUNK
    nc = seq // L
    p3 = proj.reshape(batch, seq, proj.shape[1])
    s3 = small.reshape(batch, seq, LANES)
    cb = lambda off: pl.BlockSpec((batch, L, M_W), lambda c, off=off: (0, c, off // M_W))
    full = lambda r, w: pl.BlockSpec((r, w), lambda c: (0, 0))
    out = pl.pallas_call(
        _mlstm_kernel,
        grid=(nc,),
        in_specs=[
            cb(C_MQ), cb(C_MK), cb(C_MV), cb(C_MO), cb(C_MZ),
            pl.BlockSpec((batch, L, LANES), lambda c: (0, c, 0)),
            full(CONV_K, 2 * M_W), full(1, 2 * M_W), full(1, LANES), full(1, M_W),
        ],
        out_specs=pl.BlockSpec((batch, L, M_W), lambda c: (0, c, 0)),
        out_shape=jax.ShapeDtypeStruct((batch, seq, M_W), BF16),
        scratch_shapes=[
            pltpu.VMEM((batch, 2 * L, 2 * M_W), BF16),
            pltpu.VMEM((batch * M_HEADS, M_DIM, M_DIM), F32),
            pltpu.VMEM((batch * M_HEADS, 1, M_DIM), F32),
            pltpu.VMEM((batch * M_HEADS, 1, 1), F32),
        ],
        compiler_params=_cparams("arbitrary"),
        name="mlstm",
    )(p3, p3, p3, p3, p3, s3, conv_w, conv_b.reshape(1, -1), f_bias_row, head_norm.reshape(1, -1))
    return out.reshape(batch * seq, M_W)


def _group_rms(x, w_row):
    r = lax.broadcasted_iota(jnp.int32, (LANES, LANES), 0)
    c = lax.broadcasted_iota(jnp.int32, (LANES, LANES), 1)
    same_head = jnp.where((r < N_DIM) == (c < N_DIM), 1.0, 0.0).astype(BF16)
    ms = _dot_hilo(x * x, same_head) * (1.0 / N_DIM)
    return x * lax.rsqrt(ms + EPS) * w_row


def _lane_gather01(src_lane, width=LANES):
    r = lax.broadcasted_iota(jnp.int32, (LANES, width), 0)
    c = lax.broadcasted_iota(jnp.int32, (LANES, width), 1)
    return jnp.where(r == src_lane(c), 1.0, 0.0).astype(BF16)


def _dot_hilo(x, m01):
    hi = x.astype(BF16)
    lo = (x - hi.astype(F32)).astype(BF16)
    return _dot(hi, m01) + _dot(lo, m01)


def _rope(x, cs, s1, s2):
    return x * cs + pltpu.roll(x, ROPE_HALF, 1) * s1 + pltpu.roll(x, LANES - ROPE_HALF, 1) * s2


def _dup_halves(x):
    dup = _lane_gather01(lambda c: jnp.where(c < LANES, 0, N_DIM) + (c & (N_DIM - 1)), 2 * LANES)
    both = _dot(x, dup).astype(BF16)
    return both[:, 0:LANES], both[:, LANES:2 * LANES]


def _transpose01(x):
    return _dot_nt(_lane_gather01(lambda c: c), x)


def _nsa_prep_kernel(q_ref, kv_ref, kw_ref, sm_ref, cs_ref, s1_ref, s2_ref, qn_ref, kn_ref,
                     qt_ref, ka_ref, kwd_ref, vst_ref, vwt_ref, kc_ref, vc_ref, gtt_ref,
                     *, tp, seq):
    cs, s1, s2 = cs_ref[...], s1_ref[...], s2_ref[...]
    nq = N_W // LANES
    kv = kv_ref[...].astype(F32)
    kwb = kw_ref[...].astype(F32)
    kc_ref[...] = kv[:, 0:LANES]
    vc_ref[...] = kv[:, LANES:2 * LANES]
    slabs = [q_ref[:, j * LANES:(j + 1) * LANES].astype(F32) for j in range(nq)]
    slabs += [kv[:, 2 * LANES:3 * LANES], kwb[:, 0:LANES]]
    gains = [qn_ref[...]] * nq + [kn_ref[1:2, :], kn_ref[2:3, :]]
    normed = [_group_rms(x, w) for x, w in zip(slabs, gains)]
    roped = [_rope(x, cs, s1, s2) for x in normed]
    qs = [(x * (N_DIM ** -0.5 * LOG2E)).astype(BF16) for x in roped[:nq]]
    k0, k1 = _dup_halves(roped[nq].astype(BF16))
    w0, w1 = _dup_halves(roped[nq + 1].astype(BF16))
    gates = _sigmoid(sm_ref[...])
    g_hi = gates.astype(BF16)
    g_lo = (gates - g_hi.astype(F32)).astype(BF16)
    qts = [_transpose01(x) for x in qs]
    vst = _transpose01(kv_ref[:, 3 * LANES:4 * LANES])
    vwt = _transpose01(kw_ref[:, LANES:2 * LANES])
    gtt_ref[...] = _transpose01(g_hi) + _transpose01(g_lo)
    for j in range(nq):
        qt_ref[j * LANES:(j + 1) * LANES, :] = qts[j].astype(BF16)
    pos0 = (pl.program_id(0) * tp) % seq
    pos = pos0 + lax.broadcasted_iota(jnp.int32, (tp, LANES), 0)
    blk = lax.broadcasted_iota(jnp.int32, (tp, LANES), 1)
    onehot = jnp.where(jnp.right_shift(pos, SEL_SHIFT) == blk, 1.0, 0.0).astype(BF16)
    ka_ref[:, 0:LANES] = k0
    ka_ref[:, LANES:2 * LANES] = onehot
    ka_ref[:, 2 * LANES:3 * LANES] = k1
    ka_ref[:, 3 * LANES:4 * LANES] = onehot
    fill =(lax.broadcasted_iota(jnp.int32, (V_ROWS - N_DIM, Q_BLOCK), 0) == 0).astype(BF16)
    for c in range(tp // Q_BLOCK):
        for g in range(N_GROUPS):
            vst_ref[c, g * V_ROWS:g * V_ROWS + N_DIM, :] = (
                vst[g * N_DIM:(g + 1) * N_DIM, c * Q_BLOCK:(c + 1) * Q_BLOCK].astype(BF16))
            vst_ref[c, g * V_ROWS + N_DIM:(g + 1) * V_ROWS, :] = fill
    kwd_ref[:, 0:LANES] = w0
    kwd_ref[:, LANES:2 * LANES] = w1
    vwt_ref[...] = vwt.astype(BF16)


def _nsa_prep(proj, small, cs, s1, s2, qn_row, kn_rows, batch, seq):
    tp = KV_TILE
    t = batch * seq
    nsb = seq // tp
    rb = lambda w, off: pl.BlockSpec((tp, w), lambda i, off=off, w=w: (i, off // w))
    tab = pl.BlockSpec((tp, LANES), lambda i: (i % nsb, 0))
    o = lambda w: pl.BlockSpec((tp, w), lambda i: (i, 0))
    ot = lambda r: pl.BlockSpec((r, tp), lambda i: (0, i))
    sds = lambda w, dt: jax.ShapeDtypeStruct((t, w), dt)
    return pl.pallas_call(
        functools.partial(_nsa_prep_kernel, tp=tp, seq=seq),
        grid=(t // tp,),
        in_specs=[rb(512, C_NQ), rb(512, C_KC), rb(512, C_KW), o(LANES), tab, tab, tab,
                  pl.BlockSpec((1, LANES), lambda i: (0, 0)), pl.BlockSpec((3, LANES), lambda i: (0, 0))],
        out_specs=[ot(N_W), o(512), o(256),
                   pl.BlockSpec((tp // Q_BLOCK, N_GROUPS * V_ROWS, Q_BLOCK), lambda i: (i, 0, 0)),
                   ot(LANES), o(128), o(128), ot(LANES)],
        out_shape=[jax.ShapeDtypeStruct((N_W, t), BF16), sds(512, BF16), sds(256, BF16),
                   jax.ShapeDtypeStruct((t // Q_BLOCK, N_GROUPS * V_ROWS, Q_BLOCK), BF16),
                   jax.ShapeDtypeStruct((LANES, t), BF16),
                   sds(128, F32), sds(128, F32), jax.ShapeDtypeStruct((LANES, t), F32)],
        compiler_params=_cparams("parallel"),
        name="nsa_prep",
    )(proj, proj, proj, small, cs, s1, s2, qn_row, kn_rows)


def _gelu_tanh(x):
    return 0.5 * x * (1.0 + jnp.tanh(0.7978845608028654 * (x + 0.044715 * x * x * x)))


def _compress_mlp(x_ref, pa_ref, pb_ref, wa_ref, wb_ref, w2_ref, hb_ref):
    n = x_ref.shape[0]
    x = x_ref[...]
    ha = _dot((x + pa_ref[...]).astype(BF16), wa_ref[...])
    hb_ref[0:n, :] = _dot((x + pb_ref[...]).astype(BF16), wb_ref[...])
    hb_ref[n:n + SUBLANES, :] = jnp.zeros((SUBLANES, hb_ref.shape[1]), F32)
    hid = ha + hb_ref[1:n + 1, :]
    return _dot(_gelu_tanh(hid).astype(BF16), w2_ref[...])


def _nsa_compress_kernel(kx_ref, vx_ref, kpa_ref, kpb_ref, vpa_ref, vpb_ref, kwa_ref, kwb_ref, kw2_ref,
                         vwa_ref, vwb_ref, vw2_ref, cs_ref, s1_ref, s2_ref, kn_ref,
                         kcd_ref, vct_ref, hb_ref):
    n = kx_ref.shape[0]
    rowi = lax.broadcasted_iota(jnp.int32, (n, LANES), 0)
    live = rowi < n - 1
    kc = _compress_mlp(kx_ref, kpa_ref, kpb_ref, kwa_ref, kwb_ref, kw2_ref, hb_ref)
    kc = _rope(_group_rms(kc, kn_ref[0:1, :]), cs_ref[...], s1_ref[...], s2_ref[...])
    kc = jnp.where(live, kc, 0.0)
    c0, c1 = _dup_halves(kc.astype(BF16))
    kcd_ref[:, 0:LANES] = c0
    kcd_ref[:, LANES:2 * LANES] = c1
    vc = _compress_mlp(vx_ref, vpa_ref, vpb_ref, vwa_ref, vwb_ref, vw2_ref, hb_ref)
    vc = jnp.where(live, vc, 0.0)
    vct_ref[0] = vc.T.astype(BF16)


def _nsa_compress(kx, vx, kp, vp, cs, s1, s2, kn_rows, batch, nrow):
    wide = CMP_STRIDE * LANES
    full = lambda a: pl.BlockSpec(a.shape, lambda b: (0,) * a.ndim)
    xs = pl.BlockSpec((nrow, wide), lambda b: (b, 0))
    args = (kx, vx, kp[0], kp[1], vp[0], vp[1], kp[2], kp[3], kp[4], vp[2], vp[3], vp[4], cs, s1, s2, kn_rows)
    return pl.pallas_call(
        _nsa_compress_kernel,
        grid=(batch,),
        in_specs=[xs, xs] + [full(a) for a in args[2:]],
        out_specs=[pl.BlockSpec((nrow, 2 * LANES), lambda b: (b, 0)),
                   pl.BlockSpec((1, LANES, nrow), lambda b: (b, 0, 0))],
        out_shape=[jax.ShapeDtypeStruct((batch * nrow, 2 * LANES), BF16),
                   jax.ShapeDtypeStruct((batch, LANES, nrow), BF16)],
        scratch_shapes=[pltpu.VMEM((nrow + SUBLANES, 2 * CMP_HIDDEN), F32)],
        compiler_params=_cparams("parallel"),
        name="nsa_compress",
    )(*args)


def _compress_params(pos, w1, w2):
    def expand(w1_half, pos_half):
        w = w1_half.reshape(CMP_STRIDE, N_DIM, CMP_HIDDEN)
        z = jnp.zeros_like(w)
        g0 = jnp.concatenate([w, z], axis=-1)
        g1 = jnp.concatenate([z, w], axis=-1)
        wx = jnp.stack([g0, g1], axis=1).reshape(CMP_STRIDE * LANES, 2 * CMP_HIDDEN)
        px = jnp.concatenate([pos_half, pos_half], axis=-1).reshape(1, CMP_STRIDE * LANES)
        return wx.astype(BF16), px.astype(F32)

    half = CMP_STRIDE * N_DIM
    wa, pa = expand(w1[:half], pos[:CMP_STRIDE])
    wb, pb = expand(w1[half:], pos[CMP_STRIDE:])
    z2 = jnp.zeros_like(w2)
    w2x = jnp.concatenate([jnp.concatenate([w2, z2], axis=1), jnp.concatenate([z2, w2], axis=1)], axis=0)
    return pa, pb, wa, wb, w2x.astype(BF16)


def _nsa_attn_kernel(qt_ref, gtt_ref, z_ref, ka_ref, vst_ref,
                     kw0, kw1, kw2, kw3, kw4, wt0, wt1, wt2, wt3, wt4,
                     kcd_ref, vct_ref, ovt_ref, out_ref, qa_ref, acc_ref, *, ncmp):
    QB = Q_BLOCK
    NP = N_HEADS // 2
    PW = 2 * QB
    i = pl.program_id(1)
    t0 = i * QB
    kw_refs = (kw0, kw1, kw2, kw3, kw4)
    vwt_refs = (wt0, wt1, wt2, wt3, wt4)
    nwt = len(kw_refs)

    sub = lax.broadcasted_iota(jnp.int32, (QB, QB), 0)
    lane = lax.broadcasted_iota(jnp.int32, (QB, QB), 1)
    top = sub < N_DIM
    t_row = t0 + lax.broadcasted_iota(jnp.int32, (1, QB), 1)
    t_row2 = jnp.concatenate([t_row, t_row], axis=1)

    def pair_q(pr):
        slab = qt_ref[pr * LANES:(pr + 1) * LANES, :]
        zero = jnp.zeros_like(slab)
        return jnp.concatenate([jnp.where(top, slab, zero), jnp.where(top, zero, slab)], axis=1)

    blk_f = sub.astype(F32)
    cur = jnp.right_shift(t_row, SEL_SHIFT)
    grp = lambda pr: pr // (N_HPG // 2)
    for pr in range(NP):
        qa_ref[pr, 0:LANES, :] = pair_q(pr)

    def compressed(n):
        nidx = lax.broadcasted_iota(jnp.int32, (n, PW), 0)
        ok = (nidx * CMP_STRIDE + (CMP_LEN - 1) <= t_row2) & (nidx < ncmp - 1)
        any_ok = t_row2 >= CMP_LEN - 1
        s_cmp = [_dot(kcd_ref[0:n, grp(pr) * LANES:(grp(pr) + 1) * LANES], qa_ref[pr, 0:LANES, :])
                 for pr in range(NP)]
        p_cmp = []
        for pr in range(NP):
            s = jnp.where(ok, s_cmp[pr], NEG)
            mx = jnp.max(s, axis=0, keepdims=True)
            p = jnp.exp2(s - mx)
            den = jnp.sum(p, axis=0, keepdims=True)
            p_cmp.append(p * jnp.where(any_ok, 1.0 / den, 0.0))
        outs = [_dot(vct_ref[0, grp(pr) * N_DIM:(grp(pr) + 1) * N_DIM, 0:n], p_cmp[pr].astype(BF16))
                for pr in range(NP)]
        for g in range(N_GROUPS):
            psum = jnp.zeros((n, QB), F32)
            for pp in range(N_HPG // 2):
                p = p_cmp[g * (N_HPG // 2) + pp]
                psum = psum + p[:, 0:QB] + p[:, QB:PW]
            p_hi = psum.astype(BF16)
            p_lo = (psum - p_hi.astype(F32)).astype(BF16)
            outs.append(_dot(ovt_ref[:, 0:n], p_hi) + _dot(ovt_ref[:, 0:n], p_lo))
        return tuple(outs)

    n_opts = list(range(CMP_CHUNK, ncmp + 1, CMP_CHUNK))
    need = (t0 + QB - CMP_LEN) // CMP_STRIDE + 1
    outs = lax.switch(jnp.clip((need - 1) // CMP_CHUNK, 0, len(n_opts) - 1),
                      [functools.partial(compressed, n) for n in n_opts])
    o_cmp, imps = outs[:NP], outs[NP:]
    forced = (sub == 0) | (sub == cur) | (sub == cur - 1)
    n_rounds = SEL_TOPK - 3
    valid = sub <= cur
    works0 = [jnp.where(valid, jnp.where(forced, -jnp.inf, imp), NEG) for imp in imps]

    def rounds(works, exact):
        sels = [forced for _ in range(N_GROUPS)]
        for _ in range(n_rounds):
            for g in range(N_GROUPS):
                mx = jnp.max(works[g], axis=0, keepdims=True)
                if exact:
                    idx = jnp.min(jnp.where(works[g] == mx, blk_f, float(LANES)), axis=0, keepdims=True)
                    pick = blk_f == idx
                else:
                    pick = works[g] == mx
                sels[g] = sels[g] | pick
                works[g] = jnp.where(pick, -jnp.inf, works[g])
        return tuple(jnp.where(s & valid, 0.0, NEG) for s in sels)

    bias = rounds(list(works0), exact=False)
    picked = [jnp.sum(jnp.where((b == 0.0) & jnp.logical_not(forced), 1.0, 0.0), axis=0, keepdims=True)
              for b in bias]
    tied = jnp.max(functools.reduce(jnp.maximum, picked)) > n_rounds
    bias = lax.cond(tied, lambda: rounds(list(works0), exact=True), lambda: bias)
    for g in range(N_GROUPS):
        neg = bias[g].astype(BF16)
        neg2 = jnp.concatenate([neg, neg], axis=1)
        for pp in range(N_HPG // 2):
            qa_ref[g * (N_HPG // 2) + pp, LANES:2 * LANES, :] = neg2

    CH = QB
    above = jnp.concatenate([sub > lane, sub > lane], axis=1)

    def sel_scores(cj):
        k0 = pl.multiple_of(cj * CH, CH)
        return [_dot(ka_ref[pl.ds(k0, CH), 2 * grp(pr) * LANES:2 * (grp(pr) + 1) * LANES], qa_ref[pr])
                for pr in range(NP)]

    def sel_softmax(ss, ms, masked):
        new, alphas, ps = [], [], []
        for pr in range(NP):
            s = jnp.where(above, NEG, ss[pr]) if masked else ss[pr]
            m_new = jnp.maximum(ms[pr], jnp.max(s, axis=0, keepdims=True))
            alphas.append(jnp.exp2(ms[pr] - m_new))
            ps.append(jnp.exp2(s - m_new).astype(BF16))
            new.append(m_new)
        return tuple(new), alphas, ps

    def sel_values(cj, alphas, ps):
        for pr in range(NP):
            g = grp(pr)
            acc_ref[pr] = alphas[pr] * acc_ref[pr] + _dot(vst_ref[cj, g * V_ROWS:(g + 1) * V_ROWS, :], ps[pr])

    def sel_chunk(cj, ms, masked):
        ms, alphas, ps = sel_softmax(sel_scores(cj), ms, masked)
        sel_values(cj, alphas, ps)
        return ms

    def sel_tile(jt, ms, nc):
        c0 = jt * nc
        k0 = pl.multiple_of(c0 * CH, CH)
        big = [_dot(ka_ref[pl.ds(k0, nc * CH), 2 * grp(pr) * LANES:2 * (grp(pr) + 1) * LANES], qa_ref[pr])
               for pr in range(NP)]
        for c in range(nc):
            ss = [b[c * CH:(c + 1) * CH] for b in big]
            ms, alphas, ps = sel_softmax(ss, ms, False)
            sel_values(c0 + c, alphas, ps)
        return ms

    acc_ref[...] = jnp.zeros_like(acc_ref)
    ms = tuple(jnp.full((1, PW), 2.0 * NEG, F32) for _ in range(NP))
    done = 0
    for nc in (4 * KV_TILE // CH, 2 * KV_TILE // CH, KV_TILE // CH):
        n_run = (i - done) // nc
        ms = lax.fori_loop(done // nc, done // nc + n_run, functools.partial(sel_tile, nc=nc), ms)
        done = done + n_run * nc
    ms = lax.fori_loop(done, i, lambda cj, ms: sel_chunk(cj, ms, False), ms)
    s_diag = sel_scores(i)
    s_win = [[_dot(kw_refs[c][:, grp(pr) * LANES:(grp(pr) + 1) * LANES], qa_ref[pr, 0:LANES, :])
              for c in range(nwt)] for pr in range(NP)]
    ms, alphas, ps_diag = sel_softmax(s_diag, ms, True)
    p_win, d_win = [], []
    for pr in range(NP):
        ss = []
        for c in range(nwt):
            ok = (i - (nwt - 1) + c) >= 0
            if c == 0:
                ok = ok & above
            elif c == nwt - 1:
                ok = ok & jnp.logical_not(above)
            ss.append(jnp.where(ok, s_win[pr][c], NEG))
        mx = ss[0].max(axis=0, keepdims=True)
        for c in range(1, nwt):
            mx = jnp.maximum(mx, ss[c].max(axis=0, keepdims=True))
        ps = [jnp.exp2(ss[c] - mx) for c in range(nwt)]
        den = ps[0].sum(axis=0, keepdims=True)
        for c in range(1, nwt):
            den = den + ps[c].sum(axis=0, keepdims=True)
        p_win.append([p.astype(BF16) for p in ps])
        d_win.append(den)
    sel_values(i, alphas, ps_diag)
    o_win = []
    for pr in range(NP):
        g = grp(pr)
        acc = _dot(vwt_refs[0][g * N_DIM:(g + 1) * N_DIM, :], p_win[pr][0])
        for c in range(1, nwt):
            acc = acc + _dot(vwt_refs[c][g * N_DIM:(g + 1) * N_DIM, :], p_win[pr][c])
        o_win.append(acc * (1.0 / d_win[pr]))
    o_sel = [acc_ref[pr, 0:N_DIM, :] * (1.0 / acc_ref[pr, N_DIM:N_DIM + 1, :]) for pr in range(NP)]

    for pr in range(NP):
        halves = []
        for par in range(2):
            h = 2 * pr + par
            cs = slice(par * QB, (par + 1) * QB)
            gc = gtt_ref[SM_G + 3 * h:SM_G + 3 * h + 1, :]
            gs = gtt_ref[SM_G + 3 * h + 1:SM_G + 3 * h + 2, :]
            gw = gtt_ref[SM_G + 3 * h + 2:SM_G + 3 * h + 3, :]
            halves.append(gc * o_cmp[pr][:, cs] + gs * o_sel[pr][:, cs] + gw * o_win[pr][:, cs])
        y = jnp.concatenate(halves, axis=0).T
        sl = slice(pr * LANES, (pr + 1) * LANES)
        out_ref[:, sl] = (y * _silu(z_ref[:, sl].astype(F32))).astype(out_ref.dtype)


def _nsa_attn(proj, qt, gtt, ka, vst, kwd, vwt, kcd, vct, ovt, batch, seq):
    QB = Q_BLOCK
    nqb = seq // QB
    ncmp = seq // CMP_STRIDE
    nwt = WINDOW // QB + 1
    row = lambda w: pl.BlockSpec((QB, w), lambda b, i: (b * nqb + i, 0))
    colb = lambda r: pl.BlockSpec((r, QB), lambda b, i: (0, b * nqb + i))
    wblk = lambda b, i, c: b * nqb + jnp.maximum(i - (nwt - 1) + c, 0)
    in_specs = ([colb(N_W), colb(LANES), pl.BlockSpec((QB, N_W), lambda b, i: (b * nqb + i, C_NZ // N_W)),
                 pl.BlockSpec((seq, 4 * LANES), lambda b, i: (b, 0)),
                 pl.BlockSpec((nqb, N_GROUPS * V_ROWS, QB), lambda b, i: (b, 0, 0))]
                + [pl.BlockSpec((QB, 2 * LANES), lambda b, i, c=c: (wblk(b, i, c), 0)) for c in range(nwt)]
                + [pl.BlockSpec((LANES, QB), lambda b, i, c=c: (0, wblk(b, i, c))) for c in range(nwt)]
                + [pl.BlockSpec((ncmp, 2 * LANES), lambda b, i: (b, 0)),
                   pl.BlockSpec((1, LANES, ncmp), lambda b, i: (b, 0, 0)),
                   pl.BlockSpec((LANES, ncmp), lambda b, i: (0, 0))])
    return pl.pallas_call(
        functools.partial(_nsa_attn_kernel, ncmp=ncmp),
        grid=(batch, nqb),
        in_specs=in_specs,
        out_specs=row(N_W),
        out_shape=jax.ShapeDtypeStruct((batch * seq, N_W), BF16),
        scratch_shapes=[pltpu.VMEM((N_HEADS // 2, 2 * LANES, 2 * QB), BF16),
                        pltpu.VMEM((N_HEADS // 2, V_ROWS, 2 * QB), F32)],
        compiler_params=_cparams("parallel", "arbitrary"),
        name="nsa_attn",
    )(qt, gtt, proj, ka, vst, *([kwd] * nwt), *([vwt] * nwt), kcd, vct, ovt)


def _hgrn_kernel(f_ref, i_ref, q_ref, z_ref, lb_ref, hn_ref, out_ref, st_ref):
    T, D = H_TILE, H_DIM
    W = H_W

    @pl.when(pl.program_id(1) == 0)
    def _():
        st_ref[...] = jnp.zeros_like(st_ref)

    lb = lb_ref[...]
    fp = f_ref[...]
    e = jnp.exp(-jnp.abs(fp))
    r = 1.0 / (1.0 + e)
    sig = jnp.where(fp >= 0, r, e * r)
    nsig = jnp.where(fp >= 0, e * r, r)
    logf2 = jnp.log(lb + (1.0 - lb) * sig) * LOG2E
    kk = (1.0 - lb) * nsig

    row = lax.broadcasted_iota(jnp.int32, (T, T), 0)
    col = lax.broadcasted_iota(jnp.int32, (T, T), 1)
    tri01 = jnp.where(col <= row, 1.0, 0.0).astype(BF16)
    bcum = _dot01(tri01, logf2)

    segs = []
    sg = 2 * H_LAG
    while sg <= T:
        if sg == T:
            bmid = jnp.broadcast_to(bcum[sg // 2 - 1:sg // 2, :], (T, W))
        else:
            b3 = bcum.reshape(T // sg, sg, W)
            bmid = jnp.broadcast_to(b3[:, sg // 2 - 1:sg // 2, :], (T // sg, sg, W)).reshape(T, W)
        x = bcum - bmid
        shift = sg.bit_length() - 1
        pair_ok = ((jnp.right_shift(row, shift) == jnp.right_shift(col, shift))
                   & (jnp.bitwise_and(row, sg - 1) >= sg // 2) & (jnp.bitwise_and(col, sg - 1) < sg // 2))
        xb = x.astype(BF16)
        segs.append((jnp.exp2(jnp.minimum(xb, -xb)), pair_ok))
        sg *= 2

    sub3 = lax.broadcasted_iota(jnp.int32, (T // SUBLANES, SUBLANES, 1), 1)
    lag_ok = [jnp.bitwise_and(sub3, H_LAG - 1) >= d for d in range(H_LAG)]
    b_end = bcum[T - 1:T, :]
    v_all = i_ref[...].astype(F32)
    def head_scores(h):
        hs = slice(h * D, (h + 1) * D)
        q = _silu(q_ref[:, hs].astype(F32))
        k = kk[:, hs]
        b = bcum[:, hs]
        v = v_all[:, hs]
        vb = v.astype(BF16)
        a_seg = []
        qb, kb = q.astype(BF16), k.astype(BF16)
        for fac, _ in segs:
            f = fac[:, hs]
            a_seg.append(_dot_nt(qb * f, kb * f))
        st = st_ref[h]
        o_state = _dot_nt(qb * jnp.exp2(b.astype(BF16)), st.astype(BF16))
        kt = kb * jnp.exp2((b_end[:, hs] - b).astype(BF16))
        st_ref[h] = st * jnp.exp2(b_end[:, hs]) + _dot_tn(vb, kt)
        q3, k3, b3, v3 = (a.reshape(T // SUBLANES, SUBLANES, D) for a in (q, k, b, v))
        o3 = jnp.sum(q3 * k3, axis=2, keepdims=True) * v3
        for d in range(1, H_LAG):
            ks, bs, vs = (pltpu.roll(a, d, 1) for a in (k3, b3, v3))
            a = jnp.sum(q3 * ks * jnp.exp2(b3 - bs), axis=2, keepdims=True)
            o3 = o3 + jnp.where(lag_ok[d], a, 0.0) * vs
        return hs, a_seg, o3.reshape(T, D) + o_state, vb

    def head_finish(hs, a_seg, o_part, vb):
        a_tot = jnp.zeros((T, T), F32)
        for (_, pair_ok), a in zip(segs, a_seg):
            a_tot = jnp.where(pair_ok, a, a_tot)
        oh = o_part + _dot(a_tot.astype(BF16), vb)
        on = oh * lax.rsqrt(jnp.mean(oh * oh, axis=1, keepdims=True) + EPS) * hn_ref[:, hs]
        out_ref[:, hs] = (on * _silu(z_ref[:, hs].astype(F32))).astype(out_ref.dtype)

    pending = None
    for h in range(H_HEADS):
        cur = head_scores(h)
        if pending is not None:
            head_finish(*pending)
        pending = cur
    head_finish(*pending)


def _hgrn(proj, fgate, lb_row, hn_row, batch, seq):
    T = H_TILE
    nt = seq // T
    cb = lambda j: pl.BlockSpec((T, H_W), lambda b, c, j=j: (b * nt + c, j))
    vec = pl.BlockSpec((1, H_W), lambda b, c: (0, 0))
    return pl.pallas_call(
        _hgrn_kernel,
        grid=(batch, nt),
        in_specs=[pl.BlockSpec((T, H_W), lambda b, c: (b * nt + c, 0)), cb(1), cb(2), cb(3), vec, vec],
        out_specs=pl.BlockSpec((T, H_W), lambda b, c: (b * nt + c, 0)),
        out_shape=jax.ShapeDtypeStruct((batch * seq, H_W), BF16),
        scratch_shapes=[pltpu.VMEM((H_HEADS, H_DIM, H_DIM), F32)],
        compiler_params=_cparams("parallel", "arbitrary"),
        name="hgrn2",
    )(fgate, proj, proj, proj, lb_row, hn_row)


def _even_relayout(a):
    nb = 5 * M_W + 2 * M_HEADS
    g0 = nb + N_W + 6 * N_KV_W
    col = lambda lo, n: lax.slice_in_dim(a, lo, lo + n, axis=-1)
    pad = jnp.zeros(a.shape[:-1] + (C_NZ - C_SMALL - SM_G - 3 * N_HEADS,), a.dtype)
    out = jnp.concatenate([
        col(0, 5 * M_W),
        col(nb, N_W + 6 * N_KV_W),
        col(5 * M_W, 2 * M_HEADS),
        col(g0, 3 * N_HEADS), pad,
        col(g0 + 3 * N_HEADS, N_W),
    ], axis=-1)
    assert out.shape[-1] == EV_COLS
    return out


def _rope_tables(positions):
    inv_freq = ROPE_THETA ** (-np.arange(ROPE_HALF, dtype=np.float64) / ROPE_HALF)
    ang = positions.astype(np.float64)[:, None] * inv_freq[None, :]
    cos, sin = np.cos(ang).astype(np.float32), np.sin(ang).astype(np.float32)
    n = positions.shape[0]
    z8 = np.zeros((n, ROPE_HALF), np.float32)
    rest1 = np.ones((n, N_DIM - ROPE_DIM), np.float32)
    rest0 = np.zeros((n, N_DIM - ROPE_DIM), np.float32)
    cs = np.concatenate([cos, cos, rest1], axis=1)
    s1 = np.concatenate([z8, sin, rest0], axis=1)
    s2 = np.concatenate([-sin, z8, rest0], axis=1)
    two = lambda a: np.concatenate([a, a], axis=1)
    return two(cs), two(s1), two(s2)


def _overlap_matrix(seq):
    ncmp = seq // CMP_STRIDE
    n_sel = seq // SEL_BLOCK
    cs = np.arange(ncmp) * CMP_STRIDE
    ss = np.arange(n_sel) * SEL_BLOCK
    ov = np.clip(np.minimum(cs[:, None] + CMP_LEN, ss[None, :] + SEL_BLOCK)
                 - np.maximum(cs[:, None], ss[None, :]), 0, None).astype(np.float32)
    ov[ncmp - 1, :] = 0.0
    return np.pad(ov, ((0, 0), (0, LANES - n_sel)))


def _even_layer(x2d, res, norm_w, w_in, b_in, f_bias, conv_w, conv_b, head_norm, q_norm, k_norm,
                ck_pos, ck_w1, ck_w2, cv_pos, cv_w1, cv_w2, w_out, batch, seq):
    assert seq // SEL_BLOCK <= LANES and seq % KV_TILE == 0
    outs = _norm_proj(x2d, norm_w, _even_relayout(w_in.astype(BF16)), _even_relayout(b_in),
                      (C_SMALL, C_SMALL + LANES), tn=EV_COLS // 3, res=res)
    if res:
        x2d = outs[0]
    proj, small = outs[-2:]

    f_bias_row = jnp.zeros((1, LANES), F32).at[0, SM_F:SM_F + M_HEADS].set(f_bias)
    ya = _mlstm(proj, small, conv_w, conv_b, f_bias_row, head_norm, batch, seq)

    cs, s1, s2 = (jnp.asarray(a) for a in _rope_tables(np.arange(seq)))
    qn_row = jnp.concatenate([q_norm, q_norm]).reshape(1, LANES)
    kn_rows = jnp.concatenate([k_norm, k_norm], axis=1)
    qt, ka, kwd, vst, vwt, kc, vc, gtt = _nsa_prep(proj, small, cs, s1, s2, qn_row, kn_rows, batch, seq)

    ncmp = seq // CMP_STRIDE
    ccs, cs1, cs2 = (jnp.asarray(a) for a in _rope_tables(np.arange(ncmp) * CMP_STRIDE + CMP_LEN - 1))
    kcd, vct = _nsa_compress(
        kc.reshape(batch * ncmp, CMP_STRIDE * LANES), vc.reshape(batch * ncmp, CMP_STRIDE * LANES),
        _compress_params(ck_pos, ck_w1, ck_w2), _compress_params(cv_pos, cv_w1, cv_w2),
        ccs, cs1, cs2, kn_rows, batch, ncmp)

    ovt = jnp.asarray(_overlap_matrix(seq).T).astype(BF16)
    yb = _nsa_attn(proj, qt, gtt, ka, vst, kwd, vwt, kcd, vct, ovt, batch, seq)

    w_o = w_out.astype(BF16)
    return x2d, [(ya, w_o[:M_W]), (yb, w_o[M_W:])]


def _odd_layer(x2d, res, norm_w, w_in, b_in, lb, head_norm, w_out, batch, seq):
    outs = _norm_proj(x2d, norm_w, w_in.astype(BF16), b_in, (0, H_W), tn=H_W, res=res)
    if res:
        x2d = outs[0]
    proj, fgate = outs[-2:]
    y = _hgrn(proj, fgate, lb.reshape(1, H_W), head_norm.reshape(1, H_W), batch, seq)
    return x2d, [(y, w_out.astype(BF16))]


def kernel(x, norm_w, ev_w_in, ev_b_in, mlstm_f_bias, mlstm_conv_w, mlstm_conv_b, mlstm_head_norm, nsa_q_norm, nsa_k_norm, cmp_k_pos, cmp_k_w1, cmp_k_w2, cmp_v_pos, cmp_v_w1, cmp_v_w2, ev_w_out, od_w_in, od_b_in, hgrn_lb_logits, hgrn_head_norm, od_w_out):
    batch, seq, d = x.shape
    depth = norm_w.shape[0]
    lbs = jnp.cumsum(jax.nn.softmax(hgrn_lb_logits.astype(F32), axis=0), axis=0)
    x2d = x.reshape(batch * seq, d)
    res = []
    for layer in range(depth):
        j = layer // 2
        if layer % 2 == 0:
            x2d, res = _even_layer(x2d, res, norm_w[layer], ev_w_in[j], ev_b_in[j], mlstm_f_bias[j],
                                   mlstm_conv_w[j], mlstm_conv_b[j], mlstm_head_norm[j], nsa_q_norm[j],
                                   nsa_k_norm[j], cmp_k_pos[j], cmp_k_w1[j], cmp_k_w2[j], cmp_v_pos[j],
                                   cmp_v_w1[j], cmp_v_w2[j], ev_w_out[j], batch, seq)
        else:
            x2d, res = _odd_layer(x2d, res, norm_w[layer], od_w_in[j], od_b_in[j], lbs[j], hgrn_head_norm[j],
                                  od_w_out[j], batch, seq)
    if len(res) == 2:
        x2d = _out_proj2(res[0][0], res[1][0], res[0][1], res[1][1], x2d)
    elif res:
        x2d = _out_proj1(res[0][0], res[0][1], x2d)
    return x2d.reshape(batch, seq, d)
```

```python
import functools

import numpy as np
import jax
import jax.numpy as jnp
from jax import lax
from jax.experimental import pallas as pl
from jax.experimental.pallas import tpu as pltpu

F32 = jnp.float32
BF16 = jnp.bfloat16

LANES = 128
SUBLANES = 8
BF16_SUBLANES = 16
EPS = 1e-6
NEG = -1e30
VMEM_LIMIT_BYTES = 56 * 1024 * 1024

M_HEADS = 4
M_DIM = 128
M_W = M_HEADS * M_DIM
M_CHUNK = 128
CONV_K = 4
N_HEADS = 8
N_GROUPS = 2
N_HPG = N_HEADS // N_GROUPS
N_DIM = 64
N_W = N_HEADS * N_DIM
N_KV_W = N_GROUPS * N_DIM
CMP_LEN = 32
CMP_STRIDE = 16
CMP_HIDDEN = 128
CMP_CHUNK = 128
SEL_BLOCK = 64
SEL_SHIFT = 6
SEL_TOPK = 16
WINDOW = 512
Q_BLOCK = 128
ROPE_THETA = 500000.0
ROPE_DIM = N_DIM // 4
ROPE_HALF = ROPE_DIM // 2
KV_TILE = 512
V_ROWS = 80
LOG2E = 1.4426950408889634
H_HEADS = 8
H_DIM = 128
H_W = H_HEADS * H_DIM
H_TILE = 128
H_LAG = 2

C_MQ, C_MK, C_MV, C_MO, C_MZ = 0, 512, 1024, 1536, 2048
C_NQ = 2560
C_KC, C_VC, C_KS, C_VS = 3072, 3200, 3328, 3456
C_KW, C_VW, C_SMALL = 3584, 3712, 3840
C_NZ = 4096
EV_COLS = 4608
SM_I, SM_F, SM_G = 0, 4, 8


def _cparams(*sem):
    return pltpu.CompilerParams(dimension_semantics=sem, vmem_limit_bytes=VMEM_LIMIT_BYTES)


def _dot(a, b):
    return jnp.dot(a, b, preferred_element_type=F32)


def _dot_nt(a, b):
    return lax.dot_general(a, b, (((1,), (1,)), ((), ())), preferred_element_type=F32)


def _dot_tn(a, b):
    return lax.dot_general(a, b, (((0,), (0,)), ((), ())), preferred_element_type=F32)


def _sigmoid(x):
    return 1.0 / (1.0 + jnp.exp(-x))


def _silu(x):
    return x * _sigmoid(x)


def _split3(x):
    hi = x.astype(BF16)
    r = x - hi.astype(F32)
    mid = r.astype(BF16)
    lo = (r - mid.astype(F32)).astype(BF16)
    return hi, mid, lo


def _dot01(m01, x):
    hi, mid, lo = _split3(x)
    return _dot(m01, hi) + _dot(m01, mid) + _dot(m01, lo)


def _norm_proj_kernel(*refs, n_res, g_tiles, g_off):
    x_ref = refs[0]
    y_refs, wy_refs = refs[1:1 + n_res], refs[1 + n_res:1 + 2 * n_res]
    nw_ref, w_ref, b_ref = refs[1 + 2 * n_res:4 + 2 * n_res]
    outs = refs[4 + 2 * n_res:]
    xo_ref = outs[0] if n_res else None
    o_ref, g_ref, h_ref = outs[-3:]
    j = pl.program_id(1)

    @pl.when(j == 0)
    def _():
        x = x_ref[...]
        for y_ref, wy_ref in zip(y_refs, wy_refs):
            x = x + _dot(y_ref[...], wy_ref[...])
        if n_res:
            xo_ref[...] = x
        ms = jnp.mean(x * x, axis=-1, keepdims=True)
        h_ref[...] = (x * lax.rsqrt(ms + EPS) * nw_ref[...]).astype(BF16)

    acc = _dot(h_ref[...], w_ref[...]) + b_ref[...]
    o_ref[...] = acc.astype(o_ref.dtype)

    @pl.when((j >= g_tiles[0]) & (j < g_tiles[1]))
    def _():
        g_ref[...] = acc[:, g_off:g_off + g_ref.shape[1]]


def _norm_proj(x2d, norm_w, w_bf16, bias, g_cols, tn, res=(), tm=1024):
    t, d = x2d.shape
    n = w_bf16.shape[1]
    g_w = min(g_cols[1] - g_cols[0], tn)
    g_tiles = (g_cols[0] // tn, -(-g_cols[1] // tn))
    row = lambda w: pl.BlockSpec((tm, w), lambda i, j: (i, 0))
    const = lambda a: pl.BlockSpec(a.shape, lambda i, j: (0, 0))
    ys, wys = [y for y, _ in res], [wy for _, wy in res]
    out_specs = [pl.BlockSpec((tm, tn), lambda i, j: (i, j)),
                 pl.BlockSpec((tm, g_w), lambda i, j: (i, jnp.clip(j - g_tiles[0], 0, g_tiles[1] - g_tiles[0] - 1)))]
    out_shape = [jax.ShapeDtypeStruct((t, n), BF16), jax.ShapeDtypeStruct((t, g_cols[1] - g_cols[0]), F32)]
    if res:
        out_specs = [row(d)] + out_specs
        out_shape = [jax.ShapeDtypeStruct((t, d), F32)] + out_shape
    return pl.pallas_call(
        functools.partial(_norm_proj_kernel, n_res=len(res), g_tiles=g_tiles, g_off=g_cols[0] % tn),
        grid=(t // tm, n // tn),
        in_specs=([row(d)] + [row(y.shape[1]) for y in ys] + [const(wy) for wy in wys]
                  + [pl.BlockSpec((1, d), lambda i, j: (0, 0)),
                     pl.BlockSpec((d, tn), lambda i, j: (0, j)),
                     pl.BlockSpec((1, tn), lambda i, j: (0, j))]),
        out_specs=out_specs,
        out_shape=out_shape,
        scratch_shapes=[pltpu.VMEM((tm, d), BF16)],
        compiler_params=_cparams("parallel", "arbitrary"),
        name="norm_proj",
    )(x2d, *ys, *wys, norm_w.reshape(1, d), w_bf16, bias.reshape(1, n))


def _out_proj2_kernel(ya_ref, yb_ref, wa_ref, wb_ref, x_ref, o_ref):
    o_ref[...] = x_ref[...] + _dot(ya_ref[...], wa_ref[...]) + _dot(yb_ref[...], wb_ref[...])


def _out_proj2(ya, yb, wa, wb, x2d, tm=1024):
    t, d = x2d.shape
    ka, kb = ya.shape[1], yb.shape[1]
    return pl.pallas_call(
        _out_proj2_kernel,
        grid=(t // tm,),
        in_specs=[
            pl.BlockSpec((tm, ka), lambda i: (i, 0)),
            pl.BlockSpec((tm, kb), lambda i: (i, 0)),
            pl.BlockSpec((ka, d), lambda i: (0, 0)),
            pl.BlockSpec((kb, d), lambda i: (0, 0)),
            pl.BlockSpec((tm, d), lambda i: (i, 0)),
        ],
        out_specs=pl.BlockSpec((tm, d), lambda i: (i, 0)),
        out_shape=jax.ShapeDtypeStruct((t, d), F32),
        compiler_params=_cparams("parallel"),
        name="out_proj2",
    )(ya, yb, wa, wb, x2d)


def _out_proj1_kernel(y_ref, w_ref, x_ref, o_ref):
    o_ref[...] = x_ref[...] + _dot(y_ref[...], w_ref[...])


def _out_proj1(y, w, x2d, tm=1024):
    t, d = x2d.shape
    k = y.shape[1]
    return pl.pallas_call(
        _out_proj1_kernel,
        grid=(t // tm,),
        in_specs=[
            pl.BlockSpec((tm, k), lambda i: (i, 0)),
            pl.BlockSpec((k, d), lambda i: (0, 0)),
            pl.BlockSpec((tm, d), lambda i: (i, 0)),
        ],
        out_specs=pl.BlockSpec((tm, d), lambda i: (i, 0)),
        out_shape=jax.ShapeDtypeStruct((t, d), F32),
        compiler_params=_cparams("parallel"),
        name="out_proj1",
    )(y, w, x2d)


def _mlstm_kernel(q_ref, k_ref, v_ref, o_ref, z_ref, sm_ref, cw_ref, cb_ref, fb_ref, hn_ref,
                  out_ref, xbuf, c_ref, n_ref, m_ref):
    L, D = M_CHUNK, M_DIM
    NB = q_ref.shape[0]
    XR = xbuf.shape[1]
    X0 = BF16_SUBLANES

    @pl.when(pl.program_id(0) == 0)
    def _():
        xbuf[...] = jnp.zeros_like(xbuf)
        c_ref[...] = jnp.zeros_like(c_ref)
        n_ref[...] = jnp.zeros_like(n_ref)
        m_ref[...] = jnp.zeros_like(m_ref)

    t_i = lax.broadcasted_iota(jnp.int32, (L, XR), 0)
    r_i = lax.broadcasted_iota(jnp.int32, (L, XR), 1)
    row = lax.broadcasted_iota(jnp.int32, (L, L), 0)
    col = lax.broadcasted_iota(jnp.int32, (L, L), 1)
    causal = col <= row
    lane = lax.broadcasted_iota(jnp.int32, (L, LANES), 1)
    is_f = (lane >= SM_F) & (lane < SM_F + M_HEADS)

    heads = []
    for b in range(NB):
        xbuf[b, X0:X0 + L, 0:M_W] = q_ref[b]
        xbuf[b, X0:X0 + L, M_W:2 * M_W] = k_ref[b]
        x2 = xbuf[b]
        acc = cb_ref[...] + cw_ref[CONV_K - 1:CONV_K, :] * x2[X0:X0 + L].astype(F32)
        for d in range(1, CONV_K):
            shift = jnp.where(r_i == t_i + X0 - d, 1.0, 0.0).astype(BF16)
            acc = acc + cw_ref[CONV_K - 1 - d:CONV_K - d, :] * _dot(shift, x2)
        qk = _silu(acc)
        xbuf[b, 0:X0, :] = xbuf[b, L:L + X0, :]

        fpre = sm_ref[b] + fb_ref[...]
        gates = jnp.where(is_f, jnp.minimum(fpre, 0.0) - jnp.log(1.0 + jnp.exp(-jnp.abs(fpre))), fpre)
        bcols = _dot01(jnp.where(causal, 1.0, 0.0).astype(BF16), gates)
        gates_t = gates.T
        bcols_t = bcols.T
        for h in range(M_HEADS):
            hs = slice(h * D, (h + 1) * D)
            q = qk[:, hs]
            k = qk[:, M_W + h * D:M_W + (h + 1) * D] * (D ** -0.5)
            i_col = gates[:, SM_I + h:SM_I + h + 1]
            i_row = gates_t[SM_I + h:SM_I + h + 1, :]
            bcum_col = bcols[:, SM_F + h:SM_F + h + 1]
            bcum_row = bcols_t[SM_F + h:SM_F + h + 1, :]
            m_prev = m_ref[b * M_HEADS + h]
            dmat = jnp.where(causal, bcum_col - bcum_row + i_row, -jnp.inf)
            inter = bcum_col + m_prev
            m_row = jnp.maximum(jnp.max(dmat, axis=1, keepdims=True), inter)
            b_last = bcum_col[L - 1:L, :]
            dl = b_last - bcum_col + i_col
            m_new = jnp.maximum(b_last + m_prev, jnp.max(dl, axis=0, keepdims=True))
            wkk = jnp.exp(dl - m_new) * k
            heads.append(dict(
                b=b, hs=hs, st=b * M_HEADS + h, q=q, qb=q.astype(BF16), kb=k.astype(BF16),
                vb=v_ref[b, :, hs].astype(BF16), wkk=wkk,
                e=jnp.exp(dmat - m_row), w_inter=jnp.exp(inter - m_row), floor=jnp.exp(-m_row),
                decay=jnp.exp(b_last + m_prev - m_new), m_new=m_new))
    s_qk = [_dot_nt(t["qb"], t["kb"]) for t in heads]
    s_qc = [_dot(t["qb"], c_ref[t["st"]].astype(BF16)) for t in heads]
    s_kv = [_dot_tn(t["wkk"].astype(BF16), t["vb"]) for t in heads]
    ones_b = jnp.ones((L, LANES), BF16)
    smats = [(s_qk[j] * t["e"]).astype(BF16) for j, t in enumerate(heads)]
    s_sv = [_dot(smats[j], t["vb"]) for j, t in enumerate(heads)]
    s_ds = [_dot(smats[j], ones_b) for j in range(len(heads))]
    s_qn = [_dot((t["q"] * n_ref[t["st"]]).astype(BF16), ones_b) for t in heads]
    hhs = []
    for j, t in enumerate(heads):
        st = t["st"]
        num = s_sv[j] + t["w_inter"] * s_qc[j]
        den = s_ds[j] + t["w_inter"] * s_qn[j]
        hhs.append(num / jnp.maximum(jnp.abs(den), t["floor"]))
        c_ref[st] = t["decay"] * c_ref[st] + s_kv[j]
        n_ref[st] = t["decay"] * n_ref[st] + jnp.sum(t["wkk"], axis=0, keepdims=True)
        m_ref[st] = t["m_new"]
    s_ms = [_dot((hh * hh).astype(BF16), ones_b) for hh in hhs]
    for j, t in enumerate(heads):
        b, hs = t["b"], t["hs"]
        hn = hhs[j] * lax.rsqrt(s_ms[j] * (1.0 / D) + EPS) * hn_ref[:, hs]
        gate = _sigmoid(o_ref[b, :, hs].astype(F32)) * _silu(z_ref[b, :, hs].astype(F32))
        out_ref[b, :, hs] = (hn * gate).astype(out_ref.dtype)


def _mlstm(proj, small, conv_w, conv_b, f_bias_row, head_norm, batch, seq):
    L = M_CHUNK
    nc = seq // L
    p3 = proj.reshape(batch, seq, proj.shape[1])
    s3 = small.reshape(batch, seq, LANES)
    cb = lambda off: pl.BlockSpec((batch, L, M_W), lambda c, off=off: (0, c, off // M_W))
    full = lambda r, w: pl.BlockSpec((r, w), lambda c: (0, 0))
    out = pl.pallas_call(
        _mlstm_kernel,
        grid=(nc,),
        in_specs=[
            cb(C_MQ), cb(C_MK), cb(C_MV), cb(C_MO), cb(C_MZ),
            pl.BlockSpec((batch, L, LANES), lambda c: (0, c, 0)),
            full(CONV_K, 2 * M_W), full(1, 2 * M_W), full(1, LANES), full(1, M_W),
        ],
        out_specs=pl.BlockSpec((batch, L, M_W), lambda c: (0, c, 0)),
        out_shape=jax.ShapeDtypeStruct((batch, seq, M_W), BF16),
        scratch_shapes=[
            pltpu.VMEM((batch, 2 * L, 2 * M_W), BF16),
            pltpu.VMEM((batch * M_HEADS, M_DIM, M_DIM), F32),
            pltpu.VMEM((batch * M_HEADS, 1, M_DIM), F32),
            pltpu.VMEM((batch * M_HEADS, 1, 1), F32),
        ],
        compiler_params=_cparams("arbitrary"),
        name="mlstm",
    )(p3, p3, p3, p3, p3, s3, conv_w, conv_b.reshape(1, -1), f_bias_row, head_norm.reshape(1, -1))
    return out.reshape(batch * seq, M_W)


def _group_rms(x, w_row):
    r = lax.broadcasted_iota(jnp.int32, (LANES, LANES), 0)
    c = lax.broadcasted_iota(jnp.int32, (LANES, LANES), 1)
    same_head = jnp.where((r < N_DIM) == (c < N_DIM), 1.0, 0.0).astype(BF16)
    ms = _dot_hilo(x * x, same_head) * (1.0 / N_DIM)
    return x * lax.rsqrt(ms + EPS) * w_row


def _lane_gather01(src_lane, width=LANES):
    r = lax.broadcasted_iota(jnp.int32, (LANES, width), 0)
    c = lax.broadcasted_iota(jnp.int32, (LANES, width), 1)
    return jnp.where(r == src_lane(c), 1.0, 0.0).astype(BF16)


def _dot_hilo(x, m01):
    hi = x.astype(BF16)
    lo = (x - hi.astype(F32)).astype(BF16)
    return _dot(hi, m01) + _dot(lo, m01)


def _rope(x, cs, s1, s2):
    return x * cs + pltpu.roll(x, ROPE_HALF, 1) * s1 + pltpu.roll(x, LANES - ROPE_HALF, 1) * s2


def _dup_halves(x):
    dup = _lane_gather01(lambda c: jnp.where(c < LANES, 0, N_DIM) + (c & (N_DIM - 1)), 2 * LANES)
    both = _dot(x, dup).astype(BF16)
    return both[:, 0:LANES], both[:, LANES:2 * LANES]


def _transpose01(x):
    return _dot_nt(_lane_gather01(lambda c: c), x)


def _nsa_prep_kernel(q_ref, kv_ref, kw_ref, sm_ref, cs_ref, s1_ref, s2_ref, qn_ref, kn_ref,
                     qt_ref, ka_ref, kwd_ref, vst_ref, vwt_ref, kc_ref, vc_ref, gtt_ref,
                     *, tp, seq):
    cs, s1, s2 = cs_ref[...], s1_ref[...], s2_ref[...]
    nq = N_W // LANES
    kv = kv_ref[...].astype(F32)
    kwb = kw_ref[...].astype(F32)
    kc_ref[...] = kv[:, 0:LANES]
    vc_ref[...] = kv[:, LANES:2 * LANES]
    slabs = [q_ref[:, j * LANES:(j + 1) * LANES].astype(F32) for j in range(nq)]
    slabs += [kv[:, 2 * LANES:3 * LANES], kwb[:, 0:LANES]]
    gains = [qn_ref[...]] * nq + [kn_ref[1:2, :], kn_ref[2:3, :]]
    normed = [_group_rms(x, w) for x, w in zip(slabs, gains)]
    roped = [_rope(x, cs, s1, s2) for x in normed]
    qs = [(x * (N_DIM ** -0.5 * LOG2E)).astype(BF16) for x in roped[:nq]]
    k0, k1 = _dup_halves(roped[nq].astype(BF16))
    w0, w1 = _dup_halves(roped[nq + 1].astype(BF16))
    gates = _sigmoid(sm_ref[...])
    g_hi = gates.astype(BF16)
    g_lo = (gates - g_hi.astype(F32)).astype(BF16)
    qts = [_transpose01(x) for x in qs]
    vst = _transpose01(kv_ref[:, 3 * LANES:4 * LANES])
    vwt = _transpose01(kw_ref[:, LANES:2 * LANES])
    gtt_ref[...] = _transpose01(g_hi) + _transpose01(g_lo)
    for j in range(nq):
        qt_ref[j * LANES:(j + 1) * LANES, :] = qts[j].astype(BF16)
    pos0 = (pl.program_id(0) * tp) % seq
    pos = pos0 + lax.broadcasted_iota(jnp.int32, (tp, LANES), 0)
    blk = lax.broadcasted_iota(jnp.int32, (tp, LANES), 1)
    onehot = jnp.where(jnp.right_shift(pos, SEL_SHIFT) == blk, 1.0, 0.0).astype(BF16)
    ka_ref[:, 0:LANES] = k0
    ka_ref[:, LANES:2 * LANES] = onehot
    ka_ref[:, 2 * LANES:3 * LANES] = k1
    ka_ref[:, 3 * LANES:4 * LANES] = onehot
    fill =(lax.broadcasted_iota(jnp.int32, (V_ROWS - N_DIM, Q_BLOCK), 0) == 0).astype(BF16)
    for c in range(tp // Q_BLOCK):
        for g in range(N_GROUPS):
            vst_ref[c, g * V_ROWS:g * V_ROWS + N_DIM, :] = (
                vst[g * N_DIM:(g + 1) * N_DIM, c * Q_BLOCK:(c + 1) * Q_BLOCK].astype(BF16))
            vst_ref[c, g * V_ROWS + N_DIM:(g + 1) * V_ROWS, :] = fill
    kwd_ref[:, 0:LANES] = w0
    kwd_ref[:, LANES:2 * LANES] = w1
    vwt_ref[...] = vwt.astype(BF16)


def _nsa_prep(proj, small, cs, s1, s2, qn_row, kn_rows, batch, seq):
    tp = KV_TILE
    t = batch * seq
    nsb = seq // tp
    rb = lambda w, off: pl.BlockSpec((tp, w), lambda i, off=off, w=w: (i, off // w))
    tab = pl.BlockSpec((tp, LANES), lambda i: (i % nsb, 0))
    o = lambda w: pl.BlockSpec((tp, w), lambda i: (i, 0))
    ot = lambda r: pl.BlockSpec((r, tp), lambda i: (0, i))
    sds = lambda w, dt: jax.ShapeDtypeStruct((t, w), dt)
    return pl.pallas_call(
        functools.partial(_nsa_prep_kernel, tp=tp, seq=seq),
        grid=(t // tp,),
        in_specs=[rb(512, C_NQ), rb(512, C_KC), rb(512, C_KW), o(LANES), tab, tab, tab,
                  pl.BlockSpec((1, LANES), lambda i: (0, 0)), pl.BlockSpec((3, LANES), lambda i: (0, 0))],
        out_specs=[ot(N_W), o(512), o(256),
                   pl.BlockSpec((tp // Q_BLOCK, N_GROUPS * V_ROWS, Q_BLOCK), lambda i: (i, 0, 0)),
                   ot(LANES), o(128), o(128), ot(LANES)],
        out_shape=[jax.ShapeDtypeStruct((N_W, t), BF16), sds(512, BF16), sds(256, BF16),
                   jax.ShapeDtypeStruct((t // Q_BLOCK, N_GROUPS * V_ROWS, Q_BLOCK), BF16),
                   jax.ShapeDtypeStruct((LANES, t), BF16),
                   sds(128, F32), sds(128, F32), jax.ShapeDtypeStruct((LANES, t), F32)],
        compiler_params=_cparams("parallel"),
        name="nsa_prep",
    )(proj, proj, proj, small, cs, s1, s2, qn_row, kn_rows)


def _gelu_tanh(x):
    return 0.5 * x * (1.0 + jnp.tanh(0.7978845608028654 * (x + 0.044715 * x * x * x)))


def _compress_mlp(x_ref, pa_ref, pb_ref, wa_ref, wb_ref, w2_ref, hb_ref):
    n = x_ref.shape[0] // CMP_STRIDE
    ha = hb = None
    for r in range(CMP_STRIDE):
        xr = x_ref[pl.ds(r, n, stride=CMP_STRIDE), :]
        ws = slice(r * LANES, (r + 1) * LANES)
        da = _dot((xr + pa_ref[:, ws]).astype(BF16), wa_ref[ws, :])
        db = _dot((xr + pb_ref[:, ws]).astype(BF16), wb_ref[ws, :])
        ha = da if ha is None else ha + da
        hb = db if hb is None else hb + db
    hb_ref[0:n, :] = hb
    hb_ref[n:n + SUBLANES, :] = jnp.zeros((SUBLANES, hb_ref.shape[1]), F32)
    hid = ha + hb_ref[1:n + 1, :]
    return _dot(_gelu_tanh(hid).astype(BF16), w2_ref[...])


def _nsa_compress_kernel(kx_ref, vx_ref, kpa_ref, kpb_ref, vpa_ref, vpb_ref, kwa_ref, kwb_ref, kw2_ref,
                         vwa_ref, vwb_ref, vw2_ref, cs_ref, s1_ref, s2_ref, kn_ref,
                         kcd_ref, vct_ref, hb_ref):
    n = kx_ref.shape[0] // CMP_STRIDE
    rowi = lax.broadcasted_iota(jnp.int32, (n, LANES), 0)
    live = rowi < n - 1
    kc = _compress_mlp(kx_ref, kpa_ref, kpb_ref, kwa_ref, kwb_ref, kw2_ref, hb_ref)
    kc = _rope(_group_rms(kc, kn_ref[0:1, :]), cs_ref[...], s1_ref[...], s2_ref[...])
    kc = jnp.where(live, kc, 0.0)
    c0, c1 = _dup_halves(kc.astype(BF16))
    kcd_ref[:, 0:LANES] = c0
    kcd_ref[:, LANES:2 * LANES] = c1
    vc = _compress_mlp(vx_ref, vpa_ref, vpb_ref, vwa_ref, vwb_ref, vw2_ref, hb_ref)
    vc = jnp.where(live, vc, 0.0)
    vct_ref[0] = vc.T.astype(BF16)


def _nsa_compress(kx, vx, kp, vp, cs, s1, s2, kn_rows, batch, nrow):
    full = lambda a: pl.BlockSpec(a.shape, lambda b: (0,) * a.ndim)
    xs = pl.BlockSpec((nrow * CMP_STRIDE, LANES), lambda b: (b, 0))
    args = (kx, vx, kp[0], kp[1], vp[0], vp[1], kp[2], kp[3], kp[4], vp[2], vp[3], vp[4], cs, s1, s2, kn_rows)
    return pl.pallas_call(
        _nsa_compress_kernel,
        grid=(batch,),
        in_specs=[xs, xs] + [full(a) for a in args[2:]],
        out_specs=[pl.BlockSpec((nrow, 2 * LANES), lambda b: (b, 0)),
                   pl.BlockSpec((1, LANES, nrow), lambda b: (b, 0, 0))],
        out_shape=[jax.ShapeDtypeStruct((batch * nrow, 2 * LANES), BF16),
                   jax.ShapeDtypeStruct((batch, LANES, nrow), BF16)],
        scratch_shapes=[pltpu.VMEM((nrow + SUBLANES, 2 * CMP_HIDDEN), F32)],
        compiler_params=_cparams("parallel"),
        name="nsa_compress",
    )(*args)


def _compress_params(pos, w1, w2):
    def expand(w1_half, pos_half):
        w = w1_half.reshape(CMP_STRIDE, N_DIM, CMP_HIDDEN)
        z = jnp.zeros_like(w)
        g0 = jnp.concatenate([w, z], axis=-1)
        g1 = jnp.concatenate([z, w], axis=-1)
        wx = jnp.stack([g0, g1], axis=1).reshape(CMP_STRIDE * LANES, 2 * CMP_HIDDEN)
        px = jnp.concatenate([pos_half, pos_half], axis=-1).reshape(1, CMP_STRIDE * LANES)
        return wx.astype(BF16), px.astype(F32)

    half = CMP_STRIDE * N_DIM
    wa, pa = expand(w1[:half], pos[:CMP_STRIDE])
    wb, pb = expand(w1[half:], pos[CMP_STRIDE:])
    z2 = jnp.zeros_like(w2)
    w2x = jnp.concatenate([jnp.concatenate([w2, z2], axis=1), jnp.concatenate([z2, w2], axis=1)], axis=0)
    return pa, pb, wa, wb, w2x.astype(BF16)


def _nsa_attn_kernel(qt_ref, gtt_ref, z_ref, ka_ref, vst_ref,
                     kw0, kw1, kw2, kw3, kw4, wt0, wt1, wt2, wt3, wt4,
                     kcd_ref, vct_ref, ovt_ref, out_ref, qa_ref, acc_ref, *, ncmp):
    QB = Q_BLOCK
    NP = N_HEADS // 2
    PW = 2 * QB
    i = pl.program_id(1)
    t0 = i * QB
    kw_refs = (kw0, kw1, kw2, kw3, kw4)
    vwt_refs = (wt0, wt1, wt2, wt3, wt4)
    nwt = len(kw_refs)

    sub = lax.broadcasted_iota(jnp.int32, (QB, QB), 0)
    lane = lax.broadcasted_iota(jnp.int32, (QB, QB), 1)
    top = sub < N_DIM
    t_row = t0 + lax.broadcasted_iota(jnp.int32, (1, QB), 1)
    t_row2 = jnp.concatenate([t_row, t_row], axis=1)

    def pair_q(pr):
        slab = qt_ref[pr * LANES:(pr + 1) * LANES, :]
        zero = jnp.zeros_like(slab)
        return jnp.concatenate([jnp.where(top, slab, zero), jnp.where(top, zero, slab)], axis=1)

    blk_f = sub.astype(F32)
    cur = jnp.right_shift(t_row, SEL_SHIFT)
    grp = lambda pr: pr // (N_HPG // 2)
    for pr in range(NP):
        qa_ref[pr, 0:LANES, :] = pair_q(pr)

    def compressed(n):
        nidx = lax.broadcasted_iota(jnp.int32, (n, PW), 0)
        ok = (nidx * CMP_STRIDE + (CMP_LEN - 1) <= t_row2) & (nidx < ncmp - 1)
        any_ok = t_row2 >= CMP_LEN - 1
        s_cmp = [_dot(kcd_ref[0:n, grp(pr) * LANES:(grp(pr) + 1) * LANES], qa_ref[pr, 0:LANES, :])
                 for pr in range(NP)]
        p_cmp = []
        for pr in range(NP):
            s = jnp.where(ok, s_cmp[pr], NEG)
            mx = jnp.max(s, axis=0, keepdims=True)
            p = jnp.exp2(s - mx)
            den = jnp.sum(p, axis=0, keepdims=True)
            p_cmp.append(p * jnp.where(any_ok, 1.0 / den, 0.0))
        outs = [_dot(vct_ref[0, grp(pr) * N_DIM:(grp(pr) + 1) * N_DIM, 0:n], p_cmp[pr].astype(BF16))
                for pr in range(NP)]
        for g in range(N_GROUPS):
            psum = jnp.zeros((n, QB), F32)
            for pp in range(N_HPG // 2):
                p = p_cmp[g * (N_HPG // 2) + pp]
                psum = psum + p[:, 0:QB] + p[:, QB:PW]
            p_hi = psum.astype(BF16)
            p_lo = (psum - p_hi.astype(F32)).astype(BF16)
            outs.append(_dot(ovt_ref[:, 0:n], p_hi) + _dot(ovt_ref[:, 0:n], p_lo))
        return tuple(outs)

    n_opts = list(range(CMP_CHUNK, ncmp + 1, CMP_CHUNK))
    need = (t0 + QB - CMP_LEN) // CMP_STRIDE + 1
    outs = lax.switch(jnp.clip((need - 1) // CMP_CHUNK, 0, len(n_opts) - 1),
                      [functools.partial(compressed, n) for n in n_opts])
    o_cmp, imps = outs[:NP], outs[NP:]
    forced = (sub == 0) | (sub == cur) | (sub == cur - 1)
    n_rounds = SEL_TOPK - 3
    valid = sub <= cur
    works0 = [jnp.where(valid, jnp.where(forced, -jnp.inf, imp), NEG) for imp in imps]

    def rounds(works, exact):
        sels = [forced for _ in range(N_GROUPS)]
        for _ in range(n_rounds):
            for g in range(N_GROUPS):
                mx = jnp.max(works[g], axis=0, keepdims=True)
                if exact:
                    idx = jnp.min(jnp.where(works[g] == mx, blk_f, float(LANES)), axis=0, keepdims=True)
                    pick = blk_f == idx
                else:
                    pick = works[g] == mx
                sels[g] = sels[g] | pick
                works[g] = jnp.where(pick, -jnp.inf, works[g])
        return tuple(jnp.where(s & valid, 0.0, NEG) for s in sels)

    bias = rounds(list(works0), exact=False)
    picked = [jnp.sum(jnp.where((b == 0.0) & jnp.logical_not(forced), 1.0, 0.0), axis=0, keepdims=True)
              for b in bias]
    tied = jnp.max(functools.reduce(jnp.maximum, picked)) > n_rounds
    bias = lax.cond(tied, lambda: rounds(list(works0), exact=True), lambda: bias)
    for g in range(N_GROUPS):
        neg = bias[g].astype(BF16)
        neg2 = jnp.concatenate([neg, neg], axis=1)
        for pp in range(N_HPG // 2):
            qa_ref[g * (N_HPG // 2) + pp, LANES:2 * LANES, :] = neg2

    CH = QB
    above = jnp.concatenate([sub > lane, sub > lane], axis=1)

    def sel_scores(cj):
        k0 = pl.multiple_of(cj * CH, CH)
        return [_dot(ka_ref[pl.ds(k0, CH), 2 * grp(pr) * LANES:2 * (grp(pr) + 1) * LANES], qa_ref[pr])
                for pr in range(NP)]

    def sel_softmax(ss, ms, masked):
        new, alphas, ps = [], [], []
        for pr in range(NP):
            s = jnp.where(above, NEG, ss[pr]) if masked else ss[pr]
            m_new = jnp.maximum(ms[pr], jnp.max(s, axis=0, keepdims=True))
            alphas.append(jnp.exp2(ms[pr] - m_new))
            ps.append(jnp.exp2(s - m_new).astype(BF16))
            new.append(m_new)
        return tuple(new), alphas, ps

    def sel_values(cj, alphas, ps):
        for pr in range(NP):
            g = grp(pr)
            acc_ref[pr] = alphas[pr] * acc_ref[pr] + _dot(vst_ref[cj, g * V_ROWS:(g + 1) * V_ROWS, :], ps[pr])

    def sel_chunk(cj, ms, masked):
        ms, alphas, ps = sel_softmax(sel_scores(cj), ms, masked)
        sel_values(cj, alphas, ps)
        return ms

    def sel_tile(jt, ms, nc):
        c0 = jt * nc
        k0 = pl.multiple_of(c0 * CH, CH)
        big = [_dot(ka_ref[pl.ds(k0, nc * CH), 2 * grp(pr) * LANES:2 * (grp(pr) + 1) * LANES], qa_ref[pr])
               for pr in range(NP)]
        for c in range(nc):
            ss = [b[c * CH:(c + 1) * CH] for b in big]
            ms, alphas, ps = sel_softmax(ss, ms, False)
            sel_values(c0 + c, alphas, ps)
        return ms

    acc_ref[...] = jnp.zeros_like(acc_ref)
    ms = tuple(jnp.full((1, PW), 2.0 * NEG, F32) for _ in range(NP))
    done = 0
    for nc in (4 * KV_TILE // CH, 2 * KV_TILE // CH, KV_TILE // CH):
        n_run = (i - done) // nc
        ms = lax.fori_loop(done // nc, done // nc + n_run, functools.partial(sel_tile, nc=nc), ms)
        done = done + n_run * nc
    ms = lax.fori_loop(done, i, lambda cj, ms: sel_chunk(cj, ms, False), ms)
    s_diag = sel_scores(i)
    s_win = [[_dot(kw_refs[c][:, grp(pr) * LANES:(grp(pr) + 1) * LANES], qa_ref[pr, 0:LANES, :])
              for c in range(nwt)] for pr in range(NP)]
    ms, alphas, ps_diag = sel_softmax(s_diag, ms, True)
    p_win, d_win = [], []
    for pr in range(NP):
        ss = []
        for c in range(nwt):
            ok = (i - (nwt - 1) + c) >= 0
            if c == 0:
                ok = ok & above
            elif c == nwt - 1:
                ok = ok & jnp.logical_not(above)
            ss.append(jnp.where(ok, s_win[pr][c], NEG))
        mx = ss[0].max(axis=0, keepdims=True)
        for c in range(1, nwt):
            mx = jnp.maximum(mx, ss[c].max(axis=0, keepdims=True))
        ps = [jnp.exp2(ss[c] - mx) for c in range(nwt)]
        den = ps[0].sum(axis=0, keepdims=True)
        for c in range(1, nwt):
            den = den + ps[c].sum(axis=0, keepdims=True)
        p_win.append([p.astype(BF16) for p in ps])
        d_win.append(den)
    sel_values(i, alphas, ps_diag)
    o_win = []
    for pr in range(NP):
        g = grp(pr)
        acc = _dot(vwt_refs[0][g * N_DIM:(g + 1) * N_DIM, :], p_win[pr][0])
        for c in range(1, nwt):
            acc = acc + _dot(vwt_refs[c][g * N_DIM:(g + 1) * N_DIM, :], p_win[pr][c])
        o_win.append(acc * (1.0 / d_win[pr]))
    o_sel = [acc_ref[pr, 0:N_DIM, :] * (1.0 / acc_ref[pr, N_DIM:N_DIM + 1, :]) for pr in range(NP)]

    for pr in range(NP):
        halves = []
        for par in range(2):
            h = 2 * pr + par
            cs = slice(par * QB, (par + 1) * QB)
            gc = gtt_ref[SM_G + 3 * h:SM_G + 3 * h + 1, :]
            gs = gtt_ref[SM_G + 3 * h + 1:SM_G + 3 * h + 2, :]
            gw = gtt_ref[SM_G + 3 * h + 2:SM_G + 3 * h + 3, :]
            halves.append(gc * o_cmp[pr][:, cs] + gs * o_sel[pr][:, cs] + gw * o_win[pr][:, cs])
        y = jnp.concatenate(halves, axis=0).T
        sl = slice(pr * LANES, (pr + 1) * LANES)
        out_ref[:, sl] = (y * _silu(z_ref[:, sl].astype(F32))).astype(out_ref.dtype)


def _nsa_attn(proj, qt, gtt, ka, vst, kwd, vwt, kcd, vct, ovt, batch, seq):
    QB = Q_BLOCK
    nqb = seq // QB
    ncmp = seq // CMP_STRIDE
    nwt = WINDOW // QB + 1
    row = lambda w: pl.BlockSpec((QB, w), lambda b, i: (b * nqb + i, 0))
    colb = lambda r: pl.BlockSpec((r, QB), lambda b, i: (0, b * nqb + i))
    wblk = lambda b, i, c: b * nqb + jnp.maximum(i - (nwt - 1) + c, 0)
    in_specs = ([colb(N_W), colb(LANES), pl.BlockSpec((QB, N_W), lambda b, i: (b * nqb + i, C_NZ // N_W)),
                 pl.BlockSpec((seq, 4 * LANES), lambda b, i: (b, 0)),
                 pl.BlockSpec((nqb, N_GROUPS * V_ROWS, QB), lambda b, i: (b, 0, 0))]
                + [pl.BlockSpec((QB, 2 * LANES), lambda b, i, c=c: (wblk(b, i, c), 0)) for c in range(nwt)]
                + [pl.BlockSpec((LANES, QB), lambda b, i, c=c: (0, wblk(b, i, c))) for c in range(nwt)]
                + [pl.BlockSpec((ncmp, 2 * LANES), lambda b, i: (b, 0)),
                   pl.BlockSpec((1, LANES, ncmp), lambda b, i: (b, 0, 0)),
                   pl.BlockSpec((LANES, ncmp), lambda b, i: (0, 0))])
    return pl.pallas_call(
        functools.partial(_nsa_attn_kernel, ncmp=ncmp),
        grid=(batch, nqb),
        in_specs=in_specs,
        out_specs=row(N_W),
        out_shape=jax.ShapeDtypeStruct((batch * seq, N_W), BF16),
        scratch_shapes=[pltpu.VMEM((N_HEADS // 2, 2 * LANES, 2 * QB), BF16),
                        pltpu.VMEM((N_HEADS // 2, V_ROWS, 2 * QB), F32)],
        compiler_params=_cparams("parallel", "arbitrary"),
        name="nsa_attn",
    )(qt, gtt, proj, ka, vst, *([kwd] * nwt), *([vwt] * nwt), kcd, vct, ovt)


def _hgrn_kernel(f_ref, i_ref, q_ref, z_ref, lb_ref, hn_ref, out_ref, st_ref):
    T, D = H_TILE, H_DIM
    W = H_W

    @pl.when(pl.program_id(1) == 0)
    def _():
        st_ref[...] = jnp.zeros_like(st_ref)

    lb = lb_ref[...]
    fp = f_ref[...]
    e = jnp.exp(-jnp.abs(fp))
    r = 1.0 / (1.0 + e)
    sig = jnp.where(fp >= 0, r, e * r)
    nsig = jnp.where(fp >= 0, e * r, r)
    logf2 = jnp.log(lb + (1.0 - lb) * sig) * LOG2E
    kk = (1.0 - lb) * nsig

    row = lax.broadcasted_iota(jnp.int32, (T, T), 0)
    col = lax.broadcasted_iota(jnp.int32, (T, T), 1)
    tri01 = jnp.where(col <= row, 1.0, 0.0).astype(BF16)
    bcum = _dot01(tri01, logf2)

    segs = []
    sg = 2 * H_LAG
    while sg <= T:
        if sg == T:
            bmid = jnp.broadcast_to(bcum[sg // 2 - 1:sg // 2, :], (T, W))
        else:
            b3 = bcum.reshape(T // sg, sg, W)
            bmid = jnp.broadcast_to(b3[:, sg // 2 - 1:sg // 2, :], (T // sg, sg, W)).reshape(T, W)
        x = bcum - bmid
        shift = sg.bit_length() - 1
        pair_ok = ((jnp.right_shift(row, shift) == jnp.right_shift(col, shift))
                   & (jnp.bitwise_and(row, sg - 1) >= sg // 2) & (jnp.bitwise_and(col, sg - 1) < sg // 2))
        xb = x.astype(BF16)
        segs.append((jnp.exp2(jnp.minimum(xb, -xb)), pair_ok))
        sg *= 2

    sub3 = lax.broadcasted_iota(jnp.int32, (T // SUBLANES, SUBLANES, 1), 1)
    lag_ok = [jnp.bitwise_and(sub3, H_LAG - 1) >= d for d in range(H_LAG)]
    b_end = bcum[T - 1:T, :]
    v_all = i_ref[...].astype(F32)
    def head_scores(h):
        hs = slice(h * D, (h + 1) * D)
        q = _silu(q_ref[:, hs].astype(F32))
        k = kk[:, hs]
        b = bcum[:, hs]
        v = v_all[:, hs]
        vb = v.astype(BF16)
        a_seg = []
        qb, kb = q.astype(BF16), k.astype(BF16)
        for fac, _ in segs:
            f = fac[:, hs]
            a_seg.append(_dot_nt(qb * f, kb * f))
        st = st_ref[h]
        o_state = _dot_nt(qb * jnp.exp2(b.astype(BF16)), st.astype(BF16))
        kt = kb * jnp.exp2((b_end[:, hs] - b).astype(BF16))
        st_ref[h] = st * jnp.exp2(b_end[:, hs]) + _dot_tn(vb, kt)
        q3, k3, b3, v3 = (a.reshape(T // SUBLANES, SUBLANES, D) for a in (q, k, b, v))
        o3 = jnp.sum(q3 * k3, axis=2, keepdims=True) * v3
        for d in range(1, H_LAG):
            ks, bs, vs = (pltpu.roll(a, d, 1) for a in (k3, b3, v3))
            a = jnp.sum(q3 * ks * jnp.exp2(b3 - bs), axis=2, keepdims=True)
            o3 = o3 + jnp.where(lag_ok[d], a, 0.0) * vs
        return hs, a_seg, o3.reshape(T, D) + o_state, vb

    def head_finish(hs, a_seg, o_part, vb):
        a_tot = jnp.zeros((T, T), F32)
        for (_, pair_ok), a in zip(segs, a_seg):
            a_tot = jnp.where(pair_ok, a, a_tot)
        oh = o_part + _dot(a_tot.astype(BF16), vb)
        on = oh * lax.rsqrt(jnp.mean(oh * oh, axis=1, keepdims=True) + EPS) * hn_ref[:, hs]
        out_ref[:, hs] = (on * _silu(z_ref[:, hs].astype(F32))).astype(out_ref.dtype)

    pending = None
    for h in range(H_HEADS):
        cur = head_scores(h)
        if pending is not None:
            head_finish(*pending)
        pending = cur
    head_finish(*pending)


def _hgrn(proj, fgate, lb_row, hn_row, batch, seq):
    T = H_TILE
    nt = seq // T
    cb = lambda j: pl.BlockSpec((T, H_W), lambda b, c, j=j: (b * nt + c, j))
    vec = pl.BlockSpec((1, H_W), lambda b, c: (0, 0))
    return pl.pallas_call(
        _hgrn_kernel,
        grid=(batch, nt),
        in_specs=[pl.BlockSpec((T, H_W), lambda b, c: (b * nt + c, 0)), cb(1), cb(2), cb(3), vec, vec],
        out_specs=pl.BlockSpec((T, H_W), lambda b, c: (b * nt + c, 0)),
        out_shape=jax.ShapeDtypeStruct((batch * seq, H_W), BF16),
        scratch_shapes=[pltpu.VMEM((H_HEADS, H_DIM, H_DIM), F32)],
        compiler_params=_cparams("parallel", "arbitrary"),
        name="hgrn2",
    )(fgate, proj, proj, proj, lb_row, hn_row)


def _even_relayout(a):
    nb = 5 * M_W + 2 * M_HEADS
    g0 = nb + N_W + 6 * N_KV_W
    col = lambda lo, n: lax.slice_in_dim(a, lo, lo + n, axis=-1)
    pad = jnp.zeros(a.shape[:-1] + (C_NZ - C_SMALL - SM_G - 3 * N_HEADS,), a.dtype)
    out = jnp.concatenate([
        col(0, 5 * M_W),
        col(nb, N_W + 6 * N_KV_W),
        col(5 * M_W, 2 * M_HEADS),
        col(g0, 3 * N_HEADS), pad,
        col(g0 + 3 * N_HEADS, N_W),
    ], axis=-1)
    assert out.shape[-1] == EV_COLS
    return out


def _rope_tables(positions):
    inv_freq = ROPE_THETA ** (-np.arange(ROPE_HALF, dtype=np.float64) / ROPE_HALF)
    ang = positions.astype(np.float64)[:, None] * inv_freq[None, :]
    cos, sin = np.cos(ang).astype(np.float32), np.sin(ang).astype(np.float32)
    n = positions.shape[0]
    z8 = np.zeros((n, ROPE_HALF), np.float32)
    rest1 = np.ones((n, N_DIM - ROPE_DIM), np.float32)
    rest0 = np.zeros((n, N_DIM - ROPE_DIM), np.float32)
    cs = np.concatenate([cos, cos, rest1], axis=1)
    s1 = np.concatenate([z8, sin, rest0], axis=1)
    s2 = np.concatenate([-sin, z8, rest0], axis=1)
    two = lambda a: np.concatenate([a, a], axis=1)
    return two(cs), two(s1), two(s2)


def _overlap_matrix(seq):
    ncmp = seq // CMP_STRIDE
    n_sel = seq // SEL_BLOCK
    cs = np.arange(ncmp) * CMP_STRIDE
    ss = np.arange(n_sel) * SEL_BLOCK
    ov = np.clip(np.minimum(cs[:, None] + CMP_LEN, ss[None, :] + SEL_BLOCK)
                 - np.maximum(cs[:, None], ss[None, :]), 0, None).astype(np.float32)
    ov[ncmp - 1, :] = 0.0
    return np.pad(ov, ((0, 0), (0, LANES - n_sel)))


def _even_layer(x2d, res, norm_w, w_in, b_in, f_bias, conv_w, conv_b, head_norm, q_norm, k_norm,
                ck_pos, ck_w1, ck_w2, cv_pos, cv_w1, cv_w2, w_out, batch, seq):
    assert seq // SEL_BLOCK <= LANES and seq % KV_TILE == 0
    outs = _norm_proj(x2d, norm_w, _even_relayout(w_in.astype(BF16)), _even_relayout(b_in),
                      (C_SMALL, C_SMALL + LANES), tn=EV_COLS // 3, res=res)
    if res:
        x2d = outs[0]
    proj, small = outs[-2:]

    f_bias_row = jnp.zeros((1, LANES), F32).at[0, SM_F:SM_F + M_HEADS].set(f_bias)
    ya = _mlstm(proj, small, conv_w, conv_b, f_bias_row, head_norm, batch, seq)

    cs, s1, s2 = (jnp.asarray(a) for a in _rope_tables(np.arange(seq)))
    qn_row = jnp.concatenate([q_norm, q_norm]).reshape(1, LANES)
    kn_rows = jnp.concatenate([k_norm, k_norm], axis=1)
    qt, ka, kwd, vst, vwt, kc, vc, gtt = _nsa_prep(proj, small, cs, s1, s2, qn_row, kn_rows, batch, seq)

    ncmp = seq // CMP_STRIDE
    ccs, cs1, cs2 = (jnp.asarray(a) for a in _rope_tables(np.arange(ncmp) * CMP_STRIDE + CMP_LEN - 1))
    kcd, vct = _nsa_compress(
        kc, vc,
        _compress_params(ck_pos, ck_w1, ck_w2), _compress_params(cv_pos, cv_w1, cv_w2),
        ccs, cs1, cs2, kn_rows, batch, ncmp)

    ovt = jnp.asarray(_overlap_matrix(seq).T).astype(BF16)
    yb = _nsa_attn(proj, qt, gtt, ka, vst, kwd, vwt, kcd, vct, ovt, batch, seq)

    w_o = w_out.astype(BF16)
    return x2d, [(ya, w_o[:M_W]), (yb, w_o[M_W:])]


def _odd_layer(x2d, res, norm_w, w_in, b_in, lb, head_norm, w_out, batch, seq):
    outs = _norm_proj(x2d, norm_w, w_in.astype(BF16), b_in, (0, H_W), tn=H_W, res=res)
    if res:
        x2d = outs[0]
    proj, fgate = outs[-2:]
    y = _hgrn(proj, fgate, lb.reshape(1, H_W), head_norm.reshape(1, H_W), batch, seq)
    return x2d, [(y, w_out.astype(BF16))]


def kernel(x, norm_w, ev_w_in, ev_b_in, mlstm_f_bias, mlstm_conv_w, mlstm_conv_b, mlstm_head_norm, nsa_q_norm, nsa_k_norm, cmp_k_pos, cmp_k_w1, cmp_k_w2, cmp_v_pos, cmp_v_w1, cmp_v_w2, ev_w_out, od_w_in, od_b_in, hgrn_lb_logits, hgrn_head_norm, od_w_out):
    batch, seq, d = x.shape
    depth = norm_w.shape[0]
    lbs = jnp.cumsum(jax.nn.softmax(hgrn_lb_logits.astype(F32), axis=0), axis=0)
    x2d = x.reshape(batch * seq, d)
    res = []
    for layer in range(depth):
        j = layer // 2
        if layer % 2 == 0:
            x2d, res = _even_layer(x2d, res, norm_w[layer], ev_w_in[j], ev_b_in[j], mlstm_f_bias[j],
                                   mlstm_conv_w[j], mlstm_conv_b[j], mlstm_head_norm[j], nsa_q_norm[j],
                                   nsa_k_norm[j], cmp_k_pos[j], cmp_k_w1[j], cmp_k_w2[j], cmp_v_pos[j],
                                   cmp_v_w1[j], cmp_v_w2[j], ev_w_out[j], batch, seq)
        else:
            x2d, res = _odd_layer(x2d, res, norm_w[layer], od_w_in[j], od_b_in[j], lbs[j], hgrn_head_norm[j],
                                  od_w_out[j], batch, seq)
    if len(res) == 2:
        x2d = _out_proj2(res[0][0], res[1][0], res[0][1], res[1][1], x2d)
    elif res:
        x2d = _out_proj1(res[0][0], res[0][1], x2d)
    return x2d.reshape(batch, seq, d)
```

```python
import functools

import numpy as np
import jax
import jax.numpy as jnp
from jax import lax
from jax.experimental import pallas as pl
from jax.experimental.pallas import tpu as pltpu

F32 = jnp.float32
BF16 = jnp.bfloat16

LANES = 128
SUBLANES = 8
BF16_SUBLANES = 16
EPS = 1e-6
NEG = -1e30
VMEM_LIMIT_BYTES = 56 * 1024 * 1024

M_HEADS = 4
M_DIM = 128
M_W = M_HEADS * M_DIM
M_CHUNK = 128
CONV_K = 4
N_HEADS = 8
N_GROUPS = 2
N_HPG = N_HEADS // N_GROUPS
N_DIM = 64
N_W = N_HEADS * N_DIM
N_KV_W = N_GROUPS * N_DIM
CMP_LEN = 32
CMP_STRIDE = 16
CMP_HIDDEN = 128
CMP_CHUNK = 128
SEL_BLOCK = 64
SEL_SHIFT = 6
SEL_TOPK = 16
WINDOW = 512
Q_BLOCK = 128
ROPE_THETA = 500000.0
ROPE_DIM = N_DIM // 4
ROPE_HALF = ROPE_DIM // 2
KV_TILE = 512
V_ROWS = 80
LOG2E = 1.4426950408889634
H_HEADS = 8
H_DIM = 128
H_W = H_HEADS * H_DIM
H_TILE = 128
H_LAG = 2

C_MQ, C_MK, C_MV, C_MO, C_MZ = 0, 512, 1024, 1536, 2048
C_NQ = 2560
C_KC, C_VC, C_KS, C_VS = 3072, 3200, 3328, 3456
C_KW, C_VW, C_SMALL = 3584, 3712, 3840
C_NZ = 4096
EV_COLS = 4608
SM_I, SM_F, SM_G = 0, 4, 8


def _cparams(*sem):
    return pltpu.CompilerParams(dimension_semantics=sem, vmem_limit_bytes=VMEM_LIMIT_BYTES)


def _dot(a, b):
    return jnp.dot(a, b, preferred_element_type=F32)


def _dot_nt(a, b):
    return lax.dot_general(a, b, (((1,), (1,)), ((), ())), preferred_element_type=F32)


def _dot_tn(a, b):
    return lax.dot_general(a, b, (((0,), (0,)), ((), ())), preferred_element_type=F32)


def _sigmoid(x):
    return 1.0 / (1.0 + jnp.exp(-x))


def _silu(x):
    return x * _sigmoid(x)


def _split3(x):
    hi = x.astype(BF16)
    r = x - hi.astype(F32)
    mid = r.astype(BF16)
    lo = (r - mid.astype(F32)).astype(BF16)
    return hi, mid, lo


def _dot01(m01, x):
    hi, mid, lo = _split3(x)
    return _dot(m01, hi) + _dot(m01, mid) + _dot(m01, lo)


def _norm_proj_kernel(*refs, n_res, g_tiles, g_off):
    x_ref = refs[0]
    y_refs, wy_refs = refs[1:1 + n_res], refs[1 + n_res:1 + 2 * n_res]
    nw_ref, w_ref, b_ref = refs[1 + 2 * n_res:4 + 2 * n_res]
    outs = refs[4 + 2 * n_res:]
    xo_ref = outs[0] if n_res else None
    o_ref, g_ref, h_ref = outs[-3:]
    j = pl.program_id(1)

    @pl.when(j == 0)
    def _():
        x = x_ref[...]
        for y_ref, wy_ref in zip(y_refs, wy_refs):
            x = x + _dot(y_ref[...], wy_ref[...])
        if n_res:
            xo_ref[...] = x
        ms = jnp.mean(x * x, axis=-1, keepdims=True)
        h_ref[...] = (x * lax.rsqrt(ms + EPS) * nw_ref[...]).astype(BF16)

    acc = _dot(h_ref[...], w_ref[...]) + b_ref[...]
    o_ref[...] = acc.astype(o_ref.dtype)

    @pl.when((j >= g_tiles[0]) & (j < g_tiles[1]))
    def _():
        g_ref[...] = acc[:, g_off:g_off + g_ref.shape[1]]


def _norm_proj(x2d, norm_w, w_bf16, bias, g_cols, tn, res=(), tm=1024):
    t, d = x2d.shape
    n = w_bf16.shape[1]
    g_w = min(g_cols[1] - g_cols[0], tn)
    g_tiles = (g_cols[0] // tn, -(-g_cols[1] // tn))
    row = lambda w: pl.BlockSpec((tm, w), lambda i, j: (i, 0))
    const = lambda a: pl.BlockSpec(a.shape, lambda i, j: (0, 0))
    ys, wys = [y for y, _ in res], [wy for _, wy in res]
    out_specs = [pl.BlockSpec((tm, tn), lambda i, j: (i, j)),
                 pl.BlockSpec((tm, g_w), lambda i, j: (i, jnp.clip(j - g_tiles[0], 0, g_tiles[1] - g_tiles[0] - 1)))]
    out_shape = [jax.ShapeDtypeStruct((t, n), BF16), jax.ShapeDtypeStruct((t, g_cols[1] - g_cols[0]), F32)]
    if res:
        out_specs = [row(d)] + out_specs
        out_shape = [jax.ShapeDtypeStruct((t, d), F32)] + out_shape
    return pl.pallas_call(
        functools.partial(_norm_proj_kernel, n_res=len(res), g_tiles=g_tiles, g_off=g_cols[0] % tn),
        grid=(t // tm, n // tn),
        in_specs=([row(d)] + [row(y.shape[1]) for y in ys] + [const(wy) for wy in wys]
                  + [pl.BlockSpec((1, d), lambda i, j: (0, 0)),
                     pl.BlockSpec((d, tn), lambda i, j: (0, j)),
                     pl.BlockSpec((1, tn), lambda i, j: (0, j))]),
        out_specs=out_specs,
        out_shape=out_shape,
        scratch_shapes=[pltpu.VMEM((tm, d), BF16)],
        compiler_params=_cparams("parallel", "arbitrary"),
        name="norm_proj",
    )(x2d, *ys, *wys, norm_w.reshape(1, d), w_bf16, bias.reshape(1, n))


def _out_proj2_kernel(ya_ref, yb_ref, wa_ref, wb_ref, x_ref, o_ref):
    o_ref[...] = x_ref[...] + _dot(ya_ref[...], wa_ref[...]) + _dot(yb_ref[...], wb_ref[...])


def _out_proj2(ya, yb, wa, wb, x2d, tm=1024):
    t, d = x2d.shape
    ka, kb = ya.shape[1], yb.shape[1]
    return pl.pallas_call(
        _out_proj2_kernel,
        grid=(t // tm,),
        in_specs=[
            pl.BlockSpec((tm, ka), lambda i: (i, 0)),
            pl.BlockSpec((tm, kb), lambda i: (i, 0)),
            pl.BlockSpec((ka, d), lambda i: (0, 0)),
            pl.BlockSpec((kb, d), lambda i: (0, 0)),
            pl.BlockSpec((tm, d), lambda i: (i, 0)),
        ],
        out_specs=pl.BlockSpec((tm, d), lambda i: (i, 0)),
        out_shape=jax.ShapeDtypeStruct((t, d), F32),
        compiler_params=_cparams("parallel"),
        name="out_proj2",
    )(ya, yb, wa, wb, x2d)


def _out_proj1_kernel(y_ref, w_ref, x_ref, o_ref):
    o_ref[...] = x_ref[...] + _dot(y_ref[...], w_ref[...])


def _out_proj1(y, w, x2d, tm=1024):
    t, d = x2d.shape
    k = y.shape[1]
    return pl.pallas_call(
        _out_proj1_kernel,
        grid=(t // tm,),
        in_specs=[
            pl.BlockSpec((tm, k), lambda i: (i, 0)),
            pl.BlockSpec((k, d), lambda i: (0, 0)),
            pl.BlockSpec((tm, d), lambda i: (i, 0)),
        ],
        out_specs=pl.BlockSpec((tm, d), lambda i: (i, 0)),
        out_shape=jax.ShapeDtypeStruct((t, d), F32),
        compiler_params=_cparams("parallel"),
        name="out_proj1",
    )(y, w, x2d)


def _mlstm_kernel(q_ref, k_ref, v_ref, o_ref, z_ref, sm_ref, cw_ref, cb_ref, fb_ref, hn_ref,
                  out_ref, xbuf, c_ref, n_ref, m_ref):
    L, D = M_CHUNK, M_DIM
    NB = q_ref.shape[0]
    XR = xbuf.shape[1]
    X0 = BF16_SUBLANES

    @pl.when(pl.program_id(0) == 0)
    def _():
        xbuf[...] = jnp.zeros_like(xbuf)
        c_ref[...] = jnp.zeros_like(c_ref)
        n_ref[...] = jnp.zeros_like(n_ref)
        m_ref[...] = jnp.zeros_like(m_ref)

    t_i = lax.broadcasted_iota(jnp.int32, (L, XR), 0)
    r_i = lax.broadcasted_iota(jnp.int32, (L, XR), 1)
    row = lax.broadcasted_iota(jnp.int32, (L, L), 0)
    col = lax.broadcasted_iota(jnp.int32, (L, L), 1)
    causal = col <= row
    lane = lax.broadcasted_iota(jnp.int32, (L, LANES), 1)
    is_f = (lane >= SM_F) & (lane < SM_F + M_HEADS)

    heads = []
    for b in range(NB):
        xbuf[b, X0:X0 + L, 0:M_W] = q_ref[b]
        xbuf[b, X0:X0 + L, M_W:2 * M_W] = k_ref[b]
        x2 = xbuf[b]
        acc = cb_ref[...] + cw_ref[CONV_K - 1:CONV_K, :] * x2[X0:X0 + L].astype(F32)
        for d in range(1, CONV_K):
            shift = jnp.where(r_i == t_i + X0 - d, 1.0, 0.0).astype(BF16)
            acc = acc + cw_ref[CONV_K - 1 - d:CONV_K - d, :] * _dot(shift, x2)
        qk = _silu(acc)
        xbuf[b, 0:X0, :] = xbuf[b, L:L + X0, :]

        fpre = sm_ref[b] + fb_ref[...]
        gates = jnp.where(is_f, jnp.minimum(fpre, 0.0) - jnp.log(1.0 + jnp.exp(-jnp.abs(fpre))), fpre)
        bcols = _dot01(jnp.where(causal, 1.0, 0.0).astype(BF16), gates)
        gates_t = gates.T
        bcols_t = bcols.T
        for h in range(M_HEADS):
            hs = slice(h * D, (h + 1) * D)
            q = qk[:, hs]
            k = qk[:, M_W + h * D:M_W + (h + 1) * D] * (D ** -0.5)
            i_col = gates[:, SM_I + h:SM_I + h + 1]
            i_row = gates_t[SM_I + h:SM_I + h + 1, :]
            bcum_col = bcols[:, SM_F + h:SM_F + h + 1]
            bcum_row = bcols_t[SM_F + h:SM_F + h + 1, :]
            m_prev = m_ref[b * M_HEADS + h]
            dmat = jnp.where(causal, bcum_col - bcum_row + i_row, -jnp.inf)
            inter = bcum_col + m_prev
            m_row = jnp.maximum(jnp.max(dmat, axis=1, keepdims=True), inter)
            b_last = bcum_col[L - 1:L, :]
            dl = b_last - bcum_col + i_col
            m_new = jnp.maximum(b_last + m_prev, jnp.max(dl, axis=0, keepdims=True))
            wkk = jnp.exp(dl - m_new) * k
            heads.append(dict(
                b=b, hs=hs, st=b * M_HEADS + h, q=q, qb=q.astype(BF16), kb=k.astype(BF16),
                vb=v_ref[b, :, hs].astype(BF16), wkk=wkk,
                e=jnp.exp(dmat - m_row), w_inter=jnp.exp(inter - m_row), floor=jnp.exp(-m_row),
                decay=jnp.exp(b_last + m_prev - m_new), m_new=m_new))
    s_qk = [_dot_nt(t["qb"], t["kb"]) for t in heads]
    s_qc = [_dot(t["qb"], c_ref[t["st"]].astype(BF16)) for t in heads]
    s_kv = [_dot_tn(t["wkk"].astype(BF16), t["vb"]) for t in heads]
    ones_b = jnp.ones((L, LANES), BF16)
    smats = [(s_qk[j] * t["e"]).astype(BF16) for j, t in enumerate(heads)]
    s_sv = [_dot(smats[j], t["vb"]) for j, t in enumerate(heads)]
    s_ds = [_dot(smats[j], ones_b) for j in range(len(heads))]
    s_qn = [_dot((t["q"] * n_ref[t["st"]]).astype(BF16), ones_b) for t in heads]
    hhs = []
    for j, t in enumerate(heads):
        st = t["st"]
        num = s_sv[j] + t["w_inter"] * s_qc[j]
        den = s_ds[j] + t["w_inter"] * s_qn[j]
        hhs.append(num / jnp.maximum(jnp.abs(den), t["floor"]))
        c_ref[st] = t["decay"] * c_ref[st] + s_kv[j]
        n_ref[st] = t["decay"] * n_ref[st] + jnp.sum(t["wkk"], axis=0, keepdims=True)
        m_ref[st] = t["m_new"]
    s_ms = [_dot((hh * hh).astype(BF16), ones_b) for hh in hhs]
    for j, t in enumerate(heads):
        b, hs = t["b"], t["hs"]
        hn = hhs[j] * lax.rsqrt(s_ms[j] * (1.0 / D) + EPS) * hn_ref[:, hs]
        gate = _sigmoid(o_ref[b, :, hs].astype(F32)) * _silu(z_ref[b, :, hs].astype(F32))
        out_ref[b, :, hs] = (hn * gate).astype(out_ref.dtype)


def _mlstm(proj, small, conv_w, conv_b, f_bias_row, head_norm, batch, seq):
    L = M_CHUNK
    nc = seq // L
    p3 = proj.reshape(batch, seq, proj.shape[1])
    s3 = small.reshape(batch, seq, LANES)
    cb = lambda off: pl.BlockSpec((batch, L, M_W), lambda c, off=off: (0, c, off // M_W))
    full = lambda r, w: pl.BlockSpec((r, w), lambda c: (0, 0))
    out = pl.pallas_call(
        _mlstm_kernel,
        grid=(nc,),
        in_specs=[
            cb(C_MQ), cb(C_MK), cb(C_MV), cb(C_MO), cb(C_MZ),
            pl.BlockSpec((batch, L, LANES), lambda c: (0, c, 0)),
            full(CONV_K, 2 * M_W), full(1, 2 * M_W), full(1, LANES), full(1, M_W),
        ],
        out_specs=pl.BlockSpec((batch, L, M_W), lambda c: (0, c, 0)),
        out_shape=jax.ShapeDtypeStruct((batch, seq, M_W), BF16),
        scratch_shapes=[
            pltpu.VMEM((batch, 2 * L, 2 * M_W), BF16),
            pltpu.VMEM((batch * M_HEADS, M_DIM, M_DIM), F32),
            pltpu.VMEM((batch * M_HEADS, 1, M_DIM), F32),
            pltpu.VMEM((batch * M_HEADS, 1, 1), F32),
        ],
        compiler_params=_cparams("arbitrary"),
        name="mlstm",
    )(p3, p3, p3, p3, p3, s3, conv_w, conv_b.reshape(1, -1), f_bias_row, head_norm.reshape(1, -1))
    return out.reshape(batch * seq, M_W)


def _group_rms(x, w_row):
    r = lax.broadcasted_iota(jnp.int32, (LANES, LANES), 0)
    c = lax.broadcasted_iota(jnp.int32, (LANES, LANES), 1)
    same_head = jnp.where((r < N_DIM) == (c < N_DIM), 1.0, 0.0).astype(BF16)
    ms = _dot_hilo(x * x, same_head) * (1.0 / N_DIM)
    return x * lax.rsqrt(ms + EPS) * w_row


def _lane_gather01(src_lane, width=LANES):
    r = lax.broadcasted_iota(jnp.int32, (LANES, width), 0)
    c = lax.broadcasted_iota(jnp.int32, (LANES, width), 1)
    return jnp.where(r == src_lane(c), 1.0, 0.0).astype(BF16)


def _dot_hilo(x, m01):
    hi = x.astype(BF16)
    lo = (x - hi.astype(F32)).astype(BF16)
    return _dot(hi, m01) + _dot(lo, m01)


def _rope(x, cs, s1, s2):
    return x * cs + pltpu.roll(x, ROPE_HALF, 1) * s1 + pltpu.roll(x, LANES - ROPE_HALF, 1) * s2


def _dup_halves(x):
    dup = _lane_gather01(lambda c: jnp.where(c < LANES, 0, N_DIM) + (c & (N_DIM - 1)), 2 * LANES)
    both = _dot(x, dup).astype(BF16)
    return both[:, 0:LANES], both[:, LANES:2 * LANES]


def _transpose01(x):
    return _dot_nt(_lane_gather01(lambda c: c), x)


def _nsa_prep_kernel(q_ref, kv_ref, kw_ref, sm_ref, cs_ref, s1_ref, s2_ref, qn_ref, kn_ref,
                     qt_ref, ka_ref, kwd_ref, vst_ref, vwt_ref, kc_ref, vc_ref, gtt_ref,
                     *, tp, seq):
    cs, s1, s2 = cs_ref[...], s1_ref[...], s2_ref[...]
    nq = N_W // LANES
    kv = kv_ref[...].astype(F32)
    kwb = kw_ref[...].astype(F32)
    kc_ref[...] = kv[:, 0:LANES]
    vc_ref[...] = kv[:, LANES:2 * LANES]
    slabs = [q_ref[:, j * LANES:(j + 1) * LANES].astype(F32) for j in range(nq)]
    slabs += [kv[:, 2 * LANES:3 * LANES], kwb[:, 0:LANES]]
    gains = [qn_ref[...]] * nq + [kn_ref[1:2, :], kn_ref[2:3, :]]
    normed = [_group_rms(x, w) for x, w in zip(slabs, gains)]
    roped = [_rope(x, cs, s1, s2) for x in normed]
    qs = [(x * (N_DIM ** -0.5 * LOG2E)).astype(BF16) for x in roped[:nq]]
    k0, k1 = _dup_halves(roped[nq].astype(BF16))
    w0, w1 = _dup_halves(roped[nq + 1].astype(BF16))
    gates = _sigmoid(sm_ref[...])
    g_hi = gates.astype(BF16)
    g_lo = (gates - g_hi.astype(F32)).astype(BF16)
    qts = [_transpose01(x) for x in qs]
    vst = _transpose01(kv_ref[:, 3 * LANES:4 * LANES])
    vwt = _transpose01(kw_ref[:, LANES:2 * LANES])
    gtt_ref[...] = _transpose01(g_hi) + _transpose01(g_lo)
    for j in range(nq):
        qt_ref[j * LANES:(j + 1) * LANES, :] = qts[j].astype(BF16)
    pos0 = (pl.program_id(0) * tp) % seq
    pos = pos0 + lax.broadcasted_iota(jnp.int32, (tp, LANES), 0)
    blk = lax.broadcasted_iota(jnp.int32, (tp, LANES), 1)
    onehot = jnp.where(jnp.right_shift(pos, SEL_SHIFT) == blk, 1.0, 0.0).astype(BF16)
    ka_ref[:, 0:LANES] = k0
    ka_ref[:, LANES:2 * LANES] = onehot
    ka_ref[:, 2 * LANES:3 * LANES] = k1
    ka_ref[:, 3 * LANES:4 * LANES] = onehot
    fill =(lax.broadcasted_iota(jnp.int32, (V_ROWS - N_DIM, Q_BLOCK), 0) == 0).astype(BF16)
    for c in range(tp // Q_BLOCK):
        for g in range(N_GROUPS):
            vst_ref[c, g * V_ROWS:g * V_ROWS + N_DIM, :] = (
                vst[g * N_DIM:(g + 1) * N_DIM, c * Q_BLOCK:(c + 1) * Q_BLOCK].astype(BF16))
            vst_ref[c, g * V_ROWS + N_DIM:(g + 1) * V_ROWS, :] = fill
    kwd_ref[:, 0:LANES] = w0
    kwd_ref[:, LANES:2 * LANES] = w1
    vwt_ref[...] = vwt.astype(BF16)


def _nsa_prep(proj, small, cs, s1, s2, qn_row, kn_rows, batch, seq):
    tp = 2 * KV_TILE
    t = batch * seq
    nsb = seq // tp
    rb = lambda w, off: pl.BlockSpec((tp, w), lambda i, off=off, w=w: (i, off // w))
    tab = pl.BlockSpec((tp, LANES), lambda i: (i % nsb, 0))
    o = lambda w: pl.BlockSpec((tp, w), lambda i: (i, 0))
    ot = lambda r: pl.BlockSpec((r, tp), lambda i: (0, i))
    sds = lambda w, dt: jax.ShapeDtypeStruct((t, w), dt)
    return pl.pallas_call(
        functools.partial(_nsa_prep_kernel, tp=tp, seq=seq),
        grid=(t // tp,),
        in_specs=[rb(512, C_NQ), rb(512, C_KC), rb(512, C_KW), o(LANES), tab, tab, tab,
                  pl.BlockSpec((1, LANES), lambda i: (0, 0)), pl.BlockSpec((3, LANES), lambda i: (0, 0))],
        out_specs=[ot(N_W), o(512), o(256),
                   pl.BlockSpec((tp // Q_BLOCK, N_GROUPS * V_ROWS, Q_BLOCK), lambda i: (i, 0, 0)),
                   ot(LANES), o(128), o(128), ot(LANES)],
        out_shape=[jax.ShapeDtypeStruct((N_W, t), BF16), sds(512, BF16), sds(256, BF16),
                   jax.ShapeDtypeStruct((t // Q_BLOCK, N_GROUPS * V_ROWS, Q_BLOCK), BF16),
                   jax.ShapeDtypeStruct((LANES, t), BF16),
                   sds(128, F32), sds(128, F32), jax.ShapeDtypeStruct((LANES, t), F32)],
        compiler_params=_cparams("parallel"),
        name="nsa_prep",
    )(proj, proj, proj, small, cs, s1, s2, qn_row, kn_rows)


def _gelu_tanh(x):
    return 0.5 * x * (1.0 + jnp.tanh(0.7978845608028654 * (x + 0.044715 * x * x * x)))


def _compress_mlp(x_ref, pa_ref, pb_ref, wa_ref, wb_ref, w2_ref, hb_ref):
    n = x_ref.shape[0] // CMP_STRIDE
    ha = hb = None
    for r in range(CMP_STRIDE):
        xr = x_ref[pl.ds(r, n, stride=CMP_STRIDE), :]
        ws = slice(r * LANES, (r + 1) * LANES)
        da = _dot((xr + pa_ref[:, ws]).astype(BF16), wa_ref[ws, :])
        db = _dot((xr + pb_ref[:, ws]).astype(BF16), wb_ref[ws, :])
        ha = da if ha is None else ha + da
        hb = db if hb is None else hb + db
    hb_ref[0:n, :] = hb
    hb_ref[n:n + SUBLANES, :] = jnp.zeros((SUBLANES, hb_ref.shape[1]), F32)
    hid = ha + hb_ref[1:n + 1, :]
    return _dot(_gelu_tanh(hid).astype(BF16), w2_ref[...])


def _nsa_compress_kernel(kx_ref, vx_ref, kpa_ref, kpb_ref, vpa_ref, vpb_ref, kwa_ref, kwb_ref, kw2_ref,
                         vwa_ref, vwb_ref, vw2_ref, cs_ref, s1_ref, s2_ref, kn_ref,
                         kcd_ref, vct_ref, hb_ref):
    n = kx_ref.shape[0] // CMP_STRIDE
    rowi = lax.broadcasted_iota(jnp.int32, (n, LANES), 0)
    live = rowi < n - 1
    kc = _compress_mlp(kx_ref, kpa_ref, kpb_ref, kwa_ref, kwb_ref, kw2_ref, hb_ref)
    kc = _rope(_group_rms(kc, kn_ref[0:1, :]), cs_ref[...], s1_ref[...], s2_ref[...])
    kc = jnp.where(live, kc, 0.0)
    c0, c1 = _dup_halves(kc.astype(BF16))
    kcd_ref[:, 0:LANES] = c0
    kcd_ref[:, LANES:2 * LANES] = c1
    vc = _compress_mlp(vx_ref, vpa_ref, vpb_ref, vwa_ref, vwb_ref, vw2_ref, hb_ref)
    vc = jnp.where(live, vc, 0.0)
    vct_ref[0] = vc.T.astype(BF16)


def _nsa_compress(kx, vx, kp, vp, cs, s1, s2, kn_rows, batch, nrow):
    full = lambda a: pl.BlockSpec(a.shape, lambda b: (0,) * a.ndim)
    xs = pl.BlockSpec((nrow * CMP_STRIDE, LANES), lambda b: (b, 0))
    args = (kx, vx, kp[0], kp[1], vp[0], vp[1], kp[2], kp[3], kp[4], vp[2], vp[3], vp[4], cs, s1, s2, kn_rows)
    return pl.pallas_call(
        _nsa_compress_kernel,
        grid=(batch,),
        in_specs=[xs, xs] + [full(a) for a in args[2:]],
        out_specs=[pl.BlockSpec((nrow, 2 * LANES), lambda b: (b, 0)),
                   pl.BlockSpec((1, LANES, nrow), lambda b: (b, 0, 0))],
        out_shape=[jax.ShapeDtypeStruct((batch * nrow, 2 * LANES), BF16),
                   jax.ShapeDtypeStruct((batch, LANES, nrow), BF16)],
        scratch_shapes=[pltpu.VMEM((nrow + SUBLANES, 2 * CMP_HIDDEN), F32)],
        compiler_params=_cparams("parallel"),
        name="nsa_compress",
    )(*args)


def _compress_params(pos, w1, w2):
    def expand(w1_half, pos_half):
        w = w1_half.reshape(CMP_STRIDE, N_DIM, CMP_HIDDEN)
        z = jnp.zeros_like(w)
        g0 = jnp.concatenate([w, z], axis=-1)
        g1 = jnp.concatenate([z, w], axis=-1)
        wx = jnp.stack([g0, g1], axis=1).reshape(CMP_STRIDE * LANES, 2 * CMP_HIDDEN)
        px = jnp.concatenate([pos_half, pos_half], axis=-1).reshape(1, CMP_STRIDE * LANES)
        return wx.astype(BF16), px.astype(F32)

    half = CMP_STRIDE * N_DIM
    wa, pa = expand(w1[:half], pos[:CMP_STRIDE])
    wb, pb = expand(w1[half:], pos[CMP_STRIDE:])
    z2 = jnp.zeros_like(w2)
    w2x = jnp.concatenate([jnp.concatenate([w2, z2], axis=1), jnp.concatenate([z2, w2], axis=1)], axis=0)
    return pa, pb, wa, wb, w2x.astype(BF16)


def _nsa_attn_kernel(qt_ref, gtt_ref, z_ref, ka_ref, vst_ref,
                     kw0, kw1, kw2, kw3, kw4, wt0, wt1, wt2, wt3, wt4,
                     kcd_ref, vct_ref, ovt_ref, out_ref, qa_ref, acc_ref, *, ncmp):
    QB = Q_BLOCK
    NP = N_HEADS // 2
    PW = 2 * QB
    i = pl.program_id(1)
    t0 = i * QB
    kw_refs = (kw0, kw1, kw2, kw3, kw4)
    vwt_refs = (wt0, wt1, wt2, wt3, wt4)
    nwt = len(kw_refs)

    sub = lax.broadcasted_iota(jnp.int32, (QB, QB), 0)
    lane = lax.broadcasted_iota(jnp.int32, (QB, QB), 1)
    top = sub < N_DIM
    t_row = t0 + lax.broadcasted_iota(jnp.int32, (1, QB), 1)
    t_row2 = jnp.concatenate([t_row, t_row], axis=1)

    def pair_q(pr):
        slab = qt_ref[pr * LANES:(pr + 1) * LANES, :]
        zero = jnp.zeros_like(slab)
        return jnp.concatenate([jnp.where(top, slab, zero), jnp.where(top, zero, slab)], axis=1)

    blk_f = sub.astype(F32)
    cur = jnp.right_shift(t_row, SEL_SHIFT)
    grp = lambda pr: pr // (N_HPG // 2)
    for pr in range(NP):
        qa_ref[pr, 0:LANES, :] = pair_q(pr)

    def compressed(n):
        nidx = lax.broadcasted_iota(jnp.int32, (n, PW), 0)
        ok = (nidx * CMP_STRIDE + (CMP_LEN - 1) <= t_row2) & (nidx < ncmp - 1)
        any_ok = t_row2 >= CMP_LEN - 1
        s_cmp = [_dot(kcd_ref[0:n, grp(pr) * LANES:(grp(pr) + 1) * LANES], qa_ref[pr, 0:LANES, :])
                 for pr in range(NP)]
        p_cmp = []
        for pr in range(NP):
            s = jnp.where(ok, s_cmp[pr], NEG)
            mx = jnp.max(s, axis=0, keepdims=True)
            p = jnp.exp2(s - mx)
            den = jnp.sum(p, axis=0, keepdims=True)
            p_cmp.append(p * jnp.where(any_ok, 1.0 / den, 0.0))
        outs = [_dot(vct_ref[0, grp(pr) * N_DIM:(grp(pr) + 1) * N_DIM, 0:n], p_cmp[pr].astype(BF16))
                for pr in range(NP)]
        for g in range(N_GROUPS):
            psum = jnp.zeros((n, QB), F32)
            for pp in range(N_HPG // 2):
                p = p_cmp[g * (N_HPG // 2) + pp]
                psum = psum + p[:, 0:QB] + p[:, QB:PW]
            p_hi = psum.astype(BF16)
            p_lo = (psum - p_hi.astype(F32)).astype(BF16)
            outs.append(_dot(ovt_ref[:, 0:n], p_hi) + _dot(ovt_ref[:, 0:n], p_lo))
        return tuple(outs)

    n_opts = list(range(CMP_CHUNK, ncmp + 1, CMP_CHUNK))
    need = (t0 + QB - CMP_LEN) // CMP_STRIDE + 1
    outs = lax.switch(jnp.clip((need - 1) // CMP_CHUNK, 0, len(n_opts) - 1),
                      [functools.partial(compressed, n) for n in n_opts])
    o_cmp, imps = outs[:NP], outs[NP:]
    forced = (sub == 0) | (sub == cur) | (sub == cur - 1)
    n_rounds = SEL_TOPK - 3
    valid = sub <= cur
    works0 = [jnp.where(valid, jnp.where(forced, -jnp.inf, imp), NEG) for imp in imps]

    def rounds(works, exact):
        sels = [forced for _ in range(N_GROUPS)]
        for _ in range(n_rounds):
            for g in range(N_GROUPS):
                mx = jnp.max(works[g], axis=0, keepdims=True)
                if exact:
                    idx = jnp.min(jnp.where(works[g] == mx, blk_f, float(LANES)), axis=0, keepdims=True)
                    pick = blk_f == idx
                else:
                    pick = works[g] == mx
                sels[g] = sels[g] | pick
                works[g] = jnp.where(pick, -jnp.inf, works[g])
        return tuple(jnp.where(s & valid, 0.0, NEG) for s in sels)

    bias = rounds(list(works0), exact=False)
    picked = [jnp.sum(jnp.where((b == 0.0) & jnp.logical_not(forced), 1.0, 0.0), axis=0, keepdims=True)
              for b in bias]
    tied = jnp.max(functools.reduce(jnp.maximum, picked)) > n_rounds
    bias = lax.cond(tied, lambda: rounds(list(works0), exact=True), lambda: bias)
    for g in range(N_GROUPS):
        neg = bias[g].astype(BF16)
        neg2 = jnp.concatenate([neg, neg], axis=1)
        for pp in range(N_HPG // 2):
            qa_ref[g * (N_HPG // 2) + pp, LANES:2 * LANES, :] = neg2

    CH = QB
    above = jnp.concatenate([sub > lane, sub > lane], axis=1)

    def sel_scores(cj):
        k0 = pl.multiple_of(cj * CH, CH)
        return [_dot(ka_ref[pl.ds(k0, CH), 2 * grp(pr) * LANES:2 * (grp(pr) + 1) * LANES], qa_ref[pr])
                for pr in range(NP)]

    def sel_softmax(ss, ms, masked):
        new, alphas, ps = [], [], []
        for pr in range(NP):
            s = jnp.where(above, NEG, ss[pr]) if masked else ss[pr]
            m_new = jnp.maximum(ms[pr], jnp.max(s, axis=0, keepdims=True))
            alphas.append(jnp.exp2(ms[pr] - m_new))
            ps.append(jnp.exp2(s - m_new).astype(BF16))
            new.append(m_new)
        return tuple(new), alphas, ps

    def sel_values(cj, alphas, ps):
        for pr in range(NP):
            g = grp(pr)
            acc_ref[pr] = alphas[pr] * acc_ref[pr] + _dot(vst_ref[cj, g * V_ROWS:(g + 1) * V_ROWS, :], ps[pr])

    def sel_chunk(cj, ms, masked):
        ms, alphas, ps = sel_softmax(sel_scores(cj), ms, masked)
        sel_values(cj, alphas, ps)
        return ms

    def sel_tile(jt, ms, nc):
        c0 = jt * nc
        k0 = pl.multiple_of(c0 * CH, CH)
        big = [_dot(ka_ref[pl.ds(k0, nc * CH), 2 * grp(pr) * LANES:2 * (grp(pr) + 1) * LANES], qa_ref[pr])
               for pr in range(NP)]
        for c in range(nc):
            ss = [b[c * CH:(c + 1) * CH] for b in big]
            ms, alphas, ps = sel_softmax(ss, ms, False)
            sel_values(c0 + c, alphas, ps)
        return ms

    acc_ref[...] = jnp.zeros_like(acc_ref)
    ms = tuple(jnp.full((1, PW), 2.0 * NEG, F32) for _ in range(NP))
    done = 0
    for nc in (4 * KV_TILE // CH, 2 * KV_TILE // CH, KV_TILE // CH):
        n_run = (i - done) // nc
        ms = lax.fori_loop(done // nc, done // nc + n_run, functools.partial(sel_tile, nc=nc), ms)
        done = done + n_run * nc
    ms = lax.fori_loop(done, i, lambda cj, ms: sel_chunk(cj, ms, False), ms)
    s_diag = sel_scores(i)
    s_win = [[_dot(kw_refs[c][:, grp(pr) * LANES:(grp(pr) + 1) * LANES], qa_ref[pr, 0:LANES, :])
              for c in range(nwt)] for pr in range(NP)]
    ms, alphas, ps_diag = sel_softmax(s_diag, ms, True)
    p_win, d_win = [], []
    for pr in range(NP):
        ss = []
        for c in range(nwt):
            ok = (i - (nwt - 1) + c) >= 0
            if c == 0:
                ok = ok & above
            elif c == nwt - 1:
                ok = ok & jnp.logical_not(above)
            ss.append(jnp.where(ok, s_win[pr][c], NEG))
        mx = ss[0].max(axis=0, keepdims=True)
        for c in range(1, nwt):
            mx = jnp.maximum(mx, ss[c].max(axis=0, keepdims=True))
        ps = [jnp.exp2(ss[c] - mx) for c in range(nwt)]
        den = ps[0].sum(axis=0, keepdims=True)
        for c in range(1, nwt):
            den = den + ps[c].sum(axis=0, keepdims=True)
        p_win.append([p.astype(BF16) for p in ps])
        d_win.append(den)
    sel_values(i, alphas, ps_diag)
    o_win = []
    for pr in range(NP):
        g = grp(pr)
        acc = _dot(vwt_refs[0][g * N_DIM:(g + 1) * N_DIM, :], p_win[pr][0])
        for c in range(1, nwt):
            acc = acc + _dot(vwt_refs[c][g * N_DIM:(g + 1) * N_DIM, :], p_win[pr][c])
        o_win.append(acc * (1.0 / d_win[pr]))
    o_sel = [acc_ref[pr, 0:N_DIM, :] * (1.0 / acc_ref[pr, N_DIM:N_DIM + 1, :]) for pr in range(NP)]

    for pr in range(NP):
        halves = []
        for par in range(2):
            h = 2 * pr + par
            cs = slice(par * QB, (par + 1) * QB)
            gc = gtt_ref[SM_G + 3 * h:SM_G + 3 * h + 1, :]
            gs = gtt_ref[SM_G + 3 * h + 1:SM_G + 3 * h + 2, :]
            gw = gtt_ref[SM_G + 3 * h + 2:SM_G + 3 * h + 3, :]
            halves.append(gc * o_cmp[pr][:, cs] + gs * o_sel[pr][:, cs] + gw * o_win[pr][:, cs])
        y = jnp.concatenate(halves, axis=0).T
        sl = slice(pr * LANES, (pr + 1) * LANES)
        out_ref[:, sl] = (y * _silu(z_ref[:, sl].astype(F32))).astype(out_ref.dtype)


def _nsa_attn(proj, qt, gtt, ka, vst, kwd, vwt, kcd, vct, ovt, batch, seq):
    QB = Q_BLOCK
    nqb = seq // QB
    ncmp = seq // CMP_STRIDE
    nwt = WINDOW // QB + 1
    row = lambda w: pl.BlockSpec((QB, w), lambda b, i: (b * nqb + i, 0))
    colb = lambda r: pl.BlockSpec((r, QB), lambda b, i: (0, b * nqb + i))
    wblk = lambda b, i, c: b * nqb + jnp.maximum(i - (nwt - 1) + c, 0)
    in_specs = ([colb(N_W), colb(LANES), pl.BlockSpec((QB, N_W), lambda b, i: (b * nqb + i, C_NZ // N_W)),
                 pl.BlockSpec((seq, 4 * LANES), lambda b, i: (b, 0)),
                 pl.BlockSpec((nqb, N_GROUPS * V_ROWS, QB), lambda b, i: (b, 0, 0))]
                + [pl.BlockSpec((QB, 2 * LANES), lambda b, i, c=c: (wblk(b, i, c), 0)) for c in range(nwt)]
                + [pl.BlockSpec((LANES, QB), lambda b, i, c=c: (0, wblk(b, i, c))) for c in range(nwt)]
                + [pl.BlockSpec((ncmp, 2 * LANES), lambda b, i: (b, 0)),
                   pl.BlockSpec((1, LANES, ncmp), lambda b, i: (b, 0, 0)),
                   pl.BlockSpec((LANES, ncmp), lambda b, i: (0, 0))])
    return pl.pallas_call(
        functools.partial(_nsa_attn_kernel, ncmp=ncmp),
        grid=(batch, nqb),
        in_specs=in_specs,
        out_specs=row(N_W),
        out_shape=jax.ShapeDtypeStruct((batch * seq, N_W), BF16),
        scratch_shapes=[pltpu.VMEM((N_HEADS // 2, 2 * LANES, 2 * QB), BF16),
                        pltpu.VMEM((N_HEADS // 2, V_ROWS, 2 * QB), F32)],
        compiler_params=_cparams("parallel", "arbitrary"),
        name="nsa_attn",
    )(qt, gtt, proj, ka, vst, *([kwd] * nwt), *([vwt] * nwt), kcd, vct, ovt)


def _hgrn_kernel(f_ref, i_ref, q_ref, z_ref, lb_ref, hn_ref, out_ref, st_ref):
    T, D = H_TILE, H_DIM
    W = H_W

    @pl.when(pl.program_id(1) == 0)
    def _():
        st_ref[...] = jnp.zeros_like(st_ref)

    lb = lb_ref[...]
    fp = f_ref[...]
    e = jnp.exp(-jnp.abs(fp))
    r = 1.0 / (1.0 + e)
    sig = jnp.where(fp >= 0, r, e * r)
    nsig = jnp.where(fp >= 0, e * r, r)
    logf2 = jnp.log(lb + (1.0 - lb) * sig) * LOG2E
    kk = (1.0 - lb) * nsig

    row = lax.broadcasted_iota(jnp.int32, (T, T), 0)
    col = lax.broadcasted_iota(jnp.int32, (T, T), 1)
    tri01 = jnp.where(col <= row, 1.0, 0.0).astype(BF16)
    bcum = _dot01(tri01, logf2)

    segs = []
    sg = 2 * H_LAG
    while sg <= T:
        if sg == T:
            bmid = jnp.broadcast_to(bcum[sg // 2 - 1:sg // 2, :], (T, W))
        else:
            b3 = bcum.reshape(T // sg, sg, W)
            bmid = jnp.broadcast_to(b3[:, sg // 2 - 1:sg // 2, :], (T // sg, sg, W)).reshape(T, W)
        x = bcum - bmid
        shift = sg.bit_length() - 1
        pair_ok = ((jnp.right_shift(row, shift) == jnp.right_shift(col, shift))
                   & (jnp.bitwise_and(row, sg - 1) >= sg // 2) & (jnp.bitwise_and(col, sg - 1) < sg // 2))
        xb = x.astype(BF16)
        segs.append((jnp.exp2(jnp.minimum(xb, -xb)), pair_ok))
        sg *= 2

    sub3 = lax.broadcasted_iota(jnp.int32, (T // SUBLANES, SUBLANES, 1), 1)
    lag_ok = [jnp.bitwise_and(sub3, H_LAG - 1) >= d for d in range(H_LAG)]
    b_end = bcum[T - 1:T, :]
    v_all = i_ref[...].astype(F32)
    def head_scores(h):
        hs = slice(h * D, (h + 1) * D)
        q = _silu(q_ref[:, hs].astype(F32))
        k = kk[:, hs]
        b = bcum[:, hs]
        v = v_all[:, hs]
        vb = v.astype(BF16)
        a_seg = []
        qb, kb = q.astype(BF16), k.astype(BF16)
        for fac, _ in segs:
            f = fac[:, hs]
            a_seg.append(_dot_nt(qb * f, kb * f))
        st = st_ref[h]
        o_state = _dot_nt(qb * jnp.exp2(b.astype(BF16)), st.astype(BF16))
        kt = kb * jnp.exp2((b_end[:, hs] - b).astype(BF16))
        st_ref[h] = st * jnp.exp2(b_end[:, hs]) + _dot_tn(vb, kt)
        q3, k3, b3, v3 = (a.reshape(T // SUBLANES, SUBLANES, D) for a in (q, k, b, v))
        o3 = jnp.sum(q3 * k3, axis=2, keepdims=True) * v3
        for d in range(1, H_LAG):
            ks, bs, vs = (pltpu.roll(a, d, 1) for a in (k3, b3, v3))
            a = jnp.sum(q3 * ks * jnp.exp2(b3 - bs), axis=2, keepdims=True)
            o3 = o3 + jnp.where(lag_ok[d], a, 0.0) * vs
        return hs, a_seg, o3.reshape(T, D) + o_state, vb

    def head_finish(hs, a_seg, o_part, vb):
        a_tot = jnp.zeros((T, T), F32)
        for (_, pair_ok), a in zip(segs, a_seg):
            a_tot = jnp.where(pair_ok, a, a_tot)
        oh = o_part + _dot(a_tot.astype(BF16), vb)
        on = oh * lax.rsqrt(jnp.mean(oh * oh, axis=1, keepdims=True) + EPS) * hn_ref[:, hs]
        out_ref[:, hs] = (on * _silu(z_ref[:, hs].astype(F32))).astype(out_ref.dtype)

    pending = None
    for h in range(H_HEADS):
        cur = head_scores(h)
        if pending is not None:
            head_finish(*pending)
        pending = cur
    head_finish(*pending)


def _hgrn(proj, fgate, lb_row, hn_row, batch, seq):
    T = H_TILE
    nt = seq // T
    cb = lambda j: pl.BlockSpec((T, H_W), lambda b, c, j=j: (b * nt + c, j))
    vec = pl.BlockSpec((1, H_W), lambda b, c: (0, 0))
    return pl.pallas_call(
        _hgrn_kernel,
        grid=(batch, nt),
        in_specs=[pl.BlockSpec((T, H_W), lambda b, c: (b * nt + c, 0)), cb(1), cb(2), cb(3), vec, vec],
        out_specs=pl.BlockSpec((T, H_W), lambda b, c: (b * nt + c, 0)),
        out_shape=jax.ShapeDtypeStruct((batch * seq, H_W), BF16),
        scratch_shapes=[pltpu.VMEM((H_HEADS, H_DIM, H_DIM), F32)],
        compiler_params=_cparams("parallel", "arbitrary"),
        name="hgrn2",
    )(fgate, proj, proj, proj, lb_row, hn_row)


def _even_relayout(a):
    nb = 5 * M_W + 2 * M_HEADS
    g0 = nb + N_W + 6 * N_KV_W
    col = lambda lo, n: lax.slice_in_dim(a, lo, lo + n, axis=-1)
    pad = jnp.zeros(a.shape[:-1] + (C_NZ - C_SMALL - SM_G - 3 * N_HEADS,), a.dtype)
    out = jnp.concatenate([
        col(0, 5 * M_W),
        col(nb, N_W + 6 * N_KV_W),
        col(5 * M_W, 2 * M_HEADS),
        col(g0, 3 * N_HEADS), pad,
        col(g0 + 3 * N_HEADS, N_W),
    ], axis=-1)
    assert out.shape[-1] == EV_COLS
    return out


def _rope_tables(positions):
    inv_freq = ROPE_THETA ** (-np.arange(ROPE_HALF, dtype=np.float64) / ROPE_HALF)
    ang = positions.astype(np.float64)[:, None] * inv_freq[None, :]
    cos, sin = np.cos(ang).astype(np.float32), np.sin(ang).astype(np.float32)
    n = positions.shape[0]
    z8 = np.zeros((n, ROPE_HALF), np.float32)
    rest1 = np.ones((n, N_DIM - ROPE_DIM), np.float32)
    rest0 = np.zeros((n, N_DIM - ROPE_DIM), np.float32)
    cs = np.concatenate([cos, cos, rest1], axis=1)
    s1 = np.concatenate([z8, sin, rest0], axis=1)
    s2 = np.concatenate([-sin, z8, rest0], axis=1)
    two = lambda a: np.concatenate([a, a], axis=1)
    return two(cs), two(s1), two(s2)


def _overlap_matrix(seq):
    ncmp = seq // CMP_STRIDE
    n_sel = seq // SEL_BLOCK
    cs = np.arange(ncmp) * CMP_STRIDE
    ss = np.arange(n_sel) * SEL_BLOCK
    ov = np.clip(np.minimum(cs[:, None] + CMP_LEN, ss[None, :] + SEL_BLOCK)
                 - np.maximum(cs[:, None], ss[None, :]), 0, None).astype(np.float32)
    ov[ncmp - 1, :] = 0.0
    return np.pad(ov, ((0, 0), (0, LANES - n_sel)))


def _even_layer(x2d, res, norm_w, w_in, b_in, f_bias, conv_w, conv_b, head_norm, q_norm, k_norm,
                ck_pos, ck_w1, ck_w2, cv_pos, cv_w1, cv_w2, w_out, batch, seq):
    assert seq // SEL_BLOCK <= LANES and seq % KV_TILE == 0
    outs = _norm_proj(x2d, norm_w, _even_relayout(w_in.astype(BF16)), _even_relayout(b_in),
                      (C_SMALL, C_SMALL + LANES), tn=EV_COLS // 3, res=res, tm=1024 if res else 2048)
    if res:
        x2d = outs[0]
    proj, small = outs[-2:]

    f_bias_row = jnp.zeros((1, LANES), F32).at[0, SM_F:SM_F + M_HEADS].set(f_bias)
    ya = _mlstm(proj, small, conv_w, conv_b, f_bias_row, head_norm, batch, seq)

    cs, s1, s2 = (jnp.asarray(a) for a in _rope_tables(np.arange(seq)))
    qn_row = jnp.concatenate([q_norm, q_norm]).reshape(1, LANES)
    kn_rows = jnp.concatenate([k_norm, k_norm], axis=1)
    qt, ka, kwd, vst, vwt, kc, vc, gtt = _nsa_prep(proj, small, cs, s1, s2, qn_row, kn_rows, batch, seq)

    ncmp = seq // CMP_STRIDE
    ccs, cs1, cs2 = (jnp.asarray(a) for a in _rope_tables(np.arange(ncmp) * CMP_STRIDE + CMP_LEN - 1))
    kcd, vct = _nsa_compress(
        kc, vc,
        _compress_params(ck_pos, ck_w1, ck_w2), _compress_params(cv_pos, cv_w1, cv_w2),
        ccs, cs1, cs2, kn_rows, batch, ncmp)

    ovt = jnp.asarray(_overlap_matrix(seq).T).astype(BF16)
    yb = _nsa_attn(proj, qt, gtt, ka, vst, kwd, vwt, kcd, vct, ovt, batch, seq)

    w_o = w_out.astype(BF16)
    return x2d, [(ya, w_o[:M_W]), (yb, w_o[M_W:])]


def _odd_layer(x2d, res, norm_w, w_in, b_in, lb, head_norm, w_out, batch, seq):
    outs = _norm_proj(x2d, norm_w, w_in.astype(BF16), b_in, (0, H_W), tn=H_W, res=res)
    if res:
        x2d = outs[0]
    proj, fgate = outs[-2:]
    y = _hgrn(proj, fgate, lb.reshape(1, H_W), head_norm.reshape(1, H_W), batch, seq)
    return x2d, [(y, w_out.astype(BF16))]


def kernel(x, norm_w, ev_w_in, ev_b_in, mlstm_f_bias, mlstm_conv_w, mlstm_conv_b, mlstm_head_norm, nsa_q_norm, nsa_k_norm, cmp_k_pos, cmp_k_w1, cmp_k_w2, cmp_v_pos, cmp_v_w1, cmp_v_w2, ev_w_out, od_w_in, od_b_in, hgrn_lb_logits, hgrn_head_norm, od_w_out):
    batch, seq, d = x.shape
    depth = norm_w.shape[0]
    lbs = jnp.cumsum(jax.nn.softmax(hgrn_lb_logits.astype(F32), axis=0), axis=0)
    x2d = x.reshape(batch * seq, d)
    res = []
    for layer in range(depth):
        j = layer // 2
        if layer % 2 == 0:
            x2d, res = _even_layer(x2d, res, norm_w[layer], ev_w_in[j], ev_b_in[j], mlstm_f_bias[j],
                                   mlstm_conv_w[j], mlstm_conv_b[j], mlstm_head_norm[j], nsa_q_norm[j],
                                   nsa_k_norm[j], cmp_k_pos[j], cmp_k_w1[j], cmp_k_w2[j], cmp_v_pos[j],
                                   cmp_v_w1[j], cmp_v_w2[j], ev_w_out[j], batch, seq)
        else:
            x2d, res = _odd_layer(x2d, res, norm_w[layer], od_w_in[j], od_b_in[j], lbs[j], hgrn_head_norm[j],
                                  od_w_out[j], batch, seq)
    if len(res) == 2:
        x2d = _out_proj2(res[0][0], res[1][0], res[0][1], res[1][1], x2d)
    elif res:
        x2d = _out_proj1(res[0][0], res[0][1], x2d)
    return x2d.reshape(batch, seq, d)
```

```python
import functools

import numpy as np
import jax
import jax.numpy as jnp
from jax import lax
from jax.experimental import pallas as pl
from jax.experimental.pallas import tpu as pltpu

F32 = jnp.float32
BF16 = jnp.bfloat16

LANES = 128
SUBLANES = 8
BF16_SUBLANES = 16
EPS = 1e-6
NEG = -1e30
VMEM_LIMIT_BYTES = 56 * 1024 * 1024

M_HEADS = 4
M_DIM = 128
M_W = M_HEADS * M_DIM
M_CHUNK = 128
CONV_K = 4
N_HEADS = 8
N_GROUPS = 2
N_HPG = N_HEADS // N_GROUPS
N_DIM = 64
N_W = N_HEADS * N_DIM
N_KV_W = N_GROUPS * N_DIM
CMP_LEN = 32
CMP_STRIDE = 16
CMP_HIDDEN = 128
CMP_CHUNK = 128
SEL_BLOCK = 64
SEL_SHIFT = 6
SEL_TOPK = 16
WINDOW = 512
Q_BLOCK = 128
ROPE_THETA = 500000.0
ROPE_DIM = N_DIM // 4
ROPE_HALF = ROPE_DIM // 2
KV_TILE = 512
V_ROWS = 80
LOG2E = 1.4426950408889634
H_HEADS = 8
H_DIM = 128
H_W = H_HEADS * H_DIM
H_TILE = 128
H_LAG = 2

C_MQ, C_MK, C_MV, C_MO, C_MZ = 0, 512, 1024, 1536, 2048
C_NQ = 2560
C_KC, C_VC, C_KS, C_VS = 3072, 3200, 3328, 3456
C_KW, C_VW, C_SMALL = 3584, 3712, 3840
C_NZ = 4096
EV_COLS = 4608
SM_I, SM_F, SM_G = 0, 4, 8


def _cparams(*sem):
    return pltpu.CompilerParams(dimension_semantics=sem, vmem_limit_bytes=VMEM_LIMIT_BYTES)


def _dot(a, b):
    return jnp.dot(a, b, preferred_element_type=F32)


def _dot_nt(a, b):
    return lax.dot_general(a, b, (((1,), (1,)), ((), ())), preferred_element_type=F32)


def _dot_tn(a, b):
    return lax.dot_general(a, b, (((0,), (0,)), ((), ())), preferred_element_type=F32)


def _sigmoid(x):
    return 1.0 / (1.0 + jnp.exp(-x))


def _silu(x):
    return x * _sigmoid(x)


def _split3(x):
    hi = x.astype(BF16)
    r = x - hi.astype(F32)
    mid = r.astype(BF16)
    lo = (r - mid.astype(F32)).astype(BF16)
    return hi, mid, lo


def _dot01(m01, x):
    hi, mid, lo = _split3(x)
    return _dot(m01, hi) + _dot(m01, mid) + _dot(m01, lo)


def _norm_proj_kernel(*refs, n_res, g_tiles, g_off):
    x_ref = refs[0]
    y_refs, wy_refs = refs[1:1 + n_res], refs[1 + n_res:1 + 2 * n_res]
    nw_ref, w_ref, b_ref = refs[1 + 2 * n_res:4 + 2 * n_res]
    outs = refs[4 + 2 * n_res:]
    xo_ref = outs[0] if n_res else None
    o_ref, g_ref, h_ref = outs[-3:]
    j = pl.program_id(1)

    @pl.when(j == 0)
    def _():
        x = x_ref[...]
        for y_ref, wy_ref in zip(y_refs, wy_refs):
            x = x + _dot(y_ref[...], wy_ref[...])
        if n_res:
            xo_ref[...] = x
        ms = jnp.mean(x * x, axis=-1, keepdims=True)
        h_ref[...] = (x * lax.rsqrt(ms + EPS) * nw_ref[...]).astype(BF16)

    acc = _dot(h_ref[...], w_ref[...]) + b_ref[...]
    o_ref[...] = acc.astype(o_ref.dtype)

    @pl.when((j >= g_tiles[0]) & (j < g_tiles[1]))
    def _():
        g_ref[...] = acc[:, g_off:g_off + g_ref.shape[1]]


def _norm_proj(x2d, norm_w, w_bf16, bias, g_cols, tn, res=(), tm=1024):
    t, d = x2d.shape
    n = w_bf16.shape[1]
    g_w = min(g_cols[1] - g_cols[0], tn)
    g_tiles = (g_cols[0] // tn, -(-g_cols[1] // tn))
    row = lambda w: pl.BlockSpec((tm, w), lambda i, j: (i, 0))
    const = lambda a: pl.BlockSpec(a.shape, lambda i, j: (0, 0))
    ys, wys = [y for y, _ in res], [wy for _, wy in res]
    out_specs = [pl.BlockSpec((tm, tn), lambda i, j: (i, j)),
                 pl.BlockSpec((tm, g_w), lambda i, j: (i, jnp.clip(j - g_tiles[0], 0, g_tiles[1] - g_tiles[0] - 1)))]
    out_shape = [jax.ShapeDtypeStruct((t, n), BF16), jax.ShapeDtypeStruct((t, g_cols[1] - g_cols[0]), F32)]
    if res:
        out_specs = [row(d)] + out_specs
        out_shape = [jax.ShapeDtypeStruct((t, d), F32)] + out_shape
    return pl.pallas_call(
        functools.partial(_norm_proj_kernel, n_res=len(res), g_tiles=g_tiles, g_off=g_cols[0] % tn),
        grid=(t // tm, n // tn),
        in_specs=([row(d)] + [row(y.shape[1]) for y in ys] + [const(wy) for wy in wys]
                  + [pl.BlockSpec((1, d), lambda i, j: (0, 0)),
                     pl.BlockSpec((d, tn), lambda i, j: (0, j)),
                     pl.BlockSpec((1, tn), lambda i, j: (0, j))]),
        out_specs=out_specs,
        out_shape=out_shape,
        scratch_shapes=[pltpu.VMEM((tm, d), BF16)],
        compiler_params=_cparams("parallel", "arbitrary"),
        name="norm_proj",
    )(x2d, *ys, *wys, norm_w.reshape(1, d), w_bf16, bias.reshape(1, n))


def _out_proj2_kernel(ya_ref, yb_ref, wa_ref, wb_ref, x_ref, o_ref):
    o_ref[...] = x_ref[...] + _dot(ya_ref[...], wa_ref[...]) + _dot(yb_ref[...], wb_ref[...])


def _out_proj2(ya, yb, wa, wb, x2d, tm=1024):
    t, d = x2d.shape
    ka, kb = ya.shape[1], yb.shape[1]
    return pl.pallas_call(
        _out_proj2_kernel,
        grid=(t // tm,),
        in_specs=[
            pl.BlockSpec((tm, ka), lambda i: (i, 0)),
            pl.BlockSpec((tm, kb), lambda i: (i, 0)),
            pl.BlockSpec((ka, d), lambda i: (0, 0)),
            pl.BlockSpec((kb, d), lambda i: (0, 0)),
            pl.BlockSpec((tm, d), lambda i: (i, 0)),
        ],
        out_specs=pl.BlockSpec((tm, d), lambda i: (i, 0)),
        out_shape=jax.ShapeDtypeStruct((t, d), F32),
        compiler_params=_cparams("parallel"),
        name="out_proj2",
    )(ya, yb, wa, wb, x2d)


def _out_proj1_kernel(y_ref, w_ref, x_ref, o_ref):
    o_ref[...] = x_ref[...] + _dot(y_ref[...], w_ref[...])


def _out_proj1(y, w, x2d, tm=1024):
    t, d = x2d.shape
    k = y.shape[1]
    return pl.pallas_call(
        _out_proj1_kernel,
        grid=(t // tm,),
        in_specs=[
            pl.BlockSpec((tm, k), lambda i: (i, 0)),
            pl.BlockSpec((k, d), lambda i: (0, 0)),
            pl.BlockSpec((tm, d), lambda i: (i, 0)),
        ],
        out_specs=pl.BlockSpec((tm, d), lambda i: (i, 0)),
        out_shape=jax.ShapeDtypeStruct((t, d), F32),
        compiler_params=_cparams("parallel"),
        name="out_proj1",
    )(y, w, x2d)


def _mlstm_kernel(q_ref, k_ref, v_ref, o_ref, z_ref, sm_ref, cw_ref, cb_ref, fb_ref, hn_ref,
                  out_ref, xbuf, c_ref, n_ref, m_ref):
    L, D = M_CHUNK, M_DIM
    NB = q_ref.shape[0]
    XR = xbuf.shape[1]
    X0 = BF16_SUBLANES

    @pl.when(pl.program_id(0) == 0)
    def _():
        xbuf[...] = jnp.zeros_like(xbuf)
        c_ref[...] = jnp.zeros_like(c_ref)
        n_ref[...] = jnp.zeros_like(n_ref)
        m_ref[...] = jnp.zeros_like(m_ref)

    t_i = lax.broadcasted_iota(jnp.int32, (L, XR), 0)
    r_i = lax.broadcasted_iota(jnp.int32, (L, XR), 1)
    row = lax.broadcasted_iota(jnp.int32, (L, L), 0)
    col = lax.broadcasted_iota(jnp.int32, (L, L), 1)
    causal = col <= row
    lane = lax.broadcasted_iota(jnp.int32, (L, LANES), 1)
    is_f = (lane >= SM_F) & (lane < SM_F + M_HEADS)

    heads = []
    for b in range(NB):
        xbuf[b, X0:X0 + L, 0:M_W] = q_ref[b]
        xbuf[b, X0:X0 + L, M_W:2 * M_W] = k_ref[b]
        x2 = xbuf[b]
        acc = cb_ref[...] + cw_ref[CONV_K - 1:CONV_K, :] * x2[X0:X0 + L].astype(F32)
        for d in range(1, CONV_K):
            shift = jnp.where(r_i == t_i + X0 - d, 1.0, 0.0).astype(BF16)
            acc = acc + cw_ref[CONV_K - 1 - d:CONV_K - d, :] * _dot(shift, x2)
        qk = _silu(acc)
        xbuf[b, 0:X0, :] = xbuf[b, L:L + X0, :]

        fpre = sm_ref[b] + fb_ref[...]
        gates = jnp.where(is_f, jnp.minimum(fpre, 0.0) - jnp.log(1.0 + jnp.exp(-jnp.abs(fpre))), fpre)
        bcols = _dot01(jnp.where(causal, 1.0, 0.0).astype(BF16), gates)
        gates_t = gates.T
        bcols_t = bcols.T
        for h in range(M_HEADS):
            hs = slice(h * D, (h + 1) * D)
            q = qk[:, hs]
            k = qk[:, M_W + h * D:M_W + (h + 1) * D] * (D ** -0.5)
            i_col = gates[:, SM_I + h:SM_I + h + 1]
            i_row = gates_t[SM_I + h:SM_I + h + 1, :]
            bcum_col = bcols[:, SM_F + h:SM_F + h + 1]
            bcum_row = bcols_t[SM_F + h:SM_F + h + 1, :]
            m_prev = m_ref[b * M_HEADS + h]
            dmat = jnp.where(causal, bcum_col - bcum_row + i_row, -jnp.inf)
            inter = bcum_col + m_prev
            m_row = jnp.maximum(jnp.max(dmat, axis=1, keepdims=True), inter)
            b_last = bcum_col[L - 1:L, :]
            dl = b_last - bcum_col + i_col
            m_new = jnp.maximum(b_last + m_prev, jnp.max(dl, axis=0, keepdims=True))
            wkk = jnp.exp(dl - m_new) * k
            heads.append(dict(
                b=b, hs=hs, st=b * M_HEADS + h, q=q, qb=q.astype(BF16), kb=k.astype(BF16),
                vb=v_ref[b, :, hs].astype(BF16), wkk=wkk,
                e=jnp.exp(dmat - m_row), w_inter=jnp.exp(inter - m_row), floor=jnp.exp(-m_row),
                decay=jnp.exp(b_last + m_prev - m_new), m_new=m_new))
    s_qk = [_dot_nt(t["qb"], t["kb"]) for t in heads]
    s_qc = [_dot(t["qb"], c_ref[t["st"]].astype(BF16)) for t in heads]
    s_kv = [_dot_tn(t["wkk"].astype(BF16), t["vb"]) for t in heads]
    ones_b = jnp.ones((L, LANES), BF16)
    smats = [(s_qk[j] * t["e"]).astype(BF16) for j, t in enumerate(heads)]
    s_sv = [_dot(smats[j], t["vb"]) for j, t in enumerate(heads)]
    s_ds = [_dot(smats[j], ones_b) for j in range(len(heads))]
    s_qn = [_dot((t["q"] * n_ref[t["st"]]).astype(BF16), ones_b) for t in heads]
    hhs = []
    for j, t in enumerate(heads):
        st = t["st"]
        num = s_sv[j] + t["w_inter"] * s_qc[j]
        den = s_ds[j] + t["w_inter"] * s_qn[j]
        hhs.append(num / jnp.maximum(jnp.abs(den), t["floor"]))
        c_ref[st] = t["decay"] * c_ref[st] + s_kv[j]
        n_ref[st] = t["decay"] * n_ref[st] + jnp.sum(t["wkk"], axis=0, keepdims=True)
        m_ref[st] = t["m_new"]
    s_ms = [_dot((hh * hh).astype(BF16), ones_b) for hh in hhs]
    for j, t in enumerate(heads):
        b, hs = t["b"], t["hs"]
        hn = hhs[j] * lax.rsqrt(s_ms[j] * (1.0 / D) + EPS) * hn_ref[:, hs]
        gate = _sigmoid(o_ref[b, :, hs].astype(F32)) * _silu(z_ref[b, :, hs].astype(F32))
        out_ref[b, :, hs] = (hn * gate).astype(out_ref.dtype)


def _mlstm(proj, small, conv_w, conv_b, f_bias_row, head_norm, batch, seq):
    L = M_CHUNK
    nc = seq // L
    p3 = proj.reshape(batch, seq, proj.shape[1])
    s3 = small.reshape(batch, seq, LANES)
    cb = lambda off: pl.BlockSpec((batch, L, M_W), lambda c, off=off: (0, c, off // M_W))
    full = lambda r, w: pl.BlockSpec((r, w), lambda c: (0, 0))
    out = pl.pallas_call(
        _mlstm_kernel,
        grid=(nc,),
        in_specs=[
            cb(C_MQ), cb(C_MK), cb(C_MV), cb(C_MO), cb(C_MZ),
            pl.BlockSpec((batch, L, LANES), lambda c: (0, c, 0)),
            full(CONV_K, 2 * M_W), full(1, 2 * M_W), full(1, LANES), full(1, M_W),
        ],
        out_specs=pl.BlockSpec((batch, L, M_W), lambda c: (0, c, 0)),
        out_shape=jax.ShapeDtypeStruct((batch, seq, M_W), BF16),
        scratch_shapes=[
            pltpu.VMEM((batch, 2 * L, 2 * M_W), BF16),
            pltpu.VMEM((batch * M_HEADS, M_DIM, M_DIM), F32),
            pltpu.VMEM((batch * M_HEADS, 1, M_DIM), F32),
            pltpu.VMEM((batch * M_HEADS, 1, 1), F32),
        ],
        compiler_params=_cparams("arbitrary"),
        name="mlstm",
    )(p3, p3, p3, p3, p3, s3, conv_w, conv_b.reshape(1, -1), f_bias_row, head_norm.reshape(1, -1))
    return out.reshape(batch * seq, M_W)


def _group_rms(x, w_row):
    r = lax.broadcasted_iota(jnp.int32, (LANES, LANES), 0)
    c = lax.broadcasted_iota(jnp.int32, (LANES, LANES), 1)
    same_head = jnp.where((r < N_DIM) == (c < N_DIM), 1.0, 0.0).astype(BF16)
    ms = _dot_hilo(x * x, same_head) * (1.0 / N_DIM)
    return x * lax.rsqrt(ms + EPS) * w_row


def _lane_gather01(src_lane, width=LANES):
    r = lax.broadcasted_iota(jnp.int32, (LANES, width), 0)
    c = lax.broadcasted_iota(jnp.int32, (LANES, width), 1)
    return jnp.where(r == src_lane(c), 1.0, 0.0).astype(BF16)


def _dot_hilo(x, m01):
    hi = x.astype(BF16)
    lo = (x - hi.astype(F32)).astype(BF16)
    return _dot(hi, m01) + _dot(lo, m01)


def _rope(x, cs, s1, s2):
    return x * cs + pltpu.roll(x, ROPE_HALF, 1) * s1 + pltpu.roll(x, LANES - ROPE_HALF, 1) * s2


def _dup_halves(x):
    dup = _lane_gather01(lambda c: jnp.where(c < LANES, 0, N_DIM) + (c & (N_DIM - 1)), 2 * LANES)
    both = _dot(x, dup).astype(BF16)
    return both[:, 0:LANES], both[:, LANES:2 * LANES]


def _transpose01(x):
    return _dot_nt(_lane_gather01(lambda c: c), x)


def _nsa_prep_kernel(q_ref, kv_ref, kw_ref, sm_ref, cs_ref, s1_ref, s2_ref, qn_ref, kn_ref,
                     qt_ref, ka_ref, kwd_ref, vst_ref, vwt_ref, kc_ref, vc_ref, gtt_ref,
                     *, tp, seq):
    cs, s1, s2 = cs_ref[...], s1_ref[...], s2_ref[...]
    nq = N_W // LANES
    kv = kv_ref[...].astype(F32)
    kwb = kw_ref[...].astype(F32)
    kc_ref[...] = kv[:, 0:LANES]
    vc_ref[...] = kv[:, LANES:2 * LANES]
    slabs = [q_ref[:, j * LANES:(j + 1) * LANES].astype(F32) for j in range(nq)]
    slabs += [kv[:, 2 * LANES:3 * LANES], kwb[:, 0:LANES]]
    gains = [qn_ref[...]] * nq + [kn_ref[1:2, :], kn_ref[2:3, :]]
    normed = [_group_rms(x, w) for x, w in zip(slabs, gains)]
    roped = [_rope(x, cs, s1, s2) for x in normed]
    qs = [(x * (N_DIM ** -0.5 * LOG2E)).astype(BF16) for x in roped[:nq]]
    k0, k1 = _dup_halves(roped[nq].astype(BF16))
    w0, w1 = _dup_halves(roped[nq + 1].astype(BF16))
    gates = _sigmoid(sm_ref[...])
    g_hi = gates.astype(BF16)
    g_lo = (gates - g_hi.astype(F32)).astype(BF16)
    qts = [_transpose01(x) for x in qs]
    vst = _transpose01(kv_ref[:, 3 * LANES:4 * LANES])
    vwt = _transpose01(kw_ref[:, LANES:2 * LANES])
    gtt_ref[...] = _transpose01(g_hi) + _transpose01(g_lo)
    for j in range(nq):
        qt_ref[j * LANES:(j + 1) * LANES, :] = qts[j].astype(BF16)
    pos0 = (pl.program_id(0) * tp) % seq
    pos = pos0 + lax.broadcasted_iota(jnp.int32, (tp, LANES), 0)
    blk = lax.broadcasted_iota(jnp.int32, (tp, LANES), 1)
    onehot = jnp.where(jnp.right_shift(pos, SEL_SHIFT) == blk, 1.0, 0.0).astype(BF16)
    ka_ref[:, 0:LANES] = k0
    ka_ref[:, LANES:2 * LANES] = onehot
    ka_ref[:, 2 * LANES:3 * LANES] = k1
    ka_ref[:, 3 * LANES:4 * LANES] = onehot
    fill =(lax.broadcasted_iota(jnp.int32, (V_ROWS - N_DIM, Q_BLOCK), 0) == 0).astype(BF16)
    for c in range(tp // Q_BLOCK):
        for g in range(N_GROUPS):
            vst_ref[c, g * V_ROWS:g * V_ROWS + N_DIM, :] = (
                vst[g * N_DIM:(g + 1) * N_DIM, c * Q_BLOCK:(c + 1) * Q_BLOCK].astype(BF16))
            vst_ref[c, g * V_ROWS + N_DIM:(g + 1) * V_ROWS, :] = fill
    kwd_ref[:, 0:LANES] = w0
    kwd_ref[:, LANES:2 * LANES] = w1
    vwt_ref[...] = vwt.astype(BF16)


def _nsa_prep(proj, small, cs, s1, s2, qn_row, kn_rows, batch, seq):
    tp = 2 * KV_TILE
    t = batch * seq
    nsb = seq // tp
    rb = lambda w, off: pl.BlockSpec((tp, w), lambda i, off=off, w=w: (i, off // w))
    tab = pl.BlockSpec((tp, LANES), lambda i: (i % nsb, 0))
    o = lambda w: pl.BlockSpec((tp, w), lambda i: (i, 0))
    ot = lambda r: pl.BlockSpec((r, tp), lambda i: (0, i))
    sds = lambda w, dt: jax.ShapeDtypeStruct((t, w), dt)
    return pl.pallas_call(
        functools.partial(_nsa_prep_kernel, tp=tp, seq=seq),
        grid=(t // tp,),
        in_specs=[rb(512, C_NQ), rb(512, C_KC), rb(512, C_KW), o(LANES), tab, tab, tab,
                  pl.BlockSpec((1, LANES), lambda i: (0, 0)), pl.BlockSpec((3, LANES), lambda i: (0, 0))],
        out_specs=[ot(N_W), o(512), o(256),
                   pl.BlockSpec((tp // Q_BLOCK, N_GROUPS * V_ROWS, Q_BLOCK), lambda i: (i, 0, 0)),
                   ot(LANES), o(128), o(128), ot(LANES)],
        out_shape=[jax.ShapeDtypeStruct((N_W, t), BF16), sds(512, BF16), sds(256, BF16),
                   jax.ShapeDtypeStruct((t // Q_BLOCK, N_GROUPS * V_ROWS, Q_BLOCK), BF16),
                   jax.ShapeDtypeStruct((LANES, t), BF16),
                   sds(128, F32), sds(128, F32), jax.ShapeDtypeStruct((LANES, t), F32)],
        compiler_params=_cparams("parallel"),
        name="nsa_prep",
    )(proj, proj, proj, small, cs, s1, s2, qn_row, kn_rows)


def _gelu_tanh(x):
    return 0.5 * x * (1.0 + jnp.tanh(0.7978845608028654 * (x + 0.044715 * x * x * x)))


def _compress_mlp(x_ref, pa_ref, pb_ref, wa_ref, wb_ref, w2_ref, hb_ref):
    n = x_ref.shape[0] // CMP_STRIDE
    ha = hb = None
    for r in range(CMP_STRIDE):
        xr = x_ref[pl.ds(r, n, stride=CMP_STRIDE), :]
        ws = slice(r * LANES, (r + 1) * LANES)
        da = _dot((xr + pa_ref[:, ws]).astype(BF16), wa_ref[ws, :])
        db = _dot((xr + pb_ref[:, ws]).astype(BF16), wb_ref[ws, :])
        ha = da if ha is None else ha + da
        hb = db if hb is None else hb + db
    hb_ref[0:n, :] = hb
    hb_ref[n:n + SUBLANES, :] = jnp.zeros((SUBLANES, hb_ref.shape[1]), F32)
    hid = ha + hb_ref[1:n + 1, :]
    return _dot(_gelu_tanh(hid).astype(BF16), w2_ref[...])


def _nsa_compress_kernel(kx_ref, vx_ref, kpa_ref, kpb_ref, vpa_ref, vpb_ref, kwa_ref, kwb_ref, kw2_ref,
                         vwa_ref, vwb_ref, vw2_ref, cs_ref, s1_ref, s2_ref, kn_ref,
                         kcd_ref, vct_ref, hb_ref):
    n = kx_ref.shape[0] // CMP_STRIDE
    rowi = lax.broadcasted_iota(jnp.int32, (n, LANES), 0)
    live = rowi < n - 1
    kc = _compress_mlp(kx_ref, kpa_ref, kpb_ref, kwa_ref, kwb_ref, kw2_ref, hb_ref)
    kc = _rope(_group_rms(kc, kn_ref[0:1, :]), cs_ref[...], s1_ref[...], s2_ref[...])
    kc = jnp.where(live, kc, 0.0)
    c0, c1 = _dup_halves(kc.astype(BF16))
    kcd_ref[:, 0:LANES] = c0
    kcd_ref[:, LANES:2 * LANES] = c1
    vc = _compress_mlp(vx_ref, vpa_ref, vpb_ref, vwa_ref, vwb_ref, vw2_ref, hb_ref)
    vc = jnp.where(live, vc, 0.0)
    vct_ref[0] = vc.T.astype(BF16)


def _nsa_compress(kx, vx, kp, vp, cs, s1, s2, kn_rows, batch, nrow):
    full = lambda a: pl.BlockSpec(a.shape, lambda b: (0,) * a.ndim)
    xs = pl.BlockSpec((nrow * CMP_STRIDE, LANES), lambda b: (b, 0))
    args = (kx, vx, kp[0], kp[1], vp[0], vp[1], kp[2], kp[3], kp[4], vp[2], vp[3], vp[4], cs, s1, s2, kn_rows)
    return pl.pallas_call(
        _nsa_compress_kernel,
        grid=(batch,),
        in_specs=[xs, xs] + [full(a) for a in args[2:]],
        out_specs=[pl.BlockSpec((nrow, 2 * LANES), lambda b: (b, 0)),
                   pl.BlockSpec((1, LANES, nrow), lambda b: (b, 0, 0))],
        out_shape=[jax.ShapeDtypeStruct((batch * nrow, 2 * LANES), BF16),
                   jax.ShapeDtypeStruct((batch, LANES, nrow), BF16)],
        scratch_shapes=[pltpu.VMEM((nrow + SUBLANES, 2 * CMP_HIDDEN), F32)],
        compiler_params=_cparams("parallel"),
        name="nsa_compress",
    )(*args)


def _compress_params(pos, w1, w2):
    def expand(w1_half, pos_half):
        w = w1_half.reshape(CMP_STRIDE, N_DIM, CMP_HIDDEN)
        z = jnp.zeros_like(w)
        g0 = jnp.concatenate([w, z], axis=-1)
        g1 = jnp.concatenate([z, w], axis=-1)
        wx = jnp.stack([g0, g1], axis=1).reshape(CMP_STRIDE * LANES, 2 * CMP_HIDDEN)
        px = jnp.concatenate([pos_half, pos_half], axis=-1).reshape(1, CMP_STRIDE * LANES)
        return wx.astype(BF16), px.astype(F32)

    half = CMP_STRIDE * N_DIM
    wa, pa = expand(w1[:half], pos[:CMP_STRIDE])
    wb, pb = expand(w1[half:], pos[CMP_STRIDE:])
    z2 = jnp.zeros_like(w2)
    w2x = jnp.concatenate([jnp.concatenate([w2, z2], axis=1), jnp.concatenate([z2, w2], axis=1)], axis=0)
    return pa, pb, wa, wb, w2x.astype(BF16)


def _nsa_attn_kernel(qt_ref, gtt_ref, z_ref, ka_ref, vst_ref,
                     kw0, kw1, kw2, kw3, kw4, wt0, wt1, wt2, wt3, wt4,
                     kcd_ref, vct_ref, ovt_ref, out_ref, qa_ref, acc_ref, *, ncmp):
    QB = Q_BLOCK
    NP = N_HEADS // 2
    PW = 2 * QB
    i = pl.program_id(1)
    t0 = i * QB
    kw_refs = (kw0, kw1, kw2, kw3, kw4)
    vwt_refs = (wt0, wt1, wt2, wt3, wt4)
    nwt = len(kw_refs)

    sub = lax.broadcasted_iota(jnp.int32, (QB, QB), 0)
    lane = lax.broadcasted_iota(jnp.int32, (QB, QB), 1)
    top = sub < N_DIM
    t_row = t0 + lax.broadcasted_iota(jnp.int32, (1, QB), 1)
    t_row2 = jnp.concatenate([t_row, t_row], axis=1)

    def pair_q(pr):
        slab = qt_ref[pr * LANES:(pr + 1) * LANES, :]
        zero = jnp.zeros_like(slab)
        return jnp.concatenate([jnp.where(top, slab, zero), jnp.where(top, zero, slab)], axis=1)

    blk_f = sub.astype(F32)
    cur = jnp.right_shift(t_row, SEL_SHIFT)
    grp = lambda pr: pr // (N_HPG // 2)
    for pr in range(NP):
        qa_ref[pr, 0:LANES, :] = pair_q(pr)

    def compressed(n):
        nidx = lax.broadcasted_iota(jnp.int32, (n, PW), 0)
        ok = (nidx * CMP_STRIDE + (CMP_LEN - 1) <= t_row2) & (nidx < ncmp - 1)
        any_ok = t_row2 >= CMP_LEN - 1
        s_cmp = [_dot(kcd_ref[0:n, grp(pr) * LANES:(grp(pr) + 1) * LANES], qa_ref[pr, 0:LANES, :])
                 for pr in range(NP)]
        p_cmp = []
        for pr in range(NP):
            s = jnp.where(ok, s_cmp[pr], NEG)
            mx = jnp.max(s, axis=0, keepdims=True)
            p = jnp.exp2(s - mx)
            den = jnp.sum(p, axis=0, keepdims=True)
            p_cmp.append(p * jnp.where(any_ok, 1.0 / den, 0.0))
        outs = [_dot(vct_ref[0, grp(pr) * N_DIM:(grp(pr) + 1) * N_DIM, 0:n], p_cmp[pr].astype(BF16))
                for pr in range(NP)]
        for g in range(N_GROUPS):
            psum = jnp.zeros((n, QB), F32)
            for pp in range(N_HPG // 2):
                p = p_cmp[g * (N_HPG // 2) + pp]
                psum = psum + p[:, 0:QB] + p[:, QB:PW]
            p_hi = psum.astype(BF16)
            p_lo = (psum - p_hi.astype(F32)).astype(BF16)
            outs.append(_dot(ovt_ref[:, 0:n], p_hi) + _dot(ovt_ref[:, 0:n], p_lo))
        return tuple(outs)

    n_opts = list(range(CMP_CHUNK, ncmp + 1, CMP_CHUNK))
    need = (t0 + QB - CMP_LEN) // CMP_STRIDE + 1
    outs = lax.switch(jnp.clip((need - 1) // CMP_CHUNK, 0, len(n_opts) - 1),
                      [functools.partial(compressed, n) for n in n_opts])
    o_cmp, imps = outs[:NP], outs[NP:]
    forced = (sub == 0) | (sub == cur) | (sub == cur - 1)
    n_rounds = SEL_TOPK - 3
    valid = sub <= cur
    works0 = [jnp.where(valid, jnp.where(forced, -jnp.inf, imp), NEG) for imp in imps]

    def rounds(works, exact):
        sels = [forced for _ in range(N_GROUPS)]
        for _ in range(n_rounds):
            for g in range(N_GROUPS):
                mx = jnp.max(works[g], axis=0, keepdims=True)
                if exact:
                    idx = jnp.min(jnp.where(works[g] == mx, blk_f, float(LANES)), axis=0, keepdims=True)
                    pick = blk_f == idx
                else:
                    pick = works[g] == mx
                sels[g] = sels[g] | pick
                works[g] = jnp.where(pick, -jnp.inf, works[g])
        return tuple(jnp.where(s & valid, 0.0, NEG) for s in sels)

    bias = rounds(list(works0), exact=False)
    picked = [jnp.sum(jnp.where((b == 0.0) & jnp.logical_not(forced), 1.0, 0.0), axis=0, keepdims=True)
              for b in bias]
    tied = jnp.max(functools.reduce(jnp.maximum, picked)) > n_rounds
    bias = lax.cond(tied, lambda: rounds(list(works0), exact=True), lambda: bias)
    for g in range(N_GROUPS):
        neg = bias[g].astype(BF16)
        neg2 = jnp.concatenate([neg, neg], axis=1)
        for pp in range(N_HPG // 2):
            qa_ref[g * (N_HPG // 2) + pp, LANES:2 * LANES, :] = neg2

    CH = QB
    above = jnp.concatenate([sub > lane, sub > lane], axis=1)

    def sel_scores(cj):
        k0 = pl.multiple_of(cj * CH, CH)
        return [_dot(ka_ref[pl.ds(k0, CH), 2 * grp(pr) * LANES:2 * (grp(pr) + 1) * LANES], qa_ref[pr])
                for pr in range(NP)]

    def sel_softmax(ss, ms, masked):
        new, alphas, ps = [], [], []
        for pr in range(NP):
            s = jnp.where(above, NEG, ss[pr]) if masked else ss[pr]
            m_new = jnp.maximum(ms[pr], jnp.max(s, axis=0, keepdims=True))
            alphas.append(jnp.exp2(ms[pr] - m_new))
            ps.append(jnp.exp2(s - m_new).astype(BF16))
            new.append(m_new)
        return tuple(new), alphas, ps

    def sel_values(cj, alphas, ps):
        for pr in range(NP):
            g = grp(pr)
            acc_ref[pr] = alphas[pr] * acc_ref[pr] + _dot(vst_ref[cj, g * V_ROWS:(g + 1) * V_ROWS, :], ps[pr])

    def sel_chunk(cj, ms, masked):
        ms, alphas, ps = sel_softmax(sel_scores(cj), ms, masked)
        sel_values(cj, alphas, ps)
        return ms

    def sel_tile(jt, ms, nc):
        c0 = jt * nc
        k0 = pl.multiple_of(c0 * CH, CH)
        big = [_dot(ka_ref[pl.ds(k0, nc * CH), 2 * grp(pr) * LANES:2 * (grp(pr) + 1) * LANES], qa_ref[pr])
               for pr in range(NP)]
        for c in range(nc):
            ss = [b[c * CH:(c + 1) * CH] for b in big]
            ms, alphas, ps = sel_softmax(ss, ms, False)
            sel_values(c0 + c, alphas, ps)
        return ms

    acc_ref[...] = jnp.zeros_like(acc_ref)
    ms = tuple(jnp.full((1, PW), 2.0 * NEG, F32) for _ in range(NP))
    done = 0
    for nc in (4 * KV_TILE // CH, 2 * KV_TILE // CH, KV_TILE // CH):
        n_run = (i - done) // nc
        ms = lax.fori_loop(done // nc, done // nc + n_run, functools.partial(sel_tile, nc=nc), ms)
        done = done + n_run * nc
    ms = lax.fori_loop(done, i, lambda cj, ms: sel_chunk(cj, ms, False), ms)
    s_diag = sel_scores(i)
    s_win = [[_dot(kw_refs[c][:, grp(pr) * LANES:(grp(pr) + 1) * LANES], qa_ref[pr, 0:LANES, :])
              for c in range(nwt)] for pr in range(NP)]
    ms, alphas, ps_diag = sel_softmax(s_diag, ms, True)
    p_win, d_win = [], []
    for pr in range(NP):
        ss = []
        for c in range(nwt):
            ok = (i - (nwt - 1) + c) >= 0
            if c == 0:
                ok = ok & above
            elif c == nwt - 1:
                ok = ok & jnp.logical_not(above)
            ss.append(jnp.where(ok, s_win[pr][c], NEG))
        mx = ss[0].max(axis=0, keepdims=True)
        for c in range(1, nwt):
            mx = jnp.maximum(mx, ss[c].max(axis=0, keepdims=True))
        ps = [jnp.exp2(ss[c] - mx) for c in range(nwt)]
        den = ps[0].sum(axis=0, keepdims=True)
        for c in range(1, nwt):
            den = den + ps[c].sum(axis=0, keepdims=True)
        p_win.append([p.astype(BF16) for p in ps])
        d_win.append(den)
    sel_values(i, alphas, ps_diag)
    o_win = []
    for pr in range(NP):
        g = grp(pr)
        acc = _dot(vwt_refs[0][g * N_DIM:(g + 1) * N_DIM, :], p_win[pr][0])
        for c in range(1, nwt):
            acc = acc + _dot(vwt_refs[c][g * N_DIM:(g + 1) * N_DIM, :], p_win[pr][c])
        o_win.append(acc * (1.0 / d_win[pr]))
    o_sel = [acc_ref[pr, 0:N_DIM, :] * (1.0 / acc_ref[pr, N_DIM:N_DIM + 1, :]) for pr in range(NP)]

    for pr in range(NP):
        halves = []
        for par in range(2):
            h = 2 * pr + par
            cs = slice(par * QB, (par + 1) * QB)
            gc = gtt_ref[SM_G + 3 * h:SM_G + 3 * h + 1, :]
            gs = gtt_ref[SM_G + 3 * h + 1:SM_G + 3 * h + 2, :]
            gw = gtt_ref[SM_G + 3 * h + 2:SM_G + 3 * h + 3, :]
            halves.append(gc * o_cmp[pr][:, cs] + gs * o_sel[pr][:, cs] + gw * o_win[pr][:, cs])
        y = jnp.concatenate(halves, axis=0).T
        sl = slice(pr * LANES, (pr + 1) * LANES)
        out_ref[:, sl] = (y * _silu(z_ref[:, sl].astype(F32))).astype(out_ref.dtype)


def _nsa_attn(proj, qt, gtt, ka, vst, kwd, vwt, kcd, vct, ovt, batch, seq):
    QB = Q_BLOCK
    nqb = seq // QB
    ncmp = seq // CMP_STRIDE
    nwt = WINDOW // QB + 1
    row = lambda w: pl.BlockSpec((QB, w), lambda b, i: (b * nqb + i, 0))
    colb = lambda r: pl.BlockSpec((r, QB), lambda b, i: (0, b * nqb + i))
    wblk = lambda b, i, c: b * nqb + jnp.maximum(i - (nwt - 1) + c, 0)
    in_specs = ([colb(N_W), colb(LANES), pl.BlockSpec((QB, N_W), lambda b, i: (b * nqb + i, C_NZ // N_W)),
                 pl.BlockSpec((seq, 4 * LANES), lambda b, i: (b, 0)),
                 pl.BlockSpec((nqb, N_GROUPS * V_ROWS, QB), lambda b, i: (b, 0, 0))]
                + [pl.BlockSpec((QB, 2 * LANES), lambda b, i, c=c: (wblk(b, i, c), 0)) for c in range(nwt)]
                + [pl.BlockSpec((LANES, QB), lambda b, i, c=c: (0, wblk(b, i, c))) for c in range(nwt)]
                + [pl.BlockSpec((ncmp, 2 * LANES), lambda b, i: (b, 0)),
                   pl.BlockSpec((1, LANES, ncmp), lambda b, i: (b, 0, 0)),
                   pl.BlockSpec((LANES, ncmp), lambda b, i: (0, 0))])
    return pl.pallas_call(
        functools.partial(_nsa_attn_kernel, ncmp=ncmp),
        grid=(batch, nqb),
        in_specs=in_specs,
        out_specs=row(N_W),
        out_shape=jax.ShapeDtypeStruct((batch * seq, N_W), BF16),
        scratch_shapes=[pltpu.VMEM((N_HEADS // 2, 2 * LANES, 2 * QB), BF16),
                        pltpu.VMEM((N_HEADS // 2, V_ROWS, 2 * QB), F32)],
        compiler_params=_cparams("parallel", "arbitrary"),
        name="nsa_attn",
    )(qt, gtt, proj, ka, vst, *([kwd] * nwt), *([vwt] * nwt), kcd, vct, ovt)


def _hgrn_kernel(f_ref, i_ref, q_ref, z_ref, lb_ref, hn_ref, out_ref, st_ref):
    T, D = H_TILE, H_DIM
    W = H_W

    @pl.when(pl.program_id(0) == 0)
    def _():
        st_ref[...] = jnp.zeros_like(st_ref)

    row = lax.broadcasted_iota(jnp.int32, (T, T), 0)
    col = lax.broadcasted_iota(jnp.int32, (T, T), 1)
    tri01 = jnp.where(col <= row, 1.0, 0.0).astype(BF16)
    sub3 = lax.broadcasted_iota(jnp.int32, (T // SUBLANES, SUBLANES, 1), 1)
    lag_ok = [jnp.bitwise_and(sub3, H_LAG - 1) >= d for d in range(H_LAG)]

    def batch_row(n):
        lb = lb_ref[...]
        fp = f_ref[n]
        e = jnp.exp(-jnp.abs(fp))
        r = 1.0 / (1.0 + e)
        sig = jnp.where(fp >= 0, r, e * r)
        nsig = jnp.where(fp >= 0, e * r, r)
        logf2 = jnp.log(lb + (1.0 - lb) * sig) * LOG2E
        kk = (1.0 - lb) * nsig
        bcum = _dot01(tri01, logf2)

        segs = []
        sg = 2 * H_LAG
        while sg <= T:
            if sg == T:
                bmid = jnp.broadcast_to(bcum[sg // 2 - 1:sg // 2, :], (T, W))
            else:
                b3 = bcum.reshape(T // sg, sg, W)
                bmid = jnp.broadcast_to(b3[:, sg // 2 - 1:sg // 2, :], (T // sg, sg, W)).reshape(T, W)
            x = bcum - bmid
            shift = sg.bit_length() - 1
            pair_ok = ((jnp.right_shift(row, shift) == jnp.right_shift(col, shift))
                       & (jnp.bitwise_and(row, sg - 1) >= sg // 2) & (jnp.bitwise_and(col, sg - 1) < sg // 2))
            xb = x.astype(BF16)
            segs.append((jnp.exp2(jnp.minimum(xb, -xb)), pair_ok))
            sg *= 2

        b_end = bcum[T - 1:T, :]
        v_all = i_ref[n].astype(F32)

        def head_scores(h):
            hs = slice(h * D, (h + 1) * D)
            q = _silu(q_ref[n, :, hs].astype(F32))
            k = kk[:, hs]
            b = bcum[:, hs]
            v = v_all[:, hs]
            vb = v.astype(BF16)
            a_seg = []
            qb, kb = q.astype(BF16), k.astype(BF16)
            for fac, _ in segs:
                f = fac[:, hs]
                a_seg.append(_dot_nt(qb * f, kb * f))
            st = st_ref[n * H_HEADS + h]
            o_state = _dot_nt(qb * jnp.exp2(b.astype(BF16)), st.astype(BF16))
            kt = kb * jnp.exp2((b_end[:, hs] - b).astype(BF16))
            st_ref[n * H_HEADS + h] = st * jnp.exp2(b_end[:, hs]) + _dot_tn(vb, kt)
            q3, k3, b3, v3 = (a.reshape(T // SUBLANES, SUBLANES, D) for a in (q, k, b, v))
            o3 = jnp.sum(q3 * k3, axis=2, keepdims=True) * v3
            for d in range(1, H_LAG):
                ks, bs, vs = (pltpu.roll(a, d, 1) for a in (k3, b3, v3))
                a = jnp.sum(q3 * ks * jnp.exp2(b3 - bs), axis=2, keepdims=True)
                o3 = o3 + jnp.where(lag_ok[d], a, 0.0) * vs
            return hs, a_seg, o3.reshape(T, D) + o_state, vb

        def head_finish(hs, a_seg, o_part, vb):
            a_tot = jnp.zeros((T, T), F32)
            for (_, pair_ok), a in zip(segs, a_seg):
                a_tot = jnp.where(pair_ok, a, a_tot)
            oh = o_part + _dot(a_tot.astype(BF16), vb)
            on = oh * lax.rsqrt(jnp.mean(oh * oh, axis=1, keepdims=True) + EPS) * hn_ref[:, hs]
            out_ref[n, :, hs] = (on * _silu(z_ref[n, :, hs].astype(F32))).astype(out_ref.dtype)

        return head_scores, head_finish

    pending = None
    for n in range(f_ref.shape[0]):
        head_scores, head_finish = batch_row(n)
        for h in range(H_HEADS):
            cur = head_scores(h)
            if pending is not None:
                pending[0](*pending[1])
            pending = (head_finish, cur)
    pending[0](*pending[1])


def _hgrn(proj, fgate, lb_row, hn_row, batch, seq):
    T = H_TILE
    nt = seq // T
    p3 = proj.reshape(batch, seq, proj.shape[1])
    cb = lambda j: pl.BlockSpec((batch, T, H_W), lambda c, j=j: (0, c, j))
    vec = pl.BlockSpec((1, H_W), lambda c: (0, 0))
    out = pl.pallas_call(
        _hgrn_kernel,
        grid=(nt,),
        in_specs=[cb(0), cb(1), cb(2), cb(3), vec, vec],
        out_specs=cb(0),
        out_shape=jax.ShapeDtypeStruct((batch, seq, H_W), BF16),
        scratch_shapes=[pltpu.VMEM((batch * H_HEADS, H_DIM, H_DIM), F32)],
        compiler_params=_cparams("arbitrary"),
        name="hgrn2",
    )(fgate.reshape(batch, seq, H_W), p3, p3, p3, lb_row, hn_row)
    return out.reshape(batch * seq, H_W)


def _even_relayout(a):
    nb = 5 * M_W + 2 * M_HEADS
    g0 = nb + N_W + 6 * N_KV_W
    col = lambda lo, n: lax.slice_in_dim(a, lo, lo + n, axis=-1)
    pad = jnp.zeros(a.shape[:-1] + (C_NZ - C_SMALL - SM_G - 3 * N_HEADS,), a.dtype)
    out = jnp.concatenate([
        col(0, 5 * M_W),
        col(nb, N_W + 6 * N_KV_W),
        col(5 * M_W, 2 * M_HEADS),
        col(g0, 3 * N_HEADS), pad,
        col(g0 + 3 * N_HEADS, N_W),
    ], axis=-1)
    assert out.shape[-1] == EV_COLS
    return out


def _rope_tables(positions):
    inv_freq = ROPE_THETA ** (-np.arange(ROPE_HALF, dtype=np.float64) / ROPE_HALF)
    ang = positions.astype(np.float64)[:, None] * inv_freq[None, :]
    cos, sin = np.cos(ang).astype(np.float32), np.sin(ang).astype(np.float32)
    n = positions.shape[0]
    z8 = np.zeros((n, ROPE_HALF), np.float32)
    rest1 = np.ones((n, N_DIM - ROPE_DIM), np.float32)
    rest0 = np.zeros((n, N_DIM - ROPE_DIM), np.float32)
    cs = np.concatenate([cos, cos, rest1], axis=1)
    s1 = np.concatenate([z8, sin, rest0], axis=1)
    s2 = np.concatenate([-sin, z8, rest0], axis=1)
    two = lambda a: np.concatenate([a, a], axis=1)
    return two(cs), two(s1), two(s2)


def _overlap_matrix(seq):
    ncmp = seq // CMP_STRIDE
    n_sel = seq // SEL_BLOCK
    cs = np.arange(ncmp) * CMP_STRIDE
    ss = np.arange(n_sel) * SEL_BLOCK
    ov = np.clip(np.minimum(cs[:, None] + CMP_LEN, ss[None, :] + SEL_BLOCK)
                 - np.maximum(cs[:, None], ss[None, :]), 0, None).astype(np.float32)
    ov[ncmp - 1, :] = 0.0
    return np.pad(ov, ((0, 0), (0, LANES - n_sel)))


def _even_layer(x2d, res, norm_w, w_in, b_in, f_bias, conv_w, conv_b, head_norm, q_norm, k_norm,
                ck_pos, ck_w1, ck_w2, cv_pos, cv_w1, cv_w2, w_out, batch, seq):
    assert seq // SEL_BLOCK <= LANES and seq % KV_TILE == 0
    outs = _norm_proj(x2d, norm_w, _even_relayout(w_in.astype(BF16)), _even_relayout(b_in),
                      (C_SMALL, C_SMALL + LANES), tn=EV_COLS // 3, res=res, tm=1024 if res else 2048)
    if res:
        x2d = outs[0]
    proj, small = outs[-2:]

    f_bias_row = jnp.zeros((1, LANES), F32).at[0, SM_F:SM_F + M_HEADS].set(f_bias)
    ya = _mlstm(proj, small, conv_w, conv_b, f_bias_row, head_norm, batch, seq)

    cs, s1, s2 = (jnp.asarray(a) for a in _rope_tables(np.arange(seq)))
    qn_row = jnp.concatenate([q_norm, q_norm]).reshape(1, LANES)
    kn_rows = jnp.concatenate([k_norm, k_norm], axis=1)
    qt, ka, kwd, vst, vwt, kc, vc, gtt = _nsa_prep(proj, small, cs, s1, s2, qn_row, kn_rows, batch, seq)

    ncmp = seq // CMP_STRIDE
    ccs, cs1, cs2 = (jnp.asarray(a) for a in _rope_tables(np.arange(ncmp) * CMP_STRIDE + CMP_LEN - 1))
    kcd, vct = _nsa_compress(
        kc, vc,
        _compress_params(ck_pos, ck_w1, ck_w2), _compress_params(cv_pos, cv_w1, cv_w2),
        ccs, cs1, cs2, kn_rows, batch, ncmp)

    ovt = jnp.asarray(_overlap_matrix(seq).T).astype(BF16)
    yb = _nsa_attn(proj, qt, gtt, ka, vst, kwd, vwt, kcd, vct, ovt, batch, seq)

    w_o = w_out.astype(BF16)
    return x2d, [(ya, w_o[:M_W]), (yb, w_o[M_W:])]


def _odd_layer(x2d, res, norm_w, w_in, b_in, lb, head_norm, w_out, batch, seq):
    outs = _norm_proj(x2d, norm_w, w_in.astype(BF16), b_in, (0, H_W), tn=H_W, res=res)
    if res:
        x2d = outs[0]
    proj, fgate = outs[-2:]
    y = _hgrn(proj, fgate, lb.reshape(1, H_W), head_norm.reshape(1, H_W), batch, seq)
    return x2d, [(y, w_out.astype(BF16))]


def kernel(x, norm_w, ev_w_in, ev_b_in, mlstm_f_bias, mlstm_conv_w, mlstm_conv_b, mlstm_head_norm, nsa_q_norm, nsa_k_norm, cmp_k_pos, cmp_k_w1, cmp_k_w2, cmp_v_pos, cmp_v_w1, cmp_v_w2, ev_w_out, od_w_in, od_b_in, hgrn_lb_logits, hgrn_head_norm, od_w_out):
    batch, seq, d = x.shape
    depth = norm_w.shape[0]
    lbs = jnp.cumsum(jax.nn.softmax(hgrn_lb_logits.astype(F32), axis=0), axis=0)
    x2d = x.reshape(batch * seq, d)
    res = []
    for layer in range(depth):
        j = layer // 2
        if layer % 2 == 0:
            x2d, res = _even_layer(x2d, res, norm_w[layer], ev_w_in[j], ev_b_in[j], mlstm_f_bias[j],
                                   mlstm_conv_w[j], mlstm_conv_b[j], mlstm_head_norm[j], nsa_q_norm[j],
                                   nsa_k_norm[j], cmp_k_pos[j], cmp_k_w1[j], cmp_k_w2[j], cmp_v_pos[j],
                                   cmp_v_w1[j], cmp_v_w2[j], ev_w_out[j], batch, seq)
        else:
            x2d, res = _odd_layer(x2d, res, norm_w[layer], od_w_in[j], od_b_in[j], lbs[j], hgrn_head_norm[j],
                                  od_w_out[j], batch, seq)
    if len(res) == 2:
        x2d = _out_proj2(res[0][0], res[1][0], res[0][1], res[1][1], x2d)
    elif res:
        x2d = _out_proj1(res[0][0], res[0][1], x2d)
    return x2d.reshape(batch, seq, d)
```

```python
import functools

import numpy as np
import jax
import jax.numpy as jnp
from jax import lax
from jax.experimental import pallas as pl
from jax.experimental.pallas import tpu as pltpu

F32 = jnp.float32
BF16 = jnp.bfloat16

LANES = 128
SUBLANES = 8
BF16_SUBLANES = 16
EPS = 1e-6
NEG = -1e30
VMEM_LIMIT_BYTES = 56 * 1024 * 1024

M_HEADS = 4
M_DIM = 128
M_W = M_HEADS * M_DIM
M_CHUNK = 128
CONV_K = 4
N_HEADS = 8
N_GROUPS = 2
N_HPG = N_HEADS // N_GROUPS
N_DIM = 64
N_W = N_HEADS * N_DIM
N_KV_W = N_GROUPS * N_DIM
CMP_LEN = 32
CMP_STRIDE = 16
CMP_HIDDEN = 128
CMP_CHUNK = 128
SEL_BLOCK = 64
SEL_SHIFT = 6
SEL_TOPK = 16
WINDOW = 512
Q_BLOCK = 128
ROPE_THETA = 500000.0
ROPE_DIM = N_DIM // 4
ROPE_HALF = ROPE_DIM // 2
KV_TILE = 512
V_ROWS = 80
LOG2E = 1.4426950408889634
H_HEADS = 8
H_DIM = 128
H_W = H_HEADS * H_DIM
H_TILE = 128
H_LAG = 2

C_MQ, C_MK, C_MV, C_MO, C_MZ = 0, 512, 1024, 1536, 2048
C_NQ = 2560
C_KC, C_VC, C_KS, C_VS = 3072, 3200, 3328, 3456
C_KW, C_VW, C_SMALL = 3584, 3712, 3840
C_NZ = 4096
EV_COLS = 4608
SM_I, SM_F, SM_G = 0, 4, 8


def _cparams(*sem):
    return pltpu.CompilerParams(dimension_semantics=sem, vmem_limit_bytes=VMEM_LIMIT_BYTES)


def _dot(a, b):
    return jnp.dot(a, b, preferred_element_type=F32)


def _dot_nt(a, b):
    return lax.dot_general(a, b, (((1,), (1,)), ((), ())), preferred_element_type=F32)


def _dot_tn(a, b):
    return lax.dot_general(a, b, (((0,), (0,)), ((), ())), preferred_element_type=F32)


def _sigmoid(x):
    return 1.0 / (1.0 + jnp.exp(-x))


def _silu(x):
    return x * _sigmoid(x)


def _split3(x):
    hi = x.astype(BF16)
    r = x - hi.astype(F32)
    mid = r.astype(BF16)
    lo = (r - mid.astype(F32)).astype(BF16)
    return hi, mid, lo


def _dot01(m01, x):
    hi, mid, lo = _split3(x)
    return _dot(m01, hi) + _dot(m01, mid) + _dot(m01, lo)


def _norm_proj_kernel(*refs, n_res, g_tiles, g_off):
    x_ref = refs[0]
    y_refs, wy_refs = refs[1:1 + n_res], refs[1 + n_res:1 + 2 * n_res]
    nw_ref, w_ref, b_ref = refs[1 + 2 * n_res:4 + 2 * n_res]
    outs = refs[4 + 2 * n_res:]
    xo_ref = outs[0] if n_res else None
    o_ref, g_ref, h_ref = outs[-3:]
    j = pl.program_id(1)

    @pl.when(j == 0)
    def _():
        x = x_ref[...]
        for y_ref, wy_ref in zip(y_refs, wy_refs):
            x = x + _dot(y_ref[...], wy_ref[...])
        if n_res:
            xo_ref[...] = x
        ms = jnp.mean(x * x, axis=-1, keepdims=True)
        h_ref[...] = (x * lax.rsqrt(ms + EPS) * nw_ref[...]).astype(BF16)

    acc = _dot(h_ref[...], w_ref[...]) + b_ref[...]
    o_ref[...] = acc.astype(o_ref.dtype)

    @pl.when((j >= g_tiles[0]) & (j < g_tiles[1]))
    def _():
        g_ref[...] = acc[:, g_off:g_off + g_ref.shape[1]]


def _norm_proj(x2d, norm_w, w_bf16, bias, g_cols, tn, res=(), tm=1024):
    t, d = x2d.shape
    n = w_bf16.shape[1]
    g_w = min(g_cols[1] - g_cols[0], tn)
    g_tiles = (g_cols[0] // tn, -(-g_cols[1] // tn))
    row = lambda w: pl.BlockSpec((tm, w), lambda i, j: (i, 0))
    const = lambda a: pl.BlockSpec(a.shape, lambda i, j: (0, 0))
    ys, wys = [y for y, _ in res], [wy for _, wy in res]
    out_specs = [pl.BlockSpec((tm, tn), lambda i, j: (i, j)),
                 pl.BlockSpec((tm, g_w), lambda i, j: (i, jnp.clip(j - g_tiles[0], 0, g_tiles[1] - g_tiles[0] - 1)))]
    out_shape = [jax.ShapeDtypeStruct((t, n), BF16), jax.ShapeDtypeStruct((t, g_cols[1] - g_cols[0]), F32)]
    if res:
        out_specs = [row(d)] + out_specs
        out_shape = [jax.ShapeDtypeStruct((t, d), F32)] + out_shape
    return pl.pallas_call(
        functools.partial(_norm_proj_kernel, n_res=len(res), g_tiles=g_tiles, g_off=g_cols[0] % tn),
        grid=(t // tm, n // tn),
        in_specs=([row(d)] + [row(y.shape[1]) for y in ys] + [const(wy) for wy in wys]
                  + [pl.BlockSpec((1, d), lambda i, j: (0, 0)),
                     pl.BlockSpec((d, tn), lambda i, j: (0, j)),
                     pl.BlockSpec((1, tn), lambda i, j: (0, j))]),
        out_specs=out_specs,
        out_shape=out_shape,
        scratch_shapes=[pltpu.VMEM((tm, d), BF16)],
        compiler_params=_cparams("parallel", "arbitrary"),
        name="norm_proj",
    )(x2d, *ys, *wys, norm_w.reshape(1, d), w_bf16, bias.reshape(1, n))


def _out_proj2_kernel(ya_ref, yb_ref, wa_ref, wb_ref, x_ref, o_ref):
    o_ref[...] = x_ref[...] + _dot(ya_ref[...], wa_ref[...]) + _dot(yb_ref[...], wb_ref[...])


def _out_proj2(ya, yb, wa, wb, x2d, tm=1024):
    t, d = x2d.shape
    ka, kb = ya.shape[1], yb.shape[1]
    return pl.pallas_call(
        _out_proj2_kernel,
        grid=(t // tm,),
        in_specs=[
            pl.BlockSpec((tm, ka), lambda i: (i, 0)),
            pl.BlockSpec((tm, kb), lambda i: (i, 0)),
            pl.BlockSpec((ka, d), lambda i: (0, 0)),
            pl.BlockSpec((kb, d), lambda i: (0, 0)),
            pl.BlockSpec((tm, d), lambda i: (i, 0)),
        ],
        out_specs=pl.BlockSpec((tm, d), lambda i: (i, 0)),
        out_shape=jax.ShapeDtypeStruct((t, d), F32),
        compiler_params=_cparams("parallel"),
        name="out_proj2",
    )(ya, yb, wa, wb, x2d)


def _out_proj1_kernel(y_ref, w_ref, x_ref, o_ref):
    o_ref[...] = x_ref[...] + _dot(y_ref[...], w_ref[...])


def _out_proj1(y, w, x2d, tm=2048):
    t, d = x2d.shape
    k = y.shape[1]
    return pl.pallas_call(
        _out_proj1_kernel,
        grid=(t // tm,),
        in_specs=[
            pl.BlockSpec((tm, k), lambda i: (i, 0)),
            pl.BlockSpec((k, d), lambda i: (0, 0)),
            pl.BlockSpec((tm, d), lambda i: (i, 0)),
        ],
        out_specs=pl.BlockSpec((tm, d), lambda i: (i, 0)),
        out_shape=jax.ShapeDtypeStruct((t, d), F32),
        compiler_params=_cparams("parallel"),
        name="out_proj1",
    )(y, w, x2d)


def _mlstm_kernel(q_ref, k_ref, v_ref, o_ref, z_ref, sm_ref, cw_ref, cb_ref, fb_ref, hn_ref,
                  out_ref, xbuf, c_ref, n_ref, m_ref):
    L, D = M_CHUNK, M_DIM
    NB = q_ref.shape[0]
    XR = xbuf.shape[1]
    X0 = BF16_SUBLANES

    @pl.when(pl.program_id(0) == 0)
    def _():
        xbuf[...] = jnp.zeros_like(xbuf)
        c_ref[...] = jnp.zeros_like(c_ref)
        n_ref[...] = jnp.zeros_like(n_ref)
        m_ref[...] = jnp.zeros_like(m_ref)

    t_i = lax.broadcasted_iota(jnp.int32, (L, XR), 0)
    r_i = lax.broadcasted_iota(jnp.int32, (L, XR), 1)
    row = lax.broadcasted_iota(jnp.int32, (L, L), 0)
    col = lax.broadcasted_iota(jnp.int32, (L, L), 1)
    causal = col <= row
    lane = lax.broadcasted_iota(jnp.int32, (L, LANES), 1)
    is_f = (lane >= SM_F) & (lane < SM_F + M_HEADS)

    heads = []
    for b in range(NB):
        xbuf[b, X0:X0 + L, 0:M_W] = q_ref[b]
        xbuf[b, X0:X0 + L, M_W:2 * M_W] = k_ref[b]
        x2 = xbuf[b]
        acc = cb_ref[...] + cw_ref[CONV_K - 1:CONV_K, :] * x2[X0:X0 + L].astype(F32)
        for d in range(1, CONV_K):
            shift = jnp.where(r_i == t_i + X0 - d, 1.0, 0.0).astype(BF16)
            acc = acc + cw_ref[CONV_K - 1 - d:CONV_K - d, :] * _dot(shift, x2)
        qk = _silu(acc)
        xbuf[b, 0:X0, :] = xbuf[b, L:L + X0, :]

        fpre = sm_ref[b] + fb_ref[...]
        gates = jnp.where(is_f, jnp.minimum(fpre, 0.0) - jnp.log(1.0 + jnp.exp(-jnp.abs(fpre))), fpre)
        bcols = _dot01(jnp.where(causal, 1.0, 0.0).astype(BF16), gates)
        gates_t = gates.T
        bcols_t = bcols.T
        for h in range(M_HEADS):
            hs = slice(h * D, (h + 1) * D)
            q = qk[:, hs]
            k = qk[:, M_W + h * D:M_W + (h + 1) * D] * (D ** -0.5)
            i_col = gates[:, SM_I + h:SM_I + h + 1]
            i_row = gates_t[SM_I + h:SM_I + h + 1, :]
            bcum_col = bcols[:, SM_F + h:SM_F + h + 1]
            bcum_row = bcols_t[SM_F + h:SM_F + h + 1, :]
            m_prev = m_ref[b * M_HEADS + h]
            dmat = jnp.where(causal, bcum_col - bcum_row + i_row, -jnp.inf)
            inter = bcum_col + m_prev
            m_row = jnp.maximum(jnp.max(dmat, axis=1, keepdims=True), inter)
            b_last = bcum_col[L - 1:L, :]
            dl = b_last - bcum_col + i_col
            m_new = jnp.maximum(b_last + m_prev, jnp.max(dl, axis=0, keepdims=True))
            wkk = jnp.exp(dl - m_new) * k
            heads.append(dict(
                b=b, hs=hs, st=b * M_HEADS + h, q=q, qb=q.astype(BF16), kb=k.astype(BF16),
                vb=v_ref[b, :, hs].astype(BF16), wkk=wkk,
                e=jnp.exp(dmat - m_row), w_inter=jnp.exp(inter - m_row), floor=jnp.exp(-m_row),
                decay=jnp.exp(b_last + m_prev - m_new), m_new=m_new))
    s_qk = [_dot_nt(t["qb"], t["kb"]) for t in heads]
    s_qc = [_dot(t["qb"], c_ref[t["st"]].astype(BF16)) for t in heads]
    s_kv = [_dot_tn(t["wkk"].astype(BF16), t["vb"]) for t in heads]
    ones_b = jnp.ones((L, LANES), BF16)
    smats = [(s_qk[j] * t["e"]).astype(BF16) for j, t in enumerate(heads)]
    s_sv = [_dot(smats[j], t["vb"]) for j, t in enumerate(heads)]
    s_ds = [_dot(smats[j], ones_b) for j in range(len(heads))]
    s_qn = [_dot((t["q"] * n_ref[t["st"]]).astype(BF16), ones_b) for t in heads]
    hhs = []
    for j, t in enumerate(heads):
        st = t["st"]
        num = s_sv[j] + t["w_inter"] * s_qc[j]
        den = s_ds[j] + t["w_inter"] * s_qn[j]
        hhs.append(num / jnp.maximum(jnp.abs(den), t["floor"]))
        c_ref[st] = t["decay"] * c_ref[st] + s_kv[j]
        n_ref[st] = t["decay"] * n_ref[st] + jnp.sum(t["wkk"], axis=0, keepdims=True)
        m_ref[st] = t["m_new"]
    s_ms = [_dot((hh * hh).astype(BF16), ones_b) for hh in hhs]
    for j, t in enumerate(heads):
        b, hs = t["b"], t["hs"]
        hn = hhs[j] * lax.rsqrt(s_ms[j] * (1.0 / D) + EPS) * hn_ref[:, hs]
        gate = _sigmoid(o_ref[b, :, hs].astype(F32)) * _silu(z_ref[b, :, hs].astype(F32))
        out_ref[b, :, hs] = (hn * gate).astype(out_ref.dtype)


def _mlstm(proj, small, conv_w, conv_b, f_bias_row, head_norm, batch, seq):
    L = M_CHUNK
    nc = seq // L
    p3 = proj.reshape(batch, seq, proj.shape[1])
    s3 = small.reshape(batch, seq, LANES)
    cb = lambda off: pl.BlockSpec((batch, L, M_W), lambda c, off=off: (0, c, off // M_W))
    full = lambda r, w: pl.BlockSpec((r, w), lambda c: (0, 0))
    out = pl.pallas_call(
        _mlstm_kernel,
        grid=(nc,),
        in_specs=[
            cb(C_MQ), cb(C_MK), cb(C_MV), cb(C_MO), cb(C_MZ),
            pl.BlockSpec((batch, L, LANES), lambda c: (0, c, 0)),
            full(CONV_K, 2 * M_W), full(1, 2 * M_W), full(1, LANES), full(1, M_W),
        ],
        out_specs=pl.BlockSpec((batch, L, M_W), lambda c: (0, c, 0)),
        out_shape=jax.ShapeDtypeStruct((batch, seq, M_W), BF16),
        scratch_shapes=[
            pltpu.VMEM((batch, 2 * L, 2 * M_W), BF16),
            pltpu.VMEM((batch * M_HEADS, M_DIM, M_DIM), F32),
            pltpu.VMEM((batch * M_HEADS, 1, M_DIM), F32),
            pltpu.VMEM((batch * M_HEADS, 1, 1), F32),
        ],
        compiler_params=_cparams("arbitrary"),
        name="mlstm",
    )(p3, p3, p3, p3, p3, s3, conv_w, conv_b.reshape(1, -1), f_bias_row, head_norm.reshape(1, -1))
    return out.reshape(batch * seq, M_W)


def _group_rms(x, w_row):
    r = lax.broadcasted_iota(jnp.int32, (LANES, LANES), 0)
    c = lax.broadcasted_iota(jnp.int32, (LANES, LANES), 1)
    same_head = jnp.where((r < N_DIM) == (c < N_DIM), 1.0, 0.0).astype(BF16)
    ms = _dot_hilo(x * x, same_head) * (1.0 / N_DIM)
    return x * lax.rsqrt(ms + EPS) * w_row


def _lane_gather01(src_lane, width=LANES):
    r = lax.broadcasted_iota(jnp.int32, (LANES, width), 0)
    c = lax.broadcasted_iota(jnp.int32, (LANES, width), 1)
    return jnp.where(r == src_lane(c), 1.0, 0.0).astype(BF16)


def _dot_hilo(x, m01):
    hi = x.astype(BF16)
    lo = (x - hi.astype(F32)).astype(BF16)
    return _dot(hi, m01) + _dot(lo, m01)


def _rope(x, cs, s1, s2):
    return x * cs + pltpu.roll(x, ROPE_HALF, 1) * s1 + pltpu.roll(x, LANES - ROPE_HALF, 1) * s2


def _dup_halves(x):
    dup = _lane_gather01(lambda c: jnp.where(c < LANES, 0, N_DIM) + (c & (N_DIM - 1)), 2 * LANES)
    both = _dot(x, dup).astype(BF16)
    return both[:, 0:LANES], both[:, LANES:2 * LANES]


def _transpose01(x):
    return _dot_nt(_lane_gather01(lambda c: c), x)


def _nsa_prep_kernel(q_ref, kv_ref, kw_ref, sm_ref, cs_ref, s1_ref, s2_ref, qn_ref, kn_ref,
                     qt_ref, ka_ref, kwd_ref, vst_ref, vwt_ref, kc_ref, vc_ref, gtt_ref,
                     *, tp, seq):
    cs, s1, s2 = cs_ref[...], s1_ref[...], s2_ref[...]
    nq = N_W // LANES
    kv = kv_ref[...].astype(F32)
    kwb = kw_ref[...].astype(F32)
    kc_ref[...] = kv[:, 0:LANES]
    vc_ref[...] = kv[:, LANES:2 * LANES]
    slabs = [q_ref[:, j * LANES:(j + 1) * LANES].astype(F32) for j in range(nq)]
    slabs += [kv[:, 2 * LANES:3 * LANES], kwb[:, 0:LANES]]
    gains = [qn_ref[...]] * nq + [kn_ref[1:2, :], kn_ref[2:3, :]]
    normed = [_group_rms(x, w) for x, w in zip(slabs, gains)]
    roped = [_rope(x, cs, s1, s2) for x in normed]
    qs = [(x * (N_DIM ** -0.5 * LOG2E)).astype(BF16) for x in roped[:nq]]
    k0, k1 = _dup_halves(roped[nq].astype(BF16))
    w0, w1 = _dup_halves(roped[nq + 1].astype(BF16))
    gates = _sigmoid(sm_ref[...])
    g_hi = gates.astype(BF16)
    g_lo = (gates - g_hi.astype(F32)).astype(BF16)
    qts = [_transpose01(x) for x in qs]
    vst = _transpose01(kv_ref[:, 3 * LANES:4 * LANES])
    vwt = _transpose01(kw_ref[:, LANES:2 * LANES])
    gtt_ref[...] = _transpose01(g_hi) + _transpose01(g_lo)
    for j in range(nq):
        qt_ref[j * LANES:(j + 1) * LANES, :] = qts[j].astype(BF16)
    pos0 = (pl.program_id(0) * tp) % seq
    pos = pos0 + lax.broadcasted_iota(jnp.int32, (tp, LANES), 0)
    blk = lax.broadcasted_iota(jnp.int32, (tp, LANES), 1)
    onehot = jnp.where(jnp.right_shift(pos, SEL_SHIFT) == blk, 1.0, 0.0).astype(BF16)
    ka_ref[:, 0:LANES] = k0
    ka_ref[:, LANES:2 * LANES] = onehot
    ka_ref[:, 2 * LANES:3 * LANES] = k1
    ka_ref[:, 3 * LANES:4 * LANES] = onehot
    fill =(lax.broadcasted_iota(jnp.int32, (V_ROWS - N_DIM, Q_BLOCK), 0) == 0).astype(BF16)
    for c in range(tp // Q_BLOCK):
        for g in range(N_GROUPS):
            vst_ref[c, g * V_ROWS:g * V_ROWS + N_DIM, :] = (
                vst[g * N_DIM:(g + 1) * N_DIM, c * Q_BLOCK:(c + 1) * Q_BLOCK].astype(BF16))
            vst_ref[c, g * V_ROWS + N_DIM:(g + 1) * V_ROWS, :] = fill
    kwd_ref[:, 0:LANES] = w0
    kwd_ref[:, LANES:2 * LANES] = w1
    vwt_ref[...] = vwt.astype(BF16)


def _nsa_prep(proj, small, cs, s1, s2, qn_row, kn_rows, batch, seq):
    tp = 2 * KV_TILE
    t = batch * seq
    nsb = seq // tp
    rb = lambda w, off: pl.BlockSpec((tp, w), lambda i, off=off, w=w: (i, off // w))
    tab = pl.BlockSpec((tp, LANES), lambda i: (i % nsb, 0))
    o = lambda w: pl.BlockSpec((tp, w), lambda i: (i, 0))
    ot = lambda r: pl.BlockSpec((r, tp), lambda i: (0, i))
    sds = lambda w, dt: jax.ShapeDtypeStruct((t, w), dt)
    return pl.pallas_call(
        functools.partial(_nsa_prep_kernel, tp=tp, seq=seq),
        grid=(t // tp,),
        in_specs=[rb(512, C_NQ), rb(512, C_KC), rb(512, C_KW), o(LANES), tab, tab, tab,
                  pl.BlockSpec((1, LANES), lambda i: (0, 0)), pl.BlockSpec((3, LANES), lambda i: (0, 0))],
        out_specs=[ot(N_W), o(512), o(256),
                   pl.BlockSpec((tp // Q_BLOCK, N_GROUPS * V_ROWS, Q_BLOCK), lambda i: (i, 0, 0)),
                   ot(LANES), o(128), o(128), ot(LANES)],
        out_shape=[jax.ShapeDtypeStruct((N_W, t), BF16), sds(512, BF16), sds(256, BF16),
                   jax.ShapeDtypeStruct((t // Q_BLOCK, N_GROUPS * V_ROWS, Q_BLOCK), BF16),
                   jax.ShapeDtypeStruct((LANES, t), BF16),
                   sds(128, F32), sds(128, F32), jax.ShapeDtypeStruct((LANES, t), F32)],
        compiler_params=_cparams("parallel"),
        name="nsa_prep",
    )(proj, proj, proj, small, cs, s1, s2, qn_row, kn_rows)


def _gelu_tanh(x):
    return 0.5 * x * (1.0 + jnp.tanh(0.7978845608028654 * (x + 0.044715 * x * x * x)))


def _compress_mlp(x_ref, pa_ref, pb_ref, wa_ref, wb_ref, w2_ref, hb_ref):
    n = x_ref.shape[0] // CMP_STRIDE
    ha = hb = None
    for r in range(CMP_STRIDE):
        xr = x_ref[pl.ds(r, n, stride=CMP_STRIDE), :]
        ws = slice(r * LANES, (r + 1) * LANES)
        da = _dot((xr + pa_ref[:, ws]).astype(BF16), wa_ref[ws, :])
        db = _dot((xr + pb_ref[:, ws]).astype(BF16), wb_ref[ws, :])
        ha = da if ha is None else ha + da
        hb = db if hb is None else hb + db
    hb_ref[0:n, :] = hb
    hb_ref[n:n + SUBLANES, :] = jnp.zeros((SUBLANES, hb_ref.shape[1]), F32)
    hid = ha + hb_ref[1:n + 1, :]
    return _dot(_gelu_tanh(hid).astype(BF16), w2_ref[...])


def _nsa_compress_kernel(kx_ref, vx_ref, kpa_ref, kpb_ref, vpa_ref, vpb_ref, kwa_ref, kwb_ref, kw2_ref,
                         vwa_ref, vwb_ref, vw2_ref, cs_ref, s1_ref, s2_ref, kn_ref,
                         kcd_ref, vct_ref, hb_ref):
    n = kx_ref.shape[0] // CMP_STRIDE
    rowi = lax.broadcasted_iota(jnp.int32, (n, LANES), 0)
    live = rowi < n - 1
    kc = _compress_mlp(kx_ref, kpa_ref, kpb_ref, kwa_ref, kwb_ref, kw2_ref, hb_ref)
    kc = _rope(_group_rms(kc, kn_ref[0:1, :]), cs_ref[...], s1_ref[...], s2_ref[...])
    kc = jnp.where(live, kc, 0.0)
    c0, c1 = _dup_halves(kc.astype(BF16))
    kcd_ref[:, 0:LANES] = c0
    kcd_ref[:, LANES:2 * LANES] = c1
    vc = _compress_mlp(vx_ref, vpa_ref, vpb_ref, vwa_ref, vwb_ref, vw2_ref, hb_ref)
    vc = jnp.where(live, vc, 0.0)
    vct_ref[0] = vc.T.astype(BF16)


def _nsa_compress(kx, vx, kp, vp, cs, s1, s2, kn_rows, batch, nrow):
    full = lambda a: pl.BlockSpec(a.shape, lambda b: (0,) * a.ndim)
    xs = pl.BlockSpec((nrow * CMP_STRIDE, LANES), lambda b: (b, 0))
    args = (kx, vx, kp[0], kp[1], vp[0], vp[1], kp[2], kp[3], kp[4], vp[2], vp[3], vp[4], cs, s1, s2, kn_rows)
    return pl.pallas_call(
        _nsa_compress_kernel,
        grid=(batch,),
        in_specs=[xs, xs] + [full(a) for a in args[2:]],
        out_specs=[pl.BlockSpec((nrow, 2 * LANES), lambda b: (b, 0)),
                   pl.BlockSpec((1, LANES, nrow), lambda b: (b, 0, 0))],
        out_shape=[jax.ShapeDtypeStruct((batch * nrow, 2 * LANES), BF16),
                   jax.ShapeDtypeStruct((batch, LANES, nrow), BF16)],
        scratch_shapes=[pltpu.VMEM((nrow + SUBLANES, 2 * CMP_HIDDEN), F32)],
        compiler_params=_cparams("parallel"),
        name="nsa_compress",
    )(*args)


def _compress_params(pos, w1, w2):
    def expand(w1_half, pos_half):
        w = w1_half.reshape(CMP_STRIDE, N_DIM, CMP_HIDDEN)
        z = jnp.zeros_like(w)
        g0 = jnp.concatenate([w, z], axis=-1)
        g1 = jnp.concatenate([z, w], axis=-1)
        wx = jnp.stack([g0, g1], axis=1).reshape(CMP_STRIDE * LANES, 2 * CMP_HIDDEN)
        px = jnp.concatenate([pos_half, pos_half], axis=-1).reshape(1, CMP_STRIDE * LANES)
        return wx.astype(BF16), px.astype(F32)

    half = CMP_STRIDE * N_DIM
    wa, pa = expand(w1[:half], pos[:CMP_STRIDE])
    wb, pb = expand(w1[half:], pos[CMP_STRIDE:])
    z2 = jnp.zeros_like(w2)
    w2x = jnp.concatenate([jnp.concatenate([w2, z2], axis=1), jnp.concatenate([z2, w2], axis=1)], axis=0)
    return pa, pb, wa, wb, w2x.astype(BF16)


def _nsa_attn_kernel(qt_ref, gtt_ref, z_ref, ka_ref, vst_ref,
                     kw0, kw1, kw2, kw3, kw4, wt0, wt1, wt2, wt3, wt4,
                     kcd_ref, vct_ref, ovt_ref, out_ref, qa_ref, acc_ref, *, ncmp):
    QB = Q_BLOCK
    NP = N_HEADS // 2
    PW = 2 * QB
    i = pl.program_id(1)
    t0 = i * QB
    kw_refs = (kw0, kw1, kw2, kw3, kw4)
    vwt_refs = (wt0, wt1, wt2, wt3, wt4)
    nwt = len(kw_refs)

    sub = lax.broadcasted_iota(jnp.int32, (QB, QB), 0)
    lane = lax.broadcasted_iota(jnp.int32, (QB, QB), 1)
    top = sub < N_DIM
    t_row = t0 + lax.broadcasted_iota(jnp.int32, (1, QB), 1)
    t_row2 = jnp.concatenate([t_row, t_row], axis=1)

    def pair_q(pr):
        slab = qt_ref[pr * LANES:(pr + 1) * LANES, :]
        zero = jnp.zeros_like(slab)
        return jnp.concatenate([jnp.where(top, slab, zero), jnp.where(top, zero, slab)], axis=1)

    blk_f = sub.astype(F32)
    cur = jnp.right_shift(t_row, SEL_SHIFT)
    grp = lambda pr: pr // (N_HPG // 2)
    for pr in range(NP):
        qa_ref[pr, 0:LANES, :] = pair_q(pr)

    def compressed(n):
        nidx = lax.broadcasted_iota(jnp.int32, (n, PW), 0)
        ok = (nidx * CMP_STRIDE + (CMP_LEN - 1) <= t_row2) & (nidx < ncmp - 1)
        any_ok = t_row2 >= CMP_LEN - 1
        s_cmp = [_dot(kcd_ref[0:n, grp(pr) * LANES:(grp(pr) + 1) * LANES], qa_ref[pr, 0:LANES, :])
                 for pr in range(NP)]
        p_cmp = []
        for pr in range(NP):
            s = jnp.where(ok, s_cmp[pr], NEG)
            mx = jnp.max(s, axis=0, keepdims=True)
            p = jnp.exp2(s - mx)
            den = jnp.sum(p, axis=0, keepdims=True)
            p_cmp.append(p * jnp.where(any_ok, 1.0 / den, 0.0))
        outs = [_dot(vct_ref[0, grp(pr) * N_DIM:(grp(pr) + 1) * N_DIM, 0:n], p_cmp[pr].astype(BF16))
                for pr in range(NP)]
        for g in range(N_GROUPS):
            psum = jnp.zeros((n, QB), F32)
            for pp in range(N_HPG // 2):
                p = p_cmp[g * (N_HPG // 2) + pp]
                psum = psum + p[:, 0:QB] + p[:, QB:PW]
            p_hi = psum.astype(BF16)
            p_lo = (psum - p_hi.astype(F32)).astype(BF16)
            outs.append(_dot(ovt_ref[:, 0:n], p_hi) + _dot(ovt_ref[:, 0:n], p_lo))
        return tuple(outs)

    n_opts = list(range(CMP_CHUNK, ncmp + 1, CMP_CHUNK))
    need = (t0 + QB - CMP_LEN) // CMP_STRIDE + 1
    outs = lax.switch(jnp.clip((need - 1) // CMP_CHUNK, 0, len(n_opts) - 1),
                      [functools.partial(compressed, n) for n in n_opts])
    o_cmp, imps = outs[:NP], outs[NP:]
    forced = (sub == 0) | (sub == cur) | (sub == cur - 1)
    n_rounds = SEL_TOPK - 3
    valid = sub <= cur
    works0 = [jnp.where(valid, jnp.where(forced, -jnp.inf, imp), NEG) for imp in imps]

    def rounds(works, exact):
        sels = [forced for _ in range(N_GROUPS)]
        for _ in range(n_rounds):
            for g in range(N_GROUPS):
                mx = jnp.max(works[g], axis=0, keepdims=True)
                if exact:
                    idx = jnp.min(jnp.where(works[g] == mx, blk_f, float(LANES)), axis=0, keepdims=True)
                    pick = blk_f == idx
                else:
                    pick = works[g] == mx
                sels[g] = sels[g] | pick
                works[g] = jnp.where(pick, -jnp.inf, works[g])
        return tuple(jnp.where(s & valid, 0.0, NEG) for s in sels)

    bias = rounds(list(works0), exact=False)
    picked = [jnp.sum(jnp.where((b == 0.0) & jnp.logical_not(forced), 1.0, 0.0), axis=0, keepdims=True)
              for b in bias]
    tied = jnp.max(functools.reduce(jnp.maximum, picked)) > n_rounds
    bias = lax.cond(tied, lambda: rounds(list(works0), exact=True), lambda: bias)
    for g in range(N_GROUPS):
        neg = bias[g].astype(BF16)
        neg2 = jnp.concatenate([neg, neg], axis=1)
        for pp in range(N_HPG // 2):
            qa_ref[g * (N_HPG // 2) + pp, LANES:2 * LANES, :] = neg2

    CH = QB
    above = jnp.concatenate([sub > lane, sub > lane], axis=1)

    def sel_scores(cj):
        k0 = pl.multiple_of(cj * CH, CH)
        return [_dot(ka_ref[pl.ds(k0, CH), 2 * grp(pr) * LANES:2 * (grp(pr) + 1) * LANES], qa_ref[pr])
                for pr in range(NP)]

    def sel_softmax(ss, ms, masked):
        new, alphas, ps = [], [], []
        for pr in range(NP):
            s = jnp.where(above, NEG, ss[pr]) if masked else ss[pr]
            m_new = jnp.maximum(ms[pr], jnp.max(s, axis=0, keepdims=True))
            alphas.append(jnp.exp2(ms[pr] - m_new))
            ps.append(jnp.exp2(s - m_new).astype(BF16))
            new.append(m_new)
        return tuple(new), alphas, ps

    def sel_values(cj, alphas, ps):
        for pr in range(NP):
            g = grp(pr)
            acc_ref[pr] = alphas[pr] * acc_ref[pr] + _dot(vst_ref[cj, g * V_ROWS:(g + 1) * V_ROWS, :], ps[pr])

    def sel_chunk(cj, ms, masked):
        ms, alphas, ps = sel_softmax(sel_scores(cj), ms, masked)
        sel_values(cj, alphas, ps)
        return ms

    def sel_tile(jt, ms, nc):
        c0 = jt * nc
        k0 = pl.multiple_of(c0 * CH, CH)
        big = [_dot(ka_ref[pl.ds(k0, nc * CH), 2 * grp(pr) * LANES:2 * (grp(pr) + 1) * LANES], qa_ref[pr])
               for pr in range(NP)]
        for c in range(nc):
            ss = [b[c * CH:(c + 1) * CH] for b in big]
            ms, alphas, ps = sel_softmax(ss, ms, False)
            sel_values(c0 + c, alphas, ps)
        return ms

    acc_ref[...] = jnp.zeros_like(acc_ref)
    ms = tuple(jnp.full((1, PW), 2.0 * NEG, F32) for _ in range(NP))
    done = 0
    for nc in (4 * KV_TILE // CH, 2 * KV_TILE // CH, KV_TILE // CH):
        n_run = (i - done) // nc
        ms = lax.fori_loop(done // nc, done // nc + n_run, functools.partial(sel_tile, nc=nc), ms)
        done = done + n_run * nc
    ms = lax.fori_loop(done, i, lambda cj, ms: sel_chunk(cj, ms, False), ms)
    s_diag = sel_scores(i)
    s_win = [[_dot(kw_refs[c][:, grp(pr) * LANES:(grp(pr) + 1) * LANES], qa_ref[pr, 0:LANES, :])
              for c in range(nwt)] for pr in range(NP)]
    ms, alphas, ps_diag = sel_softmax(s_diag, ms, True)
    p_win, d_win = [], []
    for pr in range(NP):
        ss = []
        for c in range(nwt):
            ok = (i - (nwt - 1) + c) >= 0
            if c == 0:
                ok = ok & above
            elif c == nwt - 1:
                ok = ok & jnp.logical_not(above)
            ss.append(jnp.where(ok, s_win[pr][c], NEG))
        mx = ss[0].max(axis=0, keepdims=True)
        for c in range(1, nwt):
            mx = jnp.maximum(mx, ss[c].max(axis=0, keepdims=True))
        ps = [jnp.exp2(ss[c] - mx) for c in range(nwt)]
        den = ps[0].sum(axis=0, keepdims=True)
        for c in range(1, nwt):
            den = den + ps[c].sum(axis=0, keepdims=True)
        p_win.append([p.astype(BF16) for p in ps])
        d_win.append(den)
    sel_values(i, alphas, ps_diag)
    o_win = []
    for pr in range(NP):
        g = grp(pr)
        acc = _dot(vwt_refs[0][g * N_DIM:(g + 1) * N_DIM, :], p_win[pr][0])
        for c in range(1, nwt):
            acc = acc + _dot(vwt_refs[c][g * N_DIM:(g + 1) * N_DIM, :], p_win[pr][c])
        o_win.append(acc * (1.0 / d_win[pr]))
    o_sel = [acc_ref[pr, 0:N_DIM, :] * (1.0 / acc_ref[pr, N_DIM:N_DIM + 1, :]) for pr in range(NP)]

    for pr in range(NP):
        halves = []
        for par in range(2):
            h = 2 * pr + par
            cs = slice(par * QB, (par + 1) * QB)
            gc = gtt_ref[SM_G + 3 * h:SM_G + 3 * h + 1, :]
            gs = gtt_ref[SM_G + 3 * h + 1:SM_G + 3 * h + 2, :]
            gw = gtt_ref[SM_G + 3 * h + 2:SM_G + 3 * h + 3, :]
            halves.append(gc * o_cmp[pr][:, cs] + gs * o_sel[pr][:, cs] + gw * o_win[pr][:, cs])
        y = jnp.concatenate(halves, axis=0).T
        sl = slice(pr * LANES, (pr + 1) * LANES)
        out_ref[:, sl] = (y * _silu(z_ref[:, sl].astype(F32))).astype(out_ref.dtype)


def _nsa_attn(proj, qt, gtt, ka, vst, kwd, vwt, kcd, vct, ovt, batch, seq):
    QB = Q_BLOCK
    nqb = seq // QB
    ncmp = seq // CMP_STRIDE
    nwt = WINDOW // QB + 1
    row = lambda w: pl.BlockSpec((QB, w), lambda b, i: (b * nqb + i, 0))
    colb = lambda r: pl.BlockSpec((r, QB), lambda b, i: (0, b * nqb + i))
    wblk = lambda b, i, c: b * nqb + jnp.maximum(i - (nwt - 1) + c, 0)
    in_specs = ([colb(N_W), colb(LANES), pl.BlockSpec((QB, N_W), lambda b, i: (b * nqb + i, C_NZ // N_W)),
                 pl.BlockSpec((seq, 4 * LANES), lambda b, i: (b, 0)),
                 pl.BlockSpec((nqb, N_GROUPS * V_ROWS, QB), lambda b, i: (b, 0, 0))]
                + [pl.BlockSpec((QB, 2 * LANES), lambda b, i, c=c: (wblk(b, i, c), 0)) for c in range(nwt)]
                + [pl.BlockSpec((LANES, QB), lambda b, i, c=c: (0, wblk(b, i, c))) for c in range(nwt)]
                + [pl.BlockSpec((ncmp, 2 * LANES), lambda b, i: (b, 0)),
                   pl.BlockSpec((1, LANES, ncmp), lambda b, i: (b, 0, 0)),
                   pl.BlockSpec((LANES, ncmp), lambda b, i: (0, 0))])
    return pl.pallas_call(
        functools.partial(_nsa_attn_kernel, ncmp=ncmp),
        grid=(batch, nqb),
        in_specs=in_specs,
        out_specs=row(N_W),
        out_shape=jax.ShapeDtypeStruct((batch * seq, N_W), BF16),
        scratch_shapes=[pltpu.VMEM((N_HEADS // 2, 2 * LANES, 2 * QB), BF16),
                        pltpu.VMEM((N_HEADS // 2, V_ROWS, 2 * QB), F32)],
        compiler_params=_cparams("parallel", "arbitrary"),
        name="nsa_attn",
    )(qt, gtt, proj, ka, vst, *([kwd] * nwt), *([vwt] * nwt), kcd, vct, ovt)


def _hgrn_kernel(f_ref, i_ref, q_ref, z_ref, lb_ref, hn_ref, out_ref, st_ref):
    T, D = H_TILE, H_DIM
    W = H_W

    @pl.when(pl.program_id(1) == 0)
    def _():
        st_ref[...] = jnp.zeros_like(st_ref)

    lb = lb_ref[...]
    fp = f_ref[...]
    e = jnp.exp(-jnp.abs(fp))
    r = 1.0 / (1.0 + e)
    sig = jnp.where(fp >= 0, r, e * r)
    nsig = jnp.where(fp >= 0, e * r, r)
    logf2 = jnp.log(lb + (1.0 - lb) * sig) * LOG2E
    kk = (1.0 - lb) * nsig

    row = lax.broadcasted_iota(jnp.int32, (T, T), 0)
    col = lax.broadcasted_iota(jnp.int32, (T, T), 1)
    tri01 = jnp.where(col <= row, 1.0, 0.0).astype(BF16)
    bcum = _dot01(tri01, logf2)

    segs = []
    sg = 2 * H_LAG
    while sg <= T:
        if sg == T:
            bmid = jnp.broadcast_to(bcum[sg // 2 - 1:sg // 2, :], (T, W))
        else:
            b3 = bcum.reshape(T // sg, sg, W)
            bmid = jnp.broadcast_to(b3[:, sg // 2 - 1:sg // 2, :], (T // sg, sg, W)).reshape(T, W)
        x = bcum - bmid
        shift = sg.bit_length() - 1
        pair_ok = ((jnp.right_shift(row, shift) == jnp.right_shift(col, shift))
                   & (jnp.bitwise_and(row, sg - 1) >= sg // 2) & (jnp.bitwise_and(col, sg - 1) < sg // 2))
        xb = x.astype(BF16)
        segs.append((jnp.exp2(jnp.minimum(xb, -xb)), pair_ok))
        sg *= 2

    sub3 = lax.broadcasted_iota(jnp.int32, (T // SUBLANES, SUBLANES, 1), 1)
    lag_ok = [jnp.bitwise_and(sub3, H_LAG - 1) >= d for d in range(H_LAG)]
    b_end = bcum[T - 1:T, :]
    v_all = i_ref[...].astype(F32)
    def head_scores(h):
        hs = slice(h * D, (h + 1) * D)
        q = _silu(q_ref[:, hs].astype(F32))
        k = kk[:, hs]
        b = bcum[:, hs]
        v = v_all[:, hs]
        vb = v.astype(BF16)
        a_seg = []
        qb, kb = q.astype(BF16), k.astype(BF16)
        for fac, _ in segs:
            f = fac[:, hs]
            a_seg.append(_dot_nt(qb * f, kb * f))
        st = st_ref[h]
        o_state = _dot_nt(qb * jnp.exp2(b.astype(BF16)), st.astype(BF16))
        kt = kb * jnp.exp2((b_end[:, hs] - b).astype(BF16))
        st_ref[h] = st * jnp.exp2(b_end[:, hs]) + _dot_tn(vb, kt)
        q3, k3, b3, v3 = (a.reshape(T // SUBLANES, SUBLANES, D) for a in (q, k, b, v))
        o3 = jnp.sum(q3 * k3, axis=2, keepdims=True) * v3
        for d in range(1, H_LAG):
            ks, bs, vs = (pltpu.roll(a, d, 1) for a in (k3, b3, v3))
            a = jnp.sum(q3 * ks * jnp.exp2(b3 - bs), axis=2, keepdims=True)
            o3 = o3 + jnp.where(lag_ok[d], a, 0.0) * vs
        return hs, a_seg, o3.reshape(T, D) + o_state, vb

    def head_finish(hs, a_seg, o_part, vb):
        a_tot = jnp.zeros((T, T), F32)
        for (_, pair_ok), a in zip(segs, a_seg):
            a_tot = jnp.where(pair_ok, a, a_tot)
        oh = o_part + _dot(a_tot.astype(BF16), vb)
        on = oh * lax.rsqrt(jnp.mean(oh * oh, axis=1, keepdims=True) + EPS) * hn_ref[:, hs]
        out_ref[:, hs] = (on * _silu(z_ref[:, hs].astype(F32))).astype(out_ref.dtype)

    pending = None
    for h in range(H_HEADS):
        cur = head_scores(h)
        if pending is not None:
            head_finish(*pending)
        pending = cur
    head_finish(*pending)


def _hgrn(proj, fgate, lb_row, hn_row, batch, seq):
    T = H_TILE
    nt = seq // T
    cb = lambda j: pl.BlockSpec((T, H_W), lambda b, c, j=j: (b * nt + c, j))
    vec = pl.BlockSpec((1, H_W), lambda b, c: (0, 0))
    return pl.pallas_call(
        _hgrn_kernel,
        grid=(batch, nt),
        in_specs=[pl.BlockSpec((T, H_W), lambda b, c: (b * nt + c, 0)), cb(1), cb(2), cb(3), vec, vec],
        out_specs=pl.BlockSpec((T, H_W), lambda b, c: (b * nt + c, 0)),
        out_shape=jax.ShapeDtypeStruct((batch * seq, H_W), BF16),
        scratch_shapes=[pltpu.VMEM((H_HEADS, H_DIM, H_DIM), F32)],
        compiler_params=_cparams("parallel", "arbitrary"),
        name="hgrn2",
    )(fgate, proj, proj, proj, lb_row, hn_row)


def _even_relayout(a):
    nb = 5 * M_W + 2 * M_HEADS
    g0 = nb + N_W + 6 * N_KV_W
    col = lambda lo, n: lax.slice_in_dim(a, lo, lo + n, axis=-1)
    pad = jnp.zeros(a.shape[:-1] + (C_NZ - C_SMALL - SM_G - 3 * N_HEADS,), a.dtype)
    out = jnp.concatenate([
        col(0, 5 * M_W),
        col(nb, N_W + 6 * N_KV_W),
        col(5 * M_W, 2 * M_HEADS),
        col(g0, 3 * N_HEADS), pad,
        col(g0 + 3 * N_HEADS, N_W),
    ], axis=-1)
    assert out.shape[-1] == EV_COLS
    return out


def _rope_tables(positions):
    inv_freq = ROPE_THETA ** (-np.arange(ROPE_HALF, dtype=np.float64) / ROPE_HALF)
    ang = positions.astype(np.float64)[:, None] * inv_freq[None, :]
    cos, sin = np.cos(ang).astype(np.float32), np.sin(ang).astype(np.float32)
    n = positions.shape[0]
    z8 = np.zeros((n, ROPE_HALF), np.float32)
    rest1 = np.ones((n, N_DIM - ROPE_DIM), np.float32)
    rest0 = np.zeros((n, N_DIM - ROPE_DIM), np.float32)
    cs = np.concatenate([cos, cos, rest1], axis=1)
    s1 = np.concatenate([z8, sin, rest0], axis=1)
    s2 = np.concatenate([-sin, z8, rest0], axis=1)
    two = lambda a: np.concatenate([a, a], axis=1)
    return two(cs), two(s1), two(s2)


def _overlap_matrix(seq):
    ncmp = seq // CMP_STRIDE
    n_sel = seq // SEL_BLOCK
    cs = np.arange(ncmp) * CMP_STRIDE
    ss = np.arange(n_sel) * SEL_BLOCK
    ov = np.clip(np.minimum(cs[:, None] + CMP_LEN, ss[None, :] + SEL_BLOCK)
                 - np.maximum(cs[:, None], ss[None, :]), 0, None).astype(np.float32)
    ov[ncmp - 1, :] = 0.0
    return np.pad(ov, ((0, 0), (0, LANES - n_sel)))


def _even_layer(x2d, res, norm_w, w_in, b_in, f_bias, conv_w, conv_b, head_norm, q_norm, k_norm,
                ck_pos, ck_w1, ck_w2, cv_pos, cv_w1, cv_w2, w_out, batch, seq):
    assert seq // SEL_BLOCK <= LANES and seq % KV_TILE == 0
    outs = _norm_proj(x2d, norm_w, _even_relayout(w_in.astype(BF16)), _even_relayout(b_in),
                      (C_SMALL, C_SMALL + LANES), tn=EV_COLS // 3, res=res, tm=1024 if res else 2048)
    if res:
        x2d = outs[0]
    proj, small = outs[-2:]

    f_bias_row = jnp.zeros((1, LANES), F32).at[0, SM_F:SM_F + M_HEADS].set(f_bias)
    ya = _mlstm(proj, small, conv_w, conv_b, f_bias_row, head_norm, batch, seq)

    cs, s1, s2 = (jnp.asarray(a) for a in _rope_tables(np.arange(seq)))
    qn_row = jnp.concatenate([q_norm, q_norm]).reshape(1, LANES)
    kn_rows = jnp.concatenate([k_norm, k_norm], axis=1)
    qt, ka, kwd, vst, vwt, kc, vc, gtt = _nsa_prep(proj, small, cs, s1, s2, qn_row, kn_rows, batch, seq)

    ncmp = seq // CMP_STRIDE
    ccs, cs1, cs2 = (jnp.asarray(a) for a in _rope_tables(np.arange(ncmp) * CMP_STRIDE + CMP_LEN - 1))
    kcd, vct = _nsa_compress(
        kc, vc,
        _compress_params(ck_pos, ck_w1, ck_w2), _compress_params(cv_pos, cv_w1, cv_w2),
        ccs, cs1, cs2, kn_rows, batch, ncmp)

    ovt = jnp.asarray(_overlap_matrix(seq).T).astype(BF16)
    yb = _nsa_attn(proj, qt, gtt, ka, vst, kwd, vwt, kcd, vct, ovt, batch, seq)

    w_o = w_out.astype(BF16)
    return x2d, [(ya, w_o[:M_W]), (yb, w_o[M_W:])]


def _odd_layer(x2d, res, norm_w, w_in, b_in, lb, head_norm, w_out, batch, seq):
    outs = _norm_proj(x2d, norm_w, w_in.astype(BF16), b_in, (0, H_W), tn=H_W, res=res)
    if res:
        x2d = outs[0]
    proj, fgate = outs[-2:]
    y = _hgrn(proj, fgate, lb.reshape(1, H_W), head_norm.reshape(1, H_W), batch, seq)
    return x2d, [(y, w_out.astype(BF16))]


def kernel(x, norm_w, ev_w_in, ev_b_in, mlstm_f_bias, mlstm_conv_w, mlstm_conv_b, mlstm_head_norm, nsa_q_norm, nsa_k_norm, cmp_k_pos, cmp_k_w1, cmp_k_w2, cmp_v_pos, cmp_v_w1, cmp_v_w2, ev_w_out, od_w_in, od_b_in, hgrn_lb_logits, hgrn_head_norm, od_w_out):
    batch, seq, d = x.shape
    depth = norm_w.shape[0]
    lbs = jnp.cumsum(jax.nn.softmax(hgrn_lb_logits.astype(F32), axis=0), axis=0)
    x2d = x.reshape(batch * seq, d)
    res = []
    for layer in range(depth):
        j = layer // 2
        if layer % 2 == 0:
            x2d, res = _even_layer(x2d, res, norm_w[layer], ev_w_in[j], ev_b_in[j], mlstm_f_bias[j],
                                   mlstm_conv_w[j], mlstm_conv_b[j], mlstm_head_norm[j], nsa_q_norm[j],
                                   nsa_k_norm[j], cmp_k_pos[j], cmp_k_w1[j], cmp_k_w2[j], cmp_v_pos[j],
                                   cmp_v_w1[j], cmp_v_w2[j], ev_w_out[j], batch, seq)
        else:
            x2d, res = _odd_layer(x2d, res, norm_w[layer], od_w_in[j], od_b_in[j], lbs[j], hgrn_head_norm[j],
                                  od_w_out[j], batch, seq)
    if len(res) == 2:
        x2d = _out_proj2(res[0][0], res[1][0], res[0][1], res[1][1], x2d)
    elif res:
        x2d = _out_proj1(res[0][0], res[0][1], x2d)
    return x2d.reshape(batch, seq, d)
```
